```python
import math
import jax
import jax.numpy as jnp
from jax import lax
import numpy as np

D_MODEL = 1024
BATCH = 8
SEQ = 2048
DEPTH = 1
DEC_BATCH = 128
DEC_SEQ = 4
PAST_LEN = 16384
PAGE_SIZE = 128

RET_HEADS = 4
RET_DK = 128
RET_DV = 128
RET_QK = RET_HEADS * RET_DK
RET_V = RET_HEADS * RET_DV
RET_CHUNK = 128
ROPE_BASE = 10000.0
CONV_CH = 512
CONV_WIDTH = 31
N_GROUPS = 4
EXP_PER_GROUP = 4
N_EXPERTS = N_GROUPS * EXP_PER_GROUP
TOP_K = 2
EXP_FF = 512
PLE_DIM = 256
ALPHA = (2 * DEPTH) ** 0.25
BETA = (8 * DEPTH) ** -0.25
LN_EPS = 1e-5
IN_WIDTHS = (RET_QK, RET_QK, RET_V, RET_V, CONV_CH, CONV_CH, D_MODEL, D_MODEL)
IN_COLS = sum(IN_WIDTHS)
IN_SPLITS = [int(s) for s in np.cumsum(IN_WIDTHS)[:-1]]

kernel_name = 'retention_conformer_hmoe_step'


def _layer_norm(x, g, b):
    xf = x.astype(jnp.float32)
    mu = jnp.mean(xf, axis=-1, keepdims=True)
    var = jnp.mean(jnp.square(xf - mu), axis=-1, keepdims=True)
    y = (xf - mu) * lax.rsqrt(var + LN_EPS) * g.astype(jnp.float32) + b.astype(jnp.float32)
    return y.astype(x.dtype)


def _rotary(t, pos):
    half = RET_DK // 2
    inv_freq = ROPE_BASE ** (-jnp.arange(half, dtype=jnp.float32) / half)
    ang = pos[:, None] * inv_freq[None, :]
    cos = jnp.cos(ang)[None, :, None, :]
    sin = jnp.sin(ang)[None, :, None, :]
    t1, t2 = t[..., :half], t[..., half:]
    return jnp.concatenate([t1 * cos - t2 * sin, t1 * sin + t2 * cos], axis=-1)


def _retention(q, k, v, s0):
    bsz, seqlen = q.shape[0], q.shape[1]
    c = math.gcd(seqlen, RET_CHUNK)
    n = seqlen // c
    log_gamma = jnp.log(1.0 - 2.0 ** (-5.0 - jnp.arange(RET_HEADS, dtype=jnp.float32)))
    idx = jnp.arange(c, dtype=jnp.float32)
    rel = idx[:, None] - idx[None, :]
    causal = rel >= 0
    decay = jnp.where(causal[None], jnp.exp(jnp.where(causal, rel, 0.0)[None] * log_gamma[:, None, None]), 0.0)
    q_decay = jnp.exp((idx[:, None] + 1.0) * log_gamma[None, :])
    k_decay = jnp.exp((c - 1.0 - idx[:, None]) * log_gamma[None, :])
    chunk_decay = jnp.exp(c * log_gamma)

    def to_chunks(t):
        return t.reshape(bsz, n, c, RET_HEADS, t.shape[-1]).transpose(1, 0, 2, 3, 4)

    def step(s, blk):
        qc, kc, vc = blk
        scores = jnp.einsum('bihd,bjhd->bhij', qc, kc) * decay[None]
        inner = jnp.einsum('bhij,bjhv->bihv', scores, vc)
        cross = jnp.einsum('bihd,bhdv->bihv', qc, s) * q_decay[None, :, :, None]
        s_new = chunk_decay[None, :, None, None] * s + jnp.einsum('bjhd,bjhv->bhdv', kc * k_decay[None, :, :, None], vc)
        return s_new, inner + cross

    s_fin, out = lax.scan(step, s0, (to_chunks(q), to_chunks(k), to_chunks(v)))
    out = out.transpose(1, 0, 2, 3, 4).reshape(bsz, seqlen, RET_HEADS, RET_DV)
    return out, s_fin


def _depthwise_causal_conv(u, buf, w, b):
    full = jnp.concatenate([buf.astype(u.dtype), u], axis=1)
    out = lax.conv_general_dilated(full, w[:, None, :].astype(u.dtype), window_strides=(1,), padding='VALID',
                                   dimension_numbers=('NWC', 'WIO', 'NWC'), feature_group_count=CONV_CH)
    return out + b, full[:, full.shape[1] - (CONV_WIDTH - 1):]


def _hier_moe(x, w_grp, b_grp, w_exp, b_exp, w_gu, w_dn):
    bsz, seqlen, d = x.shape
    xt = x.reshape(-1, d)
    ntok = xt.shape[0]
    g_logits = (xt @ w_grp + b_grp).astype(jnp.float32)
    g_prob = jax.nn.softmax(g_logits, axis=-1)
    g_idx = jnp.argmax(g_logits, axis=-1).astype(jnp.int32)
    g_w = jnp.take_along_axis(g_prob, g_idx[:, None], axis=1)
    e_logits = (xt @ w_exp + b_exp).astype(jnp.float32).reshape(ntok, N_GROUPS, EXP_PER_GROUP)
    e_logits = jnp.take_along_axis(e_logits, g_idx[:, None, None], axis=1)[:, 0]
    top_v, top_i = lax.top_k(e_logits, TOP_K)
    w_pair = jax.nn.softmax(top_v, axis=-1) * g_w
    expert_id = (g_idx[:, None] * EXP_PER_GROUP + top_i).reshape(-1).astype(jnp.int32)
    order = jnp.argsort(expert_id)
    rows = xt[order // TOP_K]
    sizes = jnp.bincount(expert_id, length=N_EXPERTS).astype(jnp.int32)
    h = lax.ragged_dot(rows, w_gu, sizes)
    h_gate, h_up = jnp.split(h, 2, axis=-1)
    h = jax.nn.silu(h_gate) * h_up
    out_sorted = lax.ragged_dot(h, w_dn, sizes)
    out = jnp.zeros_like(out_sorted).at[order].set(out_sorted).reshape(ntok, TOP_K, d)
    y = jnp.einsum('nkd,nk->nd', out, w_pair.astype(out.dtype))
    return y.reshape(bsz, seqlen, d)


def _layer(x, p, pos, s_ret, s_conv, w_in, b_in, ret_gn_g, ret_gn_b, w_ret_o, conv_w, conv_b, conv_ln_g,
           conv_ln_b, w_conv_o, w_out, ln1_g, ln1_b, w_grp, b_grp, w_exp, b_exp, w_gu, w_dn, ln2_g, ln2_b,
           w_pg, b_pg, w_ple):
    bsz, seqlen, _ = x.shape
    proj = x @ w_in + b_in
    q, k, v, g, u_a, u_b, gt_a, gt_b = jnp.split(proj, IN_SPLITS, axis=-1)
    q = _rotary(q.astype(jnp.float32).reshape(bsz, seqlen, RET_HEADS, RET_DK), pos)
    k = _rotary(k.astype(jnp.float32).reshape(bsz, seqlen, RET_HEADS, RET_DK), pos) * (RET_DK ** -0.5)
    v = v.astype(jnp.float32).reshape(bsz, seqlen, RET_HEADS, RET_DV)
    ret, s_ret_new = _retention(q, k, v, s_ret.astype(jnp.float32))
    ret = _layer_norm(ret, ret_gn_g.reshape(RET_HEADS, RET_DV), ret_gn_b.reshape(RET_HEADS, RET_DV))
    ret = ret.astype(x.dtype).reshape(bsz, seqlen, RET_V)
    branch_a = (jax.nn.silu(g) * ret) @ w_ret_o
    u = u_a * jax.nn.sigmoid(u_b)
    c_out, s_conv_new = _depthwise_causal_conv(u, s_conv, conv_w, conv_b)
    branch_b = jax.nn.silu(_layer_norm(c_out, conv_ln_g, conv_ln_b)) @ w_conv_o
    mix = (jax.nn.sigmoid(gt_a) * branch_a + jax.nn.sigmoid(gt_b) * branch_b) @ w_out
    x1 = _layer_norm(ALPHA * x + mix, ln1_g, ln1_b)
    x2 = _layer_norm(ALPHA * x1 + _hier_moe(x1, w_grp, b_grp, w_exp, b_exp, w_gu, w_dn), ln2_g, ln2_b)
    x3 = x2 + jax.nn.sigmoid(x2 @ w_pg + b_pg) * (p @ w_ple)
    return x3, s_ret_new.astype(x.dtype), s_conv_new


def setup_inputs(seed: int = 0) -> dict:
    key = jax.random.key(seed)
    ks = jax.random.split(key, 32)
    f32 = jnp.float32

    def nrm(k, shape, scale):
        return jax.random.normal(k, shape, f32) * scale

    return {
        'x_prompt': nrm(ks[0], (BATCH, SEQ, D_MODEL), 1.0),
        'x_sample': nrm(ks[1], (DEC_BATCH, DEC_SEQ, D_MODEL), 1.0),
        'state_ret': nrm(ks[2], (DEPTH, DEC_BATCH, RET_HEADS, RET_DK, RET_DV), 0.1),
        'state_conv': nrm(ks[3], (DEPTH, DEC_BATCH, CONV_WIDTH - 1, CONV_CH), 0.5),
        'p_prompt': nrm(ks[4], (DEPTH, BATCH, SEQ, PLE_DIM), 1.0),
        'p_sample': nrm(ks[5], (DEPTH, DEC_BATCH, DEC_SEQ, PLE_DIM), 1.0),
        'w_in': nrm(ks[6], (DEPTH, D_MODEL, IN_COLS), D_MODEL ** -0.5),
        'b_in': nrm(ks[7], (DEPTH, IN_COLS), 0.02),
        'ret_gn_g': 1.0 + nrm(ks[8], (DEPTH, RET_V), 0.02),
        'ret_gn_b': nrm(ks[9], (DEPTH, RET_V), 0.02),
        'w_ret_o': nrm(ks[10], (DEPTH, RET_V, D_MODEL), RET_V ** -0.5),
        'conv_w': nrm(ks[11], (DEPTH, CONV_WIDTH, CONV_CH), CONV_WIDTH ** -0.5),
        'conv_b': nrm(ks[12], (DEPTH, CONV_CH), 0.02),
        'conv_ln_g': 1.0 + nrm(ks[13], (DEPTH, CONV_CH), 0.02),
        'conv_ln_b': nrm(ks[14], (DEPTH, CONV_CH), 0.02),
        'w_conv_o': nrm(ks[15], (DEPTH, CONV_CH, D_MODEL), CONV_CH ** -0.5),
        'w_out': nrm(ks[16], (DEPTH, D_MODEL, D_MODEL), BETA * D_MODEL ** -0.5),
        'ln1_g': 1.0 + nrm(ks[17], (DEPTH, D_MODEL), 0.02),
        'ln1_b': nrm(ks[18], (DEPTH, D_MODEL), 0.02),
        'w_grp': nrm(ks[19], (DEPTH, D_MODEL, N_GROUPS), D_MODEL ** -0.5),
        'b_grp': nrm(ks[20], (DEPTH, N_GROUPS), 0.01),
        'w_exp': nrm(ks[21], (DEPTH, D_MODEL, N_EXPERTS), D_MODEL ** -0.5),
        'b_exp': nrm(ks[22], (DEPTH, N_EXPERTS), 0.01),
        'w_gu': nrm(ks[23], (DEPTH, N_EXPERTS, D_MODEL, 2 * EXP_FF), D_MODEL ** -0.5),
        'w_dn': nrm(ks[24], (DEPTH, N_EXPERTS, EXP_FF, D_MODEL), BETA * EXP_FF ** -0.5),
        'ln2_g': 1.0 + nrm(ks[25], (DEPTH, D_MODEL), 0.02),
        'ln2_b': nrm(ks[26], (DEPTH, D_MODEL), 0.02),
        'w_pg': nrm(ks[27], (DEPTH, D_MODEL, D_MODEL), D_MODEL ** -0.5),
        'b_pg': nrm(ks[28], (DEPTH, D_MODEL), 0.02),
        'w_ple': nrm(ks[29], (DEPTH, PLE_DIM, D_MODEL), PLE_DIM ** -0.5),
    }


def reference(x_prompt, x_sample, state_ret, state_conv, p_prompt, p_sample, w_in, b_in, ret_gn_g, ret_gn_b,
              w_ret_o, conv_w, conv_b, conv_ln_g, conv_ln_b, w_conv_o, w_out, ln1_g, ln1_b, w_grp, b_grp,
              w_exp, b_exp, w_gu, w_dn, ln2_g, ln2_b, w_pg, b_pg, w_ple):
    bp, lp = x_prompt.shape[0], x_prompt.shape[1]
    ls = x_sample.shape[1]
    pos_p = jnp.arange(lp, dtype=jnp.float32)
    pos_s = PAST_LEN + jnp.arange(ls, dtype=jnp.float32)
    hp, hs = x_prompt, x_sample
    ret_p, conv_p, ret_s, conv_s = [], [], [], []
    for l in range(DEPTH):
        wl = (w_in[l], b_in[l], ret_gn_g[l], ret_gn_b[l], w_ret_o[l], conv_w[l], conv_b[l], conv_ln_g[l],
              conv_ln_b[l], w_conv_o[l], w_out[l], ln1_g[l], ln1_b[l], w_grp[l], b_grp[l], w_exp[l], b_exp[l],
              w_gu[l], w_dn[l], ln2_g[l], ln2_b[l], w_pg[l], b_pg[l], w_ple[l])
        s_ret0 = jnp.zeros((bp, RET_HEADS, RET_DK, RET_DV), jnp.float32)
        s_conv0 = jnp.zeros((bp, CONV_WIDTH - 1, CONV_CH), x_prompt.dtype)
        hp, rp, cp = _layer(hp, p_prompt[l], pos_p, s_ret0, s_conv0, *wl)
        hs, rs, cs = _layer(hs, p_sample[l], pos_s, state_ret[l], state_conv[l], *wl)
        ret_p.append(rp)
        conv_p.append(cp)
        ret_s.append(rs)
        conv_s.append(cs)
    return (hp, hs, jnp.stack(ret_p), jnp.stack(conv_p), jnp.stack(ret_s), jnp.stack(conv_s))
```

```python
import functools
import math

import jax
import jax.numpy as jnp
import numpy as np
from jax import lax
from jax.experimental import pallas as pl
from jax.experimental.pallas import tpu as pltpu

F32 = jnp.float32
BF16 = jnp.bfloat16

D_MODEL = 1024
PAST_LEN = 16384
RET_HEADS = 4
RET_DK = 128
RET_DV = 128
RET_QK = RET_HEADS * RET_DK
RET_V = RET_HEADS * RET_DV
RET_CHUNK = 128
ROPE_BASE = 10000.0
CONV_CH = 512
CONV_WIDTH = 31
N_GROUPS = 4
EXP_PER_GROUP = 4
N_EXPERTS = N_GROUPS * EXP_PER_GROUP
TOP_K = 2
EXP_FF = 512
PLE_DIM = 256
DEPTH = 1
ALPHA = (2 * DEPTH) ** 0.25
LN_EPS = 1e-5
IN_WIDTHS = (RET_QK, RET_QK, RET_V, RET_V, CONV_CH, CONV_CH, D_MODEL, D_MODEL)
IN_OFFS = tuple(int(s) for s in np.cumsum((0,) + IN_WIDTHS))
IN_COLS = IN_OFFS[-1]

LANES = 128
SUBLANES = 8
VMEM_LIMIT = 56 * 1024 * 1024

TL_PROMPT = 256
BB_SAMPLE = 16
TM_FFN = 256
TL_FINAL = 256
CONV_PAD = 32
XPAD_NEW = 32
XPAD_ROWS = 40


def _ln(x, g, b):
    mu = jnp.mean(x, axis=-1, keepdims=True)
    d = x - mu
    var = jnp.mean(d * d, axis=-1, keepdims=True)
    return d * lax.rsqrt(var + LN_EPS) * g + b


def _sigmoid(x):
    return 1.0 / (1.0 + jnp.exp(-x))


def _silu(x):
    return x * _sigmoid(x)


def _bdot(a, b):
    return jnp.dot(a.astype(BF16), b, preferred_element_type=F32)


def _rot(t, cosf, sinf):
    return t * cosf + pltpu.roll(t, RET_DK // 2, axis=1) * sinf


def _route(logits):
    lane = lax.broadcasted_iota(jnp.int32, logits.shape, 1)
    lanef = lane.astype(F32)
    ninf = jnp.float32(-jnp.inf)
    big = jnp.float32(LANES)
    gmask = lane < N_GROUPS
    gl = jnp.where(gmask, logits, ninf)
    gmax = jnp.max(gl, axis=1, keepdims=True)
    gidx = jnp.min(jnp.where(gmask & (gl == gmax), lanef, big), axis=1, keepdims=True)
    sumexp = jnp.sum(jnp.where(gmask, jnp.exp(gl - gmax), 0.0), axis=1, keepdims=True)
    gw = 1.0 / sumexp
    lo = N_GROUPS + EXP_PER_GROUP * gidx
    emask = (lanef >= lo) & (lanef < lo + EXP_PER_GROUP)
    el = jnp.where(emask, logits, ninf)
    m1 = jnp.max(el, axis=1, keepdims=True)
    i1 = jnp.min(jnp.where(emask & (el == m1), lanef, big), axis=1, keepdims=True)
    emask2 = emask & (lanef != i1)
    el2 = jnp.where(emask2, logits, ninf)
    m2 = jnp.max(el2, axis=1, keepdims=True)
    i2 = jnp.min(jnp.where(emask2 & (el2 == m2), lanef, big), axis=1, keepdims=True)
    t = jnp.exp(m2 - m1)
    den = 1.0 + t
    w1 = (1.0 / den) * gw
    w2 = (t / den) * gw
    rw = jnp.where(lane == 0, w1, jnp.where(lane == 1, w2, 0.0))
    ref = jnp.where(lane == 0, i1 - N_GROUPS, jnp.where(lane == 1, i2 - N_GROUPS, 0.0))
    return rw, ref.astype(jnp.int32)


def _post_mix(x, ret_n, g, c_out, gt_a, gt_b, w):
    (w_ret_o, cln_g, cln_b, w_conv_o, w_out, ln1_g, ln1_b, wr_hi, wr_lo, b_r) = w
    branch_a = _bdot(_silu(g) * ret_n, w_ret_o[...])
    branch_b = _bdot(_silu(_ln(c_out, cln_g[...], cln_b[...])), w_conv_o[...])
    mix = _sigmoid(gt_a) * branch_a + _sigmoid(gt_b) * branch_b
    h = ALPHA * x + _bdot(mix, w_out[...])
    x1 = _ln(h, ln1_g[...], ln1_b[...])
    x1_hi = x1.astype(BF16)
    x1_lo = (x1 - x1_hi.astype(F32)).astype(BF16)
    logits = (jnp.dot(x1_hi, wr_hi[...], preferred_element_type=F32)
              + (jnp.dot(x1_lo, wr_hi[...], preferred_element_type=F32)
                 + jnp.dot(x1_hi, wr_lo[...], preferred_element_type=F32))
              + b_r[...])
    rw, re = _route(logits)
    return x1, rw, re


def _prompt_mixer_kernel(x_ref, x1s_ref, rws_ref, res_ref, *refs, n_seq, n_sample_tiles):
    bi = pl.program_id(0)
    li = pl.program_id(1)
    x1_ref, rw_ref, re_ref = refs[-8:-5]

    @pl.when(bi < n_seq)
    def _mix():
        _prompt_mixer_body(x_ref, *refs)

    @pl.when((bi == n_seq) & (li < n_sample_tiles))
    def _append():
        x1_ref[...] = x1s_ref[...]
        rw_ref[...] = rws_ref[...]
        re_ref[...] = res_ref[...]


def _prompt_mixer_body(x_ref, cos_ref, sin_ref, dec_ref, qdec_ref, kdec_ref, cdec_ref,
                       w_in, b_in, gn_g, gn_b, w_ret_o, conv_w, conv_b, cln_g, cln_b,
                       w_conv_o, w_out, ln1_g, ln1_b, wr_hi, wr_lo, b_r,
                       x1_ref, rw_ref, re_ref, sret_ref, sconv_ref,
                       ubuf, ret_scr, cout_scr):
    tl = x_ref.shape[1]
    li = pl.program_id(1)
    nl = pl.num_programs(1)

    @pl.when(li == 0)
    def _init():
        sret_ref[...] = jnp.zeros(sret_ref.shape, F32)
        ubuf[0:CONV_PAD, :] = jnp.zeros((CONV_PAD, CONV_CH), F32)

    x = x_ref[0]
    xb = x.astype(BF16)

    def proj(k):
        c0, c1 = IN_OFFS[k], IN_OFFS[k + 1]
        return jnp.dot(xb, w_in[:, c0:c1], preferred_element_type=F32) + b_in[:, c0:c1]

    q = proj(0)
    k = proj(1)
    v = proj(2)
    scale = RET_DK ** -0.5
    for c in range(tl // RET_CHUNK):
        rows = slice(c * RET_CHUNK, (c + 1) * RET_CHUNK)
        cosf = cos_ref[rows, :]
        sinf = sin_ref[rows, :]
        for h in range(RET_HEADS):
            cols = slice(h * RET_DK, (h + 1) * RET_DK)
            qh = _rot(q[rows, cols], cosf, sinf)
            kh = _rot(k[rows, cols], cosf, sinf) * scale
            qb = qh.astype(BF16)
            kb = kh.astype(BF16)
            vb = v[rows, cols].astype(BF16)
            s_old = sret_ref[0, 0, h]
            scores = lax.dot_general(qb, kb, (((1,), (1,)), ((), ())),
                                     preferred_element_type=F32) * dec_ref[h]
            inner = jnp.dot(scores.astype(BF16), vb, preferred_element_type=F32)
            cross = jnp.dot(qb, s_old.astype(BF16), preferred_element_type=F32) * qdec_ref[h]
            kd = (kh * kdec_ref[h]).astype(BF16)
            s_new = cdec_ref[h] * s_old + lax.dot_general(
                kd, vb, (((0,), (0,)), ((), ())), preferred_element_type=F32)
            sret_ref[0, 0, h] = s_new
            ret_scr[rows, cols] = _ln(inner + cross, gn_g[:, cols], gn_b[:, cols])

    u = proj(4) * _sigmoid(proj(5))
    ubuf[CONV_PAD:CONV_PAD + tl, :] = u
    rb = 32
    for r in range(tl // rb):
        acc = jnp.zeros((rb, CONV_CH), F32) + conv_b[...]
        for j in range(CONV_WIDTH):
            off = r * rb + j + (CONV_PAD - (CONV_WIDTH - 1))
            acc = acc + conv_w[j:j + 1, :] * ubuf[off:off + rb, :]
        cout_scr[r * rb:(r + 1) * rb, :] = acc
    ubuf[0:CONV_PAD, :] = ubuf[tl:tl + CONV_PAD, :]

    @pl.when(li == nl - 1)
    def _conv_state():
        sconv_ref[0, 0] = u[tl - (CONV_WIDTH - 1):tl, :]

    g = proj(3)
    gt_a = proj(6)
    gt_b = proj(7)
    x1, rw, re = _post_mix(x, ret_scr[...], g, cout_scr[...], gt_a, gt_b,
                           (w_ret_o, cln_g, cln_b, w_conv_o, w_out, ln1_g, ln1_b, wr_hi, wr_lo, b_r))
    x1_ref[...] = x1
    rw_ref[...] = rw
    re_ref[...] = re


def _sample_mixer_kernel(x_ref, cos_ref, sin_ref, pdec_ref, qdec_ref, kdec_ref, cdec_ref, wsh_ref,
                         sret_in, sconv_in,
                         w_in, b_in, gn_g, gn_b, w_ret_o, conv_b, cln_g, cln_b,
                         w_conv_o, w_out, ln1_g, ln1_b, wr_hi, wr_lo, b_r,
                         x1_ref, rw_ref, re_ref, sret_ref, sconv_ref,
                         ret_scr, cout_scr, xpad):
    t = x_ref.shape[0]
    ls = t // BB_SAMPLE
    x = x_ref[...]
    xb = x.astype(BF16)

    def proj(k):
        c0, c1 = IN_OFFS[k], IN_OFFS[k + 1]
        return jnp.dot(xb, w_in[:, c0:c1], preferred_element_type=F32) + b_in[:, c0:c1]

    q = proj(0)
    k = proj(1)
    v = proj(2)
    scale = RET_DK ** -0.5
    cosf = cos_ref[...]
    sinf = sin_ref[...]
    row = lax.broadcasted_iota(jnp.int32, (t, RET_DK), 0)
    pos = row % ls
    row8 = lax.broadcasted_iota(jnp.int32, (SUBLANES, RET_DK), 0)
    per_tile = SUBLANES // ls
    for h in range(RET_HEADS):
        cols = slice(h * RET_DK, (h + 1) * RET_DK)
        qh = _rot(q[:, cols], cosf, sinf)
        kh = _rot(k[:, cols], cosf, sinf) * scale
        vh = v[:, cols]
        inner = jnp.zeros((t, RET_DV), F32)
        for s in range(ls):
            ks = kh if s == 0 else pltpu.roll(kh, s, axis=0)
            vs = vh if s == 0 else pltpu.roll(vh, s, axis=0)
            dotp = jnp.sum(qh * ks, axis=1, keepdims=True) * pdec_ref[h, s]
            inner = inner + jnp.where(pos >= s, dotp, 0.0) * vs
        kd = kh * kdec_ref[h]
        for tile in range(t // SUBLANES):
            rows = slice(tile * SUBLANES, (tile + 1) * SUBLANES)
            q8 = qh[rows, :]
            kd8 = kd[rows, :]
            v8 = vh[rows, :]
            cross8 = jnp.zeros((SUBLANES, RET_DV), F32)
            for sub in range(per_tile):
                b = tile * per_tile + sub
                mine = (row8 >= sub * ls) & (row8 < (sub + 1) * ls)
                s_old = sret_in[0, b, h]
                c_b = jnp.dot(q8, s_old, preferred_element_type=F32)
                cross8 = jnp.where(mine, c_b, cross8)
                upd = lax.dot_general(jnp.where(mine, kd8, 0.0), v8, (((0,), (0,)), ((), ())),
                                      preferred_element_type=F32)
                sret_ref[0, b, h] = cdec_ref[h] * s_old + upd
            ret_scr[rows, cols] = inner[rows, :] + cross8 * qdec_ref[h, rows, :]
        ret_scr[:, cols] = _ln(ret_scr[:, cols], gn_g[:, cols], gn_b[:, cols])

    u = proj(4) * _sigmoid(proj(5))
    nstate = CONV_WIDTH - 1
    xpad[...] = jnp.zeros(xpad.shape, F32)
    xpad[:, 0:nstate, :] = sconv_in[0]
    for b in range(BB_SAMPLE):
        xpad[b, XPAD_NEW:XPAD_NEW + ls, :] = u[b * ls:(b + 1) * ls, :]
    for i in range(ls):
        res = jnp.sum(xpad[...] * wsh_ref[i][None], axis=1) + conv_b[...]
        for sl in range(CONV_CH // LANES):
            cout_scr[sl, pl.ds(i, BB_SAMPLE, stride=ls), :] = res[:, sl * LANES:(sl + 1) * LANES]
    sconv_ref[0, :, 0:nstate - ls, :] = xpad[:, ls:nstate, :]
    sconv_ref[0, :, nstate - ls:nstate, :] = xpad[:, XPAD_NEW:XPAD_NEW + ls, :]
    c_out = jnp.concatenate([cout_scr[sl] for sl in range(CONV_CH // LANES)], axis=1)

    g = proj(3)
    gt_a = proj(6)
    gt_b = proj(7)
    x1, rw, re = _post_mix(x, ret_scr[...], g, c_out, gt_a, gt_b,
                           (w_ret_o, cln_g, cln_b, w_conv_o, w_out, ln1_g, ln1_b, wr_hi, wr_lo, b_r))
    x1_ref[...] = x1
    rw_ref[...] = rw
    re_ref[...] = re


def _ffn_kernel(te_ref, tv_ref, src_ref, x1_hbm, w_gu, w_dn, ys_ref, xbuf, sem):
    i = pl.program_id(0)
    tm = xbuf.shape[0]

    @pl.when(tv_ref[i] == 1)
    def _tile():
        def issue(r, carry):
            tok = src_ref[0, 0, r]
            pltpu.make_async_copy(x1_hbm.at[pl.ds(tok, 1)], xbuf.at[pl.ds(r, 1)], sem).start()
            return carry
        lax.fori_loop(0, tm, issue, 0)
        pltpu.make_async_copy(x1_hbm.at[pl.ds(0, tm)], xbuf, sem).wait()
        hcat = _bdot(xbuf[...], w_gu[0])
        act = _silu(hcat[:, :EXP_FF]) * hcat[:, EXP_FF:]
        ys_ref[...] = _bdot(act, w_dn[0])

    @pl.when(tv_ref[i] == 0)
    def _skip():
        ys_ref[...] = jnp.zeros(ys_ref.shape, F32)


def _final_kernel(pos_ref, ys_hbm, x1_ref, rw_ref, pp_ref, ps_ref, ln2_g, ln2_b, w_pg, b_pg, w_ple,
                  yp_ref, ys_out_ref, ybuf, sem, *, n_prompt_tiles):
    i = pl.program_id(0)
    tl = x1_ref.shape[0]

    def issue(r, carry):
        for kk in range(TOP_K):
            p = pos_ref[0, 0, TOP_K * r + kk]
            pltpu.make_async_copy(ys_hbm.at[pl.ds(p, 1)], ybuf.at[kk, pl.ds(r, 1)], sem).start()
        return carry
    lax.fori_loop(0, tl, issue, 0)
    for kk in range(TOP_K):
        pltpu.make_async_copy(ys_hbm.at[pl.ds(0, tl)], ybuf.at[kk], sem).wait()

    x1 = x1_ref[...]
    rw = rw_ref[...]
    moe = ybuf[0] * rw[:, 0:1] + ybuf[1] * rw[:, 1:2]
    x2 = _ln(ALPHA * x1 + moe, ln2_g[...], ln2_b[...])
    gate = _sigmoid(_bdot(x2, w_pg[...]) + b_pg[...])

    @pl.when(i < n_prompt_tiles)
    def _prompt():
        yp_ref[...] = x2 + gate * _bdot(pp_ref[...], w_ple[...])

    @pl.when(i >= n_prompt_tiles)
    def _sample():
        ys_out_ref[...] = x2 + gate * _bdot(ps_ref[...], w_ple[...])


def _rope_tables(pos):
    half = RET_DK // 2
    inv_freq = ROPE_BASE ** (-jnp.arange(half, dtype=F32) / half)
    ang = pos[:, None] * inv_freq[None, :]
    cos = jnp.cos(ang)
    sin = jnp.sin(ang)
    return jnp.concatenate([cos, cos], axis=-1), jnp.concatenate([-sin, sin], axis=-1)


def _log_gamma():
    return jnp.log(1.0 - 2.0 ** (-5.0 - jnp.arange(RET_HEADS, dtype=F32)))


def _const_spec(shape):
    nd = len(shape)
    return pl.BlockSpec(shape, lambda *_: (0,) * nd, pipeline_mode=pl.Buffered(1))


def kernel(x_prompt, x_sample, state_ret, state_conv, p_prompt, p_sample, w_in, b_in, ret_gn_g, ret_gn_b,
           w_ret_o, conv_w, conv_b, conv_ln_g, conv_ln_b, w_conv_o, w_out, ln1_g, ln1_b, w_grp, b_grp,
           w_exp, b_exp, w_gu, w_dn, ln2_g, ln2_b, w_pg, b_pg, w_ple):
    assert DEPTH == 1 and w_in.shape[0] == 1
    bp, lp, d = x_prompt.shape
    bs, ls, _ = x_sample.shape
    n_p, n_s = bp * lp, bs * ls
    n_tok = n_p + n_s
    assert lp % TL_PROMPT == 0 and bs % BB_SAMPLE == 0 and SUBLANES % ls == 0
    assert n_p % TL_FINAL == 0 and n_s % TL_FINAL == 0 and n_p % (BB_SAMPLE * ls) == 0

    lg = _log_gamma()
    c = RET_CHUNK
    idx = jnp.arange(c, dtype=F32)
    rel = idx[:, None] - idx[None, :]
    causal = rel >= 0
    decay = jnp.where(causal[None], jnp.exp(jnp.where(causal, rel, 0.0)[None] * lg[:, None, None]), 0.0)
    q_decay = jnp.exp((idx[:, None] + 1.0) * lg[None, :])
    k_decay = jnp.exp((c - 1.0 - idx[:, None]) * lg[None, :])
    chunk_decay = jnp.exp(c * lg)
    qdec_p = jnp.broadcast_to(q_decay.T[:, :, None], (RET_HEADS, c, RET_DK))
    kdec_p = jnp.broadcast_to(k_decay.T[:, :, None], (RET_HEADS, c, RET_DK))
    cdec_p = jnp.broadcast_to(chunk_decay[:, None, None], (RET_HEADS, 1, RET_DV))
    cos_p, sin_p = _rope_tables(jnp.arange(lp, dtype=F32))

    ts = BB_SAMPLE * ls
    idx_s = jnp.arange(ls, dtype=F32)
    rel_s = jnp.arange(ls, dtype=F32)
    pdec_s = jnp.exp(rel_s[None, :] * lg[:, None])
    pdec_s = jnp.broadcast_to(pdec_s[:, :, None, None], (RET_HEADS, ls, 1, RET_DK))
    qd_s = jnp.exp((idx_s[:, None] + 1.0) * lg[None, :])
    kd_s = jnp.exp((ls - 1.0 - idx_s[:, None]) * lg[None, :])
    qdec_s = jnp.broadcast_to(jnp.tile(qd_s.T, (1, BB_SAMPLE))[:, :, None], (RET_HEADS, ts, RET_DK))
    kdec_s = jnp.broadcast_to(jnp.tile(kd_s.T, (1, BB_SAMPLE))[:, :, None], (RET_HEADS, ts, RET_DK))
    cdec_s = jnp.broadcast_to(jnp.exp(ls * lg)[:, None, None], (RET_HEADS, 1, RET_DV))
    pos_s = PAST_LEN + jnp.arange(ls, dtype=F32)
    cos_s, sin_s = _rope_tables(jnp.tile(pos_s, BB_SAMPLE))

    w_in_b = w_in[0].astype(BF16)
    w_ret_o_b = w_ret_o[0].astype(BF16)
    w_conv_o_b = w_conv_o[0].astype(BF16)
    w_out_b = w_out[0].astype(BF16)
    w_pg_b = w_pg[0].astype(BF16)
    w_ple_b = w_ple[0].astype(BF16)
    w_gu_b = w_gu[0].astype(BF16)
    w_dn_b = w_dn[0].astype(BF16)
    w_r = jnp.zeros((d, LANES), F32).at[:, :N_GROUPS].set(w_grp[0]).at[:, N_GROUPS:N_GROUPS + N_EXPERTS].set(w_exp[0])
    wr_hi = w_r.astype(BF16)
    wr_lo = (w_r - wr_hi.astype(F32)).astype(BF16)
    b_r = jnp.zeros((1, LANES), F32).at[0, :N_GROUPS].set(b_grp[0]).at[0, N_GROUPS:N_GROUPS + N_EXPERTS].set(b_exp[0])
    row = lambda a: a.reshape(1, -1)
    conv_w0 = conv_w[0]
    nstate = CONV_WIDTH - 1
    win_row = np.array([m if m < nstate else XPAD_NEW + (m - nstate) for m in range(nstate + ls)])
    wsh = jnp.stack([jnp.zeros((XPAD_ROWS, CONV_CH), F32).at[win_row[i:i + CONV_WIDTH]].set(conv_w0)
                     for i in range(ls)])

    shared_w = (w_in_b, row(b_in[0]), row(ret_gn_g[0]), row(ret_gn_b[0]), w_ret_o_b)
    tail_w = (row(conv_ln_g[0]), row(conv_ln_b[0]), w_conv_o_b, w_out_b, row(ln1_g[0]), row(ln1_b[0]),
              wr_hi, wr_lo, b_r)

    nbt = bs // BB_SAMPLE
    xs2 = x_sample.reshape(n_s, d)
    sample_in = ((xs2, cos_s, sin_s, pdec_s, qdec_s, kdec_s, cdec_s, wsh, state_ret, state_conv)
                 + shared_w + (row(conv_b[0]),) + tail_w)
    sample_specs = (
        [pl.BlockSpec((ts, d), lambda i: (i, 0))]
        + [_const_spec(a.shape) for a in sample_in[1:8]]
        + [pl.BlockSpec((1, BB_SAMPLE, RET_HEADS, RET_DK, RET_DV), lambda i: (0, i, 0, 0, 0)),
           pl.BlockSpec((1, BB_SAMPLE, nstate, CONV_CH), lambda i: (0, i, 0, 0))]
        + [_const_spec(a.shape) for a in sample_in[10:]]
    )
    tok_spec_s = lambda w: pl.BlockSpec((ts, w), lambda i: (i, 0))
    x1_s, rw_s, re_s, ret_s, conv_s = pl.pallas_call(
        _sample_mixer_kernel,
        grid=(nbt,),
        in_specs=sample_specs,
        out_specs=[
            tok_spec_s(d), tok_spec_s(LANES), tok_spec_s(LANES),
            pl.BlockSpec((1, BB_SAMPLE, RET_HEADS, RET_DK, RET_DV), lambda i: (0, i, 0, 0, 0)),
            pl.BlockSpec((1, BB_SAMPLE, nstate, CONV_CH), lambda i: (0, i, 0, 0)),
        ],
        out_shape=[
            jax.ShapeDtypeStruct((n_s, d), F32),
            jax.ShapeDtypeStruct((n_s, LANES), F32),
            jax.ShapeDtypeStruct((n_s, LANES), jnp.int32),
            jax.ShapeDtypeStruct(state_ret.shape, F32),
            jax.ShapeDtypeStruct(state_conv.shape, F32),
        ],
        scratch_shapes=[
            pltpu.VMEM((ts, RET_V), F32),
            pltpu.VMEM((CONV_CH // LANES, ts, LANES), F32),
            pltpu.VMEM((BB_SAMPLE, XPAD_ROWS, CONV_CH), F32),
        ],
        compiler_params=pltpu.CompilerParams(
            dimension_semantics=("arbitrary",), vmem_limit_bytes=VMEM_LIMIT),
        name="sample_mixer",
    )(*sample_in)

    nlt = lp // TL_PROMPT
    nst = n_s // TL_PROMPT
    assert n_s % TL_PROMPT == 0 and nst <= nlt
    prompt_in = ((x_prompt, x1_s, rw_s, re_s, cos_p, sin_p, decay, qdec_p, kdec_p, cdec_p)
                 + shared_w + (conv_w0, row(conv_b[0])) + tail_w)
    seq_idx = lambda b: jnp.minimum(b, bp - 1)
    tail_idx = lambda l: jnp.minimum(l, nst - 1)
    tail_spec = lambda w: pl.BlockSpec((TL_PROMPT, w), lambda b, l: (tail_idx(l), 0))
    prompt_specs = [
        pl.BlockSpec((1, TL_PROMPT, d), lambda b, l: (seq_idx(b), l, 0)),
        tail_spec(d), tail_spec(LANES), tail_spec(LANES),
        pl.BlockSpec((TL_PROMPT, RET_DK), lambda b, l: (l, 0)),
        pl.BlockSpec((TL_PROMPT, RET_DK), lambda b, l: (l, 0)),
    ] + [_const_spec(a.shape) for a in prompt_in[6:]]
    tok_spec_p = lambda w: pl.BlockSpec(
        (TL_PROMPT, w), lambda b, l: (jnp.where(b < bp, b * nlt + l, bp * nlt + tail_idx(l)), 0))
    x1_all, rw_all, re_all, ret_p, conv_p = pl.pallas_call(
        functools.partial(_prompt_mixer_kernel, n_seq=bp, n_sample_tiles=nst),
        grid=(bp + 1, nlt),
        in_specs=prompt_specs,
        out_specs=[
            tok_spec_p(d), tok_spec_p(LANES), tok_spec_p(LANES),
            pl.BlockSpec((1, 1, RET_HEADS, RET_DK, RET_DV), lambda b, l: (0, seq_idx(b), 0, 0, 0)),
            pl.BlockSpec((1, 1, nstate, CONV_CH), lambda b, l: (0, seq_idx(b), 0, 0)),
        ],
        out_shape=[
            jax.ShapeDtypeStruct((n_tok, d), F32),
            jax.ShapeDtypeStruct((n_tok, LANES), F32),
            jax.ShapeDtypeStruct((n_tok, LANES), jnp.int32),
            jax.ShapeDtypeStruct((1, bp, RET_HEADS, RET_DK, RET_DV), F32),
            jax.ShapeDtypeStruct((1, bp, nstate, CONV_CH), F32),
        ],
        scratch_shapes=[
            pltpu.VMEM((TL_PROMPT + CONV_PAD, CONV_CH), F32),
            pltpu.VMEM((TL_PROMPT, RET_V), F32),
            pltpu.VMEM((TL_PROMPT, CONV_CH), F32),
        ],
        compiler_params=pltpu.CompilerParams(
            dimension_semantics=("arbitrary", "arbitrary"), vmem_limit_bytes=VMEM_LIMIT),
        name="prompt_mixer",
    )(*prompt_in)

    n_asg = n_tok * TOP_K
    n_tiles = (n_asg + N_EXPERTS * (TM_FFN - 1)) // TM_FFN
    n_rows = n_tiles * TM_FFN
    flat_e = re_all[:, :TOP_K].reshape(n_asg)
    onehot = (flat_e[:, None] == jnp.arange(N_EXPERTS, dtype=jnp.int32)[None, :]).astype(jnp.int32)
    csum = jnp.cumsum(onehot, axis=0)
    rank = jnp.take_along_axis(csum, flat_e[:, None], axis=1)[:, 0] - 1
    counts = csum[-1]
    tiles_e = (counts + TM_FFN - 1) // TM_FFN
    tile_end = jnp.cumsum(tiles_e)
    offs = (tile_end - tiles_e) * TM_FFN
    pos = (offs[flat_e] + rank).astype(jnp.int32)
    tile_ids = jnp.arange(n_tiles, dtype=jnp.int32)
    tile_e = jnp.minimum(jnp.searchsorted(tile_end, tile_ids, side="right"), N_EXPERTS - 1).astype(jnp.int32)
    tile_v = (tile_ids < tile_end[-1]).astype(jnp.int32)
    src = jnp.zeros((n_rows,), jnp.int32).at[pos].set(jnp.arange(n_asg, dtype=jnp.int32) // TOP_K)

    ys = pl.pallas_call(
        _ffn_kernel,
        grid_spec=pltpu.PrefetchScalarGridSpec(
            num_scalar_prefetch=2,
            grid=(n_tiles,),
            in_specs=[
                pl.BlockSpec((1, 1, TM_FFN), lambda i, te, tv: (i, 0, 0), memory_space=pltpu.SMEM),
                pl.BlockSpec(memory_space=pl.ANY),
                pl.BlockSpec((1, d, 2 * EXP_FF), lambda i, te, tv: (te[i], 0, 0)),
                pl.BlockSpec((1, EXP_FF, d), lambda i, te, tv: (te[i], 0, 0)),
            ],
            out_specs=pl.BlockSpec((TM_FFN, d), lambda i, te, tv: (i, 0)),
            scratch_shapes=[pltpu.VMEM((TM_FFN, d), F32), pltpu.SemaphoreType.DMA],
        ),
        out_shape=jax.ShapeDtypeStruct((n_rows, d), F32),
        compiler_params=pltpu.CompilerParams(
            dimension_semantics=("arbitrary",), vmem_limit_bytes=VMEM_LIMIT),
        name="expert_ffn",
    )(tile_e, tile_v, src.reshape(n_tiles, 1, TM_FFN), x1_all, w_gu_b, w_dn_b)

    npt = n_p // TL_FINAL
    nft = n_tok // TL_FINAL
    pp2 = p_prompt.reshape(n_p, PLE_DIM)
    ps2 = p_sample.reshape(n_s, PLE_DIM)
    tok_f = lambda w: pl.BlockSpec((TL_FINAL, w), lambda i: (i, 0))
    y_p, y_s = pl.pallas_call(
        functools.partial(_final_kernel, n_prompt_tiles=npt),
        grid=(nft,),
        in_specs=[
            pl.BlockSpec((1, 1, TOP_K * TL_FINAL), lambda i: (i, 0, 0), memory_space=pltpu.SMEM),
            pl.BlockSpec(memory_space=pl.ANY),
            tok_f(d), tok_f(LANES),
            pl.BlockSpec((TL_FINAL, PLE_DIM), lambda i: (jnp.minimum(i, npt - 1), 0)),
            pl.BlockSpec((TL_FINAL, PLE_DIM), lambda i: (jnp.maximum(i - npt, 0), 0)),
            _const_spec((1, d)), _const_spec((1, d)), _const_spec((d, d)), _const_spec((1, d)),
            _const_spec((PLE_DIM, d)),
        ],
        out_specs=[
            pl.BlockSpec((TL_FINAL, d), lambda i: (jnp.minimum(i, npt - 1), 0)),
            pl.BlockSpec((TL_FINAL, d), lambda i: (jnp.maximum(i - npt, 0), 0)),
        ],
        out_shape=[jax.ShapeDtypeStruct((n_p, d), F32), jax.ShapeDtypeStruct((n_s, d), F32)],
        scratch_shapes=[pltpu.VMEM((TOP_K, TL_FINAL, d), F32), pltpu.SemaphoreType.DMA],
        compiler_params=pltpu.CompilerParams(
            dimension_semantics=("arbitrary",), vmem_limit_bytes=VMEM_LIMIT),
        name="moe_combine_final",
    )(pos.reshape(nft, 1, TOP_K * TL_FINAL), ys, x1_all, rw_all, pp2, ps2,
      row(ln2_g[0]), row(ln2_b[0]), w_pg_b, row(b_pg[0]), w_ple_b)

    return (y_p.reshape(bp, lp, d), y_s.reshape(bs, ls, d), ret_p, conv_p, ret_s, conv_s)
```

```python
import functools

import jax
import jax.numpy as jnp
import numpy as np
from jax import lax
from jax.experimental import pallas as pl
from jax.experimental.pallas import tpu as pltpu

F32 = jnp.float32
BF16 = jnp.bfloat16
I32 = jnp.int32
U32 = jnp.uint32

D_MODEL = 1024
PAST_LEN = 16384
RET_HEADS = 4
RET_DK = 128
RET_DV = 128
RET_QK = RET_HEADS * RET_DK
RET_V = RET_HEADS * RET_DV
RET_CHUNK = 128
ROPE_BASE = 10000.0
CONV_CH = 512
CONV_WIDTH = 31
N_GROUPS = 4
EXP_PER_GROUP = 4
N_EXPERTS = N_GROUPS * EXP_PER_GROUP
TOP_K = 2
EXP_FF = 512
PLE_DIM = 256
DEPTH = 1
ALPHA = (2 * DEPTH) ** 0.25
LN_EPS = 1e-5
IN_WIDTHS = (RET_QK, RET_QK, RET_V, RET_V, CONV_CH, CONV_CH, D_MODEL, D_MODEL)
IN_OFFS = tuple(int(s) for s in np.cumsum((0,) + IN_WIDTHS))

LANES = 128
SUBLANES = 8
VMEM_LIMIT = 56 * 1024 * 1024

TL = 256
BB_SAMPLE = 16
CHUNK = SUBLANES
BLOCK_CHUNKS = -(-(TOP_K * TL + N_EXPERTS * (CHUNK - 1)) // (CHUNK * 16)) * 16
CAP = BLOCK_CHUNKS * CHUNK
TILE_CHUNKS = 32
TM_FFN = TILE_CHUNKS * CHUNK
HALF = D_MODEL // 2
CONV_PAD = 32
XPAD_NEW = 32
XPAD_ROWS = 40


def _ln(x, g, b):
    mu = jnp.mean(x, axis=-1, keepdims=True)
    d = x - mu
    var = jnp.mean(d * d, axis=-1, keepdims=True)
    return d * lax.rsqrt(var + LN_EPS) * g + b


def _sigmoid(x):
    return 1.0 / (1.0 + jnp.exp(-x))


def _silu(x):
    return x * _sigmoid(x)


def _bdot(a, b):
    return jnp.dot(a.astype(BF16), b, preferred_element_type=F32)


def _rot(t, cosf, sinf):
    return t * cosf + pltpu.roll(t, RET_DK // 2, axis=1) * sinf


def _pack_bf16_pairs(x):
    bits = lax.bitcast_convert_type(x.astype(BF16).astype(F32), U32)
    half = x.shape[1] // 2
    return (bits[:, :half] >> 16) | (bits[:, half:] & jnp.uint32(0xFFFF0000))


def _unpack_bf16_pairs(pk):
    lo = lax.bitcast_convert_type(pk << 16, F32).astype(BF16)
    hi = lax.bitcast_convert_type(pk & jnp.uint32(0xFFFF0000), F32).astype(BF16)
    return lo, hi


def _lane_tile(cols, rows):
    lane = lax.broadcasted_iota(I32, (rows, LANES), 1)
    out = jnp.zeros((rows, LANES), F32)
    for i, col in enumerate(cols):
        out = jnp.where(lane == i, col, out)
    return out


def _route(logits):
    lane = lax.broadcasted_iota(I32, logits.shape, 1)
    lanef = lane.astype(F32)
    ninf = jnp.float32(-jnp.inf)
    big = jnp.float32(LANES)
    gmask = lane < N_GROUPS
    gl = jnp.where(gmask, logits, ninf)
    gmax = jnp.max(gl, axis=1, keepdims=True)
    gidx = jnp.min(jnp.where(gmask & (gl == gmax), lanef, big), axis=1, keepdims=True)
    sumexp = jnp.sum(jnp.where(gmask, jnp.exp(gl - gmax), 0.0), axis=1, keepdims=True)
    gw = 1.0 / sumexp
    lo = N_GROUPS + EXP_PER_GROUP * gidx
    emask = (lanef >= lo) & (lanef < lo + EXP_PER_GROUP)
    el = jnp.where(emask, logits, ninf)
    m1 = jnp.max(el, axis=1, keepdims=True)
    i1 = jnp.min(jnp.where(emask & (el == m1), lanef, big), axis=1, keepdims=True)
    emask2 = emask & (lanef != i1)
    el2 = jnp.where(emask2, logits, ninf)
    m2 = jnp.max(el2, axis=1, keepdims=True)
    i2 = jnp.min(jnp.where(emask2 & (el2 == m2), lanef, big), axis=1, keepdims=True)
    t = jnp.exp(m2 - m1)
    den = 1.0 + t
    return (1.0 / den) * gw, (t / den) * gw, i1 - N_GROUPS, i2 - N_GROUPS


def _post_mix(x, ret_n, g, c_out, gt_a, gt_b, w):
    (w_ret_o, cln_g, cln_b, w_conv_o, w_out, ln1_g, ln1_b, wr_hi, wr_lo, b_r) = w
    branch_a = _bdot(_silu(g) * ret_n, w_ret_o[...])
    branch_b = _bdot(_silu(_ln(c_out, cln_g[...], cln_b[...])), w_conv_o[...])
    mix = _sigmoid(gt_a) * branch_a + _sigmoid(gt_b) * branch_b
    h = ALPHA * x + _bdot(mix, w_out[...])
    x1 = _ln(h, ln1_g[...], ln1_b[...])
    x1_hi = x1.astype(BF16)
    x1_lo = (x1 - x1_hi.astype(F32)).astype(BF16)
    logits = (jnp.dot(x1_hi, wr_hi[...], preferred_element_type=F32)
              + (jnp.dot(x1_lo, wr_hi[...], preferred_element_type=F32)
                 + jnp.dot(x1_hi, wr_lo[...], preferred_element_type=F32))
              + b_r[...])
    return (x1,) + _route(logits)


def _sort_tile(x1, w1, w2, e1, e2, x1_ref, rw_ref, xs_ref, meta_ref):
    t = x1.shape[0]
    lane = lax.broadcasted_iota(I32, (t, LANES), 1).astype(F32)
    a1 = (lane == e1).astype(F32)
    a2 = (lane == e2).astype(F32)
    ri = lax.broadcasted_iota(I32, (t, t), 0)
    ci = lax.broadcasted_iota(I32, (t, t), 1)
    earlier = (ci < ri).astype(BF16)
    r1 = jnp.dot(earlier, a1.astype(BF16), preferred_element_type=F32)
    r2 = jnp.dot(earlier, a2.astype(BF16), preferred_element_type=F32)
    cnt1 = jnp.sum(a1, axis=0, keepdims=True)
    cnt = cnt1 + jnp.sum(a2, axis=0, keepdims=True)
    nch = jnp.floor((cnt + (CHUNK - 1.0)) * (1.0 / CHUNK))
    ui = lax.broadcasted_iota(I32, (LANES, LANES), 0)
    uj = lax.broadcasted_iota(I32, (LANES, LANES), 1)
    before = (ui < uj).astype(BF16)
    off = jnp.dot(jnp.broadcast_to(nch, (2 * SUBLANES, LANES)).astype(BF16), before,
                  preferred_element_type=F32)[0:1, :]
    base = off * CHUNK
    pos1 = jnp.sum(a1 * (base + r1), axis=1, keepdims=True)
    pos2 = jnp.sum(a2 * (base + cnt1 + r2), axis=1, keepdims=True)
    slot = lax.broadcasted_iota(I32, (t, CAP), 1).astype(F32)
    onehot_t = ((slot == pos1) | (slot == pos2)).astype(BF16)
    xs = lax.dot_general(onehot_t, x1.astype(BF16), (((0,), (0,)), ((), ())), preferred_element_type=F32)
    x1_ref[...] = x1
    rw_ref[...] = _lane_tile((w1, w2, pos1, pos2), t)
    xs_ref[...] = _pack_bf16_pairs(xs)
    srow = lax.broadcasted_iota(I32, (SUBLANES, LANES), 0)
    meta = jnp.where(srow == 0, cnt, jnp.where(srow == 1, off, 0.0))
    meta_ref[...] = meta.astype(I32)


def _prompt_mixer_kernel(x_ref, x1s_ref, rws_ref, *refs, n_seq, n_sample_tiles):
    bi = pl.program_id(0)
    li = pl.program_id(1)
    outs = refs[-9:-5]

    @pl.when(bi < n_seq)
    def _mix():
        _prompt_mixer_body(x_ref, *refs)

    @pl.when((bi == n_seq) & (li < n_sample_tiles))
    def _append():
        rws = rws_ref[...]
        _sort_tile(x1s_ref[...], rws[:, 0:1], rws[:, 1:2], rws[:, 2:3], rws[:, 3:4], *outs)


def _prompt_mixer_body(x_ref, cos_ref, sin_ref, dec_ref, qdec_ref, kdec_ref, cdec_ref,
                       w_in, b_in, gn_g, gn_b, w_ret_o, conv_w, conv_b, cln_g, cln_b,
                       w_conv_o, w_out, ln1_g, ln1_b, wr_hi, wr_lo, b_r,
                       x1_ref, rw_ref, xs_ref, meta_ref, sret_ref, sconv_ref,
                       ubuf, ret_scr, cout_scr):
    tl = x_ref.shape[1]
    li = pl.program_id(1)
    nl = pl.num_programs(1)

    @pl.when(li == 0)
    def _init():
        sret_ref[...] = jnp.zeros(sret_ref.shape, F32)
        ubuf[0:CONV_PAD, :] = jnp.zeros((CONV_PAD, CONV_CH), F32)

    x = x_ref[0]
    xb = x.astype(BF16)

    def proj(k):
        c0, c1 = IN_OFFS[k], IN_OFFS[k + 1]
        return jnp.dot(xb, w_in[:, c0:c1], preferred_element_type=F32) + b_in[:, c0:c1]

    q = proj(0)
    k = proj(1)
    v = proj(2)
    scale = RET_DK ** -0.5
    for c in range(tl // RET_CHUNK):
        rows = slice(c * RET_CHUNK, (c + 1) * RET_CHUNK)
        cosf = cos_ref[rows, :]
        sinf = sin_ref[rows, :]
        for h in range(RET_HEADS):
            cols = slice(h * RET_DK, (h + 1) * RET_DK)
            qh = _rot(q[rows, cols], cosf, sinf)
            kh = _rot(k[rows, cols], cosf, sinf) * scale
            qb = qh.astype(BF16)
            kb = kh.astype(BF16)
            vb = v[rows, cols].astype(BF16)
            s_old = sret_ref[0, 0, h]
            scores = lax.dot_general(qb, kb, (((1,), (1,)), ((), ())),
                                     preferred_element_type=F32) * dec_ref[h]
            inner = jnp.dot(scores.astype(BF16), vb, preferred_element_type=F32)
            cross = jnp.dot(qb, s_old.astype(BF16), preferred_element_type=F32) * qdec_ref[h]
            kd = (kh * kdec_ref[h]).astype(BF16)
            s_new = cdec_ref[h] * s_old + lax.dot_general(
                kd, vb, (((0,), (0,)), ((), ())), preferred_element_type=F32)
            sret_ref[0, 0, h] = s_new
            ret_scr[rows, cols] = _ln(inner + cross, gn_g[:, cols], gn_b[:, cols])

    u = proj(4) * _sigmoid(proj(5))
    ubuf[CONV_PAD:CONV_PAD + tl, :] = u
    rb = 32
    for r in range(tl // rb):
        acc = jnp.zeros((rb, CONV_CH), F32) + conv_b[...]
        for j in range(CONV_WIDTH):
            off = r * rb + j + (CONV_PAD - (CONV_WIDTH - 1))
            acc = acc + conv_w[j:j + 1, :] * ubuf[off:off + rb, :]
        cout_scr[r * rb:(r + 1) * rb, :] = acc
    ubuf[0:CONV_PAD, :] = ubuf[tl:tl + CONV_PAD, :]

    @pl.when(li == nl - 1)
    def _conv_state():
        sconv_ref[0, 0] = u[tl - (CONV_WIDTH - 1):tl, :]

    g = proj(3)
    gt_a = proj(6)
    gt_b = proj(7)
    x1, w1, w2, e1, e2 = _post_mix(
        x, ret_scr[...], g, cout_scr[...], gt_a, gt_b,
        (w_ret_o, cln_g, cln_b, w_conv_o, w_out, ln1_g, ln1_b, wr_hi, wr_lo, b_r))
    _sort_tile(x1, w1, w2, e1, e2, x1_ref, rw_ref, xs_ref, meta_ref)


def _sample_mixer_kernel(x_ref, cos_ref, sin_ref, pdec_ref, qdec_ref, kdec_ref, cdec_ref, wsh_ref,
                         sret_in, sconv_in,
                         w_in, b_in, gn_g, gn_b, w_ret_o, conv_b, cln_g, cln_b,
                         w_conv_o, w_out, ln1_g, ln1_b, wr_hi, wr_lo, b_r,
                         x1_ref, rw_ref, sret_ref, sconv_ref,
                         ret_scr, cout_scr, xpad):
    t = x_ref.shape[0]
    ls = t // BB_SAMPLE
    x = x_ref[...]
    xb = x.astype(BF16)

    def proj(k):
        c0, c1 = IN_OFFS[k], IN_OFFS[k + 1]
        return jnp.dot(xb, w_in[:, c0:c1], preferred_element_type=F32) + b_in[:, c0:c1]

    q = proj(0)
    k = proj(1)
    v = proj(2)
    scale = RET_DK ** -0.5
    cosf = cos_ref[...]
    sinf = sin_ref[...]
    row = lax.broadcasted_iota(I32, (t, RET_DK), 0)
    pos = row % ls
    row8 = lax.broadcasted_iota(I32, (SUBLANES, RET_DK), 0)
    per_tile = SUBLANES // ls
    for h in range(RET_HEADS):
        cols = slice(h * RET_DK, (h + 1) * RET_DK)
        qh = _rot(q[:, cols], cosf, sinf)
        kh = _rot(k[:, cols], cosf, sinf) * scale
        vh = v[:, cols]
        inner = jnp.zeros((t, RET_DV), F32)
        for s in range(ls):
            ks = kh if s == 0 else pltpu.roll(kh, s, axis=0)
            vs = vh if s == 0 else pltpu.roll(vh, s, axis=0)
            dotp = jnp.sum(qh * ks, axis=1, keepdims=True) * pdec_ref[h, s]
            inner = inner + jnp.where(pos >= s, dotp, 0.0) * vs
        kd = kh * kdec_ref[h]
        for tile in range(t // SUBLANES):
            rows = slice(tile * SUBLANES, (tile + 1) * SUBLANES)
            q8 = qh[rows, :]
            kd8 = kd[rows, :]
            v8 = vh[rows, :]
            cross8 = jnp.zeros((SUBLANES, RET_DV), F32)
            for sub in range(per_tile):
                b = tile * per_tile + sub
                mine = (row8 >= sub * ls) & (row8 < (sub + 1) * ls)
                s_old = sret_in[0, b, h]
                c_b = jnp.dot(q8, s_old, preferred_element_type=F32)
                cross8 = jnp.where(mine, c_b, cross8)
                upd = lax.dot_general(jnp.where(mine, kd8, 0.0), v8, (((0,), (0,)), ((), ())),
                                      preferred_element_type=F32)
                sret_ref[0, b, h] = cdec_ref[h] * s_old + upd
            ret_scr[rows, cols] = inner[rows, :] + cross8 * qdec_ref[h, rows, :]
        ret_scr[:, cols] = _ln(ret_scr[:, cols], gn_g[:, cols], gn_b[:, cols])

    u = proj(4) * _sigmoid(proj(5))
    nstate = CONV_WIDTH - 1
    xpad[...] = jnp.zeros(xpad.shape, F32)
    xpad[:, 0:nstate, :] = sconv_in[0]
    for b in range(BB_SAMPLE):
        xpad[b, XPAD_NEW:XPAD_NEW + ls, :] = u[b * ls:(b + 1) * ls, :]
    for i in range(ls):
        res = jnp.sum(xpad[...] * wsh_ref[i][None], axis=1) + conv_b[...]
        for sl in range(CONV_CH // LANES):
            cout_scr[sl, pl.ds(i, BB_SAMPLE, stride=ls), :] = res[:, sl * LANES:(sl + 1) * LANES]
    sconv_ref[0, :, 0:nstate - ls, :] = xpad[:, ls:nstate, :]
    sconv_ref[0, :, nstate - ls:nstate, :] = xpad[:, XPAD_NEW:XPAD_NEW + ls, :]
    c_out = jnp.concatenate([cout_scr[sl] for sl in range(CONV_CH // LANES)], axis=1)

    g = proj(3)
    gt_a = proj(6)
    gt_b = proj(7)
    x1, w1, w2, e1, e2 = _post_mix(
        x, ret_scr[...], g, c_out, gt_a, gt_b,
        (w_ret_o, cln_g, cln_b, w_conv_o, w_out, ln1_g, ln1_b, wr_hi, wr_lo, b_r))
    x1_ref[...] = x1
    rw_ref[...] = _lane_tile((w1, w2, e1, e2), t)


def _ffn_kernel(te_ref, nreal_ref, chunk_ref, xs_hbm, w_gu, w_dn, ys_hbm,
                xbuf, obuf, wgu_b, wdn_b, sem_in, sem_out):
    del xs_hbm
    i = pl.program_id(0)
    n = pl.num_programs(0)
    slot = i % 2

    def chunk_copy_in(tile, c, s):
        row0 = pl.multiple_of(chunk_ref[tile * TILE_CHUNKS + c] * CHUNK, CHUNK)
        return pltpu.make_async_copy(ys_hbm.at[pl.ds(row0, CHUNK)],
                                     xbuf.at[s, pl.ds(pl.multiple_of(c * CHUNK, CHUNK), CHUNK)],
                                     sem_in.at[s])

    def chunk_copy_out(tile, c, s):
        row0 = pl.multiple_of(chunk_ref[tile * TILE_CHUNKS + c] * CHUNK, CHUNK)
        return pltpu.make_async_copy(obuf.at[s, pl.ds(pl.multiple_of(c * CHUNK, CHUNK), CHUNK)],
                                     ys_hbm.at[pl.ds(row0, CHUNK)],
                                     sem_out.at[s])

    def for_chunks(tile, fn):
        def body(c, carry):
            fn(tile, c)
            return carry
        lax.fori_loop(0, nreal_ref[tile], body, 0)

    @pl.when(i == 0)
    def _first():
        xbuf[...] = jnp.zeros(xbuf.shape, U32)
        for_chunks(0, lambda tile, c: chunk_copy_in(tile, c, 0).start())

    @pl.when(i + 1 < n)
    def _prefetch():
        for_chunks(i + 1, lambda tile, c: chunk_copy_in(tile, c, 1 - slot).start())

    @pl.when(i >= 2)
    def _retire():
        for_chunks(i - 2, lambda tile, c: chunk_copy_out(tile, c, slot).wait())

    for_chunks(i, lambda tile, c: chunk_copy_in(tile, c, slot).wait())

    @pl.when(nreal_ref[i] > 0)
    def _tile():
        prev = te_ref[jnp.maximum(i - 1, 0)]

        @pl.when((i == 0) | (te_ref[i] != prev))
        def _new_expert():
            wgu_b[...] = w_gu[0].astype(BF16)
            wdn_b[...] = w_dn[0].astype(BF16)

        lo, hi = _unpack_bf16_pairs(xbuf[slot])
        hcat = (jnp.dot(lo, wgu_b[0:HALF, :], preferred_element_type=F32)
                + jnp.dot(hi, wgu_b[HALF:, :], preferred_element_type=F32))
        act = _silu(hcat[:, :EXP_FF]) * hcat[:, EXP_FF:]
        obuf[slot] = _pack_bf16_pairs(_bdot(act, wdn_b[...]))
        for_chunks(i, lambda tile, c: chunk_copy_out(tile, c, slot).start())

    @pl.when(i == n - 1)
    def _drain():
        @pl.when(i >= 1)
        def _():
            for_chunks(i - 1, lambda tile, c: chunk_copy_out(tile, c, 1 - slot).wait())
        for_chunks(i, lambda tile, c: chunk_copy_out(tile, c, slot).wait())


def _final_kernel(ys_ref, x1_ref, rw_ref, pp_ref, ps_ref, ln2_g, ln2_b, w_pg, b_pg, w_ple,
                  yp_ref, ys_out_ref, *, n_prompt_tiles):
    i = pl.program_id(0)
    tl = x1_ref.shape[0]
    x1 = x1_ref[...]
    rw = rw_ref[...]
    w1, w2, pos1, pos2 = rw[:, 0:1], rw[:, 1:2], rw[:, 2:3], rw[:, 3:4]
    slot = lax.broadcasted_iota(I32, (tl, CAP), 1).astype(F32)
    sel1 = (slot == pos1).astype(BF16)
    sel2 = (slot == pos2).astype(BF16)
    lo, hi = _unpack_bf16_pairs(ys_ref[...])
    pick = lambda sel: jnp.concatenate([jnp.dot(sel, lo, preferred_element_type=F32),
                                        jnp.dot(sel, hi, preferred_element_type=F32)], axis=1)
    moe = pick(sel1) * w1 + pick(sel2) * w2
    x2 = _ln(ALPHA * x1 + moe, ln2_g[...], ln2_b[...])
    gate = _sigmoid(_bdot(x2, w_pg[...]) + b_pg[...])

    @pl.when(i < n_prompt_tiles)
    def _prompt():
        yp_ref[...] = x2 + gate * _bdot(pp_ref[...], w_ple[...])

    @pl.when(i >= n_prompt_tiles)
    def _sample():
        ys_out_ref[...] = x2 + gate * _bdot(ps_ref[...], w_ple[...])


def _rope_tables(pos):
    half = RET_DK // 2
    inv_freq = ROPE_BASE ** (-jnp.arange(half, dtype=F32) / half)
    ang = pos[:, None] * inv_freq[None, :]
    cos = jnp.cos(ang)
    sin = jnp.sin(ang)
    return jnp.concatenate([cos, cos], axis=-1), jnp.concatenate([-sin, sin], axis=-1)


def _log_gamma():
    return jnp.log(1.0 - 2.0 ** (-5.0 - jnp.arange(RET_HEADS, dtype=F32)))


def _const_spec(shape):
    nd = len(shape)
    return pl.BlockSpec(shape, lambda *_: (0,) * nd, pipeline_mode=pl.Buffered(1))


def _chunk_plan(meta, n_blocks, n_ffn_tiles):
    m = meta.reshape(n_blocks, SUBLANES, LANES)
    cnt = m[:, 0, :N_EXPERTS]
    off = m[:, 1, :N_EXPERTS]
    nch = (cnt + (CHUNK - 1)) // CHUNK
    cum = jnp.cumsum(nch, axis=0)
    total = cum[-1]
    tiles_e = (total + TILE_CHUNKS - 1) // TILE_CHUNKS
    tile_end = jnp.cumsum(tiles_e)
    tile_ids = jnp.arange(n_ffn_tiles, dtype=I32)
    te = jnp.minimum(jnp.sum((tile_end[None, :] <= tile_ids[:, None]).astype(I32), axis=1), N_EXPERTS - 1)
    tile_start = (tile_end - tiles_e)[te]
    k = (tile_ids - tile_start)[:, None] * TILE_CHUNKS + jnp.arange(TILE_CHUNKS, dtype=I32)[None, :]
    real = (k < total[te][:, None]) & (tile_ids < tile_end[-1])[:, None]
    cum_e = cum.T[te]
    blk = jnp.sum((cum_e[:, None, :] <= k[:, :, None]).astype(I32), axis=2)
    blk = jnp.minimum(blk, n_blocks - 1)
    excl = jnp.take_along_axis(cum_e - nch.T[te], blk, axis=1)
    off_e = jnp.take_along_axis(off.T[te], blk, axis=1)
    chunk = jnp.where(real, blk * BLOCK_CHUNKS + off_e + (k - excl), 0)
    nreal = jnp.sum(real.astype(I32), axis=1)
    return te.astype(I32), nreal.astype(I32), chunk.reshape(-1).astype(I32)


def kernel(x_prompt, x_sample, state_ret, state_conv, p_prompt, p_sample, w_in, b_in, ret_gn_g, ret_gn_b,
           w_ret_o, conv_w, conv_b, conv_ln_g, conv_ln_b, w_conv_o, w_out, ln1_g, ln1_b, w_grp, b_grp,
           w_exp, b_exp, w_gu, w_dn, ln2_g, ln2_b, w_pg, b_pg, w_ple):
    assert DEPTH == 1 and w_in.shape[0] == 1
    bp, lp, d = x_prompt.shape
    bs, ls, _ = x_sample.shape
    n_p, n_s = bp * lp, bs * ls
    n_tok = n_p + n_s
    assert lp % TL == 0 and n_s % TL == 0 and bs % BB_SAMPLE == 0 and SUBLANES % ls == 0
    n_blocks = n_tok // TL

    lg = _log_gamma()
    c = RET_CHUNK
    idx = jnp.arange(c, dtype=F32)
    rel = idx[:, None] - idx[None, :]
    causal = rel >= 0
    decay = jnp.where(causal[None], jnp.exp(jnp.where(causal, rel, 0.0)[None] * lg[:, None, None]), 0.0)
    q_decay = jnp.exp((idx[:, None] + 1.0) * lg[None, :])
    k_decay = jnp.exp((c - 1.0 - idx[:, None]) * lg[None, :])
    chunk_decay = jnp.exp(c * lg)
    qdec_p = jnp.broadcast_to(q_decay.T[:, :, None], (RET_HEADS, c, RET_DK))
    kdec_p = jnp.broadcast_to(k_decay.T[:, :, None], (RET_HEADS, c, RET_DK))
    cdec_p = jnp.broadcast_to(chunk_decay[:, None, None], (RET_HEADS, 1, RET_DV))
    cos_p, sin_p = _rope_tables(jnp.arange(lp, dtype=F32))

    ts = BB_SAMPLE * ls
    idx_s = jnp.arange(ls, dtype=F32)
    pdec_s = jnp.exp(idx_s[None, :] * lg[:, None])
    pdec_s = jnp.broadcast_to(pdec_s[:, :, None, None], (RET_HEADS, ls, 1, RET_DK))
    qd_s = jnp.exp((idx_s[:, None] + 1.0) * lg[None, :])
    kd_s = jnp.exp((ls - 1.0 - idx_s[:, None]) * lg[None, :])
    qdec_s = jnp.broadcast_to(jnp.tile(qd_s.T, (1, BB_SAMPLE))[:, :, None], (RET_HEADS, ts, RET_DK))
    kdec_s = jnp.broadcast_to(jnp.tile(kd_s.T, (1, BB_SAMPLE))[:, :, None], (RET_HEADS, ts, RET_DK))
    cdec_s = jnp.broadcast_to(jnp.exp(ls * lg)[:, None, None], (RET_HEADS, 1, RET_DV))
    pos_s = PAST_LEN + jnp.arange(ls, dtype=F32)
    cos_s, sin_s = _rope_tables(jnp.tile(pos_s, BB_SAMPLE))

    w_in_b = w_in[0].astype(BF16)
    w_ret_o_b = w_ret_o[0].astype(BF16)
    w_conv_o_b = w_conv_o[0].astype(BF16)
    w_out_b = w_out[0].astype(BF16)
    w_pg_b = w_pg[0].astype(BF16)
    w_ple_b = w_ple[0].astype(BF16)
    w_r = jnp.zeros((d, LANES), F32).at[:, :N_GROUPS].set(w_grp[0]).at[:, N_GROUPS:N_GROUPS + N_EXPERTS].set(w_exp[0])
    wr_hi = w_r.astype(BF16)
    wr_lo = (w_r - wr_hi.astype(F32)).astype(BF16)
    b_r = jnp.zeros((1, LANES), F32).at[0, :N_GROUPS].set(b_grp[0]).at[0, N_GROUPS:N_GROUPS + N_EXPERTS].set(b_exp[0])
    row = lambda a: a.reshape(1, -1)
    conv_w0 = conv_w[0]
    nstate = CONV_WIDTH - 1
    win_row = np.array([m if m < nstate else XPAD_NEW + (m - nstate) for m in range(nstate + ls)])
    wsh = jnp.stack([jnp.zeros((XPAD_ROWS, CONV_CH), F32).at[win_row[i:i + CONV_WIDTH]].set(conv_w0)
                     for i in range(ls)])

    shared_w = (w_in_b, row(b_in[0]), row(ret_gn_g[0]), row(ret_gn_b[0]), w_ret_o_b)
    tail_w = (row(conv_ln_g[0]), row(conv_ln_b[0]), w_conv_o_b, w_out_b, row(ln1_g[0]), row(ln1_b[0]),
              wr_hi, wr_lo, b_r)

    nbt = bs // BB_SAMPLE
    xs2 = x_sample.reshape(n_s, d)
    sample_in = ((xs2, cos_s, sin_s, pdec_s, qdec_s, kdec_s, cdec_s, wsh, state_ret, state_conv)
                 + shared_w + (row(conv_b[0]),) + tail_w)
    sample_specs = (
        [pl.BlockSpec((ts, d), lambda i: (i, 0))]
        + [_const_spec(a.shape) for a in sample_in[1:8]]
        + [pl.BlockSpec((1, BB_SAMPLE, RET_HEADS, RET_DK, RET_DV), lambda i: (0, i, 0, 0, 0)),
           pl.BlockSpec((1, BB_SAMPLE, nstate, CONV_CH), lambda i: (0, i, 0, 0))]
        + [_const_spec(a.shape) for a in sample_in[10:]]
    )
    tok_spec_s = lambda w: pl.BlockSpec((ts, w), lambda i: (i, 0))
    x1_s, rw_s, ret_s, conv_s = pl.pallas_call(
        _sample_mixer_kernel,
        grid=(nbt,),
        in_specs=sample_specs,
        out_specs=[
            tok_spec_s(d), tok_spec_s(LANES),
            pl.BlockSpec((1, BB_SAMPLE, RET_HEADS, RET_DK, RET_DV), lambda i: (0, i, 0, 0, 0)),
            pl.BlockSpec((1, BB_SAMPLE, nstate, CONV_CH), lambda i: (0, i, 0, 0)),
        ],
        out_shape=[
            jax.ShapeDtypeStruct((n_s, d), F32),
            jax.ShapeDtypeStruct((n_s, LANES), F32),
            jax.ShapeDtypeStruct(state_ret.shape, F32),
            jax.ShapeDtypeStruct(state_conv.shape, F32),
        ],
        scratch_shapes=[
            pltpu.VMEM((ts, RET_V), F32),
            pltpu.VMEM((CONV_CH // LANES, ts, LANES), F32),
            pltpu.VMEM((BB_SAMPLE, XPAD_ROWS, CONV_CH), F32),
        ],
        compiler_params=pltpu.CompilerParams(
            dimension_semantics=("arbitrary",), vmem_limit_bytes=VMEM_LIMIT),
        name="sample_mixer",
    )(*sample_in)

    nlt = lp // TL
    nst = n_s // TL
    assert nst <= nlt
    prompt_in = ((x_prompt, x1_s, rw_s, cos_p, sin_p, decay, qdec_p, kdec_p, cdec_p)
                 + shared_w + (conv_w0, row(conv_b[0])) + tail_w)
    seq_idx = lambda b: jnp.minimum(b, bp - 1)
    tail_idx = lambda l: jnp.minimum(l, nst - 1)
    tail_spec = lambda w: pl.BlockSpec((TL, w), lambda b, l: (tail_idx(l), 0))
    prompt_specs = [
        pl.BlockSpec((1, TL, d), lambda b, l: (seq_idx(b), l, 0)),
        tail_spec(d), tail_spec(LANES),
        pl.BlockSpec((TL, RET_DK), lambda b, l: (l, 0)),
        pl.BlockSpec((TL, RET_DK), lambda b, l: (l, 0)),
    ] + [_const_spec(a.shape) for a in prompt_in[5:]]
    tile_idx = lambda b, l: jnp.where(b < bp, b * nlt + l, bp * nlt + tail_idx(l))
    tok_spec_p = lambda rows, w: pl.BlockSpec((rows, w), lambda b, l: (tile_idx(b, l), 0))
    x1_all, rw_all, xs_all, meta, ret_p, conv_p = pl.pallas_call(
        functools.partial(_prompt_mixer_kernel, n_seq=bp, n_sample_tiles=nst),
        grid=(bp + 1, nlt),
        in_specs=prompt_specs,
        out_specs=[
            tok_spec_p(TL, d), tok_spec_p(TL, LANES), tok_spec_p(CAP, HALF), tok_spec_p(SUBLANES, LANES),
            pl.BlockSpec((1, 1, RET_HEADS, RET_DK, RET_DV), lambda b, l: (0, seq_idx(b), 0, 0, 0)),
            pl.BlockSpec((1, 1, nstate, CONV_CH), lambda b, l: (0, seq_idx(b), 0, 0)),
        ],
        out_shape=[
            jax.ShapeDtypeStruct((n_tok, d), F32),
            jax.ShapeDtypeStruct((n_tok, LANES), F32),
            jax.ShapeDtypeStruct((n_blocks * CAP, HALF), U32),
            jax.ShapeDtypeStruct((n_blocks * SUBLANES, LANES), I32),
            jax.ShapeDtypeStruct((1, bp, RET_HEADS, RET_DK, RET_DV), F32),
            jax.ShapeDtypeStruct((1, bp, nstate, CONV_CH), F32),
        ],
        scratch_shapes=[
            pltpu.VMEM((TL + CONV_PAD, CONV_CH), F32),
            pltpu.VMEM((TL, RET_V), F32),
            pltpu.VMEM((TL, CONV_CH), F32),
        ],
        compiler_params=pltpu.CompilerParams(
            dimension_semantics=("arbitrary", "arbitrary"), vmem_limit_bytes=VMEM_LIMIT),
        name="prompt_mixer",
    )(*prompt_in)

    max_chunks = n_blocks * (TOP_K * TL // CHUNK + N_EXPERTS - 1)
    n_ffn_tiles = (max_chunks + N_EXPERTS * (TILE_CHUNKS - 1)) // TILE_CHUNKS
    tile_e, tile_nreal, chunk_ids = _chunk_plan(meta, n_blocks, n_ffn_tiles)

    ys_all = pl.pallas_call(
        _ffn_kernel,
        grid_spec=pltpu.PrefetchScalarGridSpec(
            num_scalar_prefetch=3,
            grid=(n_ffn_tiles,),
            in_specs=[
                pl.BlockSpec(memory_space=pl.ANY),
                pl.BlockSpec((1, d, 2 * EXP_FF), lambda i, te, nr, ch: (te[i], 0, 0)),
                pl.BlockSpec((1, EXP_FF, d), lambda i, te, nr, ch: (te[i], 0, 0)),
            ],
            out_specs=pl.BlockSpec(memory_space=pl.ANY),
            scratch_shapes=[
                pltpu.VMEM((2, TM_FFN, HALF), U32),
                pltpu.VMEM((2, TM_FFN, HALF), U32),
                pltpu.VMEM((d, 2 * EXP_FF), BF16),
                pltpu.VMEM((EXP_FF, d), BF16),
                pltpu.SemaphoreType.DMA((2,)),
                pltpu.SemaphoreType.DMA((2,)),
            ],
        ),
        out_shape=jax.ShapeDtypeStruct(xs_all.shape, U32),
        input_output_aliases={3: 0},
        compiler_params=pltpu.CompilerParams(
            dimension_semantics=("arbitrary",), vmem_limit_bytes=VMEM_LIMIT),
        name="expert_ffn",
    )(tile_e, tile_nreal, chunk_ids, xs_all, w_gu[0], w_dn[0])

    npt = n_p // TL
    pp2 = p_prompt.reshape(n_p, PLE_DIM)
    ps2 = p_sample.reshape(n_s, PLE_DIM)
    tok_f = lambda rows, w: pl.BlockSpec((rows, w), lambda i: (i, 0))
    y_p, y_s = pl.pallas_call(
        functools.partial(_final_kernel, n_prompt_tiles=npt),
        grid=(n_blocks,),
        in_specs=[
            tok_f(CAP, HALF), tok_f(TL, d), tok_f(TL, LANES),
            pl.BlockSpec((TL, PLE_DIM), lambda i: (jnp.minimum(i, npt - 1), 0)),
            pl.BlockSpec((TL, PLE_DIM), lambda i: (jnp.maximum(i - npt, 0), 0)),
            _const_spec((1, d)), _const_spec((1, d)), _const_spec((d, d)), _const_spec((1, d)),
            _const_spec((PLE_DIM, d)),
        ],
        out_specs=[
            pl.BlockSpec((TL, d), lambda i: (jnp.minimum(i, npt - 1), 0)),
            pl.BlockSpec((TL, d), lambda i: (jnp.maximum(i - npt, 0), 0)),
        ],
        out_shape=[jax.ShapeDtypeStruct((n_p, d), F32), jax.ShapeDtypeStruct((n_s, d), F32)],
        compiler_params=pltpu.CompilerParams(
            dimension_semantics=("arbitrary",), vmem_limit_bytes=VMEM_LIMIT),
        name="moe_combine_final",
    )(ys_all, x1_all, rw_all, pp2, ps2,
      row(ln2_g[0]), row(ln2_b[0]), w_pg_b, row(b_pg[0]), w_ple_b)

    return (y_p.reshape(bp, lp, d), y_s.reshape(bs, ls, d), ret_p, conv_p, ret_s, conv_s)
```

```python
import functools

import jax
import jax.numpy as jnp
import numpy as np
from jax import lax
from jax.experimental import pallas as pl
from jax.experimental.pallas import tpu as pltpu

F32 = jnp.float32
BF16 = jnp.bfloat16
I32 = jnp.int32
U32 = jnp.uint32

D_MODEL = 1024
PAST_LEN = 16384
RET_HEADS = 4
RET_DK = 128
RET_DV = 128
RET_QK = RET_HEADS * RET_DK
RET_V = RET_HEADS * RET_DV
RET_CHUNK = 128
ROPE_BASE = 10000.0
CONV_CH = 512
CONV_WIDTH = 31
N_GROUPS = 4
EXP_PER_GROUP = 4
N_EXPERTS = N_GROUPS * EXP_PER_GROUP
TOP_K = 2
EXP_FF = 512
PLE_DIM = 256
DEPTH = 1
ALPHA = (2 * DEPTH) ** 0.25
LN_EPS = 1e-5
IN_WIDTHS = (RET_QK, RET_QK, RET_V, RET_V, CONV_CH, CONV_CH, D_MODEL, D_MODEL)
IN_OFFS = tuple(int(s) for s in np.cumsum((0,) + IN_WIDTHS))

LANES = 128
SUBLANES = 8
VMEM_LIMIT = 56 * 1024 * 1024

TL = 256
BB_SAMPLE = 16
CHUNK = SUBLANES
TILE_CHUNKS = 32
BLOCK_USED = -(-(TOP_K * TL + N_EXPERTS * (CHUNK - 1)) // (CHUNK * 16)) * 16
BLOCK_SPARE = 16
BLOCK_CHUNKS = BLOCK_USED + BLOCK_SPARE
CAP = BLOCK_CHUNKS * CHUNK
TM_FFN = TILE_CHUNKS * CHUNK
HALF = D_MODEL // 2
CONV_PAD = 32
XPAD_NEW = 32
XPAD_ROWS = 40


def _ln(x, g, b):
    mu = jnp.mean(x, axis=-1, keepdims=True)
    d = x - mu
    var = jnp.mean(d * d, axis=-1, keepdims=True)
    return d * lax.rsqrt(var + LN_EPS) * g + b


def _sigmoid(x):
    return 1.0 / (1.0 + jnp.exp(-x))


def _silu(x):
    return x * _sigmoid(x)


def _bdot(a, b):
    return jnp.dot(a.astype(BF16), b, preferred_element_type=F32)


def _rot(t, cosf, sinf):
    return t * cosf + pltpu.roll(t, RET_DK // 2, axis=1) * sinf


def _pack_bf16_pairs(x):
    bits = lax.bitcast_convert_type(x.astype(BF16).astype(F32), U32)
    half = x.shape[1] // 2
    return (bits[:, :half] >> 16) | (bits[:, half:] & jnp.uint32(0xFFFF0000))


def _unpack_bf16_pairs(pk):
    lo = lax.bitcast_convert_type(pk << 16, F32).astype(BF16)
    hi = lax.bitcast_convert_type(pk & jnp.uint32(0xFFFF0000), F32).astype(BF16)
    return lo, hi


def _lane_tile(cols, rows):
    lane = lax.broadcasted_iota(I32, (rows, LANES), 1)
    out = jnp.zeros((rows, LANES), F32)
    for i, col in enumerate(cols):
        out = jnp.where(lane == i, col, out)
    return out


def _route(logits):
    lane = lax.broadcasted_iota(I32, logits.shape, 1)
    lanef = lane.astype(F32)
    ninf = jnp.float32(-jnp.inf)
    big = jnp.float32(LANES)
    gmask = lane < N_GROUPS
    gl = jnp.where(gmask, logits, ninf)
    gmax = jnp.max(gl, axis=1, keepdims=True)
    gidx = jnp.min(jnp.where(gmask & (gl == gmax), lanef, big), axis=1, keepdims=True)
    sumexp = jnp.sum(jnp.where(gmask, jnp.exp(gl - gmax), 0.0), axis=1, keepdims=True)
    gw = 1.0 / sumexp
    lo = N_GROUPS + EXP_PER_GROUP * gidx
    emask = (lanef >= lo) & (lanef < lo + EXP_PER_GROUP)
    el = jnp.where(emask, logits, ninf)
    m1 = jnp.max(el, axis=1, keepdims=True)
    i1 = jnp.min(jnp.where(emask & (el == m1), lanef, big), axis=1, keepdims=True)
    emask2 = emask & (lanef != i1)
    el2 = jnp.where(emask2, logits, ninf)
    m2 = jnp.max(el2, axis=1, keepdims=True)
    i2 = jnp.min(jnp.where(emask2 & (el2 == m2), lanef, big), axis=1, keepdims=True)
    t = jnp.exp(m2 - m1)
    den = 1.0 + t
    return (1.0 / den) * gw, (t / den) * gw, i1 - N_GROUPS, i2 - N_GROUPS


def _post_mix(x, ret_n, g, c_out, gt_a, gt_b, w):
    (w_ret_o, cln_g, cln_b, w_conv_o, w_out, ln1_g, ln1_b, wr_hi, wr_lo, b_r) = w
    branch_a = _bdot(_silu(g) * ret_n, w_ret_o[...])
    branch_b = _bdot(_silu(_ln(c_out, cln_g[...], cln_b[...])), w_conv_o[...])
    mix = _sigmoid(gt_a) * branch_a + _sigmoid(gt_b) * branch_b
    h = ALPHA * x + _bdot(mix, w_out[...])
    x1 = _ln(h, ln1_g[...], ln1_b[...])
    x1_hi = x1.astype(BF16)
    x1_lo = (x1 - x1_hi.astype(F32)).astype(BF16)
    logits = (jnp.dot(x1_hi, wr_hi[...], preferred_element_type=F32)
              + (jnp.dot(x1_lo, wr_hi[...], preferred_element_type=F32)
                 + jnp.dot(x1_hi, wr_lo[...], preferred_element_type=F32))
              + b_r[...])
    return (x1,) + _route(logits)


def _sort_tile(x1, w1, w2, e1, e2, x1_ref, rw_ref, xs_ref, meta_ref):
    t = x1.shape[0]
    lane = lax.broadcasted_iota(I32, (t, LANES), 1).astype(F32)
    a1 = (lane == e1).astype(F32)
    a2 = (lane == e2).astype(F32)
    ri = lax.broadcasted_iota(I32, (t, t), 0)
    ci = lax.broadcasted_iota(I32, (t, t), 1)
    earlier = (ci < ri).astype(BF16)
    r1 = jnp.dot(earlier, a1.astype(BF16), preferred_element_type=F32)
    r2 = jnp.dot(earlier, a2.astype(BF16), preferred_element_type=F32)
    cnt1 = jnp.sum(a1, axis=0, keepdims=True)
    cnt = cnt1 + jnp.sum(a2, axis=0, keepdims=True)
    nch = jnp.floor((cnt + (CHUNK - 1.0)) * (1.0 / CHUNK))
    ui = lax.broadcasted_iota(I32, (LANES, LANES), 0)
    uj = lax.broadcasted_iota(I32, (LANES, LANES), 1)
    before = (ui < uj).astype(BF16)
    off = jnp.dot(jnp.broadcast_to(nch, (2 * SUBLANES, LANES)).astype(BF16), before,
                  preferred_element_type=F32)[0:1, :]
    base = off * CHUNK
    pos1 = jnp.sum(a1 * (base + r1), axis=1, keepdims=True)
    pos2 = jnp.sum(a2 * (base + cnt1 + r2), axis=1, keepdims=True)
    slot = lax.broadcasted_iota(I32, (t, CAP), 1).astype(F32)
    onehot_t = ((slot == pos1) | (slot == pos2)).astype(BF16)
    xs = lax.dot_general(onehot_t, x1.astype(BF16), (((0,), (0,)), ((), ())), preferred_element_type=F32)
    x1_ref[...] = x1
    rw_ref[...] = _lane_tile((w1, w2, pos1, pos2), t)
    xs_ref[...] = _pack_bf16_pairs(xs)
    srow = lax.broadcasted_iota(I32, (SUBLANES, LANES), 0)
    meta = jnp.where(srow == 0, cnt, jnp.where(srow == 1, off, 0.0))
    meta_ref[...] = meta.astype(I32)


def _prompt_mixer_kernel(x_ref, x1s_ref, rws_ref, *refs, n_seq, n_sample_tiles):
    bi = pl.program_id(0)
    li = pl.program_id(1)
    outs = refs[-10:-6]

    @pl.when(bi < n_seq)
    def _mix():
        _prompt_mixer_body(x_ref, *refs)

    @pl.when((bi == n_seq) & (li < n_sample_tiles))
    def _append():
        rws = rws_ref[...]
        _sort_tile(x1s_ref[...], rws[:, 0:1], rws[:, 1:2], rws[:, 2:3], rws[:, 3:4], *outs)


def _prompt_mixer_body(x_ref, cos_ref, sin_ref, dec_ref, qdec_ref, kdec_ref, cdec_ref,
                       w_in, b_in, gn_g, gn_b, w_ret_o, conv_w, conv_b, cln_g, cln_b,
                       w_conv_o, w_out, ln1_g, ln1_b, wr_hi, wr_lo, b_r,
                       x1_ref, rw_ref, xs_ref, meta_ref, sret_ref, sconv_ref,
                       ubuf, ushift, ret_scr, cout_scr):
    tl = x_ref.shape[1]
    li = pl.program_id(1)
    nl = pl.num_programs(1)

    @pl.when(li == 0)
    def _init():
        sret_ref[...] = jnp.zeros(sret_ref.shape, F32)
        ubuf[0:CONV_PAD, :] = jnp.zeros((CONV_PAD, CONV_CH), F32)

    x = x_ref[0]
    xb = x.astype(BF16)

    def proj(k):
        c0, c1 = IN_OFFS[k], IN_OFFS[k + 1]
        return jnp.dot(xb, w_in[:, c0:c1], preferred_element_type=F32) + b_in[:, c0:c1]

    q = proj(0)
    k = proj(1)
    v = proj(2)
    scale = RET_DK ** -0.5
    for c in range(tl // RET_CHUNK):
        rows = slice(c * RET_CHUNK, (c + 1) * RET_CHUNK)
        cosf = cos_ref[rows, :]
        sinf = sin_ref[rows, :]
        for h in range(RET_HEADS):
            cols = slice(h * RET_DK, (h + 1) * RET_DK)
            qh = _rot(q[rows, cols], cosf, sinf)
            kh = _rot(k[rows, cols], cosf, sinf) * scale
            qb = qh.astype(BF16)
            kb = kh.astype(BF16)
            vb = v[rows, cols].astype(BF16)
            s_old = sret_ref[0, 0, h]
            scores = lax.dot_general(qb, kb, (((1,), (1,)), ((), ())),
                                     preferred_element_type=F32) * dec_ref[h]
            inner = jnp.dot(scores.astype(BF16), vb, preferred_element_type=F32)
            cross = jnp.dot(qb, s_old.astype(BF16), preferred_element_type=F32) * qdec_ref[h]
            kd = (kh * kdec_ref[h]).astype(BF16)
            s_new = cdec_ref[h] * s_old + lax.dot_general(
                kd, vb, (((0,), (0,)), ((), ())), preferred_element_type=F32)
            sret_ref[0, 0, h] = s_new
            ret_scr[rows, cols] = _ln(inner + cross, gn_g[:, cols], gn_b[:, cols])

    u = proj(4) * _sigmoid(proj(5))
    ubuf[CONV_PAD:CONV_PAD + tl, :] = u
    nsh = ushift.shape[1]
    for s in range(1, SUBLANES):
        ushift[s - 1] = ubuf[s:s + nsh, :]
    rb = 32
    for r in range(tl // rb):
        acc = jnp.zeros((rb, CONV_CH), F32) + conv_b[...]
        for j in range(CONV_WIDTH):
            off = j + (CONV_PAD - (CONV_WIDTH - 1))
            s = off % SUBLANES
            base = r * rb + off - s
            win = ubuf[base:base + rb, :] if s == 0 else ushift[s - 1, base:base + rb, :]
            acc = acc + conv_w[j:j + 1, :] * win
        cout_scr[r * rb:(r + 1) * rb, :] = acc
    ubuf[0:CONV_PAD, :] = ubuf[tl:tl + CONV_PAD, :]

    g = proj(3)
    gt_a = proj(6)
    gt_b = proj(7)
    x1, w1, w2, e1, e2 = _post_mix(
        x, ret_scr[...], g, cout_scr[...], gt_a, gt_b,
        (w_ret_o, cln_g, cln_b, w_conv_o, w_out, ln1_g, ln1_b, wr_hi, wr_lo, b_r))
    _sort_tile(x1, w1, w2, e1, e2, x1_ref, rw_ref, xs_ref, meta_ref)

    @pl.when(li == nl - 1)
    def _conv_state():
        sconv_ref[0, 0] = ubuf[CONV_PAD - (CONV_WIDTH - 1):CONV_PAD, :]


def _sample_mixer_kernel(x_ref, cos_ref, sin_ref, pdec_ref, qdec_ref, kdec_ref, cdec_ref, wsh_ref,
                         sret_in, sconv_in,
                         w_in, b_in, gn_g, gn_b, w_ret_o, conv_b, cln_g, cln_b,
                         w_conv_o, w_out, ln1_g, ln1_b, wr_hi, wr_lo, b_r,
                         x1_ref, rw_ref, sret_ref, sconv_ref,
                         ret_scr, cout_scr, xpad):
    t = x_ref.shape[0]
    ls = t // BB_SAMPLE
    x = x_ref[...]
    xb = x.astype(BF16)

    def proj(k):
        c0, c1 = IN_OFFS[k], IN_OFFS[k + 1]
        return jnp.dot(xb, w_in[:, c0:c1], preferred_element_type=F32) + b_in[:, c0:c1]

    q = proj(0)
    k = proj(1)
    v = proj(2)
    scale = RET_DK ** -0.5
    cosf = cos_ref[...]
    sinf = sin_ref[...]
    row = lax.broadcasted_iota(I32, (t, RET_DK), 0)
    pos = row % ls
    row8 = lax.broadcasted_iota(I32, (SUBLANES, RET_DK), 0)
    per_tile = SUBLANES // ls
    for h in range(RET_HEADS):
        cols = slice(h * RET_DK, (h + 1) * RET_DK)
        qh = _rot(q[:, cols], cosf, sinf)
        kh = _rot(k[:, cols], cosf, sinf) * scale
        vh = v[:, cols]
        inner = jnp.zeros((t, RET_DV), F32)
        for s in range(ls):
            ks = kh if s == 0 else pltpu.roll(kh, s, axis=0)
            vs = vh if s == 0 else pltpu.roll(vh, s, axis=0)
            dotp = jnp.sum(qh * ks, axis=1, keepdims=True) * pdec_ref[h, s]
            inner = inner + jnp.where(pos >= s, dotp, 0.0) * vs
        kd = kh * kdec_ref[h]
        for tile in range(t // SUBLANES):
            rows = slice(tile * SUBLANES, (tile + 1) * SUBLANES)
            q8 = qh[rows, :]
            kd8 = kd[rows, :]
            v8 = vh[rows, :]
            cross8 = jnp.zeros((SUBLANES, RET_DV), F32)
            for sub in range(per_tile):
                b = tile * per_tile + sub
                mine = (row8 >= sub * ls) & (row8 < (sub + 1) * ls)
                s_old = sret_in[0, b, h]
                c_b = jnp.dot(q8, s_old, preferred_element_type=F32)
                cross8 = jnp.where(mine, c_b, cross8)
                upd = lax.dot_general(jnp.where(mine, kd8, 0.0), v8, (((0,), (0,)), ((), ())),
                                      preferred_element_type=F32)
                sret_ref[0, b, h] = cdec_ref[h] * s_old + upd
            ret_scr[rows, cols] = inner[rows, :] + cross8 * qdec_ref[h, rows, :]
        ret_scr[:, cols] = _ln(ret_scr[:, cols], gn_g[:, cols], gn_b[:, cols])

    u = proj(4) * _sigmoid(proj(5))
    nstate = CONV_WIDTH - 1
    xpad[...] = jnp.zeros(xpad.shape, F32)
    xpad[:, 0:nstate, :] = sconv_in[0]
    for b in range(BB_SAMPLE):
        xpad[b, XPAD_NEW:XPAD_NEW + ls, :] = u[b * ls:(b + 1) * ls, :]
    for i in range(ls):
        res = jnp.sum(xpad[...] * wsh_ref[i][None], axis=1) + conv_b[...]
        for sl in range(CONV_CH // LANES):
            cout_scr[sl, pl.ds(i, BB_SAMPLE, stride=ls), :] = res[:, sl * LANES:(sl + 1) * LANES]
    sconv_ref[0, :, 0:nstate - ls, :] = xpad[:, ls:nstate, :]
    sconv_ref[0, :, nstate - ls:nstate, :] = xpad[:, XPAD_NEW:XPAD_NEW + ls, :]
    c_out = jnp.concatenate([cout_scr[sl] for sl in range(CONV_CH // LANES)], axis=1)

    g = proj(3)
    gt_a = proj(6)
    gt_b = proj(7)
    x1, w1, w2, e1, e2 = _post_mix(
        x, ret_scr[...], g, c_out, gt_a, gt_b,
        (w_ret_o, cln_g, cln_b, w_conv_o, w_out, ln1_g, ln1_b, wr_hi, wr_lo, b_r))
    x1_ref[...] = x1
    rw_ref[...] = _lane_tile((w1, w2, e1, e2), t)


def _ffn_kernel(te_ref, nvalid_ref, chunk_ref, xs_hbm, w_gu, w_dn, ys_hbm,
                xbuf, obuf, wgu_b, wdn_b, sem_in, sem_out):
    del xs_hbm
    i = pl.program_id(0)
    n = pl.num_programs(0)
    slot = i % 2
    nvalid = nvalid_ref[0]

    def chunk_rows(tile, c):
        return pl.ds(pl.multiple_of(chunk_ref[tile * TILE_CHUNKS + c] * CHUNK, CHUNK), CHUNK)

    def start_in(tile, s):
        for c in range(TILE_CHUNKS):
            pltpu.make_async_copy(ys_hbm.at[chunk_rows(tile, c)],
                                  xbuf.at[s, pl.ds(c * CHUNK, CHUNK)], sem_in.at[s]).start()

    def start_out(tile, s):
        for c in range(TILE_CHUNKS):
            pltpu.make_async_copy(obuf.at[s, pl.ds(c * CHUNK, CHUNK)],
                                  ys_hbm.at[chunk_rows(tile, c)], sem_out.at[s]).start()

    def wait_in(s):
        pltpu.make_async_copy(ys_hbm.at[pl.ds(0, TM_FFN)], xbuf.at[s], sem_in.at[s]).wait()

    def wait_out(s):
        pltpu.make_async_copy(obuf.at[s], ys_hbm.at[pl.ds(0, TM_FFN)], sem_out.at[s]).wait()

    @pl.when((i == 0) & (nvalid > 0))
    def _first():
        start_in(0, 0)

    @pl.when(i + 1 < nvalid)
    def _prefetch():
        start_in(i + 1, 1 - slot)

    @pl.when((i >= 2) & (i - 2 < nvalid))
    def _retire():
        wait_out(slot)

    @pl.when(i < nvalid)
    def _tile():
        wait_in(slot)
        prev = te_ref[jnp.maximum(i - 1, 0)]

        @pl.when((i == 0) | (te_ref[i] != prev))
        def _new_expert():
            wgu_b[...] = w_gu[0].astype(BF16)
            wdn_b[...] = w_dn[0].astype(BF16)

        lo, hi = _unpack_bf16_pairs(xbuf[slot])
        hcat = (jnp.dot(lo, wgu_b[0:HALF, :], preferred_element_type=F32)
                + jnp.dot(hi, wgu_b[HALF:, :], preferred_element_type=F32))
        act = _silu(hcat[:, :EXP_FF]) * hcat[:, EXP_FF:]
        obuf[slot] = _pack_bf16_pairs(_bdot(act, wdn_b[...]))
        start_out(i, slot)

    @pl.when(i == n - 1)
    def _drain():
        @pl.when((i >= 1) & (i - 1 < nvalid))
        def _():
            wait_out(1 - slot)

        @pl.when(i < nvalid)
        def _():
            wait_out(slot)


def _final_kernel(ys_ref, x1_ref, rw_ref, pp_ref, ps_ref, ln2_g, ln2_b, w_pg, b_pg, w_ple,
                  yp_ref, ys_out_ref, *, n_prompt_tiles):
    i = pl.program_id(0)
    tl = x1_ref.shape[0]
    x1 = x1_ref[...]
    rw = rw_ref[...]
    w1, w2, pos1, pos2 = rw[:, 0:1], rw[:, 1:2], rw[:, 2:3], rw[:, 3:4]
    slot = lax.broadcasted_iota(I32, (tl, CAP), 1).astype(F32)
    sel1 = (slot == pos1).astype(BF16)
    sel2 = (slot == pos2).astype(BF16)
    lo, hi = _unpack_bf16_pairs(ys_ref[...])
    pick = lambda sel: jnp.concatenate([jnp.dot(sel, lo, preferred_element_type=F32),
                                        jnp.dot(sel, hi, preferred_element_type=F32)], axis=1)
    moe = pick(sel1) * w1 + pick(sel2) * w2
    x2 = _ln(ALPHA * x1 + moe, ln2_g[...], ln2_b[...])
    gate = _sigmoid(_bdot(x2, w_pg[...]) + b_pg[...])

    @pl.when(i < n_prompt_tiles)
    def _prompt():
        yp_ref[...] = x2 + gate * _bdot(pp_ref[...], w_ple[...])

    @pl.when(i >= n_prompt_tiles)
    def _sample():
        ys_out_ref[...] = x2 + gate * _bdot(ps_ref[...], w_ple[...])


def _rope_tables(pos):
    half = RET_DK // 2
    inv_freq = ROPE_BASE ** (-jnp.arange(half, dtype=F32) / half)
    ang = pos[:, None] * inv_freq[None, :]
    cos = jnp.cos(ang)
    sin = jnp.sin(ang)
    return jnp.concatenate([cos, cos], axis=-1), jnp.concatenate([-sin, sin], axis=-1)


def _log_gamma():
    return jnp.log(1.0 - 2.0 ** (-5.0 - jnp.arange(RET_HEADS, dtype=F32)))


def _const_spec(shape):
    nd = len(shape)
    return pl.BlockSpec(shape, lambda *_: (0,) * nd, pipeline_mode=pl.Buffered(1))


def _chunk_plan(meta, n_blocks, n_ffn_tiles):
    assert n_blocks * BLOCK_SPARE >= N_EXPERTS * (TILE_CHUNKS - 1)
    m = meta.reshape(n_blocks, SUBLANES, LANES)
    cnt = m[:, 0, :N_EXPERTS]
    off = m[:, 1, :N_EXPERTS]
    nch = (cnt + (CHUNK - 1)) // CHUNK
    cum = jnp.cumsum(nch, axis=0)
    total = cum[-1]
    tiles_e = (total + TILE_CHUNKS - 1) // TILE_CHUNKS
    tile_end = jnp.cumsum(tiles_e)
    tile_ids = jnp.arange(n_ffn_tiles, dtype=I32)
    te = jnp.minimum(jnp.sum((tile_end[None, :] <= tile_ids[:, None]).astype(I32), axis=1), N_EXPERTS - 1)
    tile_start = (tile_end - tiles_e)[te]
    k = (tile_ids - tile_start)[:, None] * TILE_CHUNKS + jnp.arange(TILE_CHUNKS, dtype=I32)[None, :]
    real = k < total[te][:, None]
    cum_e = cum.T[te]
    blk = jnp.sum((cum_e[:, None, :] <= k[:, :, None]).astype(I32), axis=2)
    blk = jnp.minimum(blk, n_blocks - 1)
    excl = jnp.take_along_axis(cum_e - nch.T[te], blk, axis=1)
    off_e = jnp.take_along_axis(off.T[te], blk, axis=1)
    spare = te[:, None] * (TILE_CHUNKS - 1) + jnp.maximum(k - total[te][:, None], 0) % TILE_CHUNKS
    spare_chunk = (spare // BLOCK_SPARE) * BLOCK_CHUNKS + BLOCK_USED + spare % BLOCK_SPARE
    chunk = jnp.where(real, blk * BLOCK_CHUNKS + off_e + (k - excl), spare_chunk)
    return te.astype(I32), tile_end[-1:].astype(I32), chunk.reshape(-1).astype(I32)


def kernel(x_prompt, x_sample, state_ret, state_conv, p_prompt, p_sample, w_in, b_in, ret_gn_g, ret_gn_b,
           w_ret_o, conv_w, conv_b, conv_ln_g, conv_ln_b, w_conv_o, w_out, ln1_g, ln1_b, w_grp, b_grp,
           w_exp, b_exp, w_gu, w_dn, ln2_g, ln2_b, w_pg, b_pg, w_ple):
    assert DEPTH == 1 and w_in.shape[0] == 1
    bp, lp, d = x_prompt.shape
    bs, ls, _ = x_sample.shape
    n_p, n_s = bp * lp, bs * ls
    n_tok = n_p + n_s
    assert lp % TL == 0 and n_s % TL == 0 and bs % BB_SAMPLE == 0 and SUBLANES % ls == 0
    n_blocks = n_tok // TL

    lg = _log_gamma()
    c = RET_CHUNK
    idx = jnp.arange(c, dtype=F32)
    rel = idx[:, None] - idx[None, :]
    causal = rel >= 0
    decay = jnp.where(causal[None], jnp.exp(jnp.where(causal, rel, 0.0)[None] * lg[:, None, None]), 0.0)
    q_decay = jnp.exp((idx[:, None] + 1.0) * lg[None, :])
    k_decay = jnp.exp((c - 1.0 - idx[:, None]) * lg[None, :])
    chunk_decay = jnp.exp(c * lg)
    qdec_p = jnp.broadcast_to(q_decay.T[:, :, None], (RET_HEADS, c, RET_DK))
    kdec_p = jnp.broadcast_to(k_decay.T[:, :, None], (RET_HEADS, c, RET_DK))
    cdec_p = jnp.broadcast_to(chunk_decay[:, None, None], (RET_HEADS, 1, RET_DV))
    cos_p, sin_p = _rope_tables(jnp.arange(lp, dtype=F32))

    ts = BB_SAMPLE * ls
    idx_s = jnp.arange(ls, dtype=F32)
    pdec_s = jnp.exp(idx_s[None, :] * lg[:, None])
    pdec_s = jnp.broadcast_to(pdec_s[:, :, None, None], (RET_HEADS, ls, 1, RET_DK))
    qd_s = jnp.exp((idx_s[:, None] + 1.0) * lg[None, :])
    kd_s = jnp.exp((ls - 1.0 - idx_s[:, None]) * lg[None, :])
    qdec_s = jnp.broadcast_to(jnp.tile(qd_s.T, (1, BB_SAMPLE))[:, :, None], (RET_HEADS, ts, RET_DK))
    kdec_s = jnp.broadcast_to(jnp.tile(kd_s.T, (1, BB_SAMPLE))[:, :, None], (RET_HEADS, ts, RET_DK))
    cdec_s = jnp.broadcast_to(jnp.exp(ls * lg)[:, None, None], (RET_HEADS, 1, RET_DV))
    pos_s = PAST_LEN + jnp.arange(ls, dtype=F32)
    cos_s, sin_s = _rope_tables(jnp.tile(pos_s, BB_SAMPLE))

    w_in_b = w_in[0].astype(BF16)
    w_ret_o_b = w_ret_o[0].astype(BF16)
    w_conv_o_b = w_conv_o[0].astype(BF16)
    w_out_b = w_out[0].astype(BF16)
    w_pg_b = w_pg[0].astype(BF16)
    w_ple_b = w_ple[0].astype(BF16)
    w_r = jnp.zeros((d, LANES), F32).at[:, :N_GROUPS].set(w_grp[0]).at[:, N_GROUPS:N_GROUPS + N_EXPERTS].set(w_exp[0])
    wr_hi = w_r.astype(BF16)
    wr_lo = (w_r - wr_hi.astype(F32)).astype(BF16)
    b_r = jnp.zeros((1, LANES), F32).at[0, :N_GROUPS].set(b_grp[0]).at[0, N_GROUPS:N_GROUPS + N_EXPERTS].set(b_exp[0])
    row = lambda a: a.reshape(1, -1)
    conv_w0 = conv_w[0]
    nstate = CONV_WIDTH - 1
    win_row = np.array([m if m < nstate else XPAD_NEW + (m - nstate) for m in range(nstate + ls)])
    wsh = jnp.stack([jnp.zeros((XPAD_ROWS, CONV_CH), F32).at[win_row[i:i + CONV_WIDTH]].set(conv_w0)
                     for i in range(ls)])

    shared_w = (w_in_b, row(b_in[0]), row(ret_gn_g[0]), row(ret_gn_b[0]), w_ret_o_b)
    tail_w = (row(conv_ln_g[0]), row(conv_ln_b[0]), w_conv_o_b, w_out_b, row(ln1_g[0]), row(ln1_b[0]),
              wr_hi, wr_lo, b_r)

    nbt = bs // BB_SAMPLE
    xs2 = x_sample.reshape(n_s, d)
    sample_in = ((xs2, cos_s, sin_s, pdec_s, qdec_s, kdec_s, cdec_s, wsh, state_ret, state_conv)
                 + shared_w + (row(conv_b[0]),) + tail_w)
    sample_specs = (
        [pl.BlockSpec((ts, d), lambda i: (i, 0))]
        + [_const_spec(a.shape) for a in sample_in[1:8]]
        + [pl.BlockSpec((1, BB_SAMPLE, RET_HEADS, RET_DK, RET_DV), lambda i: (0, i, 0, 0, 0)),
           pl.BlockSpec((1, BB_SAMPLE, nstate, CONV_CH), lambda i: (0, i, 0, 0))]
        + [_const_spec(a.shape) for a in sample_in[10:]]
    )
    tok_spec_s = lambda w: pl.BlockSpec((ts, w), lambda i: (i, 0))
    x1_s, rw_s, ret_s, conv_s = pl.pallas_call(
        _sample_mixer_kernel,
        grid=(nbt,),
        in_specs=sample_specs,
        out_specs=[
            tok_spec_s(d), tok_spec_s(LANES),
            pl.BlockSpec((1, BB_SAMPLE, RET_HEADS, RET_DK, RET_DV), lambda i: (0, i, 0, 0, 0)),
            pl.BlockSpec((1, BB_SAMPLE, nstate, CONV_CH), lambda i: (0, i, 0, 0)),
        ],
        out_shape=[
            jax.ShapeDtypeStruct((n_s, d), F32),
            jax.ShapeDtypeStruct((n_s, LANES), F32),
            jax.ShapeDtypeStruct(state_ret.shape, F32),
            jax.ShapeDtypeStruct(state_conv.shape, F32),
        ],
        scratch_shapes=[
            pltpu.VMEM((ts, RET_V), F32),
            pltpu.VMEM((CONV_CH // LANES, ts, LANES), F32),
            pltpu.VMEM((BB_SAMPLE, XPAD_ROWS, CONV_CH), F32),
        ],
        compiler_params=pltpu.CompilerParams(
            dimension_semantics=("arbitrary",), vmem_limit_bytes=VMEM_LIMIT),
        name="sample_mixer",
    )(*sample_in)

    nlt = lp // TL
    nst = n_s // TL
    assert nst <= nlt
    prompt_in = ((x_prompt, x1_s, rw_s, cos_p, sin_p, decay, qdec_p, kdec_p, cdec_p)
                 + shared_w + (conv_w0, row(conv_b[0])) + tail_w)
    seq_idx = lambda b: jnp.minimum(b, bp - 1)
    tail_idx = lambda l: jnp.minimum(l, nst - 1)
    tail_spec = lambda w: pl.BlockSpec((TL, w), lambda b, l: (tail_idx(l), 0))
    prompt_specs = [
        pl.BlockSpec((1, TL, d), lambda b, l: (seq_idx(b), l, 0)),
        tail_spec(d), tail_spec(LANES),
        pl.BlockSpec((TL, RET_DK), lambda b, l: (l, 0)),
        pl.BlockSpec((TL, RET_DK), lambda b, l: (l, 0)),
    ] + [_const_spec(a.shape) for a in prompt_in[5:]]
    tile_idx = lambda b, l: jnp.where(b < bp, b * nlt + l, bp * nlt + tail_idx(l))
    tok_spec_p = lambda rows, w: pl.BlockSpec((rows, w), lambda b, l: (tile_idx(b, l), 0))
    x1_all, rw_all, xs_all, meta, ret_p, conv_p = pl.pallas_call(
        functools.partial(_prompt_mixer_kernel, n_seq=bp, n_sample_tiles=nst),
        grid=(bp + 1, nlt),
        in_specs=prompt_specs,
        out_specs=[
            tok_spec_p(TL, d), tok_spec_p(TL, LANES), tok_spec_p(CAP, HALF), tok_spec_p(SUBLANES, LANES),
            pl.BlockSpec((1, 1, RET_HEADS, RET_DK, RET_DV), lambda b, l: (0, seq_idx(b), 0, 0, 0)),
            pl.BlockSpec((1, 1, nstate, CONV_CH), lambda b, l: (0, seq_idx(b), 0, 0)),
        ],
        out_shape=[
            jax.ShapeDtypeStruct((n_tok, d), F32),
            jax.ShapeDtypeStruct((n_tok, LANES), F32),
            jax.ShapeDtypeStruct((n_blocks * CAP, HALF), U32),
            jax.ShapeDtypeStruct((n_blocks * SUBLANES, LANES), I32),
            jax.ShapeDtypeStruct((1, bp, RET_HEADS, RET_DK, RET_DV), F32),
            jax.ShapeDtypeStruct((1, bp, nstate, CONV_CH), F32),
        ],
        scratch_shapes=[
            pltpu.VMEM((TL + CONV_PAD, CONV_CH), F32),
            pltpu.VMEM((SUBLANES - 1, TL + CONV_PAD - SUBLANES, CONV_CH), F32),
            pltpu.VMEM((TL, RET_V), F32),
            pltpu.VMEM((TL, CONV_CH), F32),
        ],
        compiler_params=pltpu.CompilerParams(
            dimension_semantics=("arbitrary", "arbitrary"), vmem_limit_bytes=VMEM_LIMIT),
        name="prompt_mixer",
    )(*prompt_in)

    max_chunks = n_blocks * (TOP_K * TL // CHUNK + N_EXPERTS - 1)
    n_ffn_tiles = (max_chunks + N_EXPERTS * (TILE_CHUNKS - 1)) // TILE_CHUNKS
    tile_e, n_valid_tiles, chunk_ids = _chunk_plan(meta, n_blocks, n_ffn_tiles)

    ys_all = pl.pallas_call(
        _ffn_kernel,
        grid_spec=pltpu.PrefetchScalarGridSpec(
            num_scalar_prefetch=3,
            grid=(n_ffn_tiles,),
            in_specs=[
                pl.BlockSpec(memory_space=pl.ANY),
                pl.BlockSpec((1, d, 2 * EXP_FF), lambda i, te, nr, ch: (te[i], 0, 0)),
                pl.BlockSpec((1, EXP_FF, d), lambda i, te, nr, ch: (te[i], 0, 0)),
            ],
            out_specs=pl.BlockSpec(memory_space=pl.ANY),
            scratch_shapes=[
                pltpu.VMEM((2, TM_FFN, HALF), U32),
                pltpu.VMEM((2, TM_FFN, HALF), U32),
                pltpu.VMEM((d, 2 * EXP_FF), BF16),
                pltpu.VMEM((EXP_FF, d), BF16),
                pltpu.SemaphoreType.DMA((2,)),
                pltpu.SemaphoreType.DMA((2,)),
            ],
        ),
        out_shape=jax.ShapeDtypeStruct(xs_all.shape, U32),
        input_output_aliases={3: 0},
        compiler_params=pltpu.CompilerParams(
            dimension_semantics=("arbitrary",), vmem_limit_bytes=VMEM_LIMIT),
        name="expert_ffn",
    )(tile_e, n_valid_tiles, chunk_ids, xs_all, w_gu[0], w_dn[0])

    npt = n_p // TL
    pp2 = p_prompt.reshape(n_p, PLE_DIM)
    ps2 = p_sample.reshape(n_s, PLE_DIM)
    tok_f = lambda rows, w: pl.BlockSpec((rows, w), lambda i: (i, 0))
    y_p, y_s = pl.pallas_call(
        functools.partial(_final_kernel, n_prompt_tiles=npt),
        grid=(n_blocks,),
        in_specs=[
            tok_f(CAP, HALF), tok_f(TL, d), tok_f(TL, LANES),
            pl.BlockSpec((TL, PLE_DIM), lambda i: (jnp.minimum(i, npt - 1), 0)),
            pl.BlockSpec((TL, PLE_DIM), lambda i: (jnp.maximum(i - npt, 0), 0)),
            _const_spec((1, d)), _const_spec((1, d)), _const_spec((d, d)), _const_spec((1, d)),
            _const_spec((PLE_DIM, d)),
        ],
        out_specs=[
            pl.BlockSpec((TL, d), lambda i: (jnp.minimum(i, npt - 1), 0)),
            pl.BlockSpec((TL, d), lambda i: (jnp.maximum(i - npt, 0), 0)),
        ],
        out_shape=[jax.ShapeDtypeStruct((n_p, d), F32), jax.ShapeDtypeStruct((n_s, d), F32)],
        compiler_params=pltpu.CompilerParams(
            dimension_semantics=("arbitrary",), vmem_limit_bytes=VMEM_LIMIT),
        name="moe_combine_final",
    )(ys_all, x1_all, rw_all, pp2, ps2,
      row(ln2_g[0]), row(ln2_b[0]), w_pg_b, row(b_pg[0]), w_ple_b)

    return (y_p.reshape(bp, lp, d), y_s.reshape(bs, ls, d), ret_p, conv_p, ret_s, conv_s)
```

```python
import functools

import jax
import jax.numpy as jnp
import numpy as np
from jax import lax
from jax.experimental import pallas as pl
from jax.experimental.pallas import tpu as pltpu

F32 = jnp.float32
BF16 = jnp.bfloat16
I32 = jnp.int32
U32 = jnp.uint32

D_MODEL = 1024
PAST_LEN = 16384
RET_HEADS = 4
RET_DK = 128
RET_DV = 128
RET_QK = RET_HEADS * RET_DK
RET_V = RET_HEADS * RET_DV
RET_CHUNK = 128
ROPE_BASE = 10000.0
CONV_CH = 512
CONV_WIDTH = 31
N_GROUPS = 4
EXP_PER_GROUP = 4
N_EXPERTS = N_GROUPS * EXP_PER_GROUP
TOP_K = 2
EXP_FF = 512
PLE_DIM = 256
DEPTH = 1
ALPHA = (2 * DEPTH) ** 0.25
LN_EPS = 1e-5
IN_WIDTHS = (RET_QK, RET_QK, RET_V, RET_V, CONV_CH, CONV_CH, D_MODEL, D_MODEL)
IN_OFFS = tuple(int(s) for s in np.cumsum((0,) + IN_WIDTHS))

LANES = 128
SUBLANES = 8
VMEM_LIMIT = 56 * 1024 * 1024

TL = 256
BB_SAMPLE = 16
CHUNK = SUBLANES
TILE_CHUNKS = 32
BLOCK_USED = -(-(TOP_K * TL + N_EXPERTS * (CHUNK - 1)) // (CHUNK * 16)) * 16
BLOCK_SPARE = 16
BLOCK_CHUNKS = BLOCK_USED + BLOCK_SPARE
CAP = BLOCK_CHUNKS * CHUNK
TM_FFN = TILE_CHUNKS * CHUNK
HALF = D_MODEL // 2
CONV_PAD = 32
XPAD_NEW = 32
XPAD_ROWS = 40


def _ln(x, g, b):
    mu = jnp.mean(x, axis=-1, keepdims=True)
    d = x - mu
    var = jnp.mean(d * d, axis=-1, keepdims=True)
    return d * lax.rsqrt(var + LN_EPS) * g + b


def _sigmoid(x):
    return 1.0 / (1.0 + jnp.exp(-x))


def _silu(x):
    return x * _sigmoid(x)


def _bdot(a, b):
    return jnp.dot(a.astype(BF16), b, preferred_element_type=F32)


def _rot(t, cosf, sinf):
    return t * cosf + pltpu.roll(t, RET_DK // 2, axis=1) * sinf


def _pack_bf16_pairs(x):
    bits = lax.bitcast_convert_type(x.astype(BF16).astype(F32), U32)
    half = x.shape[1] // 2
    return (bits[:, :half] >> 16) | (bits[:, half:] & jnp.uint32(0xFFFF0000))


def _unpack_bf16_pairs(pk):
    lo = lax.bitcast_convert_type(pk << 16, F32).astype(BF16)
    hi = lax.bitcast_convert_type(pk & jnp.uint32(0xFFFF0000), F32).astype(BF16)
    return lo, hi


def _lane_tile(cols, rows):
    lane = lax.broadcasted_iota(I32, (rows, LANES), 1)
    out = jnp.zeros((rows, LANES), F32)
    for i, col in enumerate(cols):
        out = jnp.where(lane == i, col, out)
    return out


def _route(logits):
    lane = lax.broadcasted_iota(I32, logits.shape, 1)
    lanef = lane.astype(F32)
    ninf = jnp.float32(-jnp.inf)
    big = jnp.float32(LANES)
    gmask = lane < N_GROUPS
    gl = jnp.where(gmask, logits, ninf)
    gmax = jnp.max(gl, axis=1, keepdims=True)
    gidx = jnp.min(jnp.where(gmask & (gl == gmax), lanef, big), axis=1, keepdims=True)
    sumexp = jnp.sum(jnp.where(gmask, jnp.exp(gl - gmax), 0.0), axis=1, keepdims=True)
    gw = 1.0 / sumexp
    lo = N_GROUPS + EXP_PER_GROUP * gidx
    emask = (lanef >= lo) & (lanef < lo + EXP_PER_GROUP)
    el = jnp.where(emask, logits, ninf)
    m1 = jnp.max(el, axis=1, keepdims=True)
    i1 = jnp.min(jnp.where(emask & (el == m1), lanef, big), axis=1, keepdims=True)
    emask2 = emask & (lanef != i1)
    el2 = jnp.where(emask2, logits, ninf)
    m2 = jnp.max(el2, axis=1, keepdims=True)
    i2 = jnp.min(jnp.where(emask2 & (el2 == m2), lanef, big), axis=1, keepdims=True)
    t = jnp.exp(m2 - m1)
    den = 1.0 + t
    return (1.0 / den) * gw, (t / den) * gw, i1 - N_GROUPS, i2 - N_GROUPS


def _post_mix(x, ret_n, g, c_out, gt_a, gt_b, w):
    (w_ret_o, cln_g, cln_b, w_conv_o, w_out, ln1_g, ln1_b, wr_hi, wr_lo, b_r) = w
    branch_a = _bdot(_silu(g) * ret_n, w_ret_o[...])
    branch_b = _bdot(_silu(_ln(c_out, cln_g[...], cln_b[...])), w_conv_o[...])
    mix = _sigmoid(gt_a) * branch_a + _sigmoid(gt_b) * branch_b
    h = ALPHA * x + _bdot(mix, w_out[...])
    x1 = _ln(h, ln1_g[...], ln1_b[...])
    x1_hi = x1.astype(BF16)
    x1_lo = (x1 - x1_hi.astype(F32)).astype(BF16)
    logits = (jnp.dot(x1_hi, wr_hi[...], preferred_element_type=F32)
              + (jnp.dot(x1_lo, wr_hi[...], preferred_element_type=F32)
                 + jnp.dot(x1_hi, wr_lo[...], preferred_element_type=F32))
              + b_r[...])
    return (x1,) + _route(logits)


def _sort_tile(x1, w1, w2, e1, e2, x1_ref, rw_ref, xs_ref, meta_ref):
    t = x1.shape[0]
    lane = lax.broadcasted_iota(I32, (t, LANES), 1).astype(F32)
    a1 = (lane == e1).astype(F32)
    a2 = (lane == e2).astype(F32)
    ri = lax.broadcasted_iota(I32, (t, t), 0)
    ci = lax.broadcasted_iota(I32, (t, t), 1)
    earlier = (ci < ri).astype(BF16)
    r1 = jnp.dot(earlier, a1.astype(BF16), preferred_element_type=F32)
    r2 = jnp.dot(earlier, a2.astype(BF16), preferred_element_type=F32)
    cnt1 = jnp.sum(a1, axis=0, keepdims=True)
    cnt = cnt1 + jnp.sum(a2, axis=0, keepdims=True)
    nch = jnp.floor((cnt + (CHUNK - 1.0)) * (1.0 / CHUNK))
    ui = lax.broadcasted_iota(I32, (LANES, LANES), 0)
    uj = lax.broadcasted_iota(I32, (LANES, LANES), 1)
    before = (ui < uj).astype(BF16)
    off = jnp.dot(jnp.broadcast_to(nch, (2 * SUBLANES, LANES)).astype(BF16), before,
                  preferred_element_type=F32)[0:1, :]
    base = off * CHUNK
    pos1 = jnp.sum(a1 * (base + r1), axis=1, keepdims=True)
    pos2 = jnp.sum(a2 * (base + cnt1 + r2), axis=1, keepdims=True)
    slot = lax.broadcasted_iota(I32, (t, CAP), 1).astype(F32)
    onehot_t = ((slot == pos1) | (slot == pos2)).astype(BF16)
    xs = lax.dot_general(onehot_t, x1.astype(BF16), (((0,), (0,)), ((), ())), preferred_element_type=F32)
    x1_ref[...] = x1
    rw_ref[...] = _lane_tile((w1, w2, pos1, pos2), t)
    xs_ref[...] = _pack_bf16_pairs(xs)
    srow = lax.broadcasted_iota(I32, (SUBLANES, LANES), 0)
    meta = jnp.where(srow == 0, cnt, jnp.where(srow == 1, off, 0.0))
    meta_ref[...] = meta.astype(I32)


def _prompt_mixer_kernel(x_ref, x1s_ref, rws_ref, *refs, n_seq, n_sample_tiles):
    bi = pl.program_id(0)
    li = pl.program_id(1)
    outs = refs[-11:-7]

    @pl.when(bi < n_seq)
    def _mix():
        _prompt_mixer_body(x_ref, *refs)

    @pl.when((bi == n_seq) & (li < n_sample_tiles))
    def _append():
        rws = rws_ref[...]
        _sort_tile(x1s_ref[...], rws[:, 0:1], rws[:, 1:2], rws[:, 2:3], rws[:, 3:4], *outs)


def _prompt_mixer_body(x_ref, cos_ref, sin_ref, dec_ref, qdec_ref, kdec_ref, cdec_ref,
                       w_in, b_in, gn_g, gn_b, w_ret_o, conv_w, conv_b, cln_g, cln_b,
                       w_conv_o, w_out, ln1_g, ln1_b, wr_hi, wr_lo, b_r,
                       x1_ref, rw_ref, xs_ref, meta_ref, sret_ref, sconv_ref,
                       ubuf, ushift, ret_scr, cout_scr, proj_scr):
    tl = x_ref.shape[1]
    li = pl.program_id(1)
    nl = pl.num_programs(1)

    @pl.when(li == 0)
    def _init():
        sret_ref[...] = jnp.zeros(sret_ref.shape, F32)
        ubuf[0:CONV_PAD, :] = jnp.zeros((CONV_PAD, CONV_CH), F32)

    x = x_ref[0]
    xb = x.astype(BF16)

    def proj(k):
        c0, c1 = IN_OFFS[k], IN_OFFS[k + 1]
        return jnp.dot(xb, w_in[:, c0:c1], preferred_element_type=F32) + b_in[:, c0:c1]

    u = proj(4) * _sigmoid(proj(5))
    ubuf[CONV_PAD:CONV_PAD + tl, :] = u
    nsh = ushift.shape[1]
    for s in range(1, SUBLANES):
        ushift[s - 1] = ubuf[s:s + nsh, :]
    slab = 256
    others = (0, 1, 2, 3, 6, 7)
    scr_col = {}
    ncol = 0
    for kk in others:
        scr_col[kk] = ncol
        ncol += IN_WIDTHS[kk]
    slabs = [(kk, c0) for kk in others for c0 in range(IN_OFFS[kk], IN_OFFS[kk + 1], slab)]
    rb = 32
    nrb = tl // rb
    for r in range(nrb):
        acc = jnp.zeros((rb, CONV_CH), F32) + conv_b[...]
        for j in range(CONV_WIDTH):
            off = j + (CONV_PAD - (CONV_WIDTH - 1))
            s = off % SUBLANES
            base = r * rb + off - s
            win = ubuf[base:base + rb, :] if s == 0 else ushift[s - 1, base:base + rb, :]
            acc = acc + conv_w[j:j + 1, :] * win
        cout_scr[r * rb:(r + 1) * rb, :] = acc
        for kk, c0 in slabs[r * len(slabs) // nrb:(r + 1) * len(slabs) // nrb]:
            dst = scr_col[kk] + c0 - IN_OFFS[kk]
            proj_scr[:, dst:dst + slab] = (jnp.dot(xb, w_in[:, c0:c0 + slab], preferred_element_type=F32)
                                           + b_in[:, c0:c0 + slab])
    ubuf[0:CONV_PAD, :] = ubuf[tl:tl + CONV_PAD, :]

    def pcols(kk):
        return slice(scr_col[kk], scr_col[kk] + IN_WIDTHS[kk])

    scale = RET_DK ** -0.5
    for c in range(tl // RET_CHUNK):
        rows = slice(c * RET_CHUNK, (c + 1) * RET_CHUNK)
        cosf = cos_ref[rows, :]
        sinf = sin_ref[rows, :]
        for h in range(RET_HEADS):
            cols = slice(h * RET_DK, (h + 1) * RET_DK)
            hcol = lambda kk: slice(scr_col[kk] + h * RET_DK, scr_col[kk] + (h + 1) * RET_DK)
            qh = _rot(proj_scr[rows, hcol(0)], cosf, sinf)
            kh = _rot(proj_scr[rows, hcol(1)], cosf, sinf) * scale
            qb = qh.astype(BF16)
            kb = kh.astype(BF16)
            vb = proj_scr[rows, hcol(2)].astype(BF16)
            s_old = sret_ref[0, 0, h]
            scores = lax.dot_general(qb, kb, (((1,), (1,)), ((), ())),
                                     preferred_element_type=F32) * dec_ref[h]
            inner = jnp.dot(scores.astype(BF16), vb, preferred_element_type=F32)
            cross = jnp.dot(qb, s_old.astype(BF16), preferred_element_type=F32) * qdec_ref[h]
            kd = (kh * kdec_ref[h]).astype(BF16)
            s_new = cdec_ref[h] * s_old + lax.dot_general(
                kd, vb, (((0,), (0,)), ((), ())), preferred_element_type=F32)
            sret_ref[0, 0, h] = s_new
            ret_scr[rows, cols] = _ln(inner + cross, gn_g[:, cols], gn_b[:, cols])

    g = proj_scr[:, pcols(3)]
    gt_a = proj_scr[:, pcols(6)]
    gt_b = proj_scr[:, pcols(7)]
    x1, w1, w2, e1, e2 = _post_mix(
        x, ret_scr[...], g, cout_scr[...], gt_a, gt_b,
        (w_ret_o, cln_g, cln_b, w_conv_o, w_out, ln1_g, ln1_b, wr_hi, wr_lo, b_r))
    _sort_tile(x1, w1, w2, e1, e2, x1_ref, rw_ref, xs_ref, meta_ref)

    @pl.when(li == nl - 1)
    def _conv_state():
        sconv_ref[0, 0] = ubuf[CONV_PAD - (CONV_WIDTH - 1):CONV_PAD, :]


def _sample_mixer_kernel(x_ref, cos_ref, sin_ref, pdec_ref, qdec_ref, kdec_ref, cdec_ref, wsh_ref,
                         sret_in, sconv_in,
                         w_in, b_in, gn_g, gn_b, w_ret_o, conv_b, cln_g, cln_b,
                         w_conv_o, w_out, ln1_g, ln1_b, wr_hi, wr_lo, b_r,
                         x1_ref, rw_ref, sret_ref, sconv_ref,
                         ret_scr, cout_scr, xpad):
    t = x_ref.shape[0]
    ls = t // BB_SAMPLE
    x = x_ref[...]
    xb = x.astype(BF16)

    def proj(k):
        c0, c1 = IN_OFFS[k], IN_OFFS[k + 1]
        return jnp.dot(xb, w_in[:, c0:c1], preferred_element_type=F32) + b_in[:, c0:c1]

    q = proj(0)
    k = proj(1)
    v = proj(2)
    scale = RET_DK ** -0.5
    cosf = cos_ref[...]
    sinf = sin_ref[...]
    row = lax.broadcasted_iota(I32, (t, RET_DK), 0)
    pos = row % ls
    row8 = lax.broadcasted_iota(I32, (SUBLANES, RET_DK), 0)
    per_tile = SUBLANES // ls
    for h in range(RET_HEADS):
        cols = slice(h * RET_DK, (h + 1) * RET_DK)
        qh = _rot(q[:, cols], cosf, sinf)
        kh = _rot(k[:, cols], cosf, sinf) * scale
        vh = v[:, cols]
        inner = jnp.zeros((t, RET_DV), F32)
        for s in range(ls):
            ks = kh if s == 0 else pltpu.roll(kh, s, axis=0)
            vs = vh if s == 0 else pltpu.roll(vh, s, axis=0)
            dotp = jnp.sum(qh * ks, axis=1, keepdims=True) * pdec_ref[h, s]
            inner = inner + jnp.where(pos >= s, dotp, 0.0) * vs
        kd = kh * kdec_ref[h]
        for tile in range(t // SUBLANES):
            rows = slice(tile * SUBLANES, (tile + 1) * SUBLANES)
            q8 = qh[rows, :]
            kd8 = kd[rows, :]
            v8 = vh[rows, :]
            cross8 = jnp.zeros((SUBLANES, RET_DV), F32)
            for sub in range(per_tile):
                b = tile * per_tile + sub
                mine = (row8 >= sub * ls) & (row8 < (sub + 1) * ls)
                s_old = sret_in[0, b, h]
                c_b = jnp.dot(q8, s_old, preferred_element_type=F32)
                cross8 = jnp.where(mine, c_b, cross8)
                upd = lax.dot_general(jnp.where(mine, kd8, 0.0), v8, (((0,), (0,)), ((), ())),
                                      preferred_element_type=F32)
                sret_ref[0, b, h] = cdec_ref[h] * s_old + upd
            ret_scr[rows, cols] = inner[rows, :] + cross8 * qdec_ref[h, rows, :]
        ret_scr[:, cols] = _ln(ret_scr[:, cols], gn_g[:, cols], gn_b[:, cols])

    u = proj(4) * _sigmoid(proj(5))
    nstate = CONV_WIDTH - 1
    xpad[...] = jnp.zeros(xpad.shape, F32)
    xpad[:, 0:nstate, :] = sconv_in[0]
    for b in range(BB_SAMPLE):
        xpad[b, XPAD_NEW:XPAD_NEW + ls, :] = u[b * ls:(b + 1) * ls, :]
    for i in range(ls):
        res = jnp.sum(xpad[...] * wsh_ref[i][None], axis=1) + conv_b[...]
        for sl in range(CONV_CH // LANES):
            cout_scr[sl, pl.ds(i, BB_SAMPLE, stride=ls), :] = res[:, sl * LANES:(sl + 1) * LANES]
    sconv_ref[0, :, 0:nstate - ls, :] = xpad[:, ls:nstate, :]
    sconv_ref[0, :, nstate - ls:nstate, :] = xpad[:, XPAD_NEW:XPAD_NEW + ls, :]
    c_out = jnp.concatenate([cout_scr[sl] for sl in range(CONV_CH // LANES)], axis=1)

    g = proj(3)
    gt_a = proj(6)
    gt_b = proj(7)
    x1, w1, w2, e1, e2 = _post_mix(
        x, ret_scr[...], g, c_out, gt_a, gt_b,
        (w_ret_o, cln_g, cln_b, w_conv_o, w_out, ln1_g, ln1_b, wr_hi, wr_lo, b_r))
    x1_ref[...] = x1
    rw_ref[...] = _lane_tile((w1, w2, e1, e2), t)


def _ffn_kernel(te_ref, nvalid_ref, chunk_ref, xs_hbm, w_gu, w_dn, ys_hbm,
                xbuf, obuf, wgu_b, wdn_b, sem_in, sem_out):
    del xs_hbm
    i = pl.program_id(0)
    n = pl.num_programs(0)
    slot = i % 2
    nvalid = nvalid_ref[0]

    def chunk_rows(tile, c):
        return pl.ds(pl.multiple_of(chunk_ref[tile * TILE_CHUNKS + c] * CHUNK, CHUNK), CHUNK)

    def start_in(tile, s):
        for c in range(TILE_CHUNKS):
            pltpu.make_async_copy(ys_hbm.at[chunk_rows(tile, c)],
                                  xbuf.at[s, pl.ds(c * CHUNK, CHUNK)], sem_in.at[s]).start()

    def start_out(tile, s):
        for c in range(TILE_CHUNKS):
            pltpu.make_async_copy(obuf.at[s, pl.ds(c * CHUNK, CHUNK)],
                                  ys_hbm.at[chunk_rows(tile, c)], sem_out.at[s]).start()

    def wait_in(s):
        pltpu.make_async_copy(ys_hbm.at[pl.ds(0, TM_FFN)], xbuf.at[s], sem_in.at[s]).wait()

    def wait_out(s):
        pltpu.make_async_copy(obuf.at[s], ys_hbm.at[pl.ds(0, TM_FFN)], sem_out.at[s]).wait()

    @pl.when((i == 0) & (nvalid > 0))
    def _first():
        start_in(0, 0)

    @pl.when(i + 1 < nvalid)
    def _prefetch():
        start_in(i + 1, 1 - slot)

    @pl.when((i >= 2) & (i - 2 < nvalid))
    def _retire():
        wait_out(slot)

    @pl.when(i < nvalid)
    def _tile():
        wait_in(slot)
        prev = te_ref[jnp.maximum(i - 1, 0)]

        @pl.when((i == 0) | (te_ref[i] != prev))
        def _new_expert():
            wgu_b[...] = w_gu[0].astype(BF16)
            wdn_b[...] = w_dn[0].astype(BF16)

        lo, hi = _unpack_bf16_pairs(xbuf[slot])
        hcat = (jnp.dot(lo, wgu_b[0:HALF, :], preferred_element_type=F32)
                + jnp.dot(hi, wgu_b[HALF:, :], preferred_element_type=F32))
        act = _silu(hcat[:, :EXP_FF]) * hcat[:, EXP_FF:]
        obuf[slot] = _pack_bf16_pairs(_bdot(act, wdn_b[...]))
        start_out(i, slot)

    @pl.when(i == n - 1)
    def _drain():
        @pl.when((i >= 1) & (i - 1 < nvalid))
        def _():
            wait_out(1 - slot)

        @pl.when(i < nvalid)
        def _():
            wait_out(slot)


def _final_kernel(ys_ref, x1_ref, rw_ref, pp_ref, ps_ref, ln2_g, ln2_b, w_pg, b_pg, w_ple,
                  yp_ref, ys_out_ref, *, n_prompt_tiles):
    i = pl.program_id(0)
    tl = x1_ref.shape[0]
    x1 = x1_ref[...]
    rw = rw_ref[...]
    w1, w2, pos1, pos2 = rw[:, 0:1], rw[:, 1:2], rw[:, 2:3], rw[:, 3:4]
    slot = lax.broadcasted_iota(I32, (tl, CAP), 1).astype(F32)
    sel1 = (slot == pos1).astype(BF16)
    sel2 = (slot == pos2).astype(BF16)
    lo, hi = _unpack_bf16_pairs(ys_ref[...])
    pick = lambda sel: jnp.concatenate([jnp.dot(sel, lo, preferred_element_type=F32),
                                        jnp.dot(sel, hi, preferred_element_type=F32)], axis=1)
    moe = pick(sel1) * w1 + pick(sel2) * w2
    x2 = _ln(ALPHA * x1 + moe, ln2_g[...], ln2_b[...])
    gate = _sigmoid(_bdot(x2, w_pg[...]) + b_pg[...])

    @pl.when(i < n_prompt_tiles)
    def _prompt():
        yp_ref[...] = x2 + gate * _bdot(pp_ref[...], w_ple[...])

    @pl.when(i >= n_prompt_tiles)
    def _sample():
        ys_out_ref[...] = x2 + gate * _bdot(ps_ref[...], w_ple[...])


def _rope_tables(pos):
    half = RET_DK // 2
    inv_freq = ROPE_BASE ** (-jnp.arange(half, dtype=F32) / half)
    ang = pos[:, None] * inv_freq[None, :]
    cos = jnp.cos(ang)
    sin = jnp.sin(ang)
    return jnp.concatenate([cos, cos], axis=-1), jnp.concatenate([-sin, sin], axis=-1)


def _log_gamma():
    return jnp.log(1.0 - 2.0 ** (-5.0 - jnp.arange(RET_HEADS, dtype=F32)))


def _const_spec(shape):
    nd = len(shape)
    return pl.BlockSpec(shape, lambda *_: (0,) * nd, pipeline_mode=pl.Buffered(1))


def _chunk_plan(meta, n_blocks, n_ffn_tiles):
    assert n_blocks * BLOCK_SPARE >= N_EXPERTS * (TILE_CHUNKS - 1)
    m = meta.reshape(n_blocks, SUBLANES, LANES)
    cnt = m[:, 0, :N_EXPERTS]
    off = m[:, 1, :N_EXPERTS]
    nch = (cnt + (CHUNK - 1)) // CHUNK
    cum = jnp.cumsum(nch, axis=0)
    total = cum[-1]
    tiles_e = (total + TILE_CHUNKS - 1) // TILE_CHUNKS
    tile_end = jnp.cumsum(tiles_e)
    tile_ids = jnp.arange(n_ffn_tiles, dtype=I32)
    te = jnp.minimum(jnp.sum((tile_end[None, :] <= tile_ids[:, None]).astype(I32), axis=1), N_EXPERTS - 1)
    tile_start = (tile_end - tiles_e)[te]
    k = (tile_ids - tile_start)[:, None] * TILE_CHUNKS + jnp.arange(TILE_CHUNKS, dtype=I32)[None, :]
    real = k < total[te][:, None]
    cum_e = cum.T[te]
    blk = jnp.sum((cum_e[:, None, :] <= k[:, :, None]).astype(I32), axis=2)
    blk = jnp.minimum(blk, n_blocks - 1)
    excl = jnp.take_along_axis(cum_e - nch.T[te], blk, axis=1)
    off_e = jnp.take_along_axis(off.T[te], blk, axis=1)
    spare = te[:, None] * (TILE_CHUNKS - 1) + jnp.maximum(k - total[te][:, None], 0) % TILE_CHUNKS
    spare_chunk = (spare // BLOCK_SPARE) * BLOCK_CHUNKS + BLOCK_USED + spare % BLOCK_SPARE
    chunk = jnp.where(real, blk * BLOCK_CHUNKS + off_e + (k - excl), spare_chunk)
    return te.astype(I32), tile_end[-1:].astype(I32), chunk.reshape(-1).astype(I32)


def kernel(x_prompt, x_sample, state_ret, state_conv, p_prompt, p_sample, w_in, b_in, ret_gn_g, ret_gn_b,
           w_ret_o, conv_w, conv_b, conv_ln_g, conv_ln_b, w_conv_o, w_out, ln1_g, ln1_b, w_grp, b_grp,
           w_exp, b_exp, w_gu, w_dn, ln2_g, ln2_b, w_pg, b_pg, w_ple):
    assert DEPTH == 1 and w_in.shape[0] == 1
    bp, lp, d = x_prompt.shape
    bs, ls, _ = x_sample.shape
    n_p, n_s = bp * lp, bs * ls
    n_tok = n_p + n_s
    assert lp % TL == 0 and n_s % TL == 0 and bs % BB_SAMPLE == 0 and SUBLANES % ls == 0
    n_blocks = n_tok // TL

    lg = _log_gamma()
    c = RET_CHUNK
    idx = jnp.arange(c, dtype=F32)
    rel = idx[:, None] - idx[None, :]
    causal = rel >= 0
    decay = jnp.where(causal[None], jnp.exp(jnp.where(causal, rel, 0.0)[None] * lg[:, None, None]), 0.0)
    q_decay = jnp.exp((idx[:, None] + 1.0) * lg[None, :])
    k_decay = jnp.exp((c - 1.0 - idx[:, None]) * lg[None, :])
    chunk_decay = jnp.exp(c * lg)
    qdec_p = jnp.broadcast_to(q_decay.T[:, :, None], (RET_HEADS, c, RET_DK))
    kdec_p = jnp.broadcast_to(k_decay.T[:, :, None], (RET_HEADS, c, RET_DK))
    cdec_p = jnp.broadcast_to(chunk_decay[:, None, None], (RET_HEADS, 1, RET_DV))
    cos_p, sin_p = _rope_tables(jnp.arange(lp, dtype=F32))

    ts = BB_SAMPLE * ls
    idx_s = jnp.arange(ls, dtype=F32)
    pdec_s = jnp.exp(idx_s[None, :] * lg[:, None])
    pdec_s = jnp.broadcast_to(pdec_s[:, :, None, None], (RET_HEADS, ls, 1, RET_DK))
    qd_s = jnp.exp((idx_s[:, None] + 1.0) * lg[None, :])
    kd_s = jnp.exp((ls - 1.0 - idx_s[:, None]) * lg[None, :])
    qdec_s = jnp.broadcast_to(jnp.tile(qd_s.T, (1, BB_SAMPLE))[:, :, None], (RET_HEADS, ts, RET_DK))
    kdec_s = jnp.broadcast_to(jnp.tile(kd_s.T, (1, BB_SAMPLE))[:, :, None], (RET_HEADS, ts, RET_DK))
    cdec_s = jnp.broadcast_to(jnp.exp(ls * lg)[:, None, None], (RET_HEADS, 1, RET_DV))
    pos_s = PAST_LEN + jnp.arange(ls, dtype=F32)
    cos_s, sin_s = _rope_tables(jnp.tile(pos_s, BB_SAMPLE))

    w_in_b = w_in[0].astype(BF16)
    w_ret_o_b = w_ret_o[0].astype(BF16)
    w_conv_o_b = w_conv_o[0].astype(BF16)
    w_out_b = w_out[0].astype(BF16)
    w_pg_b = w_pg[0].astype(BF16)
    w_ple_b = w_ple[0].astype(BF16)
    w_r = jnp.zeros((d, LANES), F32).at[:, :N_GROUPS].set(w_grp[0]).at[:, N_GROUPS:N_GROUPS + N_EXPERTS].set(w_exp[0])
    wr_hi = w_r.astype(BF16)
    wr_lo = (w_r - wr_hi.astype(F32)).astype(BF16)
    b_r = jnp.zeros((1, LANES), F32).at[0, :N_GROUPS].set(b_grp[0]).at[0, N_GROUPS:N_GROUPS + N_EXPERTS].set(b_exp[0])
    row = lambda a: a.reshape(1, -1)
    conv_w0 = conv_w[0]
    nstate = CONV_WIDTH - 1
    win_row = np.array([m if m < nstate else XPAD_NEW + (m - nstate) for m in range(nstate + ls)])
    wsh = jnp.stack([jnp.zeros((XPAD_ROWS, CONV_CH), F32).at[win_row[i:i + CONV_WIDTH]].set(conv_w0)
                     for i in range(ls)])

    shared_w = (w_in_b, row(b_in[0]), row(ret_gn_g[0]), row(ret_gn_b[0]), w_ret_o_b)
    tail_w = (row(conv_ln_g[0]), row(conv_ln_b[0]), w_conv_o_b, w_out_b, row(ln1_g[0]), row(ln1_b[0]),
              wr_hi, wr_lo, b_r)

    nbt = bs // BB_SAMPLE
    xs2 = x_sample.reshape(n_s, d)
    sample_in = ((xs2, cos_s, sin_s, pdec_s, qdec_s, kdec_s, cdec_s, wsh, state_ret, state_conv)
                 + shared_w + (row(conv_b[0]),) + tail_w)
    sample_specs = (
        [pl.BlockSpec((ts, d), lambda i: (i, 0))]
        + [_const_spec(a.shape) for a in sample_in[1:8]]
        + [pl.BlockSpec((1, BB_SAMPLE, RET_HEADS, RET_DK, RET_DV), lambda i: (0, i, 0, 0, 0)),
           pl.BlockSpec((1, BB_SAMPLE, nstate, CONV_CH), lambda i: (0, i, 0, 0))]
        + [_const_spec(a.shape) for a in sample_in[10:]]
    )
    tok_spec_s = lambda w: pl.BlockSpec((ts, w), lambda i: (i, 0))
    x1_s, rw_s, ret_s, conv_s = pl.pallas_call(
        _sample_mixer_kernel,
        grid=(nbt,),
        in_specs=sample_specs,
        out_specs=[
            tok_spec_s(d), tok_spec_s(LANES),
            pl.BlockSpec((1, BB_SAMPLE, RET_HEADS, RET_DK, RET_DV), lambda i: (0, i, 0, 0, 0)),
            pl.BlockSpec((1, BB_SAMPLE, nstate, CONV_CH), lambda i: (0, i, 0, 0)),
        ],
        out_shape=[
            jax.ShapeDtypeStruct((n_s, d), F32),
            jax.ShapeDtypeStruct((n_s, LANES), F32),
            jax.ShapeDtypeStruct(state_ret.shape, F32),
            jax.ShapeDtypeStruct(state_conv.shape, F32),
        ],
        scratch_shapes=[
            pltpu.VMEM((ts, RET_V), F32),
            pltpu.VMEM((CONV_CH // LANES, ts, LANES), F32),
            pltpu.VMEM((BB_SAMPLE, XPAD_ROWS, CONV_CH), F32),
        ],
        compiler_params=pltpu.CompilerParams(
            dimension_semantics=("arbitrary",), vmem_limit_bytes=VMEM_LIMIT),
        name="sample_mixer",
    )(*sample_in)

    nlt = lp // TL
    nst = n_s // TL
    assert nst <= nlt
    prompt_in = ((x_prompt, x1_s, rw_s, cos_p, sin_p, decay, qdec_p, kdec_p, cdec_p)
                 + shared_w + (conv_w0, row(conv_b[0])) + tail_w)
    seq_idx = lambda b: jnp.minimum(b, bp - 1)
    tail_idx = lambda l: jnp.minimum(l, nst - 1)
    tail_spec = lambda w: pl.BlockSpec((TL, w), lambda b, l: (tail_idx(l), 0))
    prompt_specs = [
        pl.BlockSpec((1, TL, d), lambda b, l: (seq_idx(b), l, 0)),
        tail_spec(d), tail_spec(LANES),
        pl.BlockSpec((TL, RET_DK), lambda b, l: (l, 0)),
        pl.BlockSpec((TL, RET_DK), lambda b, l: (l, 0)),
    ] + [_const_spec(a.shape) for a in prompt_in[5:]]
    tile_idx = lambda b, l: jnp.where(b < bp, b * nlt + l, bp * nlt + tail_idx(l))
    tok_spec_p = lambda rows, w: pl.BlockSpec((rows, w), lambda b, l: (tile_idx(b, l), 0))
    x1_all, rw_all, xs_all, meta, ret_p, conv_p = pl.pallas_call(
        functools.partial(_prompt_mixer_kernel, n_seq=bp, n_sample_tiles=nst),
        grid=(bp + 1, nlt),
        in_specs=prompt_specs,
        out_specs=[
            tok_spec_p(TL, d), tok_spec_p(TL, LANES), tok_spec_p(CAP, HALF), tok_spec_p(SUBLANES, LANES),
            pl.BlockSpec((1, 1, RET_HEADS, RET_DK, RET_DV), lambda b, l: (0, seq_idx(b), 0, 0, 0)),
            pl.BlockSpec((1, 1, nstate, CONV_CH), lambda b, l: (0, seq_idx(b), 0, 0)),
        ],
        out_shape=[
            jax.ShapeDtypeStruct((n_tok, d), F32),
            jax.ShapeDtypeStruct((n_tok, LANES), F32),
            jax.ShapeDtypeStruct((n_blocks * CAP, HALF), U32),
            jax.ShapeDtypeStruct((n_blocks * SUBLANES, LANES), I32),
            jax.ShapeDtypeStruct((1, bp, RET_HEADS, RET_DK, RET_DV), F32),
            jax.ShapeDtypeStruct((1, bp, nstate, CONV_CH), F32),
        ],
        scratch_shapes=[
            pltpu.VMEM((TL + CONV_PAD, CONV_CH), F32),
            pltpu.VMEM((SUBLANES - 1, TL + CONV_PAD - SUBLANES, CONV_CH), F32),
            pltpu.VMEM((TL, RET_V), F32),
            pltpu.VMEM((TL, CONV_CH), F32),
            pltpu.VMEM((TL, 2 * RET_QK + 2 * RET_V + 2 * D_MODEL), F32),
        ],
        compiler_params=pltpu.CompilerParams(
            dimension_semantics=("arbitrary", "arbitrary"), vmem_limit_bytes=VMEM_LIMIT),
        name="prompt_mixer",
    )(*prompt_in)

    max_chunks = n_blocks * (TOP_K * TL // CHUNK + N_EXPERTS - 1)
    n_ffn_tiles = (max_chunks + N_EXPERTS * (TILE_CHUNKS - 1)) // TILE_CHUNKS
    tile_e, n_valid_tiles, chunk_ids = _chunk_plan(meta, n_blocks, n_ffn_tiles)

    ys_all = pl.pallas_call(
        _ffn_kernel,
        grid_spec=pltpu.PrefetchScalarGridSpec(
            num_scalar_prefetch=3,
            grid=(n_ffn_tiles,),
            in_specs=[
                pl.BlockSpec(memory_space=pl.ANY),
                pl.BlockSpec((1, d, 2 * EXP_FF), lambda i, te, nr, ch: (te[i], 0, 0)),
                pl.BlockSpec((1, EXP_FF, d), lambda i, te, nr, ch: (te[i], 0, 0)),
            ],
            out_specs=pl.BlockSpec(memory_space=pl.ANY),
            scratch_shapes=[
                pltpu.VMEM((2, TM_FFN, HALF), U32),
                pltpu.VMEM((2, TM_FFN, HALF), U32),
                pltpu.VMEM((d, 2 * EXP_FF), BF16),
                pltpu.VMEM((EXP_FF, d), BF16),
                pltpu.SemaphoreType.DMA((2,)),
                pltpu.SemaphoreType.DMA((2,)),
            ],
        ),
        out_shape=jax.ShapeDtypeStruct(xs_all.shape, U32),
        input_output_aliases={3: 0},
        compiler_params=pltpu.CompilerParams(
            dimension_semantics=("arbitrary",), vmem_limit_bytes=VMEM_LIMIT),
        name="expert_ffn",
    )(tile_e, n_valid_tiles, chunk_ids, xs_all, w_gu[0], w_dn[0])

    npt = n_p // TL
    pp2 = p_prompt.reshape(n_p, PLE_DIM)
    ps2 = p_sample.reshape(n_s, PLE_DIM)
    tok_f = lambda rows, w: pl.BlockSpec((rows, w), lambda i: (i, 0))
    y_p, y_s = pl.pallas_call(
        functools.partial(_final_kernel, n_prompt_tiles=npt),
        grid=(n_blocks,),
        in_specs=[
            tok_f(CAP, HALF), tok_f(TL, d), tok_f(TL, LANES),
            pl.BlockSpec((TL, PLE_DIM), lambda i: (jnp.minimum(i, npt - 1), 0)),
            pl.BlockSpec((TL, PLE_DIM), lambda i: (jnp.maximum(i - npt, 0), 0)),
            _const_spec((1, d)), _const_spec((1, d)), _const_spec((d, d)), _const_spec((1, d)),
            _const_spec((PLE_DIM, d)),
        ],
        out_specs=[
            pl.BlockSpec((TL, d), lambda i: (jnp.minimum(i, npt - 1), 0)),
            pl.BlockSpec((TL, d), lambda i: (jnp.maximum(i - npt, 0), 0)),
        ],
        out_shape=[jax.ShapeDtypeStruct((n_p, d), F32), jax.ShapeDtypeStruct((n_s, d), F32)],
        compiler_params=pltpu.CompilerParams(
            dimension_semantics=("arbitrary",), vmem_limit_bytes=VMEM_LIMIT),
        name="moe_combine_final",
    )(ys_all, x1_all, rw_all, pp2, ps2,
      row(ln2_g[0]), row(ln2_b[0]), w_pg_b, row(b_pg[0]), w_ple_b)

    return (y_p.reshape(bp, lp, d), y_s.reshape(bs, ls, d), ret_p, conv_p, ret_s, conv_s)
```

```python
import functools

import jax
import jax.numpy as jnp
import numpy as np
from jax import lax
from jax.experimental import pallas as pl
from jax.experimental.pallas import tpu as pltpu

F32 = jnp.float32
BF16 = jnp.bfloat16
I32 = jnp.int32
U32 = jnp.uint32

D_MODEL = 1024
PAST_LEN = 16384
RET_HEADS = 4
RET_DK = 128
RET_DV = 128
RET_QK = RET_HEADS * RET_DK
RET_V = RET_HEADS * RET_DV
RET_CHUNK = 128
ROPE_BASE = 10000.0
CONV_CH = 512
CONV_WIDTH = 31
N_GROUPS = 4
EXP_PER_GROUP = 4
N_EXPERTS = N_GROUPS * EXP_PER_GROUP
TOP_K = 2
EXP_FF = 512
PLE_DIM = 256
DEPTH = 1
ALPHA = (2 * DEPTH) ** 0.25
LN_EPS = 1e-5
IN_WIDTHS = (RET_QK, RET_QK, RET_V, RET_V, CONV_CH, CONV_CH, D_MODEL, D_MODEL)
IN_OFFS = tuple(int(s) for s in np.cumsum((0,) + IN_WIDTHS))

LANES = 128
SUBLANES = 8
VMEM_LIMIT = 56 * 1024 * 1024

TL = 256
BB_SAMPLE = 16
CHUNK = SUBLANES
TILE_CHUNKS = 32
BLOCK_USED = -(-(TOP_K * TL + N_EXPERTS * (CHUNK - 1)) // (CHUNK * 16)) * 16
BLOCK_SPARE = 16
BLOCK_CHUNKS = BLOCK_USED + BLOCK_SPARE
CAP = BLOCK_CHUNKS * CHUNK
TM_FFN = TILE_CHUNKS * CHUNK
HALF = D_MODEL // 2
CONV_PAD = 32
XPAD_NEW = 32
XPAD_ROWS = 40


def _ln(x, g, b):
    mu = jnp.mean(x, axis=-1, keepdims=True)
    d = x - mu
    var = jnp.mean(d * d, axis=-1, keepdims=True)
    return d * lax.rsqrt(var + LN_EPS) * g + b


def _sigmoid(x):
    return 1.0 / (1.0 + jnp.exp(-x))


def _silu(x):
    return x * _sigmoid(x)


def _bdot(a, b):
    return jnp.dot(a.astype(BF16), b, preferred_element_type=F32)


def _rot(t, cosf, sinf):
    return t * cosf + pltpu.roll(t, RET_DK // 2, axis=1) * sinf


def _pack_bf16_pairs(x):
    bits = lax.bitcast_convert_type(x.astype(BF16).astype(F32), U32)
    half = x.shape[1] // 2
    return (bits[:, :half] >> 16) | (bits[:, half:] & jnp.uint32(0xFFFF0000))


def _unpack_bf16_pairs(pk):
    lo = lax.bitcast_convert_type(pk << 16, F32).astype(BF16)
    hi = lax.bitcast_convert_type(pk & jnp.uint32(0xFFFF0000), F32).astype(BF16)
    return lo, hi


def _lane_tile(cols, rows):
    lane = lax.broadcasted_iota(I32, (rows, LANES), 1)
    out = jnp.zeros((rows, LANES), F32)
    for i, col in enumerate(cols):
        out = jnp.where(lane == i, col, out)
    return out


def _route(logits):
    lane = lax.broadcasted_iota(I32, logits.shape, 1)
    lanef = lane.astype(F32)
    ninf = jnp.float32(-jnp.inf)
    big = jnp.float32(LANES)
    gmask = lane < N_GROUPS
    gl = jnp.where(gmask, logits, ninf)
    gmax = jnp.max(gl, axis=1, keepdims=True)
    gidx = jnp.min(jnp.where(gmask & (gl == gmax), lanef, big), axis=1, keepdims=True)
    sumexp = jnp.sum(jnp.where(gmask, jnp.exp(gl - gmax), 0.0), axis=1, keepdims=True)
    gw = 1.0 / sumexp
    lo = N_GROUPS + EXP_PER_GROUP * gidx
    emask = (lanef >= lo) & (lanef < lo + EXP_PER_GROUP)
    el = jnp.where(emask, logits, ninf)
    m1 = jnp.max(el, axis=1, keepdims=True)
    i1 = jnp.min(jnp.where(emask & (el == m1), lanef, big), axis=1, keepdims=True)
    emask2 = emask & (lanef != i1)
    el2 = jnp.where(emask2, logits, ninf)
    m2 = jnp.max(el2, axis=1, keepdims=True)
    i2 = jnp.min(jnp.where(emask2 & (el2 == m2), lanef, big), axis=1, keepdims=True)
    t = jnp.exp(m2 - m1)
    den = 1.0 + t
    return (1.0 / den) * gw, (t / den) * gw, i1 - N_GROUPS, i2 - N_GROUPS


def _post_mix_pieces(src, w, sink):
    (w_ret_o, cln_g, cln_b, w_conv_o, w_out, ln1_g, ln1_b, wr_hi, wr_lo, b_r) = w
    st = {}

    def branch_a():
        st["a"] = _bdot(_silu(src["g"]()) * src["ret"](), w_ret_o[...])

    def branch_b():
        st["b"] = _bdot(_silu(_ln(src["cout"](), cln_g[...], cln_b[...])), w_conv_o[...])

    def merge():
        mix = _sigmoid(src["gt_a"]()) * st["a"] + _sigmoid(src["gt_b"]()) * st["b"]
        h = ALPHA * src["x"]() + _bdot(mix, w_out[...])
        st["x1"] = _ln(h, ln1_g[...], ln1_b[...])

    def router():
        x1 = st["x1"]
        x1_hi = x1.astype(BF16)
        x1_lo = (x1 - x1_hi.astype(F32)).astype(BF16)
        st["logits"] = (jnp.dot(x1_hi, wr_hi[...], preferred_element_type=F32)
                        + (jnp.dot(x1_lo, wr_hi[...], preferred_element_type=F32)
                           + jnp.dot(x1_hi, wr_lo[...], preferred_element_type=F32))
                        + b_r[...])

    def route():
        st["route"] = _route(st["logits"])

    def finish():
        sink(st["x1"], *st["route"])

    return [branch_a, branch_b, merge, router, route, finish]


def _interleave(a, b):
    j = 0
    for i, piece in enumerate(a):
        piece()
        while j < len(b) and (j + 1) * len(a) <= (i + 1) * len(b):
            b[j]()
            j += 1
    for piece in b[j:]:
        piece()


def _sort_tile(x1, w1, w2, e1, e2, x1_ref, rw_ref, xs_ref, meta_ref):
    t = x1.shape[0]
    lane = lax.broadcasted_iota(I32, (t, LANES), 1).astype(F32)
    a1 = (lane == e1).astype(F32)
    a2 = (lane == e2).astype(F32)
    ri = lax.broadcasted_iota(I32, (t, t), 0)
    ci = lax.broadcasted_iota(I32, (t, t), 1)
    earlier = (ci < ri).astype(BF16)
    r1 = jnp.dot(earlier, a1.astype(BF16), preferred_element_type=F32)
    r2 = jnp.dot(earlier, a2.astype(BF16), preferred_element_type=F32)
    cnt1 = jnp.sum(a1, axis=0, keepdims=True)
    cnt = cnt1 + jnp.sum(a2, axis=0, keepdims=True)
    nch = jnp.floor((cnt + (CHUNK - 1.0)) * (1.0 / CHUNK))
    ui = lax.broadcasted_iota(I32, (LANES, LANES), 0)
    uj = lax.broadcasted_iota(I32, (LANES, LANES), 1)
    before = (ui < uj).astype(BF16)
    off = jnp.dot(jnp.broadcast_to(nch, (2 * SUBLANES, LANES)).astype(BF16), before,
                  preferred_element_type=F32)[0:1, :]
    base = off * CHUNK
    pos1 = jnp.sum(a1 * (base + r1), axis=1, keepdims=True)
    pos2 = jnp.sum(a2 * (base + cnt1 + r2), axis=1, keepdims=True)
    slot = lax.broadcasted_iota(I32, (t, CAP), 1).astype(F32)
    onehot_t = ((slot == pos1) | (slot == pos2)).astype(BF16)
    xs = lax.dot_general(onehot_t, x1.astype(BF16), (((0,), (0,)), ((), ())), preferred_element_type=F32)
    x1_ref[...] = x1
    rw_ref[...] = _lane_tile((w1, w2, pos1, pos2), t)
    xs_ref[...] = _pack_bf16_pairs(xs)
    srow = lax.broadcasted_iota(I32, (SUBLANES, LANES), 0)
    meta = jnp.where(srow == 0, cnt, jnp.where(srow == 1, off, 0.0))
    meta_ref[...] = meta.astype(I32)


GATE_COLS = {3: 0, 6: RET_V, 7: RET_V + D_MODEL}
QKV_COLS = {0: 0, 1: RET_QK, 2: 2 * RET_QK}


def _prompt_mixer_kernel(x_ref, x1s_ref, rws_ref, cos_ref, sin_ref, dec_ref, qdec_ref, kdec_ref, cdec_ref,
                         w_in, b_in, gn_g, gn_b, w_ret_o, conv_w, conv_b, cln_g, cln_b,
                         w_conv_o, w_out, ln1_g, ln1_b, wr_hi, wr_lo, b_r,
                         x1_ref, rw_ref, xs_ref, meta_ref, sret_ref, sconv_ref,
                         ubuf, ushift, qkv_scr, ax, aret, acout, agates, bx, bret, bcout, bgates,
                         *, n_tiles, tiles_per_seq):
    s = pl.program_id(0)
    li = lax.rem(s, tiles_per_seq)
    odd = lax.rem(s, 2) == 1
    outs = (x1_ref, rw_ref, xs_ref, meta_ref)
    slot_a = dict(x=ax, ret=aret, cout=acout, gates=agates)
    slot_b = dict(x=bx, ret=bret, cout=bcout, gates=bgates)
    tail_w = (w_ret_o, cln_g, cln_b, w_conv_o, w_out, ln1_g, ln1_b, wr_hi, wr_lo, b_r)

    def head(slot):
        return _prompt_head_pieces(x_ref, cos_ref, sin_ref, dec_ref, qdec_ref, kdec_ref, cdec_ref,
                                   w_in, b_in, gn_g, gn_b, conv_w, conv_b, sret_ref,
                                   ubuf, ushift, qkv_scr, slot)

    def tail(slot):
        gcols = lambda kk: slice(GATE_COLS[kk], GATE_COLS[kk] + IN_WIDTHS[kk])
        src = dict(x=lambda: slot["x"][...], ret=lambda: slot["ret"][...], cout=lambda: slot["cout"][...],
                   g=lambda: slot["gates"][:, gcols(3)], gt_a=lambda: slot["gates"][:, gcols(6)],
                   gt_b=lambda: slot["gates"][:, gcols(7)])
        return _post_mix_pieces(src, tail_w, lambda *r: _sort_tile(*r, *outs))

    @pl.when((s < n_tiles) & (li == 0))
    def _new_sequence():
        sret_ref[...] = jnp.zeros(sret_ref.shape, F32)
        ubuf[0:CONV_PAD, :] = jnp.zeros((CONV_PAD, CONV_CH), F32)

    @pl.when(s == 0)
    def _fill():
        for piece in head(slot_a):
            piece()

    @pl.when((s >= 1) & (s < n_tiles) & odd)
    def _odd():
        _interleave(head(slot_b), tail(slot_a))

    @pl.when((s >= 2) & (s < n_tiles) & jnp.logical_not(odd))
    def _even():
        _interleave(head(slot_a), tail(slot_b))

    @pl.when(s == n_tiles)
    def _drain():
        for piece in tail(slot_b if n_tiles % 2 == 0 else slot_a):
            piece()

    @pl.when(s > n_tiles)
    def _append():
        rws = rws_ref[...]
        _sort_tile(x1s_ref[...], rws[:, 0:1], rws[:, 1:2], rws[:, 2:3], rws[:, 3:4], *outs)

    @pl.when((s < n_tiles) & (li == tiles_per_seq - 1))
    def _conv_state():
        sconv_ref[0, 0] = ubuf[CONV_PAD - (CONV_WIDTH - 1):CONV_PAD, :]


def _prompt_head_pieces(x_ref, cos_ref, sin_ref, dec_ref, qdec_ref, kdec_ref, cdec_ref,
                        w_in, b_in, gn_g, gn_b, conv_w, conv_b, sret_ref,
                        ubuf, ushift, qkv_scr, slot):
    tl = x_ref.shape[1]
    st = {}

    def slab_dot(c0, c1):
        return jnp.dot(st["xb"], w_in[:, c0:c1], preferred_element_type=F32) + b_in[:, c0:c1]

    def glu():
        x = x_ref[0]
        slot["x"][...] = x
        st["xb"] = x.astype(BF16)
        u = slab_dot(IN_OFFS[4], IN_OFFS[5]) * _sigmoid(slab_dot(IN_OFFS[5], IN_OFFS[6]))
        ubuf[CONV_PAD:CONV_PAD + tl, :] = u
        nsh = ushift.shape[1]
        for s in range(1, SUBLANES):
            ushift[s - 1] = ubuf[s:s + nsh, :]

    slab = 256
    slabs = [(kk, c0) for kk in (0, 1, 2, 3, 6, 7) for c0 in range(IN_OFFS[kk], IN_OFFS[kk + 1], slab)]
    rb = 32
    nrb = tl // rb

    def conv_block(r):
        acc = jnp.zeros((rb, CONV_CH), F32) + conv_b[...]
        for j in range(CONV_WIDTH):
            off = j + (CONV_PAD - (CONV_WIDTH - 1))
            s = off % SUBLANES
            base = r * rb + off - s
            win = ubuf[base:base + rb, :] if s == 0 else ushift[s - 1, base:base + rb, :]
            acc = acc + conv_w[j:j + 1, :] * win
        slot["cout"][r * rb:(r + 1) * rb, :] = acc
        for kk, c0 in slabs[r * len(slabs) // nrb:(r + 1) * len(slabs) // nrb]:
            val = slab_dot(c0, c0 + slab)
            if kk in QKV_COLS:
                dst = QKV_COLS[kk] + c0 - IN_OFFS[kk]
                qkv_scr[:, dst:dst + slab] = val
            else:
                dst = GATE_COLS[kk] + c0 - IN_OFFS[kk]
                slot["gates"][:, dst:dst + slab] = val
        if r == nrb - 1:
            ubuf[0:CONV_PAD, :] = ubuf[tl:tl + CONV_PAD, :]

    scale = RET_DK ** -0.5

    def retention(c, h):
        rows = slice(c * RET_CHUNK, (c + 1) * RET_CHUNK)
        cols = slice(h * RET_DK, (h + 1) * RET_DK)
        hcol = lambda kk: slice(QKV_COLS[kk] + h * RET_DK, QKV_COLS[kk] + (h + 1) * RET_DK)
        cosf = cos_ref[rows, :]
        sinf = sin_ref[rows, :]
        qh = _rot(qkv_scr[rows, hcol(0)], cosf, sinf)
        kh = _rot(qkv_scr[rows, hcol(1)], cosf, sinf) * scale
        qb = qh.astype(BF16)
        kb = kh.astype(BF16)
        vb = qkv_scr[rows, hcol(2)].astype(BF16)
        s_old = sret_ref[0, 0, h]
        scores = lax.dot_general(qb, kb, (((1,), (1,)), ((), ())),
                                 preferred_element_type=F32) * dec_ref[h]
        inner = jnp.dot(scores.astype(BF16), vb, preferred_element_type=F32)
        cross = jnp.dot(qb, s_old.astype(BF16), preferred_element_type=F32) * qdec_ref[h]
        kd = (kh * kdec_ref[h]).astype(BF16)
        s_new = cdec_ref[h] * s_old + lax.dot_general(
            kd, vb, (((0,), (0,)), ((), ())), preferred_element_type=F32)
        sret_ref[0, 0, h] = s_new
        slot["ret"][rows, cols] = _ln(inner + cross, gn_g[:, cols], gn_b[:, cols])

    pieces = [glu]
    pieces += [lambda r=r: conv_block(r) for r in range(nrb)]
    pieces += [lambda c=c, h=h: retention(c, h) for c in range(tl // RET_CHUNK) for h in range(RET_HEADS)]
    return pieces


def _sample_mixer_kernel(x_ref, cos_ref, sin_ref, pdec_ref, qdec_ref, kdec_ref, cdec_ref, wsh_ref,
                         sret_in, sconv_in,
                         w_in, b_in, gn_g, gn_b, w_ret_o, conv_b, cln_g, cln_b,
                         w_conv_o, w_out, ln1_g, ln1_b, wr_hi, wr_lo, b_r,
                         x1_ref, rw_ref, sret_ref, sconv_ref,
                         ret_scr, cout_scr, xpad):
    t = x_ref.shape[0]
    ls = t // BB_SAMPLE
    x = x_ref[...]
    xb = x.astype(BF16)

    def proj(k):
        c0, c1 = IN_OFFS[k], IN_OFFS[k + 1]
        return jnp.dot(xb, w_in[:, c0:c1], preferred_element_type=F32) + b_in[:, c0:c1]

    q = proj(0)
    k = proj(1)
    v = proj(2)
    scale = RET_DK ** -0.5
    cosf = cos_ref[...]
    sinf = sin_ref[...]
    row = lax.broadcasted_iota(I32, (t, RET_DK), 0)
    pos = row % ls
    row8 = lax.broadcasted_iota(I32, (SUBLANES, RET_DK), 0)
    per_tile = SUBLANES // ls
    for h in range(RET_HEADS):
        cols = slice(h * RET_DK, (h + 1) * RET_DK)
        qh = _rot(q[:, cols], cosf, sinf)
        kh = _rot(k[:, cols], cosf, sinf) * scale
        vh = v[:, cols]
        inner = jnp.zeros((t, RET_DV), F32)
        for s in range(ls):
            ks = kh if s == 0 else pltpu.roll(kh, s, axis=0)
            vs = vh if s == 0 else pltpu.roll(vh, s, axis=0)
            dotp = jnp.sum(qh * ks, axis=1, keepdims=True) * pdec_ref[h, s]
            inner = inner + jnp.where(pos >= s, dotp, 0.0) * vs
        kd = kh * kdec_ref[h]
        for tile in range(t // SUBLANES):
            rows = slice(tile * SUBLANES, (tile + 1) * SUBLANES)
            q8 = qh[rows, :]
            kd8 = kd[rows, :]
            v8 = vh[rows, :]
            cross8 = jnp.zeros((SUBLANES, RET_DV), F32)
            for sub in range(per_tile):
                b = tile * per_tile + sub
                mine = (row8 >= sub * ls) & (row8 < (sub + 1) * ls)
                s_old = sret_in[0, b, h]
                c_b = jnp.dot(q8, s_old, preferred_element_type=F32)
                cross8 = jnp.where(mine, c_b, cross8)
                upd = lax.dot_general(jnp.where(mine, kd8, 0.0), v8, (((0,), (0,)), ((), ())),
                                      preferred_element_type=F32)
                sret_ref[0, b, h] = cdec_ref[h] * s_old + upd
            ret_scr[rows, cols] = inner[rows, :] + cross8 * qdec_ref[h, rows, :]
        ret_scr[:, cols] = _ln(ret_scr[:, cols], gn_g[:, cols], gn_b[:, cols])

    u = proj(4) * _sigmoid(proj(5))
    nstate = CONV_WIDTH - 1
    xpad[...] = jnp.zeros(xpad.shape, F32)
    xpad[:, 0:nstate, :] = sconv_in[0]
    for b in range(BB_SAMPLE):
        xpad[b, XPAD_NEW:XPAD_NEW + ls, :] = u[b * ls:(b + 1) * ls, :]
    for i in range(ls):
        res = jnp.sum(xpad[...] * wsh_ref[i][None], axis=1) + conv_b[...]
        for sl in range(CONV_CH // LANES):
            cout_scr[sl, pl.ds(i, BB_SAMPLE, stride=ls), :] = res[:, sl * LANES:(sl + 1) * LANES]
    sconv_ref[0, :, 0:nstate - ls, :] = xpad[:, ls:nstate, :]
    sconv_ref[0, :, nstate - ls:nstate, :] = xpad[:, XPAD_NEW:XPAD_NEW + ls, :]
    c_out = jnp.concatenate([cout_scr[sl] for sl in range(CONV_CH // LANES)], axis=1)

    def sink(x1, w1, w2, e1, e2):
        x1_ref[...] = x1
        rw_ref[...] = _lane_tile((w1, w2, e1, e2), t)

    src = dict(x=lambda: x, ret=lambda: ret_scr[...], cout=lambda: c_out,
               g=lambda: proj(3), gt_a=lambda: proj(6), gt_b=lambda: proj(7))
    for piece in _post_mix_pieces(
            src, (w_ret_o, cln_g, cln_b, w_conv_o, w_out, ln1_g, ln1_b, wr_hi, wr_lo, b_r), sink):
        piece()


def _ffn_kernel(te_ref, nvalid_ref, chunk_ref, xs_hbm, w_gu, w_dn, ys_hbm,
                xbuf, obuf, wgu_b, wdn_b, sem_in, sem_out):
    del xs_hbm
    i = pl.program_id(0)
    n = pl.num_programs(0)
    slot = i % 2
    nvalid = nvalid_ref[0]

    def chunk_rows(tile, c):
        return pl.ds(pl.multiple_of(chunk_ref[tile * TILE_CHUNKS + c] * CHUNK, CHUNK), CHUNK)

    def start_in(tile, s):
        for c in range(TILE_CHUNKS):
            pltpu.make_async_copy(ys_hbm.at[chunk_rows(tile, c)],
                                  xbuf.at[s, pl.ds(c * CHUNK, CHUNK)], sem_in.at[s]).start()

    def start_out(tile, s):
        for c in range(TILE_CHUNKS):
            pltpu.make_async_copy(obuf.at[s, pl.ds(c * CHUNK, CHUNK)],
                                  ys_hbm.at[chunk_rows(tile, c)], sem_out.at[s]).start()

    def wait_in(s):
        pltpu.make_async_copy(ys_hbm.at[pl.ds(0, TM_FFN)], xbuf.at[s], sem_in.at[s]).wait()

    def wait_out(s):
        pltpu.make_async_copy(obuf.at[s], ys_hbm.at[pl.ds(0, TM_FFN)], sem_out.at[s]).wait()

    @pl.when((i == 0) & (nvalid > 0))
    def _first():
        start_in(0, 0)

    @pl.when(i + 1 < nvalid)
    def _prefetch():
        start_in(i + 1, 1 - slot)

    @pl.when((i >= 2) & (i - 2 < nvalid))
    def _retire():
        wait_out(slot)

    @pl.when(i < nvalid)
    def _tile():
        wait_in(slot)
        prev = te_ref[jnp.maximum(i - 1, 0)]

        @pl.when((i == 0) | (te_ref[i] != prev))
        def _new_expert():
            wgu_b[...] = w_gu[0].astype(BF16)
            wdn_b[...] = w_dn[0].astype(BF16)

        lo, hi = _unpack_bf16_pairs(xbuf[slot])
        hcat = (jnp.dot(lo, wgu_b[0:HALF, :], preferred_element_type=F32)
                + jnp.dot(hi, wgu_b[HALF:, :], preferred_element_type=F32))
        act = _silu(hcat[:, :EXP_FF]) * hcat[:, EXP_FF:]
        obuf[slot] = _pack_bf16_pairs(_bdot(act, wdn_b[...]))
        start_out(i, slot)

    @pl.when(i == n - 1)
    def _drain():
        @pl.when((i >= 1) & (i - 1 < nvalid))
        def _():
            wait_out(1 - slot)

        @pl.when(i < nvalid)
        def _():
            wait_out(slot)


def _final_kernel(ys_ref, x1_ref, rw_ref, pp_ref, ps_ref, ln2_g, ln2_b, w_pg, b_pg, w_ple,
                  yp_ref, ys_out_ref, *, n_prompt_tiles):
    i = pl.program_id(0)
    tl = x1_ref.shape[0]
    x1 = x1_ref[...]
    rw = rw_ref[...]
    w1, w2, pos1, pos2 = rw[:, 0:1], rw[:, 1:2], rw[:, 2:3], rw[:, 3:4]
    slot = lax.broadcasted_iota(I32, (tl, CAP), 1).astype(F32)
    sel1 = (slot == pos1).astype(BF16)
    sel2 = (slot == pos2).astype(BF16)
    lo, hi = _unpack_bf16_pairs(ys_ref[...])
    pick = lambda sel: jnp.concatenate([jnp.dot(sel, lo, preferred_element_type=F32),
                                        jnp.dot(sel, hi, preferred_element_type=F32)], axis=1)
    moe = pick(sel1) * w1 + pick(sel2) * w2
    x2 = _ln(ALPHA * x1 + moe, ln2_g[...], ln2_b[...])
    gate = _sigmoid(_bdot(x2, w_pg[...]) + b_pg[...])

    @pl.when(i < n_prompt_tiles)
    def _prompt():
        yp_ref[...] = x2 + gate * _bdot(pp_ref[...], w_ple[...])

    @pl.when(i >= n_prompt_tiles)
    def _sample():
        ys_out_ref[...] = x2 + gate * _bdot(ps_ref[...], w_ple[...])


def _rope_tables(pos):
    half = RET_DK // 2
    inv_freq = ROPE_BASE ** (-jnp.arange(half, dtype=F32) / half)
    ang = pos[:, None] * inv_freq[None, :]
    cos = jnp.cos(ang)
    sin = jnp.sin(ang)
    return jnp.concatenate([cos, cos], axis=-1), jnp.concatenate([-sin, sin], axis=-1)


def _log_gamma():
    return jnp.log(1.0 - 2.0 ** (-5.0 - jnp.arange(RET_HEADS, dtype=F32)))


def _const_spec(shape):
    nd = len(shape)
    return pl.BlockSpec(shape, lambda *_: (0,) * nd, pipeline_mode=pl.Buffered(1))


def _chunk_plan(meta, n_blocks, n_ffn_tiles):
    assert n_blocks * BLOCK_SPARE >= N_EXPERTS * (TILE_CHUNKS - 1)
    m = meta.reshape(n_blocks, SUBLANES, LANES)
    cnt = m[:, 0, :N_EXPERTS]
    off = m[:, 1, :N_EXPERTS]
    nch = (cnt + (CHUNK - 1)) // CHUNK
    cum = jnp.cumsum(nch, axis=0)
    total = cum[-1]
    tiles_e = (total + TILE_CHUNKS - 1) // TILE_CHUNKS
    tile_end = jnp.cumsum(tiles_e)
    tile_ids = jnp.arange(n_ffn_tiles, dtype=I32)
    te = jnp.minimum(jnp.sum((tile_end[None, :] <= tile_ids[:, None]).astype(I32), axis=1), N_EXPERTS - 1)
    tile_start = (tile_end - tiles_e)[te]
    k = (tile_ids - tile_start)[:, None] * TILE_CHUNKS + jnp.arange(TILE_CHUNKS, dtype=I32)[None, :]
    real = k < total[te][:, None]
    cum_e = cum.T[te]
    blk = jnp.sum((cum_e[:, None, :] <= k[:, :, None]).astype(I32), axis=2)
    blk = jnp.minimum(blk, n_blocks - 1)
    excl = jnp.take_along_axis(cum_e - nch.T[te], blk, axis=1)
    off_e = jnp.take_along_axis(off.T[te], blk, axis=1)
    spare = te[:, None] * (TILE_CHUNKS - 1) + jnp.maximum(k - total[te][:, None], 0) % TILE_CHUNKS
    spare_chunk = (spare // BLOCK_SPARE) * BLOCK_CHUNKS + BLOCK_USED + spare % BLOCK_SPARE
    chunk = jnp.where(real, blk * BLOCK_CHUNKS + off_e + (k - excl), spare_chunk)
    return te.astype(I32), tile_end[-1:].astype(I32), chunk.reshape(-1).astype(I32)


def kernel(x_prompt, x_sample, state_ret, state_conv, p_prompt, p_sample, w_in, b_in, ret_gn_g, ret_gn_b,
           w_ret_o, conv_w, conv_b, conv_ln_g, conv_ln_b, w_conv_o, w_out, ln1_g, ln1_b, w_grp, b_grp,
           w_exp, b_exp, w_gu, w_dn, ln2_g, ln2_b, w_pg, b_pg, w_ple):
    assert DEPTH == 1 and w_in.shape[0] == 1
    bp, lp, d = x_prompt.shape
    bs, ls, _ = x_sample.shape
    n_p, n_s = bp * lp, bs * ls
    n_tok = n_p + n_s
    assert lp % TL == 0 and n_s % TL == 0 and bs % BB_SAMPLE == 0 and SUBLANES % ls == 0
    n_blocks = n_tok // TL

    lg = _log_gamma()
    c = RET_CHUNK
    idx = jnp.arange(c, dtype=F32)
    rel = idx[:, None] - idx[None, :]
    causal = rel >= 0
    decay = jnp.where(causal[None], jnp.exp(jnp.where(causal, rel, 0.0)[None] * lg[:, None, None]), 0.0)
    q_decay = jnp.exp((idx[:, None] + 1.0) * lg[None, :])
    k_decay = jnp.exp((c - 1.0 - idx[:, None]) * lg[None, :])
    chunk_decay = jnp.exp(c * lg)
    qdec_p = jnp.broadcast_to(q_decay.T[:, :, None], (RET_HEADS, c, RET_DK))
    kdec_p = jnp.broadcast_to(k_decay.T[:, :, None], (RET_HEADS, c, RET_DK))
    cdec_p = jnp.broadcast_to(chunk_decay[:, None, None], (RET_HEADS, 1, RET_DV))
    cos_p, sin_p = _rope_tables(jnp.arange(lp, dtype=F32))

    ts = BB_SAMPLE * ls
    idx_s = jnp.arange(ls, dtype=F32)
    pdec_s = jnp.exp(idx_s[None, :] * lg[:, None])
    pdec_s = jnp.broadcast_to(pdec_s[:, :, None, None], (RET_HEADS, ls, 1, RET_DK))
    qd_s = jnp.exp((idx_s[:, None] + 1.0) * lg[None, :])
    kd_s = jnp.exp((ls - 1.0 - idx_s[:, None]) * lg[None, :])
    qdec_s = jnp.broadcast_to(jnp.tile(qd_s.T, (1, BB_SAMPLE))[:, :, None], (RET_HEADS, ts, RET_DK))
    kdec_s = jnp.broadcast_to(jnp.tile(kd_s.T, (1, BB_SAMPLE))[:, :, None], (RET_HEADS, ts, RET_DK))
    cdec_s = jnp.broadcast_to(jnp.exp(ls * lg)[:, None, None], (RET_HEADS, 1, RET_DV))
    pos_s = PAST_LEN + jnp.arange(ls, dtype=F32)
    cos_s, sin_s = _rope_tables(jnp.tile(pos_s, BB_SAMPLE))

    w_in_b = w_in[0].astype(BF16)
    w_ret_o_b = w_ret_o[0].astype(BF16)
    w_conv_o_b = w_conv_o[0].astype(BF16)
    w_out_b = w_out[0].astype(BF16)
    w_pg_b = w_pg[0].astype(BF16)
    w_ple_b = w_ple[0].astype(BF16)
    w_r = jnp.zeros((d, LANES), F32).at[:, :N_GROUPS].set(w_grp[0]).at[:, N_GROUPS:N_GROUPS + N_EXPERTS].set(w_exp[0])
    wr_hi = w_r.astype(BF16)
    wr_lo = (w_r - wr_hi.astype(F32)).astype(BF16)
    b_r = jnp.zeros((1, LANES), F32).at[0, :N_GROUPS].set(b_grp[0]).at[0, N_GROUPS:N_GROUPS + N_EXPERTS].set(b_exp[0])
    row = lambda a: a.reshape(1, -1)
    conv_w0 = conv_w[0]
    nstate = CONV_WIDTH - 1
    win_row = np.array([m if m < nstate else XPAD_NEW + (m - nstate) for m in range(nstate + ls)])
    wsh = jnp.stack([jnp.zeros((XPAD_ROWS, CONV_CH), F32).at[win_row[i:i + CONV_WIDTH]].set(conv_w0)
                     for i in range(ls)])

    shared_w = (w_in_b, row(b_in[0]), row(ret_gn_g[0]), row(ret_gn_b[0]), w_ret_o_b)
    tail_w = (row(conv_ln_g[0]), row(conv_ln_b[0]), w_conv_o_b, w_out_b, row(ln1_g[0]), row(ln1_b[0]),
              wr_hi, wr_lo, b_r)

    nbt = bs // BB_SAMPLE
    xs2 = x_sample.reshape(n_s, d)
    sample_in = ((xs2, cos_s, sin_s, pdec_s, qdec_s, kdec_s, cdec_s, wsh, state_ret, state_conv)
                 + shared_w + (row(conv_b[0]),) + tail_w)
    sample_specs = (
        [pl.BlockSpec((ts, d), lambda i: (i, 0))]
        + [_const_spec(a.shape) for a in sample_in[1:8]]
        + [pl.BlockSpec((1, BB_SAMPLE, RET_HEADS, RET_DK, RET_DV), lambda i: (0, i, 0, 0, 0)),
           pl.BlockSpec((1, BB_SAMPLE, nstate, CONV_CH), lambda i: (0, i, 0, 0))]
        + [_const_spec(a.shape) for a in sample_in[10:]]
    )
    tok_spec_s = lambda w: pl.BlockSpec((ts, w), lambda i: (i, 0))
    x1_s, rw_s, ret_s, conv_s = pl.pallas_call(
        _sample_mixer_kernel,
        grid=(nbt,),
        in_specs=sample_specs,
        out_specs=[
            tok_spec_s(d), tok_spec_s(LANES),
            pl.BlockSpec((1, BB_SAMPLE, RET_HEADS, RET_DK, RET_DV), lambda i: (0, i, 0, 0, 0)),
            pl.BlockSpec((1, BB_SAMPLE, nstate, CONV_CH), lambda i: (0, i, 0, 0)),
        ],
        out_shape=[
            jax.ShapeDtypeStruct((n_s, d), F32),
            jax.ShapeDtypeStruct((n_s, LANES), F32),
            jax.ShapeDtypeStruct(state_ret.shape, F32),
            jax.ShapeDtypeStruct(state_conv.shape, F32),
        ],
        scratch_shapes=[
            pltpu.VMEM((ts, RET_V), F32),
            pltpu.VMEM((CONV_CH // LANES, ts, LANES), F32),
            pltpu.VMEM((BB_SAMPLE, XPAD_ROWS, CONV_CH), F32),
        ],
        compiler_params=pltpu.CompilerParams(
            dimension_semantics=("arbitrary",), vmem_limit_bytes=VMEM_LIMIT),
        name="sample_mixer",
    )(*sample_in)

    nlt = lp // TL
    npt = n_p // TL
    nst = n_s // TL
    prompt_in = ((x_prompt, x1_s, rw_s, cos_p, sin_p, decay, qdec_p, kdec_p, cdec_p)
                 + shared_w + (conv_w0, row(conv_b[0])) + tail_w)
    head_tile = lambda s: jnp.minimum(s, npt - 1)
    done_tile = lambda s: jnp.maximum(s - 1, 0)
    sample_tile = lambda s: jnp.clip(s - npt - 1, 0, nst - 1)
    sample_spec = lambda w: pl.BlockSpec((TL, w), lambda s: (sample_tile(s), 0))
    prompt_specs = [
        pl.BlockSpec((1, TL, d), lambda s: (head_tile(s) // nlt, head_tile(s) % nlt, 0)),
        sample_spec(d), sample_spec(LANES),
        pl.BlockSpec((TL, RET_DK), lambda s: (head_tile(s) % nlt, 0)),
        pl.BlockSpec((TL, RET_DK), lambda s: (head_tile(s) % nlt, 0)),
    ] + [_const_spec(a.shape) for a in prompt_in[5:]]
    tok_spec_p = lambda rows, w: pl.BlockSpec((rows, w), lambda s: (done_tile(s), 0))
    carry_slot = [pltpu.VMEM((TL, d), F32), pltpu.VMEM((TL, RET_V), F32), pltpu.VMEM((TL, CONV_CH), F32),
                  pltpu.VMEM((TL, RET_V + 2 * D_MODEL), F32)]
    x1_all, rw_all, xs_all, meta, ret_p, conv_p = pl.pallas_call(
        functools.partial(_prompt_mixer_kernel, n_tiles=npt, tiles_per_seq=nlt),
        grid=(npt + 1 + nst,),
        in_specs=prompt_specs,
        out_specs=[
            tok_spec_p(TL, d), tok_spec_p(TL, LANES), tok_spec_p(CAP, HALF), tok_spec_p(SUBLANES, LANES),
            pl.BlockSpec((1, 1, RET_HEADS, RET_DK, RET_DV), lambda s: (0, head_tile(s) // nlt, 0, 0, 0)),
            pl.BlockSpec((1, 1, nstate, CONV_CH), lambda s: (0, head_tile(s) // nlt, 0, 0)),
        ],
        out_shape=[
            jax.ShapeDtypeStruct((n_tok, d), F32),
            jax.ShapeDtypeStruct((n_tok, LANES), F32),
            jax.ShapeDtypeStruct((n_blocks * CAP, HALF), U32),
            jax.ShapeDtypeStruct((n_blocks * SUBLANES, LANES), I32),
            jax.ShapeDtypeStruct((1, bp, RET_HEADS, RET_DK, RET_DV), F32),
            jax.ShapeDtypeStruct((1, bp, nstate, CONV_CH), F32),
        ],
        scratch_shapes=[
            pltpu.VMEM((TL + CONV_PAD, CONV_CH), F32),
            pltpu.VMEM((SUBLANES - 1, TL + CONV_PAD - SUBLANES, CONV_CH), F32),
            pltpu.VMEM((TL, 2 * RET_QK + RET_V), F32),
        ] + carry_slot + carry_slot,
        compiler_params=pltpu.CompilerParams(
            dimension_semantics=("arbitrary",), vmem_limit_bytes=VMEM_LIMIT),
        name="prompt_mixer",
    )(*prompt_in)

    max_chunks = n_blocks * (TOP_K * TL // CHUNK + N_EXPERTS - 1)
    n_ffn_tiles = (max_chunks + N_EXPERTS * (TILE_CHUNKS - 1)) // TILE_CHUNKS
    tile_e, n_valid_tiles, chunk_ids = _chunk_plan(meta, n_blocks, n_ffn_tiles)

    ys_all = pl.pallas_call(
        _ffn_kernel,
        grid_spec=pltpu.PrefetchScalarGridSpec(
            num_scalar_prefetch=3,
            grid=(n_ffn_tiles,),
            in_specs=[
                pl.BlockSpec(memory_space=pl.ANY),
                pl.BlockSpec((1, d, 2 * EXP_FF), lambda i, te, nr, ch: (te[i], 0, 0)),
                pl.BlockSpec((1, EXP_FF, d), lambda i, te, nr, ch: (te[i], 0, 0)),
            ],
            out_specs=pl.BlockSpec(memory_space=pl.ANY),
            scratch_shapes=[
                pltpu.VMEM((2, TM_FFN, HALF), U32),
                pltpu.VMEM((2, TM_FFN, HALF), U32),
                pltpu.VMEM((d, 2 * EXP_FF), BF16),
                pltpu.VMEM((EXP_FF, d), BF16),
                pltpu.SemaphoreType.DMA((2,)),
                pltpu.SemaphoreType.DMA((2,)),
            ],
        ),
        out_shape=jax.ShapeDtypeStruct(xs_all.shape, U32),
        input_output_aliases={3: 0},
        compiler_params=pltpu.CompilerParams(
            dimension_semantics=("arbitrary",), vmem_limit_bytes=VMEM_LIMIT),
        name="expert_ffn",
    )(tile_e, n_valid_tiles, chunk_ids, xs_all, w_gu[0], w_dn[0])

    npt = n_p // TL
    pp2 = p_prompt.reshape(n_p, PLE_DIM)
    ps2 = p_sample.reshape(n_s, PLE_DIM)
    tok_f = lambda rows, w: pl.BlockSpec((rows, w), lambda i: (i, 0))
    y_p, y_s = pl.pallas_call(
        functools.partial(_final_kernel, n_prompt_tiles=npt),
        grid=(n_blocks,),
        in_specs=[
            tok_f(CAP, HALF), tok_f(TL, d), tok_f(TL, LANES),
            pl.BlockSpec((TL, PLE_DIM), lambda i: (jnp.minimum(i, npt - 1), 0)),
            pl.BlockSpec((TL, PLE_DIM), lambda i: (jnp.maximum(i - npt, 0), 0)),
            _const_spec((1, d)), _const_spec((1, d)), _const_spec((d, d)), _const_spec((1, d)),
            _const_spec((PLE_DIM, d)),
        ],
        out_specs=[
            pl.BlockSpec((TL, d), lambda i: (jnp.minimum(i, npt - 1), 0)),
            pl.BlockSpec((TL, d), lambda i: (jnp.maximum(i - npt, 0), 0)),
        ],
        out_shape=[jax.ShapeDtypeStruct((n_p, d), F32), jax.ShapeDtypeStruct((n_s, d), F32)],
        compiler_params=pltpu.CompilerParams(
            dimension_semantics=("arbitrary",), vmem_limit_bytes=VMEM_LIMIT),
        name="moe_combine_final",
    )(ys_all, x1_all, rw_all, pp2, ps2,
      row(ln2_g[0]), row(ln2_b[0]), w_pg_b, row(b_pg[0]), w_ple_b)

    return (y_p.reshape(bp, lp, d), y_s.reshape(bs, ls, d), ret_p, conv_p, ret_s, conv_s)
```

```python
import functools

import jax
import jax.numpy as jnp
import numpy as np
from jax import lax
from jax.experimental import pallas as pl
from jax.experimental.pallas import tpu as pltpu

F32 = jnp.float32
BF16 = jnp.bfloat16
I32 = jnp.int32
U32 = jnp.uint32

D_MODEL = 1024
PAST_LEN = 16384
RET_HEADS = 4
RET_DK = 128
RET_DV = 128
RET_QK = RET_HEADS * RET_DK
RET_V = RET_HEADS * RET_DV
RET_CHUNK = 128
ROPE_BASE = 10000.0
CONV_CH = 512
CONV_WIDTH = 31
N_GROUPS = 4
EXP_PER_GROUP = 4
N_EXPERTS = N_GROUPS * EXP_PER_GROUP
TOP_K = 2
EXP_FF = 512
PLE_DIM = 256
DEPTH = 1
ALPHA = (2 * DEPTH) ** 0.25
LN_EPS = 1e-5
IN_WIDTHS = (RET_QK, RET_QK, RET_V, RET_V, CONV_CH, CONV_CH, D_MODEL, D_MODEL)
IN_OFFS = tuple(int(s) for s in np.cumsum((0,) + IN_WIDTHS))

LANES = 128
SUBLANES = 8
VMEM_LIMIT = 56 * 1024 * 1024

TL = 256
BB_SAMPLE = 16
CHUNK = 2 * SUBLANES
TILE_CHUNKS = 16
BLOCK_USED = -(-(TOP_K * TL + N_EXPERTS * (CHUNK - 1)) // LANES) * LANES // CHUNK
BLOCK_SPARE = LANES // CHUNK
BLOCK_CHUNKS = BLOCK_USED + BLOCK_SPARE
USED_ROWS = BLOCK_USED * CHUNK
CAP = BLOCK_CHUNKS * CHUNK
TM_FFN = TILE_CHUNKS * CHUNK
FFN_COLS = 256
CONV_PAD = 32
XPAD_NEW = 32
XPAD_ROWS = 40


def _ln(x, g, b):
    mu = jnp.mean(x, axis=-1, keepdims=True)
    d = x - mu
    var = jnp.mean(d * d, axis=-1, keepdims=True)
    return d * lax.rsqrt(var + LN_EPS) * g + b


def _sigmoid(x):
    return 1.0 / (1.0 + jnp.exp(-x))


def _silu(x):
    return x * _sigmoid(x)


def _bdot(a, b):
    return jnp.dot(a.astype(BF16), b, preferred_element_type=F32)


def _rot(t, cosf, sinf):
    return t * cosf + pltpu.roll(t, RET_DK // 2, axis=1) * sinf


def _lane_tile(cols, rows):
    lane = lax.broadcasted_iota(I32, (rows, LANES), 1)
    out = jnp.zeros((rows, LANES), F32)
    for i, col in enumerate(cols):
        out = jnp.where(lane == i, col, out)
    return out


def _route(logits):
    lane = lax.broadcasted_iota(I32, logits.shape, 1)
    lanef = lane.astype(F32)
    ninf = jnp.float32(-jnp.inf)
    big = jnp.float32(LANES)
    gmask = lane < N_GROUPS
    gl = jnp.where(gmask, logits, ninf)
    gmax = jnp.max(gl, axis=1, keepdims=True)
    gidx = jnp.min(jnp.where(gmask & (gl == gmax), lanef, big), axis=1, keepdims=True)
    sumexp = jnp.sum(jnp.where(gmask, jnp.exp(gl - gmax), 0.0), axis=1, keepdims=True)
    gw = 1.0 / sumexp
    lo = N_GROUPS + EXP_PER_GROUP * gidx
    emask = (lanef >= lo) & (lanef < lo + EXP_PER_GROUP)
    el = jnp.where(emask, logits, ninf)
    m1 = jnp.max(el, axis=1, keepdims=True)
    i1 = jnp.min(jnp.where(emask & (el == m1), lanef, big), axis=1, keepdims=True)
    emask2 = emask & (lanef != i1)
    el2 = jnp.where(emask2, logits, ninf)
    m2 = jnp.max(el2, axis=1, keepdims=True)
    i2 = jnp.min(jnp.where(emask2 & (el2 == m2), lanef, big), axis=1, keepdims=True)
    t = jnp.exp(m2 - m1)
    den = 1.0 + t
    return (1.0 / den) * gw, (t / den) * gw, i1 - N_GROUPS, i2 - N_GROUPS


def _post_mix_pieces(src, w, sink):
    (w_ret_o, cln_g, cln_b, w_conv_o, w_out, ln1_g, ln1_b, wr_hi, wr_lo, b_r) = w
    st = {}

    def branch_a():
        st["a"] = _bdot(_silu(src["g"]()) * src["ret"](), w_ret_o[...])

    def branch_b():
        st["b"] = _bdot(_silu(_ln(src["cout"](), cln_g[...], cln_b[...])), w_conv_o[...])

    def merge():
        mix = _sigmoid(src["gt_a"]()) * st["a"] + _sigmoid(src["gt_b"]()) * st["b"]
        h = ALPHA * src["x"]() + _bdot(mix, w_out[...])
        st["x1"] = _ln(h, ln1_g[...], ln1_b[...])

    def router():
        x1 = st["x1"]
        x1_hi = x1.astype(BF16)
        x1_lo = (x1 - x1_hi.astype(F32)).astype(BF16)
        st["logits"] = (jnp.dot(x1_hi, wr_hi[...], preferred_element_type=F32)
                        + (jnp.dot(x1_lo, wr_hi[...], preferred_element_type=F32)
                           + jnp.dot(x1_hi, wr_lo[...], preferred_element_type=F32))
                        + b_r[...])

    def route():
        st["route"] = _route(st["logits"])

    def finish():
        sink(st["x1"], *st["route"])

    return [branch_a, branch_b, merge, router, route, finish]


def _interleave(a, b):
    j = 0
    for i, piece in enumerate(a):
        piece()
        while j < len(b) and (j + 1) * len(a) <= (i + 1) * len(b):
            b[j]()
            j += 1
    for piece in b[j:]:
        piece()


def _sort_tile(x1, w1, w2, e1, e2, x1_ref, rw_ref, xs_ref, meta_ref):
    t = x1.shape[0]
    ids_t = _lane_tile((e1, e2), t).T
    e1r, e2r = ids_t[0:1, :], ids_t[1:2, :]
    sub = lax.broadcasted_iota(I32, (LANES, t), 0).astype(F32)
    a1 = (sub == e1r).astype(F32)
    a2 = (sub == e2r).astype(F32)
    ri = lax.broadcasted_iota(I32, (t, t), 0)
    ci = lax.broadcasted_iota(I32, (t, t), 1)
    earlier = (ri < ci).astype(BF16)
    r1 = jnp.dot(a1.astype(BF16), earlier, preferred_element_type=F32)
    r2 = jnp.dot(a2.astype(BF16), earlier, preferred_element_type=F32)
    cnt1 = jnp.sum(a1, axis=1, keepdims=True)
    cnt = cnt1 + jnp.sum(a2, axis=1, keepdims=True)
    nch = jnp.floor((cnt + (CHUNK - 1.0)) * (1.0 / CHUNK))
    ui = lax.broadcasted_iota(I32, (LANES, LANES), 0)
    uj = lax.broadcasted_iota(I32, (LANES, LANES), 1)
    before = (uj < ui).astype(BF16)
    off = jnp.dot(before, jnp.broadcast_to(nch, (LANES, LANES)).astype(BF16),
                  preferred_element_type=F32)[:, 0:1]
    base = off * CHUNK
    pos1r = jnp.sum(a1 * (base + r1), axis=0, keepdims=True)
    pos2r = jnp.sum(a2 * (base + cnt1 + r2), axis=0, keepdims=True)
    slot = lax.broadcasted_iota(I32, (CAP, t), 0).astype(F32)
    onehot = ((slot == pos1r) | (slot == pos2r)).astype(BF16)
    xs = jnp.dot(onehot, x1.astype(BF16), preferred_element_type=F32)
    pos_cols = jnp.where(sub == 2.0, pos1r, jnp.where(sub == 3.0, pos2r, 0.0)).T
    lane = lax.broadcasted_iota(I32, (t, LANES), 1)
    x1_ref[...] = x1
    rw_ref[...] = jnp.where(lane == 0, w1, jnp.where(lane == 1, w2, pos_cols))
    xs_ref[...] = xs.astype(BF16)
    mlane = lax.broadcasted_iota(I32, (LANES, LANES), 1)
    meta = jnp.where(mlane == 0, cnt, jnp.where(mlane == 1, off, 0.0))
    meta_ref[...] = meta.astype(I32)


GATE_COLS = {3: 0, 6: RET_V, 7: RET_V + D_MODEL}
QKV_COLS = {0: 0, 1: RET_QK, 2: 2 * RET_QK}


def _prompt_mixer_kernel(x_ref, x1s_ref, rws_ref, cos_ref, sin_ref, dec_ref, qdec_ref, kdec_ref, cdec_ref,
                         w_in, b_in, gn_g, gn_b, w_ret_o, conv_w, conv_b, cln_g, cln_b,
                         w_conv_o, w_out, ln1_g, ln1_b, wr_hi, wr_lo, b_r,
                         x1_ref, rw_ref, xs_ref, meta_ref, sret_ref, sconv_ref,
                         ubuf, ushift, qkv_scr, ret_scr, cout_scr, gate_scr,
                         *, n_tiles, tiles_per_seq):
    s = pl.program_id(0)
    li = lax.rem(s, tiles_per_seq)
    outs = (x1_ref, rw_ref, xs_ref, meta_ref)
    slot = dict(ret=ret_scr, cout=cout_scr, gates=gate_scr)
    tail_w = (w_ret_o, cln_g, cln_b, w_conv_o, w_out, ln1_g, ln1_b, wr_hi, wr_lo, b_r)

    @pl.when((s < n_tiles) & (li == 0))
    def _new_sequence():
        sret_ref[...] = jnp.zeros(sret_ref.shape, F32)
        ubuf[0:CONV_PAD, :] = jnp.zeros((CONV_PAD, CONV_CH), F32)

    @pl.when(s < n_tiles)
    def _mix():
        gcols = lambda kk: slice(GATE_COLS[kk], GATE_COLS[kk] + IN_WIDTHS[kk])
        src = dict(x=lambda: x_ref[0], ret=lambda: ret_scr[...], cout=lambda: cout_scr[...],
                   g=lambda: gate_scr[:, gcols(3)], gt_a=lambda: gate_scr[:, gcols(6)],
                   gt_b=lambda: gate_scr[:, gcols(7)])
        head = _prompt_head_pieces(x_ref, cos_ref, sin_ref, dec_ref, qdec_ref, kdec_ref, cdec_ref,
                                   w_in, b_in, gn_g, gn_b, conv_w, conv_b, sret_ref,
                                   ubuf, ushift, qkv_scr, slot)
        tail = _post_mix_pieces(src, tail_w, lambda *r: _sort_tile(*r, *outs))
        for piece in head + tail:
            piece()

    @pl.when(s >= n_tiles)
    def _append():
        rws = rws_ref[...]
        _sort_tile(x1s_ref[...], rws[:, 0:1], rws[:, 1:2], rws[:, 2:3], rws[:, 3:4], *outs)

    @pl.when((s < n_tiles) & (li == tiles_per_seq - 1))
    def _conv_state():
        sconv_ref[0, 0] = ubuf[CONV_PAD - (CONV_WIDTH - 1):CONV_PAD, :]


def _prompt_head_pieces(x_ref, cos_ref, sin_ref, dec_ref, qdec_ref, kdec_ref, cdec_ref,
                        w_in, b_in, gn_g, gn_b, conv_w, conv_b, sret_ref,
                        ubuf, ushift, qkv_scr, slot):
    tl = x_ref.shape[1]
    st = {}

    def slab_dot(c0, c1):
        return jnp.dot(st["xb"], w_in[:, c0:c1], preferred_element_type=F32) + b_in[:, c0:c1]

    def glu():
        st["xb"] = x_ref[0].astype(BF16)
        u = slab_dot(IN_OFFS[4], IN_OFFS[5]) * _sigmoid(slab_dot(IN_OFFS[5], IN_OFFS[6]))
        ubuf[CONV_PAD:CONV_PAD + tl, :] = u
        nsh = ushift.shape[1]
        for s in range(1, SUBLANES):
            ushift[s - 1] = ubuf[s:s + nsh, :]

    slab = 256
    slabs = [(kk, c0) for kk in (0, 1, 2, 3, 6, 7) for c0 in range(IN_OFFS[kk], IN_OFFS[kk + 1], slab)]
    rb = 32
    nrb = tl // rb

    def conv_block(r):
        acc = jnp.zeros((rb, CONV_CH), F32) + conv_b[...]
        for j in range(CONV_WIDTH):
            off = j + (CONV_PAD - (CONV_WIDTH - 1))
            s = off % SUBLANES
            base = r * rb + off - s
            win = ubuf[base:base + rb, :] if s == 0 else ushift[s - 1, base:base + rb, :]
            acc = acc + conv_w[j:j + 1, :] * win
        slot["cout"][r * rb:(r + 1) * rb, :] = acc
        for kk, c0 in slabs[r * len(slabs) // nrb:(r + 1) * len(slabs) // nrb]:
            val = slab_dot(c0, c0 + slab)
            if kk in QKV_COLS:
                dst = QKV_COLS[kk] + c0 - IN_OFFS[kk]
                qkv_scr[:, dst:dst + slab] = val
            else:
                dst = GATE_COLS[kk] + c0 - IN_OFFS[kk]
                slot["gates"][:, dst:dst + slab] = val
        if r == nrb - 1:
            ubuf[0:CONV_PAD, :] = ubuf[tl:tl + CONV_PAD, :]

    scale = RET_DK ** -0.5

    def retention(c, h):
        rows = slice(c * RET_CHUNK, (c + 1) * RET_CHUNK)
        cols = slice(h * RET_DK, (h + 1) * RET_DK)
        hcol = lambda kk: slice(QKV_COLS[kk] + h * RET_DK, QKV_COLS[kk] + (h + 1) * RET_DK)
        cosf = cos_ref[rows, :]
        sinf = sin_ref[rows, :]
        qh = _rot(qkv_scr[rows, hcol(0)], cosf, sinf)
        kh = _rot(qkv_scr[rows, hcol(1)], cosf, sinf) * scale
        qb = qh.astype(BF16)
        kb = kh.astype(BF16)
        vb = qkv_scr[rows, hcol(2)].astype(BF16)
        s_old = sret_ref[0, 0, h]
        scores = lax.dot_general(qb, kb, (((1,), (1,)), ((), ())),
                                 preferred_element_type=F32) * dec_ref[h]
        inner = jnp.dot(scores.astype(BF16), vb, preferred_element_type=F32)
        cross = jnp.dot(qb, s_old.astype(BF16), preferred_element_type=F32) * qdec_ref[h]
        kd = (kh * kdec_ref[h]).astype(BF16)
        s_new = cdec_ref[h] * s_old + lax.dot_general(
            kd, vb, (((0,), (0,)), ((), ())), preferred_element_type=F32)
        sret_ref[0, 0, h] = s_new
        slot["ret"][rows, cols] = _ln(inner + cross, gn_g[:, cols], gn_b[:, cols])

    pieces = [glu]
    pieces += [lambda r=r: conv_block(r) for r in range(nrb)]
    pieces += [lambda c=c, h=h: retention(c, h) for c in range(tl // RET_CHUNK) for h in range(RET_HEADS)]
    return pieces


def _sample_mixer_kernel(x_ref, cos_ref, sin_ref, pdec_ref, qdec_ref, kdec_ref, cdec_ref, wsh_ref,
                         sret_in, sconv_in,
                         w_in, b_in, gn_g, gn_b, w_ret_o, conv_b, cln_g, cln_b,
                         w_conv_o, w_out, ln1_g, ln1_b, wr_hi, wr_lo, b_r,
                         x1_ref, rw_ref, sret_ref, sconv_ref,
                         ret_scr, cout_scr, xpad):
    t = x_ref.shape[0]
    ls = t // BB_SAMPLE
    x = x_ref[...]
    xb = x.astype(BF16)

    def proj(k):
        c0, c1 = IN_OFFS[k], IN_OFFS[k + 1]
        return jnp.dot(xb, w_in[:, c0:c1], preferred_element_type=F32) + b_in[:, c0:c1]

    q = proj(0)
    k = proj(1)
    v = proj(2)
    scale = RET_DK ** -0.5
    cosf = cos_ref[...]
    sinf = sin_ref[...]
    row = lax.broadcasted_iota(I32, (t, RET_DK), 0)
    pos = row % ls
    row8 = lax.broadcasted_iota(I32, (SUBLANES, RET_DK), 0)
    per_tile = SUBLANES // ls
    for h in range(RET_HEADS):
        cols = slice(h * RET_DK, (h + 1) * RET_DK)
        qh = _rot(q[:, cols], cosf, sinf)
        kh = _rot(k[:, cols], cosf, sinf) * scale
        vh = v[:, cols]
        inner = jnp.zeros((t, RET_DV), F32)
        for s in range(ls):
            ks = kh if s == 0 else pltpu.roll(kh, s, axis=0)
            vs = vh if s == 0 else pltpu.roll(vh, s, axis=0)
            dotp = jnp.sum(qh * ks, axis=1, keepdims=True) * pdec_ref[h, s]
            inner = inner + jnp.where(pos >= s, dotp, 0.0) * vs
        kd = kh * kdec_ref[h]
        for tile in range(t // SUBLANES):
            rows = slice(tile * SUBLANES, (tile + 1) * SUBLANES)
            q8 = qh[rows, :]
            kd8 = kd[rows, :]
            v8 = vh[rows, :]
            cross8 = jnp.zeros((SUBLANES, RET_DV), F32)
            for sub in range(per_tile):
                b = tile * per_tile + sub
                mine = (row8 >= sub * ls) & (row8 < (sub + 1) * ls)
                s_old = sret_in[0, b, h]
                c_b = jnp.dot(q8, s_old, preferred_element_type=F32)
                cross8 = jnp.where(mine, c_b, cross8)
                upd = lax.dot_general(jnp.where(mine, kd8, 0.0), v8, (((0,), (0,)), ((), ())),
                                      preferred_element_type=F32)
                sret_ref[0, b, h] = cdec_ref[h] * s_old + upd
            ret_scr[rows, cols] = inner[rows, :] + cross8 * qdec_ref[h, rows, :]
        ret_scr[:, cols] = _ln(ret_scr[:, cols], gn_g[:, cols], gn_b[:, cols])

    u = proj(4) * _sigmoid(proj(5))
    nstate = CONV_WIDTH - 1
    xpad[...] = jnp.zeros(xpad.shape, F32)
    xpad[:, 0:nstate, :] = sconv_in[0]
    for b in range(BB_SAMPLE):
        xpad[b, XPAD_NEW:XPAD_NEW + ls, :] = u[b * ls:(b + 1) * ls, :]
    for i in range(ls):
        res = jnp.sum(xpad[...] * wsh_ref[i][None], axis=1) + conv_b[...]
        for sl in range(CONV_CH // LANES):
            cout_scr[sl, pl.ds(i, BB_SAMPLE, stride=ls), :] = res[:, sl * LANES:(sl + 1) * LANES]
    sconv_ref[0, :, 0:nstate - ls, :] = xpad[:, ls:nstate, :]
    sconv_ref[0, :, nstate - ls:nstate, :] = xpad[:, XPAD_NEW:XPAD_NEW + ls, :]
    c_out = jnp.concatenate([cout_scr[sl] for sl in range(CONV_CH // LANES)], axis=1)

    def sink(x1, w1, w2, e1, e2):
        x1_ref[...] = x1
        rw_ref[...] = _lane_tile((w1, w2, e1, e2), t)

    src = dict(x=lambda: x, ret=lambda: ret_scr[...], cout=lambda: c_out,
               g=lambda: proj(3), gt_a=lambda: proj(6), gt_b=lambda: proj(7))
    for piece in _post_mix_pieces(
            src, (w_ret_o, cln_g, cln_b, w_conv_o, w_out, ln1_g, ln1_b, wr_hi, wr_lo, b_r), sink):
        piece()


def _ffn_kernel(te_ref, nvalid_ref, chunk_ref, xs_hbm, w_gu, w_dn, ys_hbm,
                xbuf, obuf, wgu_b, wdn_b, sem_in, sem_out):
    del xs_hbm
    i = pl.program_id(0)
    n = pl.num_programs(0)
    slot = i % 2
    nvalid = nvalid_ref[0]

    def chunk_rows(tile, c):
        return pl.ds(pl.multiple_of(chunk_ref[tile * TILE_CHUNKS + c] * CHUNK, CHUNK), CHUNK)

    def start_in(tile, s):
        for c in range(TILE_CHUNKS):
            pltpu.make_async_copy(ys_hbm.at[chunk_rows(tile, c)],
                                  xbuf.at[s, pl.ds(c * CHUNK, CHUNK)], sem_in.at[s]).start()

    def start_out(tile, s):
        for c in range(TILE_CHUNKS):
            pltpu.make_async_copy(obuf.at[s, pl.ds(c * CHUNK, CHUNK)],
                                  ys_hbm.at[chunk_rows(tile, c)], sem_out.at[s]).start()

    def wait_in(s):
        pltpu.make_async_copy(ys_hbm.at[pl.ds(0, TM_FFN)], xbuf.at[s], sem_in.at[s]).wait()

    def wait_out(s):
        pltpu.make_async_copy(obuf.at[s], ys_hbm.at[pl.ds(0, TM_FFN)], sem_out.at[s]).wait()

    @pl.when((i == 0) & (nvalid > 0))
    def _first():
        start_in(0, 0)

    @pl.when(i + 1 < nvalid)
    def _prefetch():
        start_in(i + 1, 1 - slot)

    @pl.when((i >= 2) & (i - 2 < nvalid))
    def _retire():
        wait_out(slot)

    @pl.when(i < nvalid)
    def _tile():
        wait_in(slot)
        prev = te_ref[jnp.maximum(i - 1, 0)]

        @pl.when((i == 0) | (te_ref[i] != prev))
        def _new_expert():
            wgu_b[...] = w_gu[0].astype(BF16)
            wdn_b[...] = w_dn[0].astype(BF16)

        x = xbuf[slot]
        y = jnp.zeros((TM_FFN, w_dn.shape[2]), F32)
        for c0 in range(0, EXP_FF, FFN_COLS):
            hg = jnp.dot(x, wgu_b[:, c0:c0 + FFN_COLS], preferred_element_type=F32)
            hu = jnp.dot(x, wgu_b[:, EXP_FF + c0:EXP_FF + c0 + FFN_COLS], preferred_element_type=F32)
            y = y + _bdot(_silu(hg) * hu, wdn_b[c0:c0 + FFN_COLS, :])
        obuf[slot] = y.astype(BF16)
        start_out(i, slot)

    @pl.when(i == n - 1)
    def _drain():
        @pl.when((i >= 1) & (i - 1 < nvalid))
        def _():
            wait_out(1 - slot)

        @pl.when(i < nvalid)
        def _():
            wait_out(slot)


def _final_kernel(ys_ref, x1_ref, rw_ref, pp_ref, ps_ref, ln2_g, ln2_b, w_pg, b_pg, w_ple,
                  yp_ref, ys_out_ref, *, n_prompt_tiles):
    i = pl.program_id(0)
    tl = x1_ref.shape[0]
    x1 = x1_ref[...]
    rw = rw_ref[...]
    w1, w2, pos1, pos2 = rw[:, 0:1], rw[:, 1:2], rw[:, 2:3], rw[:, 3:4]
    slot = lax.broadcasted_iota(I32, (tl, USED_ROWS), 1).astype(F32)
    ys = ys_ref[0:USED_ROWS, :]
    pick = lambda pos: jnp.dot((slot == pos).astype(BF16), ys, preferred_element_type=F32)
    moe = pick(pos1) * w1 + pick(pos2) * w2
    x2 = _ln(ALPHA * x1 + moe, ln2_g[...], ln2_b[...])
    gate = _sigmoid(_bdot(x2, w_pg[...]) + b_pg[...])
    p = jnp.where(i < n_prompt_tiles, pp_ref[...], ps_ref[...])
    y = x2 + gate * _bdot(p, w_ple[...])

    @pl.when(i < n_prompt_tiles)
    def _prompt():
        yp_ref[...] = y

    @pl.when(i >= n_prompt_tiles)
    def _sample():
        ys_out_ref[...] = y


def _rope_tables(pos):
    half = RET_DK // 2
    inv_freq = ROPE_BASE ** (-np.arange(half, dtype=np.float64) / half)
    ang = np.asarray(pos, np.float64)[:, None] * inv_freq[None, :]
    cos = np.cos(ang)
    sin = np.sin(ang)
    return (np.concatenate([cos, cos], axis=-1).astype(np.float32),
            np.concatenate([-sin, sin], axis=-1).astype(np.float32))


def _log_gamma():
    return np.log(1.0 - 2.0 ** (-5.0 - np.arange(RET_HEADS, dtype=np.float64)))


def _const_spec(shape):
    nd = len(shape)
    return pl.BlockSpec(shape, lambda *_: (0,) * nd, pipeline_mode=pl.Buffered(1))


def _chunk_plan(meta, n_blocks, n_ffn_tiles):
    assert n_blocks * BLOCK_SPARE >= N_EXPERTS * (TILE_CHUNKS - 1)
    m = meta.reshape(n_blocks, LANES, LANES)
    cnt = m[:, :N_EXPERTS, 0]
    off = m[:, :N_EXPERTS, 1]
    nch = (cnt + (CHUNK - 1)) // CHUNK
    cum = jnp.cumsum(nch, axis=0)
    total = cum[-1]
    tiles_e = (total + TILE_CHUNKS - 1) // TILE_CHUNKS
    tile_end = jnp.cumsum(tiles_e)
    tile_ids = jnp.arange(n_ffn_tiles, dtype=I32)
    te = jnp.minimum(jnp.sum((tile_end[None, :] <= tile_ids[:, None]).astype(I32), axis=1), N_EXPERTS - 1)
    tile_start = (tile_end - tiles_e)[te]
    k = (tile_ids - tile_start)[:, None] * TILE_CHUNKS + jnp.arange(TILE_CHUNKS, dtype=I32)[None, :]
    real = k < total[te][:, None]
    cum_e = cum.T[te]
    blk = jnp.sum((cum_e[:, None, :] <= k[:, :, None]).astype(I32), axis=2)
    blk = jnp.minimum(blk, n_blocks - 1)
    excl = jnp.take_along_axis(cum_e - nch.T[te], blk, axis=1)
    off_e = jnp.take_along_axis(off.T[te], blk, axis=1)
    spare = te[:, None] * (TILE_CHUNKS - 1) + jnp.maximum(k - total[te][:, None], 0) % TILE_CHUNKS
    spare_chunk = (spare // BLOCK_SPARE) * BLOCK_CHUNKS + BLOCK_USED + spare % BLOCK_SPARE
    chunk = jnp.where(real, blk * BLOCK_CHUNKS + off_e + (k - excl), spare_chunk)
    return te.astype(I32), tile_end[-1:].astype(I32), chunk.reshape(-1).astype(I32)


def kernel(x_prompt, x_sample, state_ret, state_conv, p_prompt, p_sample, w_in, b_in, ret_gn_g, ret_gn_b,
           w_ret_o, conv_w, conv_b, conv_ln_g, conv_ln_b, w_conv_o, w_out, ln1_g, ln1_b, w_grp, b_grp,
           w_exp, b_exp, w_gu, w_dn, ln2_g, ln2_b, w_pg, b_pg, w_ple):
    assert DEPTH == 1 and w_in.shape[0] == 1
    bp, lp, d = x_prompt.shape
    bs, ls, _ = x_sample.shape
    n_p, n_s = bp * lp, bs * ls
    n_tok = n_p + n_s
    assert lp % TL == 0 and n_s % TL == 0 and bs % BB_SAMPLE == 0 and SUBLANES % ls == 0
    n_blocks = n_tok // TL

    f32c = lambda a, shape: jnp.asarray(np.broadcast_to(a, shape).astype(np.float32))
    lg = _log_gamma()
    c = RET_CHUNK
    idx = np.arange(c, dtype=np.float64)
    rel = idx[:, None] - idx[None, :]
    causal = rel >= 0
    decay = np.where(causal[None], np.exp(np.where(causal, rel, 0.0)[None] * lg[:, None, None]), 0.0)
    decay = f32c(decay, decay.shape)
    q_decay = np.exp((idx[:, None] + 1.0) * lg[None, :])
    k_decay = np.exp((c - 1.0 - idx[:, None]) * lg[None, :])
    chunk_decay = np.exp(c * lg)
    qdec_p = f32c(q_decay.T[:, :, None], (RET_HEADS, c, RET_DK))
    kdec_p = f32c(k_decay.T[:, :, None], (RET_HEADS, c, RET_DK))
    cdec_p = f32c(chunk_decay[:, None, None], (RET_HEADS, 1, RET_DV))
    cos_p, sin_p = (jnp.asarray(a) for a in _rope_tables(np.arange(lp)))

    ts = BB_SAMPLE * ls
    idx_s = np.arange(ls, dtype=np.float64)
    pdec_s = np.exp(idx_s[None, :] * lg[:, None])
    pdec_s = f32c(pdec_s[:, :, None, None], (RET_HEADS, ls, 1, RET_DK))
    qd_s = np.exp((idx_s[:, None] + 1.0) * lg[None, :])
    kd_s = np.exp((ls - 1.0 - idx_s[:, None]) * lg[None, :])
    qdec_s = f32c(np.tile(qd_s.T, (1, BB_SAMPLE))[:, :, None], (RET_HEADS, ts, RET_DK))
    kdec_s = f32c(np.tile(kd_s.T, (1, BB_SAMPLE))[:, :, None], (RET_HEADS, ts, RET_DK))
    cdec_s = f32c(np.exp(ls * lg)[:, None, None], (RET_HEADS, 1, RET_DV))
    cos_s, sin_s = (jnp.asarray(a) for a in _rope_tables(np.tile(PAST_LEN + np.arange(ls), BB_SAMPLE)))

    w_in_b = w_in[0].astype(BF16)
    w_ret_o_b = w_ret_o[0].astype(BF16)
    w_conv_o_b = w_conv_o[0].astype(BF16)
    w_out_b = w_out[0].astype(BF16)
    w_pg_b = w_pg[0].astype(BF16)
    w_ple_b = w_ple[0].astype(BF16)
    w_r = jnp.zeros((d, LANES), F32).at[:, :N_GROUPS].set(w_grp[0]).at[:, N_GROUPS:N_GROUPS + N_EXPERTS].set(w_exp[0])
    wr_hi = w_r.astype(BF16)
    wr_lo = (w_r - wr_hi.astype(F32)).astype(BF16)
    b_r = jnp.zeros((1, LANES), F32).at[0, :N_GROUPS].set(b_grp[0]).at[0, N_GROUPS:N_GROUPS + N_EXPERTS].set(b_exp[0])
    row = lambda a: a.reshape(1, -1)
    conv_w0 = conv_w[0]
    nstate = CONV_WIDTH - 1
    win_row = np.array([m if m < nstate else XPAD_NEW + (m - nstate) for m in range(nstate + ls)])
    wsh = jnp.stack([jnp.zeros((XPAD_ROWS, CONV_CH), F32).at[win_row[i:i + CONV_WIDTH]].set(conv_w0)
                     for i in range(ls)])

    shared_w = (w_in_b, row(b_in[0]), row(ret_gn_g[0]), row(ret_gn_b[0]), w_ret_o_b)
    tail_w = (row(conv_ln_g[0]), row(conv_ln_b[0]), w_conv_o_b, w_out_b, row(ln1_g[0]), row(ln1_b[0]),
              wr_hi, wr_lo, b_r)

    nbt = bs // BB_SAMPLE
    xs2 = x_sample.reshape(n_s, d)
    sample_in = ((xs2, cos_s, sin_s, pdec_s, qdec_s, kdec_s, cdec_s, wsh, state_ret, state_conv)
                 + shared_w + (row(conv_b[0]),) + tail_w)
    sample_specs = (
        [pl.BlockSpec((ts, d), lambda i: (i, 0))]
        + [_const_spec(a.shape) for a in sample_in[1:8]]
        + [pl.BlockSpec((1, BB_SAMPLE, RET_HEADS, RET_DK, RET_DV), lambda i: (0, i, 0, 0, 0)),
           pl.BlockSpec((1, BB_SAMPLE, nstate, CONV_CH), lambda i: (0, i, 0, 0))]
        + [_const_spec(a.shape) for a in sample_in[10:]]
    )
    tok_spec_s = lambda w: pl.BlockSpec((ts, w), lambda i: (i, 0))
    x1_s, rw_s, ret_s, conv_s = pl.pallas_call(
        _sample_mixer_kernel,
        grid=(nbt,),
        in_specs=sample_specs,
        out_specs=[
            tok_spec_s(d), tok_spec_s(LANES),
            pl.BlockSpec((1, BB_SAMPLE, RET_HEADS, RET_DK, RET_DV), lambda i: (0, i, 0, 0, 0)),
            pl.BlockSpec((1, BB_SAMPLE, nstate, CONV_CH), lambda i: (0, i, 0, 0)),
        ],
        out_shape=[
            jax.ShapeDtypeStruct((n_s, d), F32),
            jax.ShapeDtypeStruct((n_s, LANES), F32),
            jax.ShapeDtypeStruct(state_ret.shape, F32),
            jax.ShapeDtypeStruct(state_conv.shape, F32),
        ],
        scratch_shapes=[
            pltpu.VMEM((ts, RET_V), F32),
            pltpu.VMEM((CONV_CH // LANES, ts, LANES), F32),
            pltpu.VMEM((BB_SAMPLE, XPAD_ROWS, CONV_CH), F32),
        ],
        compiler_params=pltpu.CompilerParams(
            dimension_semantics=("arbitrary",), vmem_limit_bytes=VMEM_LIMIT),
        name="sample_mixer",
    )(*sample_in)

    nlt = lp // TL
    npt = n_p // TL
    nst = n_s // TL
    prompt_in = ((x_prompt, x1_s, rw_s, cos_p, sin_p, decay, qdec_p, kdec_p, cdec_p)
                 + shared_w + (conv_w0, row(conv_b[0])) + tail_w)
    head_tile = lambda s: jnp.minimum(s, npt - 1)
    sample_tile = lambda s: jnp.maximum(s - npt, 0)
    sample_spec = lambda w: pl.BlockSpec((TL, w), lambda s: (sample_tile(s), 0))
    prompt_specs = [
        pl.BlockSpec((1, TL, d), lambda s: (head_tile(s) // nlt, head_tile(s) % nlt, 0)),
        sample_spec(d), sample_spec(LANES),
        pl.BlockSpec((TL, RET_DK), lambda s: (head_tile(s) % nlt, 0)),
        pl.BlockSpec((TL, RET_DK), lambda s: (head_tile(s) % nlt, 0)),
    ] + [_const_spec(a.shape) for a in prompt_in[5:]]
    tok_spec_p = lambda rows, w: pl.BlockSpec((rows, w), lambda s: (s, 0))
    x1_all, rw_all, xs_all, meta, ret_p, conv_p = pl.pallas_call(
        functools.partial(_prompt_mixer_kernel, n_tiles=npt, tiles_per_seq=nlt),
        grid=(npt + nst,),
        in_specs=prompt_specs,
        out_specs=[
            tok_spec_p(TL, d), tok_spec_p(TL, LANES), tok_spec_p(CAP, d), tok_spec_p(LANES, LANES),
            pl.BlockSpec((1, 1, RET_HEADS, RET_DK, RET_DV), lambda s: (0, head_tile(s) // nlt, 0, 0, 0)),
            pl.BlockSpec((1, 1, nstate, CONV_CH), lambda s: (0, head_tile(s) // nlt, 0, 0)),
        ],
        out_shape=[
            jax.ShapeDtypeStruct((n_tok, d), F32),
            jax.ShapeDtypeStruct((n_tok, LANES), F32),
            jax.ShapeDtypeStruct((n_blocks * CAP, d), BF16),
            jax.ShapeDtypeStruct((n_blocks * LANES, LANES), I32),
            jax.ShapeDtypeStruct((1, bp, RET_HEADS, RET_DK, RET_DV), F32),
            jax.ShapeDtypeStruct((1, bp, nstate, CONV_CH), F32),
        ],
        scratch_shapes=[
            pltpu.VMEM((TL + CONV_PAD, CONV_CH), F32),
            pltpu.VMEM((SUBLANES - 1, TL + CONV_PAD - SUBLANES, CONV_CH), F32),
            pltpu.VMEM((TL, 2 * RET_QK + RET_V), F32),
            pltpu.VMEM((TL, RET_V), F32),
            pltpu.VMEM((TL, CONV_CH), F32),
            pltpu.VMEM((TL, RET_V + 2 * D_MODEL), F32),
        ],
        compiler_params=pltpu.CompilerParams(
            dimension_semantics=("arbitrary",), vmem_limit_bytes=VMEM_LIMIT),
        name="prompt_mixer",
    )(*prompt_in)

    max_chunks = n_blocks * (TOP_K * TL // CHUNK + N_EXPERTS - 1)
    n_ffn_tiles = (max_chunks + N_EXPERTS * (TILE_CHUNKS - 1)) // TILE_CHUNKS
    tile_e, n_valid_tiles, chunk_ids = _chunk_plan(meta, n_blocks, n_ffn_tiles)

    ys_all = pl.pallas_call(
        _ffn_kernel,
        grid_spec=pltpu.PrefetchScalarGridSpec(
            num_scalar_prefetch=3,
            grid=(n_ffn_tiles,),
            in_specs=[
                pl.BlockSpec(memory_space=pl.ANY),
                pl.BlockSpec((1, d, 2 * EXP_FF), lambda i, te, nr, ch: (te[i], 0, 0)),
                pl.BlockSpec((1, EXP_FF, d), lambda i, te, nr, ch: (te[i], 0, 0)),
            ],
            out_specs=pl.BlockSpec(memory_space=pl.ANY),
            scratch_shapes=[
                pltpu.VMEM((2, TM_FFN, d), BF16),
                pltpu.VMEM((2, TM_FFN, d), BF16),
                pltpu.VMEM((d, 2 * EXP_FF), BF16),
                pltpu.VMEM((EXP_FF, d), BF16),
                pltpu.SemaphoreType.DMA((2,)),
                pltpu.SemaphoreType.DMA((2,)),
            ],
        ),
        out_shape=jax.ShapeDtypeStruct(xs_all.shape, BF16),
        input_output_aliases={3: 0},
        compiler_params=pltpu.CompilerParams(
            dimension_semantics=("arbitrary",), vmem_limit_bytes=VMEM_LIMIT),
        name="expert_ffn",
    )(tile_e, n_valid_tiles, chunk_ids, xs_all, w_gu[0], w_dn[0])

    npt = n_p // TL
    pp2 = p_prompt.reshape(n_p, PLE_DIM)
    ps2 = p_sample.reshape(n_s, PLE_DIM)
    tok_f = lambda rows, w: pl.BlockSpec((rows, w), lambda i: (i, 0))
    y_p, y_s = pl.pallas_call(
        functools.partial(_final_kernel, n_prompt_tiles=npt),
        grid=(n_blocks,),
        in_specs=[
            tok_f(CAP, d), tok_f(TL, d), tok_f(TL, LANES),
            pl.BlockSpec((TL, PLE_DIM), lambda i: (jnp.minimum(i, npt - 1), 0)),
            pl.BlockSpec((TL, PLE_DIM), lambda i: (jnp.maximum(i - npt, 0), 0)),
            _const_spec((1, d)), _const_spec((1, d)), _const_spec((d, d)), _const_spec((1, d)),
            _const_spec((PLE_DIM, d)),
        ],
        out_specs=[
            pl.BlockSpec((TL, d), lambda i: (jnp.minimum(i, npt - 1), 0)),
            pl.BlockSpec((TL, d), lambda i: (jnp.maximum(i - npt, 0), 0)),
        ],
        out_shape=[jax.ShapeDtypeStruct((n_p, d), F32), jax.ShapeDtypeStruct((n_s, d), F32)],
        compiler_params=pltpu.CompilerParams(
            dimension_semantics=("arbitrary",), vmem_limit_bytes=VMEM_LIMIT),
        name="moe_combine_final",
    )(ys_all, x1_all, rw_all, pp2, ps2,
      row(ln2_g[0]), row(ln2_b[0]), w_pg_b, row(b_pg[0]), w_ple_b)

    return (y_p.reshape(bp, lp, d), y_s.reshape(bs, ls, d), ret_p, conv_p, ret_s, conv_s)
```

```python
import functools

import jax
import jax.numpy as jnp
import numpy as np
from jax import lax
from jax.experimental import pallas as pl
from jax.experimental.pallas import tpu as pltpu

F32 = jnp.float32
BF16 = jnp.bfloat16
I32 = jnp.int32
U32 = jnp.uint32

D_MODEL = 1024
PAST_LEN = 16384
RET_HEADS = 4
RET_DK = 128
RET_DV = 128
RET_QK = RET_HEADS * RET_DK
RET_V = RET_HEADS * RET_DV
RET_CHUNK = 128
ROPE_BASE = 10000.0
CONV_CH = 512
CONV_WIDTH = 31
N_GROUPS = 4
EXP_PER_GROUP = 4
N_EXPERTS = N_GROUPS * EXP_PER_GROUP
TOP_K = 2
EXP_FF = 512
PLE_DIM = 256
DEPTH = 1
ALPHA = (2 * DEPTH) ** 0.25
LN_EPS = 1e-5
IN_WIDTHS = (RET_QK, RET_QK, RET_V, RET_V, CONV_CH, CONV_CH, D_MODEL, D_MODEL)
IN_OFFS = tuple(int(s) for s in np.cumsum((0,) + IN_WIDTHS))

LANES = 128
SUBLANES = 8
VMEM_LIMIT = 56 * 1024 * 1024

TL = 256
BB_SAMPLE = 16
CHUNK = 2 * SUBLANES
TILE_CHUNKS = 32
BLOCK_USED = -(-(TOP_K * TL + N_EXPERTS * (CHUNK - 1)) // LANES) * LANES // CHUNK
BLOCK_SPARE = LANES // CHUNK
BLOCK_CHUNKS = BLOCK_USED + BLOCK_SPARE
USED_ROWS = BLOCK_USED * CHUNK
CAP = BLOCK_CHUNKS * CHUNK
TM_FFN = TILE_CHUNKS * CHUNK
FFN_COLS = 256
CONV_PAD = 32
XPAD_NEW = 32
XPAD_ROWS = 40


def _ln(x, g, b):
    mu = jnp.mean(x, axis=-1, keepdims=True)
    d = x - mu
    var = jnp.mean(d * d, axis=-1, keepdims=True)
    return d * lax.rsqrt(var + LN_EPS) * g + b


def _sigmoid(x):
    return 1.0 / (1.0 + jnp.exp(-x))


def _silu(x):
    return x * _sigmoid(x)


def _bdot(a, b):
    return jnp.dot(a.astype(BF16), b, preferred_element_type=F32)


def _rot(t, cosf, sinf):
    return t * cosf + pltpu.roll(t, RET_DK // 2, axis=1) * sinf


def _lane_tile(cols, rows):
    lane = lax.broadcasted_iota(I32, (rows, LANES), 1)
    out = jnp.zeros((rows, LANES), F32)
    for i, col in enumerate(cols):
        out = jnp.where(lane == i, col, out)
    return out


def _route(logits):
    lane = lax.broadcasted_iota(I32, logits.shape, 1)
    lanef = lane.astype(F32)
    ninf = jnp.float32(-jnp.inf)
    big = jnp.float32(LANES)
    gmask = lane < N_GROUPS
    gl = jnp.where(gmask, logits, ninf)
    gmax = jnp.max(gl, axis=1, keepdims=True)
    gidx = jnp.min(jnp.where(gmask & (gl == gmax), lanef, big), axis=1, keepdims=True)
    sumexp = jnp.sum(jnp.where(gmask, jnp.exp(gl - gmax), 0.0), axis=1, keepdims=True)
    gw = 1.0 / sumexp
    lo = N_GROUPS + EXP_PER_GROUP * gidx
    emask = (lanef >= lo) & (lanef < lo + EXP_PER_GROUP)
    el = jnp.where(emask, logits, ninf)
    m1 = jnp.max(el, axis=1, keepdims=True)
    i1 = jnp.min(jnp.where(emask & (el == m1), lanef, big), axis=1, keepdims=True)
    emask2 = emask & (lanef != i1)
    el2 = jnp.where(emask2, logits, ninf)
    m2 = jnp.max(el2, axis=1, keepdims=True)
    i2 = jnp.min(jnp.where(emask2 & (el2 == m2), lanef, big), axis=1, keepdims=True)
    t = jnp.exp(m2 - m1)
    den = 1.0 + t
    return (1.0 / den) * gw, (t / den) * gw, i1 - N_GROUPS, i2 - N_GROUPS


def _post_mix_pieces(src, w, sink):
    (w_ret_o, cln_g, cln_b, w_conv_o, w_out, ln1_g, ln1_b, wr_hi, wr_lo, b_r) = w
    st = {}

    def branch_a():
        st["a"] = _bdot(_silu(src["g"]()) * src["ret"](), w_ret_o[...])

    def branch_b():
        st["b"] = _bdot(_silu(_ln(src["cout"](), cln_g[...], cln_b[...])), w_conv_o[...])

    def merge():
        mix = _sigmoid(src["gt_a"]()) * st["a"] + _sigmoid(src["gt_b"]()) * st["b"]
        h = ALPHA * src["x"]() + _bdot(mix, w_out[...])
        st["x1"] = _ln(h, ln1_g[...], ln1_b[...])

    def router():
        x1 = st["x1"]
        x1_hi = x1.astype(BF16)
        x1_lo = (x1 - x1_hi.astype(F32)).astype(BF16)
        st["logits"] = (jnp.dot(x1_hi, wr_hi[...], preferred_element_type=F32)
                        + (jnp.dot(x1_lo, wr_hi[...], preferred_element_type=F32)
                           + jnp.dot(x1_hi, wr_lo[...], preferred_element_type=F32))
                        + b_r[...])

    def route():
        st["route"] = _route(st["logits"])

    def finish():
        sink(st["x1"], *st["route"])

    return [branch_a, branch_b, merge, router, route, finish]


def _interleave(a, b):
    j = 0
    for i, piece in enumerate(a):
        piece()
        while j < len(b) and (j + 1) * len(a) <= (i + 1) * len(b):
            b[j]()
            j += 1
    for piece in b[j:]:
        piece()


def _sort_tile(x1, w1, w2, e1, e2, x1_ref, rw_ref, xs_ref, meta_ref):
    t = x1.shape[0]
    ids_t = _lane_tile((e1, e2), t).T
    e1r, e2r = ids_t[0:1, :], ids_t[1:2, :]
    sub = lax.broadcasted_iota(I32, (LANES, t), 0).astype(F32)
    a1 = (sub == e1r).astype(F32)
    a2 = (sub == e2r).astype(F32)
    ri = lax.broadcasted_iota(I32, (t, t), 0)
    ci = lax.broadcasted_iota(I32, (t, t), 1)
    earlier = (ri < ci).astype(BF16)
    r1 = jnp.dot(a1.astype(BF16), earlier, preferred_element_type=F32)
    r2 = jnp.dot(a2.astype(BF16), earlier, preferred_element_type=F32)
    cnt1 = jnp.sum(a1, axis=1, keepdims=True)
    cnt = cnt1 + jnp.sum(a2, axis=1, keepdims=True)
    nch = jnp.floor((cnt + (CHUNK - 1.0)) * (1.0 / CHUNK))
    ui = lax.broadcasted_iota(I32, (LANES, LANES), 0)
    uj = lax.broadcasted_iota(I32, (LANES, LANES), 1)
    before = (uj < ui).astype(BF16)
    off = jnp.dot(before, jnp.broadcast_to(nch, (LANES, LANES)).astype(BF16),
                  preferred_element_type=F32)[:, 0:1]
    base = off * CHUNK
    pos1r = jnp.sum(a1 * (base + r1), axis=0, keepdims=True)
    pos2r = jnp.sum(a2 * (base + cnt1 + r2), axis=0, keepdims=True)
    slot = lax.broadcasted_iota(I32, (CAP, t), 0).astype(F32)
    onehot = ((slot == pos1r) | (slot == pos2r)).astype(BF16)
    xs = jnp.dot(onehot, x1.astype(BF16), preferred_element_type=F32)
    pos_cols = jnp.where(sub == 2.0, pos1r, jnp.where(sub == 3.0, pos2r, 0.0)).T
    lane = lax.broadcasted_iota(I32, (t, LANES), 1)
    x1_ref[...] = x1
    rw_ref[...] = jnp.where(lane == 0, w1, jnp.where(lane == 1, w2, pos_cols))
    xs_ref[...] = xs.astype(BF16)
    mlane = lax.broadcasted_iota(I32, (LANES, LANES), 1)
    meta = jnp.where(mlane == 0, cnt, jnp.where(mlane == 1, off, 0.0))
    meta_ref[...] = meta.astype(I32)


GATE_COLS = {3: 0, 6: RET_V, 7: RET_V + D_MODEL}
QKV_COLS = {0: 0, 1: RET_QK, 2: 2 * RET_QK}


def _prompt_mixer_kernel(x_ref, x1s_ref, rws_ref, cos_ref, sin_ref, dec_ref, qdec_ref, kdec_ref, cdec_ref,
                         w_in, b_in, gn_g, gn_b, w_ret_o, conv_w, conv_b, cln_g, cln_b,
                         w_conv_o, w_out, ln1_g, ln1_b, wr_hi, wr_lo, b_r,
                         x1_ref, rw_ref, xs_ref, meta_ref, sret_ref, sconv_ref,
                         ubuf, ushift, qkv_scr, ret_scr, cout_scr, gate_scr,
                         *, n_tiles, tiles_per_seq):
    s = pl.program_id(0)
    li = lax.rem(s, tiles_per_seq)
    outs = (x1_ref, rw_ref, xs_ref, meta_ref)
    slot = dict(ret=ret_scr, cout=cout_scr, gates=gate_scr)
    tail_w = (w_ret_o, cln_g, cln_b, w_conv_o, w_out, ln1_g, ln1_b, wr_hi, wr_lo, b_r)

    @pl.when((s < n_tiles) & (li == 0))
    def _new_sequence():
        sret_ref[...] = jnp.zeros(sret_ref.shape, F32)
        ubuf[0:CONV_PAD, :] = jnp.zeros((CONV_PAD, CONV_CH), F32)

    @pl.when(s < n_tiles)
    def _mix():
        gcols = lambda kk: slice(GATE_COLS[kk], GATE_COLS[kk] + IN_WIDTHS[kk])
        src = dict(x=lambda: x_ref[0], ret=lambda: ret_scr[...], cout=lambda: cout_scr[...],
                   g=lambda: gate_scr[:, gcols(3)], gt_a=lambda: gate_scr[:, gcols(6)],
                   gt_b=lambda: gate_scr[:, gcols(7)])
        head = _prompt_head_pieces(x_ref, cos_ref, sin_ref, dec_ref, qdec_ref, kdec_ref, cdec_ref,
                                   w_in, b_in, gn_g, gn_b, conv_w, conv_b, sret_ref,
                                   ubuf, ushift, qkv_scr, slot)
        tail = _post_mix_pieces(src, tail_w, lambda *r: _sort_tile(*r, *outs))
        for piece in head + tail:
            piece()

    @pl.when(s >= n_tiles)
    def _append():
        rws = rws_ref[...]
        _sort_tile(x1s_ref[...], rws[:, 0:1], rws[:, 1:2], rws[:, 2:3], rws[:, 3:4], *outs)

    @pl.when((s < n_tiles) & (li == tiles_per_seq - 1))
    def _conv_state():
        sconv_ref[0, 0] = ubuf[CONV_PAD - (CONV_WIDTH - 1):CONV_PAD, :]


def _prompt_head_pieces(x_ref, cos_ref, sin_ref, dec_ref, qdec_ref, kdec_ref, cdec_ref,
                        w_in, b_in, gn_g, gn_b, conv_w, conv_b, sret_ref,
                        ubuf, ushift, qkv_scr, slot):
    tl = x_ref.shape[1]
    st = {}

    def slab_dot(c0, c1):
        return jnp.dot(st["xb"], w_in[:, c0:c1], preferred_element_type=F32) + b_in[:, c0:c1]

    def glu():
        st["xb"] = x_ref[0].astype(BF16)
        u = slab_dot(IN_OFFS[4], IN_OFFS[5]) * _sigmoid(slab_dot(IN_OFFS[5], IN_OFFS[6]))
        ubuf[CONV_PAD:CONV_PAD + tl, :] = u
        nsh = ushift.shape[1]
        for s in range(1, SUBLANES):
            ushift[s - 1] = ubuf[s:s + nsh, :]

    slab = 256
    slabs = [(kk, c0) for kk in (0, 1, 2, 3, 6, 7) for c0 in range(IN_OFFS[kk], IN_OFFS[kk + 1], slab)]
    rb = 32
    nrb = tl // rb

    def conv_block(r):
        acc = jnp.zeros((rb, CONV_CH), F32) + conv_b[...]
        for j in range(CONV_WIDTH):
            off = j + (CONV_PAD - (CONV_WIDTH - 1))
            s = off % SUBLANES
            base = r * rb + off - s
            win = ubuf[base:base + rb, :] if s == 0 else ushift[s - 1, base:base + rb, :]
            acc = acc + conv_w[j:j + 1, :] * win
        slot["cout"][r * rb:(r + 1) * rb, :] = acc
        for kk, c0 in slabs[r * len(slabs) // nrb:(r + 1) * len(slabs) // nrb]:
            val = slab_dot(c0, c0 + slab)
            if kk in QKV_COLS:
                dst = QKV_COLS[kk] + c0 - IN_OFFS[kk]
                qkv_scr[:, dst:dst + slab] = val
            else:
                dst = GATE_COLS[kk] + c0 - IN_OFFS[kk]
                slot["gates"][:, dst:dst + slab] = val
        if r == nrb - 1:
            ubuf[0:CONV_PAD, :] = ubuf[tl:tl + CONV_PAD, :]

    scale = RET_DK ** -0.5

    def retention(c, h):
        rows = slice(c * RET_CHUNK, (c + 1) * RET_CHUNK)
        cols = slice(h * RET_DK, (h + 1) * RET_DK)
        hcol = lambda kk: slice(QKV_COLS[kk] + h * RET_DK, QKV_COLS[kk] + (h + 1) * RET_DK)
        cosf = cos_ref[rows, :]
        sinf = sin_ref[rows, :]
        qh = _rot(qkv_scr[rows, hcol(0)], cosf, sinf)
        kh = _rot(qkv_scr[rows, hcol(1)], cosf, sinf) * scale
        qb = qh.astype(BF16)
        kb = kh.astype(BF16)
        vb = qkv_scr[rows, hcol(2)].astype(BF16)
        s_old = sret_ref[0, 0, h]
        scores = lax.dot_general(qb, kb, (((1,), (1,)), ((), ())),
                                 preferred_element_type=F32) * dec_ref[h]
        inner = jnp.dot(scores.astype(BF16), vb, preferred_element_type=F32)
        cross = jnp.dot(qb, s_old.astype(BF16), preferred_element_type=F32) * qdec_ref[h]
        kd = (kh * kdec_ref[h]).astype(BF16)
        s_new = cdec_ref[h] * s_old + lax.dot_general(
            kd, vb, (((0,), (0,)), ((), ())), preferred_element_type=F32)
        sret_ref[0, 0, h] = s_new
        slot["ret"][rows, cols] = _ln(inner + cross, gn_g[:, cols], gn_b[:, cols])

    pieces = [glu]
    pieces += [lambda r=r: conv_block(r) for r in range(nrb)]
    pieces += [lambda c=c, h=h: retention(c, h) for c in range(tl // RET_CHUNK) for h in range(RET_HEADS)]
    return pieces


def _sample_mixer_kernel(x_ref, cos_ref, sin_ref, pdec_ref, qdec_ref, kdec_ref, cdec_ref, wsh_ref,
                         sret_in, sconv_in,
                         w_in, b_in, gn_g, gn_b, w_ret_o, conv_b, cln_g, cln_b,
                         w_conv_o, w_out, ln1_g, ln1_b, wr_hi, wr_lo, b_r,
                         x1_ref, rw_ref, sret_ref, sconv_ref,
                         ret_scr, cout_scr, xpad):
    t = x_ref.shape[0]
    ls = t // BB_SAMPLE
    x = x_ref[...]
    xb = x.astype(BF16)

    def proj(k):
        c0, c1 = IN_OFFS[k], IN_OFFS[k + 1]
        return jnp.dot(xb, w_in[:, c0:c1], preferred_element_type=F32) + b_in[:, c0:c1]

    q = proj(0)
    k = proj(1)
    v = proj(2)
    scale = RET_DK ** -0.5
    cosf = cos_ref[...]
    sinf = sin_ref[...]
    row = lax.broadcasted_iota(I32, (t, RET_DK), 0)
    pos = row % ls
    row8 = lax.broadcasted_iota(I32, (SUBLANES, RET_DK), 0)
    per_tile = SUBLANES // ls
    for h in range(RET_HEADS):
        cols = slice(h * RET_DK, (h + 1) * RET_DK)
        qh = _rot(q[:, cols], cosf, sinf)
        kh = _rot(k[:, cols], cosf, sinf) * scale
        vh = v[:, cols]
        inner = jnp.zeros((t, RET_DV), F32)
        for s in range(ls):
            ks = kh if s == 0 else pltpu.roll(kh, s, axis=0)
            vs = vh if s == 0 else pltpu.roll(vh, s, axis=0)
            dotp = jnp.sum(qh * ks, axis=1, keepdims=True) * pdec_ref[h, s]
            inner = inner + jnp.where(pos >= s, dotp, 0.0) * vs
        kd = kh * kdec_ref[h]
        for tile in range(t // SUBLANES):
            rows = slice(tile * SUBLANES, (tile + 1) * SUBLANES)
            q8 = qh[rows, :]
            kd8 = kd[rows, :]
            v8 = vh[rows, :]
            cross8 = jnp.zeros((SUBLANES, RET_DV), F32)
            for sub in range(per_tile):
                b = tile * per_tile + sub
                mine = (row8 >= sub * ls) & (row8 < (sub + 1) * ls)
                s_old = sret_in[0, b, h]
                c_b = jnp.dot(q8, s_old, preferred_element_type=F32)
                cross8 = jnp.where(mine, c_b, cross8)
                upd = lax.dot_general(jnp.where(mine, kd8, 0.0), v8, (((0,), (0,)), ((), ())),
                                      preferred_element_type=F32)
                sret_ref[0, b, h] = cdec_ref[h] * s_old + upd
            ret_scr[rows, cols] = inner[rows, :] + cross8 * qdec_ref[h, rows, :]
        ret_scr[:, cols] = _ln(ret_scr[:, cols], gn_g[:, cols], gn_b[:, cols])

    u = proj(4) * _sigmoid(proj(5))
    nstate = CONV_WIDTH - 1
    xpad[...] = jnp.zeros(xpad.shape, F32)
    xpad[:, 0:nstate, :] = sconv_in[0]
    for b in range(BB_SAMPLE):
        xpad[b, XPAD_NEW:XPAD_NEW + ls, :] = u[b * ls:(b + 1) * ls, :]
    for i in range(ls):
        res = jnp.sum(xpad[...] * wsh_ref[i][None], axis=1) + conv_b[...]
        for sl in range(CONV_CH // LANES):
            cout_scr[sl, pl.ds(i, BB_SAMPLE, stride=ls), :] = res[:, sl * LANES:(sl + 1) * LANES]
    sconv_ref[0, :, 0:nstate - ls, :] = xpad[:, ls:nstate, :]
    sconv_ref[0, :, nstate - ls:nstate, :] = xpad[:, XPAD_NEW:XPAD_NEW + ls, :]
    c_out = jnp.concatenate([cout_scr[sl] for sl in range(CONV_CH // LANES)], axis=1)

    def sink(x1, w1, w2, e1, e2):
        x1_ref[...] = x1
        rw_ref[...] = _lane_tile((w1, w2, e1, e2), t)

    src = dict(x=lambda: x, ret=lambda: ret_scr[...], cout=lambda: c_out,
               g=lambda: proj(3), gt_a=lambda: proj(6), gt_b=lambda: proj(7))
    for piece in _post_mix_pieces(
            src, (w_ret_o, cln_g, cln_b, w_conv_o, w_out, ln1_g, ln1_b, wr_hi, wr_lo, b_r), sink):
        piece()


def _ffn_kernel(te_ref, nvalid_ref, chunk_ref, xs_hbm, w_gu, w_dn, ys_hbm,
                xbuf, obuf, wgu_b, wdn_b, sem_in, sem_out):
    del xs_hbm
    i = pl.program_id(0)
    n = pl.num_programs(0)
    slot = i % 2
    nvalid = nvalid_ref[0]

    def chunk_rows(tile, c):
        return pl.ds(pl.multiple_of(chunk_ref[tile * TILE_CHUNKS + c] * CHUNK, CHUNK), CHUNK)

    def start_in(tile, s):
        for c in range(TILE_CHUNKS):
            pltpu.make_async_copy(ys_hbm.at[chunk_rows(tile, c)],
                                  xbuf.at[s, pl.ds(c * CHUNK, CHUNK)], sem_in.at[s]).start()

    def start_out(tile, s):
        for c in range(TILE_CHUNKS):
            pltpu.make_async_copy(obuf.at[s, pl.ds(c * CHUNK, CHUNK)],
                                  ys_hbm.at[chunk_rows(tile, c)], sem_out.at[s]).start()

    def wait_in(s):
        pltpu.make_async_copy(ys_hbm.at[pl.ds(0, TM_FFN)], xbuf.at[s], sem_in.at[s]).wait()

    def wait_out(s):
        pltpu.make_async_copy(obuf.at[s], ys_hbm.at[pl.ds(0, TM_FFN)], sem_out.at[s]).wait()

    @pl.when((i == 0) & (nvalid > 0))
    def _first():
        start_in(0, 0)

    @pl.when(i + 1 < nvalid)
    def _prefetch():
        start_in(i + 1, 1 - slot)

    @pl.when((i >= 2) & (i - 2 < nvalid))
    def _retire():
        wait_out(slot)

    @pl.when(i < nvalid)
    def _tile():
        wait_in(slot)
        prev = te_ref[jnp.maximum(i - 1, 0)]

        @pl.when((i == 0) | (te_ref[i] != prev))
        def _new_expert():
            wgu_b[...] = w_gu[0].astype(BF16)
            wdn_b[...] = w_dn[0].astype(BF16)

        x = xbuf[slot]
        y = jnp.zeros((TM_FFN, w_dn.shape[2]), F32)
        for c0 in range(0, EXP_FF, FFN_COLS):
            hg = jnp.dot(x, wgu_b[:, c0:c0 + FFN_COLS], preferred_element_type=F32)
            hu = jnp.dot(x, wgu_b[:, EXP_FF + c0:EXP_FF + c0 + FFN_COLS], preferred_element_type=F32)
            y = y + _bdot(_silu(hg) * hu, wdn_b[c0:c0 + FFN_COLS, :])
        obuf[slot] = y.astype(BF16)
        start_out(i, slot)

    @pl.when(i == n - 1)
    def _drain():
        @pl.when((i >= 1) & (i - 1 < nvalid))
        def _():
            wait_out(1 - slot)

        @pl.when(i < nvalid)
        def _():
            wait_out(slot)


def _final_kernel(ys_ref, x1_ref, rw_ref, pp_ref, ps_ref, ln2_g, ln2_b, w_pg, b_pg, w_ple,
                  yp_ref, ys_out_ref, *, n_prompt_tiles):
    i = pl.program_id(0)
    tl = x1_ref.shape[0]
    x1 = x1_ref[...]
    rw = rw_ref[...]
    w1, w2, pos1, pos2 = rw[:, 0:1], rw[:, 1:2], rw[:, 2:3], rw[:, 3:4]
    slot = lax.broadcasted_iota(I32, (tl, USED_ROWS), 1).astype(F32)
    ys = ys_ref[0:USED_ROWS, :]
    pick = lambda pos: jnp.dot((slot == pos).astype(BF16), ys, preferred_element_type=F32)
    moe = pick(pos1) * w1 + pick(pos2) * w2
    x2 = _ln(ALPHA * x1 + moe, ln2_g[...], ln2_b[...])
    gate = _sigmoid(_bdot(x2, w_pg[...]) + b_pg[...])
    p = jnp.where(i < n_prompt_tiles, pp_ref[...], ps_ref[...])
    y = x2 + gate * _bdot(p, w_ple[...])

    @pl.when(i < n_prompt_tiles)
    def _prompt():
        yp_ref[...] = y

    @pl.when(i >= n_prompt_tiles)
    def _sample():
        ys_out_ref[...] = y


def _rope_tables(pos):
    half = RET_DK // 2
    inv_freq = ROPE_BASE ** (-np.arange(half, dtype=np.float64) / half)
    ang = np.asarray(pos, np.float64)[:, None] * inv_freq[None, :]
    cos = np.cos(ang)
    sin = np.sin(ang)
    return (np.concatenate([cos, cos], axis=-1).astype(np.float32),
            np.concatenate([-sin, sin], axis=-1).astype(np.float32))


def _log_gamma():
    return np.log(1.0 - 2.0 ** (-5.0 - np.arange(RET_HEADS, dtype=np.float64)))


def _const_spec(shape):
    nd = len(shape)
    return pl.BlockSpec(shape, lambda *_: (0,) * nd, pipeline_mode=pl.Buffered(1))


def _chunk_plan(meta, n_blocks, n_ffn_tiles):
    assert n_blocks * BLOCK_SPARE >= N_EXPERTS * (TILE_CHUNKS - 1)
    m = meta.reshape(n_blocks, LANES, LANES)
    cnt = m[:, :N_EXPERTS, 0]
    off = m[:, :N_EXPERTS, 1]
    nch = (cnt + (CHUNK - 1)) // CHUNK
    cum = jnp.cumsum(nch, axis=0)
    total = cum[-1:]
    tiles_e = (total + TILE_CHUNKS - 1) // TILE_CHUNKS
    tile_end = jnp.cumsum(tiles_e, axis=1)
    tile_start = tile_end - tiles_e
    tid = jnp.arange(n_ffn_tiles, dtype=I32)[:, None]
    owner = (tid >= tile_start) & (tid < tile_end)
    pick_e = lambda v: jnp.sum(jnp.where(owner, v, 0), axis=1, keepdims=True)
    te = pick_e(jnp.arange(N_EXPERTS, dtype=I32)[None, :])
    k = (tid - pick_e(tile_start)) * TILE_CHUNKS + jnp.arange(TILE_CHUNKS, dtype=I32)[None, :]
    total_t = pick_e(total)
    real = k < total_t
    by_tile = lambda v: jnp.sum(jnp.where(owner[:, None, :], v[None, :, :], 0), axis=2)
    cum_t = by_tile(cum)
    blk = jnp.minimum(jnp.sum((cum_t[:, None, :] <= k[:, :, None]).astype(I32), axis=2), n_blocks - 1)
    at_blk = blk[:, :, None] == jnp.arange(n_blocks, dtype=I32)[None, None, :]
    pick_b = lambda v: jnp.sum(jnp.where(at_blk, v[:, None, :], 0), axis=2)
    excl = pick_b(cum_t - by_tile(nch))
    off_t = pick_b(by_tile(off))
    spare = te * (TILE_CHUNKS - 1) + jnp.maximum(k - total_t, 0) % TILE_CHUNKS
    spare_chunk = (spare // BLOCK_SPARE) * BLOCK_CHUNKS + BLOCK_USED + spare % BLOCK_SPARE
    chunk = jnp.where(real, blk * BLOCK_CHUNKS + off_t + (k - excl), spare_chunk)
    n_valid = jnp.sum(tiles_e, axis=1)
    te = jnp.where(tid < n_valid, te, N_EXPERTS - 1)
    return te.reshape(-1).astype(I32), n_valid.astype(I32), chunk.reshape(-1).astype(I32)


def kernel(x_prompt, x_sample, state_ret, state_conv, p_prompt, p_sample, w_in, b_in, ret_gn_g, ret_gn_b,
           w_ret_o, conv_w, conv_b, conv_ln_g, conv_ln_b, w_conv_o, w_out, ln1_g, ln1_b, w_grp, b_grp,
           w_exp, b_exp, w_gu, w_dn, ln2_g, ln2_b, w_pg, b_pg, w_ple):
    assert DEPTH == 1 and w_in.shape[0] == 1
    bp, lp, d = x_prompt.shape
    bs, ls, _ = x_sample.shape
    n_p, n_s = bp * lp, bs * ls
    n_tok = n_p + n_s
    assert lp % TL == 0 and n_s % TL == 0 and bs % BB_SAMPLE == 0 and SUBLANES % ls == 0
    n_blocks = n_tok // TL

    f32c = lambda a, shape: jnp.asarray(np.broadcast_to(a, shape).astype(np.float32))
    lg = _log_gamma()
    c = RET_CHUNK
    idx = np.arange(c, dtype=np.float64)
    rel = idx[:, None] - idx[None, :]
    causal = rel >= 0
    decay = np.where(causal[None], np.exp(np.where(causal, rel, 0.0)[None] * lg[:, None, None]), 0.0)
    decay = f32c(decay, decay.shape)
    q_decay = np.exp((idx[:, None] + 1.0) * lg[None, :])
    k_decay = np.exp((c - 1.0 - idx[:, None]) * lg[None, :])
    chunk_decay = np.exp(c * lg)
    qdec_p = f32c(q_decay.T[:, :, None], (RET_HEADS, c, RET_DK))
    kdec_p = f32c(k_decay.T[:, :, None], (RET_HEADS, c, RET_DK))
    cdec_p = f32c(chunk_decay[:, None, None], (RET_HEADS, 1, RET_DV))
    cos_p, sin_p = (jnp.asarray(a) for a in _rope_tables(np.arange(lp)))

    ts = BB_SAMPLE * ls
    idx_s = np.arange(ls, dtype=np.float64)
    pdec_s = np.exp(idx_s[None, :] * lg[:, None])
    pdec_s = f32c(pdec_s[:, :, None, None], (RET_HEADS, ls, 1, RET_DK))
    qd_s = np.exp((idx_s[:, None] + 1.0) * lg[None, :])
    kd_s = np.exp((ls - 1.0 - idx_s[:, None]) * lg[None, :])
    qdec_s = f32c(np.tile(qd_s.T, (1, BB_SAMPLE))[:, :, None], (RET_HEADS, ts, RET_DK))
    kdec_s = f32c(np.tile(kd_s.T, (1, BB_SAMPLE))[:, :, None], (RET_HEADS, ts, RET_DK))
    cdec_s = f32c(np.exp(ls * lg)[:, None, None], (RET_HEADS, 1, RET_DV))
    cos_s, sin_s = (jnp.asarray(a) for a in _rope_tables(np.tile(PAST_LEN + np.arange(ls), BB_SAMPLE)))

    w_in_b = w_in[0].astype(BF16)
    w_ret_o_b = w_ret_o[0].astype(BF16)
    w_conv_o_b = w_conv_o[0].astype(BF16)
    w_out_b = w_out[0].astype(BF16)
    w_pg_b = w_pg[0].astype(BF16)
    w_ple_b = w_ple[0].astype(BF16)
    w_r = jnp.zeros((d, LANES), F32).at[:, :N_GROUPS].set(w_grp[0]).at[:, N_GROUPS:N_GROUPS + N_EXPERTS].set(w_exp[0])
    wr_hi = w_r.astype(BF16)
    wr_lo = (w_r - wr_hi.astype(F32)).astype(BF16)
    b_r = jnp.zeros((1, LANES), F32).at[0, :N_GROUPS].set(b_grp[0]).at[0, N_GROUPS:N_GROUPS + N_EXPERTS].set(b_exp[0])
    row = lambda a: a.reshape(1, -1)
    conv_w0 = conv_w[0]
    nstate = CONV_WIDTH - 1
    win_row = np.array([m if m < nstate else XPAD_NEW + (m - nstate) for m in range(nstate + ls)])
    wsh = jnp.stack([jnp.zeros((XPAD_ROWS, CONV_CH), F32).at[win_row[i:i + CONV_WIDTH]].set(conv_w0)
                     for i in range(ls)])

    shared_w = (w_in_b, row(b_in[0]), row(ret_gn_g[0]), row(ret_gn_b[0]), w_ret_o_b)
    tail_w = (row(conv_ln_g[0]), row(conv_ln_b[0]), w_conv_o_b, w_out_b, row(ln1_g[0]), row(ln1_b[0]),
              wr_hi, wr_lo, b_r)

    nbt = bs // BB_SAMPLE
    xs2 = x_sample.reshape(n_s, d)
    sample_in = ((xs2, cos_s, sin_s, pdec_s, qdec_s, kdec_s, cdec_s, wsh, state_ret, state_conv)
                 + shared_w + (row(conv_b[0]),) + tail_w)
    sample_specs = (
        [pl.BlockSpec((ts, d), lambda i: (i, 0))]
        + [_const_spec(a.shape) for a in sample_in[1:8]]
        + [pl.BlockSpec((1, BB_SAMPLE, RET_HEADS, RET_DK, RET_DV), lambda i: (0, i, 0, 0, 0)),
           pl.BlockSpec((1, BB_SAMPLE, nstate, CONV_CH), lambda i: (0, i, 0, 0))]
        + [_const_spec(a.shape) for a in sample_in[10:]]
    )
    tok_spec_s = lambda w: pl.BlockSpec((ts, w), lambda i: (i, 0))
    x1_s, rw_s, ret_s, conv_s = pl.pallas_call(
        _sample_mixer_kernel,
        grid=(nbt,),
        in_specs=sample_specs,
        out_specs=[
            tok_spec_s(d), tok_spec_s(LANES),
            pl.BlockSpec((1, BB_SAMPLE, RET_HEADS, RET_DK, RET_DV), lambda i: (0, i, 0, 0, 0)),
            pl.BlockSpec((1, BB_SAMPLE, nstate, CONV_CH), lambda i: (0, i, 0, 0)),
        ],
        out_shape=[
            jax.ShapeDtypeStruct((n_s, d), F32),
            jax.ShapeDtypeStruct((n_s, LANES), F32),
            jax.ShapeDtypeStruct(state_ret.shape, F32),
            jax.ShapeDtypeStruct(state_conv.shape, F32),
        ],
        scratch_shapes=[
            pltpu.VMEM((ts, RET_V), F32),
            pltpu.VMEM((CONV_CH // LANES, ts, LANES), F32),
            pltpu.VMEM((BB_SAMPLE, XPAD_ROWS, CONV_CH), F32),
        ],
        compiler_params=pltpu.CompilerParams(
            dimension_semantics=("arbitrary",), vmem_limit_bytes=VMEM_LIMIT),
        name="sample_mixer",
    )(*sample_in)

    nlt = lp // TL
    npt = n_p // TL
    nst = n_s // TL
    prompt_in = ((x_prompt, x1_s, rw_s, cos_p, sin_p, decay, qdec_p, kdec_p, cdec_p)
                 + shared_w + (conv_w0, row(conv_b[0])) + tail_w)
    head_tile = lambda s: jnp.minimum(s, npt - 1)
    sample_tile = lambda s: jnp.maximum(s - npt, 0)
    sample_spec = lambda w: pl.BlockSpec((TL, w), lambda s: (sample_tile(s), 0))
    prompt_specs = [
        pl.BlockSpec((1, TL, d), lambda s: (head_tile(s) // nlt, head_tile(s) % nlt, 0)),
        sample_spec(d), sample_spec(LANES),
        pl.BlockSpec((TL, RET_DK), lambda s: (head_tile(s) % nlt, 0)),
        pl.BlockSpec((TL, RET_DK), lambda s: (head_tile(s) % nlt, 0)),
    ] + [_const_spec(a.shape) for a in prompt_in[5:]]
    tok_spec_p = lambda rows, w: pl.BlockSpec((rows, w), lambda s: (s, 0))
    x1_all, rw_all, xs_all, meta, ret_p, conv_p = pl.pallas_call(
        functools.partial(_prompt_mixer_kernel, n_tiles=npt, tiles_per_seq=nlt),
        grid=(npt + nst,),
        in_specs=prompt_specs,
        out_specs=[
            tok_spec_p(TL, d), tok_spec_p(TL, LANES), tok_spec_p(CAP, d), tok_spec_p(LANES, LANES),
            pl.BlockSpec((1, 1, RET_HEADS, RET_DK, RET_DV), lambda s: (0, head_tile(s) // nlt, 0, 0, 0)),
            pl.BlockSpec((1, 1, nstate, CONV_CH), lambda s: (0, head_tile(s) // nlt, 0, 0)),
        ],
        out_shape=[
            jax.ShapeDtypeStruct((n_tok, d), F32),
            jax.ShapeDtypeStruct((n_tok, LANES), F32),
            jax.ShapeDtypeStruct((n_blocks * CAP, d), BF16),
            jax.ShapeDtypeStruct((n_blocks * LANES, LANES), I32),
            jax.ShapeDtypeStruct((1, bp, RET_HEADS, RET_DK, RET_DV), F32),
            jax.ShapeDtypeStruct((1, bp, nstate, CONV_CH), F32),
        ],
        scratch_shapes=[
            pltpu.VMEM((TL + CONV_PAD, CONV_CH), F32),
            pltpu.VMEM((SUBLANES - 1, TL + CONV_PAD - SUBLANES, CONV_CH), F32),
            pltpu.VMEM((TL, 2 * RET_QK + RET_V), F32),
            pltpu.VMEM((TL, RET_V), F32),
            pltpu.VMEM((TL, CONV_CH), F32),
            pltpu.VMEM((TL, RET_V + 2 * D_MODEL), F32),
        ],
        compiler_params=pltpu.CompilerParams(
            dimension_semantics=("arbitrary",), vmem_limit_bytes=VMEM_LIMIT),
        name="prompt_mixer",
    )(*prompt_in)

    max_chunks = n_blocks * (TOP_K * TL // CHUNK + N_EXPERTS - 1)
    n_ffn_tiles = (max_chunks + N_EXPERTS * (TILE_CHUNKS - 1)) // TILE_CHUNKS
    tile_e, n_valid_tiles, chunk_ids = _chunk_plan(meta, n_blocks, n_ffn_tiles)

    ys_all = pl.pallas_call(
        _ffn_kernel,
        grid_spec=pltpu.PrefetchScalarGridSpec(
            num_scalar_prefetch=3,
            grid=(n_ffn_tiles,),
            in_specs=[
                pl.BlockSpec(memory_space=pl.ANY),
                pl.BlockSpec((1, d, 2 * EXP_FF), lambda i, te, nr, ch: (te[i], 0, 0)),
                pl.BlockSpec((1, EXP_FF, d), lambda i, te, nr, ch: (te[i], 0, 0)),
            ],
            out_specs=pl.BlockSpec(memory_space=pl.ANY),
            scratch_shapes=[
                pltpu.VMEM((2, TM_FFN, d), BF16),
                pltpu.VMEM((2, TM_FFN, d), BF16),
                pltpu.VMEM((d, 2 * EXP_FF), BF16),
                pltpu.VMEM((EXP_FF, d), BF16),
                pltpu.SemaphoreType.DMA((2,)),
                pltpu.SemaphoreType.DMA((2,)),
            ],
        ),
        out_shape=jax.ShapeDtypeStruct(xs_all.shape, BF16),
        input_output_aliases={3: 0},
        compiler_params=pltpu.CompilerParams(
            dimension_semantics=("arbitrary",), vmem_limit_bytes=VMEM_LIMIT),
        name="expert_ffn",
    )(tile_e, n_valid_tiles, chunk_ids, xs_all, w_gu[0], w_dn[0])

    npt = n_p // TL
    pp2 = p_prompt.reshape(n_p, PLE_DIM)
    ps2 = p_sample.reshape(n_s, PLE_DIM)
    tok_f = lambda rows, w: pl.BlockSpec((rows, w), lambda i: (i, 0))
    y_p, y_s = pl.pallas_call(
        functools.partial(_final_kernel, n_prompt_tiles=npt),
        grid=(n_blocks,),
        in_specs=[
            tok_f(CAP, d), tok_f(TL, d), tok_f(TL, LANES),
            pl.BlockSpec((TL, PLE_DIM), lambda i: (jnp.minimum(i, npt - 1), 0)),
            pl.BlockSpec((TL, PLE_DIM), lambda i: (jnp.maximum(i - npt, 0), 0)),
            _const_spec((1, d)), _const_spec((1, d)), _const_spec((d, d)), _const_spec((1, d)),
            _const_spec((PLE_DIM, d)),
        ],
        out_specs=[
            pl.BlockSpec((TL, d), lambda i: (jnp.minimum(i, npt - 1), 0)),
            pl.BlockSpec((TL, d), lambda i: (jnp.maximum(i - npt, 0), 0)),
        ],
        out_shape=[jax.ShapeDtypeStruct((n_p, d), F32), jax.ShapeDtypeStruct((n_s, d), F32)],
        compiler_params=pltpu.CompilerParams(
            dimension_semantics=("arbitrary",), vmem_limit_bytes=VMEM_LIMIT),
        name="moe_combine_final",
    )(ys_all, x1_all, rw_all, pp2, ps2,
      row(ln2_g[0]), row(ln2_b[0]), w_pg_b, row(b_pg[0]), w_ple_b)

    return (y_p.reshape(bp, lp, d), y_s.reshape(bs, ls, d), ret_p, conv_p, ret_s, conv_s)
```

```python
import functools

import jax
import jax.numpy as jnp
import numpy as np
from jax import lax
from jax.experimental import pallas as pl
from jax.experimental.pallas import tpu as pltpu

F32 = jnp.float32
BF16 = jnp.bfloat16
I32 = jnp.int32
U32 = jnp.uint32

D_MODEL = 1024
PAST_LEN = 16384
RET_HEADS = 4
RET_DK = 128
RET_DV = 128
RET_QK = RET_HEADS * RET_DK
RET_V = RET_HEADS * RET_DV
RET_CHUNK = 128
ROPE_BASE = 10000.0
CONV_CH = 512
CONV_WIDTH = 31
N_GROUPS = 4
EXP_PER_GROUP = 4
N_EXPERTS = N_GROUPS * EXP_PER_GROUP
TOP_K = 2
EXP_FF = 512
PLE_DIM = 256
DEPTH = 1
ALPHA = (2 * DEPTH) ** 0.25
LN_EPS = 1e-5
IN_WIDTHS = (RET_QK, RET_QK, RET_V, RET_V, CONV_CH, CONV_CH, D_MODEL, D_MODEL)
IN_OFFS = tuple(int(s) for s in np.cumsum((0,) + IN_WIDTHS))

LANES = 128
SUBLANES = 8
VMEM_LIMIT = 56 * 1024 * 1024

TL = 256
TLM = 512
BB_SAMPLE = 16
CHUNK = 2 * SUBLANES
TILE_CHUNKS = 32
BLOCK_USED = -(-(TOP_K * TL + N_EXPERTS * (CHUNK - 1)) // LANES) * LANES // CHUNK
BLOCK_SPARE = LANES // CHUNK
BLOCK_CHUNKS = BLOCK_USED + BLOCK_SPARE
USED_ROWS = BLOCK_USED * CHUNK
CAP = BLOCK_CHUNKS * CHUNK
TM_FFN = TILE_CHUNKS * CHUNK
FFN_COLS = 256
CONV_PAD = 32
XPAD_NEW = 32
XPAD_ROWS = 40


def _ln(x, g, b):
    mu = jnp.mean(x, axis=-1, keepdims=True)
    d = x - mu
    var = jnp.mean(d * d, axis=-1, keepdims=True)
    return d * lax.rsqrt(var + LN_EPS) * g + b


def _sigmoid(x):
    return 1.0 / (1.0 + jnp.exp(-x))


def _rep(v8, rows):
    return v8 if rows == SUBLANES else jnp.concatenate([v8] * (rows // SUBLANES), axis=0)


def _silu(x):
    return x * _sigmoid(x)


def _bdot(a, b):
    return jnp.dot(a.astype(BF16), b, preferred_element_type=F32)


def _rot(t, cosf, sinf):
    return t * cosf + pltpu.roll(t, RET_DK // 2, axis=1) * sinf


def _lane_tile(cols, rows):
    lane = lax.broadcasted_iota(I32, (rows, LANES), 1)
    out = jnp.zeros((rows, LANES), F32)
    for i, col in enumerate(cols):
        out = jnp.where(lane == i, col, out)
    return out


def _route(logits):
    lane = lax.broadcasted_iota(I32, logits.shape, 1)
    lanef = lane.astype(F32)
    ninf = jnp.float32(-jnp.inf)
    big = jnp.float32(LANES)
    gmask = lane < N_GROUPS
    gl = jnp.where(gmask, logits, ninf)
    gmax = jnp.max(gl, axis=1, keepdims=True)
    gidx = jnp.min(jnp.where(gmask & (gl == gmax), lanef, big), axis=1, keepdims=True)
    sumexp = jnp.sum(jnp.where(gmask, jnp.exp(gl - gmax), 0.0), axis=1, keepdims=True)
    gw = 1.0 / sumexp
    lo = N_GROUPS + EXP_PER_GROUP * gidx
    emask = (lanef >= lo) & (lanef < lo + EXP_PER_GROUP)
    el = jnp.where(emask, logits, ninf)
    m1 = jnp.max(el, axis=1, keepdims=True)
    i1 = jnp.min(jnp.where(emask & (el == m1), lanef, big), axis=1, keepdims=True)
    emask2 = emask & (lanef != i1)
    el2 = jnp.where(emask2, logits, ninf)
    m2 = jnp.max(el2, axis=1, keepdims=True)
    i2 = jnp.min(jnp.where(emask2 & (el2 == m2), lanef, big), axis=1, keepdims=True)
    t = jnp.exp(m2 - m1)
    den = 1.0 + t
    return (1.0 / den) * gw, (t / den) * gw, i1 - N_GROUPS, i2 - N_GROUPS


def _post_mix_pieces(src, w, sink):
    (w_ret_o, cln_g, cln_b, w_conv_o, w_out, ln1_g, ln1_b, wr_hi, wr_lo, b_r) = w
    st = {}

    def branch_a():
        st["a"] = _bdot(_silu(src["g"]()) * src["ret"](), w_ret_o[...])

    def branch_b():
        st["b"] = _bdot(_silu(_ln(src["cout"](), cln_g[...], cln_b[...])), w_conv_o[...])

    def merge():
        mix = _sigmoid(src["gt_a"]()) * st["a"] + _sigmoid(src["gt_b"]()) * st["b"]
        h = ALPHA * src["x"]() + _bdot(mix, w_out[...])
        st["x1"] = _ln(h, ln1_g[...], ln1_b[...])

    def router():
        x1 = st["x1"]
        x1_hi = x1.astype(BF16)
        x1_lo = (x1 - x1_hi.astype(F32)).astype(BF16)
        st["logits"] = (jnp.dot(x1_hi, wr_hi[...], preferred_element_type=F32)
                        + (jnp.dot(x1_lo, wr_hi[...], preferred_element_type=F32)
                           + jnp.dot(x1_hi, wr_lo[...], preferred_element_type=F32))
                        + b_r[...])

    def route():
        st["route"] = _route(st["logits"])

    def finish():
        sink(st["x1"], *st["route"])

    return [branch_a, branch_b, merge, router, route, finish]


def _interleave(a, b):
    j = 0
    for i, piece in enumerate(a):
        piece()
        while j < len(b) and (j + 1) * len(a) <= (i + 1) * len(b):
            b[j]()
            j += 1
    for piece in b[j:]:
        piece()


def _sort_tile(x1, w1, w2, e1, e2, x1_ref, rw_ref, xs_ref, meta_ref):
    t = x1.shape[0]
    ids_t = _lane_tile((e1, e2), t).T
    e1r, e2r = ids_t[0:1, :], ids_t[1:2, :]
    sub = lax.broadcasted_iota(I32, (LANES, t), 0).astype(F32)
    a1 = (sub == e1r).astype(F32)
    a2 = (sub == e2r).astype(F32)
    ri = lax.broadcasted_iota(I32, (t, t), 0)
    ci = lax.broadcasted_iota(I32, (t, t), 1)
    earlier = (ri < ci).astype(BF16)
    r1 = jnp.dot(a1.astype(BF16), earlier, preferred_element_type=F32)
    r2 = jnp.dot(a2.astype(BF16), earlier, preferred_element_type=F32)
    cnt1 = jnp.sum(a1, axis=1, keepdims=True)
    cnt = cnt1 + jnp.sum(a2, axis=1, keepdims=True)
    nch = jnp.floor((cnt + (CHUNK - 1.0)) * (1.0 / CHUNK))
    ui = lax.broadcasted_iota(I32, (LANES, LANES), 0)
    uj = lax.broadcasted_iota(I32, (LANES, LANES), 1)
    before = (uj < ui).astype(BF16)
    off = jnp.dot(before, jnp.broadcast_to(nch, (LANES, LANES)).astype(BF16),
                  preferred_element_type=F32)[:, 0:1]
    base = off * CHUNK
    pos1r = jnp.sum(a1 * (base + r1), axis=0, keepdims=True)
    pos2r = jnp.sum(a2 * (base + cnt1 + r2), axis=0, keepdims=True)
    slot = lax.broadcasted_iota(I32, (CAP, t), 0).astype(F32)
    onehot = ((slot == pos1r) | (slot == pos2r)).astype(BF16)
    xs = jnp.dot(onehot, x1.astype(BF16), preferred_element_type=F32)
    pos_cols = jnp.where(sub == 2.0, pos1r, jnp.where(sub == 3.0, pos2r, 0.0)).T
    lane = lax.broadcasted_iota(I32, (t, LANES), 1)
    x1_ref[...] = x1
    rw_ref[...] = jnp.where(lane == 0, w1, jnp.where(lane == 1, w2, pos_cols))
    xs_ref[...] = xs.astype(BF16)
    mlane = lax.broadcasted_iota(I32, (LANES, LANES), 1)
    meta = jnp.where(mlane == 0, cnt, jnp.where(mlane == 1, off, 0.0))
    meta_ref[...] = meta.astype(I32)


def _sort_tiles(x1, w1, w2, e1, e2, x1_ref, rw_ref, xs_ref, meta_ref):
    for i in range(x1.shape[0] // TL):
        rows = slice(i * TL, (i + 1) * TL)
        _sort_tile(x1[rows], w1[rows], w2[rows], e1[rows], e2[rows],
                   x1_ref.at[pl.ds(i * TL, TL)], rw_ref.at[pl.ds(i * TL, TL)],
                   xs_ref.at[pl.ds(i * CAP, CAP)], meta_ref.at[pl.ds(i * LANES, LANES)])


GATE_COLS = {3: 0, 6: RET_V, 7: RET_V + D_MODEL}
QKV_COLS = {0: 0, 1: RET_QK, 2: 2 * RET_QK}


def _prompt_mixer_kernel(x_ref, x1s_ref, rws_ref, cos_ref, sin_ref, dec_ref, qdec_ref, kdec_ref, cdec_ref,
                         w_in, b_in, gn_g, gn_b, w_ret_o, conv_w, conv_b, cln_g, cln_b,
                         w_conv_o, w_out, ln1_g, ln1_b, wr_hi, wr_lo, b_r,
                         x1_ref, rw_ref, xs_ref, meta_ref, sret_ref, sconv_ref,
                         ubuf, ushift, qkv_scr, xb_scr, ret_scr, cout_scr, gate_scr,
                         *, n_tiles, tiles_per_seq):
    s = pl.program_id(0)
    li = lax.rem(s, tiles_per_seq)
    outs = (x1_ref, rw_ref, xs_ref, meta_ref)
    slot = dict(ret=ret_scr, cout=cout_scr, gates=gate_scr)
    tail_w = (w_ret_o, cln_g, cln_b, w_conv_o, w_out, ln1_g, ln1_b, wr_hi, wr_lo, b_r)

    @pl.when((s < n_tiles) & (li == 0))
    def _new_sequence():
        sret_ref[...] = jnp.zeros(sret_ref.shape, F32)
        ubuf[0:CONV_PAD, :] = jnp.zeros((CONV_PAD, CONV_CH), F32)

    @pl.when(s < n_tiles)
    def _mix():
        gcols = lambda kk: slice(GATE_COLS[kk], GATE_COLS[kk] + IN_WIDTHS[kk])
        src = dict(x=lambda: x_ref[0], ret=lambda: ret_scr[...], cout=lambda: cout_scr[...],
                   g=lambda: gate_scr[:, gcols(3)], gt_a=lambda: gate_scr[:, gcols(6)],
                   gt_b=lambda: gate_scr[:, gcols(7)])
        head = _prompt_head_pieces(x_ref, cos_ref, sin_ref, dec_ref, qdec_ref, kdec_ref, cdec_ref,
                                   w_in, b_in, gn_g, gn_b, conv_w, conv_b, sret_ref,
                                   ubuf, ushift, qkv_scr, xb_scr, slot)
        tail = _post_mix_pieces(src, tail_w, lambda *r: _sort_tiles(*r, *outs))
        for piece in head + tail:
            piece()

    @pl.when(s >= n_tiles)
    def _append():
        rws = rws_ref[...]
        _sort_tiles(x1s_ref[...], rws[:, 0:1], rws[:, 1:2], rws[:, 2:3], rws[:, 3:4], *outs)

    @pl.when((s < n_tiles) & (li == tiles_per_seq - 1))
    def _conv_state():
        sconv_ref[0, 0] = ubuf[CONV_PAD - (CONV_WIDTH - 1):CONV_PAD, :]


def _prompt_head_pieces(x_ref, cos_ref, sin_ref, dec_ref, qdec_ref, kdec_ref, cdec_ref,
                        w_in, b_in, gn_g, gn_b, conv_w, conv_b, sret_ref,
                        ubuf, ushift, qkv_scr, xb_scr, slot):
    tl = x_ref.shape[1]
    st = {}

    def slab_dot(c0, c1):
        return jnp.dot(xb_scr[...], w_in[:, c0:c1], preferred_element_type=F32) + _rep(b_in[:, c0:c1], tl)

    def glu():
        xb_scr[...] = x_ref[0].astype(BF16)
        u = slab_dot(IN_OFFS[4], IN_OFFS[5]) * _sigmoid(slab_dot(IN_OFFS[5], IN_OFFS[6]))
        ubuf[CONV_PAD:CONV_PAD + tl, :] = u

    nsh = ushift.shape[1]
    span = nsh - (CONV_PAD - SUBLANES)

    def shift_span(h):
        for s in range(1, SUBLANES):
            ushift[s - 1] = ubuf[h * span + s:h * span + s + nsh, :]

    slab = 256
    slabs = [(kk, c0) for kk in (0, 1, 2, 3, 6, 7) for c0 in range(IN_OFFS[kk], IN_OFFS[kk + 1], slab)]
    rb = 32
    nrb = tl // rb

    def conv_block(r):
        h, rl = divmod(r * rb, span)
        acc = jnp.zeros((rb, CONV_CH), F32) + conv_b[...]
        for j in range(CONV_WIDTH):
            off = j + (CONV_PAD - (CONV_WIDTH - 1))
            s = off % SUBLANES
            base = rl + off - s
            win = (ubuf[h * span + base:h * span + base + rb, :] if s == 0
                   else ushift[s - 1, base:base + rb, :])
            acc = acc + _rep(conv_w[j], rb) * win
        slot["cout"][r * rb:(r + 1) * rb, :] = acc
        for kk, c0 in slabs[r * len(slabs) // nrb:(r + 1) * len(slabs) // nrb]:
            val = slab_dot(c0, c0 + slab)
            if kk in QKV_COLS:
                dst = QKV_COLS[kk] + c0 - IN_OFFS[kk]
                qkv_scr[:, dst:dst + slab] = val
            else:
                dst = GATE_COLS[kk] + c0 - IN_OFFS[kk]
                slot["gates"][:, dst:dst + slab] = val
        if r == nrb - 1:
            ubuf[0:CONV_PAD, :] = ubuf[tl:tl + CONV_PAD, :]

    scale = RET_DK ** -0.5

    def retention(c, h):
        rows = slice(c * RET_CHUNK, (c + 1) * RET_CHUNK)
        cols = slice(h * RET_DK, (h + 1) * RET_DK)
        hcol = lambda kk: slice(QKV_COLS[kk] + h * RET_DK, QKV_COLS[kk] + (h + 1) * RET_DK)
        cosf = cos_ref[rows, :]
        sinf = sin_ref[rows, :]
        qh = _rot(qkv_scr[rows, hcol(0)], cosf, sinf)
        kh = _rot(qkv_scr[rows, hcol(1)], cosf, sinf) * scale
        qb = qh.astype(BF16)
        kb = kh.astype(BF16)
        vb = qkv_scr[rows, hcol(2)].astype(BF16)
        s_old = sret_ref[0, 0, h]
        scores = lax.dot_general(qb, kb, (((1,), (1,)), ((), ())),
                                 preferred_element_type=F32) * dec_ref[h]
        inner = jnp.dot(scores.astype(BF16), vb, preferred_element_type=F32)
        cross = jnp.dot(qb, s_old.astype(BF16), preferred_element_type=F32) * qdec_ref[h]
        kd = (kh * kdec_ref[h]).astype(BF16)
        s_new = cdec_ref[h] * s_old + lax.dot_general(
            kd, vb, (((0,), (0,)), ((), ())), preferred_element_type=F32)
        sret_ref[0, 0, h] = s_new
        slot["ret"][rows, cols] = _ln(inner + cross, gn_g[:, cols], gn_b[:, cols])

    pieces = [glu]
    for r in range(nrb):
        if (r * rb) % span == 0:
            pieces.append(lambda h=(r * rb) // span: shift_span(h))
        pieces.append(lambda r=r: conv_block(r))
    pieces += [lambda c=c, h=h: retention(c, h) for c in range(tl // RET_CHUNK) for h in range(RET_HEADS)]
    return pieces


def _sample_mixer_kernel(x_ref, cos_ref, sin_ref, pdec_ref, qdec_ref, kdec_ref, cdec_ref, wsh_ref,
                         sret_in, sconv_in,
                         w_in, b_in, gn_g, gn_b, w_ret_o, conv_b, cln_g, cln_b,
                         w_conv_o, w_out, ln1_g, ln1_b, wr_hi, wr_lo, b_r,
                         x1_ref, rw_ref, sret_ref, sconv_ref,
                         ret_scr, cout_scr, xpad):
    t = x_ref.shape[0]
    ls = t // BB_SAMPLE
    x = x_ref[...]
    xb = x.astype(BF16)

    def proj(k):
        c0, c1 = IN_OFFS[k], IN_OFFS[k + 1]
        return jnp.dot(xb, w_in[:, c0:c1], preferred_element_type=F32) + _rep(b_in[:, c0:c1], t)

    q = proj(0)
    k = proj(1)
    v = proj(2)
    scale = RET_DK ** -0.5
    cosf = cos_ref[...]
    sinf = sin_ref[...]
    row = lax.broadcasted_iota(I32, (t, RET_DK), 0)
    pos = row % ls
    row8 = lax.broadcasted_iota(I32, (SUBLANES, RET_DK), 0)
    per_tile = SUBLANES // ls
    for h in range(RET_HEADS):
        cols = slice(h * RET_DK, (h + 1) * RET_DK)
        qh = _rot(q[:, cols], cosf, sinf)
        kh = _rot(k[:, cols], cosf, sinf) * scale
        vh = v[:, cols]
        inner = jnp.zeros((t, RET_DV), F32)
        for s in range(ls):
            ks = kh if s == 0 else pltpu.roll(kh, s, axis=0)
            vs = vh if s == 0 else pltpu.roll(vh, s, axis=0)
            dotp = jnp.sum(qh * ks, axis=1, keepdims=True) * pdec_ref[h, s]
            inner = inner + jnp.where(pos >= s, dotp, 0.0) * vs
        kd = kh * kdec_ref[h]
        for tile in range(t // SUBLANES):
            rows = slice(tile * SUBLANES, (tile + 1) * SUBLANES)
            q8 = qh[rows, :]
            kd8 = kd[rows, :]
            v8 = vh[rows, :]
            cross8 = jnp.zeros((SUBLANES, RET_DV), F32)
            for sub in range(per_tile):
                b = tile * per_tile + sub
                mine = (row8 >= sub * ls) & (row8 < (sub + 1) * ls)
                s_old = sret_in[0, b, h]
                c_b = jnp.dot(q8, s_old, preferred_element_type=F32)
                cross8 = jnp.where(mine, c_b, cross8)
                upd = lax.dot_general(jnp.where(mine, kd8, 0.0), v8, (((0,), (0,)), ((), ())),
                                      preferred_element_type=F32)
                sret_ref[0, b, h] = cdec_ref[h] * s_old + upd
            ret_scr[rows, cols] = inner[rows, :] + cross8 * qdec_ref[h, rows, :]
        ret_scr[:, cols] = _ln(ret_scr[:, cols], gn_g[:, cols], gn_b[:, cols])

    u = proj(4) * _sigmoid(proj(5))
    nstate = CONV_WIDTH - 1
    xpad[...] = jnp.zeros(xpad.shape, F32)
    xpad[:, 0:nstate, :] = sconv_in[0]
    for b in range(BB_SAMPLE):
        xpad[b, XPAD_NEW:XPAD_NEW + ls, :] = u[b * ls:(b + 1) * ls, :]
    for i in range(ls):
        res = jnp.sum(xpad[...] * wsh_ref[i][None], axis=1) + conv_b[...]
        for sl in range(CONV_CH // LANES):
            cout_scr[sl, pl.ds(i, BB_SAMPLE, stride=ls), :] = res[:, sl * LANES:(sl + 1) * LANES]
    sconv_ref[0, :, 0:nstate - ls, :] = xpad[:, ls:nstate, :]
    sconv_ref[0, :, nstate - ls:nstate, :] = xpad[:, XPAD_NEW:XPAD_NEW + ls, :]
    c_out = jnp.concatenate([cout_scr[sl] for sl in range(CONV_CH // LANES)], axis=1)

    def sink(x1, w1, w2, e1, e2):
        x1_ref[...] = x1
        rw_ref[...] = _lane_tile((w1, w2, e1, e2), t)

    src = dict(x=lambda: x, ret=lambda: ret_scr[...], cout=lambda: c_out,
               g=lambda: proj(3), gt_a=lambda: proj(6), gt_b=lambda: proj(7))
    for piece in _post_mix_pieces(
            src, (w_ret_o, cln_g, cln_b, w_conv_o, w_out, ln1_g, ln1_b, wr_hi, wr_lo, b_r), sink):
        piece()


def _ffn_kernel(te_ref, nvalid_ref, chunk_ref, xs_hbm, w_gu, w_dn, ys_hbm,
                xbuf, obuf, wgu_b, wdn_b, sem_in, sem_out):
    del xs_hbm
    i = pl.program_id(0)
    n = pl.num_programs(0)
    slot = i % 2
    nvalid = nvalid_ref[0]

    def chunk_rows(tile, c):
        return pl.ds(pl.multiple_of(chunk_ref[tile * TILE_CHUNKS + c] * CHUNK, CHUNK), CHUNK)

    def start_in(tile, s):
        for c in range(TILE_CHUNKS):
            pltpu.make_async_copy(ys_hbm.at[chunk_rows(tile, c)],
                                  xbuf.at[s, pl.ds(c * CHUNK, CHUNK)], sem_in.at[s]).start()

    def start_out(tile, s):
        for c in range(TILE_CHUNKS):
            pltpu.make_async_copy(obuf.at[s, pl.ds(c * CHUNK, CHUNK)],
                                  ys_hbm.at[chunk_rows(tile, c)], sem_out.at[s]).start()

    def wait_in(s):
        pltpu.make_async_copy(ys_hbm.at[pl.ds(0, TM_FFN)], xbuf.at[s], sem_in.at[s]).wait()

    def wait_out(s):
        pltpu.make_async_copy(obuf.at[s], ys_hbm.at[pl.ds(0, TM_FFN)], sem_out.at[s]).wait()

    @pl.when((i == 0) & (nvalid > 0))
    def _first():
        start_in(0, 0)

    @pl.when(i + 1 < nvalid)
    def _prefetch():
        start_in(i + 1, 1 - slot)

    @pl.when((i >= 2) & (i - 2 < nvalid))
    def _retire():
        wait_out(slot)

    @pl.when(i < nvalid)
    def _tile():
        wait_in(slot)
        prev = te_ref[jnp.maximum(i - 1, 0)]

        @pl.when((i == 0) | (te_ref[i] != prev))
        def _new_expert():
            wgu_b[...] = w_gu[0].astype(BF16)
            wdn_b[...] = w_dn[0].astype(BF16)

        x = xbuf[slot]
        y = jnp.zeros((TM_FFN, w_dn.shape[2]), F32)
        for c0 in range(0, EXP_FF, FFN_COLS):
            hg = jnp.dot(x, wgu_b[:, c0:c0 + FFN_COLS], preferred_element_type=F32)
            hu = jnp.dot(x, wgu_b[:, EXP_FF + c0:EXP_FF + c0 + FFN_COLS], preferred_element_type=F32)
            y = y + _bdot(_silu(hg) * hu, wdn_b[c0:c0 + FFN_COLS, :])
        obuf[slot] = y.astype(BF16)
        start_out(i, slot)

    @pl.when(i == n - 1)
    def _drain():
        @pl.when((i >= 1) & (i - 1 < nvalid))
        def _():
            wait_out(1 - slot)

        @pl.when(i < nvalid)
        def _():
            wait_out(slot)


def _final_kernel(ys_ref, x1_ref, rw_ref, pp_ref, ps_ref, ln2_g, ln2_b, w_pg, b_pg, w_ple,
                  yp_ref, ys_out_ref, *, n_prompt_tiles):
    i = pl.program_id(0)
    tl = x1_ref.shape[0]
    x1 = x1_ref[...]
    rw = rw_ref[...]
    w1, w2, pos1, pos2 = rw[:, 0:1], rw[:, 1:2], rw[:, 2:3], rw[:, 3:4]
    slot = lax.broadcasted_iota(I32, (tl, USED_ROWS), 1).astype(F32)
    ys = ys_ref[0:USED_ROWS, :]
    pick = lambda pos: jnp.dot((slot == pos).astype(BF16), ys, preferred_element_type=F32)
    moe = pick(pos1) * w1 + pick(pos2) * w2
    x2 = _ln(ALPHA * x1 + moe, ln2_g[...], ln2_b[...])
    gate = _sigmoid(_bdot(x2, w_pg[...]) + b_pg[...])
    p = jnp.where(i < n_prompt_tiles, pp_ref[...], ps_ref[...])
    y = x2 + gate * _bdot(p, w_ple[...])

    @pl.when(i < n_prompt_tiles)
    def _prompt():
        yp_ref[...] = y

    @pl.when(i >= n_prompt_tiles)
    def _sample():
        ys_out_ref[...] = y


def _rope_tables(pos):
    half = RET_DK // 2
    inv_freq = ROPE_BASE ** (-np.arange(half, dtype=np.float64) / half)
    ang = np.asarray(pos, np.float64)[:, None] * inv_freq[None, :]
    cos = np.cos(ang)
    sin = np.sin(ang)
    return (np.concatenate([cos, cos], axis=-1).astype(np.float32),
            np.concatenate([-sin, sin], axis=-1).astype(np.float32))


def _log_gamma():
    return np.log(1.0 - 2.0 ** (-5.0 - np.arange(RET_HEADS, dtype=np.float64)))


def _const_spec(shape):
    nd = len(shape)
    return pl.BlockSpec(shape, lambda *_: (0,) * nd, pipeline_mode=pl.Buffered(1))


def _chunk_plan(meta, n_blocks, n_ffn_tiles):
    assert n_blocks * BLOCK_SPARE >= N_EXPERTS * (TILE_CHUNKS - 1)
    m = meta.reshape(n_blocks, LANES, LANES)
    cnt = m[:, :N_EXPERTS, 0]
    off = m[:, :N_EXPERTS, 1]
    nch = (cnt + (CHUNK - 1)) // CHUNK
    cum = jnp.cumsum(nch, axis=0)
    total = cum[-1:]
    tiles_e = (total + TILE_CHUNKS - 1) // TILE_CHUNKS
    tile_end = jnp.cumsum(tiles_e, axis=1)
    tile_start = tile_end - tiles_e
    tid = jnp.arange(n_ffn_tiles, dtype=I32)[:, None]
    owner = (tid >= tile_start) & (tid < tile_end)
    pick_e = lambda v: jnp.sum(jnp.where(owner, v, 0), axis=1, keepdims=True)
    te = pick_e(jnp.arange(N_EXPERTS, dtype=I32)[None, :])
    k = (tid - pick_e(tile_start)) * TILE_CHUNKS + jnp.arange(TILE_CHUNKS, dtype=I32)[None, :]
    total_t = pick_e(total)
    real = k < total_t
    by_tile = lambda v: jnp.sum(jnp.where(owner[:, None, :], v[None, :, :], 0), axis=2)
    cum_t = by_tile(cum)
    blk = jnp.minimum(jnp.sum((cum_t[:, None, :] <= k[:, :, None]).astype(I32), axis=2), n_blocks - 1)
    at_blk = blk[:, :, None] == jnp.arange(n_blocks, dtype=I32)[None, None, :]
    pick_b = lambda v: jnp.sum(jnp.where(at_blk, v[:, None, :], 0), axis=2)
    excl = pick_b(cum_t - by_tile(nch))
    off_t = pick_b(by_tile(off))
    spare = te * (TILE_CHUNKS - 1) + jnp.maximum(k - total_t, 0) % TILE_CHUNKS
    spare_chunk = (spare // BLOCK_SPARE) * BLOCK_CHUNKS + BLOCK_USED + spare % BLOCK_SPARE
    chunk = jnp.where(real, blk * BLOCK_CHUNKS + off_t + (k - excl), spare_chunk)
    n_valid = jnp.sum(tiles_e, axis=1)
    te = jnp.where(tid < n_valid, te, N_EXPERTS - 1)
    return te.reshape(-1).astype(I32), n_valid.astype(I32), chunk.reshape(-1).astype(I32)


def kernel(x_prompt, x_sample, state_ret, state_conv, p_prompt, p_sample, w_in, b_in, ret_gn_g, ret_gn_b,
           w_ret_o, conv_w, conv_b, conv_ln_g, conv_ln_b, w_conv_o, w_out, ln1_g, ln1_b, w_grp, b_grp,
           w_exp, b_exp, w_gu, w_dn, ln2_g, ln2_b, w_pg, b_pg, w_ple):
    assert DEPTH == 1 and w_in.shape[0] == 1
    bp, lp, d = x_prompt.shape
    bs, ls, _ = x_sample.shape
    n_p, n_s = bp * lp, bs * ls
    n_tok = n_p + n_s
    assert lp % TL == 0 and n_s % TL == 0 and bs % BB_SAMPLE == 0 and SUBLANES % ls == 0
    n_blocks = n_tok // TL

    f32c = lambda a, shape: jnp.asarray(np.broadcast_to(a, shape).astype(np.float32))
    lg = _log_gamma()
    c = RET_CHUNK
    idx = np.arange(c, dtype=np.float64)
    rel = idx[:, None] - idx[None, :]
    causal = rel >= 0
    decay = np.where(causal[None], np.exp(np.where(causal, rel, 0.0)[None] * lg[:, None, None]), 0.0)
    decay = f32c(decay, decay.shape)
    q_decay = np.exp((idx[:, None] + 1.0) * lg[None, :])
    k_decay = np.exp((c - 1.0 - idx[:, None]) * lg[None, :])
    chunk_decay = np.exp(c * lg)
    qdec_p = f32c(q_decay.T[:, :, None], (RET_HEADS, c, RET_DK))
    kdec_p = f32c(k_decay.T[:, :, None], (RET_HEADS, c, RET_DK))
    cdec_p = f32c(chunk_decay[:, None, None], (RET_HEADS, 1, RET_DV))
    cos_p, sin_p = (jnp.asarray(a) for a in _rope_tables(np.arange(lp)))

    ts = BB_SAMPLE * ls
    idx_s = np.arange(ls, dtype=np.float64)
    pdec_s = np.exp(idx_s[None, :] * lg[:, None])
    pdec_s = f32c(pdec_s[:, :, None, None], (RET_HEADS, ls, 1, RET_DK))
    qd_s = np.exp((idx_s[:, None] + 1.0) * lg[None, :])
    kd_s = np.exp((ls - 1.0 - idx_s[:, None]) * lg[None, :])
    qdec_s = f32c(np.tile(qd_s.T, (1, BB_SAMPLE))[:, :, None], (RET_HEADS, ts, RET_DK))
    kdec_s = f32c(np.tile(kd_s.T, (1, BB_SAMPLE))[:, :, None], (RET_HEADS, ts, RET_DK))
    cdec_s = f32c(np.exp(ls * lg)[:, None, None], (RET_HEADS, 1, RET_DV))
    cos_s, sin_s = (jnp.asarray(a) for a in _rope_tables(np.tile(PAST_LEN + np.arange(ls), BB_SAMPLE)))

    w_in_b = w_in[0].astype(BF16)
    w_ret_o_b = w_ret_o[0].astype(BF16)
    w_conv_o_b = w_conv_o[0].astype(BF16)
    w_out_b = w_out[0].astype(BF16)
    w_pg_b = w_pg[0].astype(BF16)
    w_ple_b = w_ple[0].astype(BF16)
    w_r = jnp.zeros((d, LANES), F32).at[:, :N_GROUPS].set(w_grp[0]).at[:, N_GROUPS:N_GROUPS + N_EXPERTS].set(w_exp[0])
    wr_hi = w_r.astype(BF16)
    wr_lo = (w_r - wr_hi.astype(F32)).astype(BF16)
    b_r = jnp.zeros((1, LANES), F32).at[0, :N_GROUPS].set(b_grp[0]).at[0, N_GROUPS:N_GROUPS + N_EXPERTS].set(b_exp[0])
    row = lambda a: a.reshape(1, -1)
    conv_w0 = conv_w[0]
    nstate = CONV_WIDTH - 1
    win_row = np.array([m if m < nstate else XPAD_NEW + (m - nstate) for m in range(nstate + ls)])
    wsh = jnp.stack([jnp.zeros((XPAD_ROWS, CONV_CH), F32).at[win_row[i:i + CONV_WIDTH]].set(conv_w0)
                     for i in range(ls)])

    rep8 = lambda a: jnp.broadcast_to(a[..., None, :], a.shape[:-1] + (SUBLANES, a.shape[-1]))
    shared_w = (w_in_b, rep8(b_in[0]), row(ret_gn_g[0]), row(ret_gn_b[0]), w_ret_o_b)
    tail_w = (row(conv_ln_g[0]), row(conv_ln_b[0]), w_conv_o_b, w_out_b, row(ln1_g[0]), row(ln1_b[0]),
              wr_hi, wr_lo, b_r)

    nbt = bs // BB_SAMPLE
    xs2 = x_sample.reshape(n_s, d)
    sample_in = ((xs2, cos_s, sin_s, pdec_s, qdec_s, kdec_s, cdec_s, wsh, state_ret, state_conv)
                 + shared_w + (row(conv_b[0]),) + tail_w)
    sample_specs = (
        [pl.BlockSpec((ts, d), lambda i: (i, 0))]
        + [_const_spec(a.shape) for a in sample_in[1:8]]
        + [pl.BlockSpec((1, BB_SAMPLE, RET_HEADS, RET_DK, RET_DV), lambda i: (0, i, 0, 0, 0)),
           pl.BlockSpec((1, BB_SAMPLE, nstate, CONV_CH), lambda i: (0, i, 0, 0))]
        + [_const_spec(a.shape) for a in sample_in[10:]]
    )
    tok_spec_s = lambda w: pl.BlockSpec((ts, w), lambda i: (i, 0))
    x1_s, rw_s, ret_s, conv_s = pl.pallas_call(
        _sample_mixer_kernel,
        grid=(nbt,),
        in_specs=sample_specs,
        out_specs=[
            tok_spec_s(d), tok_spec_s(LANES),
            pl.BlockSpec((1, BB_SAMPLE, RET_HEADS, RET_DK, RET_DV), lambda i: (0, i, 0, 0, 0)),
            pl.BlockSpec((1, BB_SAMPLE, nstate, CONV_CH), lambda i: (0, i, 0, 0)),
        ],
        out_shape=[
            jax.ShapeDtypeStruct((n_s, d), F32),
            jax.ShapeDtypeStruct((n_s, LANES), F32),
            jax.ShapeDtypeStruct(state_ret.shape, F32),
            jax.ShapeDtypeStruct(state_conv.shape, F32),
        ],
        scratch_shapes=[
            pltpu.VMEM((ts, RET_V), F32),
            pltpu.VMEM((CONV_CH // LANES, ts, LANES), F32),
            pltpu.VMEM((BB_SAMPLE, XPAD_ROWS, CONV_CH), F32),
        ],
        compiler_params=pltpu.CompilerParams(
            dimension_semantics=("arbitrary",), vmem_limit_bytes=VMEM_LIMIT),
        name="sample_mixer",
    )(*sample_in)

    assert lp % TLM == 0 and n_s % TLM == 0 and TLM % TL == 0
    nlt = lp // TLM
    npt = n_p // TLM
    nst = n_s // TLM
    sub = TLM // TL
    prompt_in = ((x_prompt, x1_s, rw_s, cos_p, sin_p, decay, qdec_p, kdec_p, cdec_p)
                 + shared_w + (rep8(conv_w0), row(conv_b[0])) + tail_w)
    head_tile = lambda s: jnp.minimum(s, npt - 1)
    sample_tile = lambda s: jnp.maximum(s - npt, 0)
    sample_spec = lambda w: pl.BlockSpec((TLM, w), lambda s: (sample_tile(s), 0))
    prompt_specs = [
        pl.BlockSpec((1, TLM, d), lambda s: (head_tile(s) // nlt, head_tile(s) % nlt, 0)),
        sample_spec(d), sample_spec(LANES),
        pl.BlockSpec((TLM, RET_DK), lambda s: (head_tile(s) % nlt, 0)),
        pl.BlockSpec((TLM, RET_DK), lambda s: (head_tile(s) % nlt, 0)),
    ] + [_const_spec(a.shape) for a in prompt_in[5:]]
    tok_spec_p = lambda rows, w: pl.BlockSpec((rows, w), lambda s: (s, 0))
    x1_all, rw_all, xs_all, meta, ret_p, conv_p = pl.pallas_call(
        functools.partial(_prompt_mixer_kernel, n_tiles=npt, tiles_per_seq=nlt),
        grid=(npt + nst,),
        in_specs=prompt_specs,
        out_specs=[
            tok_spec_p(TLM, d), tok_spec_p(TLM, LANES), tok_spec_p(sub * CAP, d), tok_spec_p(sub * LANES, LANES),
            pl.BlockSpec((1, 1, RET_HEADS, RET_DK, RET_DV), lambda s: (0, head_tile(s) // nlt, 0, 0, 0)),
            pl.BlockSpec((1, 1, nstate, CONV_CH), lambda s: (0, head_tile(s) // nlt, 0, 0)),
        ],
        out_shape=[
            jax.ShapeDtypeStruct((n_tok, d), F32),
            jax.ShapeDtypeStruct((n_tok, LANES), F32),
            jax.ShapeDtypeStruct((n_blocks * CAP, d), BF16),
            jax.ShapeDtypeStruct((n_blocks * LANES, LANES), I32),
            jax.ShapeDtypeStruct((1, bp, RET_HEADS, RET_DK, RET_DV), F32),
            jax.ShapeDtypeStruct((1, bp, nstate, CONV_CH), F32),
        ],
        scratch_shapes=[
            pltpu.VMEM((TLM + CONV_PAD, CONV_CH), F32),
            pltpu.VMEM((SUBLANES - 1, TL + CONV_PAD - SUBLANES, CONV_CH), F32),
            pltpu.VMEM((TLM, 2 * RET_QK + RET_V), F32),
            pltpu.VMEM((TLM, d), BF16),
            pltpu.VMEM((TLM, RET_V), F32),
            pltpu.VMEM((TLM, CONV_CH), F32),
            pltpu.VMEM((TLM, RET_V + 2 * D_MODEL), F32),
        ],
        compiler_params=pltpu.CompilerParams(
            dimension_semantics=("arbitrary",), vmem_limit_bytes=VMEM_LIMIT),
        name="prompt_mixer",
    )(*prompt_in)

    max_chunks = n_blocks * (TOP_K * TL // CHUNK + N_EXPERTS - 1)
    n_ffn_tiles = (max_chunks + N_EXPERTS * (TILE_CHUNKS - 1)) // TILE_CHUNKS
    tile_e, n_valid_tiles, chunk_ids = _chunk_plan(meta, n_blocks, n_ffn_tiles)

    ys_all = pl.pallas_call(
        _ffn_kernel,
        grid_spec=pltpu.PrefetchScalarGridSpec(
            num_scalar_prefetch=3,
            grid=(n_ffn_tiles,),
            in_specs=[
                pl.BlockSpec(memory_space=pl.ANY),
                pl.BlockSpec((1, d, 2 * EXP_FF), lambda i, te, nr, ch: (te[i], 0, 0)),
                pl.BlockSpec((1, EXP_FF, d), lambda i, te, nr, ch: (te[i], 0, 0)),
            ],
            out_specs=pl.BlockSpec(memory_space=pl.ANY),
            scratch_shapes=[
                pltpu.VMEM((2, TM_FFN, d), BF16),
                pltpu.VMEM((2, TM_FFN, d), BF16),
                pltpu.VMEM((d, 2 * EXP_FF), BF16),
                pltpu.VMEM((EXP_FF, d), BF16),
                pltpu.SemaphoreType.DMA((2,)),
                pltpu.SemaphoreType.DMA((2,)),
            ],
        ),
        out_shape=jax.ShapeDtypeStruct(xs_all.shape, BF16),
        input_output_aliases={3: 0},
        compiler_params=pltpu.CompilerParams(
            dimension_semantics=("arbitrary",), vmem_limit_bytes=VMEM_LIMIT),
        name="expert_ffn",
    )(tile_e, n_valid_tiles, chunk_ids, xs_all, w_gu[0], w_dn[0])

    npt = n_p // TL
    pp2 = p_prompt.reshape(n_p, PLE_DIM)
    ps2 = p_sample.reshape(n_s, PLE_DIM)
    tok_f = lambda rows, w: pl.BlockSpec((rows, w), lambda i: (i, 0))
    y_p, y_s = pl.pallas_call(
        functools.partial(_final_kernel, n_prompt_tiles=npt),
        grid=(n_blocks,),
        in_specs=[
            tok_f(CAP, d), tok_f(TL, d), tok_f(TL, LANES),
            pl.BlockSpec((TL, PLE_DIM), lambda i: (jnp.minimum(i, npt - 1), 0)),
            pl.BlockSpec((TL, PLE_DIM), lambda i: (jnp.maximum(i - npt, 0), 0)),
            _const_spec((1, d)), _const_spec((1, d)), _const_spec((d, d)), _const_spec((1, d)),
            _const_spec((PLE_DIM, d)),
        ],
        out_specs=[
            pl.BlockSpec((TL, d), lambda i: (jnp.minimum(i, npt - 1), 0)),
            pl.BlockSpec((TL, d), lambda i: (jnp.maximum(i - npt, 0), 0)),
        ],
        out_shape=[jax.ShapeDtypeStruct((n_p, d), F32), jax.ShapeDtypeStruct((n_s, d), F32)],
        compiler_params=pltpu.CompilerParams(
            dimension_semantics=("arbitrary",), vmem_limit_bytes=VMEM_LIMIT),
        name="moe_combine_final",
    )(ys_all, x1_all, rw_all, pp2, ps2,
      row(ln2_g[0]), row(ln2_b[0]), w_pg_b, row(b_pg[0]), w_ple_b)

    return (y_p.reshape(bp, lp, d), y_s.reshape(bs, ls, d), ret_p, conv_p, ret_s, conv_s)
```

```python
import functools

import jax
import jax.numpy as jnp
import numpy as np
from jax import lax
from jax.experimental import pallas as pl
from jax.experimental.pallas import tpu as pltpu

F32 = jnp.float32
BF16 = jnp.bfloat16
I32 = jnp.int32
U32 = jnp.uint32

D_MODEL = 1024
PAST_LEN = 16384
RET_HEADS = 4
RET_DK = 128
RET_DV = 128
RET_QK = RET_HEADS * RET_DK
RET_V = RET_HEADS * RET_DV
RET_CHUNK = 128
ROPE_BASE = 10000.0
CONV_CH = 512
CONV_WIDTH = 31
N_GROUPS = 4
EXP_PER_GROUP = 4
N_EXPERTS = N_GROUPS * EXP_PER_GROUP
TOP_K = 2
EXP_FF = 512
PLE_DIM = 256
DEPTH = 1
ALPHA = (2 * DEPTH) ** 0.25
LN_EPS = 1e-5
IN_WIDTHS = (RET_QK, RET_QK, RET_V, RET_V, CONV_CH, CONV_CH, D_MODEL, D_MODEL)
IN_OFFS = tuple(int(s) for s in np.cumsum((0,) + IN_WIDTHS))

LANES = 128
SUBLANES = 8
VMEM_LIMIT = 56 * 1024 * 1024

TL = 256
TLM = 512
TLF = 512
BB_SAMPLE = 16
CHUNK = 2 * SUBLANES
TILE_CHUNKS = 32
BLOCK_USED = -(-(TOP_K * TL + N_EXPERTS * (CHUNK - 1)) // LANES) * LANES // CHUNK
BLOCK_SPARE = LANES // CHUNK
BLOCK_CHUNKS = BLOCK_USED + BLOCK_SPARE
USED_ROWS = BLOCK_USED * CHUNK
CAP = BLOCK_CHUNKS * CHUNK
TM_FFN = TILE_CHUNKS * CHUNK
FFN_COLS = 256
CONV_PAD = 32
XPAD_NEW = 32
XPAD_ROWS = 40


def _ln(x, g, b):
    mu = jnp.mean(x, axis=-1, keepdims=True)
    d = x - mu
    var = jnp.mean(d * d, axis=-1, keepdims=True)
    return d * lax.rsqrt(var + LN_EPS) * g + b


def _sigmoid(x):
    return 1.0 / (1.0 + jnp.exp(-x))


def _rep(v8, rows):
    return v8 if rows == SUBLANES else jnp.concatenate([v8] * (rows // SUBLANES), axis=0)


def _silu(x):
    return x * _sigmoid(x)


def _bdot(a, b):
    return jnp.dot(a.astype(BF16), b, preferred_element_type=F32)


def _rot(t, cosf, sinf):
    return t * cosf + pltpu.roll(t, RET_DK // 2, axis=1) * sinf


def _lane_tile(cols, rows):
    lane = lax.broadcasted_iota(I32, (rows, LANES), 1)
    out = jnp.zeros((rows, LANES), F32)
    for i, col in enumerate(cols):
        out = jnp.where(lane == i, col, out)
    return out


def _route(logits):
    lane = lax.broadcasted_iota(I32, logits.shape, 1)
    lanef = lane.astype(F32)
    ninf = jnp.float32(-jnp.inf)
    big = jnp.float32(LANES)
    gmask = lane < N_GROUPS
    gl = jnp.where(gmask, logits, ninf)
    gmax = jnp.max(gl, axis=1, keepdims=True)
    gidx = jnp.min(jnp.where(gmask & (gl == gmax), lanef, big), axis=1, keepdims=True)
    sumexp = jnp.sum(jnp.where(gmask, jnp.exp(gl - gmax), 0.0), axis=1, keepdims=True)
    gw = 1.0 / sumexp
    lo = N_GROUPS + EXP_PER_GROUP * gidx
    emask = (lanef >= lo) & (lanef < lo + EXP_PER_GROUP)
    el = jnp.where(emask, logits, ninf)
    m1 = jnp.max(el, axis=1, keepdims=True)
    i1 = jnp.min(jnp.where(emask & (el == m1), lanef, big), axis=1, keepdims=True)
    emask2 = emask & (lanef != i1)
    el2 = jnp.where(emask2, logits, ninf)
    m2 = jnp.max(el2, axis=1, keepdims=True)
    i2 = jnp.min(jnp.where(emask2 & (el2 == m2), lanef, big), axis=1, keepdims=True)
    t = jnp.exp(m2 - m1)
    den = 1.0 + t
    return (1.0 / den) * gw, (t / den) * gw, i1 - N_GROUPS, i2 - N_GROUPS


def _post_mix_pieces(src, w, sink):
    (w_ret_o, cln_g, cln_b, w_conv_o, w_out, ln1_g, ln1_b, wr_hi, wr_lo, b_r) = w
    st = {}

    def branch_a():
        st["a"] = _bdot(_silu(src["g"]()) * src["ret"](), w_ret_o[...])

    def branch_b():
        st["b"] = _bdot(_silu(_ln(src["cout"](), cln_g[...], cln_b[...])), w_conv_o[...])

    def merge():
        mix = _sigmoid(src["gt_a"]()) * st["a"] + _sigmoid(src["gt_b"]()) * st["b"]
        h = ALPHA * src["x"]() + _bdot(mix, w_out[...])
        st["x1"] = _ln(h, ln1_g[...], ln1_b[...])

    def router():
        x1 = st["x1"]
        x1_hi = x1.astype(BF16)
        x1_lo = (x1 - x1_hi.astype(F32)).astype(BF16)
        st["logits"] = (jnp.dot(x1_hi, wr_hi[...], preferred_element_type=F32)
                        + (jnp.dot(x1_lo, wr_hi[...], preferred_element_type=F32)
                           + jnp.dot(x1_hi, wr_lo[...], preferred_element_type=F32))
                        + b_r[...])

    def route():
        st["route"] = _route(st["logits"])

    def finish():
        sink(st["x1"], *st["route"])

    return [branch_a, branch_b, merge, router, route, finish]


def _interleave(a, b):
    j = 0
    for i, piece in enumerate(a):
        piece()
        while j < len(b) and (j + 1) * len(a) <= (i + 1) * len(b):
            b[j]()
            j += 1
    for piece in b[j:]:
        piece()


def _sort_tile(x1, w1, w2, e1, e2, x1_ref, rw_ref, xs_ref, meta_ref):
    t = x1.shape[0]
    ids_t = _lane_tile((e1, e2), t).T
    e1r, e2r = ids_t[0:1, :], ids_t[1:2, :]
    sub = lax.broadcasted_iota(I32, (LANES, t), 0).astype(F32)
    a1 = (sub == e1r).astype(F32)
    a2 = (sub == e2r).astype(F32)
    ri = lax.broadcasted_iota(I32, (t, t), 0)
    ci = lax.broadcasted_iota(I32, (t, t), 1)
    earlier = (ri < ci).astype(BF16)
    r1 = jnp.dot(a1.astype(BF16), earlier, preferred_element_type=F32)
    r2 = jnp.dot(a2.astype(BF16), earlier, preferred_element_type=F32)
    cnt1 = jnp.sum(a1, axis=1, keepdims=True)
    cnt = cnt1 + jnp.sum(a2, axis=1, keepdims=True)
    nch = jnp.floor((cnt + (CHUNK - 1.0)) * (1.0 / CHUNK))
    ui = lax.broadcasted_iota(I32, (LANES, LANES), 0)
    uj = lax.broadcasted_iota(I32, (LANES, LANES), 1)
    before = (uj < ui).astype(BF16)
    off = jnp.dot(before, jnp.broadcast_to(nch, (LANES, LANES)).astype(BF16),
                  preferred_element_type=F32)[:, 0:1]
    base = off * CHUNK
    pos1r = jnp.sum(a1 * (base + r1), axis=0, keepdims=True)
    pos2r = jnp.sum(a2 * (base + cnt1 + r2), axis=0, keepdims=True)
    slot = lax.broadcasted_iota(I32, (CAP, t), 0).astype(F32)
    onehot = ((slot == pos1r) | (slot == pos2r)).astype(BF16)
    xs = jnp.dot(onehot, x1.astype(BF16), preferred_element_type=F32)
    pos_cols = jnp.where(sub == 2.0, pos1r, jnp.where(sub == 3.0, pos2r, 0.0)).T
    lane = lax.broadcasted_iota(I32, (t, LANES), 1)
    x1_ref[...] = x1
    rw_ref[...] = jnp.where(lane == 0, w1, jnp.where(lane == 1, w2, pos_cols))
    xs_ref[...] = xs.astype(BF16)
    mlane = lax.broadcasted_iota(I32, (LANES, LANES), 1)
    meta = jnp.where(mlane == 0, cnt, jnp.where(mlane == 1, off, 0.0))
    meta_ref[...] = meta.astype(I32)


def _sort_tiles(x1, w1, w2, e1, e2, x1_ref, rw_ref, xs_ref, meta_ref):
    for i in range(x1.shape[0] // TL):
        rows = slice(i * TL, (i + 1) * TL)
        _sort_tile(x1[rows], w1[rows], w2[rows], e1[rows], e2[rows],
                   x1_ref.at[pl.ds(i * TL, TL)], rw_ref.at[pl.ds(i * TL, TL)],
                   xs_ref.at[pl.ds(i * CAP, CAP)], meta_ref.at[pl.ds(i * LANES, LANES)])


GATE_COLS = {3: 0, 6: RET_V, 7: RET_V + D_MODEL}
QKV_COLS = {0: 0, 1: RET_QK, 2: 2 * RET_QK}


def _prompt_mixer_kernel(x_ref, x1s_ref, rws_ref, cos_ref, sin_ref, dec_ref, qdec_ref, kdec_ref, cdec_ref,
                         w_in, b_in, gn_g, gn_b, w_ret_o, conv_w, conv_b, cln_g, cln_b,
                         w_conv_o, w_out, ln1_g, ln1_b, wr_hi, wr_lo, b_r,
                         x1_ref, rw_ref, xs_ref, meta_ref, sret_ref, sconv_ref,
                         ubuf, ushift, qkv_scr, xb_scr, ret_scr, cout_scr, gate_scr,
                         *, n_tiles, tiles_per_seq):
    s = pl.program_id(0)
    li = lax.rem(s, tiles_per_seq)
    outs = (x1_ref, rw_ref, xs_ref, meta_ref)
    slot = dict(ret=ret_scr, cout=cout_scr, gates=gate_scr)
    tail_w = (w_ret_o, cln_g, cln_b, w_conv_o, w_out, ln1_g, ln1_b, wr_hi, wr_lo, b_r)

    @pl.when((s < n_tiles) & (li == 0))
    def _new_sequence():
        sret_ref[...] = jnp.zeros(sret_ref.shape, F32)
        ubuf[0:CONV_PAD, :] = jnp.zeros((CONV_PAD, CONV_CH), F32)

    @pl.when(s < n_tiles)
    def _mix():
        gcols = lambda kk: slice(GATE_COLS[kk], GATE_COLS[kk] + IN_WIDTHS[kk])
        src = dict(x=lambda: x_ref[0], ret=lambda: ret_scr[...], cout=lambda: cout_scr[...],
                   g=lambda: gate_scr[:, gcols(3)], gt_a=lambda: gate_scr[:, gcols(6)],
                   gt_b=lambda: gate_scr[:, gcols(7)])
        head = _prompt_head_pieces(x_ref, cos_ref, sin_ref, dec_ref, qdec_ref, kdec_ref, cdec_ref,
                                   w_in, b_in, gn_g, gn_b, conv_w, conv_b, sret_ref,
                                   ubuf, ushift, qkv_scr, xb_scr, slot)
        tail = _post_mix_pieces(src, tail_w, lambda *r: _sort_tiles(*r, *outs))
        for piece in head + tail:
            piece()

    @pl.when(s >= n_tiles)
    def _append():
        rws = rws_ref[...]
        _sort_tiles(x1s_ref[...], rws[:, 0:1], rws[:, 1:2], rws[:, 2:3], rws[:, 3:4], *outs)

    @pl.when((s < n_tiles) & (li == tiles_per_seq - 1))
    def _conv_state():
        sconv_ref[0, 0] = ubuf[CONV_PAD - (CONV_WIDTH - 1):CONV_PAD, :]


def _prompt_head_pieces(x_ref, cos_ref, sin_ref, dec_ref, qdec_ref, kdec_ref, cdec_ref,
                        w_in, b_in, gn_g, gn_b, conv_w, conv_b, sret_ref,
                        ubuf, ushift, qkv_scr, xb_scr, slot):
    tl = x_ref.shape[1]
    st = {}

    def slab_dot(c0, c1):
        return jnp.dot(xb_scr[...], w_in[:, c0:c1], preferred_element_type=F32) + _rep(b_in[:, c0:c1], tl)

    def glu():
        xb_scr[...] = x_ref[0].astype(BF16)
        u = slab_dot(IN_OFFS[4], IN_OFFS[5]) * _sigmoid(slab_dot(IN_OFFS[5], IN_OFFS[6]))
        ubuf[CONV_PAD:CONV_PAD + tl, :] = u

    nsh = ushift.shape[1]
    span = nsh - (CONV_PAD - SUBLANES)

    def shift_span(h):
        for s in range(1, SUBLANES):
            ushift[s - 1] = ubuf[h * span + s:h * span + s + nsh, :]

    slab = 256
    slabs = [(kk, c0) for kk in (0, 1, 2, 3, 6, 7) for c0 in range(IN_OFFS[kk], IN_OFFS[kk + 1], slab)]
    rb = 32
    nrb = tl // rb

    def conv_block(r):
        h, rl = divmod(r * rb, span)
        acc = jnp.zeros((rb, CONV_CH), F32) + conv_b[...]
        for j in range(CONV_WIDTH):
            off = j + (CONV_PAD - (CONV_WIDTH - 1))
            s = off % SUBLANES
            base = rl + off - s
            win = (ubuf[h * span + base:h * span + base + rb, :] if s == 0
                   else ushift[s - 1, base:base + rb, :])
            acc = acc + _rep(conv_w[j], rb) * win
        slot["cout"][r * rb:(r + 1) * rb, :] = acc
        for kk, c0 in slabs[r * len(slabs) // nrb:(r + 1) * len(slabs) // nrb]:
            val = slab_dot(c0, c0 + slab)
            if kk in QKV_COLS:
                dst = QKV_COLS[kk] + c0 - IN_OFFS[kk]
                qkv_scr[:, dst:dst + slab] = val
            else:
                dst = GATE_COLS[kk] + c0 - IN_OFFS[kk]
                slot["gates"][:, dst:dst + slab] = val
        if r == nrb - 1:
            ubuf[0:CONV_PAD, :] = ubuf[tl:tl + CONV_PAD, :]

    scale = RET_DK ** -0.5

    def retention(c, h):
        rows = slice(c * RET_CHUNK, (c + 1) * RET_CHUNK)
        cols = slice(h * RET_DK, (h + 1) * RET_DK)
        hcol = lambda kk: slice(QKV_COLS[kk] + h * RET_DK, QKV_COLS[kk] + (h + 1) * RET_DK)
        cosf = cos_ref[rows, :]
        sinf = sin_ref[rows, :]
        qh = _rot(qkv_scr[rows, hcol(0)], cosf, sinf)
        kh = _rot(qkv_scr[rows, hcol(1)], cosf, sinf) * scale
        qb = qh.astype(BF16)
        kb = kh.astype(BF16)
        vb = qkv_scr[rows, hcol(2)].astype(BF16)
        s_old = sret_ref[0, 0, h]
        scores = lax.dot_general(qb, kb, (((1,), (1,)), ((), ())),
                                 preferred_element_type=F32) * dec_ref[h]
        inner = jnp.dot(scores.astype(BF16), vb, preferred_element_type=F32)
        cross = jnp.dot(qb, s_old.astype(BF16), preferred_element_type=F32) * qdec_ref[h]
        kd = (kh * kdec_ref[h]).astype(BF16)
        s_new = cdec_ref[h] * s_old + lax.dot_general(
            kd, vb, (((0,), (0,)), ((), ())), preferred_element_type=F32)
        sret_ref[0, 0, h] = s_new
        slot["ret"][rows, cols] = _ln(inner + cross, gn_g[:, cols], gn_b[:, cols])

    pieces = [glu]
    for r in range(nrb):
        if (r * rb) % span == 0:
            pieces.append(lambda h=(r * rb) // span: shift_span(h))
        pieces.append(lambda r=r: conv_block(r))
    pieces += [lambda c=c, h=h: retention(c, h) for c in range(tl // RET_CHUNK) for h in range(RET_HEADS)]
    return pieces


def _sample_mixer_kernel(x_ref, cos_ref, sin_ref, pdec_ref, qdec_ref, kdec_ref, cdec_ref, wsh_ref,
                         sret_in, sconv_in,
                         w_in, b_in, gn_g, gn_b, w_ret_o, conv_b, cln_g, cln_b,
                         w_conv_o, w_out, ln1_g, ln1_b, wr_hi, wr_lo, b_r,
                         x1_ref, rw_ref, sret_ref, sconv_ref,
                         ret_scr, cout_scr, xpad):
    t = x_ref.shape[0]
    ls = t // BB_SAMPLE
    x = x_ref[...]
    xb = x.astype(BF16)

    def proj(k):
        c0, c1 = IN_OFFS[k], IN_OFFS[k + 1]
        return jnp.dot(xb, w_in[:, c0:c1], preferred_element_type=F32) + _rep(b_in[:, c0:c1], t)

    q = proj(0)
    k = proj(1)
    v = proj(2)
    scale = RET_DK ** -0.5
    cosf = cos_ref[...]
    sinf = sin_ref[...]
    row = lax.broadcasted_iota(I32, (t, RET_DK), 0)
    pos = row % ls
    row8 = lax.broadcasted_iota(I32, (SUBLANES, RET_DK), 0)
    per_tile = SUBLANES // ls
    for h in range(RET_HEADS):
        cols = slice(h * RET_DK, (h + 1) * RET_DK)
        qh = _rot(q[:, cols], cosf, sinf)
        kh = _rot(k[:, cols], cosf, sinf) * scale
        vh = v[:, cols]
        inner = jnp.zeros((t, RET_DV), F32)
        for s in range(ls):
            ks = kh if s == 0 else pltpu.roll(kh, s, axis=0)
            vs = vh if s == 0 else pltpu.roll(vh, s, axis=0)
            dotp = jnp.sum(qh * ks, axis=1, keepdims=True) * pdec_ref[h, s]
            inner = inner + jnp.where(pos >= s, dotp, 0.0) * vs
        kd = kh * kdec_ref[h]
        for tile in range(t // SUBLANES):
            rows = slice(tile * SUBLANES, (tile + 1) * SUBLANES)
            q8 = qh[rows, :]
            kd8 = kd[rows, :]
            v8 = vh[rows, :]
            seqs = [tile * per_tile + sub for sub in range(per_tile)]
            mine = [(row8 >= sub * ls) & (row8 < (sub + 1) * ls) for sub in range(per_tile)]
            s_old = [sret_in[0, b, h] for b in seqs]
            c_all = jnp.dot(q8, jnp.concatenate(s_old, axis=1), preferred_element_type=F32)
            upd = lax.dot_general(jnp.concatenate([jnp.where(m, kd8, 0.0) for m in mine], axis=1), v8,
                                  (((0,), (0,)), ((), ())), preferred_element_type=F32)
            cross8 = jnp.zeros((SUBLANES, RET_DV), F32)
            for sub, b in enumerate(seqs):
                cross8 = jnp.where(mine[sub], c_all[:, sub * RET_DV:(sub + 1) * RET_DV], cross8)
                sret_ref[0, b, h] = cdec_ref[h] * s_old[sub] + upd[sub * RET_DK:(sub + 1) * RET_DK, :]
            ret_scr[rows, cols] = inner[rows, :] + cross8 * qdec_ref[h, rows, :]
        ret_scr[:, cols] = _ln(ret_scr[:, cols], gn_g[:, cols], gn_b[:, cols])

    u = proj(4) * _sigmoid(proj(5))
    nstate = CONV_WIDTH - 1
    xpad[...] = jnp.zeros(xpad.shape, F32)
    xpad[:, 0:nstate, :] = sconv_in[0]
    for b in range(BB_SAMPLE):
        xpad[b, XPAD_NEW:XPAD_NEW + ls, :] = u[b * ls:(b + 1) * ls, :]
    for p in range(ls):
        res = jnp.sum(xpad[...] * wsh_ref[p][None], axis=1) + conv_b[...]
        for sl in range(CONV_CH // LANES):
            cout_scr[sl, pl.ds(p, BB_SAMPLE, stride=ls), :] = res[:, sl * LANES:(sl + 1) * LANES]
    sconv_ref[0, :, 0:nstate - ls, :] = xpad[:, ls:nstate, :]
    sconv_ref[0, :, nstate - ls:nstate, :] = xpad[:, XPAD_NEW:XPAD_NEW + ls, :]
    c_out = jnp.concatenate([cout_scr[sl] for sl in range(CONV_CH // LANES)], axis=1)

    def sink(x1, w1, w2, e1, e2):
        x1_ref[...] = x1
        rw_ref[...] = _lane_tile((w1, w2, e1, e2), t)

    src = dict(x=lambda: x, ret=lambda: ret_scr[...], cout=lambda: c_out,
               g=lambda: proj(3), gt_a=lambda: proj(6), gt_b=lambda: proj(7))
    for piece in _post_mix_pieces(
            src, (w_ret_o, cln_g, cln_b, w_conv_o, w_out, ln1_g, ln1_b, wr_hi, wr_lo, b_r), sink):
        piece()


def _ffn_kernel(te_ref, nvalid_ref, chunk_ref, xs_hbm, w_gu, w_dn, ys_hbm,
                xbuf, obuf, wgu_b, wdn_b, sem_in, sem_out):
    del xs_hbm
    i = pl.program_id(0)
    n = pl.num_programs(0)
    slot = i % 2
    nvalid = nvalid_ref[0]

    def chunk_rows(tile, c):
        return pl.ds(pl.multiple_of(chunk_ref[tile * TILE_CHUNKS + c] * CHUNK, CHUNK), CHUNK)

    def start_in(tile, s):
        for c in range(TILE_CHUNKS):
            pltpu.make_async_copy(ys_hbm.at[chunk_rows(tile, c)],
                                  xbuf.at[s, pl.ds(c * CHUNK, CHUNK)], sem_in.at[s]).start()

    def start_out(tile, s):
        for c in range(TILE_CHUNKS):
            pltpu.make_async_copy(obuf.at[s, pl.ds(c * CHUNK, CHUNK)],
                                  ys_hbm.at[chunk_rows(tile, c)], sem_out.at[s]).start()

    def wait_in(s):
        pltpu.make_async_copy(ys_hbm.at[pl.ds(0, TM_FFN)], xbuf.at[s], sem_in.at[s]).wait()

    def wait_out(s):
        pltpu.make_async_copy(obuf.at[s], ys_hbm.at[pl.ds(0, TM_FFN)], sem_out.at[s]).wait()

    @pl.when((i == 0) & (nvalid > 0))
    def _first():
        start_in(0, 0)

    @pl.when(i + 1 < nvalid)
    def _prefetch():
        start_in(i + 1, 1 - slot)

    @pl.when((i >= 2) & (i - 2 < nvalid))
    def _retire():
        wait_out(slot)

    @pl.when(i < nvalid)
    def _tile():
        wait_in(slot)
        prev = te_ref[jnp.maximum(i - 1, 0)]

        @pl.when((i == 0) | (te_ref[i] != prev))
        def _new_expert():
            wgu_b[...] = w_gu[0].astype(BF16)
            wdn_b[...] = w_dn[0].astype(BF16)

        x = xbuf[slot]
        y = jnp.zeros((TM_FFN, w_dn.shape[2]), F32)
        for c0 in range(0, EXP_FF, FFN_COLS):
            hg = jnp.dot(x, wgu_b[:, c0:c0 + FFN_COLS], preferred_element_type=F32)
            hu = jnp.dot(x, wgu_b[:, EXP_FF + c0:EXP_FF + c0 + FFN_COLS], preferred_element_type=F32)
            y = y + _bdot(_silu(hg) * hu, wdn_b[c0:c0 + FFN_COLS, :])
        obuf[slot] = y.astype(BF16)
        start_out(i, slot)

    @pl.when(i == n - 1)
    def _drain():
        @pl.when((i >= 1) & (i - 1 < nvalid))
        def _():
            wait_out(1 - slot)

        @pl.when(i < nvalid)
        def _():
            wait_out(slot)


def _final_kernel(ys_ref, x1_ref, rw_ref, pp_ref, ps_ref, ln2_g, ln2_b, w_pg, b_pg, w_ple,
                  yp_ref, ys_out_ref, *, n_prompt_tiles):
    i = pl.program_id(0)
    x1 = x1_ref[...]
    slot = lax.broadcasted_iota(I32, (TL, USED_ROWS), 1).astype(F32)
    parts = []
    for b in range(x1.shape[0] // TL):
        rw = rw_ref[b * TL:(b + 1) * TL, :]
        w1, w2, pos1, pos2 = rw[:, 0:1], rw[:, 1:2], rw[:, 2:3], rw[:, 3:4]
        ys = ys_ref[b * CAP:b * CAP + USED_ROWS, :]
        pick = lambda pos: jnp.dot((slot == pos).astype(BF16), ys, preferred_element_type=F32)
        parts.append(pick(pos1) * w1 + pick(pos2) * w2)
    moe = parts[0] if len(parts) == 1 else jnp.concatenate(parts, axis=0)
    x2 = _ln(ALPHA * x1 + moe, ln2_g[...], ln2_b[...])
    gate = _sigmoid(_bdot(x2, w_pg[...]) + b_pg[...])
    p = jnp.where(i < n_prompt_tiles, pp_ref[...], ps_ref[...])
    y = x2 + gate * _bdot(p, w_ple[...])

    @pl.when(i < n_prompt_tiles)
    def _prompt():
        yp_ref[...] = y

    @pl.when(i >= n_prompt_tiles)
    def _sample():
        ys_out_ref[...] = y


def _rope_tables(pos):
    half = RET_DK // 2
    inv_freq = ROPE_BASE ** (-np.arange(half, dtype=np.float64) / half)
    ang = np.asarray(pos, np.float64)[:, None] * inv_freq[None, :]
    cos = np.cos(ang)
    sin = np.sin(ang)
    return (np.concatenate([cos, cos], axis=-1).astype(np.float32),
            np.concatenate([-sin, sin], axis=-1).astype(np.float32))


def _log_gamma():
    return np.log(1.0 - 2.0 ** (-5.0 - np.arange(RET_HEADS, dtype=np.float64)))


def _const_spec(shape):
    nd = len(shape)
    return pl.BlockSpec(shape, lambda *_: (0,) * nd, pipeline_mode=pl.Buffered(1))


def _chunk_plan(meta, n_blocks, n_ffn_tiles):
    assert n_blocks * BLOCK_SPARE >= N_EXPERTS * (TILE_CHUNKS - 1)
    m = meta.reshape(n_blocks, LANES, LANES)
    cnt = m[:, :N_EXPERTS, 0]
    off = m[:, :N_EXPERTS, 1]
    nch = (cnt + (CHUNK - 1)) // CHUNK
    cum = jnp.cumsum(nch, axis=0)
    total = cum[-1:]
    tiles_e = (total + TILE_CHUNKS - 1) // TILE_CHUNKS
    tile_end = jnp.cumsum(tiles_e, axis=1)
    tile_start = tile_end - tiles_e
    tid = jnp.arange(n_ffn_tiles, dtype=I32)[:, None]
    owner = (tid >= tile_start) & (tid < tile_end)
    pick_e = lambda v: jnp.sum(jnp.where(owner, v, 0), axis=1, keepdims=True)
    te = pick_e(jnp.arange(N_EXPERTS, dtype=I32)[None, :])
    k = (tid - pick_e(tile_start)) * TILE_CHUNKS + jnp.arange(TILE_CHUNKS, dtype=I32)[None, :]
    total_t = pick_e(total)
    real = k < total_t
    by_tile = lambda v: jnp.sum(jnp.where(owner[:, None, :], v[None, :, :], 0), axis=2)
    cum_t = by_tile(cum)
    blk = jnp.minimum(jnp.sum((cum_t[:, None, :] <= k[:, :, None]).astype(I32), axis=2), n_blocks - 1)
    at_blk = blk[:, :, None] == jnp.arange(n_blocks, dtype=I32)[None, None, :]
    pick_b = lambda v: jnp.sum(jnp.where(at_blk, v[:, None, :], 0), axis=2)
    excl = pick_b(cum_t - by_tile(nch))
    off_t = pick_b(by_tile(off))
    spare = te * (TILE_CHUNKS - 1) + jnp.maximum(k - total_t, 0) % TILE_CHUNKS
    spare_chunk = (spare // BLOCK_SPARE) * BLOCK_CHUNKS + BLOCK_USED + spare % BLOCK_SPARE
    chunk = jnp.where(real, blk * BLOCK_CHUNKS + off_t + (k - excl), spare_chunk)
    n_valid = jnp.sum(tiles_e, axis=1)
    te = jnp.where(tid < n_valid, te, N_EXPERTS - 1)
    return te.reshape(-1).astype(I32), n_valid.astype(I32), chunk.reshape(-1).astype(I32)


def kernel(x_prompt, x_sample, state_ret, state_conv, p_prompt, p_sample, w_in, b_in, ret_gn_g, ret_gn_b,
           w_ret_o, conv_w, conv_b, conv_ln_g, conv_ln_b, w_conv_o, w_out, ln1_g, ln1_b, w_grp, b_grp,
           w_exp, b_exp, w_gu, w_dn, ln2_g, ln2_b, w_pg, b_pg, w_ple):
    assert DEPTH == 1 and w_in.shape[0] == 1
    bp, lp, d = x_prompt.shape
    bs, ls, _ = x_sample.shape
    n_p, n_s = bp * lp, bs * ls
    n_tok = n_p + n_s
    assert lp % TL == 0 and n_s % TL == 0 and bs % BB_SAMPLE == 0 and SUBLANES % ls == 0
    n_blocks = n_tok // TL

    f32c = lambda a, shape: jnp.asarray(np.broadcast_to(a, shape).astype(np.float32))
    lg = _log_gamma()
    c = RET_CHUNK
    idx = np.arange(c, dtype=np.float64)
    rel = idx[:, None] - idx[None, :]
    causal = rel >= 0
    decay = np.where(causal[None], np.exp(np.where(causal, rel, 0.0)[None] * lg[:, None, None]), 0.0)
    decay = f32c(decay, decay.shape)
    q_decay = np.exp((idx[:, None] + 1.0) * lg[None, :])
    k_decay = np.exp((c - 1.0 - idx[:, None]) * lg[None, :])
    chunk_decay = np.exp(c * lg)
    qdec_p = f32c(q_decay.T[:, :, None], (RET_HEADS, c, RET_DK))
    kdec_p = f32c(k_decay.T[:, :, None], (RET_HEADS, c, RET_DK))
    cdec_p = f32c(chunk_decay[:, None, None], (RET_HEADS, 1, RET_DV))
    cos_p, sin_p = (jnp.asarray(a) for a in _rope_tables(np.arange(lp)))

    ts = BB_SAMPLE * ls
    idx_s = np.arange(ls, dtype=np.float64)
    pdec_s = np.exp(idx_s[None, :] * lg[:, None])
    pdec_s = f32c(pdec_s[:, :, None, None], (RET_HEADS, ls, 1, RET_DK))
    qd_s = np.exp((idx_s[:, None] + 1.0) * lg[None, :])
    kd_s = np.exp((ls - 1.0 - idx_s[:, None]) * lg[None, :])
    qdec_s = f32c(np.tile(qd_s.T, (1, BB_SAMPLE))[:, :, None], (RET_HEADS, ts, RET_DK))
    kdec_s = f32c(np.tile(kd_s.T, (1, BB_SAMPLE))[:, :, None], (RET_HEADS, ts, RET_DK))
    cdec_s = f32c(np.exp(ls * lg)[:, None, None], (RET_HEADS, 1, RET_DV))
    cos_s, sin_s = (jnp.asarray(a) for a in _rope_tables(np.tile(PAST_LEN + np.arange(ls), BB_SAMPLE)))

    w_in_b = w_in[0].astype(BF16)
    w_ret_o_b = w_ret_o[0].astype(BF16)
    w_conv_o_b = w_conv_o[0].astype(BF16)
    w_out_b = w_out[0].astype(BF16)
    w_pg_b = w_pg[0].astype(BF16)
    w_ple_b = w_ple[0].astype(BF16)
    w_r = jnp.zeros((d, LANES), F32).at[:, :N_GROUPS].set(w_grp[0]).at[:, N_GROUPS:N_GROUPS + N_EXPERTS].set(w_exp[0])
    wr_hi = w_r.astype(BF16)
    wr_lo = (w_r - wr_hi.astype(F32)).astype(BF16)
    b_r = jnp.zeros((1, LANES), F32).at[0, :N_GROUPS].set(b_grp[0]).at[0, N_GROUPS:N_GROUPS + N_EXPERTS].set(b_exp[0])
    row = lambda a: a.reshape(1, -1)
    conv_w0 = conv_w[0]
    nstate = CONV_WIDTH - 1
    win_row = np.array([m if m < nstate else XPAD_NEW + (m - nstate) for m in range(nstate + ls)])
    wsh = jnp.stack([jnp.zeros((XPAD_ROWS, CONV_CH), F32).at[win_row[i:i + CONV_WIDTH]].set(conv_w0)
                     for i in range(ls)])

    rep8 = lambda a: jnp.broadcast_to(a[..., None, :], a.shape[:-1] + (SUBLANES, a.shape[-1]))
    shared_w = (w_in_b, rep8(b_in[0]), row(ret_gn_g[0]), row(ret_gn_b[0]), w_ret_o_b)
    tail_w = (row(conv_ln_g[0]), row(conv_ln_b[0]), w_conv_o_b, w_out_b, row(ln1_g[0]), row(ln1_b[0]),
              wr_hi, wr_lo, b_r)

    nbt = bs // BB_SAMPLE
    xs2 = x_sample.reshape(n_s, d)
    sample_in = ((xs2, cos_s, sin_s, pdec_s, qdec_s, kdec_s, cdec_s, wsh, state_ret, state_conv)
                 + shared_w + (row(conv_b[0]),) + tail_w)
    sample_specs = (
        [pl.BlockSpec((ts, d), lambda i: (i, 0))]
        + [_const_spec(a.shape) for a in sample_in[1:8]]
        + [pl.BlockSpec((1, BB_SAMPLE, RET_HEADS, RET_DK, RET_DV), lambda i: (0, i, 0, 0, 0)),
           pl.BlockSpec((1, BB_SAMPLE, nstate, CONV_CH), lambda i: (0, i, 0, 0))]
        + [_const_spec(a.shape) for a in sample_in[10:]]
    )
    tok_spec_s = lambda w: pl.BlockSpec((ts, w), lambda i: (i, 0))
    x1_s, rw_s, ret_s, conv_s = pl.pallas_call(
        _sample_mixer_kernel,
        grid=(nbt,),
        in_specs=sample_specs,
        out_specs=[
            tok_spec_s(d), tok_spec_s(LANES),
            pl.BlockSpec((1, BB_SAMPLE, RET_HEADS, RET_DK, RET_DV), lambda i: (0, i, 0, 0, 0)),
            pl.BlockSpec((1, BB_SAMPLE, nstate, CONV_CH), lambda i: (0, i, 0, 0)),
        ],
        out_shape=[
            jax.ShapeDtypeStruct((n_s, d), F32),
            jax.ShapeDtypeStruct((n_s, LANES), F32),
            jax.ShapeDtypeStruct(state_ret.shape, F32),
            jax.ShapeDtypeStruct(state_conv.shape, F32),
        ],
        scratch_shapes=[
            pltpu.VMEM((ts, RET_V), F32),
            pltpu.VMEM((CONV_CH // LANES, ts, LANES), F32),
            pltpu.VMEM((BB_SAMPLE, XPAD_ROWS, CONV_CH), F32),
        ],
        compiler_params=pltpu.CompilerParams(
            dimension_semantics=("arbitrary",), vmem_limit_bytes=VMEM_LIMIT),
        name="sample_mixer",
    )(*sample_in)

    assert lp % TLM == 0 and n_s % TLM == 0 and TLM % TL == 0
    nlt = lp // TLM
    npt = n_p // TLM
    nst = n_s // TLM
    sub = TLM // TL
    prompt_in = ((x_prompt, x1_s, rw_s, cos_p, sin_p, decay, qdec_p, kdec_p, cdec_p)
                 + shared_w + (rep8(conv_w0), row(conv_b[0])) + tail_w)
    head_tile = lambda s: jnp.minimum(s, npt - 1)
    sample_tile = lambda s: jnp.maximum(s - npt, 0)
    sample_spec = lambda w: pl.BlockSpec((TLM, w), lambda s: (sample_tile(s), 0))
    prompt_specs = [
        pl.BlockSpec((1, TLM, d), lambda s: (head_tile(s) // nlt, head_tile(s) % nlt, 0)),
        sample_spec(d), sample_spec(LANES),
        pl.BlockSpec((TLM, RET_DK), lambda s: (head_tile(s) % nlt, 0)),
        pl.BlockSpec((TLM, RET_DK), lambda s: (head_tile(s) % nlt, 0)),
    ] + [_const_spec(a.shape) for a in prompt_in[5:]]
    tok_spec_p = lambda rows, w: pl.BlockSpec((rows, w), lambda s: (s, 0))
    x1_all, rw_all, xs_all, meta, ret_p, conv_p = pl.pallas_call(
        functools.partial(_prompt_mixer_kernel, n_tiles=npt, tiles_per_seq=nlt),
        grid=(npt + nst,),
        in_specs=prompt_specs,
        out_specs=[
            tok_spec_p(TLM, d), tok_spec_p(TLM, LANES), tok_spec_p(sub * CAP, d), tok_spec_p(sub * LANES, LANES),
            pl.BlockSpec((1, 1, RET_HEADS, RET_DK, RET_DV), lambda s: (0, head_tile(s) // nlt, 0, 0, 0)),
            pl.BlockSpec((1, 1, nstate, CONV_CH), lambda s: (0, head_tile(s) // nlt, 0, 0)),
        ],
        out_shape=[
            jax.ShapeDtypeStruct((n_tok, d), F32),
            jax.ShapeDtypeStruct((n_tok, LANES), F32),
            jax.ShapeDtypeStruct((n_blocks * CAP, d), BF16),
            jax.ShapeDtypeStruct((n_blocks * LANES, LANES), I32),
            jax.ShapeDtypeStruct((1, bp, RET_HEADS, RET_DK, RET_DV), F32),
            jax.ShapeDtypeStruct((1, bp, nstate, CONV_CH), F32),
        ],
        scratch_shapes=[
            pltpu.VMEM((TLM + CONV_PAD, CONV_CH), F32),
            pltpu.VMEM((SUBLANES - 1, TL + CONV_PAD - SUBLANES, CONV_CH), F32),
            pltpu.VMEM((TLM, 2 * RET_QK + RET_V), F32),
            pltpu.VMEM((TLM, d), BF16),
            pltpu.VMEM((TLM, RET_V), F32),
            pltpu.VMEM((TLM, CONV_CH), F32),
            pltpu.VMEM((TLM, RET_V + 2 * D_MODEL), F32),
        ],
        compiler_params=pltpu.CompilerParams(
            dimension_semantics=("arbitrary",), vmem_limit_bytes=VMEM_LIMIT),
        name="prompt_mixer",
    )(*prompt_in)

    max_chunks = n_blocks * (TOP_K * TL // CHUNK + N_EXPERTS - 1)
    n_ffn_tiles = (max_chunks + N_EXPERTS * (TILE_CHUNKS - 1)) // TILE_CHUNKS
    tile_e, n_valid_tiles, chunk_ids = _chunk_plan(meta, n_blocks, n_ffn_tiles)

    ys_all = pl.pallas_call(
        _ffn_kernel,
        grid_spec=pltpu.PrefetchScalarGridSpec(
            num_scalar_prefetch=3,
            grid=(n_ffn_tiles,),
            in_specs=[
                pl.BlockSpec(memory_space=pl.ANY),
                pl.BlockSpec((1, d, 2 * EXP_FF), lambda i, te, nr, ch: (te[i], 0, 0)),
                pl.BlockSpec((1, EXP_FF, d), lambda i, te, nr, ch: (te[i], 0, 0)),
            ],
            out_specs=pl.BlockSpec(memory_space=pl.ANY),
            scratch_shapes=[
                pltpu.VMEM((2, TM_FFN, d), BF16),
                pltpu.VMEM((2, TM_FFN, d), BF16),
                pltpu.VMEM((d, 2 * EXP_FF), BF16),
                pltpu.VMEM((EXP_FF, d), BF16),
                pltpu.SemaphoreType.DMA((2,)),
                pltpu.SemaphoreType.DMA((2,)),
            ],
        ),
        out_shape=jax.ShapeDtypeStruct(xs_all.shape, BF16),
        input_output_aliases={3: 0},
        compiler_params=pltpu.CompilerParams(
            dimension_semantics=("arbitrary",), vmem_limit_bytes=VMEM_LIMIT),
        name="expert_ffn",
    )(tile_e, n_valid_tiles, chunk_ids, xs_all, w_gu[0], w_dn[0])

    assert n_p % TLF == 0 and n_s % TLF == 0
    npt = n_p // TLF
    fsub = TLF // TL
    pp2 = p_prompt.reshape(n_p, PLE_DIM)
    ps2 = p_sample.reshape(n_s, PLE_DIM)
    tok_f = lambda rows, w: pl.BlockSpec((rows, w), lambda i: (i, 0))
    y_p, y_s = pl.pallas_call(
        functools.partial(_final_kernel, n_prompt_tiles=npt),
        grid=(n_tok // TLF,),
        in_specs=[
            tok_f(fsub * CAP, d), tok_f(TLF, d), tok_f(TLF, LANES),
            pl.BlockSpec((TLF, PLE_DIM), lambda i: (jnp.minimum(i, npt - 1), 0)),
            pl.BlockSpec((TLF, PLE_DIM), lambda i: (jnp.maximum(i - npt, 0), 0)),
            _const_spec((1, d)), _const_spec((1, d)), _const_spec((d, d)), _const_spec((1, d)),
            _const_spec((PLE_DIM, d)),
        ],
        out_specs=[
            pl.BlockSpec((TLF, d), lambda i: (jnp.minimum(i, npt - 1), 0)),
            pl.BlockSpec((TLF, d), lambda i: (jnp.maximum(i - npt, 0), 0)),
        ],
        out_shape=[jax.ShapeDtypeStruct((n_p, d), F32), jax.ShapeDtypeStruct((n_s, d), F32)],
        compiler_params=pltpu.CompilerParams(
            dimension_semantics=("arbitrary",), vmem_limit_bytes=VMEM_LIMIT),
        name="moe_combine_final",
    )(ys_all, x1_all, rw_all, pp2, ps2,
      row(ln2_g[0]), row(ln2_b[0]), w_pg_b, row(b_pg[0]), w_ple_b)

    return (y_p.reshape(bp, lp, d), y_s.reshape(bs, ls, d), ret_p, conv_p, ret_s, conv_s)
```

```python
import functools

import jax
import jax.numpy as jnp
import numpy as np
from jax import lax
from jax.experimental import pallas as pl
from jax.experimental.pallas import tpu as pltpu

F32 = jnp.float32
BF16 = jnp.bfloat16
I32 = jnp.int32
U32 = jnp.uint32

D_MODEL = 1024
PAST_LEN = 16384
RET_HEADS = 4
RET_DK = 128
RET_DV = 128
RET_QK = RET_HEADS * RET_DK
RET_V = RET_HEADS * RET_DV
RET_CHUNK = 128
ROPE_BASE = 10000.0
CONV_CH = 512
CONV_WIDTH = 31
N_GROUPS = 4
EXP_PER_GROUP = 4
N_EXPERTS = N_GROUPS * EXP_PER_GROUP
TOP_K = 2
EXP_FF = 512
PLE_DIM = 256
DEPTH = 1
ALPHA = (2 * DEPTH) ** 0.25
LN_EPS = 1e-5
IN_WIDTHS = (RET_QK, RET_QK, RET_V, RET_V, CONV_CH, CONV_CH, D_MODEL, D_MODEL)
IN_OFFS = tuple(int(s) for s in np.cumsum((0,) + IN_WIDTHS))

LANES = 128
SUBLANES = 8
VMEM_LIMIT = 56 * 1024 * 1024

TL = 256
TLM = 512
TLF = 512
BB_SAMPLE = 16
CHUNK = 2 * SUBLANES
TILE_CHUNKS = 32
BLOCK_USED = -(-(TOP_K * TL + N_EXPERTS * (CHUNK - 1)) // LANES) * LANES // CHUNK
BLOCK_SPARE = LANES // CHUNK
BLOCK_CHUNKS = BLOCK_USED + BLOCK_SPARE
USED_ROWS = BLOCK_USED * CHUNK
CAP = BLOCK_CHUNKS * CHUNK
TM_FFN = TILE_CHUNKS * CHUNK
FFN_COLS = 256
CONV_PAD = 32
XPAD_NEW = 32
XPAD_ROWS = 40


def _ln(x, g, b):
    mu = jnp.mean(x, axis=-1, keepdims=True)
    d = x - mu
    var = jnp.mean(d * d, axis=-1, keepdims=True)
    return d * lax.rsqrt(var + LN_EPS) * g + b


def _sigmoid(x):
    return 1.0 / (1.0 + jnp.exp(-x))


def _rep(v8, rows):
    return v8 if rows == SUBLANES else jnp.concatenate([v8] * (rows // SUBLANES), axis=0)


def _silu(x):
    return x * _sigmoid(x)


def _bdot(a, b):
    return jnp.dot(a.astype(BF16), b, preferred_element_type=F32)


def _rot(t, cosf, sinf):
    return t * cosf + pltpu.roll(t, RET_DK // 2, axis=1) * sinf


def _lane_tile(cols, rows):
    lane = lax.broadcasted_iota(I32, (rows, LANES), 1)
    out = jnp.zeros((rows, LANES), F32)
    for i, col in enumerate(cols):
        out = jnp.where(lane == i, col, out)
    return out


def _route(logits):
    lane = lax.broadcasted_iota(I32, logits.shape, 1)
    lanef = lane.astype(F32)
    ninf = jnp.float32(-jnp.inf)
    big = jnp.float32(LANES)
    gmask = lane < N_GROUPS
    gl = jnp.where(gmask, logits, ninf)
    gmax = jnp.max(gl, axis=1, keepdims=True)
    gidx = jnp.min(jnp.where(gmask & (gl == gmax), lanef, big), axis=1, keepdims=True)
    sumexp = jnp.sum(jnp.where(gmask, jnp.exp(gl - gmax), 0.0), axis=1, keepdims=True)
    gw = 1.0 / sumexp
    lo = N_GROUPS + EXP_PER_GROUP * gidx
    emask = (lanef >= lo) & (lanef < lo + EXP_PER_GROUP)
    el = jnp.where(emask, logits, ninf)
    m1 = jnp.max(el, axis=1, keepdims=True)
    i1 = jnp.min(jnp.where(emask & (el == m1), lanef, big), axis=1, keepdims=True)
    emask2 = emask & (lanef != i1)
    el2 = jnp.where(emask2, logits, ninf)
    m2 = jnp.max(el2, axis=1, keepdims=True)
    i2 = jnp.min(jnp.where(emask2 & (el2 == m2), lanef, big), axis=1, keepdims=True)
    t = jnp.exp(m2 - m1)
    den = 1.0 + t
    return (1.0 / den) * gw, (t / den) * gw, i1 - N_GROUPS, i2 - N_GROUPS


def _post_mix_pieces(src, w, sink):
    (w_ret_o, cln_g, cln_b, w_conv_o, w_out, ln1_g, ln1_b, wr_hi, wr_lo, b_r) = w
    st = {}

    def branch_a():
        st["a"] = _bdot(_silu(src["g"]()) * src["ret"](), w_ret_o[...])

    def branch_b():
        st["b"] = _bdot(_silu(_ln(src["cout"](), cln_g[...], cln_b[...])), w_conv_o[...])

    def merge():
        mix = _sigmoid(src["gt_a"]()) * st["a"] + _sigmoid(src["gt_b"]()) * st["b"]
        h = ALPHA * src["x"]() + _bdot(mix, w_out[...])
        st["x1"] = _ln(h, ln1_g[...], ln1_b[...])

    def router():
        x1 = st["x1"]
        x1_hi = x1.astype(BF16)
        x1_lo = (x1 - x1_hi.astype(F32)).astype(BF16)
        st["logits"] = (jnp.dot(x1_hi, wr_hi[...], preferred_element_type=F32)
                        + (jnp.dot(x1_lo, wr_hi[...], preferred_element_type=F32)
                           + jnp.dot(x1_hi, wr_lo[...], preferred_element_type=F32))
                        + b_r[...])

    def route():
        st["route"] = _route(st["logits"])

    def finish():
        sink(st["x1"], *st["route"])

    return [branch_a, branch_b, merge, router, route, finish]


def _interleave(a, b):
    j = 0
    for i, piece in enumerate(a):
        piece()
        while j < len(b) and (j + 1) * len(a) <= (i + 1) * len(b):
            b[j]()
            j += 1
    for piece in b[j:]:
        piece()


def _sort_tile(x1, w1, w2, e1, e2, x1_ref, rw_ref, xs_ref, meta_ref):
    t = x1.shape[0]
    ids_t = _lane_tile((e1, e2), t).T
    e1r, e2r = ids_t[0:1, :], ids_t[1:2, :]
    sub = lax.broadcasted_iota(I32, (LANES, t), 0).astype(F32)
    a1 = (sub == e1r).astype(F32)
    a2 = (sub == e2r).astype(F32)
    ri = lax.broadcasted_iota(I32, (t, t), 0)
    ci = lax.broadcasted_iota(I32, (t, t), 1)
    earlier = (ri < ci).astype(BF16)
    r1 = jnp.dot(a1.astype(BF16), earlier, preferred_element_type=F32)
    r2 = jnp.dot(a2.astype(BF16), earlier, preferred_element_type=F32)
    cnt1 = jnp.sum(a1, axis=1, keepdims=True)
    cnt = cnt1 + jnp.sum(a2, axis=1, keepdims=True)
    nch = jnp.floor((cnt + (CHUNK - 1.0)) * (1.0 / CHUNK))
    ui = lax.broadcasted_iota(I32, (LANES, LANES), 0)
    uj = lax.broadcasted_iota(I32, (LANES, LANES), 1)
    before = (uj < ui).astype(BF16)
    off = jnp.dot(before, jnp.broadcast_to(nch, (LANES, LANES)).astype(BF16),
                  preferred_element_type=F32)[:, 0:1]
    base = off * CHUNK
    pos1r = jnp.sum(a1 * (base + r1), axis=0, keepdims=True)
    pos2r = jnp.sum(a2 * (base + cnt1 + r2), axis=0, keepdims=True)
    slot = lax.broadcasted_iota(I32, (CAP, t), 0).astype(F32)
    onehot = ((slot == pos1r) | (slot == pos2r)).astype(BF16)
    xs = jnp.dot(onehot, x1.astype(BF16), preferred_element_type=F32)
    pos_cols = jnp.where(sub == 2.0, pos1r, jnp.where(sub == 3.0, pos2r, 0.0)).T
    lane = lax.broadcasted_iota(I32, (t, LANES), 1)
    x1_ref[...] = x1
    rw_ref[...] = jnp.where(lane == 0, w1, jnp.where(lane == 1, w2, pos_cols))
    xs_ref[...] = xs.astype(BF16)
    mlane = lax.broadcasted_iota(I32, (LANES, LANES), 1)
    meta = jnp.where(mlane == 0, cnt, jnp.where(mlane == 1, off, 0.0))
    meta_ref[...] = meta.astype(I32)


def _sort_tiles(x1, w1, w2, e1, e2, x1_ref, rw_ref, xs_ref, meta_ref):
    for i in range(x1.shape[0] // TL):
        rows = slice(i * TL, (i + 1) * TL)
        _sort_tile(x1[rows], w1[rows], w2[rows], e1[rows], e2[rows],
                   x1_ref.at[pl.ds(i * TL, TL)], rw_ref.at[pl.ds(i * TL, TL)],
                   xs_ref.at[pl.ds(i * CAP, CAP)], meta_ref.at[pl.ds(i * LANES, LANES)])


GATE_COLS = {3: 0, 6: RET_V, 7: RET_V + D_MODEL}
QKV_COLS = {0: 0, 1: RET_QK, 2: 2 * RET_QK}


def _prompt_mixer_kernel(x_ref, x1s_ref, rws_ref, cos_ref, sin_ref, dec_ref, qdec_ref, kdec_ref, cdec_ref,
                         w_in, b_in, gn_g, gn_b, w_ret_o, conv_w, conv_b, cln_g, cln_b,
                         w_conv_o, w_out, ln1_g, ln1_b, wr_hi, wr_lo, b_r,
                         x1_ref, rw_ref, xs_ref, meta_ref, sret_ref, sconv_ref,
                         ubuf, ushift, qkv_scr, xb_scr, ret_scr, cout_scr, gate_scr,
                         *, n_tiles, tiles_per_seq):
    s = pl.program_id(0)
    li = lax.rem(s, tiles_per_seq)
    outs = (x1_ref, rw_ref, xs_ref, meta_ref)
    slot = dict(ret=ret_scr, cout=cout_scr, gates=gate_scr)
    tail_w = (w_ret_o, cln_g, cln_b, w_conv_o, w_out, ln1_g, ln1_b, wr_hi, wr_lo, b_r)

    @pl.when((s < n_tiles) & (li == 0))
    def _new_sequence():
        sret_ref[...] = jnp.zeros(sret_ref.shape, F32)
        ubuf[0:CONV_PAD, :] = jnp.zeros((CONV_PAD, CONV_CH), F32)

    @pl.when(s < n_tiles)
    def _mix():
        gcols = lambda kk: slice(GATE_COLS[kk], GATE_COLS[kk] + IN_WIDTHS[kk])
        src = dict(x=lambda: x_ref[0], ret=lambda: ret_scr[...], cout=lambda: cout_scr[...],
                   g=lambda: gate_scr[:, gcols(3)], gt_a=lambda: gate_scr[:, gcols(6)],
                   gt_b=lambda: gate_scr[:, gcols(7)])
        head = _prompt_head_pieces(x_ref, cos_ref, sin_ref, dec_ref, qdec_ref, kdec_ref, cdec_ref,
                                   w_in, b_in, gn_g, gn_b, conv_w, conv_b, sret_ref,
                                   ubuf, ushift, qkv_scr, xb_scr, slot)
        tail = _post_mix_pieces(src, tail_w, lambda *r: _sort_tiles(*r, *outs))
        for piece in head + tail:
            piece()

    @pl.when(s >= n_tiles)
    def _append():
        rws = rws_ref[...]
        _sort_tiles(x1s_ref[...], rws[:, 0:1], rws[:, 1:2], rws[:, 2:3], rws[:, 3:4], *outs)

    @pl.when((s < n_tiles) & (li == tiles_per_seq - 1))
    def _conv_state():
        sconv_ref[0, 0] = ubuf[CONV_PAD - (CONV_WIDTH - 1):CONV_PAD, :]


def _prompt_head_pieces(x_ref, cos_ref, sin_ref, dec_ref, qdec_ref, kdec_ref, cdec_ref,
                        w_in, b_in, gn_g, gn_b, conv_w, conv_b, sret_ref,
                        ubuf, ushift, qkv_scr, xb_scr, slot):
    tl = x_ref.shape[1]
    st = {}

    def slab_dot(c0, c1):
        return jnp.dot(xb_scr[...], w_in[:, c0:c1], preferred_element_type=F32) + _rep(b_in[:, c0:c1], tl)

    def glu():
        xb_scr[...] = x_ref[0].astype(BF16)
        u = slab_dot(IN_OFFS[4], IN_OFFS[5]) * _sigmoid(slab_dot(IN_OFFS[5], IN_OFFS[6]))
        ubuf[CONV_PAD:CONV_PAD + tl, :] = u

    nsh = ushift.shape[1]
    span = nsh - (CONV_PAD - SUBLANES)

    def shift_span(h):
        for s in range(1, SUBLANES):
            ushift[s - 1] = ubuf[h * span + s:h * span + s + nsh, :]

    slab = 256
    slabs = [(kk, c0) for kk in (0, 1, 2, 3, 6, 7) for c0 in range(IN_OFFS[kk], IN_OFFS[kk + 1], slab)]
    rb = 32
    nrb = tl // rb

    def conv_block(r):
        h, rl = divmod(r * rb, span)
        acc = jnp.zeros((rb, CONV_CH), F32) + conv_b[...]
        for j in range(CONV_WIDTH):
            off = j + (CONV_PAD - (CONV_WIDTH - 1))
            s = off % SUBLANES
            base = rl + off - s
            win = (ubuf[h * span + base:h * span + base + rb, :] if s == 0
                   else ushift[s - 1, base:base + rb, :])
            acc = acc + _rep(conv_w[j], rb) * win
        slot["cout"][r * rb:(r + 1) * rb, :] = acc
        for kk, c0 in slabs[r * len(slabs) // nrb:(r + 1) * len(slabs) // nrb]:
            val = slab_dot(c0, c0 + slab)
            if kk in QKV_COLS:
                dst = QKV_COLS[kk] + c0 - IN_OFFS[kk]
                qkv_scr[:, dst:dst + slab] = val
            else:
                dst = GATE_COLS[kk] + c0 - IN_OFFS[kk]
                slot["gates"][:, dst:dst + slab] = val
        if r == nrb - 1:
            ubuf[0:CONV_PAD, :] = ubuf[tl:tl + CONV_PAD, :]

    scale = RET_DK ** -0.5

    def retention(c, h):
        rows = slice(c * RET_CHUNK, (c + 1) * RET_CHUNK)
        cols = slice(h * RET_DK, (h + 1) * RET_DK)
        hcol = lambda kk: slice(QKV_COLS[kk] + h * RET_DK, QKV_COLS[kk] + (h + 1) * RET_DK)
        cosf = cos_ref[rows, :]
        sinf = sin_ref[rows, :]
        qh = _rot(qkv_scr[rows, hcol(0)], cosf, sinf)
        kh = _rot(qkv_scr[rows, hcol(1)], cosf, sinf) * scale
        qb = qh.astype(BF16)
        kb = kh.astype(BF16)
        vb = qkv_scr[rows, hcol(2)].astype(BF16)
        s_old = sret_ref[0, 0, h]
        scores = lax.dot_general(qb, kb, (((1,), (1,)), ((), ())),
                                 preferred_element_type=F32) * dec_ref[h]
        inner = jnp.dot(scores.astype(BF16), vb, preferred_element_type=F32)
        cross = jnp.dot(qb, s_old.astype(BF16), preferred_element_type=F32) * qdec_ref[h]
        kd = (kh * kdec_ref[h]).astype(BF16)
        s_new = cdec_ref[h] * s_old + lax.dot_general(
            kd, vb, (((0,), (0,)), ((), ())), preferred_element_type=F32)
        sret_ref[0, 0, h] = s_new
        slot["ret"][rows, cols] = _ln(inner + cross, gn_g[:, cols], gn_b[:, cols])

    pieces = [glu]
    for r in range(nrb):
        if (r * rb) % span == 0:
            pieces.append(lambda h=(r * rb) // span: shift_span(h))
        pieces.append(lambda r=r: conv_block(r))
    pieces += [lambda c=c, h=h: retention(c, h) for c in range(tl // RET_CHUNK) for h in range(RET_HEADS)]
    return pieces


def _sample_mixer_kernel(x_ref, cos_ref, sin_ref, pdec_ref, qdec_ref, kdec_ref, cdec_ref, wsh_ref,
                         sret_in, sconv_in,
                         w_in, b_in, gn_g, gn_b, w_ret_o, conv_b, cln_g, cln_b,
                         w_conv_o, w_out, ln1_g, ln1_b, wr_hi, wr_lo, b_r,
                         x1_ref, rw_ref, sret_ref, sconv_ref,
                         ret_scr, cout_scr, xpad):
    t = x_ref.shape[0]
    ls = t // BB_SAMPLE
    x = x_ref[...]
    xb = x.astype(BF16)

    def proj(k):
        c0, c1 = IN_OFFS[k], IN_OFFS[k + 1]
        return jnp.dot(xb, w_in[:, c0:c1], preferred_element_type=F32) + _rep(b_in[:, c0:c1], t)

    q = proj(0)
    k = proj(1)
    v = proj(2)
    scale = RET_DK ** -0.5
    cosf = cos_ref[...]
    sinf = sin_ref[...]
    row = lax.broadcasted_iota(I32, (t, RET_DK), 0)
    pos = row % ls
    row8 = lax.broadcasted_iota(I32, (SUBLANES, RET_DK), 0)
    per_tile = SUBLANES // ls
    for h in range(RET_HEADS):
        cols = slice(h * RET_DK, (h + 1) * RET_DK)
        qh = _rot(q[:, cols], cosf, sinf)
        kh = _rot(k[:, cols], cosf, sinf) * scale
        vh = v[:, cols]
        inner = jnp.zeros((t, RET_DV), F32)
        for s in range(ls):
            ks = kh if s == 0 else pltpu.roll(kh, s, axis=0)
            vs = vh if s == 0 else pltpu.roll(vh, s, axis=0)
            dotp = jnp.sum(qh * ks, axis=1, keepdims=True) * pdec_ref[h, s]
            inner = inner + jnp.where(pos >= s, dotp, 0.0) * vs
        kd = kh * kdec_ref[h]
        for tile in range(t // SUBLANES):
            rows = slice(tile * SUBLANES, (tile + 1) * SUBLANES)
            q8 = qh[rows, :]
            kd8 = kd[rows, :]
            v8 = vh[rows, :]
            seqs = [tile * per_tile + sub for sub in range(per_tile)]
            mine = [(row8 >= sub * ls) & (row8 < (sub + 1) * ls) for sub in range(per_tile)]
            s_old = [sret_in[0, b, h] for b in seqs]
            c_all = jnp.dot(q8, jnp.concatenate(s_old, axis=1), preferred_element_type=F32)
            upd = lax.dot_general(jnp.concatenate([jnp.where(m, kd8, 0.0) for m in mine], axis=1), v8,
                                  (((0,), (0,)), ((), ())), preferred_element_type=F32)
            cross8 = jnp.zeros((SUBLANES, RET_DV), F32)
            for sub, b in enumerate(seqs):
                cross8 = jnp.where(mine[sub], c_all[:, sub * RET_DV:(sub + 1) * RET_DV], cross8)
                sret_ref[0, b, h] = cdec_ref[h] * s_old[sub] + upd[sub * RET_DK:(sub + 1) * RET_DK, :]
            ret_scr[rows, cols] = inner[rows, :] + cross8 * qdec_ref[h, rows, :]
        ret_scr[:, cols] = _ln(ret_scr[:, cols], gn_g[:, cols], gn_b[:, cols])

    u = proj(4) * _sigmoid(proj(5))
    nstate = CONV_WIDTH - 1
    xpad[...] = jnp.zeros(xpad.shape, F32)
    xpad[:, 0:nstate, :] = sconv_in[0]
    for b in range(BB_SAMPLE):
        xpad[b, XPAD_NEW:XPAD_NEW + ls, :] = u[b * ls:(b + 1) * ls, :]
    for p in range(ls):
        res = jnp.sum(xpad[...] * wsh_ref[p][None], axis=1) + conv_b[...]
        for sl in range(CONV_CH // LANES):
            cout_scr[sl, pl.ds(p, BB_SAMPLE, stride=ls), :] = res[:, sl * LANES:(sl + 1) * LANES]
    sconv_ref[0, :, 0:nstate - ls, :] = xpad[:, ls:nstate, :]
    sconv_ref[0, :, nstate - ls:nstate, :] = xpad[:, XPAD_NEW:XPAD_NEW + ls, :]
    c_out = jnp.concatenate([cout_scr[sl] for sl in range(CONV_CH // LANES)], axis=1)

    def sink(x1, w1, w2, e1, e2):
        x1_ref[...] = x1
        rw_ref[...] = _lane_tile((w1, w2, e1, e2), t)

    src = dict(x=lambda: x, ret=lambda: ret_scr[...], cout=lambda: c_out,
               g=lambda: proj(3), gt_a=lambda: proj(6), gt_b=lambda: proj(7))
    for piece in _post_mix_pieces(
            src, (w_ret_o, cln_g, cln_b, w_conv_o, w_out, ln1_g, ln1_b, wr_hi, wr_lo, b_r), sink):
        piece()


def _ffn_kernel(te_ref, nvalid_ref, chunk_ref, xs_hbm, w_gu, w_dn, ys_hbm,
                xbuf, obuf, wgu_b, wdn_b, sem_in, sem_out):
    del xs_hbm
    i = pl.program_id(0)
    n = pl.num_programs(0)
    slot = i % 2
    nvalid = nvalid_ref[0]

    def chunk_rows(tile, c):
        return pl.ds(pl.multiple_of(chunk_ref[tile * TILE_CHUNKS + c] * CHUNK, CHUNK), CHUNK)

    def start_in(tile, s):
        for c in range(TILE_CHUNKS):
            pltpu.make_async_copy(ys_hbm.at[chunk_rows(tile, c)],
                                  xbuf.at[s, pl.ds(c * CHUNK, CHUNK)], sem_in.at[s]).start()

    def start_out(tile, s):
        for c in range(TILE_CHUNKS):
            pltpu.make_async_copy(obuf.at[s, pl.ds(c * CHUNK, CHUNK)],
                                  ys_hbm.at[chunk_rows(tile, c)], sem_out.at[s]).start()

    def wait_in(s):
        pltpu.make_async_copy(ys_hbm.at[pl.ds(0, TM_FFN)], xbuf.at[s], sem_in.at[s]).wait()

    def wait_out(s):
        pltpu.make_async_copy(obuf.at[s], ys_hbm.at[pl.ds(0, TM_FFN)], sem_out.at[s]).wait()

    @pl.when((i == 0) & (nvalid > 0))
    def _first():
        start_in(0, 0)

    @pl.when(i + 1 < nvalid)
    def _prefetch():
        start_in(i + 1, 1 - slot)

    @pl.when((i >= 2) & (i - 2 < nvalid))
    def _retire():
        wait_out(slot)

    @pl.when(i < nvalid)
    def _tile():
        wait_in(slot)
        prev = te_ref[jnp.maximum(i - 1, 0)]

        @pl.when((i == 0) | (te_ref[i] != prev))
        def _new_expert():
            wgu_b[...] = w_gu[0].astype(BF16)
            wdn_b[...] = w_dn[0].astype(BF16)

        x = xbuf[slot]
        y = jnp.zeros((TM_FFN, w_dn.shape[2]), F32)
        for c0 in range(0, EXP_FF, FFN_COLS):
            hg = jnp.dot(x, wgu_b[:, c0:c0 + FFN_COLS], preferred_element_type=F32)
            hu = jnp.dot(x, wgu_b[:, EXP_FF + c0:EXP_FF + c0 + FFN_COLS], preferred_element_type=F32)
            y = y + _bdot(_silu(hg) * hu, wdn_b[c0:c0 + FFN_COLS, :])
        obuf[slot] = y.astype(BF16)
        start_out(i, slot)

    @pl.when(i == n - 1)
    def _drain():
        @pl.when((i >= 1) & (i - 1 < nvalid))
        def _():
            wait_out(1 - slot)

        @pl.when(i < nvalid)
        def _():
            wait_out(slot)


def _final_kernel(ys_ref, x1_ref, rw_ref, pp_ref, ps_ref, ln2_g, ln2_b, w_pg, b_pg, w_ple,
                  yp_ref, ys_out_ref, *, n_prompt_tiles):
    i = pl.program_id(0)
    x1 = x1_ref[...]
    slot = lax.broadcasted_iota(I32, (TL, USED_ROWS), 1).astype(F32)
    parts = []
    for b in range(x1.shape[0] // TL):
        rw = rw_ref[b * TL:(b + 1) * TL, :]
        w1, w2, pos1, pos2 = rw[:, 0:1], rw[:, 1:2], rw[:, 2:3], rw[:, 3:4]
        ys = ys_ref[b * CAP:b * CAP + USED_ROWS, :]
        comb = jnp.where(slot == pos1, w1, jnp.where(slot == pos2, w2, 0.0)).astype(BF16)
        parts.append(jnp.dot(comb, ys, preferred_element_type=F32))
    moe = parts[0] if len(parts) == 1 else jnp.concatenate(parts, axis=0)
    x2 = _ln(ALPHA * x1 + moe, ln2_g[...], ln2_b[...])
    gate = _sigmoid(_bdot(x2, w_pg[...]) + b_pg[...])
    p = jnp.where(i < n_prompt_tiles, pp_ref[...], ps_ref[...])
    y = x2 + gate * _bdot(p, w_ple[...])

    @pl.when(i < n_prompt_tiles)
    def _prompt():
        yp_ref[...] = y

    @pl.when(i >= n_prompt_tiles)
    def _sample():
        ys_out_ref[...] = y


def _rope_tables(pos):
    half = RET_DK // 2
    inv_freq = ROPE_BASE ** (-np.arange(half, dtype=np.float64) / half)
    ang = np.asarray(pos, np.float64)[:, None] * inv_freq[None, :]
    cos = np.cos(ang)
    sin = np.sin(ang)
    return (np.concatenate([cos, cos], axis=-1).astype(np.float32),
            np.concatenate([-sin, sin], axis=-1).astype(np.float32))


def _log_gamma():
    return np.log(1.0 - 2.0 ** (-5.0 - np.arange(RET_HEADS, dtype=np.float64)))


def _const_spec(shape):
    nd = len(shape)
    return pl.BlockSpec(shape, lambda *_: (0,) * nd, pipeline_mode=pl.Buffered(1))


def _chunk_plan(meta, n_blocks, n_ffn_tiles):
    assert n_blocks * BLOCK_SPARE >= N_EXPERTS * (TILE_CHUNKS - 1)
    m = meta.reshape(n_blocks, LANES, LANES)
    cnt = m[:, :N_EXPERTS, 0]
    off = m[:, :N_EXPERTS, 1]
    nch = (cnt + (CHUNK - 1)) // CHUNK
    cum = jnp.cumsum(nch, axis=0)
    total = cum[-1:]
    tiles_e = (total + TILE_CHUNKS - 1) // TILE_CHUNKS
    tile_end = jnp.cumsum(tiles_e, axis=1)
    tile_start = tile_end - tiles_e
    tid = jnp.arange(n_ffn_tiles, dtype=I32)[:, None]
    owner = (tid >= tile_start) & (tid < tile_end)
    pick_e = lambda v: jnp.sum(jnp.where(owner, v, 0), axis=1, keepdims=True)
    te = pick_e(jnp.arange(N_EXPERTS, dtype=I32)[None, :])
    k = (tid - pick_e(tile_start)) * TILE_CHUNKS + jnp.arange(TILE_CHUNKS, dtype=I32)[None, :]
    total_t = pick_e(total)
    real = k < total_t
    by_tile = lambda v: jnp.sum(jnp.where(owner[:, None, :], v[None, :, :], 0), axis=2)
    cum_t = by_tile(cum)
    blk = jnp.minimum(jnp.sum((cum_t[:, None, :] <= k[:, :, None]).astype(I32), axis=2), n_blocks - 1)
    at_blk = blk[:, :, None] == jnp.arange(n_blocks, dtype=I32)[None, None, :]
    pick_b = lambda v: jnp.sum(jnp.where(at_blk, v[:, None, :], 0), axis=2)
    excl = pick_b(cum_t - by_tile(nch))
    off_t = pick_b(by_tile(off))
    spare = te * (TILE_CHUNKS - 1) + jnp.maximum(k - total_t, 0) % TILE_CHUNKS
    spare_chunk = (spare // BLOCK_SPARE) * BLOCK_CHUNKS + BLOCK_USED + spare % BLOCK_SPARE
    chunk = jnp.where(real, blk * BLOCK_CHUNKS + off_t + (k - excl), spare_chunk)
    n_valid = jnp.sum(tiles_e, axis=1)
    te = jnp.where(tid < n_valid, te, N_EXPERTS - 1)
    return te.reshape(-1).astype(I32), n_valid.astype(I32), chunk.reshape(-1).astype(I32)


def kernel(x_prompt, x_sample, state_ret, state_conv, p_prompt, p_sample, w_in, b_in, ret_gn_g, ret_gn_b,
           w_ret_o, conv_w, conv_b, conv_ln_g, conv_ln_b, w_conv_o, w_out, ln1_g, ln1_b, w_grp, b_grp,
           w_exp, b_exp, w_gu, w_dn, ln2_g, ln2_b, w_pg, b_pg, w_ple):
    assert DEPTH == 1 and w_in.shape[0] == 1
    bp, lp, d = x_prompt.shape
    bs, ls, _ = x_sample.shape
    n_p, n_s = bp * lp, bs * ls
    n_tok = n_p + n_s
    assert lp % TL == 0 and n_s % TL == 0 and bs % BB_SAMPLE == 0 and SUBLANES % ls == 0
    n_blocks = n_tok // TL

    f32c = lambda a, shape: jnp.asarray(np.broadcast_to(a, shape).astype(np.float32))
    lg = _log_gamma()
    c = RET_CHUNK
    idx = np.arange(c, dtype=np.float64)
    rel = idx[:, None] - idx[None, :]
    causal = rel >= 0
    decay = np.where(causal[None], np.exp(np.where(causal, rel, 0.0)[None] * lg[:, None, None]), 0.0)
    decay = f32c(decay, decay.shape)
    q_decay = np.exp((idx[:, None] + 1.0) * lg[None, :])
    k_decay = np.exp((c - 1.0 - idx[:, None]) * lg[None, :])
    chunk_decay = np.exp(c * lg)
    qdec_p = f32c(q_decay.T[:, :, None], (RET_HEADS, c, RET_DK))
    kdec_p = f32c(k_decay.T[:, :, None], (RET_HEADS, c, RET_DK))
    cdec_p = f32c(chunk_decay[:, None, None], (RET_HEADS, 1, RET_DV))
    cos_p, sin_p = (jnp.asarray(a) for a in _rope_tables(np.arange(lp)))

    ts = BB_SAMPLE * ls
    idx_s = np.arange(ls, dtype=np.float64)
    pdec_s = np.exp(idx_s[None, :] * lg[:, None])
    pdec_s = f32c(pdec_s[:, :, None, None], (RET_HEADS, ls, 1, RET_DK))
    qd_s = np.exp((idx_s[:, None] + 1.0) * lg[None, :])
    kd_s = np.exp((ls - 1.0 - idx_s[:, None]) * lg[None, :])
    qdec_s = f32c(np.tile(qd_s.T, (1, BB_SAMPLE))[:, :, None], (RET_HEADS, ts, RET_DK))
    kdec_s = f32c(np.tile(kd_s.T, (1, BB_SAMPLE))[:, :, None], (RET_HEADS, ts, RET_DK))
    cdec_s = f32c(np.exp(ls * lg)[:, None, None], (RET_HEADS, 1, RET_DV))
    cos_s, sin_s = (jnp.asarray(a) for a in _rope_tables(np.tile(PAST_LEN + np.arange(ls), BB_SAMPLE)))

    w_in_b = w_in[0].astype(BF16)
    w_ret_o_b = w_ret_o[0].astype(BF16)
    w_conv_o_b = w_conv_o[0].astype(BF16)
    w_out_b = w_out[0].astype(BF16)
    w_pg_b = w_pg[0].astype(BF16)
    w_ple_b = w_ple[0].astype(BF16)
    n_route = N_GROUPS + N_EXPERTS
    w_r = jnp.concatenate([w_grp[0], w_exp[0], jnp.zeros((d, LANES - n_route), F32)], axis=1)
    wr_hi = w_r.astype(BF16)
    wr_lo = (w_r - wr_hi.astype(F32)).astype(BF16)
    b_r = jnp.concatenate([b_grp[0], b_exp[0], jnp.zeros((LANES - n_route,), F32)]).reshape(1, LANES)
    row = lambda a: a.reshape(1, -1)
    conv_w0 = conv_w[0]
    nstate = CONV_WIDTH - 1
    zrows = lambda n: jnp.zeros((n, CONV_CH), F32)
    wsh = jnp.stack([
        jnp.concatenate([zrows(i), conv_w0[:nstate - i], zrows(XPAD_NEW - nstate),
                         conv_w0[nstate - i:], zrows(XPAD_ROWS - XPAD_NEW - i - 1)], axis=0)
        for i in range(ls)])

    rep8 = lambda a: jnp.broadcast_to(a[..., None, :], a.shape[:-1] + (SUBLANES, a.shape[-1]))
    shared_w = (w_in_b, rep8(b_in[0]), row(ret_gn_g[0]), row(ret_gn_b[0]), w_ret_o_b)
    tail_w = (row(conv_ln_g[0]), row(conv_ln_b[0]), w_conv_o_b, w_out_b, row(ln1_g[0]), row(ln1_b[0]),
              wr_hi, wr_lo, b_r)

    nbt = bs // BB_SAMPLE
    xs2 = x_sample.reshape(n_s, d)
    sample_in = ((xs2, cos_s, sin_s, pdec_s, qdec_s, kdec_s, cdec_s, wsh, state_ret, state_conv)
                 + shared_w + (row(conv_b[0]),) + tail_w)
    sample_specs = (
        [pl.BlockSpec((ts, d), lambda i: (i, 0))]
        + [_const_spec(a.shape) for a in sample_in[1:8]]
        + [pl.BlockSpec((1, BB_SAMPLE, RET_HEADS, RET_DK, RET_DV), lambda i: (0, i, 0, 0, 0)),
           pl.BlockSpec((1, BB_SAMPLE, nstate, CONV_CH), lambda i: (0, i, 0, 0))]
        + [_const_spec(a.shape) for a in sample_in[10:]]
    )
    tok_spec_s = lambda w: pl.BlockSpec((ts, w), lambda i: (i, 0))
    x1_s, rw_s, ret_s, conv_s = pl.pallas_call(
        _sample_mixer_kernel,
        grid=(nbt,),
        in_specs=sample_specs,
        out_specs=[
            tok_spec_s(d), tok_spec_s(LANES),
            pl.BlockSpec((1, BB_SAMPLE, RET_HEADS, RET_DK, RET_DV), lambda i: (0, i, 0, 0, 0)),
            pl.BlockSpec((1, BB_SAMPLE, nstate, CONV_CH), lambda i: (0, i, 0, 0)),
        ],
        out_shape=[
            jax.ShapeDtypeStruct((n_s, d), F32),
            jax.ShapeDtypeStruct((n_s, LANES), F32),
            jax.ShapeDtypeStruct(state_ret.shape, F32),
            jax.ShapeDtypeStruct(state_conv.shape, F32),
        ],
        scratch_shapes=[
            pltpu.VMEM((ts, RET_V), F32),
            pltpu.VMEM((CONV_CH // LANES, ts, LANES), F32),
            pltpu.VMEM((BB_SAMPLE, XPAD_ROWS, CONV_CH), F32),
        ],
        compiler_params=pltpu.CompilerParams(
            dimension_semantics=("arbitrary",), vmem_limit_bytes=VMEM_LIMIT),
        name="sample_mixer",
    )(*sample_in)

    assert lp % TLM == 0 and n_s % TLM == 0 and TLM % TL == 0
    nlt = lp // TLM
    npt = n_p // TLM
    nst = n_s // TLM
    sub = TLM // TL
    prompt_in = ((x_prompt, x1_s, rw_s, cos_p, sin_p, decay, qdec_p, kdec_p, cdec_p)
                 + shared_w + (rep8(conv_w0), row(conv_b[0])) + tail_w)
    head_tile = lambda s: jnp.minimum(s, npt - 1)
    sample_tile = lambda s: jnp.maximum(s - npt, 0)
    sample_spec = lambda w: pl.BlockSpec((TLM, w), lambda s: (sample_tile(s), 0))
    prompt_specs = [
        pl.BlockSpec((1, TLM, d), lambda s: (head_tile(s) // nlt, head_tile(s) % nlt, 0)),
        sample_spec(d), sample_spec(LANES),
        pl.BlockSpec((TLM, RET_DK), lambda s: (head_tile(s) % nlt, 0)),
        pl.BlockSpec((TLM, RET_DK), lambda s: (head_tile(s) % nlt, 0)),
    ] + [_const_spec(a.shape) for a in prompt_in[5:]]
    tok_spec_p = lambda rows, w: pl.BlockSpec((rows, w), lambda s: (s, 0))
    x1_all, rw_all, xs_all, meta, ret_p, conv_p = pl.pallas_call(
        functools.partial(_prompt_mixer_kernel, n_tiles=npt, tiles_per_seq=nlt),
        grid=(npt + nst,),
        in_specs=prompt_specs,
        out_specs=[
            tok_spec_p(TLM, d), tok_spec_p(TLM, LANES), tok_spec_p(sub * CAP, d), tok_spec_p(sub * LANES, LANES),
            pl.BlockSpec((1, 1, RET_HEADS, RET_DK, RET_DV), lambda s: (0, head_tile(s) // nlt, 0, 0, 0)),
            pl.BlockSpec((1, 1, nstate, CONV_CH), lambda s: (0, head_tile(s) // nlt, 0, 0)),
        ],
        out_shape=[
            jax.ShapeDtypeStruct((n_tok, d), F32),
            jax.ShapeDtypeStruct((n_tok, LANES), F32),
            jax.ShapeDtypeStruct((n_blocks * CAP, d), BF16),
            jax.ShapeDtypeStruct((n_blocks * LANES, LANES), I32),
            jax.ShapeDtypeStruct((1, bp, RET_HEADS, RET_DK, RET_DV), F32),
            jax.ShapeDtypeStruct((1, bp, nstate, CONV_CH), F32),
        ],
        scratch_shapes=[
            pltpu.VMEM((TLM + CONV_PAD, CONV_CH), F32),
            pltpu.VMEM((SUBLANES - 1, TL + CONV_PAD - SUBLANES, CONV_CH), F32),
            pltpu.VMEM((TLM, 2 * RET_QK + RET_V), F32),
            pltpu.VMEM((TLM, d), BF16),
            pltpu.VMEM((TLM, RET_V), F32),
            pltpu.VMEM((TLM, CONV_CH), F32),
            pltpu.VMEM((TLM, RET_V + 2 * D_MODEL), F32),
        ],
        compiler_params=pltpu.CompilerParams(
            dimension_semantics=("arbitrary",), vmem_limit_bytes=VMEM_LIMIT),
        name="prompt_mixer",
    )(*prompt_in)

    max_chunks = n_blocks * (TOP_K * TL // CHUNK + N_EXPERTS - 1)
    n_ffn_tiles = (max_chunks + N_EXPERTS * (TILE_CHUNKS - 1)) // TILE_CHUNKS
    tile_e, n_valid_tiles, chunk_ids = _chunk_plan(meta, n_blocks, n_ffn_tiles)

    ys_all = pl.pallas_call(
        _ffn_kernel,
        grid_spec=pltpu.PrefetchScalarGridSpec(
            num_scalar_prefetch=3,
            grid=(n_ffn_tiles,),
            in_specs=[
                pl.BlockSpec(memory_space=pl.ANY),
                pl.BlockSpec((1, d, 2 * EXP_FF), lambda i, te, nr, ch: (te[i], 0, 0)),
                pl.BlockSpec((1, EXP_FF, d), lambda i, te, nr, ch: (te[i], 0, 0)),
            ],
            out_specs=pl.BlockSpec(memory_space=pl.ANY),
            scratch_shapes=[
                pltpu.VMEM((2, TM_FFN, d), BF16),
                pltpu.VMEM((2, TM_FFN, d), BF16),
                pltpu.VMEM((d, 2 * EXP_FF), BF16),
                pltpu.VMEM((EXP_FF, d), BF16),
                pltpu.SemaphoreType.DMA((2,)),
                pltpu.SemaphoreType.DMA((2,)),
            ],
        ),
        out_shape=jax.ShapeDtypeStruct(xs_all.shape, BF16),
        input_output_aliases={3: 0},
        compiler_params=pltpu.CompilerParams(
            dimension_semantics=("arbitrary",), vmem_limit_bytes=VMEM_LIMIT),
        name="expert_ffn",
    )(tile_e, n_valid_tiles, chunk_ids, xs_all, w_gu[0], w_dn[0])

    assert n_p % TLF == 0 and n_s % TLF == 0
    npt = n_p // TLF
    fsub = TLF // TL
    pp2 = p_prompt.reshape(n_p, PLE_DIM)
    ps2 = p_sample.reshape(n_s, PLE_DIM)
    tok_f = lambda rows, w: pl.BlockSpec((rows, w), lambda i: (i, 0))
    y_p, y_s = pl.pallas_call(
        functools.partial(_final_kernel, n_prompt_tiles=npt),
        grid=(n_tok // TLF,),
        in_specs=[
            tok_f(fsub * CAP, d), tok_f(TLF, d), tok_f(TLF, LANES),
            pl.BlockSpec((TLF, PLE_DIM), lambda i: (jnp.minimum(i, npt - 1), 0)),
            pl.BlockSpec((TLF, PLE_DIM), lambda i: (jnp.maximum(i - npt, 0), 0)),
            _const_spec((1, d)), _const_spec((1, d)), _const_spec((d, d)), _const_spec((1, d)),
            _const_spec((PLE_DIM, d)),
        ],
        out_specs=[
            pl.BlockSpec((TLF, d), lambda i: (jnp.minimum(i, npt - 1), 0)),
            pl.BlockSpec((TLF, d), lambda i: (jnp.maximum(i - npt, 0), 0)),
        ],
        out_shape=[jax.ShapeDtypeStruct((n_p, d), F32), jax.ShapeDtypeStruct((n_s, d), F32)],
        compiler_params=pltpu.CompilerParams(
            dimension_semantics=("arbitrary",), vmem_limit_bytes=VMEM_LIMIT),
        name="moe_combine_final",
    )(ys_all, x1_all, rw_all, pp2, ps2,
      row(ln2_g[0]), row(ln2_b[0]), w_pg_b, row(b_pg[0]), w_ple_b)

    return (y_p.reshape(bp, lp, d), y_s.reshape(bs, ls, d), ret_p, conv_p, ret_s, conv_s)
```

```python
import functools

import jax
import jax.numpy as jnp
import numpy as np
from jax import lax
from jax.experimental import pallas as pl
from jax.experimental.pallas import tpu as pltpu

F32 = jnp.float32
BF16 = jnp.bfloat16
I32 = jnp.int32

D_MODEL = 1024
PAST_LEN = 16384
RET_HEADS = 4
RET_DK = 128
RET_DV = 128
RET_QK = RET_HEADS * RET_DK
RET_V = RET_HEADS * RET_DV
RET_CHUNK = 128
ROPE_BASE = 10000.0
CONV_CH = 512
CONV_WIDTH = 31
N_GROUPS = 4
EXP_PER_GROUP = 4
N_EXPERTS = N_GROUPS * EXP_PER_GROUP
TOP_K = 2
EXP_FF = 512
PLE_DIM = 256
DEPTH = 1
ALPHA = (2 * DEPTH) ** 0.25
LN_EPS = 1e-5
IN_WIDTHS = (RET_QK, RET_QK, RET_V, RET_V, CONV_CH, CONV_CH, D_MODEL, D_MODEL)
IN_OFFS = tuple(int(s) for s in np.cumsum((0,) + IN_WIDTHS))

LANES = 128
SUBLANES = 8
VMEM_LIMIT = 56 * 1024 * 1024

TL = 256
TLM = 512
TLF = 512
BB_SAMPLE = 16
CHUNK = 2 * SUBLANES
TILE_CHUNKS = 32
BLOCK_USED = -(-(TOP_K * TL + N_EXPERTS * (CHUNK - 1)) // LANES) * LANES // CHUNK
BLOCK_SPARE = LANES // CHUNK
BLOCK_CHUNKS = BLOCK_USED + BLOCK_SPARE
USED_ROWS = BLOCK_USED * CHUNK
CAP = BLOCK_CHUNKS * CHUNK
TM_FFN = TILE_CHUNKS * CHUNK
FFN_COLS = 256
CONV_PAD = 32
XPAD_NEW = 32
XPAD_ROWS = 40


def _ln(x, g, b):
    mu = jnp.mean(x, axis=-1, keepdims=True)
    d = x - mu
    var = jnp.mean(d * d, axis=-1, keepdims=True)
    return d * lax.rsqrt(var + LN_EPS) * g + b


def _sigmoid(x):
    return 1.0 / (1.0 + jnp.exp(-x))


def _rep(v8, rows):
    return v8 if rows == SUBLANES else jnp.concatenate([v8] * (rows // SUBLANES), axis=0)


def _silu(x):
    return x * _sigmoid(x)


def _bdot(a, b):
    return jnp.dot(a.astype(BF16), b, preferred_element_type=F32)


def _rot(t, cosf, sinf):
    return t * cosf + pltpu.roll(t, RET_DK // 2, axis=1) * sinf


def _lane_tile(cols, rows):
    lane = lax.broadcasted_iota(I32, (rows, LANES), 1)
    out = jnp.zeros((rows, LANES), F32)
    for i, col in enumerate(cols):
        out = jnp.where(lane == i, col, out)
    return out


def _route(logits):
    lane = lax.broadcasted_iota(I32, logits.shape, 1)
    lanef = lane.astype(F32)
    ninf = jnp.float32(-jnp.inf)
    big = jnp.float32(LANES)
    gmask = lane < N_GROUPS
    gl = jnp.where(gmask, logits, ninf)
    gmax = jnp.max(gl, axis=1, keepdims=True)
    gidx = jnp.min(jnp.where(gmask & (gl == gmax), lanef, big), axis=1, keepdims=True)
    sumexp = jnp.sum(jnp.where(gmask, jnp.exp(gl - gmax), 0.0), axis=1, keepdims=True)
    gw = 1.0 / sumexp
    lo = N_GROUPS + EXP_PER_GROUP * gidx
    emask = (lanef >= lo) & (lanef < lo + EXP_PER_GROUP)
    el = jnp.where(emask, logits, ninf)
    m1 = jnp.max(el, axis=1, keepdims=True)
    i1 = jnp.min(jnp.where(emask & (el == m1), lanef, big), axis=1, keepdims=True)
    emask2 = emask & (lanef != i1)
    el2 = jnp.where(emask2, logits, ninf)
    m2 = jnp.max(el2, axis=1, keepdims=True)
    i2 = jnp.min(jnp.where(emask2 & (el2 == m2), lanef, big), axis=1, keepdims=True)
    t = jnp.exp(m2 - m1)
    den = 1.0 + t
    return (1.0 / den) * gw, (t / den) * gw, i1 - N_GROUPS, i2 - N_GROUPS


def _post_mix_pieces(src, w, sink):
    (w_ret_o, cln_g, cln_b, w_conv_o, w_out, ln1_g, ln1_b, wr_hi, wr_lo, b_r) = w
    st = {}

    def branch_a():
        st["a"] = _bdot(_silu(src["g"]()) * src["ret"](), w_ret_o[...])

    def branch_b():
        st["b"] = _bdot(_silu(_ln(src["cout"](), cln_g[...], cln_b[...])), w_conv_o[...])

    def merge():
        mix = _sigmoid(src["gt_a"]()) * st["a"] + _sigmoid(src["gt_b"]()) * st["b"]
        h = ALPHA * src["x"]() + _bdot(mix, w_out[...])
        st["x1"] = _ln(h, ln1_g[...], ln1_b[...])

    def router():
        x1 = st["x1"]
        x1_hi = x1.astype(BF16)
        x1_lo = (x1 - x1_hi.astype(F32)).astype(BF16)
        st["logits"] = (jnp.dot(x1_hi, wr_hi[...], preferred_element_type=F32)
                        + (jnp.dot(x1_lo, wr_hi[...], preferred_element_type=F32)
                           + jnp.dot(x1_hi, wr_lo[...], preferred_element_type=F32))
                        + b_r[...])

    def route():
        st["route"] = _route(st["logits"])

    def finish():
        sink(st["x1"], *st["route"])

    return [branch_a, branch_b, merge, router, route, finish]


def _sort_tile(x1, w1, w2, e1, e2, x1_ref, rw_ref, xs_ref, meta_ref):
    t = x1.shape[0]
    ids_t = _lane_tile((e1, e2), t).T
    e1r, e2r = ids_t[0:1, :], ids_t[1:2, :]
    sub = lax.broadcasted_iota(I32, (LANES, t), 0).astype(F32)
    a1 = (sub == e1r).astype(F32)
    a2 = (sub == e2r).astype(F32)
    ri = lax.broadcasted_iota(I32, (t, t), 0)
    ci = lax.broadcasted_iota(I32, (t, t), 1)
    earlier = (ri < ci).astype(BF16)
    r1 = jnp.dot(a1.astype(BF16), earlier, preferred_element_type=F32)
    r2 = jnp.dot(a2.astype(BF16), earlier, preferred_element_type=F32)
    cnt1 = jnp.sum(a1, axis=1, keepdims=True)
    cnt = cnt1 + jnp.sum(a2, axis=1, keepdims=True)
    nch = jnp.floor((cnt + (CHUNK - 1.0)) * (1.0 / CHUNK))
    ui = lax.broadcasted_iota(I32, (LANES, LANES), 0)
    uj = lax.broadcasted_iota(I32, (LANES, LANES), 1)
    before = (uj < ui).astype(BF16)
    off = jnp.dot(before, jnp.broadcast_to(nch, (LANES, LANES)).astype(BF16),
                  preferred_element_type=F32)[:, 0:1]
    base = off * CHUNK
    pos1r = jnp.sum(a1 * (base + r1), axis=0, keepdims=True)
    pos2r = jnp.sum(a2 * (base + cnt1 + r2), axis=0, keepdims=True)
    slot = lax.broadcasted_iota(I32, (CAP, t), 0).astype(F32)
    onehot = ((slot == pos1r) | (slot == pos2r)).astype(BF16)
    xs = jnp.dot(onehot, x1.astype(BF16), preferred_element_type=F32)
    pos_cols = jnp.where(sub == 2.0, pos1r, jnp.where(sub == 3.0, pos2r, 0.0)).T
    lane = lax.broadcasted_iota(I32, (t, LANES), 1)
    x1_ref[...] = x1
    rw_ref[...] = jnp.where(lane == 0, w1, jnp.where(lane == 1, w2, pos_cols))
    xs_ref[...] = xs.astype(BF16)
    mlane = lax.broadcasted_iota(I32, (LANES, LANES), 1)
    meta = jnp.where(mlane == 0, cnt, jnp.where(mlane == 1, off, 0.0))
    meta_ref[...] = meta.astype(I32)


def _sort_tiles(x1, w1, w2, e1, e2, x1_ref, rw_ref, xs_ref, meta_ref):
    for i in range(x1.shape[0] // TL):
        rows = slice(i * TL, (i + 1) * TL)
        _sort_tile(x1[rows], w1[rows], w2[rows], e1[rows], e2[rows],
                   x1_ref.at[pl.ds(i * TL, TL)], rw_ref.at[pl.ds(i * TL, TL)],
                   xs_ref.at[pl.ds(i * CAP, CAP)], meta_ref.at[pl.ds(i * LANES, LANES)])


GATE_COLS = {3: 0, 6: RET_V, 7: RET_V + D_MODEL}
QKV_COLS = {0: 0, 1: RET_QK, 2: 2 * RET_QK}


def _prompt_mixer_kernel(x_ref, x1s_ref, rws_ref, cos_ref, sin_ref, dec_ref, qdec_ref, kdec_ref, cdec_ref,
                         w_in, b_in, gn_g, gn_b, w_ret_o, conv_w, conv_b, cln_g, cln_b,
                         w_conv_o, w_out, ln1_g, ln1_b, wr_hi, wr_lo, b_r,
                         x1_ref, rw_ref, xs_ref, meta_ref, sret_ref, sconv_ref,
                         ubuf, ushift, qkv_scr, xb_scr, ret_scr, cout_scr, gate_scr,
                         *, n_tiles, tiles_per_seq):
    s = pl.program_id(0)
    li = lax.rem(s, tiles_per_seq)
    outs = (x1_ref, rw_ref, xs_ref, meta_ref)
    slot = dict(ret=ret_scr, cout=cout_scr, gates=gate_scr)
    tail_w = (w_ret_o, cln_g, cln_b, w_conv_o, w_out, ln1_g, ln1_b, wr_hi, wr_lo, b_r)

    @pl.when((s < n_tiles) & (li == 0))
    def _new_sequence():
        sret_ref[...] = jnp.zeros(sret_ref.shape, F32)
        ubuf[0:CONV_PAD, :] = jnp.zeros((CONV_PAD, CONV_CH), F32)

    @pl.when(s < n_tiles)
    def _mix():
        gcols = lambda kk: slice(GATE_COLS[kk], GATE_COLS[kk] + IN_WIDTHS[kk])
        src = dict(x=lambda: x_ref[0], ret=lambda: ret_scr[...], cout=lambda: cout_scr[...],
                   g=lambda: gate_scr[:, gcols(3)], gt_a=lambda: gate_scr[:, gcols(6)],
                   gt_b=lambda: gate_scr[:, gcols(7)])
        head = _prompt_head_pieces(x_ref, cos_ref, sin_ref, dec_ref, qdec_ref, kdec_ref, cdec_ref,
                                   w_in, b_in, gn_g, gn_b, conv_w, conv_b, sret_ref,
                                   ubuf, ushift, qkv_scr, xb_scr, slot)
        tail = _post_mix_pieces(src, tail_w, lambda *r: _sort_tiles(*r, *outs))
        for piece in head + tail:
            piece()

    @pl.when(s >= n_tiles)
    def _append():
        rws = rws_ref[...]
        _sort_tiles(x1s_ref[...], rws[:, 0:1], rws[:, 1:2], rws[:, 2:3], rws[:, 3:4], *outs)

    @pl.when((s < n_tiles) & (li == tiles_per_seq - 1))
    def _conv_state():
        sconv_ref[0, 0] = ubuf[CONV_PAD - (CONV_WIDTH - 1):CONV_PAD, :]


def _prompt_head_pieces(x_ref, cos_ref, sin_ref, dec_ref, qdec_ref, kdec_ref, cdec_ref,
                        w_in, b_in, gn_g, gn_b, conv_w, conv_b, sret_ref,
                        ubuf, ushift, qkv_scr, xb_scr, slot):
    tl = x_ref.shape[1]
    st = {}

    def slab_dot(c0, c1):
        return jnp.dot(xb_scr[...], w_in[:, c0:c1], preferred_element_type=F32) + _rep(b_in[:, c0:c1], tl)

    def glu():
        xb_scr[...] = x_ref[0].astype(BF16)
        u = slab_dot(IN_OFFS[4], IN_OFFS[5]) * _sigmoid(slab_dot(IN_OFFS[5], IN_OFFS[6]))
        ubuf[CONV_PAD:CONV_PAD + tl, :] = u

    nsh = ushift.shape[1]
    span = nsh - (CONV_PAD - SUBLANES)

    def shift_copy(h, s):
        ushift[s - 1] = ubuf[h * span + s:h * span + s + nsh, :]

    slab = 256
    slabs = [(kk, c0) for kk in (0, 1, 2, 3, 6, 7) for c0 in range(IN_OFFS[kk], IN_OFFS[kk + 1], slab)]
    rb = 32
    nrb = tl // rb

    def proj_slab(kk, c0):
        val = slab_dot(c0, c0 + slab)
        if kk in QKV_COLS:
            dst = QKV_COLS[kk] + c0 - IN_OFFS[kk]
            qkv_scr[:, dst:dst + slab] = val
        else:
            dst = GATE_COLS[kk] + c0 - IN_OFFS[kk]
            slot["gates"][:, dst:dst + slab] = val

    def conv_block(r):
        h, rl = divmod(r * rb, span)
        acc = jnp.zeros((rb, CONV_CH), F32) + conv_b[...]
        for j in range(CONV_WIDTH):
            off = j + (CONV_PAD - (CONV_WIDTH - 1))
            s = off % SUBLANES
            base = rl + off - s
            win = (ubuf[h * span + base:h * span + base + rb, :] if s == 0
                   else ushift[s - 1, base:base + rb, :])
            acc = acc + _rep(conv_w[j], rb) * win
        slot["cout"][r * rb:(r + 1) * rb, :] = acc
        if r == nrb - 1:
            ubuf[0:CONV_PAD, :] = ubuf[tl:tl + CONV_PAD, :]

    scale = RET_DK ** -0.5

    def retention(c, h):
        rows = slice(c * RET_CHUNK, (c + 1) * RET_CHUNK)
        cols = slice(h * RET_DK, (h + 1) * RET_DK)
        hcol = lambda kk: slice(QKV_COLS[kk] + h * RET_DK, QKV_COLS[kk] + (h + 1) * RET_DK)
        cosf = cos_ref[rows, :]
        sinf = sin_ref[rows, :]
        qh = _rot(qkv_scr[rows, hcol(0)], cosf, sinf)
        kh = _rot(qkv_scr[rows, hcol(1)], cosf, sinf) * scale
        qb = qh.astype(BF16)
        kb = kh.astype(BF16)
        vb = qkv_scr[rows, hcol(2)].astype(BF16)
        s_old = sret_ref[0, 0, h]
        scores = lax.dot_general(qb, kb, (((1,), (1,)), ((), ())),
                                 preferred_element_type=F32) * dec_ref[h]
        inner = jnp.dot(scores.astype(BF16), vb, preferred_element_type=F32)
        cross = jnp.dot(qb, s_old.astype(BF16), preferred_element_type=F32) * qdec_ref[h]
        kd = (kh * kdec_ref[h]).astype(BF16)
        s_new = cdec_ref[h] * s_old + lax.dot_general(
            kd, vb, (((0,), (0,)), ((), ())), preferred_element_type=F32)
        sret_ref[0, 0, h] = s_new
        slot["ret"][rows, cols] = _ln(inner + cross, gn_g[:, cols], gn_b[:, cols])

    vector_pieces = []
    for r in range(nrb):
        if (r * rb) % span == 0:
            vector_pieces += [lambda h=(r * rb) // span, s=s: shift_copy(h, s) for s in range(1, SUBLANES)]
        vector_pieces.append(lambda r=r: conv_block(r))
    pieces = [glu]
    for i, piece in enumerate(vector_pieces):
        pieces.append(piece)
        for kk, c0 in slabs[i * len(slabs) // len(vector_pieces):(i + 1) * len(slabs) // len(vector_pieces)]:
            pieces.append(lambda kk=kk, c0=c0: proj_slab(kk, c0))
    pieces += [lambda c=c, h=h: retention(c, h) for c in range(tl // RET_CHUNK) for h in range(RET_HEADS)]
    return pieces


def _sample_mixer_kernel(x_ref, cos_ref, sin_ref, pdec_ref, qdec_ref, kdec_ref, cdec_ref, wsh_ref,
                         sret_in, sconv_in,
                         w_in, b_in, gn_g, gn_b, w_ret_o, conv_b, cln_g, cln_b,
                         w_conv_o, w_out, ln1_g, ln1_b, wr_hi, wr_lo, b_r,
                         x1_ref, rw_ref, sret_ref, sconv_ref,
                         ret_scr, cout_scr, xpad):
    t = x_ref.shape[0]
    ls = t // BB_SAMPLE
    x = x_ref[...]
    xb = x.astype(BF16)

    def proj(k):
        c0, c1 = IN_OFFS[k], IN_OFFS[k + 1]
        return jnp.dot(xb, w_in[:, c0:c1], preferred_element_type=F32) + _rep(b_in[:, c0:c1], t)

    q = proj(0)
    k = proj(1)
    v = proj(2)
    scale = RET_DK ** -0.5
    cosf = cos_ref[...]
    sinf = sin_ref[...]
    row = lax.broadcasted_iota(I32, (t, RET_DK), 0)
    pos = row % ls
    row8 = lax.broadcasted_iota(I32, (SUBLANES, RET_DK), 0)
    per_tile = SUBLANES // ls
    for h in range(RET_HEADS):
        cols = slice(h * RET_DK, (h + 1) * RET_DK)
        qh = _rot(q[:, cols], cosf, sinf)
        kh = _rot(k[:, cols], cosf, sinf) * scale
        vh = v[:, cols]
        inner = jnp.zeros((t, RET_DV), F32)
        for s in range(ls):
            ks = kh if s == 0 else pltpu.roll(kh, s, axis=0)
            vs = vh if s == 0 else pltpu.roll(vh, s, axis=0)
            dotp = jnp.sum(qh * ks, axis=1, keepdims=True) * pdec_ref[h, s]
            inner = inner + jnp.where(pos >= s, dotp, 0.0) * vs
        kd = kh * kdec_ref[h]
        for tile in range(t // SUBLANES):
            rows = slice(tile * SUBLANES, (tile + 1) * SUBLANES)
            q8 = qh[rows, :]
            kd8 = kd[rows, :]
            v8 = vh[rows, :]
            seqs = [tile * per_tile + sub for sub in range(per_tile)]
            mine = [(row8 >= sub * ls) & (row8 < (sub + 1) * ls) for sub in range(per_tile)]
            s_old = [sret_in[0, b, h] for b in seqs]
            c_all = jnp.dot(q8, jnp.concatenate(s_old, axis=1), preferred_element_type=F32)
            upd = lax.dot_general(jnp.concatenate([jnp.where(m, kd8, 0.0) for m in mine], axis=1), v8,
                                  (((0,), (0,)), ((), ())), preferred_element_type=F32)
            cross8 = jnp.zeros((SUBLANES, RET_DV), F32)
            for sub, b in enumerate(seqs):
                cross8 = jnp.where(mine[sub], c_all[:, sub * RET_DV:(sub + 1) * RET_DV], cross8)
                sret_ref[0, b, h] = cdec_ref[h] * s_old[sub] + upd[sub * RET_DK:(sub + 1) * RET_DK, :]
            ret_scr[rows, cols] = inner[rows, :] + cross8 * qdec_ref[h, rows, :]
        ret_scr[:, cols] = _ln(ret_scr[:, cols], gn_g[:, cols], gn_b[:, cols])

    u = proj(4) * _sigmoid(proj(5))
    nstate = CONV_WIDTH - 1
    xpad[...] = jnp.zeros(xpad.shape, F32)
    xpad[:, 0:nstate, :] = sconv_in[0]
    for b in range(BB_SAMPLE):
        xpad[b, XPAD_NEW:XPAD_NEW + ls, :] = u[b * ls:(b + 1) * ls, :]
    for p in range(ls):
        res = jnp.sum(xpad[...] * wsh_ref[p][None], axis=1) + conv_b[...]
        for sl in range(CONV_CH // LANES):
            cout_scr[sl, pl.ds(p, BB_SAMPLE, stride=ls), :] = res[:, sl * LANES:(sl + 1) * LANES]
    sconv_ref[0, :, 0:nstate - ls, :] = xpad[:, ls:nstate, :]
    sconv_ref[0, :, nstate - ls:nstate, :] = xpad[:, XPAD_NEW:XPAD_NEW + ls, :]
    c_out = jnp.concatenate([cout_scr[sl] for sl in range(CONV_CH // LANES)], axis=1)

    def sink(x1, w1, w2, e1, e2):
        x1_ref[...] = x1
        rw_ref[...] = _lane_tile((w1, w2, e1, e2), t)

    src = dict(x=lambda: x, ret=lambda: ret_scr[...], cout=lambda: c_out,
               g=lambda: proj(3), gt_a=lambda: proj(6), gt_b=lambda: proj(7))
    for piece in _post_mix_pieces(
            src, (w_ret_o, cln_g, cln_b, w_conv_o, w_out, ln1_g, ln1_b, wr_hi, wr_lo, b_r), sink):
        piece()


def _ffn_kernel(te_ref, nvalid_ref, chunk_ref, xs_hbm, w_gu, w_dn, ys_hbm,
                xbuf, obuf, wgu_b, wdn_b, sem_in, sem_out):
    del xs_hbm
    i = pl.program_id(0)
    n = pl.num_programs(0)
    slot = i % 2
    nvalid = nvalid_ref[0]

    def chunk_rows(tile, c):
        return pl.ds(pl.multiple_of(chunk_ref[tile * TILE_CHUNKS + c] * CHUNK, CHUNK), CHUNK)

    def start_in(tile, s):
        for c in range(TILE_CHUNKS):
            pltpu.make_async_copy(ys_hbm.at[chunk_rows(tile, c)],
                                  xbuf.at[s, pl.ds(c * CHUNK, CHUNK)], sem_in.at[s]).start()

    def start_out(tile, s):
        for c in range(TILE_CHUNKS):
            pltpu.make_async_copy(obuf.at[s, pl.ds(c * CHUNK, CHUNK)],
                                  ys_hbm.at[chunk_rows(tile, c)], sem_out.at[s]).start()

    def wait_in(s):
        pltpu.make_async_copy(ys_hbm.at[pl.ds(0, TM_FFN)], xbuf.at[s], sem_in.at[s]).wait()

    def wait_out(s):
        pltpu.make_async_copy(obuf.at[s], ys_hbm.at[pl.ds(0, TM_FFN)], sem_out.at[s]).wait()

    @pl.when((i == 0) & (nvalid > 0))
    def _first():
        start_in(0, 0)

    @pl.when(i + 1 < nvalid)
    def _prefetch():
        start_in(i + 1, 1 - slot)

    @pl.when((i >= 2) & (i - 2 < nvalid))
    def _retire():
        wait_out(slot)

    @pl.when(i < nvalid)
    def _tile():
        wait_in(slot)
        prev = te_ref[jnp.maximum(i - 1, 0)]

        @pl.when((i == 0) | (te_ref[i] != prev))
        def _new_expert():
            wgu_b[...] = w_gu[0].astype(BF16)
            wdn_b[...] = w_dn[0].astype(BF16)

        x = xbuf[slot]
        y = jnp.zeros((TM_FFN, w_dn.shape[2]), F32)
        for c0 in range(0, EXP_FF, FFN_COLS):
            hg = jnp.dot(x, wgu_b[:, c0:c0 + FFN_COLS], preferred_element_type=F32)
            hu = jnp.dot(x, wgu_b[:, EXP_FF + c0:EXP_FF + c0 + FFN_COLS], preferred_element_type=F32)
            y = y + _bdot(_silu(hg) * hu, wdn_b[c0:c0 + FFN_COLS, :])
        obuf[slot] = y.astype(BF16)
        start_out(i, slot)

    @pl.when(i == n - 1)
    def _drain():
        @pl.when((i >= 1) & (i - 1 < nvalid))
        def _():
            wait_out(1 - slot)

        @pl.when(i < nvalid)
        def _():
            wait_out(slot)


def _final_kernel(ys_ref, x1_ref, rw_ref, pp_ref, ps_ref, ln2_g, ln2_b, w_pg, b_pg, w_ple,
                  yp_ref, ys_out_ref, *, n_prompt_tiles):
    i = pl.program_id(0)
    x1 = x1_ref[...]
    slot = lax.broadcasted_iota(I32, (TL, USED_ROWS), 1).astype(F32)
    parts = []
    for b in range(x1.shape[0] // TL):
        rw = rw_ref[b * TL:(b + 1) * TL, :]
        w1, w2, pos1, pos2 = rw[:, 0:1], rw[:, 1:2], rw[:, 2:3], rw[:, 3:4]
        ys = ys_ref[b * CAP:b * CAP + USED_ROWS, :]
        comb = jnp.where(slot == pos1, w1, jnp.where(slot == pos2, w2, 0.0)).astype(BF16)
        parts.append(jnp.dot(comb, ys, preferred_element_type=F32))
    moe = parts[0] if len(parts) == 1 else jnp.concatenate(parts, axis=0)
    x2 = _ln(ALPHA * x1 + moe, ln2_g[...], ln2_b[...])
    gate = _sigmoid(_bdot(x2, w_pg[...]) + b_pg[...])
    p = jnp.where(i < n_prompt_tiles, pp_ref[...], ps_ref[...])
    y = x2 + gate * _bdot(p, w_ple[...])

    @pl.when(i < n_prompt_tiles)
    def _prompt():
        yp_ref[...] = y

    @pl.when(i >= n_prompt_tiles)
    def _sample():
        ys_out_ref[...] = y


def _rope_tables(pos):
    half = RET_DK // 2
    inv_freq = ROPE_BASE ** (-np.arange(half, dtype=np.float64) / half)
    ang = np.asarray(pos, np.float64)[:, None] * inv_freq[None, :]
    cos = np.cos(ang)
    sin = np.sin(ang)
    return (np.concatenate([cos, cos], axis=-1).astype(np.float32),
            np.concatenate([-sin, sin], axis=-1).astype(np.float32))


def _log_gamma():
    return np.log(1.0 - 2.0 ** (-5.0 - np.arange(RET_HEADS, dtype=np.float64)))


def _const_spec(shape):
    nd = len(shape)
    return pl.BlockSpec(shape, lambda *_: (0,) * nd, pipeline_mode=pl.Buffered(1))


def _chunk_plan(meta, n_blocks, n_ffn_tiles):
    assert n_blocks * BLOCK_SPARE >= N_EXPERTS * (TILE_CHUNKS - 1)
    m = meta.reshape(n_blocks, LANES, LANES)
    cnt = m[:, :N_EXPERTS, 0]
    off = m[:, :N_EXPERTS, 1]
    nch = (cnt + (CHUNK - 1)) // CHUNK
    cum = jnp.cumsum(nch, axis=0)
    total = cum[-1:]
    tiles_e = (total + TILE_CHUNKS - 1) // TILE_CHUNKS
    tile_end = jnp.cumsum(tiles_e, axis=1)
    tile_start = tile_end - tiles_e
    tid = jnp.arange(n_ffn_tiles, dtype=I32)[:, None]
    owner = (tid >= tile_start) & (tid < tile_end)
    pick_e = lambda v: jnp.sum(jnp.where(owner, v, 0), axis=1, keepdims=True)
    te = pick_e(jnp.arange(N_EXPERTS, dtype=I32)[None, :])
    k = (tid - pick_e(tile_start)) * TILE_CHUNKS + jnp.arange(TILE_CHUNKS, dtype=I32)[None, :]
    total_t = pick_e(total)
    real = k < total_t
    by_tile = lambda v: jnp.sum(jnp.where(owner[:, None, :], v[None, :, :], 0), axis=2)
    cum_t = by_tile(cum)
    blk = jnp.minimum(jnp.sum((cum_t[:, None, :] <= k[:, :, None]).astype(I32), axis=2), n_blocks - 1)
    at_blk = blk[:, :, None] == jnp.arange(n_blocks, dtype=I32)[None, None, :]
    pick_b = lambda v: jnp.sum(jnp.where(at_blk, v[:, None, :], 0), axis=2)
    excl = pick_b(cum_t - by_tile(nch))
    off_t = pick_b(by_tile(off))
    spare = te * (TILE_CHUNKS - 1) + jnp.maximum(k - total_t, 0) % TILE_CHUNKS
    spare_chunk = (spare // BLOCK_SPARE) * BLOCK_CHUNKS + BLOCK_USED + spare % BLOCK_SPARE
    chunk = jnp.where(real, blk * BLOCK_CHUNKS + off_t + (k - excl), spare_chunk)
    n_valid = jnp.sum(tiles_e, axis=1)
    te = jnp.where(tid < n_valid, te, N_EXPERTS - 1)
    return te.reshape(-1).astype(I32), n_valid.astype(I32), chunk.reshape(-1).astype(I32)


def kernel(x_prompt, x_sample, state_ret, state_conv, p_prompt, p_sample, w_in, b_in, ret_gn_g, ret_gn_b,
           w_ret_o, conv_w, conv_b, conv_ln_g, conv_ln_b, w_conv_o, w_out, ln1_g, ln1_b, w_grp, b_grp,
           w_exp, b_exp, w_gu, w_dn, ln2_g, ln2_b, w_pg, b_pg, w_ple):
    assert DEPTH == 1 and w_in.shape[0] == 1
    bp, lp, d = x_prompt.shape
    bs, ls, _ = x_sample.shape
    n_p, n_s = bp * lp, bs * ls
    n_tok = n_p + n_s
    assert lp % TL == 0 and n_s % TL == 0 and bs % BB_SAMPLE == 0 and SUBLANES % ls == 0
    n_blocks = n_tok // TL

    f32c = lambda a, shape: jnp.asarray(np.broadcast_to(a, shape).astype(np.float32))
    lg = _log_gamma()
    c = RET_CHUNK
    idx = np.arange(c, dtype=np.float64)
    rel = idx[:, None] - idx[None, :]
    causal = rel >= 0
    decay = np.where(causal[None], np.exp(np.where(causal, rel, 0.0)[None] * lg[:, None, None]), 0.0)
    decay = f32c(decay, decay.shape)
    q_decay = np.exp((idx[:, None] + 1.0) * lg[None, :])
    k_decay = np.exp((c - 1.0 - idx[:, None]) * lg[None, :])
    chunk_decay = np.exp(c * lg)
    qdec_p = f32c(q_decay.T[:, :, None], (RET_HEADS, c, RET_DK))
    kdec_p = f32c(k_decay.T[:, :, None], (RET_HEADS, c, RET_DK))
    cdec_p = f32c(chunk_decay[:, None, None], (RET_HEADS, 1, RET_DV))
    cos_p, sin_p = (jnp.asarray(a) for a in _rope_tables(np.arange(lp)))

    ts = BB_SAMPLE * ls
    idx_s = np.arange(ls, dtype=np.float64)
    pdec_s = np.exp(idx_s[None, :] * lg[:, None])
    pdec_s = f32c(pdec_s[:, :, None, None], (RET_HEADS, ls, 1, RET_DK))
    qd_s = np.exp((idx_s[:, None] + 1.0) * lg[None, :])
    kd_s = np.exp((ls - 1.0 - idx_s[:, None]) * lg[None, :])
    qdec_s = f32c(np.tile(qd_s.T, (1, BB_SAMPLE))[:, :, None], (RET_HEADS, ts, RET_DK))
    kdec_s = f32c(np.tile(kd_s.T, (1, BB_SAMPLE))[:, :, None], (RET_HEADS, ts, RET_DK))
    cdec_s = f32c(np.exp(ls * lg)[:, None, None], (RET_HEADS, 1, RET_DV))
    cos_s, sin_s = (jnp.asarray(a) for a in _rope_tables(np.tile(PAST_LEN + np.arange(ls), BB_SAMPLE)))

    w_in_b = w_in[0].astype(BF16)
    w_ret_o_b = w_ret_o[0].astype(BF16)
    w_conv_o_b = w_conv_o[0].astype(BF16)
    w_out_b = w_out[0].astype(BF16)
    w_pg_b = w_pg[0].astype(BF16)
    w_ple_b = w_ple[0].astype(BF16)
    n_route = N_GROUPS + N_EXPERTS
    w_r = jnp.concatenate([w_grp[0], w_exp[0], jnp.zeros((d, LANES - n_route), F32)], axis=1)
    wr_hi = w_r.astype(BF16)
    wr_lo = (w_r - wr_hi.astype(F32)).astype(BF16)
    b_r = jnp.concatenate([b_grp[0], b_exp[0], jnp.zeros((LANES - n_route,), F32)]).reshape(1, LANES)
    row = lambda a: a.reshape(1, -1)
    conv_w0 = conv_w[0]
    nstate = CONV_WIDTH - 1
    zrows = lambda n: jnp.zeros((n, CONV_CH), F32)
    wsh = jnp.stack([
        jnp.concatenate([zrows(i), conv_w0[:nstate - i], zrows(XPAD_NEW - nstate),
                         conv_w0[nstate - i:], zrows(XPAD_ROWS - XPAD_NEW - i - 1)], axis=0)
        for i in range(ls)])

    rep8 = lambda a: jnp.broadcast_to(a[..., None, :], a.shape[:-1] + (SUBLANES, a.shape[-1]))
    shared_w = (w_in_b, rep8(b_in[0]), row(ret_gn_g[0]), row(ret_gn_b[0]), w_ret_o_b)
    tail_w = (row(conv_ln_g[0]), row(conv_ln_b[0]), w_conv_o_b, w_out_b, row(ln1_g[0]), row(ln1_b[0]),
              wr_hi, wr_lo, b_r)

    nbt = bs // BB_SAMPLE
    xs2 = x_sample.reshape(n_s, d)
    sample_in = ((xs2, cos_s, sin_s, pdec_s, qdec_s, kdec_s, cdec_s, wsh, state_ret, state_conv)
                 + shared_w + (row(conv_b[0]),) + tail_w)
    sample_specs = (
        [pl.BlockSpec((ts, d), lambda i: (i, 0))]
        + [_const_spec(a.shape) for a in sample_in[1:8]]
        + [pl.BlockSpec((1, BB_SAMPLE, RET_HEADS, RET_DK, RET_DV), lambda i: (0, i, 0, 0, 0)),
           pl.BlockSpec((1, BB_SAMPLE, nstate, CONV_CH), lambda i: (0, i, 0, 0))]
        + [_const_spec(a.shape) for a in sample_in[10:]]
    )
    tok_spec_s = lambda w: pl.BlockSpec((ts, w), lambda i: (i, 0))
    x1_s, rw_s, ret_s, conv_s = pl.pallas_call(
        _sample_mixer_kernel,
        grid=(nbt,),
        in_specs=sample_specs,
        out_specs=[
            tok_spec_s(d), tok_spec_s(LANES),
            pl.BlockSpec((1, BB_SAMPLE, RET_HEADS, RET_DK, RET_DV), lambda i: (0, i, 0, 0, 0)),
            pl.BlockSpec((1, BB_SAMPLE, nstate, CONV_CH), lambda i: (0, i, 0, 0)),
        ],
        out_shape=[
            jax.ShapeDtypeStruct((n_s, d), F32),
            jax.ShapeDtypeStruct((n_s, LANES), F32),
            jax.ShapeDtypeStruct(state_ret.shape, F32),
            jax.ShapeDtypeStruct(state_conv.shape, F32),
        ],
        scratch_shapes=[
            pltpu.VMEM((ts, RET_V), F32),
            pltpu.VMEM((CONV_CH // LANES, ts, LANES), F32),
            pltpu.VMEM((BB_SAMPLE, XPAD_ROWS, CONV_CH), F32),
        ],
        compiler_params=pltpu.CompilerParams(
            dimension_semantics=("arbitrary",), vmem_limit_bytes=VMEM_LIMIT),
        name="sample_mixer",
    )(*sample_in)

    assert lp % TLM == 0 and n_s % TLM == 0 and TLM % TL == 0
    nlt = lp // TLM
    npt = n_p // TLM
    nst = n_s // TLM
    sub = TLM // TL
    prompt_in = ((x_prompt, x1_s, rw_s, cos_p, sin_p, decay, qdec_p, kdec_p, cdec_p)
                 + shared_w + (rep8(conv_w0), row(conv_b[0])) + tail_w)
    head_tile = lambda s: jnp.minimum(s, npt - 1)
    sample_tile = lambda s: jnp.maximum(s - npt, 0)
    sample_spec = lambda w: pl.BlockSpec((TLM, w), lambda s: (sample_tile(s), 0))
    prompt_specs = [
        pl.BlockSpec((1, TLM, d), lambda s: (head_tile(s) // nlt, head_tile(s) % nlt, 0)),
        sample_spec(d), sample_spec(LANES),
        pl.BlockSpec((TLM, RET_DK), lambda s: (head_tile(s) % nlt, 0)),
        pl.BlockSpec((TLM, RET_DK), lambda s: (head_tile(s) % nlt, 0)),
    ] + [_const_spec(a.shape) for a in prompt_in[5:]]
    tok_spec_p = lambda rows, w: pl.BlockSpec((rows, w), lambda s: (s, 0))
    x1_all, rw_all, xs_all, meta, ret_p, conv_p = pl.pallas_call(
        functools.partial(_prompt_mixer_kernel, n_tiles=npt, tiles_per_seq=nlt),
        grid=(npt + nst,),
        in_specs=prompt_specs,
        out_specs=[
            tok_spec_p(TLM, d), tok_spec_p(TLM, LANES), tok_spec_p(sub * CAP, d), tok_spec_p(sub * LANES, LANES),
            pl.BlockSpec((1, 1, RET_HEADS, RET_DK, RET_DV), lambda s: (0, head_tile(s) // nlt, 0, 0, 0)),
            pl.BlockSpec((1, 1, nstate, CONV_CH), lambda s: (0, head_tile(s) // nlt, 0, 0)),
        ],
        out_shape=[
            jax.ShapeDtypeStruct((n_tok, d), F32),
            jax.ShapeDtypeStruct((n_tok, LANES), F32),
            jax.ShapeDtypeStruct((n_blocks * CAP, d), BF16),
            jax.ShapeDtypeStruct((n_blocks * LANES, LANES), I32),
            jax.ShapeDtypeStruct((1, bp, RET_HEADS, RET_DK, RET_DV), F32),
            jax.ShapeDtypeStruct((1, bp, nstate, CONV_CH), F32),
        ],
        scratch_shapes=[
            pltpu.VMEM((TLM + CONV_PAD, CONV_CH), F32),
            pltpu.VMEM((SUBLANES - 1, TL + CONV_PAD - SUBLANES, CONV_CH), F32),
            pltpu.VMEM((TLM, 2 * RET_QK + RET_V), F32),
            pltpu.VMEM((TLM, d), BF16),
            pltpu.VMEM((TLM, RET_V), F32),
            pltpu.VMEM((TLM, CONV_CH), F32),
            pltpu.VMEM((TLM, RET_V + 2 * D_MODEL), F32),
        ],
        compiler_params=pltpu.CompilerParams(
            dimension_semantics=("arbitrary",), vmem_limit_bytes=VMEM_LIMIT),
        name="prompt_mixer",
    )(*prompt_in)

    max_chunks = n_blocks * (TOP_K * TL // CHUNK + N_EXPERTS - 1)
    n_ffn_tiles = (max_chunks + N_EXPERTS * (TILE_CHUNKS - 1)) // TILE_CHUNKS
    tile_e, n_valid_tiles, chunk_ids = _chunk_plan(meta, n_blocks, n_ffn_tiles)

    ys_all = pl.pallas_call(
        _ffn_kernel,
        grid_spec=pltpu.PrefetchScalarGridSpec(
            num_scalar_prefetch=3,
            grid=(n_ffn_tiles,),
            in_specs=[
                pl.BlockSpec(memory_space=pl.ANY),
                pl.BlockSpec((1, d, 2 * EXP_FF), lambda i, te, nr, ch: (te[i], 0, 0)),
                pl.BlockSpec((1, EXP_FF, d), lambda i, te, nr, ch: (te[i], 0, 0)),
            ],
            out_specs=pl.BlockSpec(memory_space=pl.ANY),
            scratch_shapes=[
                pltpu.VMEM((2, TM_FFN, d), BF16),
                pltpu.VMEM((2, TM_FFN, d), BF16),
                pltpu.VMEM((d, 2 * EXP_FF), BF16),
                pltpu.VMEM((EXP_FF, d), BF16),
                pltpu.SemaphoreType.DMA((2,)),
                pltpu.SemaphoreType.DMA((2,)),
            ],
        ),
        out_shape=jax.ShapeDtypeStruct(xs_all.shape, BF16),
        input_output_aliases={3: 0},
        compiler_params=pltpu.CompilerParams(
            dimension_semantics=("arbitrary",), vmem_limit_bytes=VMEM_LIMIT),
        name="expert_ffn",
    )(tile_e, n_valid_tiles, chunk_ids, xs_all, w_gu[0], w_dn[0])

    assert n_p % TLF == 0 and n_s % TLF == 0
    npt = n_p // TLF
    fsub = TLF // TL
    pp2 = p_prompt.reshape(n_p, PLE_DIM)
    ps2 = p_sample.reshape(n_s, PLE_DIM)
    tok_f = lambda rows, w: pl.BlockSpec((rows, w), lambda i: (i, 0))
    y_p, y_s = pl.pallas_call(
        functools.partial(_final_kernel, n_prompt_tiles=npt),
        grid=(n_tok // TLF,),
        in_specs=[
            tok_f(fsub * CAP, d), tok_f(TLF, d), tok_f(TLF, LANES),
            pl.BlockSpec((TLF, PLE_DIM), lambda i: (jnp.minimum(i, npt - 1), 0)),
            pl.BlockSpec((TLF, PLE_DIM), lambda i: (jnp.maximum(i - npt, 0), 0)),
            _const_spec((1, d)), _const_spec((1, d)), _const_spec((d, d)), _const_spec((1, d)),
            _const_spec((PLE_DIM, d)),
        ],
        out_specs=[
            pl.BlockSpec((TLF, d), lambda i: (jnp.minimum(i, npt - 1), 0)),
            pl.BlockSpec((TLF, d), lambda i: (jnp.maximum(i - npt, 0), 0)),
        ],
        out_shape=[jax.ShapeDtypeStruct((n_p, d), F32), jax.ShapeDtypeStruct((n_s, d), F32)],
        compiler_params=pltpu.CompilerParams(
            dimension_semantics=("arbitrary",), vmem_limit_bytes=VMEM_LIMIT),
        name="moe_combine_final",
    )(ys_all, x1_all, rw_all, pp2, ps2,
      row(ln2_g[0]), row(ln2_b[0]), w_pg_b, row(b_pg[0]), w_ple_b)

    return (y_p.reshape(bp, lp, d), y_s.reshape(bs, ls, d), ret_p, conv_p, ret_s, conv_s)
```

```python
import functools

import jax
import jax.numpy as jnp
import numpy as np
from jax import lax
from jax.experimental import pallas as pl
from jax.experimental.pallas import tpu as pltpu

F32 = jnp.float32
BF16 = jnp.bfloat16
I32 = jnp.int32

D_MODEL = 1024
PAST_LEN = 16384
RET_HEADS = 4
RET_DK = 128
RET_DV = 128
RET_QK = RET_HEADS * RET_DK
RET_V = RET_HEADS * RET_DV
RET_CHUNK = 128
ROPE_BASE = 10000.0
CONV_CH = 512
CONV_WIDTH = 31
N_GROUPS = 4
EXP_PER_GROUP = 4
N_EXPERTS = N_GROUPS * EXP_PER_GROUP
TOP_K = 2
EXP_FF = 512
PLE_DIM = 256
DEPTH = 1
ALPHA = (2 * DEPTH) ** 0.25
LN_EPS = 1e-5
IN_WIDTHS = (RET_QK, RET_QK, RET_V, RET_V, CONV_CH, CONV_CH, D_MODEL, D_MODEL)
IN_OFFS = tuple(int(s) for s in np.cumsum((0,) + IN_WIDTHS))

LANES = 128
SUBLANES = 8
VMEM_LIMIT = 56 * 1024 * 1024

TL = 256
TLM = 512
TLF = 512
BB_SAMPLE = 16
CHUNK = 2 * SUBLANES
TILE_CHUNKS = 32
BLOCK_USED = -(-(TOP_K * TL + N_EXPERTS * (CHUNK - 1)) // LANES) * LANES // CHUNK
BLOCK_SPARE = LANES // CHUNK
BLOCK_CHUNKS = BLOCK_USED + BLOCK_SPARE
USED_ROWS = BLOCK_USED * CHUNK
CAP = BLOCK_CHUNKS * CHUNK
TM_FFN = TILE_CHUNKS * CHUNK
FFN_COLS = 256
CONV_PAD = 32
XPAD_NEW = 32
XPAD_ROWS = 40


def _ln(x, g, b):
    mu = jnp.mean(x, axis=-1, keepdims=True)
    d = x - mu
    var = jnp.mean(d * d, axis=-1, keepdims=True)
    return d * lax.rsqrt(var + LN_EPS) * g + b


def _sigmoid(x):
    return 1.0 / (1.0 + jnp.exp(-x))


def _rep(v8, rows):
    return v8 if rows == SUBLANES else jnp.concatenate([v8] * (rows // SUBLANES), axis=0)


def _silu(x):
    return x * _sigmoid(x)


def _bdot(a, b):
    return jnp.dot(a.astype(BF16), b, preferred_element_type=F32)


def _rot(t, cosf, sinf):
    return t * cosf + pltpu.roll(t, RET_DK // 2, axis=1) * sinf


def _lane_tile(cols, rows):
    lane = lax.broadcasted_iota(I32, (rows, LANES), 1)
    out = jnp.zeros((rows, LANES), F32)
    for i, col in enumerate(cols):
        out = jnp.where(lane == i, col, out)
    return out


def _route(logits):
    lane = lax.broadcasted_iota(I32, logits.shape, 1)
    lanef = lane.astype(F32)
    ninf = jnp.float32(-jnp.inf)
    big = jnp.float32(LANES)
    gmask = lane < N_GROUPS
    gl = jnp.where(gmask, logits, ninf)
    gmax = jnp.max(gl, axis=1, keepdims=True)
    gidx = jnp.min(jnp.where(gmask & (gl == gmax), lanef, big), axis=1, keepdims=True)
    sumexp = jnp.sum(jnp.where(gmask, jnp.exp(gl - gmax), 0.0), axis=1, keepdims=True)
    gw = 1.0 / sumexp
    lo = N_GROUPS + EXP_PER_GROUP * gidx
    emask = (lanef >= lo) & (lanef < lo + EXP_PER_GROUP)
    el = jnp.where(emask, logits, ninf)
    m1 = jnp.max(el, axis=1, keepdims=True)
    i1 = jnp.min(jnp.where(emask & (el == m1), lanef, big), axis=1, keepdims=True)
    emask2 = emask & (lanef != i1)
    el2 = jnp.where(emask2, logits, ninf)
    m2 = jnp.max(el2, axis=1, keepdims=True)
    i2 = jnp.min(jnp.where(emask2 & (el2 == m2), lanef, big), axis=1, keepdims=True)
    t = jnp.exp(m2 - m1)
    den = 1.0 + t
    return (1.0 / den) * gw, (t / den) * gw, i1 - N_GROUPS, i2 - N_GROUPS


def _post_mix_pieces(src, w, sink):
    (w_ret_o, cln_g, cln_b, w_conv_o, w_out, ln1_g, ln1_b, wr_hi, wr_lo, b_r) = w
    st = {}

    def branch_a():
        st["a"] = _bdot(_silu(src["g"]()) * src["ret"](), w_ret_o[...])

    def branch_b():
        st["b"] = _bdot(_silu(_ln(src["cout"](), cln_g[...], cln_b[...])), w_conv_o[...])

    def merge():
        mix = _sigmoid(src["gt_a"]()) * st["a"] + _sigmoid(src["gt_b"]()) * st["b"]
        h = ALPHA * src["x"]() + _bdot(mix, w_out[...])
        st["x1"] = _ln(h, ln1_g[...], ln1_b[...])

    def router():
        x1 = st["x1"]
        x1_hi = x1.astype(BF16)
        x1_lo = (x1 - x1_hi.astype(F32)).astype(BF16)
        st["logits"] = (jnp.dot(x1_hi, wr_hi[...], preferred_element_type=F32)
                        + (jnp.dot(x1_lo, wr_hi[...], preferred_element_type=F32)
                           + jnp.dot(x1_hi, wr_lo[...], preferred_element_type=F32))
                        + b_r[...])

    def route():
        st["route"] = _route(st["logits"])

    def finish():
        sink(st["x1"], *st["route"])

    return [branch_a, branch_b, merge, router, route, finish]


def _sort_tile(x1, w1, w2, e1, e2, x1_ref, rw_ref, xs_ref, meta_ref):
    t = x1.shape[0]
    ids_t = _lane_tile((e1, e2), t).T
    e1r, e2r = ids_t[0:1, :], ids_t[1:2, :]
    sub = lax.broadcasted_iota(I32, (LANES, t), 0).astype(F32)
    a1 = (sub == e1r).astype(F32)
    a2 = (sub == e2r).astype(F32)
    ri = lax.broadcasted_iota(I32, (t, t), 0)
    ci = lax.broadcasted_iota(I32, (t, t), 1)
    earlier = (ri < ci).astype(BF16)
    r1 = jnp.dot(a1.astype(BF16), earlier, preferred_element_type=F32)
    r2 = jnp.dot(a2.astype(BF16), earlier, preferred_element_type=F32)
    cnt1 = jnp.sum(a1, axis=1, keepdims=True)
    cnt = cnt1 + jnp.sum(a2, axis=1, keepdims=True)
    nch = jnp.floor((cnt + (CHUNK - 1.0)) * (1.0 / CHUNK))
    ui = lax.broadcasted_iota(I32, (LANES, LANES), 0)
    uj = lax.broadcasted_iota(I32, (LANES, LANES), 1)
    before = (uj < ui).astype(BF16)
    off = jnp.dot(before, jnp.broadcast_to(nch, (LANES, LANES)).astype(BF16),
                  preferred_element_type=F32)[:, 0:1]
    base = off * CHUNK
    pos1r = jnp.sum(a1 * (base + r1), axis=0, keepdims=True)
    pos2r = jnp.sum(a2 * (base + cnt1 + r2), axis=0, keepdims=True)
    slot = lax.broadcasted_iota(I32, (CAP, t), 0).astype(F32)
    onehot = ((slot == pos1r) | (slot == pos2r)).astype(BF16)
    xs = jnp.dot(onehot, x1.astype(BF16), preferred_element_type=F32)
    pos_cols = jnp.where(sub == 2.0, pos1r, jnp.where(sub == 3.0, pos2r, 0.0)).T
    lane = lax.broadcasted_iota(I32, (t, LANES), 1)
    x1_ref[...] = x1
    rw_ref[...] = jnp.where(lane == 0, w1, jnp.where(lane == 1, w2, pos_cols))
    xs_ref[...] = xs.astype(BF16)
    mlane = lax.broadcasted_iota(I32, (LANES, LANES), 1)
    meta = jnp.where(mlane == 0, cnt, jnp.where(mlane == 1, off, 0.0))
    meta_ref[...] = meta.astype(I32)


def _sort_tiles(x1, w1, w2, e1, e2, x1_ref, rw_ref, xs_ref, meta_ref):
    for i in range(x1.shape[0] // TL):
        rows = slice(i * TL, (i + 1) * TL)
        _sort_tile(x1[rows], w1[rows], w2[rows], e1[rows], e2[rows],
                   x1_ref.at[pl.ds(i * TL, TL)], rw_ref.at[pl.ds(i * TL, TL)],
                   xs_ref.at[pl.ds(i * CAP, CAP)], meta_ref.at[pl.ds(i * LANES, LANES)])


GATE_COLS = {3: 0, 6: RET_V, 7: RET_V + D_MODEL}
QKV_COLS = {0: 0, 1: RET_QK, 2: 2 * RET_QK}


def _prompt_mixer_kernel(x_ref, x1s_ref, rws_ref, cos_ref, sin_ref, dec_ref, qdec_ref, kdec_ref, cdec_ref,
                         w_in, b_in, gn_g, gn_b, w_ret_o, conv_w, conv_b, cln_g, cln_b,
                         w_conv_o, w_out, ln1_g, ln1_b, wr_hi, wr_lo, b_r,
                         x1_ref, rw_ref, xs_ref, meta_ref, sret_ref, sconv_ref,
                         ubuf, ushift, qkv_scr, xb_scr, ret_scr, cout_scr, gate_scr,
                         *, n_tiles, tiles_per_seq):
    s = pl.program_id(0)
    li = lax.rem(s, tiles_per_seq)
    outs = (x1_ref, rw_ref, xs_ref, meta_ref)
    slot = dict(ret=ret_scr, cout=cout_scr, gates=gate_scr)
    tail_w = (w_ret_o, cln_g, cln_b, w_conv_o, w_out, ln1_g, ln1_b, wr_hi, wr_lo, b_r)

    @pl.when((s < n_tiles) & (li == 0))
    def _new_sequence():
        sret_ref[...] = jnp.zeros(sret_ref.shape, F32)
        ubuf[0:CONV_PAD, :] = jnp.zeros((CONV_PAD, CONV_CH), F32)

    @pl.when(s < n_tiles)
    def _mix():
        gcols = lambda kk: slice(GATE_COLS[kk], GATE_COLS[kk] + IN_WIDTHS[kk])
        src = dict(x=lambda: x_ref[0], ret=lambda: ret_scr[...], cout=lambda: cout_scr[...],
                   g=lambda: gate_scr[:, gcols(3)], gt_a=lambda: gate_scr[:, gcols(6)],
                   gt_b=lambda: gate_scr[:, gcols(7)])
        head = _prompt_head_pieces(x_ref, cos_ref, sin_ref, dec_ref, qdec_ref, kdec_ref, cdec_ref,
                                   w_in, b_in, gn_g, gn_b, conv_w, conv_b, sret_ref,
                                   ubuf, ushift, qkv_scr, xb_scr, slot)
        tail = _post_mix_pieces(src, tail_w, lambda *r: _sort_tiles(*r, *outs))
        for piece in head + tail:
            piece()

    @pl.when(s >= n_tiles)
    def _append():
        rws = rws_ref[...]
        _sort_tiles(x1s_ref[...], rws[:, 0:1], rws[:, 1:2], rws[:, 2:3], rws[:, 3:4], *outs)

    @pl.when((s < n_tiles) & (li == tiles_per_seq - 1))
    def _conv_state():
        sconv_ref[0, 0] = ubuf[CONV_PAD - (CONV_WIDTH - 1):CONV_PAD, :]


def _prompt_head_pieces(x_ref, cos_ref, sin_ref, dec_ref, qdec_ref, kdec_ref, cdec_ref,
                        w_in, b_in, gn_g, gn_b, conv_w, conv_b, sret_ref,
                        ubuf, ushift, qkv_scr, xb_scr, slot):
    tl = x_ref.shape[1]
    st = {}

    def slab_dot(c0, c1):
        return jnp.dot(xb_scr[...], w_in[:, c0:c1], preferred_element_type=F32) + _rep(b_in[:, c0:c1], tl)

    def glu():
        xb_scr[...] = x_ref[0].astype(BF16)
        u = slab_dot(IN_OFFS[4], IN_OFFS[5]) * _sigmoid(slab_dot(IN_OFFS[5], IN_OFFS[6]))
        ubuf[CONV_PAD:CONV_PAD + tl, :] = u

    nsh = ushift.shape[1]
    span = nsh - (CONV_PAD - SUBLANES)

    def shift_copy(h, s):
        ushift[s - 1] = ubuf[h * span + s:h * span + s + nsh, :]

    slab = 256
    slabs = [(kk, c0) for kk in (0, 1, 2, 3, 6, 7) for c0 in range(IN_OFFS[kk], IN_OFFS[kk + 1], slab)]
    rb = 32
    nrb = tl // rb

    def proj_slab(kk, c0):
        val = slab_dot(c0, c0 + slab)
        if kk in QKV_COLS:
            dst = QKV_COLS[kk] + c0 - IN_OFFS[kk]
            qkv_scr[:, dst:dst + slab] = val
        else:
            dst = GATE_COLS[kk] + c0 - IN_OFFS[kk]
            slot["gates"][:, dst:dst + slab] = val

    def conv_block(r):
        h, rl = divmod(r * rb, span)
        acc = jnp.zeros((rb, CONV_CH), F32) + conv_b[...]
        for j in range(CONV_WIDTH):
            off = j + (CONV_PAD - (CONV_WIDTH - 1))
            s = off % SUBLANES
            base = rl + off - s
            win = (ubuf[h * span + base:h * span + base + rb, :] if s == 0
                   else ushift[s - 1, base:base + rb, :])
            acc = acc + _rep(conv_w[j], rb) * win
        slot["cout"][r * rb:(r + 1) * rb, :] = acc
        if r == nrb - 1:
            ubuf[0:CONV_PAD, :] = ubuf[tl:tl + CONV_PAD, :]

    scale = RET_DK ** -0.5

    def retention(c, h):
        rows = slice(c * RET_CHUNK, (c + 1) * RET_CHUNK)
        cols = slice(h * RET_DK, (h + 1) * RET_DK)
        hcol = lambda kk: slice(QKV_COLS[kk] + h * RET_DK, QKV_COLS[kk] + (h + 1) * RET_DK)
        cosf = cos_ref[rows, :]
        sinf = sin_ref[rows, :]
        qh = _rot(qkv_scr[rows, hcol(0)], cosf, sinf)
        kh = _rot(qkv_scr[rows, hcol(1)], cosf, sinf) * scale
        qb = qh.astype(BF16)
        kb = kh.astype(BF16)
        vb = qkv_scr[rows, hcol(2)].astype(BF16)
        s_old = sret_ref[0, 0, h]
        scores = lax.dot_general(qb, kb, (((1,), (1,)), ((), ())),
                                 preferred_element_type=F32) * dec_ref[h]
        inner = jnp.dot(scores.astype(BF16), vb, preferred_element_type=F32)
        cross = jnp.dot(qb, s_old.astype(BF16), preferred_element_type=F32) * qdec_ref[h]
        kd = (kh * kdec_ref[h]).astype(BF16)
        s_new = cdec_ref[h] * s_old + lax.dot_general(
            kd, vb, (((0,), (0,)), ((), ())), preferred_element_type=F32)
        sret_ref[0, 0, h] = s_new
        slot["ret"][rows, cols] = _ln(inner + cross, gn_g[:, cols], gn_b[:, cols])

    vector_pieces = []
    for r in range(nrb):
        if (r * rb) % span == 0:
            vector_pieces += [lambda h=(r * rb) // span, s=s: shift_copy(h, s) for s in range(1, SUBLANES)]
        vector_pieces.append(lambda r=r: conv_block(r))
    pieces = [glu]
    for i, piece in enumerate(vector_pieces):
        pieces.append(piece)
        for kk, c0 in slabs[i * len(slabs) // len(vector_pieces):(i + 1) * len(slabs) // len(vector_pieces)]:
            pieces.append(lambda kk=kk, c0=c0: proj_slab(kk, c0))
    pieces += [lambda c=c, h=h: retention(c, h) for c in range(tl // RET_CHUNK) for h in range(RET_HEADS)]
    return pieces


def _sample_mixer_kernel(x_ref, cos_ref, sin_ref, pdec_ref, qdec_ref, kdec_ref, cdec_ref, conv_w,
                         sret_in, sconv_in,
                         w_in, b_in, gn_g, gn_b, w_ret_o, conv_b, cln_g, cln_b,
                         w_conv_o, w_out, ln1_g, ln1_b, wr_hi, wr_lo, b_r,
                         x1_ref, rw_ref, sret_ref, sconv_ref,
                         ret_scr, cout_scr, xpad, wsh):
    t = x_ref.shape[0]
    ls = t // BB_SAMPLE
    nstate = CONV_WIDTH - 1
    x = x_ref[...]
    xb = x.astype(BF16)

    @pl.when(pl.program_id(0) == 0)
    def _tap_tables():
        wsh[...] = jnp.zeros(wsh.shape, F32)
        for p in range(ls):
            wsh[p, p:nstate, :] = conv_w[0:nstate - p, :]
            wsh[p, XPAD_NEW:XPAD_NEW + p + 1, :] = conv_w[nstate - p:CONV_WIDTH, :]

    def proj(k):
        c0, c1 = IN_OFFS[k], IN_OFFS[k + 1]
        return jnp.dot(xb, w_in[:, c0:c1], preferred_element_type=F32) + _rep(b_in[:, c0:c1], t)

    q = proj(0)
    k = proj(1)
    v = proj(2)
    scale = RET_DK ** -0.5
    cosf = cos_ref[...]
    sinf = sin_ref[...]
    row = lax.broadcasted_iota(I32, (t, RET_DK), 0)
    pos = row % ls
    row8 = lax.broadcasted_iota(I32, (SUBLANES, RET_DK), 0)
    per_tile = SUBLANES // ls
    for h in range(RET_HEADS):
        cols = slice(h * RET_DK, (h + 1) * RET_DK)
        qh = _rot(q[:, cols], cosf, sinf)
        kh = _rot(k[:, cols], cosf, sinf) * scale
        vh = v[:, cols]
        inner = jnp.zeros((t, RET_DV), F32)
        for s in range(ls):
            ks = kh if s == 0 else pltpu.roll(kh, s, axis=0)
            vs = vh if s == 0 else pltpu.roll(vh, s, axis=0)
            dotp = jnp.sum(qh * ks, axis=1, keepdims=True) * pdec_ref[h, s]
            inner = inner + jnp.where(pos >= s, dotp, 0.0) * vs
        kd = kh * kdec_ref[h]
        for tile in range(t // SUBLANES):
            rows = slice(tile * SUBLANES, (tile + 1) * SUBLANES)
            q8 = qh[rows, :]
            kd8 = kd[rows, :]
            v8 = vh[rows, :]
            seqs = [tile * per_tile + sub for sub in range(per_tile)]
            mine = [(row8 >= sub * ls) & (row8 < (sub + 1) * ls) for sub in range(per_tile)]
            s_old = [sret_in[0, b, h] for b in seqs]
            c_all = jnp.dot(q8, jnp.concatenate(s_old, axis=1), preferred_element_type=F32)
            upd = lax.dot_general(jnp.concatenate([jnp.where(m, kd8, 0.0) for m in mine], axis=1), v8,
                                  (((0,), (0,)), ((), ())), preferred_element_type=F32)
            cross8 = jnp.zeros((SUBLANES, RET_DV), F32)
            for sub, b in enumerate(seqs):
                cross8 = jnp.where(mine[sub], c_all[:, sub * RET_DV:(sub + 1) * RET_DV], cross8)
                sret_ref[0, b, h] = cdec_ref[h] * s_old[sub] + upd[sub * RET_DK:(sub + 1) * RET_DK, :]
            ret_scr[rows, cols] = inner[rows, :] + cross8 * qdec_ref[h, rows, :]
        ret_scr[:, cols] = _ln(ret_scr[:, cols], gn_g[:, cols], gn_b[:, cols])

    u = proj(4) * _sigmoid(proj(5))
    xpad[...] = jnp.zeros(xpad.shape, F32)
    xpad[:, 0:nstate, :] = sconv_in[0]
    for b in range(BB_SAMPLE):
        xpad[b, XPAD_NEW:XPAD_NEW + ls, :] = u[b * ls:(b + 1) * ls, :]
    for p in range(ls):
        res = jnp.sum(xpad[...] * wsh[p][None], axis=1) + conv_b[...]
        for sl in range(CONV_CH // LANES):
            cout_scr[sl, pl.ds(p, BB_SAMPLE, stride=ls), :] = res[:, sl * LANES:(sl + 1) * LANES]
    sconv_ref[0, :, 0:nstate - ls, :] = xpad[:, ls:nstate, :]
    sconv_ref[0, :, nstate - ls:nstate, :] = xpad[:, XPAD_NEW:XPAD_NEW + ls, :]
    c_out = jnp.concatenate([cout_scr[sl] for sl in range(CONV_CH // LANES)], axis=1)

    def sink(x1, w1, w2, e1, e2):
        x1_ref[...] = x1
        rw_ref[...] = _lane_tile((w1, w2, e1, e2), t)

    src = dict(x=lambda: x, ret=lambda: ret_scr[...], cout=lambda: c_out,
               g=lambda: proj(3), gt_a=lambda: proj(6), gt_b=lambda: proj(7))
    for piece in _post_mix_pieces(
            src, (w_ret_o, cln_g, cln_b, w_conv_o, w_out, ln1_g, ln1_b, wr_hi, wr_lo, b_r), sink):
        piece()


def _ffn_kernel(te_ref, nvalid_ref, chunk_ref, xs_hbm, w_gu, w_dn, ys_hbm,
                xbuf, obuf, wgu_b, wdn_b, sem_in, sem_out):
    del xs_hbm
    i = pl.program_id(0)
    n = pl.num_programs(0)
    slot = i % 2
    nvalid = nvalid_ref[0]

    def chunk_rows(tile, c):
        return pl.ds(pl.multiple_of(chunk_ref[tile * TILE_CHUNKS + c] * CHUNK, CHUNK), CHUNK)

    def start_in(tile, s):
        for c in range(TILE_CHUNKS):
            pltpu.make_async_copy(ys_hbm.at[chunk_rows(tile, c)],
                                  xbuf.at[s, pl.ds(c * CHUNK, CHUNK)], sem_in.at[s]).start()

    def start_out(tile, s):
        for c in range(TILE_CHUNKS):
            pltpu.make_async_copy(obuf.at[s, pl.ds(c * CHUNK, CHUNK)],
                                  ys_hbm.at[chunk_rows(tile, c)], sem_out.at[s]).start()

    def wait_in(s):
        pltpu.make_async_copy(ys_hbm.at[pl.ds(0, TM_FFN)], xbuf.at[s], sem_in.at[s]).wait()

    def wait_out(s):
        pltpu.make_async_copy(obuf.at[s], ys_hbm.at[pl.ds(0, TM_FFN)], sem_out.at[s]).wait()

    @pl.when((i == 0) & (nvalid > 0))
    def _first():
        start_in(0, 0)

    @pl.when(i + 1 < nvalid)
    def _prefetch():
        start_in(i + 1, 1 - slot)

    @pl.when((i >= 2) & (i - 2 < nvalid))
    def _retire():
        wait_out(slot)

    @pl.when(i < nvalid)
    def _tile():
        wait_in(slot)
        prev = te_ref[jnp.maximum(i - 1, 0)]

        @pl.when((i == 0) | (te_ref[i] != prev))
        def _new_expert():
            wgu_b[...] = w_gu[0].astype(BF16)
            wdn_b[...] = w_dn[0].astype(BF16)

        x = xbuf[slot]
        y = jnp.zeros((TM_FFN, w_dn.shape[2]), F32)
        for c0 in range(0, EXP_FF, FFN_COLS):
            hg = jnp.dot(x, wgu_b[:, c0:c0 + FFN_COLS], preferred_element_type=F32)
            hu = jnp.dot(x, wgu_b[:, EXP_FF + c0:EXP_FF + c0 + FFN_COLS], preferred_element_type=F32)
            y = y + _bdot(_silu(hg) * hu, wdn_b[c0:c0 + FFN_COLS, :])
        obuf[slot] = y.astype(BF16)
        start_out(i, slot)

    @pl.when(i == n - 1)
    def _drain():
        @pl.when((i >= 1) & (i - 1 < nvalid))
        def _():
            wait_out(1 - slot)

        @pl.when(i < nvalid)
        def _():
            wait_out(slot)


def _final_kernel(ys_ref, x1_ref, rw_ref, pp_ref, ps_ref, ln2_g, ln2_b, w_pg, b_pg, w_ple,
                  yp_ref, ys_out_ref, *, n_prompt_tiles):
    i = pl.program_id(0)
    x1 = x1_ref[...]
    slot = lax.broadcasted_iota(I32, (TL, USED_ROWS), 1).astype(F32)
    parts = []
    for b in range(x1.shape[0] // TL):
        rw = rw_ref[b * TL:(b + 1) * TL, :]
        w1, w2, pos1, pos2 = rw[:, 0:1], rw[:, 1:2], rw[:, 2:3], rw[:, 3:4]
        ys = ys_ref[b * CAP:b * CAP + USED_ROWS, :]
        comb = jnp.where(slot == pos1, w1, jnp.where(slot == pos2, w2, 0.0)).astype(BF16)
        parts.append(jnp.dot(comb, ys, preferred_element_type=F32))
    moe = parts[0] if len(parts) == 1 else jnp.concatenate(parts, axis=0)
    x2 = _ln(ALPHA * x1 + moe, ln2_g[...], ln2_b[...])
    gate = _sigmoid(_bdot(x2, w_pg[...]) + b_pg[...])
    p = jnp.where(i < n_prompt_tiles, pp_ref[...], ps_ref[...])
    y = x2 + gate * _bdot(p, w_ple[...])

    @pl.when(i < n_prompt_tiles)
    def _prompt():
        yp_ref[...] = y

    @pl.when(i >= n_prompt_tiles)
    def _sample():
        ys_out_ref[...] = y


def _rope_tables(pos):
    half = RET_DK // 2
    inv_freq = ROPE_BASE ** (-np.arange(half, dtype=np.float64) / half)
    ang = np.asarray(pos, np.float64)[:, None] * inv_freq[None, :]
    cos = np.cos(ang)
    sin = np.sin(ang)
    return (np.concatenate([cos, cos], axis=-1).astype(np.float32),
            np.concatenate([-sin, sin], axis=-1).astype(np.float32))


def _log_gamma():
    return np.log(1.0 - 2.0 ** (-5.0 - np.arange(RET_HEADS, dtype=np.float64)))


def _const_spec(shape):
    nd = len(shape)
    return pl.BlockSpec(shape, lambda *_: (0,) * nd, pipeline_mode=pl.Buffered(1))


def _chunk_plan(meta, n_blocks, n_ffn_tiles):
    assert n_blocks * BLOCK_SPARE >= N_EXPERTS * (TILE_CHUNKS - 1)
    m = meta.reshape(n_blocks, LANES, LANES)
    cnt = m[:, :N_EXPERTS, 0]
    off = m[:, :N_EXPERTS, 1]
    nch = (cnt + (CHUNK - 1)) // CHUNK
    cum = jnp.cumsum(nch, axis=0)
    total = cum[-1:]
    tiles_e = (total + TILE_CHUNKS - 1) // TILE_CHUNKS
    tile_end = jnp.cumsum(tiles_e, axis=1)
    tile_start = tile_end - tiles_e
    tid = jnp.arange(n_ffn_tiles, dtype=I32)[:, None]
    owner = (tid >= tile_start) & (tid < tile_end)
    pick_e = lambda v: jnp.sum(jnp.where(owner, v, 0), axis=1, keepdims=True)
    te = pick_e(jnp.arange(N_EXPERTS, dtype=I32)[None, :])
    k = (tid - pick_e(tile_start)) * TILE_CHUNKS + jnp.arange(TILE_CHUNKS, dtype=I32)[None, :]
    total_t = pick_e(total)
    real = k < total_t
    by_tile = lambda v: jnp.sum(jnp.where(owner[:, None, :], v[None, :, :], 0), axis=2)
    cum_t = by_tile(cum)
    blk = jnp.minimum(jnp.sum((cum_t[:, None, :] <= k[:, :, None]).astype(I32), axis=2), n_blocks - 1)
    at_blk = blk[:, :, None] == jnp.arange(n_blocks, dtype=I32)[None, None, :]
    pick_b = lambda v: jnp.sum(jnp.where(at_blk, v[:, None, :], 0), axis=2)
    excl = pick_b(cum_t - by_tile(nch))
    off_t = pick_b(by_tile(off))
    spare = te * (TILE_CHUNKS - 1) + jnp.maximum(k - total_t, 0) % TILE_CHUNKS
    spare_chunk = (spare // BLOCK_SPARE) * BLOCK_CHUNKS + BLOCK_USED + spare % BLOCK_SPARE
    chunk = jnp.where(real, blk * BLOCK_CHUNKS + off_t + (k - excl), spare_chunk)
    n_valid = jnp.sum(tiles_e, axis=1)
    te = jnp.where(tid < n_valid, te, N_EXPERTS - 1)
    return te.reshape(-1).astype(I32), n_valid.astype(I32), chunk.reshape(-1).astype(I32)


def kernel(x_prompt, x_sample, state_ret, state_conv, p_prompt, p_sample, w_in, b_in, ret_gn_g, ret_gn_b,
           w_ret_o, conv_w, conv_b, conv_ln_g, conv_ln_b, w_conv_o, w_out, ln1_g, ln1_b, w_grp, b_grp,
           w_exp, b_exp, w_gu, w_dn, ln2_g, ln2_b, w_pg, b_pg, w_ple):
    assert DEPTH == 1 and w_in.shape[0] == 1
    bp, lp, d = x_prompt.shape
    bs, ls, _ = x_sample.shape
    n_p, n_s = bp * lp, bs * ls
    n_tok = n_p + n_s
    assert lp % TL == 0 and n_s % TL == 0 and bs % BB_SAMPLE == 0 and SUBLANES % ls == 0
    n_blocks = n_tok // TL

    f32c = lambda a, shape: jnp.asarray(np.broadcast_to(a, shape).astype(np.float32))
    lg = _log_gamma()
    c = RET_CHUNK
    idx = np.arange(c, dtype=np.float64)
    rel = idx[:, None] - idx[None, :]
    causal = rel >= 0
    decay = np.where(causal[None], np.exp(np.where(causal, rel, 0.0)[None] * lg[:, None, None]), 0.0)
    decay = f32c(decay, decay.shape)
    q_decay = np.exp((idx[:, None] + 1.0) * lg[None, :])
    k_decay = np.exp((c - 1.0 - idx[:, None]) * lg[None, :])
    chunk_decay = np.exp(c * lg)
    qdec_p = f32c(q_decay.T[:, :, None], (RET_HEADS, c, RET_DK))
    kdec_p = f32c(k_decay.T[:, :, None], (RET_HEADS, c, RET_DK))
    cdec_p = f32c(chunk_decay[:, None, None], (RET_HEADS, 1, RET_DV))
    cos_p, sin_p = (jnp.asarray(a) for a in _rope_tables(np.arange(lp)))

    ts = BB_SAMPLE * ls
    idx_s = np.arange(ls, dtype=np.float64)
    pdec_s = np.exp(idx_s[None, :] * lg[:, None])
    pdec_s = f32c(pdec_s[:, :, None, None], (RET_HEADS, ls, 1, RET_DK))
    qd_s = np.exp((idx_s[:, None] + 1.0) * lg[None, :])
    kd_s = np.exp((ls - 1.0 - idx_s[:, None]) * lg[None, :])
    qdec_s = f32c(np.tile(qd_s.T, (1, BB_SAMPLE))[:, :, None], (RET_HEADS, ts, RET_DK))
    kdec_s = f32c(np.tile(kd_s.T, (1, BB_SAMPLE))[:, :, None], (RET_HEADS, ts, RET_DK))
    cdec_s = f32c(np.exp(ls * lg)[:, None, None], (RET_HEADS, 1, RET_DV))
    cos_s, sin_s = (jnp.asarray(a) for a in _rope_tables(np.tile(PAST_LEN + np.arange(ls), BB_SAMPLE)))

    w_in_b = w_in[0].astype(BF16)
    w_ret_o_b = w_ret_o[0].astype(BF16)
    w_conv_o_b = w_conv_o[0].astype(BF16)
    w_out_b = w_out[0].astype(BF16)
    w_pg_b = w_pg[0].astype(BF16)
    w_ple_b = w_ple[0].astype(BF16)
    n_route = N_GROUPS + N_EXPERTS
    w_r = jnp.concatenate([w_grp[0], w_exp[0], jnp.zeros((d, LANES - n_route), F32)], axis=1)
    wr_hi = w_r.astype(BF16)
    wr_lo = (w_r - wr_hi.astype(F32)).astype(BF16)
    b_r = jnp.concatenate([b_grp[0], b_exp[0], jnp.zeros((LANES - n_route,), F32)]).reshape(1, LANES)
    row = lambda a: a.reshape(1, -1)
    conv_w0 = conv_w[0]
    nstate = CONV_WIDTH - 1

    rep8 = lambda a: jnp.broadcast_to(a[..., None, :], a.shape[:-1] + (SUBLANES, a.shape[-1]))
    shared_w = (w_in_b, rep8(b_in[0]), row(ret_gn_g[0]), row(ret_gn_b[0]), w_ret_o_b)
    tail_w = (row(conv_ln_g[0]), row(conv_ln_b[0]), w_conv_o_b, w_out_b, row(ln1_g[0]), row(ln1_b[0]),
              wr_hi, wr_lo, b_r)

    nbt = bs // BB_SAMPLE
    xs2 = x_sample.reshape(n_s, d)
    sample_in = ((xs2, cos_s, sin_s, pdec_s, qdec_s, kdec_s, cdec_s, conv_w0, state_ret, state_conv)
                 + shared_w + (row(conv_b[0]),) + tail_w)
    sample_specs = (
        [pl.BlockSpec((ts, d), lambda i: (i, 0))]
        + [_const_spec(a.shape) for a in sample_in[1:8]]
        + [pl.BlockSpec((1, BB_SAMPLE, RET_HEADS, RET_DK, RET_DV), lambda i: (0, i, 0, 0, 0)),
           pl.BlockSpec((1, BB_SAMPLE, nstate, CONV_CH), lambda i: (0, i, 0, 0))]
        + [_const_spec(a.shape) for a in sample_in[10:]]
    )
    tok_spec_s = lambda w: pl.BlockSpec((ts, w), lambda i: (i, 0))
    x1_s, rw_s, ret_s, conv_s = pl.pallas_call(
        _sample_mixer_kernel,
        grid=(nbt,),
        in_specs=sample_specs,
        out_specs=[
            tok_spec_s(d), tok_spec_s(LANES),
            pl.BlockSpec((1, BB_SAMPLE, RET_HEADS, RET_DK, RET_DV), lambda i: (0, i, 0, 0, 0)),
            pl.BlockSpec((1, BB_SAMPLE, nstate, CONV_CH), lambda i: (0, i, 0, 0)),
        ],
        out_shape=[
            jax.ShapeDtypeStruct((n_s, d), F32),
            jax.ShapeDtypeStruct((n_s, LANES), F32),
            jax.ShapeDtypeStruct(state_ret.shape, F32),
            jax.ShapeDtypeStruct(state_conv.shape, F32),
        ],
        scratch_shapes=[
            pltpu.VMEM((ts, RET_V), F32),
            pltpu.VMEM((CONV_CH // LANES, ts, LANES), F32),
            pltpu.VMEM((BB_SAMPLE, XPAD_ROWS, CONV_CH), F32),
            pltpu.VMEM((ls, XPAD_ROWS, CONV_CH), F32),
        ],
        compiler_params=pltpu.CompilerParams(
            dimension_semantics=("arbitrary",), vmem_limit_bytes=VMEM_LIMIT),
        name="sample_mixer",
    )(*sample_in)

    assert lp % TLM == 0 and n_s % TLM == 0 and TLM % TL == 0
    nlt = lp // TLM
    npt = n_p // TLM
    nst = n_s // TLM
    sub = TLM // TL
    prompt_in = ((x_prompt, x1_s, rw_s, cos_p, sin_p, decay, qdec_p, kdec_p, cdec_p)
                 + shared_w + (rep8(conv_w0), row(conv_b[0])) + tail_w)
    head_tile = lambda s: jnp.minimum(s, npt - 1)
    sample_tile = lambda s: jnp.maximum(s - npt, 0)
    sample_spec = lambda w: pl.BlockSpec((TLM, w), lambda s: (sample_tile(s), 0))
    prompt_specs = [
        pl.BlockSpec((1, TLM, d), lambda s: (head_tile(s) // nlt, head_tile(s) % nlt, 0)),
        sample_spec(d), sample_spec(LANES),
        pl.BlockSpec((TLM, RET_DK), lambda s: (head_tile(s) % nlt, 0)),
        pl.BlockSpec((TLM, RET_DK), lambda s: (head_tile(s) % nlt, 0)),
    ] + [_const_spec(a.shape) for a in prompt_in[5:]]
    tok_spec_p = lambda rows, w: pl.BlockSpec((rows, w), lambda s: (s, 0))
    x1_all, rw_all, xs_all, meta, ret_p, conv_p = pl.pallas_call(
        functools.partial(_prompt_mixer_kernel, n_tiles=npt, tiles_per_seq=nlt),
        grid=(npt + nst,),
        in_specs=prompt_specs,
        out_specs=[
            tok_spec_p(TLM, d), tok_spec_p(TLM, LANES), tok_spec_p(sub * CAP, d), tok_spec_p(sub * LANES, LANES),
            pl.BlockSpec((1, 1, RET_HEADS, RET_DK, RET_DV), lambda s: (0, head_tile(s) // nlt, 0, 0, 0)),
            pl.BlockSpec((1, 1, nstate, CONV_CH), lambda s: (0, head_tile(s) // nlt, 0, 0)),
        ],
        out_shape=[
            jax.ShapeDtypeStruct((n_tok, d), F32),
            jax.ShapeDtypeStruct((n_tok, LANES), F32),
            jax.ShapeDtypeStruct((n_blocks * CAP, d), BF16),
            jax.ShapeDtypeStruct((n_blocks * LANES, LANES), I32),
            jax.ShapeDtypeStruct((1, bp, RET_HEADS, RET_DK, RET_DV), F32),
            jax.ShapeDtypeStruct((1, bp, nstate, CONV_CH), F32),
        ],
        scratch_shapes=[
            pltpu.VMEM((TLM + CONV_PAD, CONV_CH), F32),
            pltpu.VMEM((SUBLANES - 1, TL + CONV_PAD - SUBLANES, CONV_CH), F32),
            pltpu.VMEM((TLM, 2 * RET_QK + RET_V), F32),
            pltpu.VMEM((TLM, d), BF16),
            pltpu.VMEM((TLM, RET_V), F32),
            pltpu.VMEM((TLM, CONV_CH), F32),
            pltpu.VMEM((TLM, RET_V + 2 * D_MODEL), F32),
        ],
        compiler_params=pltpu.CompilerParams(
            dimension_semantics=("arbitrary",), vmem_limit_bytes=VMEM_LIMIT),
        name="prompt_mixer",
    )(*prompt_in)

    max_chunks = n_blocks * (TOP_K * TL // CHUNK + N_EXPERTS - 1)
    n_ffn_tiles = (max_chunks + N_EXPERTS * (TILE_CHUNKS - 1)) // TILE_CHUNKS
    tile_e, n_valid_tiles, chunk_ids = _chunk_plan(meta, n_blocks, n_ffn_tiles)

    ys_all = pl.pallas_call(
        _ffn_kernel,
        grid_spec=pltpu.PrefetchScalarGridSpec(
            num_scalar_prefetch=3,
            grid=(n_ffn_tiles,),
            in_specs=[
                pl.BlockSpec(memory_space=pl.ANY),
                pl.BlockSpec((1, d, 2 * EXP_FF), lambda i, te, nr, ch: (te[i], 0, 0)),
                pl.BlockSpec((1, EXP_FF, d), lambda i, te, nr, ch: (te[i], 0, 0)),
            ],
            out_specs=pl.BlockSpec(memory_space=pl.ANY),
            scratch_shapes=[
                pltpu.VMEM((2, TM_FFN, d), BF16),
                pltpu.VMEM((2, TM_FFN, d), BF16),
                pltpu.VMEM((d, 2 * EXP_FF), BF16),
                pltpu.VMEM((EXP_FF, d), BF16),
                pltpu.SemaphoreType.DMA((2,)),
                pltpu.SemaphoreType.DMA((2,)),
            ],
        ),
        out_shape=jax.ShapeDtypeStruct(xs_all.shape, BF16),
        input_output_aliases={3: 0},
        compiler_params=pltpu.CompilerParams(
            dimension_semantics=("arbitrary",), vmem_limit_bytes=VMEM_LIMIT),
        name="expert_ffn",
    )(tile_e, n_valid_tiles, chunk_ids, xs_all, w_gu[0], w_dn[0])

    assert n_p % TLF == 0 and n_s % TLF == 0
    npt = n_p // TLF
    fsub = TLF // TL
    pp2 = p_prompt.reshape(n_p, PLE_DIM)
    ps2 = p_sample.reshape(n_s, PLE_DIM)
    tok_f = lambda rows, w: pl.BlockSpec((rows, w), lambda i: (i, 0))
    y_p, y_s = pl.pallas_call(
        functools.partial(_final_kernel, n_prompt_tiles=npt),
        grid=(n_tok // TLF,),
        in_specs=[
            tok_f(fsub * CAP, d), tok_f(TLF, d), tok_f(TLF, LANES),
            pl.BlockSpec((TLF, PLE_DIM), lambda i: (jnp.minimum(i, npt - 1), 0)),
            pl.BlockSpec((TLF, PLE_DIM), lambda i: (jnp.maximum(i - npt, 0), 0)),
            _const_spec((1, d)), _const_spec((1, d)), _const_spec((d, d)), _const_spec((1, d)),
            _const_spec((PLE_DIM, d)),
        ],
        out_specs=[
            pl.BlockSpec((TLF, d), lambda i: (jnp.minimum(i, npt - 1), 0)),
            pl.BlockSpec((TLF, d), lambda i: (jnp.maximum(i - npt, 0), 0)),
        ],
        out_shape=[jax.ShapeDtypeStruct((n_p, d), F32), jax.ShapeDtypeStruct((n_s, d), F32)],
        compiler_params=pltpu.CompilerParams(
            dimension_semantics=("arbitrary",), vmem_limit_bytes=VMEM_LIMIT),
        name="moe_combine_final",
    )(ys_all, x1_all, rw_all, pp2, ps2,
      row(ln2_g[0]), row(ln2_b[0]), w_pg_b, row(b_pg[0]), w_ple_b)

    return (y_p.reshape(bp, lp, d), y_s.reshape(bs, ls, d), ret_p, conv_p, ret_s, conv_s)
```

```python
import functools

import jax
import jax.numpy as jnp
import numpy as np
from jax import lax
from jax.experimental import pallas as pl
from jax.experimental.pallas import tpu as pltpu

F32 = jnp.float32
BF16 = jnp.bfloat16
I32 = jnp.int32

D_MODEL = 1024
PAST_LEN = 16384
RET_HEADS = 4
RET_DK = 128
RET_DV = 128
RET_QK = RET_HEADS * RET_DK
RET_V = RET_HEADS * RET_DV
RET_CHUNK = 128
ROPE_BASE = 10000.0
CONV_CH = 512
CONV_WIDTH = 31
N_GROUPS = 4
EXP_PER_GROUP = 4
N_EXPERTS = N_GROUPS * EXP_PER_GROUP
TOP_K = 2
EXP_FF = 512
PLE_DIM = 256
DEPTH = 1
ALPHA = (2 * DEPTH) ** 0.25
LN_EPS = 1e-5
IN_WIDTHS = (RET_QK, RET_QK, RET_V, RET_V, CONV_CH, CONV_CH, D_MODEL, D_MODEL)
IN_OFFS = tuple(int(s) for s in np.cumsum((0,) + IN_WIDTHS))

LANES = 128
SUBLANES = 8
VMEM_LIMIT = 56 * 1024 * 1024

TL = 256
TLM = 512
TLF = 512
BB_SAMPLE = 16
CHUNK = 2 * SUBLANES
TILE_CHUNKS = 32
BLOCK_USED = -(-(TOP_K * TL + N_EXPERTS * (CHUNK - 1)) // LANES) * LANES // CHUNK
BLOCK_SPARE = LANES // CHUNK
BLOCK_CHUNKS = BLOCK_USED + BLOCK_SPARE
USED_ROWS = BLOCK_USED * CHUNK
CAP = BLOCK_CHUNKS * CHUNK
TM_FFN = TILE_CHUNKS * CHUNK
FFN_COLS = 256
CONV_PAD = 32
XPAD_NEW = 32
XPAD_ROWS = 40


def _ln(x, g, b):
    mu = jnp.mean(x, axis=-1, keepdims=True)
    d = x - mu
    var = jnp.mean(d * d, axis=-1, keepdims=True)
    return d * lax.rsqrt(var + LN_EPS) * g + b


def _sigmoid(x):
    return 1.0 / (1.0 + jnp.exp(-x))


def _rep(v8, rows):
    return v8 if rows == SUBLANES else jnp.concatenate([v8] * (rows // SUBLANES), axis=0)


def _silu(x):
    return x * _sigmoid(x)


def _bdot(a, b):
    return jnp.dot(a.astype(BF16), b, preferred_element_type=F32)


def _rot(t, cosf, sinf):
    return t * cosf + pltpu.roll(t, RET_DK // 2, axis=1) * sinf


def _lane_tile(cols, rows):
    lane = lax.broadcasted_iota(I32, (rows, LANES), 1)
    out = jnp.zeros((rows, LANES), F32)
    for i, col in enumerate(cols):
        out = jnp.where(lane == i, col, out)
    return out


def _route(logits):
    lane = lax.broadcasted_iota(I32, logits.shape, 1)
    lanef = lane.astype(F32)
    ninf = jnp.float32(-jnp.inf)
    big = jnp.float32(LANES)
    gmask = lane < N_GROUPS
    gl = jnp.where(gmask, logits, ninf)
    gmax = jnp.max(gl, axis=1, keepdims=True)
    gidx = jnp.min(jnp.where(gmask & (gl == gmax), lanef, big), axis=1, keepdims=True)
    sumexp = jnp.sum(jnp.where(gmask, jnp.exp(gl - gmax), 0.0), axis=1, keepdims=True)
    gw = 1.0 / sumexp
    lo = N_GROUPS + EXP_PER_GROUP * gidx
    emask = (lanef >= lo) & (lanef < lo + EXP_PER_GROUP)
    el = jnp.where(emask, logits, ninf)
    m1 = jnp.max(el, axis=1, keepdims=True)
    i1 = jnp.min(jnp.where(emask & (el == m1), lanef, big), axis=1, keepdims=True)
    emask2 = emask & (lanef != i1)
    el2 = jnp.where(emask2, logits, ninf)
    m2 = jnp.max(el2, axis=1, keepdims=True)
    i2 = jnp.min(jnp.where(emask2 & (el2 == m2), lanef, big), axis=1, keepdims=True)
    t = jnp.exp(m2 - m1)
    den = 1.0 + t
    return (1.0 / den) * gw, (t / den) * gw, i1 - N_GROUPS, i2 - N_GROUPS


def _post_mix_pieces(src, w, sink):
    (w_ret_o, cln_g, cln_b, w_conv_o, w_out, ln1_g, ln1_b, wr_hi, wr_lo, b_r) = w
    st = {}

    def branch_a():
        st["a"] = _bdot(_silu(src["g"]()) * src["ret"](), w_ret_o[...])

    def branch_b():
        st["b"] = _bdot(_silu(_ln(src["cout"](), cln_g[...], cln_b[...])), w_conv_o[...])

    def merge():
        mix = _sigmoid(src["gt_a"]()) * st["a"] + _sigmoid(src["gt_b"]()) * st["b"]
        h = ALPHA * src["x"]() + _bdot(mix, w_out[...])
        st["x1"] = _ln(h, ln1_g[...], ln1_b[...])

    def router():
        x1 = st["x1"]
        x1_hi = x1.astype(BF16)
        x1_lo = (x1 - x1_hi.astype(F32)).astype(BF16)
        st["logits"] = (jnp.dot(x1_hi, wr_hi[...], preferred_element_type=F32)
                        + (jnp.dot(x1_lo, wr_hi[...], preferred_element_type=F32)
                           + jnp.dot(x1_hi, wr_lo[...], preferred_element_type=F32))
                        + b_r[...])

    def route():
        st["route"] = _route(st["logits"])

    def finish():
        sink(st["x1"], *st["route"])

    return [branch_a, branch_b, merge, router, route, finish]


def _sort_tile(x1, w1, w2, e1, e2, x1_ref, rw_ref, xs_ref, meta_ref):
    t = x1.shape[0]
    ids_t = _lane_tile((e1, e2), t).T
    e1r, e2r = ids_t[0:1, :], ids_t[1:2, :]
    sub = lax.broadcasted_iota(I32, (LANES, t), 0).astype(F32)
    a1 = (sub == e1r).astype(F32)
    a2 = (sub == e2r).astype(F32)
    ri = lax.broadcasted_iota(I32, (t, t), 0)
    ci = lax.broadcasted_iota(I32, (t, t), 1)
    earlier = (ri < ci).astype(BF16)
    r1 = jnp.dot(a1.astype(BF16), earlier, preferred_element_type=F32)
    r2 = jnp.dot(a2.astype(BF16), earlier, preferred_element_type=F32)
    cnt1 = jnp.sum(a1, axis=1, keepdims=True)
    cnt = cnt1 + jnp.sum(a2, axis=1, keepdims=True)
    nch = jnp.floor((cnt + (CHUNK - 1.0)) * (1.0 / CHUNK))
    ui = lax.broadcasted_iota(I32, (LANES, LANES), 0)
    uj = lax.broadcasted_iota(I32, (LANES, LANES), 1)
    before = (uj < ui).astype(BF16)
    off = jnp.dot(before, jnp.broadcast_to(nch, (LANES, LANES)).astype(BF16),
                  preferred_element_type=F32)[:, 0:1]
    base = off * CHUNK
    pos1r = jnp.sum(a1 * (base + r1), axis=0, keepdims=True)
    pos2r = jnp.sum(a2 * (base + cnt1 + r2), axis=0, keepdims=True)
    slot = lax.broadcasted_iota(I32, (USED_ROWS, t), 0).astype(F32)
    onehot = ((slot == pos1r) | (slot == pos2r)).astype(BF16)
    xs = jnp.dot(onehot, x1.astype(BF16), preferred_element_type=F32)
    pos_cols = jnp.where(sub == 2.0, pos1r, jnp.where(sub == 3.0, pos2r, 0.0)).T
    lane = lax.broadcasted_iota(I32, (t, LANES), 1)
    x1_ref[...] = x1
    rw_ref[...] = jnp.where(lane == 0, w1, jnp.where(lane == 1, w2, pos_cols))
    xs_ref[0:USED_ROWS, :] = xs.astype(BF16)
    xs_ref[USED_ROWS:CAP, :] = jnp.zeros((CAP - USED_ROWS, x1.shape[1]), BF16)
    mlane = lax.broadcasted_iota(I32, (LANES, LANES), 1)
    meta = jnp.where(mlane == 0, cnt, jnp.where(mlane == 1, off, 0.0))
    meta_ref[...] = meta.astype(I32)


def _sort_tiles(x1, w1, w2, e1, e2, x1_ref, rw_ref, xs_ref, meta_ref):
    for i in range(x1.shape[0] // TL):
        rows = slice(i * TL, (i + 1) * TL)
        _sort_tile(x1[rows], w1[rows], w2[rows], e1[rows], e2[rows],
                   x1_ref.at[pl.ds(i * TL, TL)], rw_ref.at[pl.ds(i * TL, TL)],
                   xs_ref.at[pl.ds(i * CAP, CAP)], meta_ref.at[pl.ds(i * LANES, LANES)])


GATE_COLS = {3: 0, 6: RET_V, 7: RET_V + D_MODEL}
QKV_COLS = {0: 0, 1: RET_QK, 2: 2 * RET_QK}


def _prompt_mixer_kernel(x_ref, x1s_ref, rws_ref, cos_ref, sin_ref, dec_ref, qdec_ref, kdec_ref, cdec_ref,
                         w_in, b_in, gn_g, gn_b, w_ret_o, conv_w, conv_b, cln_g, cln_b,
                         w_conv_o, w_out, ln1_g, ln1_b, wr_hi, wr_lo, b_r,
                         x1_ref, rw_ref, xs_ref, meta_ref, sret_ref, sconv_ref,
                         ubuf, ushift, qkv_scr, xb_scr, ret_scr, cout_scr, gate_scr,
                         *, n_tiles, tiles_per_seq):
    s = pl.program_id(0)
    li = lax.rem(s, tiles_per_seq)
    outs = (x1_ref, rw_ref, xs_ref, meta_ref)
    slot = dict(ret=ret_scr, cout=cout_scr, gates=gate_scr)
    tail_w = (w_ret_o, cln_g, cln_b, w_conv_o, w_out, ln1_g, ln1_b, wr_hi, wr_lo, b_r)

    @pl.when((s < n_tiles) & (li == 0))
    def _new_sequence():
        sret_ref[...] = jnp.zeros(sret_ref.shape, F32)
        ubuf[0:CONV_PAD, :] = jnp.zeros((CONV_PAD, CONV_CH), F32)

    @pl.when(s < n_tiles)
    def _mix():
        gcols = lambda kk: slice(GATE_COLS[kk], GATE_COLS[kk] + IN_WIDTHS[kk])
        src = dict(x=lambda: x_ref[0], ret=lambda: ret_scr[...], cout=lambda: cout_scr[...],
                   g=lambda: gate_scr[:, gcols(3)], gt_a=lambda: gate_scr[:, gcols(6)],
                   gt_b=lambda: gate_scr[:, gcols(7)])
        head = _prompt_head_pieces(x_ref, cos_ref, sin_ref, dec_ref, qdec_ref, kdec_ref, cdec_ref,
                                   w_in, b_in, gn_g, gn_b, conv_w, conv_b, sret_ref,
                                   ubuf, ushift, qkv_scr, xb_scr, slot)
        tail = _post_mix_pieces(src, tail_w, lambda *r: _sort_tiles(*r, *outs))
        for piece in head + tail:
            piece()

    @pl.when(s >= n_tiles)
    def _append():
        rws = rws_ref[...]
        _sort_tiles(x1s_ref[...], rws[:, 0:1], rws[:, 1:2], rws[:, 2:3], rws[:, 3:4], *outs)

    @pl.when((s < n_tiles) & (li == tiles_per_seq - 1))
    def _conv_state():
        sconv_ref[0, 0] = ubuf[CONV_PAD - (CONV_WIDTH - 1):CONV_PAD, :]


def _prompt_head_pieces(x_ref, cos_ref, sin_ref, dec_ref, qdec_ref, kdec_ref, cdec_ref,
                        w_in, b_in, gn_g, gn_b, conv_w, conv_b, sret_ref,
                        ubuf, ushift, qkv_scr, xb_scr, slot):
    tl = x_ref.shape[1]
    st = {}

    def slab_dot(c0, c1):
        return jnp.dot(xb_scr[...], w_in[:, c0:c1], preferred_element_type=F32) + _rep(b_in[:, c0:c1], tl)

    def glu():
        xb_scr[...] = x_ref[0].astype(BF16)
        u = slab_dot(IN_OFFS[4], IN_OFFS[5]) * _sigmoid(slab_dot(IN_OFFS[5], IN_OFFS[6]))
        ubuf[CONV_PAD:CONV_PAD + tl, :] = u

    nsh = ushift.shape[1]
    span = nsh - (CONV_PAD - SUBLANES)

    def shift_copy(h, s):
        ushift[s - 1] = ubuf[h * span + s:h * span + s + nsh, :]

    slab = 256
    slabs = [(kk, c0) for kk in (0, 1, 2, 3, 6, 7) for c0 in range(IN_OFFS[kk], IN_OFFS[kk + 1], slab)]
    rb = 32
    nrb = tl // rb

    def proj_slab(kk, c0):
        val = slab_dot(c0, c0 + slab)
        if kk in QKV_COLS:
            dst = QKV_COLS[kk] + c0 - IN_OFFS[kk]
            qkv_scr[:, dst:dst + slab] = val
        else:
            dst = GATE_COLS[kk] + c0 - IN_OFFS[kk]
            slot["gates"][:, dst:dst + slab] = val

    def conv_block(r):
        h, rl = divmod(r * rb, span)
        acc = jnp.zeros((rb, CONV_CH), F32) + conv_b[...]
        for j in range(CONV_WIDTH):
            off = j + (CONV_PAD - (CONV_WIDTH - 1))
            s = off % SUBLANES
            base = rl + off - s
            win = (ubuf[h * span + base:h * span + base + rb, :] if s == 0
                   else ushift[s - 1, base:base + rb, :])
            acc = acc + _rep(conv_w[j], rb) * win
        slot["cout"][r * rb:(r + 1) * rb, :] = acc
        if r == nrb - 1:
            ubuf[0:CONV_PAD, :] = ubuf[tl:tl + CONV_PAD, :]

    scale = RET_DK ** -0.5

    def retention(c, h):
        rows = slice(c * RET_CHUNK, (c + 1) * RET_CHUNK)
        cols = slice(h * RET_DK, (h + 1) * RET_DK)
        hcol = lambda kk: slice(QKV_COLS[kk] + h * RET_DK, QKV_COLS[kk] + (h + 1) * RET_DK)
        cosf = cos_ref[rows, :]
        sinf = sin_ref[rows, :]
        qh = _rot(qkv_scr[rows, hcol(0)], cosf, sinf)
        kh = _rot(qkv_scr[rows, hcol(1)], cosf, sinf) * scale
        qb = qh.astype(BF16)
        kb = kh.astype(BF16)
        vb = qkv_scr[rows, hcol(2)].astype(BF16)
        s_old = sret_ref[0, 0, h]
        scores = lax.dot_general(qb, kb, (((1,), (1,)), ((), ())),
                                 preferred_element_type=F32) * dec_ref[h]
        inner = jnp.dot(scores.astype(BF16), vb, preferred_element_type=F32)
        cross = jnp.dot(qb, s_old.astype(BF16), preferred_element_type=F32) * qdec_ref[h]
        kd = (kh * kdec_ref[h]).astype(BF16)
        s_new = cdec_ref[h] * s_old + lax.dot_general(
            kd, vb, (((0,), (0,)), ((), ())), preferred_element_type=F32)
        sret_ref[0, 0, h] = s_new
        slot["ret"][rows, cols] = _ln(inner + cross, gn_g[:, cols], gn_b[:, cols])

    vector_pieces = []
    for r in range(nrb):
        if (r * rb) % span == 0:
            vector_pieces += [lambda h=(r * rb) // span, s=s: shift_copy(h, s) for s in range(1, SUBLANES)]
        vector_pieces.append(lambda r=r: conv_block(r))
    pieces = [glu]
    for i, piece in enumerate(vector_pieces):
        pieces.append(piece)
        for kk, c0 in slabs[i * len(slabs) // len(vector_pieces):(i + 1) * len(slabs) // len(vector_pieces)]:
            pieces.append(lambda kk=kk, c0=c0: proj_slab(kk, c0))
    pieces += [lambda c=c, h=h: retention(c, h) for c in range(tl // RET_CHUNK) for h in range(RET_HEADS)]
    return pieces


def _sample_mixer_kernel(x_ref, cos_ref, sin_ref, pdec_ref, qdec_ref, kdec_ref, cdec_ref, conv_w,
                         sret_in, sconv_in,
                         w_in, b_in, gn_g, gn_b, w_ret_o, conv_b, cln_g, cln_b,
                         w_conv_o, w_out, ln1_g, ln1_b, wr_hi, wr_lo, b_r,
                         x1_ref, rw_ref, sret_ref, sconv_ref,
                         ret_scr, cout_scr, xpad, wsh):
    t = x_ref.shape[0]
    ls = t // BB_SAMPLE
    nstate = CONV_WIDTH - 1
    x = x_ref[...]
    xb = x.astype(BF16)

    @pl.when(pl.program_id(0) == 0)
    def _tap_tables():
        wsh[...] = jnp.zeros(wsh.shape, F32)
        for p in range(ls):
            wsh[p, p:nstate, :] = conv_w[0:nstate - p, :]
            wsh[p, XPAD_NEW:XPAD_NEW + p + 1, :] = conv_w[nstate - p:CONV_WIDTH, :]

    def proj(k):
        c0, c1 = IN_OFFS[k], IN_OFFS[k + 1]
        return jnp.dot(xb, w_in[:, c0:c1], preferred_element_type=F32) + _rep(b_in[:, c0:c1], t)

    q = proj(0)
    k = proj(1)
    v = proj(2)
    scale = RET_DK ** -0.5
    cosf = cos_ref[...]
    sinf = sin_ref[...]
    row = lax.broadcasted_iota(I32, (t, RET_DK), 0)
    pos = row % ls
    row8 = lax.broadcasted_iota(I32, (SUBLANES, RET_DK), 0)
    per_tile = SUBLANES // ls
    for h in range(RET_HEADS):
        cols = slice(h * RET_DK, (h + 1) * RET_DK)
        qh = _rot(q[:, cols], cosf, sinf)
        kh = _rot(k[:, cols], cosf, sinf) * scale
        vh = v[:, cols]
        inner = jnp.zeros((t, RET_DV), F32)
        for s in range(ls):
            ks = kh if s == 0 else pltpu.roll(kh, s, axis=0)
            vs = vh if s == 0 else pltpu.roll(vh, s, axis=0)
            dotp = jnp.sum(qh * ks, axis=1, keepdims=True) * pdec_ref[h, s]
            inner = inner + jnp.where(pos >= s, dotp, 0.0) * vs
        kd = kh * kdec_ref[h]
        for tile in range(t // SUBLANES):
            rows = slice(tile * SUBLANES, (tile + 1) * SUBLANES)
            q8 = qh[rows, :]
            kd8 = kd[rows, :]
            v8 = vh[rows, :]
            seqs = [tile * per_tile + sub for sub in range(per_tile)]
            mine = [(row8 >= sub * ls) & (row8 < (sub + 1) * ls) for sub in range(per_tile)]
            s_old = [sret_in[0, b, h] for b in seqs]
            c_all = jnp.dot(q8, jnp.concatenate(s_old, axis=1), preferred_element_type=F32)
            upd = lax.dot_general(jnp.concatenate([jnp.where(m, kd8, 0.0) for m in mine], axis=1), v8,
                                  (((0,), (0,)), ((), ())), preferred_element_type=F32)
            cross8 = jnp.zeros((SUBLANES, RET_DV), F32)
            for sub, b in enumerate(seqs):
                cross8 = jnp.where(mine[sub], c_all[:, sub * RET_DV:(sub + 1) * RET_DV], cross8)
                sret_ref[0, b, h] = cdec_ref[h] * s_old[sub] + upd[sub * RET_DK:(sub + 1) * RET_DK, :]
            ret_scr[rows, cols] = inner[rows, :] + cross8 * qdec_ref[h, rows, :]
        ret_scr[:, cols] = _ln(ret_scr[:, cols], gn_g[:, cols], gn_b[:, cols])

    u = proj(4) * _sigmoid(proj(5))
    xpad[...] = jnp.zeros(xpad.shape, F32)
    xpad[:, 0:nstate, :] = sconv_in[0]
    for b in range(BB_SAMPLE):
        xpad[b, XPAD_NEW:XPAD_NEW + ls, :] = u[b * ls:(b + 1) * ls, :]
    for p in range(ls):
        res = jnp.sum(xpad[...] * wsh[p][None], axis=1) + conv_b[...]
        for sl in range(CONV_CH // LANES):
            cout_scr[sl, pl.ds(p, BB_SAMPLE, stride=ls), :] = res[:, sl * LANES:(sl + 1) * LANES]
    sconv_ref[0, :, 0:nstate - ls, :] = xpad[:, ls:nstate, :]
    sconv_ref[0, :, nstate - ls:nstate, :] = xpad[:, XPAD_NEW:XPAD_NEW + ls, :]
    c_out = jnp.concatenate([cout_scr[sl] for sl in range(CONV_CH // LANES)], axis=1)

    def sink(x1, w1, w2, e1, e2):
        x1_ref[...] = x1
        rw_ref[...] = _lane_tile((w1, w2, e1, e2), t)

    src = dict(x=lambda: x, ret=lambda: ret_scr[...], cout=lambda: c_out,
               g=lambda: proj(3), gt_a=lambda: proj(6), gt_b=lambda: proj(7))
    for piece in _post_mix_pieces(
            src, (w_ret_o, cln_g, cln_b, w_conv_o, w_out, ln1_g, ln1_b, wr_hi, wr_lo, b_r), sink):
        piece()


def _ffn_kernel(te_ref, nvalid_ref, chunk_ref, xs_hbm, w_gu, w_dn, ys_hbm,
                xbuf, obuf, wgu_b, wdn_b, sem_in, sem_out):
    del xs_hbm
    i = pl.program_id(0)
    n = pl.num_programs(0)
    slot = i % 2
    nvalid = nvalid_ref[0]

    def chunk_rows(tile, c):
        return pl.ds(pl.multiple_of(chunk_ref[tile * TILE_CHUNKS + c] * CHUNK, CHUNK), CHUNK)

    def start_in(tile, s):
        for c in range(TILE_CHUNKS):
            pltpu.make_async_copy(ys_hbm.at[chunk_rows(tile, c)],
                                  xbuf.at[s, pl.ds(c * CHUNK, CHUNK)], sem_in.at[s]).start()

    def start_out(tile, s):
        for c in range(TILE_CHUNKS):
            pltpu.make_async_copy(obuf.at[s, pl.ds(c * CHUNK, CHUNK)],
                                  ys_hbm.at[chunk_rows(tile, c)], sem_out.at[s]).start()

    def wait_in(s):
        pltpu.make_async_copy(ys_hbm.at[pl.ds(0, TM_FFN)], xbuf.at[s], sem_in.at[s]).wait()

    def wait_out(s):
        pltpu.make_async_copy(obuf.at[s], ys_hbm.at[pl.ds(0, TM_FFN)], sem_out.at[s]).wait()

    @pl.when((i == 0) & (nvalid > 0))
    def _first():
        start_in(0, 0)

    @pl.when((i >= 2) & (i - 2 < nvalid))
    def _retire():
        wait_out(slot)

    @pl.when(i < nvalid)
    def _tile():
        wait_in(slot)
        prev = te_ref[jnp.maximum(i - 1, 0)]

        @pl.when((i == 0) | (te_ref[i] != prev))
        def _new_expert():
            wgu_b[...] = w_gu[0].astype(BF16)
            wdn_b[...] = w_dn[0].astype(BF16)

        x = xbuf[slot]
        y = jnp.zeros((TM_FFN, w_dn.shape[2]), F32)
        for c0 in range(0, EXP_FF, FFN_COLS):
            hg = jnp.dot(x, wgu_b[:, c0:c0 + FFN_COLS], preferred_element_type=F32)
            hu = jnp.dot(x, wgu_b[:, EXP_FF + c0:EXP_FF + c0 + FFN_COLS], preferred_element_type=F32)
            y = y + _bdot(_silu(hg) * hu, wdn_b[c0:c0 + FFN_COLS, :])
            if c0 == 0:
                start_in(jnp.where(i + 1 < nvalid, i + 1, 0), 1 - slot)
        obuf[slot] = y.astype(BF16)
        start_out(i, slot)

    @pl.when(i == n - 1)
    def _drain():
        @pl.when((i >= 1) & (i - 1 < nvalid))
        def _():
            wait_out(1 - slot)

        @pl.when(i < nvalid)
        def _():
            wait_out(slot)

        @pl.when(nvalid > 0)
        def _():
            wait_in(nvalid % 2)


def _final_kernel(ys_ref, x1_ref, rw_ref, pp_ref, ps_ref, ln2_g, ln2_b, w_pg, b_pg, w_ple,
                  yp_ref, ys_out_ref, *, n_prompt_tiles):
    i = pl.program_id(0)
    x1 = x1_ref[...]
    slot = lax.broadcasted_iota(I32, (TL, USED_ROWS), 1).astype(F32)
    parts = []
    for b in range(x1.shape[0] // TL):
        rw = rw_ref[b * TL:(b + 1) * TL, :]
        w1, w2, pos1, pos2 = rw[:, 0:1], rw[:, 1:2], rw[:, 2:3], rw[:, 3:4]
        ys = ys_ref[b * CAP:b * CAP + USED_ROWS, :]
        comb = jnp.where(slot == pos1, w1, jnp.where(slot == pos2, w2, 0.0)).astype(BF16)
        parts.append(jnp.dot(comb, ys, preferred_element_type=F32))
    moe = parts[0] if len(parts) == 1 else jnp.concatenate(parts, axis=0)
    x2 = _ln(ALPHA * x1 + moe, ln2_g[...], ln2_b[...])
    gate = _sigmoid(_bdot(x2, w_pg[...]) + b_pg[...])
    p = jnp.where(i < n_prompt_tiles, pp_ref[...], ps_ref[...])
    y = x2 + gate * _bdot(p, w_ple[...])

    @pl.when(i < n_prompt_tiles)
    def _prompt():
        yp_ref[...] = y

    @pl.when(i >= n_prompt_tiles)
    def _sample():
        ys_out_ref[...] = y


def _rope_tables(pos):
    half = RET_DK // 2
    inv_freq = ROPE_BASE ** (-np.arange(half, dtype=np.float64) / half)
    ang = np.asarray(pos, np.float64)[:, None] * inv_freq[None, :]
    cos = np.cos(ang)
    sin = np.sin(ang)
    return (np.concatenate([cos, cos], axis=-1).astype(np.float32),
            np.concatenate([-sin, sin], axis=-1).astype(np.float32))


def _log_gamma():
    return np.log(1.0 - 2.0 ** (-5.0 - np.arange(RET_HEADS, dtype=np.float64)))


def _const_spec(shape):
    nd = len(shape)
    return pl.BlockSpec(shape, lambda *_: (0,) * nd, pipeline_mode=pl.Buffered(1))


def _chunk_plan(meta, n_blocks, n_ffn_tiles):
    assert n_blocks * BLOCK_SPARE >= N_EXPERTS * (TILE_CHUNKS - 1)
    m = meta.reshape(n_blocks, LANES, LANES)
    cnt = m[:, :N_EXPERTS, 0]
    off = m[:, :N_EXPERTS, 1]
    nch = (cnt + (CHUNK - 1)) // CHUNK
    cum = jnp.cumsum(nch, axis=0)
    total = cum[-1:]
    tiles_e = (total + TILE_CHUNKS - 1) // TILE_CHUNKS
    tile_end = jnp.cumsum(tiles_e, axis=1)
    tile_start = tile_end - tiles_e
    tid = jnp.arange(n_ffn_tiles, dtype=I32)[:, None]
    owner = (tid >= tile_start) & (tid < tile_end)
    pick_e = lambda v: jnp.sum(jnp.where(owner, v, 0), axis=1, keepdims=True)
    te = pick_e(jnp.arange(N_EXPERTS, dtype=I32)[None, :])
    k = (tid - pick_e(tile_start)) * TILE_CHUNKS + jnp.arange(TILE_CHUNKS, dtype=I32)[None, :]
    total_t = pick_e(total)
    real = k < total_t
    by_tile = lambda v: jnp.sum(jnp.where(owner[:, None, :], v[None, :, :], 0), axis=2)
    cum_t = by_tile(cum)
    blk = jnp.minimum(jnp.sum((cum_t[:, None, :] <= k[:, :, None]).astype(I32), axis=2), n_blocks - 1)
    at_blk = blk[:, :, None] == jnp.arange(n_blocks, dtype=I32)[None, None, :]
    pick_b = lambda v: jnp.sum(jnp.where(at_blk, v[:, None, :], 0), axis=2)
    excl = pick_b(cum_t - by_tile(nch))
    off_t = pick_b(by_tile(off))
    spare = te * (TILE_CHUNKS - 1) + jnp.maximum(k - total_t, 0) % TILE_CHUNKS
    spare_chunk = (spare // BLOCK_SPARE) * BLOCK_CHUNKS + BLOCK_USED + spare % BLOCK_SPARE
    chunk = jnp.where(real, blk * BLOCK_CHUNKS + off_t + (k - excl), spare_chunk)
    n_valid = jnp.sum(tiles_e, axis=1)
    te = jnp.where(tid < n_valid, te, N_EXPERTS - 1)
    return te.reshape(-1).astype(I32), n_valid.astype(I32), chunk.reshape(-1).astype(I32)


def kernel(x_prompt, x_sample, state_ret, state_conv, p_prompt, p_sample, w_in, b_in, ret_gn_g, ret_gn_b,
           w_ret_o, conv_w, conv_b, conv_ln_g, conv_ln_b, w_conv_o, w_out, ln1_g, ln1_b, w_grp, b_grp,
           w_exp, b_exp, w_gu, w_dn, ln2_g, ln2_b, w_pg, b_pg, w_ple):
    assert DEPTH == 1 and w_in.shape[0] == 1
    bp, lp, d = x_prompt.shape
    bs, ls, _ = x_sample.shape
    n_p, n_s = bp * lp, bs * ls
    n_tok = n_p + n_s
    assert lp % TL == 0 and n_s % TL == 0 and bs % BB_SAMPLE == 0 and SUBLANES % ls == 0
    n_blocks = n_tok // TL

    f32c = lambda a, shape: jnp.asarray(np.broadcast_to(a, shape).astype(np.float32))
    lg = _log_gamma()
    c = RET_CHUNK
    idx = np.arange(c, dtype=np.float64)
    rel = idx[:, None] - idx[None, :]
    causal = rel >= 0
    decay = np.where(causal[None], np.exp(np.where(causal, rel, 0.0)[None] * lg[:, None, None]), 0.0)
    decay = f32c(decay, decay.shape)
    q_decay = np.exp((idx[:, None] + 1.0) * lg[None, :])
    k_decay = np.exp((c - 1.0 - idx[:, None]) * lg[None, :])
    chunk_decay = np.exp(c * lg)
    qdec_p = f32c(q_decay.T[:, :, None], (RET_HEADS, c, RET_DK))
    kdec_p = f32c(k_decay.T[:, :, None], (RET_HEADS, c, RET_DK))
    cdec_p = f32c(chunk_decay[:, None, None], (RET_HEADS, 1, RET_DV))
    cos_p, sin_p = (jnp.asarray(a) for a in _rope_tables(np.arange(lp)))

    ts = BB_SAMPLE * ls
    idx_s = np.arange(ls, dtype=np.float64)
    pdec_s = np.exp(idx_s[None, :] * lg[:, None])
    pdec_s = f32c(pdec_s[:, :, None, None], (RET_HEADS, ls, 1, RET_DK))
    qd_s = np.exp((idx_s[:, None] + 1.0) * lg[None, :])
    kd_s = np.exp((ls - 1.0 - idx_s[:, None]) * lg[None, :])
    qdec_s = f32c(np.tile(qd_s.T, (1, BB_SAMPLE))[:, :, None], (RET_HEADS, ts, RET_DK))
    kdec_s = f32c(np.tile(kd_s.T, (1, BB_SAMPLE))[:, :, None], (RET_HEADS, ts, RET_DK))
    cdec_s = f32c(np.exp(ls * lg)[:, None, None], (RET_HEADS, 1, RET_DV))
    cos_s, sin_s = (jnp.asarray(a) for a in _rope_tables(np.tile(PAST_LEN + np.arange(ls), BB_SAMPLE)))

    w_in_b = w_in[0].astype(BF16)
    w_ret_o_b = w_ret_o[0].astype(BF16)
    w_conv_o_b = w_conv_o[0].astype(BF16)
    w_out_b = w_out[0].astype(BF16)
    w_pg_b = w_pg[0].astype(BF16)
    w_ple_b = w_ple[0].astype(BF16)
    n_route = N_GROUPS + N_EXPERTS
    w_r = jnp.concatenate([w_grp[0], w_exp[0], jnp.zeros((d, LANES - n_route), F32)], axis=1)
    wr_hi = w_r.astype(BF16)
    wr_lo = (w_r - wr_hi.astype(F32)).astype(BF16)
    b_r = jnp.concatenate([b_grp[0], b_exp[0], jnp.zeros((LANES - n_route,), F32)]).reshape(1, LANES)
    row = lambda a: a.reshape(1, -1)
    conv_w0 = conv_w[0]
    nstate = CONV_WIDTH - 1

    rep8 = lambda a: jnp.broadcast_to(a[..., None, :], a.shape[:-1] + (SUBLANES, a.shape[-1]))
    shared_w = (w_in_b, rep8(b_in[0]), row(ret_gn_g[0]), row(ret_gn_b[0]), w_ret_o_b)
    tail_w = (row(conv_ln_g[0]), row(conv_ln_b[0]), w_conv_o_b, w_out_b, row(ln1_g[0]), row(ln1_b[0]),
              wr_hi, wr_lo, b_r)

    nbt = bs // BB_SAMPLE
    xs2 = x_sample.reshape(n_s, d)
    sample_in = ((xs2, cos_s, sin_s, pdec_s, qdec_s, kdec_s, cdec_s, conv_w0, state_ret, state_conv)
                 + shared_w + (row(conv_b[0]),) + tail_w)
    sample_specs = (
        [pl.BlockSpec((ts, d), lambda i: (i, 0))]
        + [_const_spec(a.shape) for a in sample_in[1:8]]
        + [pl.BlockSpec((1, BB_SAMPLE, RET_HEADS, RET_DK, RET_DV), lambda i: (0, i, 0, 0, 0)),
           pl.BlockSpec((1, BB_SAMPLE, nstate, CONV_CH), lambda i: (0, i, 0, 0))]
        + [_const_spec(a.shape) for a in sample_in[10:]]
    )
    tok_spec_s = lambda w: pl.BlockSpec((ts, w), lambda i: (i, 0))
    x1_s, rw_s, ret_s, conv_s = pl.pallas_call(
        _sample_mixer_kernel,
        grid=(nbt,),
        in_specs=sample_specs,
        out_specs=[
            tok_spec_s(d), tok_spec_s(LANES),
            pl.BlockSpec((1, BB_SAMPLE, RET_HEADS, RET_DK, RET_DV), lambda i: (0, i, 0, 0, 0)),
            pl.BlockSpec((1, BB_SAMPLE, nstate, CONV_CH), lambda i: (0, i, 0, 0)),
        ],
        out_shape=[
            jax.ShapeDtypeStruct((n_s, d), F32),
            jax.ShapeDtypeStruct((n_s, LANES), F32),
            jax.ShapeDtypeStruct(state_ret.shape, F32),
            jax.ShapeDtypeStruct(state_conv.shape, F32),
        ],
        scratch_shapes=[
            pltpu.VMEM((ts, RET_V), F32),
            pltpu.VMEM((CONV_CH // LANES, ts, LANES), F32),
            pltpu.VMEM((BB_SAMPLE, XPAD_ROWS, CONV_CH), F32),
            pltpu.VMEM((ls, XPAD_ROWS, CONV_CH), F32),
        ],
        compiler_params=pltpu.CompilerParams(
            dimension_semantics=("arbitrary",), vmem_limit_bytes=VMEM_LIMIT),
        name="sample_mixer",
    )(*sample_in)

    assert lp % TLM == 0 and n_s % TLM == 0 and TLM % TL == 0
    nlt = lp // TLM
    npt = n_p // TLM
    nst = n_s // TLM
    sub = TLM // TL
    prompt_in = ((x_prompt, x1_s, rw_s, cos_p, sin_p, decay, qdec_p, kdec_p, cdec_p)
                 + shared_w + (rep8(conv_w0), row(conv_b[0])) + tail_w)
    head_tile = lambda s: jnp.minimum(s, npt - 1)
    sample_tile = lambda s: jnp.maximum(s - npt, 0)
    sample_spec = lambda w: pl.BlockSpec((TLM, w), lambda s: (sample_tile(s), 0))
    prompt_specs = [
        pl.BlockSpec((1, TLM, d), lambda s: (head_tile(s) // nlt, head_tile(s) % nlt, 0)),
        sample_spec(d), sample_spec(LANES),
        pl.BlockSpec((TLM, RET_DK), lambda s: (head_tile(s) % nlt, 0)),
        pl.BlockSpec((TLM, RET_DK), lambda s: (head_tile(s) % nlt, 0)),
    ] + [_const_spec(a.shape) for a in prompt_in[5:]]
    tok_spec_p = lambda rows, w: pl.BlockSpec((rows, w), lambda s: (s, 0))
    x1_all, rw_all, xs_all, meta, ret_p, conv_p = pl.pallas_call(
        functools.partial(_prompt_mixer_kernel, n_tiles=npt, tiles_per_seq=nlt),
        grid=(npt + nst,),
        in_specs=prompt_specs,
        out_specs=[
            tok_spec_p(TLM, d), tok_spec_p(TLM, LANES), tok_spec_p(sub * CAP, d), tok_spec_p(sub * LANES, LANES),
            pl.BlockSpec((1, 1, RET_HEADS, RET_DK, RET_DV), lambda s: (0, head_tile(s) // nlt, 0, 0, 0)),
            pl.BlockSpec((1, 1, nstate, CONV_CH), lambda s: (0, head_tile(s) // nlt, 0, 0)),
        ],
        out_shape=[
            jax.ShapeDtypeStruct((n_tok, d), F32),
            jax.ShapeDtypeStruct((n_tok, LANES), F32),
            jax.ShapeDtypeStruct((n_blocks * CAP, d), BF16),
            jax.ShapeDtypeStruct((n_blocks * LANES, LANES), I32),
            jax.ShapeDtypeStruct((1, bp, RET_HEADS, RET_DK, RET_DV), F32),
            jax.ShapeDtypeStruct((1, bp, nstate, CONV_CH), F32),
        ],
        scratch_shapes=[
            pltpu.VMEM((TLM + CONV_PAD, CONV_CH), F32),
            pltpu.VMEM((SUBLANES - 1, TL + CONV_PAD - SUBLANES, CONV_CH), F32),
            pltpu.VMEM((TLM, 2 * RET_QK + RET_V), F32),
            pltpu.VMEM((TLM, d), BF16),
            pltpu.VMEM((TLM, RET_V), F32),
            pltpu.VMEM((TLM, CONV_CH), F32),
            pltpu.VMEM((TLM, RET_V + 2 * D_MODEL), F32),
        ],
        compiler_params=pltpu.CompilerParams(
            dimension_semantics=("arbitrary",), vmem_limit_bytes=VMEM_LIMIT),
        name="prompt_mixer",
    )(*prompt_in)

    assert TOP_K * n_tok // TM_FFN >= 3
    max_chunks = n_blocks * (TOP_K * TL // CHUNK + N_EXPERTS - 1)
    n_ffn_tiles = (max_chunks + N_EXPERTS * (TILE_CHUNKS - 1)) // TILE_CHUNKS
    tile_e, n_valid_tiles, chunk_ids = _chunk_plan(meta, n_blocks, n_ffn_tiles)

    ys_all = pl.pallas_call(
        _ffn_kernel,
        grid_spec=pltpu.PrefetchScalarGridSpec(
            num_scalar_prefetch=3,
            grid=(n_ffn_tiles,),
            in_specs=[
                pl.BlockSpec(memory_space=pl.ANY),
                pl.BlockSpec((1, d, 2 * EXP_FF), lambda i, te, nr, ch: (te[i], 0, 0)),
                pl.BlockSpec((1, EXP_FF, d), lambda i, te, nr, ch: (te[i], 0, 0)),
            ],
            out_specs=pl.BlockSpec(memory_space=pl.ANY),
            scratch_shapes=[
                pltpu.VMEM((2, TM_FFN, d), BF16),
                pltpu.VMEM((2, TM_FFN, d), BF16),
                pltpu.VMEM((d, 2 * EXP_FF), BF16),
                pltpu.VMEM((EXP_FF, d), BF16),
                pltpu.SemaphoreType.DMA((2,)),
                pltpu.SemaphoreType.DMA((2,)),
            ],
        ),
        out_shape=jax.ShapeDtypeStruct(xs_all.shape, BF16),
        input_output_aliases={3: 0},
        compiler_params=pltpu.CompilerParams(
            dimension_semantics=("arbitrary",), vmem_limit_bytes=VMEM_LIMIT),
        name="expert_ffn",
    )(tile_e, n_valid_tiles, chunk_ids, xs_all, w_gu[0], w_dn[0])

    assert n_p % TLF == 0 and n_s % TLF == 0
    npt = n_p // TLF
    fsub = TLF // TL
    pp2 = p_prompt.reshape(n_p, PLE_DIM)
    ps2 = p_sample.reshape(n_s, PLE_DIM)
    tok_f = lambda rows, w: pl.BlockSpec((rows, w), lambda i: (i, 0))
    y_p, y_s = pl.pallas_call(
        functools.partial(_final_kernel, n_prompt_tiles=npt),
        grid=(n_tok // TLF,),
        in_specs=[
            tok_f(fsub * CAP, d), tok_f(TLF, d), tok_f(TLF, LANES),
            pl.BlockSpec((TLF, PLE_DIM), lambda i: (jnp.minimum(i, npt - 1), 0)),
            pl.BlockSpec((TLF, PLE_DIM), lambda i: (jnp.maximum(i - npt, 0), 0)),
            _const_spec((1, d)), _const_spec((1, d)), _const_spec((d, d)), _const_spec((1, d)),
            _const_spec((PLE_DIM, d)),
        ],
        out_specs=[
            pl.BlockSpec((TLF, d), lambda i: (jnp.minimum(i, npt - 1), 0)),
            pl.BlockSpec((TLF, d), lambda i: (jnp.maximum(i - npt, 0), 0)),
        ],
        out_shape=[jax.ShapeDtypeStruct((n_p, d), F32), jax.ShapeDtypeStruct((n_s, d), F32)],
        compiler_params=pltpu.CompilerParams(
            dimension_semantics=("arbitrary",), vmem_limit_bytes=VMEM_LIMIT),
        name="moe_combine_final",
    )(ys_all, x1_all, rw_all, pp2, ps2,
      row(ln2_g[0]), row(ln2_b[0]), w_pg_b, row(b_pg[0]), w_ple_b)

    return (y_p.reshape(bp, lp, d), y_s.reshape(bs, ls, d), ret_p, conv_p, ret_s, conv_s)
```

```python
import functools

import jax
import jax.numpy as jnp
import numpy as np
from jax import lax
from jax.experimental import pallas as pl
from jax.experimental.pallas import tpu as pltpu

F32 = jnp.float32
BF16 = jnp.bfloat16
I32 = jnp.int32

D_MODEL = 1024
PAST_LEN = 16384
RET_HEADS = 4
RET_DK = 128
RET_DV = 128
RET_QK = RET_HEADS * RET_DK
RET_V = RET_HEADS * RET_DV
RET_CHUNK = 128
ROPE_BASE = 10000.0
CONV_CH = 512
CONV_WIDTH = 31
N_GROUPS = 4
EXP_PER_GROUP = 4
N_EXPERTS = N_GROUPS * EXP_PER_GROUP
TOP_K = 2
EXP_FF = 512
PLE_DIM = 256
DEPTH = 1
ALPHA = (2 * DEPTH) ** 0.25
LN_EPS = 1e-5
IN_WIDTHS = (RET_QK, RET_QK, RET_V, RET_V, CONV_CH, CONV_CH, D_MODEL, D_MODEL)
IN_OFFS = tuple(int(s) for s in np.cumsum((0,) + IN_WIDTHS))

LANES = 128
SUBLANES = 8
VMEM_LIMIT = 56 * 1024 * 1024

TL = 256
TLM = 512
TLF = 512
BB_SAMPLE = 16
CHUNK = 2 * SUBLANES
TILE_CHUNKS = 32
BLOCK_USED = -(-(TOP_K * TL + N_EXPERTS * (CHUNK - 1)) // LANES) * LANES // CHUNK
BLOCK_SPARE = LANES // CHUNK
BLOCK_CHUNKS = BLOCK_USED + BLOCK_SPARE
USED_ROWS = BLOCK_USED * CHUNK
CAP = BLOCK_CHUNKS * CHUNK
TM_FFN = TILE_CHUNKS * CHUNK
FFN_COLS = 256
CONV_PAD = 32
XPAD_NEW = 32
XPAD_ROWS = 40


def _ln(x, g, b):
    mu = jnp.mean(x, axis=-1, keepdims=True)
    d = x - mu
    var = jnp.mean(d * d, axis=-1, keepdims=True)
    return d * lax.rsqrt(var + LN_EPS) * g + b


def _sigmoid(x):
    return 1.0 / (1.0 + jnp.exp(-x))


def _rep(v8, rows):
    return v8 if rows == SUBLANES else jnp.concatenate([v8] * (rows // SUBLANES), axis=0)


def _silu(x):
    return x * _sigmoid(x)


def _bdot(a, b):
    return jnp.dot(a.astype(BF16), b, preferred_element_type=F32)


def _rot(t, cosf, sinf):
    return t * cosf + pltpu.roll(t, RET_DK // 2, axis=1) * sinf


def _lane_tile(cols, rows):
    lane = lax.broadcasted_iota(I32, (rows, LANES), 1)
    out = jnp.zeros((rows, LANES), F32)
    for i, col in enumerate(cols):
        out = jnp.where(lane == i, col, out)
    return out


def _route(logits):
    lane = lax.broadcasted_iota(I32, logits.shape, 1)
    lanef = lane.astype(F32)
    ninf = jnp.float32(-jnp.inf)
    big = jnp.float32(LANES)
    gmask = lane < N_GROUPS
    gl = jnp.where(gmask, logits, ninf)
    gmax = jnp.max(gl, axis=1, keepdims=True)
    gidx = jnp.min(jnp.where(gmask & (gl == gmax), lanef, big), axis=1, keepdims=True)
    sumexp = jnp.sum(jnp.where(gmask, jnp.exp(gl - gmax), 0.0), axis=1, keepdims=True)
    gw = 1.0 / sumexp
    lo = N_GROUPS + EXP_PER_GROUP * gidx
    emask = (lanef >= lo) & (lanef < lo + EXP_PER_GROUP)
    el = jnp.where(emask, logits, ninf)
    m1 = jnp.max(el, axis=1, keepdims=True)
    i1 = jnp.min(jnp.where(emask & (el == m1), lanef, big), axis=1, keepdims=True)
    emask2 = emask & (lanef != i1)
    el2 = jnp.where(emask2, logits, ninf)
    m2 = jnp.max(el2, axis=1, keepdims=True)
    i2 = jnp.min(jnp.where(emask2 & (el2 == m2), lanef, big), axis=1, keepdims=True)
    t = jnp.exp(m2 - m1)
    den = 1.0 + t
    return (1.0 / den) * gw, (t / den) * gw, i1 - N_GROUPS, i2 - N_GROUPS


def _post_mix_pieces(src, w, sink):
    (w_ret_o, cln_g, cln_b, w_conv_o, w_out, ln1_g, ln1_b, wr_hi, wr_lo, b_r) = w
    st = {}

    def branch_a():
        st["a"] = _bdot(_silu(src["g"]()) * src["ret"](), w_ret_o[...])

    def branch_b():
        st["b"] = _bdot(_silu(_ln(src["cout"](), cln_g[...], cln_b[...])), w_conv_o[...])

    def merge():
        mix = _sigmoid(src["gt_a"]()) * st["a"] + _sigmoid(src["gt_b"]()) * st["b"]
        h = ALPHA * src["x"]() + _bdot(mix, w_out[...])
        st["x1"] = _ln(h, ln1_g[...], ln1_b[...])

    def router():
        x1 = st["x1"]
        x1_hi = x1.astype(BF16)
        x1_lo = (x1 - x1_hi.astype(F32)).astype(BF16)
        st["logits"] = (jnp.dot(x1_hi, wr_hi[...], preferred_element_type=F32)
                        + (jnp.dot(x1_lo, wr_hi[...], preferred_element_type=F32)
                           + jnp.dot(x1_hi, wr_lo[...], preferred_element_type=F32))
                        + b_r[...])

    def route():
        st["route"] = _route(st["logits"])

    def finish():
        sink(st["x1"], *st["route"])

    return [branch_a, branch_b, merge, router, route, finish]


def _sort_tile(x1, w1, w2, e1, e2, x1_ref, rw_ref, xs_ref, meta_ref):
    t = x1.shape[0]
    ids_t = _lane_tile((e1, e2), t).T
    e1r, e2r = ids_t[0:1, :], ids_t[1:2, :]
    sub = lax.broadcasted_iota(I32, (LANES, t), 0).astype(F32)
    a1 = (sub == e1r).astype(F32)
    a2 = (sub == e2r).astype(F32)
    ri = lax.broadcasted_iota(I32, (t, t), 0)
    ci = lax.broadcasted_iota(I32, (t, t), 1)
    earlier = (ri < ci).astype(BF16)
    r1 = jnp.dot(a1.astype(BF16), earlier, preferred_element_type=F32)
    r2 = jnp.dot(a2.astype(BF16), earlier, preferred_element_type=F32)
    cnt1 = jnp.sum(a1, axis=1, keepdims=True)
    cnt = cnt1 + jnp.sum(a2, axis=1, keepdims=True)
    nch = jnp.floor((cnt + (CHUNK - 1.0)) * (1.0 / CHUNK))
    ui = lax.broadcasted_iota(I32, (LANES, LANES), 0)
    uj = lax.broadcasted_iota(I32, (LANES, LANES), 1)
    before = (uj < ui).astype(BF16)
    off = jnp.dot(before, jnp.broadcast_to(nch, (LANES, LANES)).astype(BF16),
                  preferred_element_type=F32)[:, 0:1]
    base = off * CHUNK
    pos1r = jnp.sum(a1 * (base + r1), axis=0, keepdims=True)
    pos2r = jnp.sum(a2 * (base + cnt1 + r2), axis=0, keepdims=True)
    slot = lax.broadcasted_iota(I32, (USED_ROWS, t), 0).astype(F32)
    onehot = ((slot == pos1r) | (slot == pos2r)).astype(BF16)
    xs = jnp.dot(onehot, x1.astype(BF16), preferred_element_type=F32)
    pos_cols = jnp.where(sub == 2.0, pos1r, jnp.where(sub == 3.0, pos2r, 0.0)).T
    lane = lax.broadcasted_iota(I32, (t, LANES), 1)
    x1_ref[...] = x1
    rw_ref[...] = jnp.where(lane == 0, w1, jnp.where(lane == 1, w2, pos_cols))
    xs_ref[0:USED_ROWS, :] = xs.astype(BF16)
    xs_ref[USED_ROWS:CAP, :] = jnp.zeros((CAP - USED_ROWS, x1.shape[1]), BF16)
    mlane = lax.broadcasted_iota(I32, (LANES, LANES), 1)
    meta = jnp.where(mlane == 0, cnt, jnp.where(mlane == 1, off, 0.0))
    meta_ref[...] = meta.astype(I32)


def _sort_tiles(x1, w1, w2, e1, e2, x1_ref, rw_ref, xs_ref, meta_ref):
    for i in range(x1.shape[0] // TL):
        rows = slice(i * TL, (i + 1) * TL)
        _sort_tile(x1[rows], w1[rows], w2[rows], e1[rows], e2[rows],
                   x1_ref.at[pl.ds(i * TL, TL)], rw_ref.at[pl.ds(i * TL, TL)],
                   xs_ref.at[pl.ds(i * CAP, CAP)], meta_ref.at[pl.ds(i * LANES, LANES)])


GATE_COLS = {3: 0, 6: RET_V, 7: RET_V + D_MODEL}
QKV_COLS = {0: 0, 1: RET_QK, 2: 2 * RET_QK}


def _prompt_mixer_kernel(x_ref, x1s_ref, rws_ref, cos_ref, sin_ref, dec_ref, qdec_ref, kdec_ref, cdec_ref,
                         w_in, b_in, gn_g, gn_b, w_ret_o, conv_w, conv_b, cln_g, cln_b,
                         w_conv_o, w_out, ln1_g, ln1_b, wr_hi, wr_lo, b_r,
                         x1_ref, rw_ref, xs_ref, meta_ref, sret_ref, sconv_ref,
                         ubuf, ushift, qkv_scr, xb_scr, ret_scr, cout_scr, gate_scr,
                         *, n_tiles, tiles_per_seq):
    s = pl.program_id(0)
    li = lax.rem(s, tiles_per_seq)
    outs = (x1_ref, rw_ref, xs_ref, meta_ref)
    slot = dict(ret=ret_scr, cout=cout_scr, gates=gate_scr)
    tail_w = (w_ret_o, cln_g, cln_b, w_conv_o, w_out, ln1_g, ln1_b, wr_hi, wr_lo, b_r)

    @pl.when((s < n_tiles) & (li == 0))
    def _new_sequence():
        sret_ref[...] = jnp.zeros(sret_ref.shape, F32)
        ubuf[0:CONV_PAD, :] = jnp.zeros((CONV_PAD, CONV_CH), F32)

    @pl.when(s < n_tiles)
    def _mix():
        gcols = lambda kk: slice(GATE_COLS[kk], GATE_COLS[kk] + IN_WIDTHS[kk])
        src = dict(x=lambda: x_ref[0], ret=lambda: ret_scr[...], cout=lambda: cout_scr[...],
                   g=lambda: gate_scr[:, gcols(3)], gt_a=lambda: gate_scr[:, gcols(6)],
                   gt_b=lambda: gate_scr[:, gcols(7)])
        head = _prompt_head_pieces(x_ref, cos_ref, sin_ref, dec_ref, qdec_ref, kdec_ref, cdec_ref,
                                   w_in, b_in, gn_g, gn_b, conv_w, conv_b, sret_ref,
                                   ubuf, ushift, qkv_scr, xb_scr, slot)
        tail = _post_mix_pieces(src, tail_w, lambda *r: _sort_tiles(*r, *outs))
        for piece in head + tail:
            piece()

    @pl.when(s >= n_tiles)
    def _append():
        rws = rws_ref[...]
        _sort_tiles(x1s_ref[...], rws[:, 0:1], rws[:, 1:2], rws[:, 2:3], rws[:, 3:4], *outs)

    @pl.when((s < n_tiles) & (li == tiles_per_seq - 1))
    def _conv_state():
        sconv_ref[0, 0] = ubuf[CONV_PAD - (CONV_WIDTH - 1):CONV_PAD, :]


def _prompt_head_pieces(x_ref, cos_ref, sin_ref, dec_ref, qdec_ref, kdec_ref, cdec_ref,
                        w_in, b_in, gn_g, gn_b, conv_w, conv_b, sret_ref,
                        ubuf, ushift, qkv_scr, xb_scr, slot):
    tl = x_ref.shape[1]
    st = {}

    def slab_dot(c0, c1):
        return jnp.dot(xb_scr[...], w_in[:, c0:c1], preferred_element_type=F32) + _rep(b_in[:, c0:c1], tl)

    def glu():
        xb_scr[...] = x_ref[0].astype(BF16)
        u = slab_dot(IN_OFFS[4], IN_OFFS[5]) * _sigmoid(slab_dot(IN_OFFS[5], IN_OFFS[6]))
        ubuf[CONV_PAD:CONV_PAD + tl, :] = u

    nsh = ushift.shape[1]
    span = nsh - (CONV_PAD - SUBLANES)

    def shift_copy(h, s):
        ushift[s - 1] = ubuf[h * span + s:h * span + s + nsh, :]

    slab = 256
    slabs = [(kk, c0) for kk in (0, 1, 2, 3, 6, 7) for c0 in range(IN_OFFS[kk], IN_OFFS[kk + 1], slab)]
    rb = 32
    nrb = tl // rb

    def proj_slab(kk, c0):
        val = slab_dot(c0, c0 + slab)
        if kk in QKV_COLS:
            dst = QKV_COLS[kk] + c0 - IN_OFFS[kk]
            qkv_scr[:, dst:dst + slab] = val
        else:
            dst = GATE_COLS[kk] + c0 - IN_OFFS[kk]
            slot["gates"][:, dst:dst + slab] = val

    def conv_block(r):
        h, rl = divmod(r * rb, span)
        acc = jnp.zeros((rb, CONV_CH), F32) + conv_b[...]
        for j in range(CONV_WIDTH):
            off = j + (CONV_PAD - (CONV_WIDTH - 1))
            s = off % SUBLANES
            base = rl + off - s
            win = (ubuf[h * span + base:h * span + base + rb, :] if s == 0
                   else ushift[s - 1, base:base + rb, :])
            acc = acc + _rep(conv_w[j], rb) * win
        slot["cout"][r * rb:(r + 1) * rb, :] = acc
        if r == nrb - 1:
            ubuf[0:CONV_PAD, :] = ubuf[tl:tl + CONV_PAD, :]

    scale = RET_DK ** -0.5

    def retention(c, h):
        rows = slice(c * RET_CHUNK, (c + 1) * RET_CHUNK)
        cols = slice(h * RET_DK, (h + 1) * RET_DK)
        hcol = lambda kk: slice(QKV_COLS[kk] + h * RET_DK, QKV_COLS[kk] + (h + 1) * RET_DK)
        cosf = cos_ref[rows, :]
        sinf = sin_ref[rows, :]
        qh = _rot(qkv_scr[rows, hcol(0)], cosf, sinf)
        kh = _rot(qkv_scr[rows, hcol(1)], cosf, sinf) * scale
        qb = qh.astype(BF16)
        kb = kh.astype(BF16)
        vb = qkv_scr[rows, hcol(2)].astype(BF16)
        s_old = sret_ref[0, 0, h]
        scores = lax.dot_general(qb, kb, (((1,), (1,)), ((), ())),
                                 preferred_element_type=F32) * dec_ref[h]
        inner = jnp.dot(scores.astype(BF16), vb, preferred_element_type=F32)
        cross = jnp.dot(qb, s_old.astype(BF16), preferred_element_type=F32) * qdec_ref[h]
        kd = (kh * kdec_ref[h]).astype(BF16)
        s_new = cdec_ref[h] * s_old + lax.dot_general(
            kd, vb, (((0,), (0,)), ((), ())), preferred_element_type=F32)
        sret_ref[0, 0, h] = s_new
        slot["ret"][rows, cols] = _ln(inner + cross, gn_g[:, cols], gn_b[:, cols])

    vector_pieces = []
    for r in range(nrb):
        if (r * rb) % span == 0:
            vector_pieces += [lambda h=(r * rb) // span, s=s: shift_copy(h, s) for s in range(1, SUBLANES)]
        vector_pieces.append(lambda r=r: conv_block(r))
    pieces = [glu]
    for i, piece in enumerate(vector_pieces):
        pieces.append(piece)
        for kk, c0 in slabs[i * len(slabs) // len(vector_pieces):(i + 1) * len(slabs) // len(vector_pieces)]:
            pieces.append(lambda kk=kk, c0=c0: proj_slab(kk, c0))
    pieces += [lambda c=c, h=h: retention(c, h) for c in range(tl // RET_CHUNK) for h in range(RET_HEADS)]
    return pieces


def _sample_mixer_kernel(x_ref, cos_ref, sin_ref, pdec_ref, qdec_ref, kdec_ref, cdec_ref, conv_w,
                         sret_in, sconv_in,
                         w_in, b_in, gn_g, gn_b, w_ret_o, conv_b, cln_g, cln_b,
                         w_conv_o, w_out, ln1_g, ln1_b, wr_hi, wr_lo, b_r,
                         x1_ref, rw_ref, sret_ref, sconv_ref,
                         ret_scr, cout_scr, gate_scr, xpad, wsh):
    i = pl.program_id(0)
    t = cos_ref.shape[0]
    ls = t // BB_SAMPLE
    nstate = CONV_WIDTH - 1
    r0 = pl.multiple_of(i * t, t)
    xb = x_ref[pl.ds(r0, t), :].astype(BF16)

    @pl.when(i == 0)
    def _tap_tables():
        wsh[...] = jnp.zeros(wsh.shape, F32)
        for p in range(ls):
            wsh[p, p:nstate, :] = conv_w[0:nstate - p, :]
            wsh[p, XPAD_NEW:XPAD_NEW + p + 1, :] = conv_w[nstate - p:CONV_WIDTH, :]

    def proj(k):
        c0, c1 = IN_OFFS[k], IN_OFFS[k + 1]
        return jnp.dot(xb, w_in[:, c0:c1], preferred_element_type=F32) + _rep(b_in[:, c0:c1], t)

    q = proj(0)
    k = proj(1)
    v = proj(2)
    scale = RET_DK ** -0.5
    cosf = cos_ref[...]
    sinf = sin_ref[...]
    row = lax.broadcasted_iota(I32, (t, RET_DK), 0)
    pos = row % ls
    row8 = lax.broadcasted_iota(I32, (SUBLANES, RET_DK), 0)
    per_tile = SUBLANES // ls
    for h in range(RET_HEADS):
        cols = slice(h * RET_DK, (h + 1) * RET_DK)
        qh = _rot(q[:, cols], cosf, sinf)
        kh = _rot(k[:, cols], cosf, sinf) * scale
        vh = v[:, cols]
        inner = jnp.zeros((t, RET_DV), F32)
        for s in range(ls):
            ks = kh if s == 0 else pltpu.roll(kh, s, axis=0)
            vs = vh if s == 0 else pltpu.roll(vh, s, axis=0)
            dotp = jnp.sum(qh * ks, axis=1, keepdims=True) * pdec_ref[h, s]
            inner = inner + jnp.where(pos >= s, dotp, 0.0) * vs
        kd = kh * kdec_ref[h]
        for tile in range(t // SUBLANES):
            rows = slice(tile * SUBLANES, (tile + 1) * SUBLANES)
            q8 = qh[rows, :]
            kd8 = kd[rows, :]
            v8 = vh[rows, :]
            seqs = [tile * per_tile + sub for sub in range(per_tile)]
            mine = [(row8 >= sub * ls) & (row8 < (sub + 1) * ls) for sub in range(per_tile)]
            s_old = [sret_in[0, b, h] for b in seqs]
            c_all = jnp.dot(q8, jnp.concatenate(s_old, axis=1), preferred_element_type=F32)
            upd = lax.dot_general(jnp.concatenate([jnp.where(m, kd8, 0.0) for m in mine], axis=1), v8,
                                  (((0,), (0,)), ((), ())), preferred_element_type=F32)
            cross8 = jnp.zeros((SUBLANES, RET_DV), F32)
            for sub, b in enumerate(seqs):
                cross8 = jnp.where(mine[sub], c_all[:, sub * RET_DV:(sub + 1) * RET_DV], cross8)
                sret_ref[0, b, h] = cdec_ref[h] * s_old[sub] + upd[sub * RET_DK:(sub + 1) * RET_DK, :]
            ret_scr[pl.ds(r0 + tile * SUBLANES, SUBLANES), cols] = (
                inner[rows, :] + cross8 * qdec_ref[h, rows, :])
        ret_scr[pl.ds(r0, t), cols] = _ln(ret_scr[pl.ds(r0, t), cols], gn_g[:, cols], gn_b[:, cols])

    u = proj(4) * _sigmoid(proj(5))
    xpad[...] = jnp.zeros(xpad.shape, F32)
    xpad[:, 0:nstate, :] = sconv_in[0]
    for b in range(BB_SAMPLE):
        xpad[b, XPAD_NEW:XPAD_NEW + ls, :] = u[b * ls:(b + 1) * ls, :]
    for p in range(ls):
        res = jnp.sum(xpad[...] * wsh[p][None], axis=1) + conv_b[...]
        for sl in range(CONV_CH // LANES):
            cout_scr[sl, pl.ds(r0 + p, BB_SAMPLE, stride=ls), :] = res[:, sl * LANES:(sl + 1) * LANES]
    sconv_ref[0, :, 0:nstate - ls, :] = xpad[:, ls:nstate, :]
    sconv_ref[0, :, nstate - ls:nstate, :] = xpad[:, XPAD_NEW:XPAD_NEW + ls, :]
    for kk in (3, 6, 7):
        gate_scr[pl.ds(r0, t), GATE_COLS[kk]:GATE_COLS[kk] + IN_WIDTHS[kk]] = proj(kk)

    @pl.when(i == pl.num_programs(0) - 1)
    def _second_half():
        n = x_ref.shape[0]

        def sink(x1, w1, w2, e1, e2):
            x1_ref[...] = x1
            rw_ref[...] = _lane_tile((w1, w2, e1, e2), n)

        gcols = lambda kk: slice(GATE_COLS[kk], GATE_COLS[kk] + IN_WIDTHS[kk])
        src = dict(x=lambda: x_ref[...], ret=lambda: ret_scr[...],
                   cout=lambda: jnp.concatenate([cout_scr[sl] for sl in range(CONV_CH // LANES)], axis=1),
                   g=lambda: gate_scr[:, gcols(3)], gt_a=lambda: gate_scr[:, gcols(6)],
                   gt_b=lambda: gate_scr[:, gcols(7)])
        for piece in _post_mix_pieces(
                src, (w_ret_o, cln_g, cln_b, w_conv_o, w_out, ln1_g, ln1_b, wr_hi, wr_lo, b_r), sink):
            piece()


def _ffn_kernel(te_ref, nvalid_ref, chunk_ref, xs_hbm, w_gu, w_dn, ys_hbm,
                xbuf, obuf, wgu_b, wdn_b, sem_in, sem_out):
    del xs_hbm
    i = pl.program_id(0)
    n = pl.num_programs(0)
    slot = i % 2
    nvalid = nvalid_ref[0]

    def chunk_rows(tile, c):
        return pl.ds(pl.multiple_of(chunk_ref[tile * TILE_CHUNKS + c] * CHUNK, CHUNK), CHUNK)

    def start_in(tile, s):
        for c in range(TILE_CHUNKS):
            pltpu.make_async_copy(ys_hbm.at[chunk_rows(tile, c)],
                                  xbuf.at[s, pl.ds(c * CHUNK, CHUNK)], sem_in.at[s]).start()

    def start_out(tile, s):
        for c in range(TILE_CHUNKS):
            pltpu.make_async_copy(obuf.at[s, pl.ds(c * CHUNK, CHUNK)],
                                  ys_hbm.at[chunk_rows(tile, c)], sem_out.at[s]).start()

    def wait_in(s):
        pltpu.make_async_copy(ys_hbm.at[pl.ds(0, TM_FFN)], xbuf.at[s], sem_in.at[s]).wait()

    def wait_out(s):
        pltpu.make_async_copy(obuf.at[s], ys_hbm.at[pl.ds(0, TM_FFN)], sem_out.at[s]).wait()

    @pl.when((i == 0) & (nvalid > 0))
    def _first():
        start_in(0, 0)

    @pl.when((i >= 2) & (i - 2 < nvalid))
    def _retire():
        wait_out(slot)

    @pl.when(i < nvalid)
    def _tile():
        wait_in(slot)
        prev = te_ref[jnp.maximum(i - 1, 0)]

        @pl.when((i == 0) | (te_ref[i] != prev))
        def _new_expert():
            wgu_b[...] = w_gu[0].astype(BF16)
            wdn_b[...] = w_dn[0].astype(BF16)

        x = xbuf[slot]
        y = jnp.zeros((TM_FFN, w_dn.shape[2]), F32)
        for c0 in range(0, EXP_FF, FFN_COLS):
            hg = jnp.dot(x, wgu_b[:, c0:c0 + FFN_COLS], preferred_element_type=F32)
            hu = jnp.dot(x, wgu_b[:, EXP_FF + c0:EXP_FF + c0 + FFN_COLS], preferred_element_type=F32)
            y = y + _bdot(_silu(hg) * hu, wdn_b[c0:c0 + FFN_COLS, :])
            if c0 == 0:
                start_in(jnp.where(i + 1 < nvalid, i + 1, 0), 1 - slot)
        obuf[slot] = y.astype(BF16)
        start_out(i, slot)

    @pl.when(i == n - 1)
    def _drain():
        @pl.when((i >= 1) & (i - 1 < nvalid))
        def _():
            wait_out(1 - slot)

        @pl.when(i < nvalid)
        def _():
            wait_out(slot)

        @pl.when(nvalid > 0)
        def _():
            wait_in(nvalid % 2)


def _final_kernel(ys_ref, x1_ref, rw_ref, pp_ref, ps_ref, ln2_g, ln2_b, w_pg, b_pg, w_ple,
                  yp_ref, ys_out_ref, *, n_prompt_tiles):
    i = pl.program_id(0)
    x1 = x1_ref[...]
    slot = lax.broadcasted_iota(I32, (TL, USED_ROWS), 1).astype(F32)
    parts = []
    for b in range(x1.shape[0] // TL):
        rw = rw_ref[b * TL:(b + 1) * TL, :]
        w1, w2, pos1, pos2 = rw[:, 0:1], rw[:, 1:2], rw[:, 2:3], rw[:, 3:4]
        ys = ys_ref[b * CAP:b * CAP + USED_ROWS, :]
        comb = jnp.where(slot == pos1, w1, jnp.where(slot == pos2, w2, 0.0)).astype(BF16)
        parts.append(jnp.dot(comb, ys, preferred_element_type=F32))
    moe = parts[0] if len(parts) == 1 else jnp.concatenate(parts, axis=0)
    x2 = _ln(ALPHA * x1 + moe, ln2_g[...], ln2_b[...])
    gate = _sigmoid(_bdot(x2, w_pg[...]) + b_pg[...])
    p = jnp.where(i < n_prompt_tiles, pp_ref[...], ps_ref[...])
    y = x2 + gate * _bdot(p, w_ple[...])

    @pl.when(i < n_prompt_tiles)
    def _prompt():
        yp_ref[...] = y

    @pl.when(i >= n_prompt_tiles)
    def _sample():
        ys_out_ref[...] = y


def _rope_tables(pos):
    half = RET_DK // 2
    inv_freq = ROPE_BASE ** (-np.arange(half, dtype=np.float64) / half)
    ang = np.asarray(pos, np.float64)[:, None] * inv_freq[None, :]
    cos = np.cos(ang)
    sin = np.sin(ang)
    return (np.concatenate([cos, cos], axis=-1).astype(np.float32),
            np.concatenate([-sin, sin], axis=-1).astype(np.float32))


def _log_gamma():
    return np.log(1.0 - 2.0 ** (-5.0 - np.arange(RET_HEADS, dtype=np.float64)))


def _const_spec(shape):
    nd = len(shape)
    return pl.BlockSpec(shape, lambda *_: (0,) * nd, pipeline_mode=pl.Buffered(1))


def _chunk_plan(meta, n_blocks, n_ffn_tiles):
    assert n_blocks * BLOCK_SPARE >= N_EXPERTS * (TILE_CHUNKS - 1)
    m = meta.reshape(n_blocks, LANES, LANES)
    cnt = m[:, :N_EXPERTS, 0]
    off = m[:, :N_EXPERTS, 1]
    nch = (cnt + (CHUNK - 1)) // CHUNK
    cum = jnp.cumsum(nch, axis=0)
    total = cum[-1:]
    tiles_e = (total + TILE_CHUNKS - 1) // TILE_CHUNKS
    tile_end = jnp.cumsum(tiles_e, axis=1)
    tile_start = tile_end - tiles_e
    tid = jnp.arange(n_ffn_tiles, dtype=I32)[:, None]
    owner = (tid >= tile_start) & (tid < tile_end)
    pick_e = lambda v: jnp.sum(jnp.where(owner, v, 0), axis=1, keepdims=True)
    te = pick_e(jnp.arange(N_EXPERTS, dtype=I32)[None, :])
    k = (tid - pick_e(tile_start)) * TILE_CHUNKS + jnp.arange(TILE_CHUNKS, dtype=I32)[None, :]
    total_t = pick_e(total)
    real = k < total_t
    by_tile = lambda v: jnp.sum(jnp.where(owner[:, None, :], v[None, :, :], 0), axis=2)
    cum_t = by_tile(cum)
    blk = jnp.minimum(jnp.sum((cum_t[:, None, :] <= k[:, :, None]).astype(I32), axis=2), n_blocks - 1)
    at_blk = blk[:, :, None] == jnp.arange(n_blocks, dtype=I32)[None, None, :]
    pick_b = lambda v: jnp.sum(jnp.where(at_blk, v[:, None, :], 0), axis=2)
    excl = pick_b(cum_t - by_tile(nch))
    off_t = pick_b(by_tile(off))
    spare = te * (TILE_CHUNKS - 1) + jnp.maximum(k - total_t, 0) % TILE_CHUNKS
    spare_chunk = (spare // BLOCK_SPARE) * BLOCK_CHUNKS + BLOCK_USED + spare % BLOCK_SPARE
    chunk = jnp.where(real, blk * BLOCK_CHUNKS + off_t + (k - excl), spare_chunk)
    n_valid = jnp.sum(tiles_e, axis=1)
    te = jnp.where(tid < n_valid, te, N_EXPERTS - 1)
    return te.reshape(-1).astype(I32), n_valid.astype(I32), chunk.reshape(-1).astype(I32)


def kernel(x_prompt, x_sample, state_ret, state_conv, p_prompt, p_sample, w_in, b_in, ret_gn_g, ret_gn_b,
           w_ret_o, conv_w, conv_b, conv_ln_g, conv_ln_b, w_conv_o, w_out, ln1_g, ln1_b, w_grp, b_grp,
           w_exp, b_exp, w_gu, w_dn, ln2_g, ln2_b, w_pg, b_pg, w_ple):
    assert DEPTH == 1 and w_in.shape[0] == 1
    bp, lp, d = x_prompt.shape
    bs, ls, _ = x_sample.shape
    n_p, n_s = bp * lp, bs * ls
    n_tok = n_p + n_s
    assert lp % TL == 0 and n_s % TL == 0 and bs % BB_SAMPLE == 0 and SUBLANES % ls == 0
    n_blocks = n_tok // TL

    f32c = lambda a, shape: jnp.asarray(np.broadcast_to(a, shape).astype(np.float32))
    lg = _log_gamma()
    c = RET_CHUNK
    idx = np.arange(c, dtype=np.float64)
    rel = idx[:, None] - idx[None, :]
    causal = rel >= 0
    decay = np.where(causal[None], np.exp(np.where(causal, rel, 0.0)[None] * lg[:, None, None]), 0.0)
    decay = f32c(decay, decay.shape)
    q_decay = np.exp((idx[:, None] + 1.0) * lg[None, :])
    k_decay = np.exp((c - 1.0 - idx[:, None]) * lg[None, :])
    chunk_decay = np.exp(c * lg)
    qdec_p = f32c(q_decay.T[:, :, None], (RET_HEADS, c, RET_DK))
    kdec_p = f32c(k_decay.T[:, :, None], (RET_HEADS, c, RET_DK))
    cdec_p = f32c(chunk_decay[:, None, None], (RET_HEADS, 1, RET_DV))
    cos_p, sin_p = (jnp.asarray(a) for a in _rope_tables(np.arange(lp)))

    ts = BB_SAMPLE * ls
    idx_s = np.arange(ls, dtype=np.float64)
    pdec_s = np.exp(idx_s[None, :] * lg[:, None])
    pdec_s = f32c(pdec_s[:, :, None, None], (RET_HEADS, ls, 1, RET_DK))
    qd_s = np.exp((idx_s[:, None] + 1.0) * lg[None, :])
    kd_s = np.exp((ls - 1.0 - idx_s[:, None]) * lg[None, :])
    qdec_s = f32c(np.tile(qd_s.T, (1, BB_SAMPLE))[:, :, None], (RET_HEADS, ts, RET_DK))
    kdec_s = f32c(np.tile(kd_s.T, (1, BB_SAMPLE))[:, :, None], (RET_HEADS, ts, RET_DK))
    cdec_s = f32c(np.exp(ls * lg)[:, None, None], (RET_HEADS, 1, RET_DV))
    cos_s, sin_s = (jnp.asarray(a) for a in _rope_tables(np.tile(PAST_LEN + np.arange(ls), BB_SAMPLE)))

    w_in_b = w_in[0].astype(BF16)
    w_ret_o_b = w_ret_o[0].astype(BF16)
    w_conv_o_b = w_conv_o[0].astype(BF16)
    w_out_b = w_out[0].astype(BF16)
    w_pg_b = w_pg[0].astype(BF16)
    w_ple_b = w_ple[0].astype(BF16)
    n_route = N_GROUPS + N_EXPERTS
    w_r = jnp.concatenate([w_grp[0], w_exp[0], jnp.zeros((d, LANES - n_route), F32)], axis=1)
    wr_hi = w_r.astype(BF16)
    wr_lo = (w_r - wr_hi.astype(F32)).astype(BF16)
    b_r = jnp.concatenate([b_grp[0], b_exp[0], jnp.zeros((LANES - n_route,), F32)]).reshape(1, LANES)
    row = lambda a: a.reshape(1, -1)
    conv_w0 = conv_w[0]
    nstate = CONV_WIDTH - 1

    rep8 = lambda a: jnp.broadcast_to(a[..., None, :], a.shape[:-1] + (SUBLANES, a.shape[-1]))
    shared_w = (w_in_b, rep8(b_in[0]), row(ret_gn_g[0]), row(ret_gn_b[0]), w_ret_o_b)
    tail_w = (row(conv_ln_g[0]), row(conv_ln_b[0]), w_conv_o_b, w_out_b, row(ln1_g[0]), row(ln1_b[0]),
              wr_hi, wr_lo, b_r)

    nbt = bs // BB_SAMPLE
    xs2 = x_sample.reshape(n_s, d)
    sample_in = ((xs2, cos_s, sin_s, pdec_s, qdec_s, kdec_s, cdec_s, conv_w0, state_ret, state_conv)
                 + shared_w + (row(conv_b[0]),) + tail_w)
    sample_specs = (
        [_const_spec(a.shape) for a in sample_in[0:8]]
        + [pl.BlockSpec((1, BB_SAMPLE, RET_HEADS, RET_DK, RET_DV), lambda i: (0, i, 0, 0, 0)),
           pl.BlockSpec((1, BB_SAMPLE, nstate, CONV_CH), lambda i: (0, i, 0, 0))]
        + [_const_spec(a.shape) for a in sample_in[10:]]
    )
    tok_spec_s = lambda w: pl.BlockSpec((n_s, w), lambda i: (0, 0))
    x1_s, rw_s, ret_s, conv_s = pl.pallas_call(
        _sample_mixer_kernel,
        grid=(nbt,),
        in_specs=sample_specs,
        out_specs=[
            tok_spec_s(d), tok_spec_s(LANES),
            pl.BlockSpec((1, BB_SAMPLE, RET_HEADS, RET_DK, RET_DV), lambda i: (0, i, 0, 0, 0)),
            pl.BlockSpec((1, BB_SAMPLE, nstate, CONV_CH), lambda i: (0, i, 0, 0)),
        ],
        out_shape=[
            jax.ShapeDtypeStruct((n_s, d), F32),
            jax.ShapeDtypeStruct((n_s, LANES), F32),
            jax.ShapeDtypeStruct(state_ret.shape, F32),
            jax.ShapeDtypeStruct(state_conv.shape, F32),
        ],
        scratch_shapes=[
            pltpu.VMEM((n_s, RET_V), F32),
            pltpu.VMEM((CONV_CH // LANES, n_s, LANES), F32),
            pltpu.VMEM((n_s, RET_V + 2 * D_MODEL), F32),
            pltpu.VMEM((BB_SAMPLE, XPAD_ROWS, CONV_CH), F32),
            pltpu.VMEM((ls, XPAD_ROWS, CONV_CH), F32),
        ],
        compiler_params=pltpu.CompilerParams(
            dimension_semantics=("arbitrary",), vmem_limit_bytes=VMEM_LIMIT),
        name="sample_mixer",
    )(*sample_in)

    assert lp % TLM == 0 and n_s % TLM == 0 and TLM % TL == 0
    nlt = lp // TLM
    npt = n_p // TLM
    nst = n_s // TLM
    sub = TLM // TL
    prompt_in = ((x_prompt, x1_s, rw_s, cos_p, sin_p, decay, qdec_p, kdec_p, cdec_p)
                 + shared_w + (rep8(conv_w0), row(conv_b[0])) + tail_w)
    head_tile = lambda s: jnp.minimum(s, npt - 1)
    sample_tile = lambda s: jnp.maximum(s - npt, 0)
    sample_spec = lambda w: pl.BlockSpec((TLM, w), lambda s: (sample_tile(s), 0))
    prompt_specs = [
        pl.BlockSpec((1, TLM, d), lambda s: (head_tile(s) // nlt, head_tile(s) % nlt, 0)),
        sample_spec(d), sample_spec(LANES),
        pl.BlockSpec((TLM, RET_DK), lambda s: (head_tile(s) % nlt, 0)),
        pl.BlockSpec((TLM, RET_DK), lambda s: (head_tile(s) % nlt, 0)),
    ] + [_const_spec(a.shape) for a in prompt_in[5:]]
    tok_spec_p = lambda rows, w: pl.BlockSpec((rows, w), lambda s: (s, 0))
    x1_all, rw_all, xs_all, meta, ret_p, conv_p = pl.pallas_call(
        functools.partial(_prompt_mixer_kernel, n_tiles=npt, tiles_per_seq=nlt),
        grid=(npt + nst,),
        in_specs=prompt_specs,
        out_specs=[
            tok_spec_p(TLM, d), tok_spec_p(TLM, LANES), tok_spec_p(sub * CAP, d), tok_spec_p(sub * LANES, LANES),
            pl.BlockSpec((1, 1, RET_HEADS, RET_DK, RET_DV), lambda s: (0, head_tile(s) // nlt, 0, 0, 0)),
            pl.BlockSpec((1, 1, nstate, CONV_CH), lambda s: (0, head_tile(s) // nlt, 0, 0)),
        ],
        out_shape=[
            jax.ShapeDtypeStruct((n_tok, d), F32),
            jax.ShapeDtypeStruct((n_tok, LANES), F32),
            jax.ShapeDtypeStruct((n_blocks * CAP, d), BF16),
            jax.ShapeDtypeStruct((n_blocks * LANES, LANES), I32),
            jax.ShapeDtypeStruct((1, bp, RET_HEADS, RET_DK, RET_DV), F32),
            jax.ShapeDtypeStruct((1, bp, nstate, CONV_CH), F32),
        ],
        scratch_shapes=[
            pltpu.VMEM((TLM + CONV_PAD, CONV_CH), F32),
            pltpu.VMEM((SUBLANES - 1, TL + CONV_PAD - SUBLANES, CONV_CH), F32),
            pltpu.VMEM((TLM, 2 * RET_QK + RET_V), F32),
            pltpu.VMEM((TLM, d), BF16),
            pltpu.VMEM((TLM, RET_V), F32),
            pltpu.VMEM((TLM, CONV_CH), F32),
            pltpu.VMEM((TLM, RET_V + 2 * D_MODEL), F32),
        ],
        compiler_params=pltpu.CompilerParams(
            dimension_semantics=("arbitrary",), vmem_limit_bytes=VMEM_LIMIT),
        name="prompt_mixer",
    )(*prompt_in)

    assert TOP_K * n_tok // TM_FFN >= 3
    max_chunks = n_blocks * (TOP_K * TL // CHUNK + N_EXPERTS - 1)
    n_ffn_tiles = (max_chunks + N_EXPERTS * (TILE_CHUNKS - 1)) // TILE_CHUNKS
    tile_e, n_valid_tiles, chunk_ids = _chunk_plan(meta, n_blocks, n_ffn_tiles)

    ys_all = pl.pallas_call(
        _ffn_kernel,
        grid_spec=pltpu.PrefetchScalarGridSpec(
            num_scalar_prefetch=3,
            grid=(n_ffn_tiles,),
            in_specs=[
                pl.BlockSpec(memory_space=pl.ANY),
                pl.BlockSpec((1, d, 2 * EXP_FF), lambda i, te, nr, ch: (te[i], 0, 0)),
                pl.BlockSpec((1, EXP_FF, d), lambda i, te, nr, ch: (te[i], 0, 0)),
            ],
            out_specs=pl.BlockSpec(memory_space=pl.ANY),
            scratch_shapes=[
                pltpu.VMEM((2, TM_FFN, d), BF16),
                pltpu.VMEM((2, TM_FFN, d), BF16),
                pltpu.VMEM((d, 2 * EXP_FF), BF16),
                pltpu.VMEM((EXP_FF, d), BF16),
                pltpu.SemaphoreType.DMA((2,)),
                pltpu.SemaphoreType.DMA((2,)),
            ],
        ),
        out_shape=jax.ShapeDtypeStruct(xs_all.shape, BF16),
        input_output_aliases={3: 0},
        compiler_params=pltpu.CompilerParams(
            dimension_semantics=("arbitrary",), vmem_limit_bytes=VMEM_LIMIT),
        name="expert_ffn",
    )(tile_e, n_valid_tiles, chunk_ids, xs_all, w_gu[0], w_dn[0])

    assert n_p % TLF == 0 and n_s % TLF == 0
    npt = n_p // TLF
    fsub = TLF // TL
    pp2 = p_prompt.reshape(n_p, PLE_DIM)
    ps2 = p_sample.reshape(n_s, PLE_DIM)
    tok_f = lambda rows, w: pl.BlockSpec((rows, w), lambda i: (i, 0))
    y_p, y_s = pl.pallas_call(
        functools.partial(_final_kernel, n_prompt_tiles=npt),
        grid=(n_tok // TLF,),
        in_specs=[
            tok_f(fsub * CAP, d), tok_f(TLF, d), tok_f(TLF, LANES),
            pl.BlockSpec((TLF, PLE_DIM), lambda i: (jnp.minimum(i, npt - 1), 0)),
            pl.BlockSpec((TLF, PLE_DIM), lambda i: (jnp.maximum(i - npt, 0), 0)),
            _const_spec((1, d)), _const_spec((1, d)), _const_spec((d, d)), _const_spec((1, d)),
            _const_spec((PLE_DIM, d)),
        ],
        out_specs=[
            pl.BlockSpec((TLF, d), lambda i: (jnp.minimum(i, npt - 1), 0)),
            pl.BlockSpec((TLF, d), lambda i: (jnp.maximum(i - npt, 0), 0)),
        ],
        out_shape=[jax.ShapeDtypeStruct((n_p, d), F32), jax.ShapeDtypeStruct((n_s, d), F32)],
        compiler_params=pltpu.CompilerParams(
            dimension_semantics=("arbitrary",), vmem_limit_bytes=VMEM_LIMIT),
        name="moe_combine_final",
    )(ys_all, x1_all, rw_all, pp2, ps2,
      row(ln2_g[0]), row(ln2_b[0]), w_pg_b, row(b_pg[0]), w_ple_b)

    return (y_p.reshape(bp, lp, d), y_s.reshape(bs, ls, d), ret_p, conv_p, ret_s, conv_s)
```

```python
import functools

import jax
import jax.numpy as jnp
import numpy as np
from jax import lax
from jax.experimental import pallas as pl
from jax.experimental.pallas import tpu as pltpu

F32 = jnp.float32
BF16 = jnp.bfloat16
I32 = jnp.int32

D_MODEL = 1024
PAST_LEN = 16384
RET_HEADS = 4
RET_DK = 128
RET_DV = 128
RET_QK = RET_HEADS * RET_DK
RET_V = RET_HEADS * RET_DV
RET_CHUNK = 128
ROPE_BASE = 10000.0
CONV_CH = 512
CONV_WIDTH = 31
N_GROUPS = 4
EXP_PER_GROUP = 4
N_EXPERTS = N_GROUPS * EXP_PER_GROUP
TOP_K = 2
EXP_FF = 512
PLE_DIM = 256
DEPTH = 1
ALPHA = (2 * DEPTH) ** 0.25
LN_EPS = 1e-5
IN_WIDTHS = (RET_QK, RET_QK, RET_V, RET_V, CONV_CH, CONV_CH, D_MODEL, D_MODEL)
IN_OFFS = tuple(int(s) for s in np.cumsum((0,) + IN_WIDTHS))

LANES = 128
SUBLANES = 8
VMEM_LIMIT = 56 * 1024 * 1024

TL = 256
TLM = 512
TLF = 512
BB_SAMPLE = 16
CHUNK = 2 * SUBLANES
TILE_CHUNKS = 32
BLOCK_USED = -(-(TOP_K * TL + N_EXPERTS * (CHUNK - 1)) // LANES) * LANES // CHUNK
BLOCK_SPARE = LANES // CHUNK
BLOCK_CHUNKS = BLOCK_USED + BLOCK_SPARE
USED_ROWS = BLOCK_USED * CHUNK
CAP = BLOCK_CHUNKS * CHUNK
TM_FFN = TILE_CHUNKS * CHUNK
FFN_COLS = 256
CONV_PAD = 32
XPAD_NEW = 32
XPAD_ROWS = 40


def _ln(x, g, b):
    mu = jnp.mean(x, axis=-1, keepdims=True)
    d = x - mu
    var = jnp.mean(d * d, axis=-1, keepdims=True)
    return d * lax.rsqrt(var + LN_EPS) * g + b


def _sigmoid(x):
    return 1.0 / (1.0 + jnp.exp(-x))


def _rep(v8, rows):
    return v8 if rows == SUBLANES else jnp.concatenate([v8] * (rows // SUBLANES), axis=0)


def _silu(x):
    return x * _sigmoid(x)


def _bdot(a, b):
    return jnp.dot(a.astype(BF16), b, preferred_element_type=F32)


def _rot(t, cosf, sinf):
    return t * cosf + pltpu.roll(t, RET_DK // 2, axis=1) * sinf


def _lane_tile(cols, rows):
    lane = lax.broadcasted_iota(I32, (rows, LANES), 1)
    out = jnp.zeros((rows, LANES), F32)
    for i, col in enumerate(cols):
        out = jnp.where(lane == i, col, out)
    return out


def _route(logits):
    lane = lax.broadcasted_iota(I32, logits.shape, 1)
    lanef = lane.astype(F32)
    ninf = jnp.float32(-jnp.inf)
    big = jnp.float32(LANES)
    gmask = lane < N_GROUPS
    gl = jnp.where(gmask, logits, ninf)
    gmax = jnp.max(gl, axis=1, keepdims=True)
    gidx = jnp.min(jnp.where(gmask & (gl == gmax), lanef, big), axis=1, keepdims=True)
    sumexp = jnp.sum(jnp.where(gmask, jnp.exp(gl - gmax), 0.0), axis=1, keepdims=True)
    gw = 1.0 / sumexp
    lo = N_GROUPS + EXP_PER_GROUP * gidx
    emask = (lanef >= lo) & (lanef < lo + EXP_PER_GROUP)
    el = jnp.where(emask, logits, ninf)
    m1 = jnp.max(el, axis=1, keepdims=True)
    i1 = jnp.min(jnp.where(emask & (el == m1), lanef, big), axis=1, keepdims=True)
    emask2 = emask & (lanef != i1)
    el2 = jnp.where(emask2, logits, ninf)
    m2 = jnp.max(el2, axis=1, keepdims=True)
    i2 = jnp.min(jnp.where(emask2 & (el2 == m2), lanef, big), axis=1, keepdims=True)
    t = jnp.exp(m2 - m1)
    den = 1.0 + t
    return (1.0 / den) * gw, (t / den) * gw, i1 - N_GROUPS, i2 - N_GROUPS


def _post_mix_pieces(src, w, sink):
    (w_ret_o, cln_g, cln_b, w_conv_o, w_out, ln1_g, ln1_b, wr_hi, wr_lo, b_r) = w
    st = {}

    def branch_a():
        st["a"] = _bdot(_silu(src["g"]()) * src["ret"](), w_ret_o[...])

    def branch_b():
        st["b"] = _bdot(_silu(_ln(src["cout"](), cln_g[...], cln_b[...])), w_conv_o[...])

    def merge():
        mix = _sigmoid(src["gt_a"]()) * st["a"] + _sigmoid(src["gt_b"]()) * st["b"]
        h = ALPHA * src["x"]() + _bdot(mix, w_out[...])
        st["x1"] = _ln(h, ln1_g[...], ln1_b[...])

    def router():
        x1 = st["x1"]
        x1_hi = x1.astype(BF16)
        x1_lo = (x1 - x1_hi.astype(F32)).astype(BF16)
        both = jnp.dot(x1_hi, wr_lo[...], preferred_element_type=F32)
        st["logits"] = (both[:, :LANES]
                        + (jnp.dot(x1_lo, wr_hi[...], preferred_element_type=F32) + both[:, LANES:])
                        + b_r[...])

    def route():
        st["route"] = _route(st["logits"])

    def finish():
        sink(st["x1"], *st["route"])

    return [branch_a, branch_b, merge, router, route, finish]


def _sort_tile(x1, w1, w2, e1, e2, x1_ref, rw_ref, xs_ref, meta_ref):
    t = x1.shape[0]
    ids_t = _lane_tile((e1, e2), t).T
    e1r, e2r = ids_t[0:1, :], ids_t[1:2, :]
    sub = lax.broadcasted_iota(I32, (LANES, t), 0).astype(F32)
    a1 = (sub == e1r).astype(F32)
    a2 = (sub == e2r).astype(F32)
    ri = lax.broadcasted_iota(I32, (t, t), 0)
    ci = lax.broadcasted_iota(I32, (t, t), 1)
    earlier = (ri < ci).astype(BF16)
    r1 = jnp.dot(a1.astype(BF16), earlier, preferred_element_type=F32)
    r2 = jnp.dot(a2.astype(BF16), earlier, preferred_element_type=F32)
    cnt1 = jnp.sum(a1, axis=1, keepdims=True)
    cnt = cnt1 + jnp.sum(a2, axis=1, keepdims=True)
    nch = jnp.floor((cnt + (CHUNK - 1.0)) * (1.0 / CHUNK))
    ui = lax.broadcasted_iota(I32, (LANES, LANES), 0)
    uj = lax.broadcasted_iota(I32, (LANES, LANES), 1)
    before = (uj < ui).astype(BF16)
    off = jnp.dot(before, jnp.broadcast_to(nch, (LANES, LANES)).astype(BF16),
                  preferred_element_type=F32)[:, 0:1]
    base = off * CHUNK
    pos1r = jnp.sum(a1 * (base + r1), axis=0, keepdims=True)
    pos2r = jnp.sum(a2 * (base + cnt1 + r2), axis=0, keepdims=True)
    slot = lax.broadcasted_iota(I32, (USED_ROWS, t), 0).astype(F32)
    onehot = ((slot == pos1r) | (slot == pos2r)).astype(BF16)
    xs = jnp.dot(onehot, x1.astype(BF16), preferred_element_type=F32)
    pos_cols = jnp.where(sub == 2.0, pos1r, jnp.where(sub == 3.0, pos2r, 0.0)).T
    lane = lax.broadcasted_iota(I32, (t, LANES), 1)
    x1_ref[...] = x1
    rw_ref[...] = jnp.where(lane == 0, w1, jnp.where(lane == 1, w2, pos_cols))
    xs_ref[0:USED_ROWS, :] = xs.astype(BF16)
    xs_ref[USED_ROWS:CAP, :] = jnp.zeros((CAP - USED_ROWS, x1.shape[1]), BF16)
    mlane = lax.broadcasted_iota(I32, (LANES, LANES), 1)
    meta = jnp.where(mlane == 0, cnt, jnp.where(mlane == 1, off, 0.0))
    meta_ref[...] = meta.astype(I32)


def _sort_tiles(x1, w1, w2, e1, e2, x1_ref, rw_ref, xs_ref, meta_ref):
    for i in range(x1.shape[0] // TL):
        rows = slice(i * TL, (i + 1) * TL)
        _sort_tile(x1[rows], w1[rows], w2[rows], e1[rows], e2[rows],
                   x1_ref.at[pl.ds(i * TL, TL)], rw_ref.at[pl.ds(i * TL, TL)],
                   xs_ref.at[pl.ds(i * CAP, CAP)], meta_ref.at[pl.ds(i * LANES, LANES)])


GATE_COLS = {3: 0, 6: RET_V, 7: RET_V + D_MODEL}
QKV_COLS = {0: 0, 1: RET_QK, 2: 2 * RET_QK}


def _prompt_mixer_kernel(x_ref, x1s_ref, rws_ref, cos_ref, sin_ref, dec_ref, qdec_ref, kdec_ref, cdec_ref,
                         w_in, b_in, gn_g, gn_b, w_ret_o, conv_w, conv_b, cln_g, cln_b,
                         w_conv_o, w_out, ln1_g, ln1_b, wr_hi, wr_lo, b_r,
                         x1_ref, rw_ref, xs_ref, meta_ref, sret_ref, sconv_ref,
                         ubuf, ushift, qkv_scr, xb_scr, ret_scr, cout_scr, gate_scr,
                         *, n_tiles, tiles_per_seq):
    s = pl.program_id(0)
    li = lax.rem(s, tiles_per_seq)
    outs = (x1_ref, rw_ref, xs_ref, meta_ref)
    slot = dict(ret=ret_scr, cout=cout_scr, gates=gate_scr)
    tail_w = (w_ret_o, cln_g, cln_b, w_conv_o, w_out, ln1_g, ln1_b, wr_hi, wr_lo, b_r)

    @pl.when((s < n_tiles) & (li == 0))
    def _new_sequence():
        sret_ref[...] = jnp.zeros(sret_ref.shape, F32)
        ubuf[0:CONV_PAD, :] = jnp.zeros((CONV_PAD, CONV_CH), F32)

    @pl.when(s < n_tiles)
    def _mix():
        gcols = lambda kk: slice(GATE_COLS[kk], GATE_COLS[kk] + IN_WIDTHS[kk])
        src = dict(x=lambda: x_ref[0], ret=lambda: ret_scr[...], cout=lambda: cout_scr[...],
                   g=lambda: gate_scr[:, gcols(3)], gt_a=lambda: gate_scr[:, gcols(6)],
                   gt_b=lambda: gate_scr[:, gcols(7)])
        head = _prompt_head_pieces(x_ref, cos_ref, sin_ref, dec_ref, qdec_ref, kdec_ref, cdec_ref,
                                   w_in, b_in, gn_g, gn_b, conv_w, conv_b, sret_ref,
                                   ubuf, ushift, qkv_scr, xb_scr, slot)
        tail = _post_mix_pieces(src, tail_w, lambda *r: _sort_tiles(*r, *outs))
        for piece in head + tail:
            piece()

    @pl.when(s >= n_tiles)
    def _append():
        rws = rws_ref[...]
        _sort_tiles(x1s_ref[...], rws[:, 0:1], rws[:, 1:2], rws[:, 2:3], rws[:, 3:4], *outs)

    @pl.when((s < n_tiles) & (li == tiles_per_seq - 1))
    def _conv_state():
        sconv_ref[0, 0] = ubuf[CONV_PAD - (CONV_WIDTH - 1):CONV_PAD, :]


def _prompt_head_pieces(x_ref, cos_ref, sin_ref, dec_ref, qdec_ref, kdec_ref, cdec_ref,
                        w_in, b_in, gn_g, gn_b, conv_w, conv_b, sret_ref,
                        ubuf, ushift, qkv_scr, xb_scr, slot):
    tl = x_ref.shape[1]
    st = {}

    def slab_dot(c0, c1):
        return jnp.dot(xb_scr[...], w_in[:, c0:c1], preferred_element_type=F32) + _rep(b_in[:, c0:c1], tl)

    def glu():
        xb_scr[...] = x_ref[0].astype(BF16)
        u = slab_dot(IN_OFFS[4], IN_OFFS[5]) * _sigmoid(slab_dot(IN_OFFS[5], IN_OFFS[6]))
        ubuf[CONV_PAD:CONV_PAD + tl, :] = u

    nsh = ushift.shape[1]
    span = nsh - (CONV_PAD - SUBLANES)

    def shift_copy(h, s):
        ushift[s - 1] = ubuf[h * span + s:h * span + s + nsh, :]

    slab = 256
    slabs = [(kk, c0) for kk in (0, 1, 2, 3, 6, 7) for c0 in range(IN_OFFS[kk], IN_OFFS[kk + 1], slab)]
    rb = 32
    nrb = tl // rb

    def proj_slab(kk, c0):
        val = slab_dot(c0, c0 + slab)
        if kk in QKV_COLS:
            dst = QKV_COLS[kk] + c0 - IN_OFFS[kk]
            qkv_scr[:, dst:dst + slab] = val
        else:
            dst = GATE_COLS[kk] + c0 - IN_OFFS[kk]
            slot["gates"][:, dst:dst + slab] = val

    def conv_block(r):
        h, rl = divmod(r * rb, span)
        acc = jnp.zeros((rb, CONV_CH), F32) + conv_b[...]
        for j in range(CONV_WIDTH):
            off = j + (CONV_PAD - (CONV_WIDTH - 1))
            s = off % SUBLANES
            base = rl + off - s
            win = (ubuf[h * span + base:h * span + base + rb, :] if s == 0
                   else ushift[s - 1, base:base + rb, :])
            acc = acc + _rep(conv_w[j], rb) * win
        slot["cout"][r * rb:(r + 1) * rb, :] = acc
        if r == nrb - 1:
            ubuf[0:CONV_PAD, :] = ubuf[tl:tl + CONV_PAD, :]

    scale = RET_DK ** -0.5

    def retention(c, h):
        rows = slice(c * RET_CHUNK, (c + 1) * RET_CHUNK)
        cols = slice(h * RET_DK, (h + 1) * RET_DK)
        hcol = lambda kk: slice(QKV_COLS[kk] + h * RET_DK, QKV_COLS[kk] + (h + 1) * RET_DK)
        cosf = cos_ref[rows, :]
        sinf = sin_ref[rows, :]
        qh = _rot(qkv_scr[rows, hcol(0)], cosf, sinf)
        kh = _rot(qkv_scr[rows, hcol(1)], cosf, sinf) * scale
        qb = qh.astype(BF16)
        kb = kh.astype(BF16)
        vb = qkv_scr[rows, hcol(2)].astype(BF16)
        s_old = sret_ref[0, 0, h]
        scores = lax.dot_general(qb, kb, (((1,), (1,)), ((), ())),
                                 preferred_element_type=F32) * dec_ref[h]
        inner = jnp.dot(scores.astype(BF16), vb, preferred_element_type=F32)
        cross = jnp.dot(qb, s_old.astype(BF16), preferred_element_type=F32) * qdec_ref[h]
        kd = (kh * kdec_ref[h]).astype(BF16)
        s_new = cdec_ref[h] * s_old + lax.dot_general(
            kd, vb, (((0,), (0,)), ((), ())), preferred_element_type=F32)
        sret_ref[0, 0, h] = s_new
        slot["ret"][rows, cols] = _ln(inner + cross, gn_g[:, cols], gn_b[:, cols])

    vector_pieces = []
    for r in range(nrb):
        if (r * rb) % span == 0:
            vector_pieces += [lambda h=(r * rb) // span, s=s: shift_copy(h, s) for s in range(1, SUBLANES)]
        vector_pieces.append(lambda r=r: conv_block(r))
    pieces = [glu]
    for i, piece in enumerate(vector_pieces):
        pieces.append(piece)
        for kk, c0 in slabs[i * len(slabs) // len(vector_pieces):(i + 1) * len(slabs) // len(vector_pieces)]:
            pieces.append(lambda kk=kk, c0=c0: proj_slab(kk, c0))
    pieces += [lambda c=c, h=h: retention(c, h) for c in range(tl // RET_CHUNK) for h in range(RET_HEADS)]
    return pieces


def _sample_mixer_kernel(x_ref, cos_ref, sin_ref, pdec_ref, qdec_ref, kdec_ref, cdec_ref, conv_w,
                         sret_in, sconv_in,
                         w_in, b_in, gn_g, gn_b, w_ret_o, conv_b, cln_g, cln_b,
                         w_conv_o, w_out, ln1_g, ln1_b, wr_hi, wr_lo, b_r,
                         x1_ref, rw_ref, sret_ref, sconv_ref,
                         ret_scr, cout_scr, gate_scr, xpad, wsh):
    i = pl.program_id(0)
    t = cos_ref.shape[0]
    ls = t // BB_SAMPLE
    nstate = CONV_WIDTH - 1
    r0 = pl.multiple_of(i * t, t)
    xb = x_ref[pl.ds(r0, t), :].astype(BF16)

    @pl.when(i == 0)
    def _tap_tables():
        wsh[...] = jnp.zeros(wsh.shape, F32)
        for p in range(ls):
            wsh[p, p:nstate, :] = conv_w[0:nstate - p, :]
            wsh[p, XPAD_NEW:XPAD_NEW + p + 1, :] = conv_w[nstate - p:CONV_WIDTH, :]

    def proj(k):
        c0, c1 = IN_OFFS[k], IN_OFFS[k + 1]
        return jnp.dot(xb, w_in[:, c0:c1], preferred_element_type=F32) + _rep(b_in[:, c0:c1], t)

    q = proj(0)
    k = proj(1)
    v = proj(2)
    scale = RET_DK ** -0.5
    cosf = cos_ref[...]
    sinf = sin_ref[...]
    row = lax.broadcasted_iota(I32, (t, RET_DK), 0)
    pos = row % ls
    row8 = lax.broadcasted_iota(I32, (SUBLANES, RET_DK), 0)
    per_tile = SUBLANES // ls
    for h in range(RET_HEADS):
        cols = slice(h * RET_DK, (h + 1) * RET_DK)
        qh = _rot(q[:, cols], cosf, sinf)
        kh = _rot(k[:, cols], cosf, sinf) * scale
        vh = v[:, cols]
        inner = jnp.zeros((t, RET_DV), F32)
        for s in range(ls):
            ks = kh if s == 0 else pltpu.roll(kh, s, axis=0)
            vs = vh if s == 0 else pltpu.roll(vh, s, axis=0)
            dotp = jnp.sum(qh * ks, axis=1, keepdims=True) * pdec_ref[h, s]
            inner = inner + jnp.where(pos >= s, dotp, 0.0) * vs
        kd = kh * kdec_ref[h]
        for tile in range(t // SUBLANES):
            rows = slice(tile * SUBLANES, (tile + 1) * SUBLANES)
            q8 = qh[rows, :]
            kd8 = kd[rows, :]
            v8 = vh[rows, :]
            seqs = [tile * per_tile + sub for sub in range(per_tile)]
            mine = [(row8 >= sub * ls) & (row8 < (sub + 1) * ls) for sub in range(per_tile)]
            s_old = [sret_in[0, b, h] for b in seqs]
            c_all = jnp.dot(q8, jnp.concatenate(s_old, axis=1), preferred_element_type=F32)
            upd = lax.dot_general(jnp.concatenate([jnp.where(m, kd8, 0.0) for m in mine], axis=1), v8,
                                  (((0,), (0,)), ((), ())), preferred_element_type=F32)
            cross8 = jnp.zeros((SUBLANES, RET_DV), F32)
            for sub, b in enumerate(seqs):
                cross8 = jnp.where(mine[sub], c_all[:, sub * RET_DV:(sub + 1) * RET_DV], cross8)
                sret_ref[0, b, h] = cdec_ref[h] * s_old[sub] + upd[sub * RET_DK:(sub + 1) * RET_DK, :]
            ret_scr[pl.ds(r0 + tile * SUBLANES, SUBLANES), cols] = (
                inner[rows, :] + cross8 * qdec_ref[h, rows, :])
        ret_scr[pl.ds(r0, t), cols] = _ln(ret_scr[pl.ds(r0, t), cols], gn_g[:, cols], gn_b[:, cols])

    u = proj(4) * _sigmoid(proj(5))
    xpad[...] = jnp.zeros(xpad.shape, F32)
    xpad[:, 0:nstate, :] = sconv_in[0]
    for b in range(BB_SAMPLE):
        xpad[b, XPAD_NEW:XPAD_NEW + ls, :] = u[b * ls:(b + 1) * ls, :]
    for p in range(ls):
        res = jnp.sum(xpad[...] * wsh[p][None], axis=1) + conv_b[...]
        for sl in range(CONV_CH // LANES):
            cout_scr[sl, pl.ds(r0 + p, BB_SAMPLE, stride=ls), :] = res[:, sl * LANES:(sl + 1) * LANES]
    sconv_ref[0, :, 0:nstate - ls, :] = xpad[:, ls:nstate, :]
    sconv_ref[0, :, nstate - ls:nstate, :] = xpad[:, XPAD_NEW:XPAD_NEW + ls, :]
    for kk in (3, 6, 7):
        gate_scr[pl.ds(r0, t), GATE_COLS[kk]:GATE_COLS[kk] + IN_WIDTHS[kk]] = proj(kk)

    @pl.when(i == pl.num_programs(0) - 1)
    def _second_half():
        n = x_ref.shape[0]

        def sink(x1, w1, w2, e1, e2):
            x1_ref[...] = x1
            rw_ref[...] = _lane_tile((w1, w2, e1, e2), n)

        gcols = lambda kk: slice(GATE_COLS[kk], GATE_COLS[kk] + IN_WIDTHS[kk])
        src = dict(x=lambda: x_ref[...], ret=lambda: ret_scr[...],
                   cout=lambda: jnp.concatenate([cout_scr[sl] for sl in range(CONV_CH // LANES)], axis=1),
                   g=lambda: gate_scr[:, gcols(3)], gt_a=lambda: gate_scr[:, gcols(6)],
                   gt_b=lambda: gate_scr[:, gcols(7)])
        for piece in _post_mix_pieces(
                src, (w_ret_o, cln_g, cln_b, w_conv_o, w_out, ln1_g, ln1_b, wr_hi, wr_lo, b_r), sink):
            piece()


def _ffn_kernel(te_ref, nvalid_ref, chunk_ref, xs_hbm, w_gu, w_dn, ys_hbm,
                xbuf, obuf, wgu_b, wdn_b, sem_in, sem_out):
    del xs_hbm
    i = pl.program_id(0)
    n = pl.num_programs(0)
    slot = i % 2
    nvalid = nvalid_ref[0]

    def chunk_rows(tile, c):
        return pl.ds(pl.multiple_of(chunk_ref[tile * TILE_CHUNKS + c] * CHUNK, CHUNK), CHUNK)

    def start_in(tile, s):
        for c in range(TILE_CHUNKS):
            pltpu.make_async_copy(ys_hbm.at[chunk_rows(tile, c)],
                                  xbuf.at[s, pl.ds(c * CHUNK, CHUNK)], sem_in.at[s]).start()

    def start_out(tile, s):
        for c in range(TILE_CHUNKS):
            pltpu.make_async_copy(obuf.at[s, pl.ds(c * CHUNK, CHUNK)],
                                  ys_hbm.at[chunk_rows(tile, c)], sem_out.at[s]).start()

    def wait_in(s):
        pltpu.make_async_copy(ys_hbm.at[pl.ds(0, TM_FFN)], xbuf.at[s], sem_in.at[s]).wait()

    def wait_out(s):
        pltpu.make_async_copy(obuf.at[s], ys_hbm.at[pl.ds(0, TM_FFN)], sem_out.at[s]).wait()

    @pl.when((i == 0) & (nvalid > 0))
    def _first():
        start_in(0, 0)

    @pl.when((i >= 2) & (i - 2 < nvalid))
    def _retire():
        wait_out(slot)

    @pl.when(i < nvalid)
    def _tile():
        wait_in(slot)
        prev = te_ref[jnp.maximum(i - 1, 0)]

        @pl.when((i == 0) | (te_ref[i] != prev))
        def _new_expert():
            wgu_b[...] = w_gu[0].astype(BF16)
            wdn_b[...] = w_dn[0].astype(BF16)

        x = xbuf[slot]
        y = jnp.zeros((TM_FFN, w_dn.shape[2]), F32)
        for c0 in range(0, EXP_FF, FFN_COLS):
            hg = jnp.dot(x, wgu_b[:, c0:c0 + FFN_COLS], preferred_element_type=F32)
            hu = jnp.dot(x, wgu_b[:, EXP_FF + c0:EXP_FF + c0 + FFN_COLS], preferred_element_type=F32)
            y = y + _bdot(_silu(hg) * hu, wdn_b[c0:c0 + FFN_COLS, :])
            if c0 == 0:
                start_in(jnp.where(i + 1 < nvalid, i + 1, 0), 1 - slot)
        obuf[slot] = y.astype(BF16)
        start_out(i, slot)

    @pl.when(i == n - 1)
    def _drain():
        @pl.when((i >= 1) & (i - 1 < nvalid))
        def _():
            wait_out(1 - slot)

        @pl.when(i < nvalid)
        def _():
            wait_out(slot)

        @pl.when(nvalid > 0)
        def _():
            wait_in(nvalid % 2)


def _final_kernel(ys_ref, x1_ref, rw_ref, pp_ref, ps_ref, ln2_g, ln2_b, w_pg, b_pg, w_ple,
                  yp_ref, ys_out_ref, *, n_prompt_tiles):
    i = pl.program_id(0)
    x1 = x1_ref[...]
    slot = lax.broadcasted_iota(I32, (TL, USED_ROWS), 1).astype(F32)
    parts = []
    for b in range(x1.shape[0] // TL):
        rw = rw_ref[b * TL:(b + 1) * TL, :]
        w1, w2, pos1, pos2 = rw[:, 0:1], rw[:, 1:2], rw[:, 2:3], rw[:, 3:4]
        ys = ys_ref[b * CAP:b * CAP + USED_ROWS, :]
        comb = jnp.where(slot == pos1, w1, jnp.where(slot == pos2, w2, 0.0)).astype(BF16)
        parts.append(jnp.dot(comb, ys, preferred_element_type=F32))
    moe = parts[0] if len(parts) == 1 else jnp.concatenate(parts, axis=0)
    x2 = _ln(ALPHA * x1 + moe, ln2_g[...], ln2_b[...])
    gate = _sigmoid(_bdot(x2, w_pg[...]) + b_pg[...])
    p = jnp.where(i < n_prompt_tiles, pp_ref[...], ps_ref[...])
    y = x2 + gate * _bdot(p, w_ple[...])

    @pl.when(i < n_prompt_tiles)
    def _prompt():
        yp_ref[...] = y

    @pl.when(i >= n_prompt_tiles)
    def _sample():
        ys_out_ref[...] = y


def _rope_tables(pos):
    half = RET_DK // 2
    inv_freq = ROPE_BASE ** (-np.arange(half, dtype=np.float64) / half)
    ang = np.asarray(pos, np.float64)[:, None] * inv_freq[None, :]
    cos = np.cos(ang)
    sin = np.sin(ang)
    return (np.concatenate([cos, cos], axis=-1).astype(np.float32),
            np.concatenate([-sin, sin], axis=-1).astype(np.float32))


def _log_gamma():
    return np.log(1.0 - 2.0 ** (-5.0 - np.arange(RET_HEADS, dtype=np.float64)))


def _const_spec(shape):
    nd = len(shape)
    return pl.BlockSpec(shape, lambda *_: (0,) * nd, pipeline_mode=pl.Buffered(1))


def _chunk_plan(meta, n_blocks, n_ffn_tiles):
    assert n_blocks * BLOCK_SPARE >= N_EXPERTS * (TILE_CHUNKS - 1)
    m = meta.reshape(n_blocks, LANES, LANES)
    cnt = m[:, :N_EXPERTS, 0]
    off = m[:, :N_EXPERTS, 1]
    nch = (cnt + (CHUNK - 1)) // CHUNK
    cum = jnp.cumsum(nch, axis=0)
    total = cum[-1:]
    tiles_e = (total + TILE_CHUNKS - 1) // TILE_CHUNKS
    tile_end = jnp.cumsum(tiles_e, axis=1)
    tile_start = tile_end - tiles_e
    tid = jnp.arange(n_ffn_tiles, dtype=I32)[:, None]
    owner = (tid >= tile_start) & (tid < tile_end)
    pick_e = lambda v: jnp.sum(jnp.where(owner, v, 0), axis=1, keepdims=True)
    te = pick_e(jnp.arange(N_EXPERTS, dtype=I32)[None, :])
    k = (tid - pick_e(tile_start)) * TILE_CHUNKS + jnp.arange(TILE_CHUNKS, dtype=I32)[None, :]
    total_t = pick_e(total)
    real = k < total_t
    by_tile = lambda v: jnp.sum(jnp.where(owner[:, None, :], v[None, :, :], 0), axis=2)
    cum_t = by_tile(cum)
    blk = jnp.minimum(jnp.sum((cum_t[:, None, :] <= k[:, :, None]).astype(I32), axis=2), n_blocks - 1)
    at_blk = blk[:, :, None] == jnp.arange(n_blocks, dtype=I32)[None, None, :]
    pick_b = lambda v: jnp.sum(jnp.where(at_blk, v[:, None, :], 0), axis=2)
    excl = pick_b(cum_t - by_tile(nch))
    off_t = pick_b(by_tile(off))
    spare = te * (TILE_CHUNKS - 1) + jnp.maximum(k - total_t, 0) % TILE_CHUNKS
    spare_chunk = (spare // BLOCK_SPARE) * BLOCK_CHUNKS + BLOCK_USED + spare % BLOCK_SPARE
    chunk = jnp.where(real, blk * BLOCK_CHUNKS + off_t + (k - excl), spare_chunk)
    n_valid = jnp.sum(tiles_e, axis=1)
    te = jnp.where(tid < n_valid, te, N_EXPERTS - 1)
    return te.reshape(-1).astype(I32), n_valid.astype(I32), chunk.reshape(-1).astype(I32)


def kernel(x_prompt, x_sample, state_ret, state_conv, p_prompt, p_sample, w_in, b_in, ret_gn_g, ret_gn_b,
           w_ret_o, conv_w, conv_b, conv_ln_g, conv_ln_b, w_conv_o, w_out, ln1_g, ln1_b, w_grp, b_grp,
           w_exp, b_exp, w_gu, w_dn, ln2_g, ln2_b, w_pg, b_pg, w_ple):
    assert DEPTH == 1 and w_in.shape[0] == 1
    bp, lp, d = x_prompt.shape
    bs, ls, _ = x_sample.shape
    n_p, n_s = bp * lp, bs * ls
    n_tok = n_p + n_s
    assert lp % TL == 0 and n_s % TL == 0 and bs % BB_SAMPLE == 0 and SUBLANES % ls == 0
    n_blocks = n_tok // TL

    f32c = lambda a, shape: jnp.asarray(np.broadcast_to(a, shape).astype(np.float32))
    lg = _log_gamma()
    c = RET_CHUNK
    idx = np.arange(c, dtype=np.float64)
    rel = idx[:, None] - idx[None, :]
    causal = rel >= 0
    decay = np.where(causal[None], np.exp(np.where(causal, rel, 0.0)[None] * lg[:, None, None]), 0.0)
    decay = f32c(decay, decay.shape)
    q_decay = np.exp((idx[:, None] + 1.0) * lg[None, :])
    k_decay = np.exp((c - 1.0 - idx[:, None]) * lg[None, :])
    chunk_decay = np.exp(c * lg)
    qdec_p = f32c(q_decay.T[:, :, None], (RET_HEADS, c, RET_DK))
    kdec_p = f32c(k_decay.T[:, :, None], (RET_HEADS, c, RET_DK))
    cdec_p = f32c(chunk_decay[:, None, None], (RET_HEADS, 1, RET_DV))
    cos_p, sin_p = (jnp.asarray(a) for a in _rope_tables(np.arange(lp)))

    ts = BB_SAMPLE * ls
    idx_s = np.arange(ls, dtype=np.float64)
    pdec_s = np.exp(idx_s[None, :] * lg[:, None])
    pdec_s = f32c(pdec_s[:, :, None, None], (RET_HEADS, ls, 1, RET_DK))
    qd_s = np.exp((idx_s[:, None] + 1.0) * lg[None, :])
    kd_s = np.exp((ls - 1.0 - idx_s[:, None]) * lg[None, :])
    qdec_s = f32c(np.tile(qd_s.T, (1, BB_SAMPLE))[:, :, None], (RET_HEADS, ts, RET_DK))
    kdec_s = f32c(np.tile(kd_s.T, (1, BB_SAMPLE))[:, :, None], (RET_HEADS, ts, RET_DK))
    cdec_s = f32c(np.exp(ls * lg)[:, None, None], (RET_HEADS, 1, RET_DV))
    cos_s, sin_s = (jnp.asarray(a) for a in _rope_tables(np.tile(PAST_LEN + np.arange(ls), BB_SAMPLE)))

    w_in_b = w_in[0].astype(BF16)
    w_ret_o_b = w_ret_o[0].astype(BF16)
    w_conv_o_b = w_conv_o[0].astype(BF16)
    w_out_b = w_out[0].astype(BF16)
    w_pg_b = w_pg[0].astype(BF16)
    w_ple_b = w_ple[0].astype(BF16)
    n_route = N_GROUPS + N_EXPERTS
    w_r = jnp.concatenate([w_grp[0], w_exp[0], jnp.zeros((d, LANES - n_route), F32)], axis=1)
    wr_hi = w_r.astype(BF16)
    wr_lo = jnp.concatenate([wr_hi, (w_r - wr_hi.astype(F32)).astype(BF16)], axis=1)
    b_r = jnp.concatenate([b_grp[0], b_exp[0], jnp.zeros((LANES - n_route,), F32)]).reshape(1, LANES)
    row = lambda a: a.reshape(1, -1)
    conv_w0 = conv_w[0]
    nstate = CONV_WIDTH - 1

    rep8 = lambda a: jnp.broadcast_to(a[..., None, :], a.shape[:-1] + (SUBLANES, a.shape[-1]))
    shared_w = (w_in_b, rep8(b_in[0]), row(ret_gn_g[0]), row(ret_gn_b[0]), w_ret_o_b)
    tail_w = (row(conv_ln_g[0]), row(conv_ln_b[0]), w_conv_o_b, w_out_b, row(ln1_g[0]), row(ln1_b[0]),
              wr_hi, wr_lo, b_r)

    nbt = bs // BB_SAMPLE
    xs2 = x_sample.reshape(n_s, d)
    sample_in = ((xs2, cos_s, sin_s, pdec_s, qdec_s, kdec_s, cdec_s, conv_w0, state_ret, state_conv)
                 + shared_w + (row(conv_b[0]),) + tail_w)
    sample_specs = (
        [_const_spec(a.shape) for a in sample_in[0:8]]
        + [pl.BlockSpec((1, BB_SAMPLE, RET_HEADS, RET_DK, RET_DV), lambda i: (0, i, 0, 0, 0)),
           pl.BlockSpec((1, BB_SAMPLE, nstate, CONV_CH), lambda i: (0, i, 0, 0))]
        + [_const_spec(a.shape) for a in sample_in[10:]]
    )
    tok_spec_s = lambda w: pl.BlockSpec((n_s, w), lambda i: (0, 0))
    x1_s, rw_s, ret_s, conv_s = pl.pallas_call(
        _sample_mixer_kernel,
        grid=(nbt,),
        in_specs=sample_specs,
        out_specs=[
            tok_spec_s(d), tok_spec_s(LANES),
            pl.BlockSpec((1, BB_SAMPLE, RET_HEADS, RET_DK, RET_DV), lambda i: (0, i, 0, 0, 0)),
            pl.BlockSpec((1, BB_SAMPLE, nstate, CONV_CH), lambda i: (0, i, 0, 0)),
        ],
        out_shape=[
            jax.ShapeDtypeStruct((n_s, d), F32),
            jax.ShapeDtypeStruct((n_s, LANES), F32),
            jax.ShapeDtypeStruct(state_ret.shape, F32),
            jax.ShapeDtypeStruct(state_conv.shape, F32),
        ],
        scratch_shapes=[
            pltpu.VMEM((n_s, RET_V), F32),
            pltpu.VMEM((CONV_CH // LANES, n_s, LANES), F32),
            pltpu.VMEM((n_s, RET_V + 2 * D_MODEL), F32),
            pltpu.VMEM((BB_SAMPLE, XPAD_ROWS, CONV_CH), F32),
            pltpu.VMEM((ls, XPAD_ROWS, CONV_CH), F32),
        ],
        compiler_params=pltpu.CompilerParams(
            dimension_semantics=("arbitrary",), vmem_limit_bytes=VMEM_LIMIT),
        name="sample_mixer",
    )(*sample_in)

    assert lp % TLM == 0 and n_s % TLM == 0 and TLM % TL == 0
    nlt = lp // TLM
    npt = n_p // TLM
    nst = n_s // TLM
    sub = TLM // TL
    prompt_in = ((x_prompt, x1_s, rw_s, cos_p, sin_p, decay, qdec_p, kdec_p, cdec_p)
                 + shared_w + (rep8(conv_w0), row(conv_b[0])) + tail_w)
    head_tile = lambda s: jnp.minimum(s, npt - 1)
    sample_tile = lambda s: jnp.maximum(s - npt, 0)
    sample_spec = lambda w: pl.BlockSpec((TLM, w), lambda s: (sample_tile(s), 0))
    prompt_specs = [
        pl.BlockSpec((1, TLM, d), lambda s: (head_tile(s) // nlt, head_tile(s) % nlt, 0)),
        sample_spec(d), sample_spec(LANES),
        pl.BlockSpec((TLM, RET_DK), lambda s: (head_tile(s) % nlt, 0)),
        pl.BlockSpec((TLM, RET_DK), lambda s: (head_tile(s) % nlt, 0)),
    ] + [_const_spec(a.shape) for a in prompt_in[5:]]
    tok_spec_p = lambda rows, w: pl.BlockSpec((rows, w), lambda s: (s, 0))
    x1_all, rw_all, xs_all, meta, ret_p, conv_p = pl.pallas_call(
        functools.partial(_prompt_mixer_kernel, n_tiles=npt, tiles_per_seq=nlt),
        grid=(npt + nst,),
        in_specs=prompt_specs,
        out_specs=[
            tok_spec_p(TLM, d), tok_spec_p(TLM, LANES), tok_spec_p(sub * CAP, d), tok_spec_p(sub * LANES, LANES),
            pl.BlockSpec((1, 1, RET_HEADS, RET_DK, RET_DV), lambda s: (0, head_tile(s) // nlt, 0, 0, 0)),
            pl.BlockSpec((1, 1, nstate, CONV_CH), lambda s: (0, head_tile(s) // nlt, 0, 0)),
        ],
        out_shape=[
            jax.ShapeDtypeStruct((n_tok, d), F32),
            jax.ShapeDtypeStruct((n_tok, LANES), F32),
            jax.ShapeDtypeStruct((n_blocks * CAP, d), BF16),
            jax.ShapeDtypeStruct((n_blocks * LANES, LANES), I32),
            jax.ShapeDtypeStruct((1, bp, RET_HEADS, RET_DK, RET_DV), F32),
            jax.ShapeDtypeStruct((1, bp, nstate, CONV_CH), F32),
        ],
        scratch_shapes=[
            pltpu.VMEM((TLM + CONV_PAD, CONV_CH), F32),
            pltpu.VMEM((SUBLANES - 1, TL + CONV_PAD - SUBLANES, CONV_CH), F32),
            pltpu.VMEM((TLM, 2 * RET_QK + RET_V), F32),
            pltpu.VMEM((TLM, d), BF16),
            pltpu.VMEM((TLM, RET_V), F32),
            pltpu.VMEM((TLM, CONV_CH), F32),
            pltpu.VMEM((TLM, RET_V + 2 * D_MODEL), F32),
        ],
        compiler_params=pltpu.CompilerParams(
            dimension_semantics=("arbitrary",), vmem_limit_bytes=VMEM_LIMIT),
        name="prompt_mixer",
    )(*prompt_in)

    assert TOP_K * n_tok // TM_FFN >= 3
    max_chunks = n_blocks * (TOP_K * TL // CHUNK + N_EXPERTS - 1)
    n_ffn_tiles = (max_chunks + N_EXPERTS * (TILE_CHUNKS - 1)) // TILE_CHUNKS
    tile_e, n_valid_tiles, chunk_ids = _chunk_plan(meta, n_blocks, n_ffn_tiles)

    ys_all = pl.pallas_call(
        _ffn_kernel,
        grid_spec=pltpu.PrefetchScalarGridSpec(
            num_scalar_prefetch=3,
            grid=(n_ffn_tiles,),
            in_specs=[
                pl.BlockSpec(memory_space=pl.ANY),
                pl.BlockSpec((1, d, 2 * EXP_FF), lambda i, te, nr, ch: (te[i], 0, 0)),
                pl.BlockSpec((1, EXP_FF, d), lambda i, te, nr, ch: (te[i], 0, 0)),
            ],
            out_specs=pl.BlockSpec(memory_space=pl.ANY),
            scratch_shapes=[
                pltpu.VMEM((2, TM_FFN, d), BF16),
                pltpu.VMEM((2, TM_FFN, d), BF16),
                pltpu.VMEM((d, 2 * EXP_FF), BF16),
                pltpu.VMEM((EXP_FF, d), BF16),
                pltpu.SemaphoreType.DMA((2,)),
                pltpu.SemaphoreType.DMA((2,)),
            ],
        ),
        out_shape=jax.ShapeDtypeStruct(xs_all.shape, BF16),
        input_output_aliases={3: 0},
        compiler_params=pltpu.CompilerParams(
            dimension_semantics=("arbitrary",), vmem_limit_bytes=VMEM_LIMIT),
        name="expert_ffn",
    )(tile_e, n_valid_tiles, chunk_ids, xs_all, w_gu[0], w_dn[0])

    assert n_p % TLF == 0 and n_s % TLF == 0
    npt = n_p // TLF
    fsub = TLF // TL
    pp2 = p_prompt.reshape(n_p, PLE_DIM)
    ps2 = p_sample.reshape(n_s, PLE_DIM)
    tok_f = lambda rows, w: pl.BlockSpec((rows, w), lambda i: (i, 0))
    y_p, y_s = pl.pallas_call(
        functools.partial(_final_kernel, n_prompt_tiles=npt),
        grid=(n_tok // TLF,),
        in_specs=[
            tok_f(fsub * CAP, d), tok_f(TLF, d), tok_f(TLF, LANES),
            pl.BlockSpec((TLF, PLE_DIM), lambda i: (jnp.minimum(i, npt - 1), 0)),
            pl.BlockSpec((TLF, PLE_DIM), lambda i: (jnp.maximum(i - npt, 0), 0)),
            _const_spec((1, d)), _const_spec((1, d)), _const_spec((d, d)), _const_spec((1, d)),
            _const_spec((PLE_DIM, d)),
        ],
        out_specs=[
            pl.BlockSpec((TLF, d), lambda i: (jnp.minimum(i, npt - 1), 0)),
            pl.BlockSpec((TLF, d), lambda i: (jnp.maximum(i - npt, 0), 0)),
        ],
        out_shape=[jax.ShapeDtypeStruct((n_p, d), F32), jax.ShapeDtypeStruct((n_s, d), F32)],
        compiler_params=pltpu.CompilerParams(
            dimension_semantics=("arbitrary",), vmem_limit_bytes=VMEM_LIMIT),
        name="moe_combine_final",
    )(ys_all, x1_all, rw_all, pp2, ps2,
      row(ln2_g[0]), row(ln2_b[0]), w_pg_b, row(b_pg[0]), w_ple_b)

    return (y_p.reshape(bp, lp, d), y_s.reshape(bs, ls, d), ret_p, conv_p, ret_s, conv_s)
```

```python
import functools

import jax
import jax.numpy as jnp
import numpy as np
from jax import lax
from jax.experimental import pallas as pl
from jax.experimental.pallas import tpu as pltpu

F32 = jnp.float32
BF16 = jnp.bfloat16
I32 = jnp.int32

D_MODEL = 1024
PAST_LEN = 16384
RET_HEADS = 4
RET_DK = 128
RET_DV = 128
RET_QK = RET_HEADS * RET_DK
RET_V = RET_HEADS * RET_DV
RET_CHUNK = 128
ROPE_BASE = 10000.0
CONV_CH = 512
CONV_WIDTH = 31
N_GROUPS = 4
EXP_PER_GROUP = 4
N_EXPERTS = N_GROUPS * EXP_PER_GROUP
TOP_K = 2
EXP_FF = 512
PLE_DIM = 256
DEPTH = 1
ALPHA = (2 * DEPTH) ** 0.25
LN_EPS = 1e-5
IN_WIDTHS = (RET_QK, RET_QK, RET_V, RET_V, CONV_CH, CONV_CH, D_MODEL, D_MODEL)
IN_OFFS = tuple(int(s) for s in np.cumsum((0,) + IN_WIDTHS))

LANES = 128
SUBLANES = 8
VMEM_LIMIT = 56 * 1024 * 1024

TL = 256
TLM = 512
TLF = 512
BB_SAMPLE = 16
CHUNK = 2 * SUBLANES
TILE_CHUNKS = 32
BLOCK_USED = -(-(TOP_K * TL + N_EXPERTS * (CHUNK - 1)) // LANES) * LANES // CHUNK
BLOCK_SPARE = LANES // CHUNK
BLOCK_CHUNKS = BLOCK_USED + BLOCK_SPARE
USED_ROWS = BLOCK_USED * CHUNK
CAP = BLOCK_CHUNKS * CHUNK
TM_FFN = TILE_CHUNKS * CHUNK
FFN_COLS = 256
CONV_PAD = 32
XPAD_NEW = 32
XPAD_ROWS = 40


def _ln(x, g, b):
    mu = jnp.mean(x, axis=-1, keepdims=True)
    d = x - mu
    var = jnp.mean(d * d, axis=-1, keepdims=True)
    return d * lax.rsqrt(var + LN_EPS) * g + b


def _sigmoid(x):
    return 1.0 / (1.0 + jnp.exp(-x))


def _rep(v8, rows):
    return v8 if rows == SUBLANES else jnp.concatenate([v8] * (rows // SUBLANES), axis=0)


def _silu(x):
    return x * _sigmoid(x)


def _bdot(a, b):
    return jnp.dot(a.astype(BF16), b, preferred_element_type=F32)


def _rot(t, cosf, sinf):
    return t * cosf + pltpu.roll(t, RET_DK // 2, axis=1) * sinf


def _lane_tile(cols, rows):
    lane = lax.broadcasted_iota(I32, (rows, LANES), 1)
    out = jnp.zeros((rows, LANES), F32)
    for i, col in enumerate(cols):
        out = jnp.where(lane == i, col, out)
    return out


def _route(logits):
    lane = lax.broadcasted_iota(I32, logits.shape, 1)
    lanef = lane.astype(F32)
    ninf = jnp.float32(-jnp.inf)
    big = jnp.float32(LANES)
    gmask = lane < N_GROUPS
    gl = jnp.where(gmask, logits, ninf)
    gmax = jnp.max(gl, axis=1, keepdims=True)
    gidx = jnp.min(jnp.where(gmask & (gl == gmax), lanef, big), axis=1, keepdims=True)
    sumexp = jnp.sum(jnp.where(gmask, jnp.exp(gl - gmax), 0.0), axis=1, keepdims=True)
    gw = 1.0 / sumexp
    lo = N_GROUPS + EXP_PER_GROUP * gidx
    emask = (lanef >= lo) & (lanef < lo + EXP_PER_GROUP)
    el = jnp.where(emask, logits, ninf)
    m1 = jnp.max(el, axis=1, keepdims=True)
    i1 = jnp.min(jnp.where(emask & (el == m1), lanef, big), axis=1, keepdims=True)
    emask2 = emask & (lanef != i1)
    el2 = jnp.where(emask2, logits, ninf)
    m2 = jnp.max(el2, axis=1, keepdims=True)
    i2 = jnp.min(jnp.where(emask2 & (el2 == m2), lanef, big), axis=1, keepdims=True)
    t = jnp.exp(m2 - m1)
    den = 1.0 + t
    return (1.0 / den) * gw, (t / den) * gw, i1 - N_GROUPS, i2 - N_GROUPS


def _post_mix_pieces(src, w, sink):
    (w_ret_o, cln_g, cln_b, w_conv_o, w_out, ln1_g, ln1_b, wr_hi, wr_lo, b_r) = w
    st = {}

    def branch_a():
        st["a"] = _bdot(_silu(src["g"]()) * src["ret"](), w_ret_o[...])

    def branch_b():
        st["b"] = _bdot(_silu(_ln(src["cout"](), cln_g[...], cln_b[...])), w_conv_o[...])

    def merge():
        mix = _sigmoid(src["gt_a"]()) * st["a"] + _sigmoid(src["gt_b"]()) * st["b"]
        h = ALPHA * src["x"]() + _bdot(mix, w_out[...])
        st["x1"] = _ln(h, ln1_g[...], ln1_b[...])

    def router():
        x1 = st["x1"]
        x1_hi = x1.astype(BF16)
        x1_lo = (x1 - x1_hi.astype(F32)).astype(BF16)
        both = jnp.dot(x1_hi, wr_lo[...], preferred_element_type=F32)
        st["logits"] = (both[:, :LANES]
                        + (jnp.dot(x1_lo, wr_hi[...], preferred_element_type=F32) + both[:, LANES:])
                        + b_r[...])

    def route():
        st["route"] = _route(st["logits"])

    def finish():
        sink(st["x1"], *st["route"])

    return [branch_a, branch_b, merge, router, route, finish]


def _sort_tile(x1, w1, w2, e1, e2, x1_ref, rw_ref, xs_ref, meta_ref):
    t = x1.shape[0]
    ids_t = _lane_tile((e1, e2), t).T
    e1r, e2r = ids_t[0:1, :], ids_t[1:2, :]
    sub = lax.broadcasted_iota(I32, (LANES, t), 0).astype(F32)
    a1 = (sub == e1r).astype(F32)
    a2 = (sub == e2r).astype(F32)
    ri = lax.broadcasted_iota(I32, (t, t), 0)
    ci = lax.broadcasted_iota(I32, (t, t), 1)
    earlier = (ri < ci).astype(BF16)
    r1 = jnp.dot(a1.astype(BF16), earlier, preferred_element_type=F32)
    r2 = jnp.dot(a2.astype(BF16), earlier, preferred_element_type=F32)
    cnt1 = jnp.sum(a1, axis=1, keepdims=True)
    cnt = cnt1 + jnp.sum(a2, axis=1, keepdims=True)
    nch = jnp.floor((cnt + (CHUNK - 1.0)) * (1.0 / CHUNK))
    ui = lax.broadcasted_iota(I32, (LANES, LANES), 0)
    uj = lax.broadcasted_iota(I32, (LANES, LANES), 1)
    before = (uj < ui).astype(BF16)
    off = jnp.dot(before, jnp.broadcast_to(nch, (LANES, LANES)).astype(BF16),
                  preferred_element_type=F32)[:, 0:1]
    base = off * CHUNK
    pos1r = jnp.sum(a1 * (base + r1), axis=0, keepdims=True)
    pos2r = jnp.sum(a2 * (base + cnt1 + r2), axis=0, keepdims=True)
    slot = lax.broadcasted_iota(I32, (USED_ROWS, t), 0).astype(F32)
    onehot = ((slot == pos1r) | (slot == pos2r)).astype(BF16)
    xs = jnp.dot(onehot, x1.astype(BF16), preferred_element_type=F32)
    pos_cols = jnp.where(sub == 2.0, pos1r, jnp.where(sub == 3.0, pos2r, 0.0)).T
    lane = lax.broadcasted_iota(I32, (t, LANES), 1)
    x1_ref[...] = x1
    rw_ref[...] = jnp.where(lane == 0, w1, jnp.where(lane == 1, w2, pos_cols))
    xs_ref[0:USED_ROWS, :] = xs.astype(BF16)
    xs_ref[USED_ROWS:CAP, :] = jnp.zeros((CAP - USED_ROWS, x1.shape[1]), BF16)
    mlane = lax.broadcasted_iota(I32, (LANES, LANES), 1)
    meta = jnp.where(mlane == 0, cnt, jnp.where(mlane == 1, off, 0.0))
    meta_ref[...] = meta.astype(I32)


def _sort_tiles(x1, w1, w2, e1, e2, x1_ref, rw_ref, xs_ref, meta_ref):
    for i in range(x1.shape[0] // TL):
        rows = slice(i * TL, (i + 1) * TL)
        _sort_tile(x1[rows], w1[rows], w2[rows], e1[rows], e2[rows],
                   x1_ref.at[pl.ds(i * TL, TL)], rw_ref.at[pl.ds(i * TL, TL)],
                   xs_ref.at[pl.ds(i * CAP, CAP)], meta_ref.at[pl.ds(i * LANES, LANES)])


GATE_COLS = {3: 0, 6: RET_V, 7: RET_V + D_MODEL}
QKV_COLS = {0: 0, 1: RET_QK, 2: 2 * RET_QK}


def _prompt_mixer_kernel(x_ref, x1s_ref, rws_ref, cos_ref, sin_ref, dec_ref, qdec_ref, kdec_ref, cdec_ref,
                         w_in, b_in, gn_g, gn_b, w_ret_o, conv_w, conv_b, cln_g, cln_b,
                         w_conv_o, w_out, ln1_g, ln1_b, wr_hi, wr_lo, b_r,
                         x1_ref, rw_ref, xs_ref, meta_ref, sret_ref, sconv_ref,
                         ubuf, ushift, qkv_scr, xb_scr, ret_scr, cout_scr, gate_scr,
                         *, n_tiles, tiles_per_seq):
    s = pl.program_id(0)
    li = lax.rem(s, tiles_per_seq)
    outs = (x1_ref, rw_ref, xs_ref, meta_ref)
    slot = dict(ret=ret_scr, cout=cout_scr, gates=gate_scr)
    tail_w = (w_ret_o, cln_g, cln_b, w_conv_o, w_out, ln1_g, ln1_b, wr_hi, wr_lo, b_r)

    @pl.when((s < n_tiles) & (li == 0))
    def _new_sequence():
        sret_ref[...] = jnp.zeros(sret_ref.shape, F32)
        ubuf[0:CONV_PAD, :] = jnp.zeros((CONV_PAD, CONV_CH), F32)

    @pl.when(s < n_tiles)
    def _mix():
        gcols = lambda kk: slice(GATE_COLS[kk], GATE_COLS[kk] + IN_WIDTHS[kk])
        src = dict(x=lambda: x_ref[0], ret=lambda: ret_scr[...], cout=lambda: cout_scr[...],
                   g=lambda: gate_scr[:, gcols(3)], gt_a=lambda: gate_scr[:, gcols(6)],
                   gt_b=lambda: gate_scr[:, gcols(7)])
        head = _prompt_head_pieces(x_ref, cos_ref, sin_ref, dec_ref, qdec_ref, kdec_ref, cdec_ref,
                                   w_in, b_in, gn_g, gn_b, conv_w, conv_b, sret_ref,
                                   ubuf, ushift, qkv_scr, xb_scr, slot)
        tail = _post_mix_pieces(src, tail_w, lambda *r: _sort_tiles(*r, *outs))
        for piece in head + tail:
            piece()

    @pl.when(s >= n_tiles)
    def _append():
        rws = rws_ref[...]
        _sort_tiles(x1s_ref[...], rws[:, 0:1], rws[:, 1:2], rws[:, 2:3], rws[:, 3:4], *outs)

    @pl.when((s < n_tiles) & (li == tiles_per_seq - 1))
    def _conv_state():
        sconv_ref[0, 0] = ubuf[CONV_PAD - (CONV_WIDTH - 1):CONV_PAD, :]


def _prompt_head_pieces(x_ref, cos_ref, sin_ref, dec_ref, qdec_ref, kdec_ref, cdec_ref,
                        w_in, b_in, gn_g, gn_b, conv_w, conv_b, sret_ref,
                        ubuf, ushift, qkv_scr, xb_scr, slot):
    tl = x_ref.shape[1]
    st = {}

    def slab_dot(c0, c1):
        return jnp.dot(xb_scr[...], w_in[:, c0:c1], preferred_element_type=F32) + _rep(b_in[:, c0:c1], tl)

    def glu():
        xb_scr[...] = x_ref[0].astype(BF16)
        u = slab_dot(IN_OFFS[4], IN_OFFS[5]) * _sigmoid(slab_dot(IN_OFFS[5], IN_OFFS[6]))
        ubuf[CONV_PAD:CONV_PAD + tl, :] = u

    nsh = ushift.shape[1]
    span = nsh - (CONV_PAD - SUBLANES)

    def shift_copy(h, s):
        ushift[s - 1] = ubuf[h * span + s:h * span + s + nsh, :]

    slab = 256
    slabs = [(kk, c0) for kk in (0, 1, 2, 3, 6, 7) for c0 in range(IN_OFFS[kk], IN_OFFS[kk + 1], slab)]
    rb = 32
    nrb = tl // rb

    def proj_slab(kk, c0):
        val = slab_dot(c0, c0 + slab)
        if kk in QKV_COLS:
            dst = QKV_COLS[kk] + c0 - IN_OFFS[kk]
            qkv_scr[:, dst:dst + slab] = val
        else:
            dst = GATE_COLS[kk] + c0 - IN_OFFS[kk]
            slot["gates"][:, dst:dst + slab] = val

    def conv_block(r):
        h, rl = divmod(r * rb, span)
        acc = jnp.zeros((rb, CONV_CH), F32) + conv_b[...]
        for j in range(CONV_WIDTH):
            off = j + (CONV_PAD - (CONV_WIDTH - 1))
            s = off % SUBLANES
            base = rl + off - s
            win = (ubuf[h * span + base:h * span + base + rb, :] if s == 0
                   else ushift[s - 1, base:base + rb, :])
            acc = acc + _rep(conv_w[j], rb) * win
        slot["cout"][r * rb:(r + 1) * rb, :] = acc
        if r == nrb - 1:
            ubuf[0:CONV_PAD, :] = ubuf[tl:tl + CONV_PAD, :]

    scale = RET_DK ** -0.5

    def retention(c, h):
        rows = slice(c * RET_CHUNK, (c + 1) * RET_CHUNK)
        cols = slice(h * RET_DK, (h + 1) * RET_DK)
        hcol = lambda kk: slice(QKV_COLS[kk] + h * RET_DK, QKV_COLS[kk] + (h + 1) * RET_DK)
        cosf = cos_ref[rows, :]
        sinf = sin_ref[rows, :]
        qh = _rot(qkv_scr[rows, hcol(0)], cosf, sinf)
        kh = _rot(qkv_scr[rows, hcol(1)], cosf, sinf) * scale
        qb = qh.astype(BF16)
        kb = kh.astype(BF16)
        vb = qkv_scr[rows, hcol(2)].astype(BF16)
        s_old = sret_ref[0, 0, h]
        scores = lax.dot_general(qb, kb, (((1,), (1,)), ((), ())),
                                 preferred_element_type=F32) * dec_ref[h]
        inner = jnp.dot(scores.astype(BF16), vb, preferred_element_type=F32)
        cross = jnp.dot(qb, s_old.astype(BF16), preferred_element_type=F32) * qdec_ref[h]
        kd = (kh * kdec_ref[h]).astype(BF16)
        s_new = cdec_ref[h] * s_old + lax.dot_general(
            kd, vb, (((0,), (0,)), ((), ())), preferred_element_type=F32)
        sret_ref[0, 0, h] = s_new
        slot["ret"][rows, cols] = _ln(inner + cross, gn_g[:, cols], gn_b[:, cols])

    vector_pieces = []
    for r in range(nrb):
        if (r * rb) % span == 0:
            vector_pieces += [lambda h=(r * rb) // span, s=s: shift_copy(h, s) for s in range(1, SUBLANES)]
        vector_pieces.append(lambda r=r: conv_block(r))
    pieces = [glu]
    for i, piece in enumerate(vector_pieces):
        pieces.append(piece)
        for kk, c0 in slabs[i * len(slabs) // len(vector_pieces):(i + 1) * len(slabs) // len(vector_pieces)]:
            pieces.append(lambda kk=kk, c0=c0: proj_slab(kk, c0))
    pieces += [lambda c=c, h=h: retention(c, h) for c in range(tl // RET_CHUNK) for h in range(RET_HEADS)]
    return pieces


def _sample_mixer_kernel(x_ref, cos_ref, sin_ref, pdec_ref, qdec_ref, kdec_ref, cdec_ref, conv_w,
                         sret_in, sconv_in,
                         w_in, b_in, gn_g, gn_b, w_ret_o, conv_b, cln_g, cln_b,
                         w_conv_o, w_out, ln1_g, ln1_b, wr_hi, wr_lo, b_r,
                         x1_ref, rw_ref, sret_ref, sconv_ref,
                         ret_scr, cout_scr, gate_scr, xpad, wsh):
    i = pl.program_id(0)
    t = cos_ref.shape[0]
    ls = t // BB_SAMPLE
    nstate = CONV_WIDTH - 1
    r0 = pl.multiple_of(i * t, t)
    xb = x_ref[pl.ds(r0, t), :].astype(BF16)

    @pl.when(i == 0)
    def _tap_tables():
        wsh[...] = jnp.zeros(wsh.shape, F32)
        for p in range(ls):
            wsh[p, p:nstate, :] = conv_w[0:nstate - p, :]
            wsh[p, XPAD_NEW:XPAD_NEW + p + 1, :] = conv_w[nstate - p:CONV_WIDTH, :]

    def proj(k):
        c0, c1 = IN_OFFS[k], IN_OFFS[k + 1]
        return jnp.dot(xb, w_in[:, c0:c1], preferred_element_type=F32) + _rep(b_in[:, c0:c1], t)

    q = proj(0)
    k = proj(1)
    v = proj(2)
    scale = RET_DK ** -0.5
    cosf = cos_ref[...]
    sinf = sin_ref[...]
    row = lax.broadcasted_iota(I32, (t, RET_DK), 0)
    pos = row % ls
    row8 = lax.broadcasted_iota(I32, (SUBLANES, RET_DK), 0)
    per_tile = SUBLANES // ls
    for h in range(RET_HEADS):
        cols = slice(h * RET_DK, (h + 1) * RET_DK)
        qh = _rot(q[:, cols], cosf, sinf)
        kh = _rot(k[:, cols], cosf, sinf) * scale
        vh = v[:, cols]
        inner = jnp.zeros((t, RET_DV), F32)
        for s in range(ls):
            ks = kh if s == 0 else pltpu.roll(kh, s, axis=0)
            vs = vh if s == 0 else pltpu.roll(vh, s, axis=0)
            dotp = jnp.sum(qh * ks, axis=1, keepdims=True) * pdec_ref[h, s]
            inner = inner + jnp.where(pos >= s, dotp, 0.0) * vs
        kd = kh * kdec_ref[h]
        for tile in range(t // SUBLANES):
            rows = slice(tile * SUBLANES, (tile + 1) * SUBLANES)
            q8 = qh[rows, :]
            kd8 = kd[rows, :]
            v8 = vh[rows, :]
            seqs = [tile * per_tile + sub for sub in range(per_tile)]
            mine = [(row8 >= sub * ls) & (row8 < (sub + 1) * ls) for sub in range(per_tile)]
            s_old = [sret_in[0, b, h] for b in seqs]
            c_all = jnp.dot(q8, jnp.concatenate(s_old, axis=1), preferred_element_type=F32)
            upd = lax.dot_general(jnp.concatenate([jnp.where(m, kd8, 0.0) for m in mine], axis=1), v8,
                                  (((0,), (0,)), ((), ())), preferred_element_type=F32)
            cross8 = jnp.zeros((SUBLANES, RET_DV), F32)
            for sub, b in enumerate(seqs):
                cross8 = jnp.where(mine[sub], c_all[:, sub * RET_DV:(sub + 1) * RET_DV], cross8)
                sret_ref[0, b, h] = cdec_ref[h] * s_old[sub] + upd[sub * RET_DK:(sub + 1) * RET_DK, :]
            ret_scr[pl.ds(r0 + tile * SUBLANES, SUBLANES), cols] = (
                inner[rows, :] + cross8 * qdec_ref[h, rows, :])
        ret_scr[pl.ds(r0, t), cols] = _ln(ret_scr[pl.ds(r0, t), cols], gn_g[:, cols], gn_b[:, cols])

    u = proj(4) * _sigmoid(proj(5))
    xpad[...] = jnp.zeros(xpad.shape, F32)
    xpad[:, 0:nstate, :] = sconv_in[0]
    for b in range(BB_SAMPLE):
        xpad[b, XPAD_NEW:XPAD_NEW + ls, :] = u[b * ls:(b + 1) * ls, :]
    for p in range(ls):
        res = jnp.sum(xpad[...] * wsh[p][None], axis=1) + conv_b[...]
        for sl in range(CONV_CH // LANES):
            cout_scr[sl, pl.ds(r0 + p, BB_SAMPLE, stride=ls), :] = res[:, sl * LANES:(sl + 1) * LANES]
    sconv_ref[0, :, 0:nstate - ls, :] = xpad[:, ls:nstate, :]
    sconv_ref[0, :, nstate - ls:nstate, :] = xpad[:, XPAD_NEW:XPAD_NEW + ls, :]
    for kk in (3, 6, 7):
        gate_scr[pl.ds(r0, t), GATE_COLS[kk]:GATE_COLS[kk] + IN_WIDTHS[kk]] = proj(kk)

    @pl.when(i == pl.num_programs(0) - 1)
    def _second_half():
        n = x_ref.shape[0]

        def sink(x1, w1, w2, e1, e2):
            x1_ref[...] = x1
            rw_ref[...] = _lane_tile((w1, w2, e1, e2), n)

        gcols = lambda kk: slice(GATE_COLS[kk], GATE_COLS[kk] + IN_WIDTHS[kk])
        src = dict(x=lambda: x_ref[...], ret=lambda: ret_scr[...],
                   cout=lambda: jnp.concatenate([cout_scr[sl] for sl in range(CONV_CH // LANES)], axis=1),
                   g=lambda: gate_scr[:, gcols(3)], gt_a=lambda: gate_scr[:, gcols(6)],
                   gt_b=lambda: gate_scr[:, gcols(7)])
        for piece in _post_mix_pieces(
                src, (w_ret_o, cln_g, cln_b, w_conv_o, w_out, ln1_g, ln1_b, wr_hi, wr_lo, b_r), sink):
            piece()


def _ffn_kernel(te_ref, nvalid_ref, nexte_ref, chunk_ref, xs_hbm, wgu_hbm, wdn_hbm, ys_hbm,
                xbuf, obuf, wgu_f, wdn_f, wgu_b, wdn_b, sem_in, sem_out, sem_w):
    del xs_hbm
    nvalid = nvalid_ref[0]

    def chunk_rows(tile, c):
        return pl.ds(pl.multiple_of(chunk_ref[tile * TILE_CHUNKS + c] * CHUNK, CHUNK), CHUNK)

    def start_in(tile, s):
        for c in range(TILE_CHUNKS):
            pltpu.make_async_copy(ys_hbm.at[chunk_rows(tile, c)],
                                  xbuf.at[s, pl.ds(c * CHUNK, CHUNK)], sem_in.at[s]).start()

    def start_out(tile, s):
        for c in range(TILE_CHUNKS):
            pltpu.make_async_copy(obuf.at[s, pl.ds(c * CHUNK, CHUNK)],
                                  ys_hbm.at[chunk_rows(tile, c)], sem_out.at[s]).start()

    def wait_in(s):
        pltpu.make_async_copy(ys_hbm.at[pl.ds(0, TM_FFN)], xbuf.at[s], sem_in.at[s]).wait()

    def wait_out(s):
        pltpu.make_async_copy(obuf.at[s], ys_hbm.at[pl.ds(0, TM_FFN)], sem_out.at[s]).wait()

    def weight_copies(e, s):
        return (pltpu.make_async_copy(wgu_hbm.at[e], wgu_f.at[s], sem_w.at[s]),
                pltpu.make_async_copy(wdn_hbm.at[e], wdn_f.at[s], sem_w.at[s]))

    def tile(i, wslot):
        slot = i % 2

        @pl.when(i >= 2)
        def _retire():
            wait_out(slot)

        wait_in(slot)
        changed = (i == 0) | (te_ref[i] != te_ref[jnp.maximum(i - 1, 0)])

        @pl.when(changed)
        def _new_expert():
            for cp in weight_copies(te_ref[i], wslot):
                cp.wait()
            wgu_b[...] = wgu_f[wslot].astype(BF16)
            wdn_b[...] = wdn_f[wslot].astype(BF16)

            @pl.when(nexte_ref[i] >= 0)
            def _():
                for cp in weight_copies(nexte_ref[i], 1 - wslot):
                    cp.start()

        x = xbuf[slot]
        y = jnp.zeros((TM_FFN, wdn_b.shape[1]), F32)
        for c0 in range(0, EXP_FF, FFN_COLS):
            hg = jnp.dot(x, wgu_b[:, c0:c0 + FFN_COLS], preferred_element_type=F32)
            hu = jnp.dot(x, wgu_b[:, EXP_FF + c0:EXP_FF + c0 + FFN_COLS], preferred_element_type=F32)
            y = y + _bdot(_silu(hg) * hu, wdn_b[c0:c0 + FFN_COLS, :])
            if c0 == 0:
                start_in(jnp.where(i + 1 < nvalid, i + 1, 0), 1 - slot)
        obuf[slot] = y.astype(BF16)
        start_out(i, slot)
        return jnp.where(changed, 1 - wslot, wslot)

    for cp in weight_copies(te_ref[0], 0):
        cp.start()
    start_in(0, 0)
    lax.fori_loop(0, nvalid, tile, jnp.int32(0))
    wait_out(nvalid % 2)
    wait_out(1 - nvalid % 2)
    wait_in(nvalid % 2)


def _final_kernel(ys_ref, x1_ref, rw_ref, pp_ref, ps_ref, ln2_g, ln2_b, w_pg, b_pg, w_ple,
                  yp_ref, ys_out_ref, *, n_prompt_tiles):
    i = pl.program_id(0)
    x1 = x1_ref[...]
    slot = lax.broadcasted_iota(I32, (TL, USED_ROWS), 1).astype(F32)
    parts = []
    for b in range(x1.shape[0] // TL):
        rw = rw_ref[b * TL:(b + 1) * TL, :]
        w1, w2, pos1, pos2 = rw[:, 0:1], rw[:, 1:2], rw[:, 2:3], rw[:, 3:4]
        ys = ys_ref[b * CAP:b * CAP + USED_ROWS, :]
        comb = jnp.where(slot == pos1, w1, jnp.where(slot == pos2, w2, 0.0)).astype(BF16)
        parts.append(jnp.dot(comb, ys, preferred_element_type=F32))
    moe = parts[0] if len(parts) == 1 else jnp.concatenate(parts, axis=0)
    x2 = _ln(ALPHA * x1 + moe, ln2_g[...], ln2_b[...])
    gate = _sigmoid(_bdot(x2, w_pg[...]) + b_pg[...])
    p = jnp.where(i < n_prompt_tiles, pp_ref[...], ps_ref[...])
    y = x2 + gate * _bdot(p, w_ple[...])

    @pl.when(i < n_prompt_tiles)
    def _prompt():
        yp_ref[...] = y

    @pl.when(i >= n_prompt_tiles)
    def _sample():
        ys_out_ref[...] = y


def _rope_tables(pos):
    half = RET_DK // 2
    inv_freq = ROPE_BASE ** (-np.arange(half, dtype=np.float64) / half)
    ang = np.asarray(pos, np.float64)[:, None] * inv_freq[None, :]
    cos = np.cos(ang)
    sin = np.sin(ang)
    return (np.concatenate([cos, cos], axis=-1).astype(np.float32),
            np.concatenate([-sin, sin], axis=-1).astype(np.float32))


def _log_gamma():
    return np.log(1.0 - 2.0 ** (-5.0 - np.arange(RET_HEADS, dtype=np.float64)))


def _const_spec(shape):
    nd = len(shape)
    return pl.BlockSpec(shape, lambda *_: (0,) * nd, pipeline_mode=pl.Buffered(1))


def _chunk_plan(meta, n_blocks, n_ffn_tiles):
    assert n_blocks * BLOCK_SPARE >= N_EXPERTS * (TILE_CHUNKS - 1)
    m = meta.reshape(n_blocks, LANES, LANES)
    cnt = m[:, :N_EXPERTS, 0]
    off = m[:, :N_EXPERTS, 1]
    nch = (cnt + (CHUNK - 1)) // CHUNK
    cum = jnp.cumsum(nch, axis=0)
    total = cum[-1:]
    tiles_e = (total + TILE_CHUNKS - 1) // TILE_CHUNKS
    tile_end = jnp.cumsum(tiles_e, axis=1)
    tile_start = tile_end - tiles_e
    tid = jnp.arange(n_ffn_tiles, dtype=I32)[:, None]
    owner = (tid >= tile_start) & (tid < tile_end)
    pick_e = lambda v: jnp.sum(jnp.where(owner, v, 0), axis=1, keepdims=True)
    te = pick_e(jnp.arange(N_EXPERTS, dtype=I32)[None, :])
    k = (tid - pick_e(tile_start)) * TILE_CHUNKS + jnp.arange(TILE_CHUNKS, dtype=I32)[None, :]
    total_t = pick_e(total)
    real = k < total_t
    by_tile = lambda v: jnp.sum(jnp.where(owner[:, None, :], v[None, :, :], 0), axis=2)
    cum_t = by_tile(cum)
    blk = jnp.minimum(jnp.sum((cum_t[:, None, :] <= k[:, :, None]).astype(I32), axis=2), n_blocks - 1)
    at_blk = blk[:, :, None] == jnp.arange(n_blocks, dtype=I32)[None, None, :]
    pick_b = lambda v: jnp.sum(jnp.where(at_blk, v[:, None, :], 0), axis=2)
    excl = pick_b(cum_t - by_tile(nch))
    off_t = pick_b(by_tile(off))
    spare = te * (TILE_CHUNKS - 1) + jnp.maximum(k - total_t, 0) % TILE_CHUNKS
    spare_chunk = (spare // BLOCK_SPARE) * BLOCK_CHUNKS + BLOCK_USED + spare % BLOCK_SPARE
    chunk = jnp.where(real, blk * BLOCK_CHUNKS + off_t + (k - excl), spare_chunk)
    n_valid = jnp.sum(tiles_e, axis=1)
    eid = jnp.arange(N_EXPERTS, dtype=I32)
    later = (eid[None, :] > eid[:, None]) & (tiles_e > 0)
    next_e = jnp.min(jnp.where(later, eid[None, :], N_EXPERTS), axis=1)[None, :]
    next_t = pick_e(jnp.where(next_e < N_EXPERTS, next_e, -1))
    return (te.reshape(-1).astype(I32), n_valid.astype(I32), next_t.reshape(-1).astype(I32),
            chunk.reshape(-1).astype(I32))


def kernel(x_prompt, x_sample, state_ret, state_conv, p_prompt, p_sample, w_in, b_in, ret_gn_g, ret_gn_b,
           w_ret_o, conv_w, conv_b, conv_ln_g, conv_ln_b, w_conv_o, w_out, ln1_g, ln1_b, w_grp, b_grp,
           w_exp, b_exp, w_gu, w_dn, ln2_g, ln2_b, w_pg, b_pg, w_ple):
    assert DEPTH == 1 and w_in.shape[0] == 1
    bp, lp, d = x_prompt.shape
    bs, ls, _ = x_sample.shape
    n_p, n_s = bp * lp, bs * ls
    n_tok = n_p + n_s
    assert lp % TL == 0 and n_s % TL == 0 and bs % BB_SAMPLE == 0 and SUBLANES % ls == 0
    n_blocks = n_tok // TL

    f32c = lambda a, shape: jnp.asarray(np.broadcast_to(a, shape).astype(np.float32))
    lg = _log_gamma()
    c = RET_CHUNK
    idx = np.arange(c, dtype=np.float64)
    rel = idx[:, None] - idx[None, :]
    causal = rel >= 0
    decay = np.where(causal[None], np.exp(np.where(causal, rel, 0.0)[None] * lg[:, None, None]), 0.0)
    decay = f32c(decay, decay.shape)
    q_decay = np.exp((idx[:, None] + 1.0) * lg[None, :])
    k_decay = np.exp((c - 1.0 - idx[:, None]) * lg[None, :])
    chunk_decay = np.exp(c * lg)
    qdec_p = f32c(q_decay.T[:, :, None], (RET_HEADS, c, RET_DK))
    kdec_p = f32c(k_decay.T[:, :, None], (RET_HEADS, c, RET_DK))
    cdec_p = f32c(chunk_decay[:, None, None], (RET_HEADS, 1, RET_DV))
    cos_p, sin_p = (jnp.asarray(a) for a in _rope_tables(np.arange(lp)))

    ts = BB_SAMPLE * ls
    idx_s = np.arange(ls, dtype=np.float64)
    pdec_s = np.exp(idx_s[None, :] * lg[:, None])
    pdec_s = f32c(pdec_s[:, :, None, None], (RET_HEADS, ls, 1, RET_DK))
    qd_s = np.exp((idx_s[:, None] + 1.0) * lg[None, :])
    kd_s = np.exp((ls - 1.0 - idx_s[:, None]) * lg[None, :])
    qdec_s = f32c(np.tile(qd_s.T, (1, BB_SAMPLE))[:, :, None], (RET_HEADS, ts, RET_DK))
    kdec_s = f32c(np.tile(kd_s.T, (1, BB_SAMPLE))[:, :, None], (RET_HEADS, ts, RET_DK))
    cdec_s = f32c(np.exp(ls * lg)[:, None, None], (RET_HEADS, 1, RET_DV))
    cos_s, sin_s = (jnp.asarray(a) for a in _rope_tables(np.tile(PAST_LEN + np.arange(ls), BB_SAMPLE)))

    w_in_b = w_in[0].astype(BF16)
    w_ret_o_b = w_ret_o[0].astype(BF16)
    w_conv_o_b = w_conv_o[0].astype(BF16)
    w_out_b = w_out[0].astype(BF16)
    w_pg_b = w_pg[0].astype(BF16)
    w_ple_b = w_ple[0].astype(BF16)
    n_route = N_GROUPS + N_EXPERTS
    w_r = jnp.concatenate([w_grp[0], w_exp[0], jnp.zeros((d, LANES - n_route), F32)], axis=1)
    wr_hi = w_r.astype(BF16)
    wr_lo = jnp.concatenate([wr_hi, (w_r - wr_hi.astype(F32)).astype(BF16)], axis=1)
    b_r = jnp.concatenate([b_grp[0], b_exp[0], jnp.zeros((LANES - n_route,), F32)]).reshape(1, LANES)
    row = lambda a: a.reshape(1, -1)
    conv_w0 = conv_w[0]
    nstate = CONV_WIDTH - 1

    rep8 = lambda a: jnp.broadcast_to(a[..., None, :], a.shape[:-1] + (SUBLANES, a.shape[-1]))
    shared_w = (w_in_b, rep8(b_in[0]), row(ret_gn_g[0]), row(ret_gn_b[0]), w_ret_o_b)
    tail_w = (row(conv_ln_g[0]), row(conv_ln_b[0]), w_conv_o_b, w_out_b, row(ln1_g[0]), row(ln1_b[0]),
              wr_hi, wr_lo, b_r)

    nbt = bs // BB_SAMPLE
    xs2 = x_sample.reshape(n_s, d)
    sample_in = ((xs2, cos_s, sin_s, pdec_s, qdec_s, kdec_s, cdec_s, conv_w0, state_ret, state_conv)
                 + shared_w + (row(conv_b[0]),) + tail_w)
    sample_specs = (
        [_const_spec(a.shape) for a in sample_in[0:8]]
        + [pl.BlockSpec((1, BB_SAMPLE, RET_HEADS, RET_DK, RET_DV), lambda i: (0, i, 0, 0, 0)),
           pl.BlockSpec((1, BB_SAMPLE, nstate, CONV_CH), lambda i: (0, i, 0, 0))]
        + [_const_spec(a.shape) for a in sample_in[10:]]
    )
    tok_spec_s = lambda w: pl.BlockSpec((n_s, w), lambda i: (0, 0))
    x1_s, rw_s, ret_s, conv_s = pl.pallas_call(
        _sample_mixer_kernel,
        grid=(nbt,),
        in_specs=sample_specs,
        out_specs=[
            tok_spec_s(d), tok_spec_s(LANES),
            pl.BlockSpec((1, BB_SAMPLE, RET_HEADS, RET_DK, RET_DV), lambda i: (0, i, 0, 0, 0)),
            pl.BlockSpec((1, BB_SAMPLE, nstate, CONV_CH), lambda i: (0, i, 0, 0)),
        ],
        out_shape=[
            jax.ShapeDtypeStruct((n_s, d), F32),
            jax.ShapeDtypeStruct((n_s, LANES), F32),
            jax.ShapeDtypeStruct(state_ret.shape, F32),
            jax.ShapeDtypeStruct(state_conv.shape, F32),
        ],
        scratch_shapes=[
            pltpu.VMEM((n_s, RET_V), F32),
            pltpu.VMEM((CONV_CH // LANES, n_s, LANES), F32),
            pltpu.VMEM((n_s, RET_V + 2 * D_MODEL), F32),
            pltpu.VMEM((BB_SAMPLE, XPAD_ROWS, CONV_CH), F32),
            pltpu.VMEM((ls, XPAD_ROWS, CONV_CH), F32),
        ],
        compiler_params=pltpu.CompilerParams(
            dimension_semantics=("arbitrary",), vmem_limit_bytes=VMEM_LIMIT),
        name="sample_mixer",
    )(*sample_in)

    assert lp % TLM == 0 and n_s % TLM == 0 and TLM % TL == 0
    nlt = lp // TLM
    npt = n_p // TLM
    nst = n_s // TLM
    sub = TLM // TL
    prompt_in = ((x_prompt, x1_s, rw_s, cos_p, sin_p, decay, qdec_p, kdec_p, cdec_p)
                 + shared_w + (rep8(conv_w0), row(conv_b[0])) + tail_w)
    head_tile = lambda s: jnp.minimum(s, npt - 1)
    sample_tile = lambda s: jnp.maximum(s - npt, 0)
    sample_spec = lambda w: pl.BlockSpec((TLM, w), lambda s: (sample_tile(s), 0))
    prompt_specs = [
        pl.BlockSpec((1, TLM, d), lambda s: (head_tile(s) // nlt, head_tile(s) % nlt, 0)),
        sample_spec(d), sample_spec(LANES),
        pl.BlockSpec((TLM, RET_DK), lambda s: (head_tile(s) % nlt, 0)),
        pl.BlockSpec((TLM, RET_DK), lambda s: (head_tile(s) % nlt, 0)),
    ] + [_const_spec(a.shape) for a in prompt_in[5:]]
    tok_spec_p = lambda rows, w: pl.BlockSpec((rows, w), lambda s: (s, 0))
    x1_all, rw_all, xs_all, meta, ret_p, conv_p = pl.pallas_call(
        functools.partial(_prompt_mixer_kernel, n_tiles=npt, tiles_per_seq=nlt),
        grid=(npt + nst,),
        in_specs=prompt_specs,
        out_specs=[
            tok_spec_p(TLM, d), tok_spec_p(TLM, LANES), tok_spec_p(sub * CAP, d), tok_spec_p(sub * LANES, LANES),
            pl.BlockSpec((1, 1, RET_HEADS, RET_DK, RET_DV), lambda s: (0, head_tile(s) // nlt, 0, 0, 0)),
            pl.BlockSpec((1, 1, nstate, CONV_CH), lambda s: (0, head_tile(s) // nlt, 0, 0)),
        ],
        out_shape=[
            jax.ShapeDtypeStruct((n_tok, d), F32),
            jax.ShapeDtypeStruct((n_tok, LANES), F32),
            jax.ShapeDtypeStruct((n_blocks * CAP, d), BF16),
            jax.ShapeDtypeStruct((n_blocks * LANES, LANES), I32),
            jax.ShapeDtypeStruct((1, bp, RET_HEADS, RET_DK, RET_DV), F32),
            jax.ShapeDtypeStruct((1, bp, nstate, CONV_CH), F32),
        ],
        scratch_shapes=[
            pltpu.VMEM((TLM + CONV_PAD, CONV_CH), F32),
            pltpu.VMEM((SUBLANES - 1, TL + CONV_PAD - SUBLANES, CONV_CH), F32),
            pltpu.VMEM((TLM, 2 * RET_QK + RET_V), F32),
            pltpu.VMEM((TLM, d), BF16),
            pltpu.VMEM((TLM, RET_V), F32),
            pltpu.VMEM((TLM, CONV_CH), F32),
            pltpu.VMEM((TLM, RET_V + 2 * D_MODEL), F32),
        ],
        compiler_params=pltpu.CompilerParams(
            dimension_semantics=("arbitrary",), vmem_limit_bytes=VMEM_LIMIT),
        name="prompt_mixer",
    )(*prompt_in)

    assert TOP_K * n_tok // TM_FFN >= 3
    max_chunks = n_blocks * (TOP_K * TL // CHUNK + N_EXPERTS - 1)
    n_ffn_tiles = (max_chunks + N_EXPERTS * (TILE_CHUNKS - 1)) // TILE_CHUNKS
    tile_e, n_valid_tiles, next_e, chunk_ids = _chunk_plan(meta, n_blocks, n_ffn_tiles)

    ys_all = pl.pallas_call(
        _ffn_kernel,
        grid_spec=pltpu.PrefetchScalarGridSpec(
            num_scalar_prefetch=4,
            grid=(1,),
            in_specs=[pl.BlockSpec(memory_space=pl.ANY)] * 3,
            out_specs=pl.BlockSpec(memory_space=pl.ANY),
            scratch_shapes=[
                pltpu.VMEM((2, TM_FFN, d), BF16),
                pltpu.VMEM((2, TM_FFN, d), BF16),
                pltpu.VMEM((2, d, 2 * EXP_FF), F32),
                pltpu.VMEM((2, EXP_FF, d), F32),
                pltpu.VMEM((d, 2 * EXP_FF), BF16),
                pltpu.VMEM((EXP_FF, d), BF16),
                pltpu.SemaphoreType.DMA((2,)),
                pltpu.SemaphoreType.DMA((2,)),
                pltpu.SemaphoreType.DMA((2,)),
            ],
        ),
        out_shape=jax.ShapeDtypeStruct(xs_all.shape, BF16),
        input_output_aliases={4: 0},
        compiler_params=pltpu.CompilerParams(
            dimension_semantics=("arbitrary",), vmem_limit_bytes=VMEM_LIMIT),
        name="expert_ffn",
    )(tile_e, n_valid_tiles, next_e, chunk_ids, xs_all, w_gu[0], w_dn[0])

    assert n_p % TLF == 0 and n_s % TLF == 0
    npt = n_p // TLF
    fsub = TLF // TL
    pp2 = p_prompt.reshape(n_p, PLE_DIM)
    ps2 = p_sample.reshape(n_s, PLE_DIM)
    tok_f = lambda rows, w: pl.BlockSpec((rows, w), lambda i: (i, 0))
    y_p, y_s = pl.pallas_call(
        functools.partial(_final_kernel, n_prompt_tiles=npt),
        grid=(n_tok // TLF,),
        in_specs=[
            tok_f(fsub * CAP, d), tok_f(TLF, d), tok_f(TLF, LANES),
            pl.BlockSpec((TLF, PLE_DIM), lambda i: (jnp.minimum(i, npt - 1), 0)),
            pl.BlockSpec((TLF, PLE_DIM), lambda i: (jnp.maximum(i - npt, 0), 0)),
            _const_spec((1, d)), _const_spec((1, d)), _const_spec((d, d)), _const_spec((1, d)),
            _const_spec((PLE_DIM, d)),
        ],
        out_specs=[
            pl.BlockSpec((TLF, d), lambda i: (jnp.minimum(i, npt - 1), 0)),
            pl.BlockSpec((TLF, d), lambda i: (jnp.maximum(i - npt, 0), 0)),
        ],
        out_shape=[jax.ShapeDtypeStruct((n_p, d), F32), jax.ShapeDtypeStruct((n_s, d), F32)],
        compiler_params=pltpu.CompilerParams(
            dimension_semantics=("arbitrary",), vmem_limit_bytes=VMEM_LIMIT),
        name="moe_combine_final",
    )(ys_all, x1_all, rw_all, pp2, ps2,
      row(ln2_g[0]), row(ln2_b[0]), w_pg_b, row(b_pg[0]), w_ple_b)

    return (y_p.reshape(bp, lp, d), y_s.reshape(bs, ls, d), ret_p, conv_p, ret_s, conv_s)
```

```python
import functools

import jax
import jax.numpy as jnp
import numpy as np
from jax import lax
from jax.experimental import pallas as pl
from jax.experimental.pallas import tpu as pltpu

F32 = jnp.float32
BF16 = jnp.bfloat16
I32 = jnp.int32

D_MODEL = 1024
PAST_LEN = 16384
RET_HEADS = 4
RET_DK = 128
RET_DV = 128
RET_QK = RET_HEADS * RET_DK
RET_V = RET_HEADS * RET_DV
RET_CHUNK = 128
ROPE_BASE = 10000.0
CONV_CH = 512
CONV_WIDTH = 31
N_GROUPS = 4
EXP_PER_GROUP = 4
N_EXPERTS = N_GROUPS * EXP_PER_GROUP
TOP_K = 2
EXP_FF = 512
PLE_DIM = 256
DEPTH = 1
ALPHA = (2 * DEPTH) ** 0.25
LN_EPS = 1e-5
IN_WIDTHS = (RET_QK, RET_QK, RET_V, RET_V, CONV_CH, CONV_CH, D_MODEL, D_MODEL)
IN_OFFS = tuple(int(s) for s in np.cumsum((0,) + IN_WIDTHS))

LANES = 128
SUBLANES = 8
VMEM_LIMIT = 56 * 1024 * 1024

TL = 256
TLM = 512
TLF = 512
BB_SAMPLE = 16
CHUNK = 2 * SUBLANES
TILE_CHUNKS = 32
BLOCK_USED = -(-(TOP_K * TL + N_EXPERTS * (CHUNK - 1)) // LANES) * LANES // CHUNK
BLOCK_SPARE = LANES // CHUNK
BLOCK_CHUNKS = BLOCK_USED + BLOCK_SPARE
USED_ROWS = BLOCK_USED * CHUNK
CAP = BLOCK_CHUNKS * CHUNK
TM_FFN = TILE_CHUNKS * CHUNK
FFN_COLS = 256
CONV_PAD = 32
XPAD_NEW = 32
XPAD_ROWS = 40


def _ln(x, g, b):
    mu = jnp.mean(x, axis=-1, keepdims=True)
    d = x - mu
    var = jnp.mean(d * d, axis=-1, keepdims=True)
    return d * lax.rsqrt(var + LN_EPS) * g + b


def _sigmoid(x):
    return 1.0 / (1.0 + jnp.exp(-x))


def _rep(v8, rows):
    return v8 if rows == SUBLANES else jnp.concatenate([v8] * (rows // SUBLANES), axis=0)


def _silu(x):
    return x * _sigmoid(x)


def _bdot(a, b):
    return jnp.dot(a.astype(BF16), b, preferred_element_type=F32)


def _rot(t, cosf, sinf):
    return t * cosf + pltpu.roll(t, RET_DK // 2, axis=1) * sinf


def _lane_tile(cols, rows):
    lane = lax.broadcasted_iota(I32, (rows, LANES), 1)
    out = jnp.zeros((rows, LANES), F32)
    for i, col in enumerate(cols):
        out = jnp.where(lane == i, col, out)
    return out


def _route(logits):
    lane = lax.broadcasted_iota(I32, logits.shape, 1)
    lanef = lane.astype(F32)
    ninf = jnp.float32(-jnp.inf)
    big = jnp.float32(LANES)
    gmask = lane < N_GROUPS
    gl = jnp.where(gmask, logits, ninf)
    gmax = jnp.max(gl, axis=1, keepdims=True)
    gidx = jnp.min(jnp.where(gmask & (gl == gmax), lanef, big), axis=1, keepdims=True)
    sumexp = jnp.sum(jnp.where(gmask, jnp.exp(gl - gmax), 0.0), axis=1, keepdims=True)
    gw = 1.0 / sumexp
    lo = N_GROUPS + EXP_PER_GROUP * gidx
    emask = (lanef >= lo) & (lanef < lo + EXP_PER_GROUP)
    el = jnp.where(emask, logits, ninf)
    m1 = jnp.max(el, axis=1, keepdims=True)
    i1 = jnp.min(jnp.where(emask & (el == m1), lanef, big), axis=1, keepdims=True)
    emask2 = emask & (lanef != i1)
    el2 = jnp.where(emask2, logits, ninf)
    m2 = jnp.max(el2, axis=1, keepdims=True)
    i2 = jnp.min(jnp.where(emask2 & (el2 == m2), lanef, big), axis=1, keepdims=True)
    t = jnp.exp(m2 - m1)
    den = 1.0 + t
    return (1.0 / den) * gw, (t / den) * gw, i1 - N_GROUPS, i2 - N_GROUPS


def _post_mix_pieces(src, w, sink):
    (w_ret_o, cln_g, cln_b, w_conv_o, w_out, ln1_g, ln1_b, wr_hi, wr_lo, b_r) = w
    st = {}

    def branch_a():
        st["a"] = _bdot(_silu(src["g"]()) * src["ret"](), w_ret_o[...])

    def branch_b():
        st["b"] = _bdot(_silu(_ln(src["cout"](), cln_g[...], cln_b[...])), w_conv_o[...])

    def merge():
        mix = _sigmoid(src["gt_a"]()) * st["a"] + _sigmoid(src["gt_b"]()) * st["b"]
        h = ALPHA * src["x"]() + _bdot(mix, w_out[...])
        st["x1"] = _ln(h, ln1_g[...], ln1_b[...])

    def router():
        x1 = st["x1"]
        x1_hi = x1.astype(BF16)
        x1_lo = (x1 - x1_hi.astype(F32)).astype(BF16)
        both = jnp.dot(x1_hi, wr_lo[...], preferred_element_type=F32)
        st["logits"] = (both[:, :LANES]
                        + (jnp.dot(x1_lo, wr_hi[...], preferred_element_type=F32) + both[:, LANES:])
                        + b_r[...])

    def route():
        st["route"] = _route(st["logits"])

    def finish():
        sink(st["x1"], *st["route"])

    return [branch_a, branch_b, merge, router, route, finish]


def _sort_tile(x1, w1, w2, e1, e2, x1_ref, rw_ref, xs_ref, meta_ref):
    t = x1.shape[0]
    ids_t = _lane_tile((e1, e2), t).T
    e1r, e2r = ids_t[0:1, :], ids_t[1:2, :]
    sub = lax.broadcasted_iota(I32, (LANES, t), 0).astype(F32)
    a1 = (sub == e1r).astype(F32)
    a2 = (sub == e2r).astype(F32)
    ri = lax.broadcasted_iota(I32, (t, t), 0)
    ci = lax.broadcasted_iota(I32, (t, t), 1)
    earlier = (ri < ci).astype(BF16)
    r1 = jnp.dot(a1.astype(BF16), earlier, preferred_element_type=F32)
    r2 = jnp.dot(a2.astype(BF16), earlier, preferred_element_type=F32)
    cnt1 = jnp.sum(a1, axis=1, keepdims=True)
    cnt = cnt1 + jnp.sum(a2, axis=1, keepdims=True)
    nch = jnp.floor((cnt + (CHUNK - 1.0)) * (1.0 / CHUNK))
    ui = lax.broadcasted_iota(I32, (LANES, LANES), 0)
    uj = lax.broadcasted_iota(I32, (LANES, LANES), 1)
    before = (uj < ui).astype(BF16)
    off = jnp.dot(before, jnp.broadcast_to(nch, (LANES, LANES)).astype(BF16),
                  preferred_element_type=F32)[:, 0:1]
    base = off * CHUNK
    pos1r = jnp.sum(a1 * (base + r1), axis=0, keepdims=True)
    pos2r = jnp.sum(a2 * (base + cnt1 + r2), axis=0, keepdims=True)
    slot = lax.broadcasted_iota(I32, (USED_ROWS, t), 0).astype(F32)
    onehot = ((slot == pos1r) | (slot == pos2r)).astype(BF16)
    xs = jnp.dot(onehot, x1.astype(BF16), preferred_element_type=F32)
    pos_cols = jnp.where(sub == 2.0, pos1r, jnp.where(sub == 3.0, pos2r, 0.0)).T
    lane = lax.broadcasted_iota(I32, (t, LANES), 1)
    x1_ref[...] = x1
    rw_ref[...] = jnp.where(lane == 0, w1, jnp.where(lane == 1, w2, pos_cols))
    xs_ref[0:USED_ROWS, :] = xs.astype(BF16)
    xs_ref[USED_ROWS:CAP, :] = jnp.zeros((CAP - USED_ROWS, x1.shape[1]), BF16)
    mlane = lax.broadcasted_iota(I32, (LANES, LANES), 1)
    meta = jnp.where(mlane == 0, cnt, jnp.where(mlane == 1, off, 0.0))
    meta_ref[...] = meta.astype(I32)


def _sort_tiles(x1, w1, w2, e1, e2, x1_ref, rw_ref, xs_ref, meta_ref):
    for i in range(x1.shape[0] // TL):
        rows = slice(i * TL, (i + 1) * TL)
        _sort_tile(x1[rows], w1[rows], w2[rows], e1[rows], e2[rows],
                   x1_ref.at[pl.ds(i * TL, TL)], rw_ref.at[pl.ds(i * TL, TL)],
                   xs_ref.at[pl.ds(i * CAP, CAP)], meta_ref.at[pl.ds(i * LANES, LANES)])


GATE_COLS = {3: 0, 6: RET_V, 7: RET_V + D_MODEL}
QKV_COLS = {0: 0, 1: RET_QK, 2: 2 * RET_QK}


def _prompt_mixer_kernel(x_ref, x1s_ref, rws_ref, cos_ref, sin_ref, dec_ref, qdec_ref, kdec_ref, cdec_ref,
                         w_in, b_in, gn_g, gn_b, w_ret_o, conv_w, conv_b, cln_g, cln_b,
                         w_conv_o, w_out, ln1_g, ln1_b, wr_hi, wr_lo, b_r,
                         x1_ref, rw_ref, xs_ref, meta_ref, sret_ref, sconv_ref,
                         ubuf, ushift, qkv_scr, xb_scr, ret_scr, cout_scr, gate_scr,
                         *, n_tiles, tiles_per_seq):
    s = pl.program_id(0)
    li = lax.rem(s, tiles_per_seq)
    outs = (x1_ref, rw_ref, xs_ref, meta_ref)
    slot = dict(ret=ret_scr, cout=cout_scr, gates=gate_scr)
    tail_w = (w_ret_o, cln_g, cln_b, w_conv_o, w_out, ln1_g, ln1_b, wr_hi, wr_lo, b_r)

    @pl.when((s < n_tiles) & (li == 0))
    def _new_sequence():
        sret_ref[...] = jnp.zeros(sret_ref.shape, F32)
        ubuf[0:CONV_PAD, :] = jnp.zeros((CONV_PAD, CONV_CH), F32)

    @pl.when(s < n_tiles)
    def _mix():
        gcols = lambda kk: slice(GATE_COLS[kk], GATE_COLS[kk] + IN_WIDTHS[kk])
        src = dict(x=lambda: x_ref[0], ret=lambda: ret_scr[...], cout=lambda: cout_scr[...],
                   g=lambda: gate_scr[:, gcols(3)], gt_a=lambda: gate_scr[:, gcols(6)],
                   gt_b=lambda: gate_scr[:, gcols(7)])
        head = _prompt_head_pieces(x_ref, cos_ref, sin_ref, dec_ref, qdec_ref, kdec_ref, cdec_ref,
                                   w_in, b_in, gn_g, gn_b, conv_w, conv_b, sret_ref,
                                   ubuf, ushift, qkv_scr, xb_scr, slot)
        tail = _post_mix_pieces(src, tail_w, lambda *r: _sort_tiles(*r, *outs))
        for piece in head + tail:
            piece()

    @pl.when(s >= n_tiles)
    def _append():
        rws = rws_ref[...]
        _sort_tiles(x1s_ref[...], rws[:, 0:1], rws[:, 1:2], rws[:, 2:3], rws[:, 3:4], *outs)

    @pl.when((s < n_tiles) & (li == tiles_per_seq - 1))
    def _conv_state():
        sconv_ref[0, 0] = ubuf[CONV_PAD - (CONV_WIDTH - 1):CONV_PAD, :]


def _prompt_head_pieces(x_ref, cos_ref, sin_ref, dec_ref, qdec_ref, kdec_ref, cdec_ref,
                        w_in, b_in, gn_g, gn_b, conv_w, conv_b, sret_ref,
                        ubuf, ushift, qkv_scr, xb_scr, slot):
    tl = x_ref.shape[1]
    st = {}

    def slab_dot(c0, c1):
        return jnp.dot(xb_scr[...], w_in[:, c0:c1], preferred_element_type=F32) + _rep(b_in[:, c0:c1], tl)

    def glu():
        xb_scr[...] = x_ref[0].astype(BF16)
        u = slab_dot(IN_OFFS[4], IN_OFFS[5]) * _sigmoid(slab_dot(IN_OFFS[5], IN_OFFS[6]))
        ubuf[CONV_PAD:CONV_PAD + tl, :] = u

    nsh = ushift.shape[1]
    span = nsh - (CONV_PAD - SUBLANES)

    def shift_copy(h, s):
        ushift[s - 1] = ubuf[h * span + s:h * span + s + nsh, :]

    slab = 256
    slabs = [(kk, c0) for kk in (0, 1, 2, 3, 6, 7) for c0 in range(IN_OFFS[kk], IN_OFFS[kk + 1], slab)]
    rb = 32
    nrb = tl // rb

    def proj_slab(kk, c0):
        val = slab_dot(c0, c0 + slab)
        if kk in QKV_COLS:
            dst = QKV_COLS[kk] + c0 - IN_OFFS[kk]
            qkv_scr[:, dst:dst + slab] = val
        else:
            dst = GATE_COLS[kk] + c0 - IN_OFFS[kk]
            slot["gates"][:, dst:dst + slab] = val

    def conv_block(r):
        h, rl = divmod(r * rb, span)
        acc = jnp.zeros((rb, CONV_CH), F32) + conv_b[...]
        for j in range(CONV_WIDTH):
            off = j + (CONV_PAD - (CONV_WIDTH - 1))
            s = off % SUBLANES
            base = rl + off - s
            win = (ubuf[h * span + base:h * span + base + rb, :] if s == 0
                   else ushift[s - 1, base:base + rb, :])
            acc = acc + _rep(conv_w[j], rb) * win
        slot["cout"][r * rb:(r + 1) * rb, :] = acc
        if r == nrb - 1:
            ubuf[0:CONV_PAD, :] = ubuf[tl:tl + CONV_PAD, :]

    scale = RET_DK ** -0.5

    def retention(c, h):
        rows = slice(c * RET_CHUNK, (c + 1) * RET_CHUNK)
        cols = slice(h * RET_DK, (h + 1) * RET_DK)
        hcol = lambda kk: slice(QKV_COLS[kk] + h * RET_DK, QKV_COLS[kk] + (h + 1) * RET_DK)
        cosf = cos_ref[rows, :]
        sinf = sin_ref[rows, :]
        qh = _rot(qkv_scr[rows, hcol(0)], cosf, sinf)
        kh = _rot(qkv_scr[rows, hcol(1)], cosf, sinf) * scale
        qb = qh.astype(BF16)
        kb = kh.astype(BF16)
        vb = qkv_scr[rows, hcol(2)].astype(BF16)
        s_old = sret_ref[0, 0, h]
        scores = lax.dot_general(qb, kb, (((1,), (1,)), ((), ())),
                                 preferred_element_type=F32) * dec_ref[h]
        inner = jnp.dot(scores.astype(BF16), vb, preferred_element_type=F32)
        cross = jnp.dot(qb, s_old.astype(BF16), preferred_element_type=F32) * qdec_ref[h]
        kd = (kh * kdec_ref[h]).astype(BF16)
        s_new = cdec_ref[h] * s_old + lax.dot_general(
            kd, vb, (((0,), (0,)), ((), ())), preferred_element_type=F32)
        sret_ref[0, 0, h] = s_new
        slot["ret"][rows, cols] = _ln(inner + cross, gn_g[:, cols], gn_b[:, cols])

    vector_pieces = []
    for r in range(nrb):
        if (r * rb) % span == 0:
            vector_pieces += [lambda h=(r * rb) // span, s=s: shift_copy(h, s) for s in range(1, SUBLANES)]
        vector_pieces.append(lambda r=r: conv_block(r))
    pieces = [glu]
    for i, piece in enumerate(vector_pieces):
        pieces.append(piece)
        for kk, c0 in slabs[i * len(slabs) // len(vector_pieces):(i + 1) * len(slabs) // len(vector_pieces)]:
            pieces.append(lambda kk=kk, c0=c0: proj_slab(kk, c0))
    pieces += [lambda c=c, h=h: retention(c, h) for c in range(tl // RET_CHUNK) for h in range(RET_HEADS)]
    return pieces


def _sample_mixer_kernel(x_ref, cos_ref, sin_ref, pdec_ref, qdec_ref, kdec_ref, cdec_ref, conv_w,
                         sret_in, sconv_in,
                         w_in, b_in, gn_g, gn_b, w_ret_o, conv_b, cln_g, cln_b,
                         w_conv_o, w_out, ln1_g, ln1_b, wr_hi, wr_lo, b_r,
                         x1_ref, rw_ref, sret_ref, sconv_ref,
                         ret_scr, cout_scr, gate_scr, xwin, u_scr):
    i = pl.program_id(0)
    t = cos_ref.shape[0]
    ls = t // BB_SAMPLE
    nstate = CONV_WIDTH - 1
    r0 = pl.multiple_of(i * t, t)
    xb = x_ref[pl.ds(r0, t), :].astype(BF16)

    def proj(k):
        c0, c1 = IN_OFFS[k], IN_OFFS[k + 1]
        return jnp.dot(xb, w_in[:, c0:c1], preferred_element_type=F32) + _rep(b_in[:, c0:c1], t)

    q = proj(0)
    k = proj(1)
    v = proj(2)
    scale = RET_DK ** -0.5
    cosf = cos_ref[...]
    sinf = sin_ref[...]
    row = lax.broadcasted_iota(I32, (t, RET_DK), 0)
    pos = row % ls
    row8 = lax.broadcasted_iota(I32, (SUBLANES, RET_DK), 0)
    per_tile = SUBLANES // ls
    for h in range(RET_HEADS):
        cols = slice(h * RET_DK, (h + 1) * RET_DK)
        qh = _rot(q[:, cols], cosf, sinf)
        kh = _rot(k[:, cols], cosf, sinf) * scale
        vh = v[:, cols]
        inner = jnp.zeros((t, RET_DV), F32)
        for s in range(ls):
            ks = kh if s == 0 else pltpu.roll(kh, s, axis=0)
            vs = vh if s == 0 else pltpu.roll(vh, s, axis=0)
            dotp = jnp.sum(qh * ks, axis=1, keepdims=True) * pdec_ref[h, s]
            inner = inner + jnp.where(pos >= s, dotp, 0.0) * vs
        kd = kh * kdec_ref[h]
        for tile in range(t // SUBLANES):
            rows = slice(tile * SUBLANES, (tile + 1) * SUBLANES)
            q8 = qh[rows, :]
            kd8 = kd[rows, :]
            v8 = vh[rows, :]
            seqs = [tile * per_tile + sub for sub in range(per_tile)]
            mine = [(row8 >= sub * ls) & (row8 < (sub + 1) * ls) for sub in range(per_tile)]
            s_old = [sret_in[0, b, h] for b in seqs]
            c_all = jnp.dot(q8, jnp.concatenate(s_old, axis=1), preferred_element_type=F32)
            upd = lax.dot_general(jnp.concatenate([jnp.where(m, kd8, 0.0) for m in mine], axis=1), v8,
                                  (((0,), (0,)), ((), ())), preferred_element_type=F32)
            cross8 = jnp.zeros((SUBLANES, RET_DV), F32)
            for sub, b in enumerate(seqs):
                cross8 = jnp.where(mine[sub], c_all[:, sub * RET_DV:(sub + 1) * RET_DV], cross8)
                sret_ref[0, b, h] = cdec_ref[h] * s_old[sub] + upd[sub * RET_DK:(sub + 1) * RET_DK, :]
            ret_scr[pl.ds(r0 + tile * SUBLANES, SUBLANES), cols] = (
                inner[rows, :] + cross8 * qdec_ref[h, rows, :])
        ret_scr[pl.ds(r0, t), cols] = _ln(ret_scr[pl.ds(r0, t), cols], gn_g[:, cols], gn_b[:, cols])

    u = proj(4) * _sigmoid(proj(5))
    nslab = CONV_CH // LANES
    xwin[0:nstate] = sconv_in[...]
    for sl in range(nslab):
        u_scr[sl] = u[:, sl * LANES:(sl + 1) * LANES]
    for p in range(ls):
        for sl in range(nslab):
            xwin[XPAD_NEW + p, :, sl * LANES:(sl + 1) * LANES] = u_scr[sl, pl.ds(p, BB_SAMPLE, stride=ls), :]
    win_row = lambda m: m if m < nstate else XPAD_NEW + (m - nstate)
    for p in range(ls):
        res = jnp.zeros((BB_SAMPLE, CONV_CH), F32) + conv_b[...]
        for j in range(CONV_WIDTH):
            res = res + _rep(conv_w[j], BB_SAMPLE) * xwin[win_row(p + j)]
        for sl in range(nslab):
            cout_scr[sl, pl.ds(r0 + p, BB_SAMPLE, stride=ls), :] = res[:, sl * LANES:(sl + 1) * LANES]
    sconv_ref[0:nstate - ls] = xwin[ls:nstate]
    sconv_ref[nstate - ls:nstate] = xwin[XPAD_NEW:XPAD_NEW + ls]
    for kk in (3, 6, 7):
        gate_scr[pl.ds(r0, t), GATE_COLS[kk]:GATE_COLS[kk] + IN_WIDTHS[kk]] = proj(kk)

    @pl.when(i == pl.num_programs(0) - 1)
    def _second_half():
        n = x_ref.shape[0]

        def sink(x1, w1, w2, e1, e2):
            x1_ref[...] = x1
            rw_ref[...] = _lane_tile((w1, w2, e1, e2), n)

        gcols = lambda kk: slice(GATE_COLS[kk], GATE_COLS[kk] + IN_WIDTHS[kk])
        src = dict(x=lambda: x_ref[...], ret=lambda: ret_scr[...],
                   cout=lambda: jnp.concatenate([cout_scr[sl] for sl in range(CONV_CH // LANES)], axis=1),
                   g=lambda: gate_scr[:, gcols(3)], gt_a=lambda: gate_scr[:, gcols(6)],
                   gt_b=lambda: gate_scr[:, gcols(7)])
        for piece in _post_mix_pieces(
                src, (w_ret_o, cln_g, cln_b, w_conv_o, w_out, ln1_g, ln1_b, wr_hi, wr_lo, b_r), sink):
            piece()


def _ffn_kernel(te_ref, nvalid_ref, nexte_ref, chunk_ref, xs_hbm, wgu_hbm, wdn_hbm, ys_hbm,
                xbuf, obuf, wgu_f, wdn_f, wgu_b, wdn_b, sem_in, sem_out, sem_w):
    del xs_hbm
    nvalid = nvalid_ref[0]

    def chunk_rows(tile, c):
        return pl.ds(pl.multiple_of(chunk_ref[tile * TILE_CHUNKS + c] * CHUNK, CHUNK), CHUNK)

    def start_in(tile, s):
        for c in range(TILE_CHUNKS):
            pltpu.make_async_copy(ys_hbm.at[chunk_rows(tile, c)],
                                  xbuf.at[s, pl.ds(c * CHUNK, CHUNK)], sem_in.at[s]).start()

    def start_out(tile, s):
        for c in range(TILE_CHUNKS):
            pltpu.make_async_copy(obuf.at[s, pl.ds(c * CHUNK, CHUNK)],
                                  ys_hbm.at[chunk_rows(tile, c)], sem_out.at[s]).start()

    def wait_in(s):
        pltpu.make_async_copy(ys_hbm.at[pl.ds(0, TM_FFN)], xbuf.at[s], sem_in.at[s]).wait()

    def wait_out(s):
        pltpu.make_async_copy(obuf.at[s], ys_hbm.at[pl.ds(0, TM_FFN)], sem_out.at[s]).wait()

    def weight_copies(e, s):
        return (pltpu.make_async_copy(wgu_hbm.at[e], wgu_f.at[s], sem_w.at[s]),
                pltpu.make_async_copy(wdn_hbm.at[e], wdn_f.at[s], sem_w.at[s]))

    def tile(i, wslot):
        slot = i % 2

        @pl.when(i >= 2)
        def _retire():
            wait_out(slot)

        wait_in(slot)
        changed = (i == 0) | (te_ref[i] != te_ref[jnp.maximum(i - 1, 0)])

        @pl.when(changed)
        def _new_expert():
            for cp in weight_copies(te_ref[i], wslot):
                cp.wait()
            wgu_b[...] = wgu_f[wslot].astype(BF16)
            wdn_b[...] = wdn_f[wslot].astype(BF16)

            @pl.when(nexte_ref[i] >= 0)
            def _():
                for cp in weight_copies(nexte_ref[i], 1 - wslot):
                    cp.start()

        x = xbuf[slot]
        y = jnp.zeros((TM_FFN, wdn_b.shape[1]), F32)
        for c0 in range(0, EXP_FF, FFN_COLS):
            hg = jnp.dot(x, wgu_b[:, c0:c0 + FFN_COLS], preferred_element_type=F32)
            hu = jnp.dot(x, wgu_b[:, EXP_FF + c0:EXP_FF + c0 + FFN_COLS], preferred_element_type=F32)
            y = y + _bdot(_silu(hg) * hu, wdn_b[c0:c0 + FFN_COLS, :])
            if c0 == 0:
                start_in(jnp.where(i + 1 < nvalid, i + 1, 0), 1 - slot)
        obuf[slot] = y.astype(BF16)
        start_out(i, slot)
        return jnp.where(changed, 1 - wslot, wslot)

    for cp in weight_copies(te_ref[0], 0):
        cp.start()
    start_in(0, 0)
    lax.fori_loop(0, nvalid, tile, jnp.int32(0))
    wait_out(nvalid % 2)
    wait_out(1 - nvalid % 2)
    wait_in(nvalid % 2)


def _final_kernel(ys_ref, x1_ref, rw_ref, pp_ref, ps_ref, ln2_g, ln2_b, w_pg, b_pg, w_ple,
                  yp_ref, ys_out_ref, *, n_prompt_tiles):
    i = pl.program_id(0)
    x1 = x1_ref[...]
    slot = lax.broadcasted_iota(I32, (TL, USED_ROWS), 1).astype(F32)
    parts = []
    for b in range(x1.shape[0] // TL):
        rw = rw_ref[b * TL:(b + 1) * TL, :]
        w1, w2, pos1, pos2 = rw[:, 0:1], rw[:, 1:2], rw[:, 2:3], rw[:, 3:4]
        ys = ys_ref[b * CAP:b * CAP + USED_ROWS, :]
        comb = jnp.where(slot == pos1, w1, jnp.where(slot == pos2, w2, 0.0)).astype(BF16)
        parts.append(jnp.dot(comb, ys, preferred_element_type=F32))
    moe = parts[0] if len(parts) == 1 else jnp.concatenate(parts, axis=0)
    x2 = _ln(ALPHA * x1 + moe, ln2_g[...], ln2_b[...])
    gate = _sigmoid(_bdot(x2, w_pg[...]) + b_pg[...])
    p = jnp.where(i < n_prompt_tiles, pp_ref[...], ps_ref[...])
    y = x2 + gate * _bdot(p, w_ple[...])

    @pl.when(i < n_prompt_tiles)
    def _prompt():
        yp_ref[...] = y

    @pl.when(i >= n_prompt_tiles)
    def _sample():
        ys_out_ref[...] = y


def _rope_tables(pos):
    half = RET_DK // 2
    inv_freq = ROPE_BASE ** (-np.arange(half, dtype=np.float64) / half)
    ang = np.asarray(pos, np.float64)[:, None] * inv_freq[None, :]
    cos = np.cos(ang)
    sin = np.sin(ang)
    return (np.concatenate([cos, cos], axis=-1).astype(np.float32),
            np.concatenate([-sin, sin], axis=-1).astype(np.float32))


def _log_gamma():
    return np.log(1.0 - 2.0 ** (-5.0 - np.arange(RET_HEADS, dtype=np.float64)))


def _const_spec(shape):
    nd = len(shape)
    return pl.BlockSpec(shape, lambda *_: (0,) * nd, pipeline_mode=pl.Buffered(1))


def _chunk_plan(meta, n_blocks, n_ffn_tiles):
    assert n_blocks * BLOCK_SPARE >= N_EXPERTS * (TILE_CHUNKS - 1)
    m = meta.reshape(n_blocks, LANES, LANES)
    cnt = m[:, :N_EXPERTS, 0]
    off = m[:, :N_EXPERTS, 1]
    nch = (cnt + (CHUNK - 1)) // CHUNK
    cum = jnp.cumsum(nch, axis=0)
    total = cum[-1:]
    tiles_e = (total + TILE_CHUNKS - 1) // TILE_CHUNKS
    tile_end = jnp.cumsum(tiles_e, axis=1)
    tile_start = tile_end - tiles_e
    tid = jnp.arange(n_ffn_tiles, dtype=I32)[:, None]
    owner = (tid >= tile_start) & (tid < tile_end)
    pick_e = lambda v: jnp.sum(jnp.where(owner, v, 0), axis=1, keepdims=True)
    te = pick_e(jnp.arange(N_EXPERTS, dtype=I32)[None, :])
    k = (tid - pick_e(tile_start)) * TILE_CHUNKS + jnp.arange(TILE_CHUNKS, dtype=I32)[None, :]
    total_t = pick_e(total)
    real = k < total_t
    by_tile = lambda v: jnp.sum(jnp.where(owner[:, None, :], v[None, :, :], 0), axis=2)
    cum_t = by_tile(cum)
    blk = jnp.minimum(jnp.sum((cum_t[:, None, :] <= k[:, :, None]).astype(I32), axis=2), n_blocks - 1)
    at_blk = blk[:, :, None] == jnp.arange(n_blocks, dtype=I32)[None, None, :]
    pick_b = lambda v: jnp.sum(jnp.where(at_blk, v[:, None, :], 0), axis=2)
    excl = pick_b(cum_t - by_tile(nch))
    off_t = pick_b(by_tile(off))
    spare = te * (TILE_CHUNKS - 1) + jnp.maximum(k - total_t, 0) % TILE_CHUNKS
    spare_chunk = (spare // BLOCK_SPARE) * BLOCK_CHUNKS + BLOCK_USED + spare % BLOCK_SPARE
    chunk = jnp.where(real, blk * BLOCK_CHUNKS + off_t + (k - excl), spare_chunk)
    n_valid = jnp.sum(tiles_e, axis=1)
    eid = jnp.arange(N_EXPERTS, dtype=I32)
    later = (eid[None, :] > eid[:, None]) & (tiles_e > 0)
    next_e = jnp.min(jnp.where(later, eid[None, :], N_EXPERTS), axis=1)[None, :]
    next_t = pick_e(jnp.where(next_e < N_EXPERTS, next_e, -1))
    return (te.reshape(-1).astype(I32), n_valid.astype(I32), next_t.reshape(-1).astype(I32),
            chunk.reshape(-1).astype(I32))


def kernel(x_prompt, x_sample, state_ret, state_conv, p_prompt, p_sample, w_in, b_in, ret_gn_g, ret_gn_b,
           w_ret_o, conv_w, conv_b, conv_ln_g, conv_ln_b, w_conv_o, w_out, ln1_g, ln1_b, w_grp, b_grp,
           w_exp, b_exp, w_gu, w_dn, ln2_g, ln2_b, w_pg, b_pg, w_ple):
    assert DEPTH == 1 and w_in.shape[0] == 1
    bp, lp, d = x_prompt.shape
    bs, ls, _ = x_sample.shape
    n_p, n_s = bp * lp, bs * ls
    n_tok = n_p + n_s
    assert lp % TL == 0 and n_s % TL == 0 and bs % BB_SAMPLE == 0 and SUBLANES % ls == 0
    n_blocks = n_tok // TL

    f32c = lambda a, shape: jnp.asarray(np.broadcast_to(a, shape).astype(np.float32))
    lg = _log_gamma()
    c = RET_CHUNK
    idx = np.arange(c, dtype=np.float64)
    rel = idx[:, None] - idx[None, :]
    causal = rel >= 0
    decay = np.where(causal[None], np.exp(np.where(causal, rel, 0.0)[None] * lg[:, None, None]), 0.0)
    decay = f32c(decay, decay.shape)
    q_decay = np.exp((idx[:, None] + 1.0) * lg[None, :])
    k_decay = np.exp((c - 1.0 - idx[:, None]) * lg[None, :])
    chunk_decay = np.exp(c * lg)
    qdec_p = f32c(q_decay.T[:, :, None], (RET_HEADS, c, RET_DK))
    kdec_p = f32c(k_decay.T[:, :, None], (RET_HEADS, c, RET_DK))
    cdec_p = f32c(chunk_decay[:, None, None], (RET_HEADS, 1, RET_DV))
    cos_p, sin_p = (jnp.asarray(a) for a in _rope_tables(np.arange(lp)))

    ts = BB_SAMPLE * ls
    idx_s = np.arange(ls, dtype=np.float64)
    pdec_s = np.exp(idx_s[None, :] * lg[:, None])
    pdec_s = f32c(pdec_s[:, :, None, None], (RET_HEADS, ls, 1, RET_DK))
    qd_s = np.exp((idx_s[:, None] + 1.0) * lg[None, :])
    kd_s = np.exp((ls - 1.0 - idx_s[:, None]) * lg[None, :])
    qdec_s = f32c(np.tile(qd_s.T, (1, BB_SAMPLE))[:, :, None], (RET_HEADS, ts, RET_DK))
    kdec_s = f32c(np.tile(kd_s.T, (1, BB_SAMPLE))[:, :, None], (RET_HEADS, ts, RET_DK))
    cdec_s = f32c(np.exp(ls * lg)[:, None, None], (RET_HEADS, 1, RET_DV))
    cos_s, sin_s = (jnp.asarray(a) for a in _rope_tables(np.tile(PAST_LEN + np.arange(ls), BB_SAMPLE)))

    w_in_b = w_in[0].astype(BF16)
    w_ret_o_b = w_ret_o[0].astype(BF16)
    w_conv_o_b = w_conv_o[0].astype(BF16)
    w_out_b = w_out[0].astype(BF16)
    w_pg_b = w_pg[0].astype(BF16)
    w_ple_b = w_ple[0].astype(BF16)
    n_route = N_GROUPS + N_EXPERTS
    w_r = jnp.concatenate([w_grp[0], w_exp[0], jnp.zeros((d, LANES - n_route), F32)], axis=1)
    wr_hi = w_r.astype(BF16)
    wr_lo = jnp.concatenate([wr_hi, (w_r - wr_hi.astype(F32)).astype(BF16)], axis=1)
    b_r = jnp.concatenate([b_grp[0], b_exp[0], jnp.zeros((LANES - n_route,), F32)]).reshape(1, LANES)
    row = lambda a: a.reshape(1, -1)
    conv_w0 = conv_w[0]
    nstate = CONV_WIDTH - 1

    rep8 = lambda a: jnp.broadcast_to(a[..., None, :], a.shape[:-1] + (SUBLANES, a.shape[-1]))
    shared_w = (w_in_b, rep8(b_in[0]), row(ret_gn_g[0]), row(ret_gn_b[0]), w_ret_o_b)
    tail_w = (row(conv_ln_g[0]), row(conv_ln_b[0]), w_conv_o_b, w_out_b, row(ln1_g[0]), row(ln1_b[0]),
              wr_hi, wr_lo, b_r)

    nbt = bs // BB_SAMPLE
    xs2 = x_sample.reshape(n_s, d)
    conv_state_t = jnp.transpose(state_conv[0], (1, 0, 2))
    sample_in = ((xs2, cos_s, sin_s, pdec_s, qdec_s, kdec_s, cdec_s, rep8(conv_w0), state_ret, conv_state_t)
                 + shared_w + (row(conv_b[0]),) + tail_w)
    conv_state_spec = pl.BlockSpec((nstate, BB_SAMPLE, CONV_CH), lambda i: (0, i, 0))
    sample_specs = (
        [_const_spec(a.shape) for a in sample_in[0:8]]
        + [pl.BlockSpec((1, BB_SAMPLE, RET_HEADS, RET_DK, RET_DV), lambda i: (0, i, 0, 0, 0)), conv_state_spec]
        + [_const_spec(a.shape) for a in sample_in[10:]]
    )
    tok_spec_s = lambda w: pl.BlockSpec((n_s, w), lambda i: (0, 0))
    x1_s, rw_s, ret_s, conv_s_t = pl.pallas_call(
        _sample_mixer_kernel,
        grid=(nbt,),
        in_specs=sample_specs,
        out_specs=[
            tok_spec_s(d), tok_spec_s(LANES),
            pl.BlockSpec((1, BB_SAMPLE, RET_HEADS, RET_DK, RET_DV), lambda i: (0, i, 0, 0, 0)),
            conv_state_spec,
        ],
        out_shape=[
            jax.ShapeDtypeStruct((n_s, d), F32),
            jax.ShapeDtypeStruct((n_s, LANES), F32),
            jax.ShapeDtypeStruct(state_ret.shape, F32),
            jax.ShapeDtypeStruct(conv_state_t.shape, F32),
        ],
        scratch_shapes=[
            pltpu.VMEM((n_s, RET_V), F32),
            pltpu.VMEM((CONV_CH // LANES, n_s, LANES), F32),
            pltpu.VMEM((n_s, RET_V + 2 * D_MODEL), F32),
            pltpu.VMEM((XPAD_ROWS, BB_SAMPLE, CONV_CH), F32),
            pltpu.VMEM((CONV_CH // LANES, ts, LANES), F32),
        ],
        compiler_params=pltpu.CompilerParams(
            dimension_semantics=("arbitrary",), vmem_limit_bytes=VMEM_LIMIT),
        name="sample_mixer",
    )(*sample_in)
    conv_s = jnp.transpose(conv_s_t, (1, 0, 2))[None]

    assert lp % TLM == 0 and n_s % TLM == 0 and TLM % TL == 0
    nlt = lp // TLM
    npt = n_p // TLM
    nst = n_s // TLM
    sub = TLM // TL
    prompt_in = ((x_prompt, x1_s, rw_s, cos_p, sin_p, decay, qdec_p, kdec_p, cdec_p)
                 + shared_w + (rep8(conv_w0), row(conv_b[0])) + tail_w)
    head_tile = lambda s: jnp.minimum(s, npt - 1)
    sample_tile = lambda s: jnp.maximum(s - npt, 0)
    sample_spec = lambda w: pl.BlockSpec((TLM, w), lambda s: (sample_tile(s), 0))
    prompt_specs = [
        pl.BlockSpec((1, TLM, d), lambda s: (head_tile(s) // nlt, head_tile(s) % nlt, 0)),
        sample_spec(d), sample_spec(LANES),
        pl.BlockSpec((TLM, RET_DK), lambda s: (head_tile(s) % nlt, 0)),
        pl.BlockSpec((TLM, RET_DK), lambda s: (head_tile(s) % nlt, 0)),
    ] + [_const_spec(a.shape) for a in prompt_in[5:]]
    tok_spec_p = lambda rows, w: pl.BlockSpec((rows, w), lambda s: (s, 0))
    x1_all, rw_all, xs_all, meta, ret_p, conv_p = pl.pallas_call(
        functools.partial(_prompt_mixer_kernel, n_tiles=npt, tiles_per_seq=nlt),
        grid=(npt + nst,),
        in_specs=prompt_specs,
        out_specs=[
            tok_spec_p(TLM, d), tok_spec_p(TLM, LANES), tok_spec_p(sub * CAP, d), tok_spec_p(sub * LANES, LANES),
            pl.BlockSpec((1, 1, RET_HEADS, RET_DK, RET_DV), lambda s: (0, head_tile(s) // nlt, 0, 0, 0)),
            pl.BlockSpec((1, 1, nstate, CONV_CH), lambda s: (0, head_tile(s) // nlt, 0, 0)),
        ],
        out_shape=[
            jax.ShapeDtypeStruct((n_tok, d), F32),
            jax.ShapeDtypeStruct((n_tok, LANES), F32),
            jax.ShapeDtypeStruct((n_blocks * CAP, d), BF16),
            jax.ShapeDtypeStruct((n_blocks * LANES, LANES), I32),
            jax.ShapeDtypeStruct((1, bp, RET_HEADS, RET_DK, RET_DV), F32),
            jax.ShapeDtypeStruct((1, bp, nstate, CONV_CH), F32),
        ],
        scratch_shapes=[
            pltpu.VMEM((TLM + CONV_PAD, CONV_CH), F32),
            pltpu.VMEM((SUBLANES - 1, TL + CONV_PAD - SUBLANES, CONV_CH), F32),
            pltpu.VMEM((TLM, 2 * RET_QK + RET_V), F32),
            pltpu.VMEM((TLM, d), BF16),
            pltpu.VMEM((TLM, RET_V), F32),
            pltpu.VMEM((TLM, CONV_CH), F32),
            pltpu.VMEM((TLM, RET_V + 2 * D_MODEL), F32),
        ],
        compiler_params=pltpu.CompilerParams(
            dimension_semantics=("arbitrary",), vmem_limit_bytes=VMEM_LIMIT),
        name="prompt_mixer",
    )(*prompt_in)

    assert TOP_K * n_tok // TM_FFN >= 3
    max_chunks = n_blocks * (TOP_K * TL // CHUNK + N_EXPERTS - 1)
    n_ffn_tiles = (max_chunks + N_EXPERTS * (TILE_CHUNKS - 1)) // TILE_CHUNKS
    tile_e, n_valid_tiles, next_e, chunk_ids = _chunk_plan(meta, n_blocks, n_ffn_tiles)

    ys_all = pl.pallas_call(
        _ffn_kernel,
        grid_spec=pltpu.PrefetchScalarGridSpec(
            num_scalar_prefetch=4,
            grid=(1,),
            in_specs=[pl.BlockSpec(memory_space=pl.ANY)] * 3,
            out_specs=pl.BlockSpec(memory_space=pl.ANY),
            scratch_shapes=[
                pltpu.VMEM((2, TM_FFN, d), BF16),
                pltpu.VMEM((2, TM_FFN, d), BF16),
                pltpu.VMEM((2, d, 2 * EXP_FF), F32),
                pltpu.VMEM((2, EXP_FF, d), F32),
                pltpu.VMEM((d, 2 * EXP_FF), BF16),
                pltpu.VMEM((EXP_FF, d), BF16),
                pltpu.SemaphoreType.DMA((2,)),
                pltpu.SemaphoreType.DMA((2,)),
                pltpu.SemaphoreType.DMA((2,)),
            ],
        ),
        out_shape=jax.ShapeDtypeStruct(xs_all.shape, BF16),
        input_output_aliases={4: 0},
        compiler_params=pltpu.CompilerParams(
            dimension_semantics=("arbitrary",), vmem_limit_bytes=VMEM_LIMIT),
        name="expert_ffn",
    )(tile_e, n_valid_tiles, next_e, chunk_ids, xs_all, w_gu[0], w_dn[0])

    assert n_p % TLF == 0 and n_s % TLF == 0
    npt = n_p // TLF
    fsub = TLF // TL
    pp2 = p_prompt.reshape(n_p, PLE_DIM)
    ps2 = p_sample.reshape(n_s, PLE_DIM)
    tok_f = lambda rows, w: pl.BlockSpec((rows, w), lambda i: (i, 0))
    y_p, y_s = pl.pallas_call(
        functools.partial(_final_kernel, n_prompt_tiles=npt),
        grid=(n_tok // TLF,),
        in_specs=[
            tok_f(fsub * CAP, d), tok_f(TLF, d), tok_f(TLF, LANES),
            pl.BlockSpec((TLF, PLE_DIM), lambda i: (jnp.minimum(i, npt - 1), 0)),
            pl.BlockSpec((TLF, PLE_DIM), lambda i: (jnp.maximum(i - npt, 0), 0)),
            _const_spec((1, d)), _const_spec((1, d)), _const_spec((d, d)), _const_spec((1, d)),
            _const_spec((PLE_DIM, d)),
        ],
        out_specs=[
            pl.BlockSpec((TLF, d), lambda i: (jnp.minimum(i, npt - 1), 0)),
            pl.BlockSpec((TLF, d), lambda i: (jnp.maximum(i - npt, 0), 0)),
        ],
        out_shape=[jax.ShapeDtypeStruct((n_p, d), F32), jax.ShapeDtypeStruct((n_s, d), F32)],
        compiler_params=pltpu.CompilerParams(
            dimension_semantics=("arbitrary",), vmem_limit_bytes=VMEM_LIMIT),
        name="moe_combine_final",
    )(ys_all, x1_all, rw_all, pp2, ps2,
      row(ln2_g[0]), row(ln2_b[0]), w_pg_b, row(b_pg[0]), w_ple_b)

    return (y_p.reshape(bp, lp, d), y_s.reshape(bs, ls, d), ret_p, conv_p, ret_s, conv_s)
```

```python
import functools

import jax
import jax.numpy as jnp
import numpy as np
from jax import lax
from jax.experimental import pallas as pl
from jax.experimental.pallas import tpu as pltpu

F32 = jnp.float32
BF16 = jnp.bfloat16
I32 = jnp.int32

D_MODEL = 1024
PAST_LEN = 16384
RET_HEADS = 4
RET_DK = 128
RET_DV = 128
RET_QK = RET_HEADS * RET_DK
RET_V = RET_HEADS * RET_DV
RET_CHUNK = 128
ROPE_BASE = 10000.0
CONV_CH = 512
CONV_WIDTH = 31
N_GROUPS = 4
EXP_PER_GROUP = 4
N_EXPERTS = N_GROUPS * EXP_PER_GROUP
TOP_K = 2
EXP_FF = 512
PLE_DIM = 256
DEPTH = 1
ALPHA = (2 * DEPTH) ** 0.25
LN_EPS = 1e-5
IN_WIDTHS = (RET_QK, RET_QK, RET_V, RET_V, CONV_CH, CONV_CH, D_MODEL, D_MODEL)
IN_OFFS = tuple(int(s) for s in np.cumsum((0,) + IN_WIDTHS))

LANES = 128
SUBLANES = 8
VMEM_LIMIT = 56 * 1024 * 1024

TL = 256
TLM = 512
TLF = 512
BB_SAMPLE = 16
CHUNK = 2 * SUBLANES
TILE_CHUNKS = 32
BLOCK_USED = -(-(TOP_K * TL + N_EXPERTS * (CHUNK - 1)) // LANES) * LANES // CHUNK
BLOCK_SPARE = LANES // CHUNK
BLOCK_CHUNKS = BLOCK_USED + BLOCK_SPARE
USED_ROWS = BLOCK_USED * CHUNK
CAP = BLOCK_CHUNKS * CHUNK
TM_FFN = TILE_CHUNKS * CHUNK
FFN_COLS = 256
CONV_PAD = 32
XPAD_NEW = 32
XPAD_ROWS = 40


def _ln(x, g, b):
    mu = jnp.mean(x, axis=-1, keepdims=True)
    d = x - mu
    var = jnp.mean(d * d, axis=-1, keepdims=True)
    return d * lax.rsqrt(var + LN_EPS) * g + b


def _sigmoid(x):
    return 1.0 / (1.0 + jnp.exp(-x))


def _rep(v8, rows):
    return v8 if rows == SUBLANES else jnp.concatenate([v8] * (rows // SUBLANES), axis=0)


def _silu(x):
    return x * _sigmoid(x)


def _bdot(a, b):
    return jnp.dot(a.astype(BF16), b, preferred_element_type=F32)


def _rot(t, cosf, sinf):
    return t * cosf + pltpu.roll(t, RET_DK // 2, axis=1) * sinf


def _lane_tile(cols, rows):
    lane = lax.broadcasted_iota(I32, (rows, LANES), 1)
    out = jnp.zeros((rows, LANES), F32)
    for i, col in enumerate(cols):
        out = jnp.where(lane == i, col, out)
    return out


def _route(logits):
    lane = lax.broadcasted_iota(I32, logits.shape, 1)
    lanef = lane.astype(F32)
    ninf = jnp.float32(-jnp.inf)
    big = jnp.float32(LANES)
    gmask = lane < N_GROUPS
    gl = jnp.where(gmask, logits, ninf)
    gmax = jnp.max(gl, axis=1, keepdims=True)
    gidx = jnp.min(jnp.where(gmask & (gl == gmax), lanef, big), axis=1, keepdims=True)
    sumexp = jnp.sum(jnp.where(gmask, jnp.exp(gl - gmax), 0.0), axis=1, keepdims=True)
    gw = 1.0 / sumexp
    lo = N_GROUPS + EXP_PER_GROUP * gidx
    emask = (lanef >= lo) & (lanef < lo + EXP_PER_GROUP)
    el = jnp.where(emask, logits, ninf)
    m1 = jnp.max(el, axis=1, keepdims=True)
    i1 = jnp.min(jnp.where(emask & (el == m1), lanef, big), axis=1, keepdims=True)
    emask2 = emask & (lanef != i1)
    el2 = jnp.where(emask2, logits, ninf)
    m2 = jnp.max(el2, axis=1, keepdims=True)
    i2 = jnp.min(jnp.where(emask2 & (el2 == m2), lanef, big), axis=1, keepdims=True)
    t = jnp.exp(m2 - m1)
    den = 1.0 + t
    return (1.0 / den) * gw, (t / den) * gw, i1 - N_GROUPS, i2 - N_GROUPS


def _post_mix_pieces(src, w, sink, n_rows, parts=1):
    (w_ret_o, cln_g, cln_b, w_conv_o, w_out, ln1_g, ln1_b, wr_hi, wr_lo, b_r) = w
    st = {}
    pr = n_rows // parts

    def branch_a():
        st["a"] = _bdot(_silu(src["g"]()) * src["ret"](), w_ret_o[...])

    def branch_b():
        st["b"] = _bdot(_silu(_ln(src["cout"](), cln_g[...], cln_b[...])), w_conv_o[...])

    def merge(p):
        rows = slice(p * pr, (p + 1) * pr)
        mix = _sigmoid(src["gt_a"](rows)) * st["a"][rows] + _sigmoid(src["gt_b"](rows)) * st["b"][rows]
        h = ALPHA * src["x"](rows) + _bdot(mix, w_out[...])
        st["x1", p] = _ln(h, ln1_g[...], ln1_b[...])

    def router(p):
        x1 = st["x1", p]
        x1_hi = x1.astype(BF16)
        x1_lo = (x1 - x1_hi.astype(F32)).astype(BF16)
        both = jnp.dot(x1_hi, wr_lo[...], preferred_element_type=F32)
        st["logits", p] = (both[:, :LANES]
                           + (jnp.dot(x1_lo, wr_hi[...], preferred_element_type=F32) + both[:, LANES:])
                           + b_r[...])

    def route(p):
        st["route", p] = _route(st["logits", p])

    def finish(p):
        sink(p, st["x1", p], *st["route", p])

    staged = [lambda p=p, stage=stage: stage(p) for stage in (merge, router, route, finish) for p in range(parts)]
    return [branch_a, branch_b] + staged


def _sort_tile(x1, w1, w2, e1, e2, x1_ref, rw_ref, xs_ref, meta_ref):
    t = x1.shape[0]
    ids_t = _lane_tile((e1, e2), t).T
    e1r, e2r = ids_t[0:1, :], ids_t[1:2, :]
    sub = lax.broadcasted_iota(I32, (LANES, t), 0).astype(F32)
    a1 = (sub == e1r).astype(F32)
    a2 = (sub == e2r).astype(F32)
    ri = lax.broadcasted_iota(I32, (t, t), 0)
    ci = lax.broadcasted_iota(I32, (t, t), 1)
    earlier = (ri < ci).astype(BF16)
    r1 = jnp.dot(a1.astype(BF16), earlier, preferred_element_type=F32)
    r2 = jnp.dot(a2.astype(BF16), earlier, preferred_element_type=F32)
    cnt1 = jnp.sum(a1, axis=1, keepdims=True)
    cnt = cnt1 + jnp.sum(a2, axis=1, keepdims=True)
    nch = jnp.floor((cnt + (CHUNK - 1.0)) * (1.0 / CHUNK))
    ui = lax.broadcasted_iota(I32, (LANES, LANES), 0)
    uj = lax.broadcasted_iota(I32, (LANES, LANES), 1)
    before = (uj < ui).astype(BF16)
    off = jnp.dot(before, jnp.broadcast_to(nch, (LANES, LANES)).astype(BF16),
                  preferred_element_type=F32)[:, 0:1]
    base = off * CHUNK
    pos1r = jnp.sum(a1 * (base + r1), axis=0, keepdims=True)
    pos2r = jnp.sum(a2 * (base + cnt1 + r2), axis=0, keepdims=True)
    slot = lax.broadcasted_iota(I32, (USED_ROWS, t), 0).astype(F32)
    onehot = ((slot == pos1r) | (slot == pos2r)).astype(BF16)
    xs = jnp.dot(onehot, x1.astype(BF16), preferred_element_type=F32)
    pos_cols = jnp.where(sub == 2.0, pos1r, jnp.where(sub == 3.0, pos2r, 0.0)).T
    lane = lax.broadcasted_iota(I32, (t, LANES), 1)
    x1_ref[...] = x1
    rw_ref[...] = jnp.where(lane == 0, w1, jnp.where(lane == 1, w2, pos_cols))
    xs_ref[0:USED_ROWS, :] = xs.astype(BF16)
    xs_ref[USED_ROWS:CAP, :] = jnp.zeros((CAP - USED_ROWS, x1.shape[1]), BF16)
    mlane = lax.broadcasted_iota(I32, (LANES, LANES), 1)
    meta = jnp.where(mlane == 0, cnt, jnp.where(mlane == 1, off, 0.0))
    meta_ref[...] = meta.astype(I32)


def _sort_part(i, x1, w1, w2, e1, e2, x1_ref, rw_ref, xs_ref, meta_ref):
    _sort_tile(x1, w1, w2, e1, e2,
               x1_ref.at[pl.ds(i * TL, TL)], rw_ref.at[pl.ds(i * TL, TL)],
               xs_ref.at[pl.ds(i * CAP, CAP)], meta_ref.at[pl.ds(i * LANES, LANES)])


def _sort_tiles(x1, w1, w2, e1, e2, *outs):
    for i in range(x1.shape[0] // TL):
        rows = slice(i * TL, (i + 1) * TL)
        _sort_part(i, x1[rows], w1[rows], w2[rows], e1[rows], e2[rows], *outs)


GATE_COLS = {3: 0, 6: RET_V, 7: RET_V + D_MODEL}
QKV_COLS = {0: 0, 1: RET_QK, 2: 2 * RET_QK}


def _prompt_mixer_kernel(x_ref, x1s_ref, rws_ref, cos_ref, sin_ref, dec_ref, qdec_ref, kdec_ref, cdec_ref,
                         w_in, b_in, gn_g, gn_b, w_ret_o, conv_w, conv_b, cln_g, cln_b,
                         w_conv_o, w_out, ln1_g, ln1_b, wr_hi, wr_lo, b_r,
                         x1_ref, rw_ref, xs_ref, meta_ref, sret_ref, sconv_ref,
                         ubuf, ushift, qkv_scr, xb_scr, ret_scr, cout_scr, gate_scr,
                         *, n_tiles, tiles_per_seq):
    s = pl.program_id(0)
    li = lax.rem(s, tiles_per_seq)
    outs = (x1_ref, rw_ref, xs_ref, meta_ref)
    slot = dict(ret=ret_scr, cout=cout_scr, gates=gate_scr)
    tail_w = (w_ret_o, cln_g, cln_b, w_conv_o, w_out, ln1_g, ln1_b, wr_hi, wr_lo, b_r)

    @pl.when((s < n_tiles) & (li == 0))
    def _new_sequence():
        sret_ref[...] = jnp.zeros(sret_ref.shape, F32)
        ubuf[0:CONV_PAD, :] = jnp.zeros((CONV_PAD, CONV_CH), F32)

    @pl.when(s < n_tiles)
    def _mix():
        gcols = lambda kk: slice(GATE_COLS[kk], GATE_COLS[kk] + IN_WIDTHS[kk])
        src = dict(x=lambda r: x_ref[0, r, :], ret=lambda: ret_scr[...], cout=lambda: cout_scr[...],
                   g=lambda: gate_scr[:, gcols(3)], gt_a=lambda r: gate_scr[r, gcols(6)],
                   gt_b=lambda r: gate_scr[r, gcols(7)])
        head = _prompt_head_pieces(x_ref, cos_ref, sin_ref, dec_ref, qdec_ref, kdec_ref, cdec_ref,
                                   w_in, b_in, gn_g, gn_b, conv_w, conv_b, sret_ref,
                                   ubuf, ushift, qkv_scr, xb_scr, slot)
        tl = x_ref.shape[1]
        tail = _post_mix_pieces(src, tail_w, lambda *r: _sort_part(*r, *outs), tl, parts=tl // TL)
        for piece in head + tail:
            piece()

    @pl.when(s >= n_tiles)
    def _append():
        rws = rws_ref[...]
        _sort_tiles(x1s_ref[...], rws[:, 0:1], rws[:, 1:2], rws[:, 2:3], rws[:, 3:4], *outs)

    @pl.when((s < n_tiles) & (li == tiles_per_seq - 1))
    def _conv_state():
        sconv_ref[0, 0] = ubuf[CONV_PAD - (CONV_WIDTH - 1):CONV_PAD, :]


def _prompt_head_pieces(x_ref, cos_ref, sin_ref, dec_ref, qdec_ref, kdec_ref, cdec_ref,
                        w_in, b_in, gn_g, gn_b, conv_w, conv_b, sret_ref,
                        ubuf, ushift, qkv_scr, xb_scr, slot):
    tl = x_ref.shape[1]
    st = {}

    def slab_dot(c0, c1):
        return jnp.dot(xb_scr[...], w_in[:, c0:c1], preferred_element_type=F32) + _rep(b_in[:, c0:c1], tl)

    def glu():
        xb_scr[...] = x_ref[0].astype(BF16)
        u = slab_dot(IN_OFFS[4], IN_OFFS[5]) * _sigmoid(slab_dot(IN_OFFS[5], IN_OFFS[6]))
        ubuf[CONV_PAD:CONV_PAD + tl, :] = u

    nsh = ushift.shape[1]
    span = nsh - (CONV_PAD - SUBLANES)

    def shift_copy(h, s):
        ushift[s - 1] = ubuf[h * span + s:h * span + s + nsh, :]

    slab = 256
    slabs = [(kk, c0) for kk in (0, 1, 2, 3, 6, 7) for c0 in range(IN_OFFS[kk], IN_OFFS[kk + 1], slab)]
    rb = 32
    nrb = tl // rb

    def proj_slab(kk, c0):
        val = slab_dot(c0, c0 + slab)
        if kk in QKV_COLS:
            dst = QKV_COLS[kk] + c0 - IN_OFFS[kk]
            qkv_scr[:, dst:dst + slab] = val
        else:
            dst = GATE_COLS[kk] + c0 - IN_OFFS[kk]
            slot["gates"][:, dst:dst + slab] = val

    def conv_block(r):
        h, rl = divmod(r * rb, span)
        acc = jnp.zeros((rb, CONV_CH), F32) + conv_b[...]
        for j in range(CONV_WIDTH):
            off = j + (CONV_PAD - (CONV_WIDTH - 1))
            s = off % SUBLANES
            base = rl + off - s
            win = (ubuf[h * span + base:h * span + base + rb, :] if s == 0
                   else ushift[s - 1, base:base + rb, :])
            acc = acc + _rep(conv_w[j], rb) * win
        slot["cout"][r * rb:(r + 1) * rb, :] = acc
        if r == nrb - 1:
            ubuf[0:CONV_PAD, :] = ubuf[tl:tl + CONV_PAD, :]

    scale = RET_DK ** -0.5

    def retention(c, h):
        rows = slice(c * RET_CHUNK, (c + 1) * RET_CHUNK)
        cols = slice(h * RET_DK, (h + 1) * RET_DK)
        hcol = lambda kk: slice(QKV_COLS[kk] + h * RET_DK, QKV_COLS[kk] + (h + 1) * RET_DK)
        cosf = cos_ref[rows, :]
        sinf = sin_ref[rows, :]
        qh = _rot(qkv_scr[rows, hcol(0)], cosf, sinf)
        kh = _rot(qkv_scr[rows, hcol(1)], cosf, sinf) * scale
        qb = qh.astype(BF16)
        kb = kh.astype(BF16)
        vb = qkv_scr[rows, hcol(2)].astype(BF16)
        s_old = sret_ref[0, 0, h]
        scores = lax.dot_general(qb, kb, (((1,), (1,)), ((), ())),
                                 preferred_element_type=F32) * dec_ref[h]
        inner = jnp.dot(scores.astype(BF16), vb, preferred_element_type=F32)
        cross = jnp.dot(qb, s_old.astype(BF16), preferred_element_type=F32) * qdec_ref[h]
        kd = (kh * kdec_ref[h]).astype(BF16)
        s_new = cdec_ref[h] * s_old + lax.dot_general(
            kd, vb, (((0,), (0,)), ((), ())), preferred_element_type=F32)
        sret_ref[0, 0, h] = s_new
        slot["ret"][rows, cols] = _ln(inner + cross, gn_g[:, cols], gn_b[:, cols])

    vector_pieces = []
    for r in range(nrb):
        if (r * rb) % span == 0:
            vector_pieces += [lambda h=(r * rb) // span, s=s: shift_copy(h, s) for s in range(1, SUBLANES)]
        vector_pieces.append(lambda r=r: conv_block(r))
    def spread(work, between):
        out = []
        for i, piece in enumerate(work):
            out.append(piece)
            for kk, c0 in between[i * len(between) // len(work):(i + 1) * len(between) // len(work)]:
                out.append(lambda kk=kk, c0=c0: proj_slab(kk, c0))
        return out

    early = [sl for sl in slabs if sl[0] in (0, 1, 2, 3)]
    late = [sl for sl in slabs if sl[0] not in (0, 1, 2, 3)]
    ret_pieces = [lambda c=c, h=h: retention(c, h) for c in range(tl // RET_CHUNK) for h in range(RET_HEADS)]
    return [glu] + spread(vector_pieces, early) + spread(ret_pieces, late)


def _sample_mixer_kernel(x_ref, cos_ref, sin_ref, pdec_ref, qdec_ref, kdec_ref, cdec_ref, conv_w,
                         sret_in, sconv_in,
                         w_in, b_in, gn_g, gn_b, w_ret_o, conv_b, cln_g, cln_b,
                         w_conv_o, w_out, ln1_g, ln1_b, wr_hi, wr_lo, b_r,
                         x1_ref, rw_ref, sret_ref, sconv_ref,
                         ret_scr, cout_scr, gate_scr, xwin, u_scr):
    i = pl.program_id(0)
    t = cos_ref.shape[0]
    ls = t // BB_SAMPLE
    nstate = CONV_WIDTH - 1
    r0 = pl.multiple_of(i * t, t)
    xb = x_ref[pl.ds(r0, t), :].astype(BF16)

    def proj(k):
        c0, c1 = IN_OFFS[k], IN_OFFS[k + 1]
        return jnp.dot(xb, w_in[:, c0:c1], preferred_element_type=F32) + _rep(b_in[:, c0:c1], t)

    q = proj(0)
    k = proj(1)
    v = proj(2)
    scale = RET_DK ** -0.5
    cosf = cos_ref[...]
    sinf = sin_ref[...]
    row = lax.broadcasted_iota(I32, (t, RET_DK), 0)
    pos = row % ls
    row8 = lax.broadcasted_iota(I32, (SUBLANES, RET_DK), 0)
    per_tile = SUBLANES // ls
    for h in range(RET_HEADS):
        cols = slice(h * RET_DK, (h + 1) * RET_DK)
        qh = _rot(q[:, cols], cosf, sinf)
        kh = _rot(k[:, cols], cosf, sinf) * scale
        vh = v[:, cols]
        inner = jnp.zeros((t, RET_DV), F32)
        for s in range(ls):
            ks = kh if s == 0 else pltpu.roll(kh, s, axis=0)
            vs = vh if s == 0 else pltpu.roll(vh, s, axis=0)
            dotp = jnp.sum(qh * ks, axis=1, keepdims=True) * pdec_ref[h, s]
            inner = inner + jnp.where(pos >= s, dotp, 0.0) * vs
        kd = kh * kdec_ref[h]
        for tile in range(t // SUBLANES):
            rows = slice(tile * SUBLANES, (tile + 1) * SUBLANES)
            q8 = qh[rows, :]
            kd8 = kd[rows, :]
            v8 = vh[rows, :]
            seqs = [tile * per_tile + sub for sub in range(per_tile)]
            mine = [(row8 >= sub * ls) & (row8 < (sub + 1) * ls) for sub in range(per_tile)]
            s_old = [sret_in[0, b, h] for b in seqs]
            c_all = jnp.dot(q8, jnp.concatenate(s_old, axis=1), preferred_element_type=F32)
            upd = lax.dot_general(jnp.concatenate([jnp.where(m, kd8, 0.0) for m in mine], axis=1), v8,
                                  (((0,), (0,)), ((), ())), preferred_element_type=F32)
            cross8 = jnp.zeros((SUBLANES, RET_DV), F32)
            for sub, b in enumerate(seqs):
                cross8 = jnp.where(mine[sub], c_all[:, sub * RET_DV:(sub + 1) * RET_DV], cross8)
                sret_ref[0, b, h] = cdec_ref[h] * s_old[sub] + upd[sub * RET_DK:(sub + 1) * RET_DK, :]
            ret_scr[pl.ds(r0 + tile * SUBLANES, SUBLANES), cols] = (
                inner[rows, :] + cross8 * qdec_ref[h, rows, :])
        ret_scr[pl.ds(r0, t), cols] = _ln(ret_scr[pl.ds(r0, t), cols], gn_g[:, cols], gn_b[:, cols])

    u = proj(4) * _sigmoid(proj(5))
    nslab = CONV_CH // LANES
    xwin[0:nstate] = sconv_in[...]
    for sl in range(nslab):
        u_scr[sl] = u[:, sl * LANES:(sl + 1) * LANES]
    for p in range(ls):
        for sl in range(nslab):
            xwin[XPAD_NEW + p, :, sl * LANES:(sl + 1) * LANES] = u_scr[sl, pl.ds(p, BB_SAMPLE, stride=ls), :]
    win_row = lambda m: m if m < nstate else XPAD_NEW + (m - nstate)
    for p in range(ls):
        res = jnp.zeros((BB_SAMPLE, CONV_CH), F32) + conv_b[...]
        for j in range(CONV_WIDTH):
            res = res + _rep(conv_w[j], BB_SAMPLE) * xwin[win_row(p + j)]
        for sl in range(nslab):
            cout_scr[sl, pl.ds(r0 + p, BB_SAMPLE, stride=ls), :] = res[:, sl * LANES:(sl + 1) * LANES]
    sconv_ref[0:nstate - ls] = xwin[ls:nstate]
    sconv_ref[nstate - ls:nstate] = xwin[XPAD_NEW:XPAD_NEW + ls]
    for kk in (3, 6, 7):
        gate_scr[pl.ds(r0, t), GATE_COLS[kk]:GATE_COLS[kk] + IN_WIDTHS[kk]] = proj(kk)

    @pl.when(i == pl.num_programs(0) - 1)
    def _second_half():
        n = x_ref.shape[0]

        def sink(p, x1, w1, w2, e1, e2):
            x1_ref[...] = x1
            rw_ref[...] = _lane_tile((w1, w2, e1, e2), n)

        gcols = lambda kk: slice(GATE_COLS[kk], GATE_COLS[kk] + IN_WIDTHS[kk])
        src = dict(x=lambda r: x_ref[r, :], ret=lambda: ret_scr[...],
                   cout=lambda: jnp.concatenate([cout_scr[sl] for sl in range(CONV_CH // LANES)], axis=1),
                   g=lambda: gate_scr[:, gcols(3)], gt_a=lambda r: gate_scr[r, gcols(6)],
                   gt_b=lambda r: gate_scr[r, gcols(7)])
        for piece in _post_mix_pieces(
                src, (w_ret_o, cln_g, cln_b, w_conv_o, w_out, ln1_g, ln1_b, wr_hi, wr_lo, b_r), sink, n):
            piece()


def _ffn_kernel(te_ref, nvalid_ref, nexte_ref, chunk_ref, xs_hbm, wgu_hbm, wdn_hbm, ys_hbm,
                xbuf, obuf, wgu_f, wdn_f, wgu_b, wdn_b, sem_in, sem_out, sem_w):
    del xs_hbm
    nvalid = nvalid_ref[0]

    def chunk_rows(tile, c):
        return pl.ds(pl.multiple_of(chunk_ref[tile * TILE_CHUNKS + c] * CHUNK, CHUNK), CHUNK)

    def start_in(tile, s):
        for c in range(TILE_CHUNKS):
            pltpu.make_async_copy(ys_hbm.at[chunk_rows(tile, c)],
                                  xbuf.at[s, pl.ds(c * CHUNK, CHUNK)], sem_in.at[s]).start()

    def start_out(tile, s):
        for c in range(TILE_CHUNKS):
            pltpu.make_async_copy(obuf.at[s, pl.ds(c * CHUNK, CHUNK)],
                                  ys_hbm.at[chunk_rows(tile, c)], sem_out.at[s]).start()

    def wait_in(s):
        pltpu.make_async_copy(ys_hbm.at[pl.ds(0, TM_FFN)], xbuf.at[s], sem_in.at[s]).wait()

    def wait_out(s):
        pltpu.make_async_copy(obuf.at[s], ys_hbm.at[pl.ds(0, TM_FFN)], sem_out.at[s]).wait()

    def weight_copies(e, s):
        return (pltpu.make_async_copy(wgu_hbm.at[e], wgu_f.at[s], sem_w.at[s]),
                pltpu.make_async_copy(wdn_hbm.at[e], wdn_f.at[s], sem_w.at[s]))

    def tile(i, wslot):
        slot = i % 2

        @pl.when(i >= 2)
        def _retire():
            wait_out(slot)

        wait_in(slot)
        changed = (i == 0) | (te_ref[i] != te_ref[jnp.maximum(i - 1, 0)])

        @pl.when(changed)
        def _new_expert():
            for cp in weight_copies(te_ref[i], wslot):
                cp.wait()
            wgu_b[...] = wgu_f[wslot].astype(BF16)
            wdn_b[...] = wdn_f[wslot].astype(BF16)

            @pl.when(nexte_ref[i] >= 0)
            def _():
                for cp in weight_copies(nexte_ref[i], 1 - wslot):
                    cp.start()

        x = xbuf[slot]
        y = jnp.zeros((TM_FFN, wdn_b.shape[1]), F32)
        for c0 in range(0, EXP_FF, FFN_COLS):
            hg = jnp.dot(x, wgu_b[:, c0:c0 + FFN_COLS], preferred_element_type=F32)
            hu = jnp.dot(x, wgu_b[:, EXP_FF + c0:EXP_FF + c0 + FFN_COLS], preferred_element_type=F32)
            y = y + _bdot(_silu(hg) * hu, wdn_b[c0:c0 + FFN_COLS, :])
            if c0 == 0:
                start_in(jnp.where(i + 1 < nvalid, i + 1, 0), 1 - slot)
        obuf[slot] = y.astype(BF16)
        start_out(i, slot)
        return jnp.where(changed, 1 - wslot, wslot)

    for cp in weight_copies(te_ref[0], 0):
        cp.start()
    start_in(0, 0)
    lax.fori_loop(0, nvalid, tile, jnp.int32(0))
    wait_out(nvalid % 2)
    wait_out(1 - nvalid % 2)
    wait_in(nvalid % 2)


def _final_kernel(ys_ref, x1_ref, rw_ref, pp_ref, ps_ref, ln2_g, ln2_b, w_pg, b_pg, w_ple,
                  yp_ref, ys_out_ref, *, n_prompt_tiles):
    i = pl.program_id(0)
    x1 = x1_ref[...]
    slot = lax.broadcasted_iota(I32, (TL, USED_ROWS), 1).astype(F32)
    parts = []
    for b in range(x1.shape[0] // TL):
        rw = rw_ref[b * TL:(b + 1) * TL, :]
        w1, w2, pos1, pos2 = rw[:, 0:1], rw[:, 1:2], rw[:, 2:3], rw[:, 3:4]
        ys = ys_ref[b * CAP:b * CAP + USED_ROWS, :]
        comb = jnp.where(slot == pos1, w1, jnp.where(slot == pos2, w2, 0.0)).astype(BF16)
        parts.append(jnp.dot(comb, ys, preferred_element_type=F32))
    moe = parts[0] if len(parts) == 1 else jnp.concatenate(parts, axis=0)
    x2 = _ln(ALPHA * x1 + moe, ln2_g[...], ln2_b[...])
    gate = _sigmoid(_bdot(x2, w_pg[...]) + b_pg[...])
    p = jnp.where(i < n_prompt_tiles, pp_ref[...], ps_ref[...])
    y = x2 + gate * _bdot(p, w_ple[...])

    @pl.when(i < n_prompt_tiles)
    def _prompt():
        yp_ref[...] = y

    @pl.when(i >= n_prompt_tiles)
    def _sample():
        ys_out_ref[...] = y


def _rope_tables(pos):
    half = RET_DK // 2
    inv_freq = ROPE_BASE ** (-np.arange(half, dtype=np.float64) / half)
    ang = np.asarray(pos, np.float64)[:, None] * inv_freq[None, :]
    cos = np.cos(ang)
    sin = np.sin(ang)
    return (np.concatenate([cos, cos], axis=-1).astype(np.float32),
            np.concatenate([-sin, sin], axis=-1).astype(np.float32))


def _log_gamma():
    return np.log(1.0 - 2.0 ** (-5.0 - np.arange(RET_HEADS, dtype=np.float64)))


def _const_spec(shape):
    nd = len(shape)
    return pl.BlockSpec(shape, lambda *_: (0,) * nd, pipeline_mode=pl.Buffered(1))


def _chunk_plan(meta, n_blocks, n_ffn_tiles):
    assert n_blocks * BLOCK_SPARE >= N_EXPERTS * (TILE_CHUNKS - 1)
    m = meta.reshape(n_blocks, LANES, LANES)
    cnt = m[:, :N_EXPERTS, 0]
    off = m[:, :N_EXPERTS, 1]
    nch = (cnt + (CHUNK - 1)) // CHUNK
    cum = jnp.cumsum(nch, axis=0)
    total = cum[-1:]
    tiles_e = (total + TILE_CHUNKS - 1) // TILE_CHUNKS
    tile_end = jnp.cumsum(tiles_e, axis=1)
    tile_start = tile_end - tiles_e
    tid = jnp.arange(n_ffn_tiles, dtype=I32)[:, None]
    owner = (tid >= tile_start) & (tid < tile_end)
    pick_e = lambda v: jnp.sum(jnp.where(owner, v, 0), axis=1, keepdims=True)
    te = pick_e(jnp.arange(N_EXPERTS, dtype=I32)[None, :])
    k = (tid - pick_e(tile_start)) * TILE_CHUNKS + jnp.arange(TILE_CHUNKS, dtype=I32)[None, :]
    total_t = pick_e(total)
    real = k < total_t
    by_tile = lambda v: jnp.sum(jnp.where(owner[:, None, :], v[None, :, :], 0), axis=2)
    cum_t = by_tile(cum)
    blk = jnp.minimum(jnp.sum((cum_t[:, None, :] <= k[:, :, None]).astype(I32), axis=2), n_blocks - 1)
    at_blk = blk[:, :, None] == jnp.arange(n_blocks, dtype=I32)[None, None, :]
    pick_b = lambda v: jnp.sum(jnp.where(at_blk, v[:, None, :], 0), axis=2)
    excl = pick_b(cum_t - by_tile(nch))
    off_t = pick_b(by_tile(off))
    spare = te * (TILE_CHUNKS - 1) + jnp.maximum(k - total_t, 0) % TILE_CHUNKS
    spare_chunk = (spare // BLOCK_SPARE) * BLOCK_CHUNKS + BLOCK_USED + spare % BLOCK_SPARE
    chunk = jnp.where(real, blk * BLOCK_CHUNKS + off_t + (k - excl), spare_chunk)
    n_valid = jnp.sum(tiles_e, axis=1)
    eid = jnp.arange(N_EXPERTS, dtype=I32)
    later = (eid[None, :] > eid[:, None]) & (tiles_e > 0)
    next_e = jnp.min(jnp.where(later, eid[None, :], N_EXPERTS), axis=1)[None, :]
    next_t = pick_e(jnp.where(next_e < N_EXPERTS, next_e, -1))
    return (te.reshape(-1).astype(I32), n_valid.astype(I32), next_t.reshape(-1).astype(I32),
            chunk.reshape(-1).astype(I32))


def kernel(x_prompt, x_sample, state_ret, state_conv, p_prompt, p_sample, w_in, b_in, ret_gn_g, ret_gn_b,
           w_ret_o, conv_w, conv_b, conv_ln_g, conv_ln_b, w_conv_o, w_out, ln1_g, ln1_b, w_grp, b_grp,
           w_exp, b_exp, w_gu, w_dn, ln2_g, ln2_b, w_pg, b_pg, w_ple):
    assert DEPTH == 1 and w_in.shape[0] == 1
    bp, lp, d = x_prompt.shape
    bs, ls, _ = x_sample.shape
    n_p, n_s = bp * lp, bs * ls
    n_tok = n_p + n_s
    assert lp % TL == 0 and n_s % TL == 0 and bs % BB_SAMPLE == 0 and SUBLANES % ls == 0
    n_blocks = n_tok // TL

    f32c = lambda a, shape: jnp.asarray(np.broadcast_to(a, shape).astype(np.float32))
    lg = _log_gamma()
    c = RET_CHUNK
    idx = np.arange(c, dtype=np.float64)
    rel = idx[:, None] - idx[None, :]
    causal = rel >= 0
    decay = np.where(causal[None], np.exp(np.where(causal, rel, 0.0)[None] * lg[:, None, None]), 0.0)
    decay = f32c(decay, decay.shape)
    q_decay = np.exp((idx[:, None] + 1.0) * lg[None, :])
    k_decay = np.exp((c - 1.0 - idx[:, None]) * lg[None, :])
    chunk_decay = np.exp(c * lg)
    qdec_p = f32c(q_decay.T[:, :, None], (RET_HEADS, c, RET_DK))
    kdec_p = f32c(k_decay.T[:, :, None], (RET_HEADS, c, RET_DK))
    cdec_p = f32c(chunk_decay[:, None, None], (RET_HEADS, 1, RET_DV))
    cos_p, sin_p = (jnp.asarray(a) for a in _rope_tables(np.arange(lp)))

    ts = BB_SAMPLE * ls
    idx_s = np.arange(ls, dtype=np.float64)
    pdec_s = np.exp(idx_s[None, :] * lg[:, None])
    pdec_s = f32c(pdec_s[:, :, None, None], (RET_HEADS, ls, 1, RET_DK))
    qd_s = np.exp((idx_s[:, None] + 1.0) * lg[None, :])
    kd_s = np.exp((ls - 1.0 - idx_s[:, None]) * lg[None, :])
    qdec_s = f32c(np.tile(qd_s.T, (1, BB_SAMPLE))[:, :, None], (RET_HEADS, ts, RET_DK))
    kdec_s = f32c(np.tile(kd_s.T, (1, BB_SAMPLE))[:, :, None], (RET_HEADS, ts, RET_DK))
    cdec_s = f32c(np.exp(ls * lg)[:, None, None], (RET_HEADS, 1, RET_DV))
    cos_s, sin_s = (jnp.asarray(a) for a in _rope_tables(np.tile(PAST_LEN + np.arange(ls), BB_SAMPLE)))

    w_in_b = w_in[0].astype(BF16)
    w_ret_o_b = w_ret_o[0].astype(BF16)
    w_conv_o_b = w_conv_o[0].astype(BF16)
    w_out_b = w_out[0].astype(BF16)
    w_pg_b = w_pg[0].astype(BF16)
    w_ple_b = w_ple[0].astype(BF16)
    n_route = N_GROUPS + N_EXPERTS
    w_r = jnp.concatenate([w_grp[0], w_exp[0], jnp.zeros((d, LANES - n_route), F32)], axis=1)
    wr_hi = w_r.astype(BF16)
    wr_lo = jnp.concatenate([wr_hi, (w_r - wr_hi.astype(F32)).astype(BF16)], axis=1)
    b_r = jnp.concatenate([b_grp[0], b_exp[0], jnp.zeros((LANES - n_route,), F32)]).reshape(1, LANES)
    row = lambda a: a.reshape(1, -1)
    conv_w0 = conv_w[0]
    nstate = CONV_WIDTH - 1

    rep8 = lambda a: jnp.broadcast_to(a[..., None, :], a.shape[:-1] + (SUBLANES, a.shape[-1]))
    shared_w = (w_in_b, rep8(b_in[0]), row(ret_gn_g[0]), row(ret_gn_b[0]), w_ret_o_b)
    tail_w = (row(conv_ln_g[0]), row(conv_ln_b[0]), w_conv_o_b, w_out_b, row(ln1_g[0]), row(ln1_b[0]),
              wr_hi, wr_lo, b_r)

    nbt = bs // BB_SAMPLE
    xs2 = x_sample.reshape(n_s, d)
    conv_state_t = jnp.transpose(state_conv[0], (1, 0, 2))
    sample_in = ((xs2, cos_s, sin_s, pdec_s, qdec_s, kdec_s, cdec_s, rep8(conv_w0), state_ret, conv_state_t)
                 + shared_w + (row(conv_b[0]),) + tail_w)
    conv_state_spec = pl.BlockSpec((nstate, BB_SAMPLE, CONV_CH), lambda i: (0, i, 0))
    sample_specs = (
        [_const_spec(a.shape) for a in sample_in[0:8]]
        + [pl.BlockSpec((1, BB_SAMPLE, RET_HEADS, RET_DK, RET_DV), lambda i: (0, i, 0, 0, 0)), conv_state_spec]
        + [_const_spec(a.shape) for a in sample_in[10:]]
    )
    tok_spec_s = lambda w: pl.BlockSpec((n_s, w), lambda i: (0, 0))
    x1_s, rw_s, ret_s, conv_s_t = pl.pallas_call(
        _sample_mixer_kernel,
        grid=(nbt,),
        in_specs=sample_specs,
        out_specs=[
            tok_spec_s(d), tok_spec_s(LANES),
            pl.BlockSpec((1, BB_SAMPLE, RET_HEADS, RET_DK, RET_DV), lambda i: (0, i, 0, 0, 0)),
            conv_state_spec,
        ],
        out_shape=[
            jax.ShapeDtypeStruct((n_s, d), F32),
            jax.ShapeDtypeStruct((n_s, LANES), F32),
            jax.ShapeDtypeStruct(state_ret.shape, F32),
            jax.ShapeDtypeStruct(conv_state_t.shape, F32),
        ],
        scratch_shapes=[
            pltpu.VMEM((n_s, RET_V), F32),
            pltpu.VMEM((CONV_CH // LANES, n_s, LANES), F32),
            pltpu.VMEM((n_s, RET_V + 2 * D_MODEL), F32),
            pltpu.VMEM((XPAD_ROWS, BB_SAMPLE, CONV_CH), F32),
            pltpu.VMEM((CONV_CH // LANES, ts, LANES), F32),
        ],
        compiler_params=pltpu.CompilerParams(
            dimension_semantics=("arbitrary",), vmem_limit_bytes=VMEM_LIMIT),
        name="sample_mixer",
    )(*sample_in)
    conv_s = jnp.transpose(conv_s_t, (1, 0, 2))[None]

    assert lp % TLM == 0 and n_s % TLM == 0 and TLM % TL == 0
    nlt = lp // TLM
    npt = n_p // TLM
    nst = n_s // TLM
    sub = TLM // TL
    prompt_in = ((x_prompt, x1_s, rw_s, cos_p, sin_p, decay, qdec_p, kdec_p, cdec_p)
                 + shared_w + (rep8(conv_w0), row(conv_b[0])) + tail_w)
    head_tile = lambda s: jnp.minimum(s, npt - 1)
    sample_tile = lambda s: jnp.maximum(s - npt, 0)
    sample_spec = lambda w: pl.BlockSpec((TLM, w), lambda s: (sample_tile(s), 0))
    prompt_specs = [
        pl.BlockSpec((1, TLM, d), lambda s: (head_tile(s) // nlt, head_tile(s) % nlt, 0)),
        sample_spec(d), sample_spec(LANES),
        pl.BlockSpec((TLM, RET_DK), lambda s: (head_tile(s) % nlt, 0)),
        pl.BlockSpec((TLM, RET_DK), lambda s: (head_tile(s) % nlt, 0)),
    ] + [_const_spec(a.shape) for a in prompt_in[5:]]
    tok_spec_p = lambda rows, w: pl.BlockSpec((rows, w), lambda s: (s, 0))
    x1_all, rw_all, xs_all, meta, ret_p, conv_p = pl.pallas_call(
        functools.partial(_prompt_mixer_kernel, n_tiles=npt, tiles_per_seq=nlt),
        grid=(npt + nst,),
        in_specs=prompt_specs,
        out_specs=[
            tok_spec_p(TLM, d), tok_spec_p(TLM, LANES), tok_spec_p(sub * CAP, d), tok_spec_p(sub * LANES, LANES),
            pl.BlockSpec((1, 1, RET_HEADS, RET_DK, RET_DV), lambda s: (0, head_tile(s) // nlt, 0, 0, 0)),
            pl.BlockSpec((1, 1, nstate, CONV_CH), lambda s: (0, head_tile(s) // nlt, 0, 0)),
        ],
        out_shape=[
            jax.ShapeDtypeStruct((n_tok, d), F32),
            jax.ShapeDtypeStruct((n_tok, LANES), F32),
            jax.ShapeDtypeStruct((n_blocks * CAP, d), BF16),
            jax.ShapeDtypeStruct((n_blocks * LANES, LANES), I32),
            jax.ShapeDtypeStruct((1, bp, RET_HEADS, RET_DK, RET_DV), F32),
            jax.ShapeDtypeStruct((1, bp, nstate, CONV_CH), F32),
        ],
        scratch_shapes=[
            pltpu.VMEM((TLM + CONV_PAD, CONV_CH), F32),
            pltpu.VMEM((SUBLANES - 1, TL + CONV_PAD - SUBLANES, CONV_CH), F32),
            pltpu.VMEM((TLM, 2 * RET_QK + RET_V), F32),
            pltpu.VMEM((TLM, d), BF16),
            pltpu.VMEM((TLM, RET_V), F32),
            pltpu.VMEM((TLM, CONV_CH), F32),
            pltpu.VMEM((TLM, RET_V + 2 * D_MODEL), F32),
        ],
        compiler_params=pltpu.CompilerParams(
            dimension_semantics=("arbitrary",), vmem_limit_bytes=VMEM_LIMIT),
        name="prompt_mixer",
    )(*prompt_in)

    assert TOP_K * n_tok // TM_FFN >= 3
    max_chunks = n_blocks * (TOP_K * TL // CHUNK + N_EXPERTS - 1)
    n_ffn_tiles = (max_chunks + N_EXPERTS * (TILE_CHUNKS - 1)) // TILE_CHUNKS
    tile_e, n_valid_tiles, next_e, chunk_ids = _chunk_plan(meta, n_blocks, n_ffn_tiles)

    ys_all = pl.pallas_call(
        _ffn_kernel,
        grid_spec=pltpu.PrefetchScalarGridSpec(
            num_scalar_prefetch=4,
            grid=(1,),
            in_specs=[pl.BlockSpec(memory_space=pl.ANY)] * 3,
            out_specs=pl.BlockSpec(memory_space=pl.ANY),
            scratch_shapes=[
                pltpu.VMEM((2, TM_FFN, d), BF16),
                pltpu.VMEM((2, TM_FFN, d), BF16),
                pltpu.VMEM((2, d, 2 * EXP_FF), F32),
                pltpu.VMEM((2, EXP_FF, d), F32),
                pltpu.VMEM((d, 2 * EXP_FF), BF16),
                pltpu.VMEM((EXP_FF, d), BF16),
                pltpu.SemaphoreType.DMA((2,)),
                pltpu.SemaphoreType.DMA((2,)),
                pltpu.SemaphoreType.DMA((2,)),
            ],
        ),
        out_shape=jax.ShapeDtypeStruct(xs_all.shape, BF16),
        input_output_aliases={4: 0},
        compiler_params=pltpu.CompilerParams(
            dimension_semantics=("arbitrary",), vmem_limit_bytes=VMEM_LIMIT),
        name="expert_ffn",
    )(tile_e, n_valid_tiles, next_e, chunk_ids, xs_all, w_gu[0], w_dn[0])

    assert n_p % TLF == 0 and n_s % TLF == 0
    npt = n_p // TLF
    fsub = TLF // TL
    pp2 = p_prompt.reshape(n_p, PLE_DIM)
    ps2 = p_sample.reshape(n_s, PLE_DIM)
    tok_f = lambda rows, w: pl.BlockSpec((rows, w), lambda i: (i, 0))
    y_p, y_s = pl.pallas_call(
        functools.partial(_final_kernel, n_prompt_tiles=npt),
        grid=(n_tok // TLF,),
        in_specs=[
            tok_f(fsub * CAP, d), tok_f(TLF, d), tok_f(TLF, LANES),
            pl.BlockSpec((TLF, PLE_DIM), lambda i: (jnp.minimum(i, npt - 1), 0)),
            pl.BlockSpec((TLF, PLE_DIM), lambda i: (jnp.maximum(i - npt, 0), 0)),
            _const_spec((1, d)), _const_spec((1, d)), _const_spec((d, d)), _const_spec((1, d)),
            _const_spec((PLE_DIM, d)),
        ],
        out_specs=[
            pl.BlockSpec((TLF, d), lambda i: (jnp.minimum(i, npt - 1), 0)),
            pl.BlockSpec((TLF, d), lambda i: (jnp.maximum(i - npt, 0), 0)),
        ],
        out_shape=[jax.ShapeDtypeStruct((n_p, d), F32), jax.ShapeDtypeStruct((n_s, d), F32)],
        compiler_params=pltpu.CompilerParams(
            dimension_semantics=("arbitrary",), vmem_limit_bytes=VMEM_LIMIT),
        name="moe_combine_final",
    )(ys_all, x1_all, rw_all, pp2, ps2,
      row(ln2_g[0]), row(ln2_b[0]), w_pg_b, row(b_pg[0]), w_ple_b)

    return (y_p.reshape(bp, lp, d), y_s.reshape(bs, ls, d), ret_p, conv_p, ret_s, conv_s)
```

```python
import functools

import jax
import jax.numpy as jnp
import numpy as np
from jax import lax
from jax.experimental import pallas as pl
from jax.experimental.pallas import tpu as pltpu

F32 = jnp.float32
BF16 = jnp.bfloat16
I32 = jnp.int32

D_MODEL = 1024
PAST_LEN = 16384
RET_HEADS = 4
RET_DK = 128
RET_DV = 128
RET_QK = RET_HEADS * RET_DK
RET_V = RET_HEADS * RET_DV
RET_CHUNK = 128
ROPE_BASE = 10000.0
CONV_CH = 512
CONV_WIDTH = 31
N_GROUPS = 4
EXP_PER_GROUP = 4
N_EXPERTS = N_GROUPS * EXP_PER_GROUP
TOP_K = 2
EXP_FF = 512
PLE_DIM = 256
DEPTH = 1
ALPHA = (2 * DEPTH) ** 0.25
LN_EPS = 1e-5
IN_WIDTHS = (RET_QK, RET_QK, RET_V, RET_V, CONV_CH, CONV_CH, D_MODEL, D_MODEL)
IN_OFFS = tuple(int(s) for s in np.cumsum((0,) + IN_WIDTHS))

LANES = 128
SUBLANES = 8
VMEM_LIMIT = 56 * 1024 * 1024

TL = 256
TLM = 512
TLF = 512
BB_SAMPLE = 16
CHUNK = 2 * SUBLANES
TILE_CHUNKS = 32
BLOCK_USED = -(-(TOP_K * TL + N_EXPERTS * (CHUNK - 1)) // LANES) * LANES // CHUNK
BLOCK_SPARE = LANES // CHUNK
BLOCK_CHUNKS = BLOCK_USED + BLOCK_SPARE
USED_ROWS = BLOCK_USED * CHUNK
CAP = BLOCK_CHUNKS * CHUNK
TM_FFN = TILE_CHUNKS * CHUNK
FFN_COLS = 256
CONV_PAD = 32
XPAD_NEW = 32
XPAD_ROWS = 40


def _ln(x, g, b):
    mu = jnp.mean(x, axis=-1, keepdims=True)
    d = x - mu
    var = jnp.mean(d * d, axis=-1, keepdims=True)
    return d * lax.rsqrt(var + LN_EPS) * g + b


def _sigmoid(x):
    return 1.0 / (1.0 + jnp.exp(-x))


def _rep(v8, rows):
    return v8 if rows == SUBLANES else jnp.concatenate([v8] * (rows // SUBLANES), axis=0)


def _silu(x):
    return x * _sigmoid(x)


def _bdot(a, b):
    return jnp.dot(a.astype(BF16), b, preferred_element_type=F32)


def _rot(t, cosf, sinf):
    return t * cosf + pltpu.roll(t, RET_DK // 2, axis=1) * sinf


def _lane_tile(cols, rows):
    lane = lax.broadcasted_iota(I32, (rows, LANES), 1)
    out = jnp.zeros((rows, LANES), F32)
    for i, col in enumerate(cols):
        out = jnp.where(lane == i, col, out)
    return out


def _route(logits):
    lane = lax.broadcasted_iota(I32, logits.shape, 1)
    lanef = lane.astype(F32)
    ninf = jnp.float32(-jnp.inf)
    big = jnp.float32(LANES)
    gmask = lane < N_GROUPS
    gl = jnp.where(gmask, logits, ninf)
    gmax = jnp.max(gl, axis=1, keepdims=True)
    gidx = jnp.min(jnp.where(gmask & (gl == gmax), lanef, big), axis=1, keepdims=True)
    sumexp = jnp.sum(jnp.where(gmask, jnp.exp(gl - gmax), 0.0), axis=1, keepdims=True)
    gw = 1.0 / sumexp
    lo = N_GROUPS + EXP_PER_GROUP * gidx
    emask = (lanef >= lo) & (lanef < lo + EXP_PER_GROUP)
    el = jnp.where(emask, logits, ninf)
    m1 = jnp.max(el, axis=1, keepdims=True)
    i1 = jnp.min(jnp.where(emask & (el == m1), lanef, big), axis=1, keepdims=True)
    emask2 = emask & (lanef != i1)
    el2 = jnp.where(emask2, logits, ninf)
    m2 = jnp.max(el2, axis=1, keepdims=True)
    i2 = jnp.min(jnp.where(emask2 & (el2 == m2), lanef, big), axis=1, keepdims=True)
    t = jnp.exp(m2 - m1)
    den = 1.0 + t
    return (1.0 / den) * gw, (t / den) * gw, i1 - N_GROUPS, i2 - N_GROUPS


def _post_mix_pieces(src, w, sink, n_rows, parts=1, carry=None):
    (w_ret_o, cln_g, cln_b, w_conv_o, w_out, ln1_g, ln1_b, wr_hi, wr_lo, b_r) = w
    st = {}
    pr = n_rows // parts

    def branch_a():
        st["a"] = _bdot(_silu(src["g"]()) * src["ret"](), w_ret_o[...])

    def branch_b():
        st["b"] = _bdot(_silu(_ln(src["cout"](), cln_g[...], cln_b[...])), w_conv_o[...])

    def merge(p):
        rows = slice(p * pr, (p + 1) * pr)
        mix = _sigmoid(src["gt_a"](rows)) * st["a"][rows] + _sigmoid(src["gt_b"](rows)) * st["b"][rows]
        h = ALPHA * src["x"](rows) + _bdot(mix, w_out[...])
        st["x1", p] = _ln(h, ln1_g[...], ln1_b[...])

    def router(p):
        x1 = st["x1", p]
        x1_hi = x1.astype(BF16)
        x1_lo = (x1 - x1_hi.astype(F32)).astype(BF16)
        both = jnp.dot(x1_hi, wr_lo[...], preferred_element_type=F32)
        st["logits", p] = (both[:, :LANES]
                           + (jnp.dot(x1_lo, wr_hi[...], preferred_element_type=F32) + both[:, LANES:])
                           + b_r[...])

    def route(p):
        st["route", p] = _route(st["logits", p])

    def finish(p):
        sink(p, st["x1", p], *st["route", p])

    def park(p):
        rows = slice(p * pr, (p + 1) * pr)
        carry[0][rows, :] = st["x1", p]
        carry[1][rows, :] = st["logits", p]

    stages = (merge, router, route, finish) if carry is None else (merge, router, park)
    return [branch_a, branch_b] + [lambda p=p, stage=stage: stage(p) for stage in stages for p in range(parts)]


def _parked_pieces(carry, sink, n_rows, parts):
    pr = n_rows // parts
    st = {}

    def route(p):
        st[p] = _route(carry[1][p * pr:(p + 1) * pr, :])

    def finish(p):
        sink(p, carry[0][p * pr:(p + 1) * pr, :], *st[p])

    return [lambda p=p, stage=stage: stage(p) for stage in (route, finish) for p in range(parts)]


def _sort_tile(x1, w1, w2, e1, e2, x1_ref, rw_ref, xs_ref, meta_ref):
    t = x1.shape[0]
    ids_t = _lane_tile((e1, e2), t).T
    e1r, e2r = ids_t[0:1, :], ids_t[1:2, :]
    sub = lax.broadcasted_iota(I32, (LANES, t), 0).astype(F32)
    a1 = (sub == e1r).astype(F32)
    a2 = (sub == e2r).astype(F32)
    ri = lax.broadcasted_iota(I32, (t, t), 0)
    ci = lax.broadcasted_iota(I32, (t, t), 1)
    earlier = (ri < ci).astype(BF16)
    r1 = jnp.dot(a1.astype(BF16), earlier, preferred_element_type=F32)
    r2 = jnp.dot(a2.astype(BF16), earlier, preferred_element_type=F32)
    cnt1 = jnp.sum(a1, axis=1, keepdims=True)
    cnt = cnt1 + jnp.sum(a2, axis=1, keepdims=True)
    nch = jnp.floor((cnt + (CHUNK - 1.0)) * (1.0 / CHUNK))
    ui = lax.broadcasted_iota(I32, (LANES, LANES), 0)
    uj = lax.broadcasted_iota(I32, (LANES, LANES), 1)
    before = (uj < ui).astype(BF16)
    off = jnp.dot(before, jnp.broadcast_to(nch, (LANES, LANES)).astype(BF16),
                  preferred_element_type=F32)[:, 0:1]
    base = off * CHUNK
    pos1r = jnp.sum(a1 * (base + r1), axis=0, keepdims=True)
    pos2r = jnp.sum(a2 * (base + cnt1 + r2), axis=0, keepdims=True)
    slot = lax.broadcasted_iota(I32, (USED_ROWS, t), 0).astype(F32)
    onehot = ((slot == pos1r) | (slot == pos2r)).astype(BF16)
    xs = jnp.dot(onehot, x1.astype(BF16), preferred_element_type=F32)
    pos_cols = jnp.where(sub == 2.0, pos1r, jnp.where(sub == 3.0, pos2r, 0.0)).T
    lane = lax.broadcasted_iota(I32, (t, LANES), 1)
    x1_ref[...] = x1
    rw_ref[...] = jnp.where(lane == 0, w1, jnp.where(lane == 1, w2, pos_cols))
    xs_ref[0:USED_ROWS, :] = xs.astype(BF16)
    xs_ref[USED_ROWS:CAP, :] = jnp.zeros((CAP - USED_ROWS, x1.shape[1]), BF16)
    mlane = lax.broadcasted_iota(I32, (LANES, LANES), 1)
    meta = jnp.where(mlane == 0, cnt, jnp.where(mlane == 1, off, 0.0))
    meta_ref[...] = meta.astype(I32)


def _sort_part(i, x1, w1, w2, e1, e2, x1_ref, rw_ref, xs_ref, meta_ref):
    _sort_tile(x1, w1, w2, e1, e2,
               x1_ref.at[pl.ds(i * TL, TL)], rw_ref.at[pl.ds(i * TL, TL)],
               xs_ref.at[pl.ds(i * CAP, CAP)], meta_ref.at[pl.ds(i * LANES, LANES)])


def _sort_tiles(x1, w1, w2, e1, e2, *outs):
    for i in range(x1.shape[0] // TL):
        rows = slice(i * TL, (i + 1) * TL)
        _sort_part(i, x1[rows], w1[rows], w2[rows], e1[rows], e2[rows], *outs)


GATE_COLS = {3: 0, 6: RET_V, 7: RET_V + D_MODEL}
QKV_COLS = {0: 0, 1: RET_QK, 2: 2 * RET_QK}


def _prompt_mixer_kernel(x_ref, x1s_ref, rws_ref, cos_ref, sin_ref, dec_ref, qdec_ref, kdec_ref, cdec_ref,
                         w_in, b_in, gn_g, gn_b, w_ret_o, conv_w, conv_b, cln_g, cln_b,
                         w_conv_o, w_out, ln1_g, ln1_b, wr_hi, wr_lo, b_r,
                         x1_ref, rw_ref, xs_ref, meta_ref, sret_ref, sconv_ref,
                         ubuf, ushift, qkv_scr, xb_scr, ret_scr, cout_scr, gate_scr, x1_carry, lg_carry,
                         *, n_tiles, tiles_per_seq):
    s = pl.program_id(0)
    li = lax.rem(s, tiles_per_seq)
    outs = (x1_ref, rw_ref, xs_ref, meta_ref)
    slot = dict(ret=ret_scr, cout=cout_scr, gates=gate_scr)
    tail_w = (w_ret_o, cln_g, cln_b, w_conv_o, w_out, ln1_g, ln1_b, wr_hi, wr_lo, b_r)
    carry = (x1_carry, lg_carry)
    tl = x_ref.shape[1]
    parts = tl // TL
    sort_sink = lambda *r: _sort_part(*r, *outs)

    @pl.when(s == 0)
    def _first():
        x1_carry[...] = jnp.zeros(x1_carry.shape, F32)
        lg_carry[...] = jnp.zeros(lg_carry.shape, F32)

    @pl.when((s < n_tiles) & (li == 0))
    def _new_sequence():
        sret_ref[...] = jnp.zeros(sret_ref.shape, F32)
        ubuf[0:CONV_PAD, :] = jnp.zeros((CONV_PAD, CONV_CH), F32)

    @pl.when(s < n_tiles)
    def _mix():
        gcols = lambda kk: slice(GATE_COLS[kk], GATE_COLS[kk] + IN_WIDTHS[kk])
        src = dict(x=lambda r: x_ref[0, r, :], ret=lambda: ret_scr[...], cout=lambda: cout_scr[...],
                   g=lambda: gate_scr[:, gcols(3)], gt_a=lambda r: gate_scr[r, gcols(6)],
                   gt_b=lambda r: gate_scr[r, gcols(7)])
        parked = _parked_pieces(carry, sort_sink, tl, parts)
        head = _prompt_head_pieces(x_ref, cos_ref, sin_ref, dec_ref, qdec_ref, kdec_ref, cdec_ref,
                                   w_in, b_in, gn_g, gn_b, conv_w, conv_b, sret_ref,
                                   ubuf, ushift, qkv_scr, xb_scr, slot, extra=parked)
        tail = _post_mix_pieces(src, tail_w, None, tl, parts=parts, carry=carry)
        for piece in head + tail:
            piece()

    @pl.when(s == n_tiles)
    def _drain():
        for piece in _parked_pieces(carry, sort_sink, tl, parts):
            piece()

    @pl.when(s > n_tiles)
    def _append():
        rws = rws_ref[...]
        _sort_tiles(x1s_ref[...], rws[:, 0:1], rws[:, 1:2], rws[:, 2:3], rws[:, 3:4], *outs)

    @pl.when((s < n_tiles) & (li == tiles_per_seq - 1))
    def _conv_state():
        sconv_ref[0, 0] = ubuf[CONV_PAD - (CONV_WIDTH - 1):CONV_PAD, :]


def _prompt_head_pieces(x_ref, cos_ref, sin_ref, dec_ref, qdec_ref, kdec_ref, cdec_ref,
                        w_in, b_in, gn_g, gn_b, conv_w, conv_b, sret_ref,
                        ubuf, ushift, qkv_scr, xb_scr, slot, extra=()):
    tl = x_ref.shape[1]
    st = {}

    def slab_dot(c0, c1):
        return jnp.dot(xb_scr[...], w_in[:, c0:c1], preferred_element_type=F32) + _rep(b_in[:, c0:c1], tl)

    def glu():
        xb_scr[...] = x_ref[0].astype(BF16)
        u = slab_dot(IN_OFFS[4], IN_OFFS[5]) * _sigmoid(slab_dot(IN_OFFS[5], IN_OFFS[6]))
        ubuf[CONV_PAD:CONV_PAD + tl, :] = u

    nsh = ushift.shape[1]
    span = nsh - (CONV_PAD - SUBLANES)

    def shift_copy(h, s):
        ushift[s - 1] = ubuf[h * span + s:h * span + s + nsh, :]

    slab = 256
    slabs = [(kk, c0) for kk in (0, 1, 2, 3, 6, 7) for c0 in range(IN_OFFS[kk], IN_OFFS[kk + 1], slab)]
    rb = 32
    nrb = tl // rb

    def proj_slab(kk, c0):
        val = slab_dot(c0, c0 + slab)
        if kk in QKV_COLS:
            dst = QKV_COLS[kk] + c0 - IN_OFFS[kk]
            qkv_scr[:, dst:dst + slab] = val
        else:
            dst = GATE_COLS[kk] + c0 - IN_OFFS[kk]
            slot["gates"][:, dst:dst + slab] = val

    def conv_block(r):
        h, rl = divmod(r * rb, span)
        acc = jnp.zeros((rb, CONV_CH), F32) + conv_b[...]
        for j in range(CONV_WIDTH):
            off = j + (CONV_PAD - (CONV_WIDTH - 1))
            s = off % SUBLANES
            base = rl + off - s
            win = (ubuf[h * span + base:h * span + base + rb, :] if s == 0
                   else ushift[s - 1, base:base + rb, :])
            acc = acc + _rep(conv_w[j], rb) * win
        slot["cout"][r * rb:(r + 1) * rb, :] = acc
        if r == nrb - 1:
            ubuf[0:CONV_PAD, :] = ubuf[tl:tl + CONV_PAD, :]

    scale = RET_DK ** -0.5

    def retention(c, h):
        rows = slice(c * RET_CHUNK, (c + 1) * RET_CHUNK)
        cols = slice(h * RET_DK, (h + 1) * RET_DK)
        hcol = lambda kk: slice(QKV_COLS[kk] + h * RET_DK, QKV_COLS[kk] + (h + 1) * RET_DK)
        cosf = cos_ref[rows, :]
        sinf = sin_ref[rows, :]
        qh = _rot(qkv_scr[rows, hcol(0)], cosf, sinf)
        kh = _rot(qkv_scr[rows, hcol(1)], cosf, sinf) * scale
        qb = qh.astype(BF16)
        kb = kh.astype(BF16)
        vb = qkv_scr[rows, hcol(2)].astype(BF16)
        s_old = sret_ref[0, 0, h]
        scores = lax.dot_general(qb, kb, (((1,), (1,)), ((), ())),
                                 preferred_element_type=F32) * dec_ref[h]
        inner = jnp.dot(scores.astype(BF16), vb, preferred_element_type=F32)
        cross = jnp.dot(qb, s_old.astype(BF16), preferred_element_type=F32) * qdec_ref[h]
        kd = (kh * kdec_ref[h]).astype(BF16)
        s_new = cdec_ref[h] * s_old + lax.dot_general(
            kd, vb, (((0,), (0,)), ((), ())), preferred_element_type=F32)
        sret_ref[0, 0, h] = s_new
        slot["ret"][rows, cols] = _ln(inner + cross, gn_g[:, cols], gn_b[:, cols])

    vector_pieces = []
    for r in range(nrb):
        if (r * rb) % span == 0:
            vector_pieces += [lambda h=(r * rb) // span, s=s: shift_copy(h, s) for s in range(1, SUBLANES)]
        vector_pieces.append(lambda r=r: conv_block(r))
    def spread(work, between):
        out = []
        for i, piece in enumerate(work):
            out.append(piece)
            out += between[i * len(between) // len(work):(i + 1) * len(between) // len(work)]
        return out

    as_piece = lambda sl: (lambda: proj_slab(*sl))
    early = [as_piece(sl) for sl in slabs if sl[0] in (0, 1, 2, 3)]
    late = [as_piece(sl) for sl in slabs if sl[0] not in (0, 1, 2, 3)]
    for i, piece in enumerate(extra):
        early.insert(min(2 + 2 * i, len(early)), piece)
    ret_pieces = [lambda c=c, h=h: retention(c, h) for c in range(tl // RET_CHUNK) for h in range(RET_HEADS)]
    return [glu] + spread(vector_pieces, early) + spread(ret_pieces, late)


def _sample_mixer_kernel(x_ref, cos_ref, sin_ref, pdec_ref, qdec_ref, kdec_ref, cdec_ref, conv_w,
                         sret_in, sconv_in,
                         w_in, b_in, gn_g, gn_b, w_ret_o, conv_b, cln_g, cln_b,
                         w_conv_o, w_out, ln1_g, ln1_b, wr_hi, wr_lo, b_r,
                         x1_ref, rw_ref, sret_ref, sconv_ref,
                         ret_scr, cout_scr, gate_scr, xwin, u_scr):
    i = pl.program_id(0)
    t = cos_ref.shape[0]
    ls = t // BB_SAMPLE
    nstate = CONV_WIDTH - 1
    r0 = pl.multiple_of(i * t, t)
    xb = x_ref[pl.ds(r0, t), :].astype(BF16)

    def proj(k):
        c0, c1 = IN_OFFS[k], IN_OFFS[k + 1]
        return jnp.dot(xb, w_in[:, c0:c1], preferred_element_type=F32) + _rep(b_in[:, c0:c1], t)

    q = proj(0)
    k = proj(1)
    v = proj(2)
    scale = RET_DK ** -0.5
    cosf = cos_ref[...]
    sinf = sin_ref[...]
    row = lax.broadcasted_iota(I32, (t, RET_DK), 0)
    pos = row % ls
    row8 = lax.broadcasted_iota(I32, (SUBLANES, RET_DK), 0)
    per_tile = SUBLANES // ls
    for h in range(RET_HEADS):
        cols = slice(h * RET_DK, (h + 1) * RET_DK)
        qh = _rot(q[:, cols], cosf, sinf)
        kh = _rot(k[:, cols], cosf, sinf) * scale
        vh = v[:, cols]
        inner = jnp.zeros((t, RET_DV), F32)
        for s in range(ls):
            ks = kh if s == 0 else pltpu.roll(kh, s, axis=0)
            vs = vh if s == 0 else pltpu.roll(vh, s, axis=0)
            dotp = jnp.sum(qh * ks, axis=1, keepdims=True) * pdec_ref[h, s]
            inner = inner + jnp.where(pos >= s, dotp, 0.0) * vs
        kd = kh * kdec_ref[h]
        for tile in range(t // SUBLANES):
            rows = slice(tile * SUBLANES, (tile + 1) * SUBLANES)
            q8 = qh[rows, :]
            kd8 = kd[rows, :]
            v8 = vh[rows, :]
            seqs = [tile * per_tile + sub for sub in range(per_tile)]
            mine = [(row8 >= sub * ls) & (row8 < (sub + 1) * ls) for sub in range(per_tile)]
            s_old = [sret_in[0, b, h] for b in seqs]
            c_all = jnp.dot(q8, jnp.concatenate(s_old, axis=1), preferred_element_type=F32)
            upd = lax.dot_general(jnp.concatenate([jnp.where(m, kd8, 0.0) for m in mine], axis=1), v8,
                                  (((0,), (0,)), ((), ())), preferred_element_type=F32)
            cross8 = jnp.zeros((SUBLANES, RET_DV), F32)
            for sub, b in enumerate(seqs):
                cross8 = jnp.where(mine[sub], c_all[:, sub * RET_DV:(sub + 1) * RET_DV], cross8)
                sret_ref[0, b, h] = cdec_ref[h] * s_old[sub] + upd[sub * RET_DK:(sub + 1) * RET_DK, :]
            ret_scr[pl.ds(r0 + tile * SUBLANES, SUBLANES), cols] = (
                inner[rows, :] + cross8 * qdec_ref[h, rows, :])
        ret_scr[pl.ds(r0, t), cols] = _ln(ret_scr[pl.ds(r0, t), cols], gn_g[:, cols], gn_b[:, cols])

    u = proj(4) * _sigmoid(proj(5))
    nslab = CONV_CH // LANES
    xwin[0:nstate] = sconv_in[...]
    for sl in range(nslab):
        u_scr[sl] = u[:, sl * LANES:(sl + 1) * LANES]
    for p in range(ls):
        for sl in range(nslab):
            xwin[XPAD_NEW + p, :, sl * LANES:(sl + 1) * LANES] = u_scr[sl, pl.ds(p, BB_SAMPLE, stride=ls), :]
    win_row = lambda m: m if m < nstate else XPAD_NEW + (m - nstate)
    for p in range(ls):
        res = jnp.zeros((BB_SAMPLE, CONV_CH), F32) + conv_b[...]
        for j in range(CONV_WIDTH):
            res = res + _rep(conv_w[j], BB_SAMPLE) * xwin[win_row(p + j)]
        for sl in range(nslab):
            cout_scr[sl, pl.ds(r0 + p, BB_SAMPLE, stride=ls), :] = res[:, sl * LANES:(sl + 1) * LANES]
    sconv_ref[0:nstate - ls] = xwin[ls:nstate]
    sconv_ref[nstate - ls:nstate] = xwin[XPAD_NEW:XPAD_NEW + ls]
    for kk in (3, 6, 7):
        gate_scr[pl.ds(r0, t), GATE_COLS[kk]:GATE_COLS[kk] + IN_WIDTHS[kk]] = proj(kk)

    @pl.when(i == pl.num_programs(0) - 1)
    def _second_half():
        n = x_ref.shape[0]

        def sink(p, x1, w1, w2, e1, e2):
            x1_ref[...] = x1
            rw_ref[...] = _lane_tile((w1, w2, e1, e2), n)

        gcols = lambda kk: slice(GATE_COLS[kk], GATE_COLS[kk] + IN_WIDTHS[kk])
        src = dict(x=lambda r: x_ref[r, :], ret=lambda: ret_scr[...],
                   cout=lambda: jnp.concatenate([cout_scr[sl] for sl in range(CONV_CH // LANES)], axis=1),
                   g=lambda: gate_scr[:, gcols(3)], gt_a=lambda r: gate_scr[r, gcols(6)],
                   gt_b=lambda r: gate_scr[r, gcols(7)])
        for piece in _post_mix_pieces(
                src, (w_ret_o, cln_g, cln_b, w_conv_o, w_out, ln1_g, ln1_b, wr_hi, wr_lo, b_r), sink, n):
            piece()


def _ffn_kernel(te_ref, nvalid_ref, nexte_ref, chunk_ref, xs_hbm, wgu_hbm, wdn_hbm, ys_hbm,
                xbuf, obuf, wgu_f, wdn_f, wgu_b, wdn_b, sem_in, sem_out, sem_w):
    del xs_hbm
    nvalid = nvalid_ref[0]

    def chunk_rows(tile, c):
        return pl.ds(pl.multiple_of(chunk_ref[tile * TILE_CHUNKS + c] * CHUNK, CHUNK), CHUNK)

    def start_in(tile, s):
        for c in range(TILE_CHUNKS):
            pltpu.make_async_copy(ys_hbm.at[chunk_rows(tile, c)],
                                  xbuf.at[s, pl.ds(c * CHUNK, CHUNK)], sem_in.at[s]).start()

    def start_out(tile, s):
        for c in range(TILE_CHUNKS):
            pltpu.make_async_copy(obuf.at[s, pl.ds(c * CHUNK, CHUNK)],
                                  ys_hbm.at[chunk_rows(tile, c)], sem_out.at[s]).start()

    def wait_in(s):
        pltpu.make_async_copy(ys_hbm.at[pl.ds(0, TM_FFN)], xbuf.at[s], sem_in.at[s]).wait()

    def wait_out(s):
        pltpu.make_async_copy(obuf.at[s], ys_hbm.at[pl.ds(0, TM_FFN)], sem_out.at[s]).wait()

    def weight_copies(e, s):
        return (pltpu.make_async_copy(wgu_hbm.at[e], wgu_f.at[s], sem_w.at[s]),
                pltpu.make_async_copy(wdn_hbm.at[e], wdn_f.at[s], sem_w.at[s]))

    def tile(i, wslot):
        slot = i % 2

        @pl.when(i >= 2)
        def _retire():
            wait_out(slot)

        wait_in(slot)
        changed = (i == 0) | (te_ref[i] != te_ref[jnp.maximum(i - 1, 0)])

        @pl.when(changed)
        def _new_expert():
            for cp in weight_copies(te_ref[i], wslot):
                cp.wait()
            wgu_b[...] = wgu_f[wslot].astype(BF16)
            wdn_b[...] = wdn_f[wslot].astype(BF16)

            @pl.when(nexte_ref[i] >= 0)
            def _():
                for cp in weight_copies(nexte_ref[i], 1 - wslot):
                    cp.start()

        x = xbuf[slot]
        y = jnp.zeros((TM_FFN, wdn_b.shape[1]), F32)
        for c0 in range(0, EXP_FF, FFN_COLS):
            hg = jnp.dot(x, wgu_b[:, c0:c0 + FFN_COLS], preferred_element_type=F32)
            hu = jnp.dot(x, wgu_b[:, EXP_FF + c0:EXP_FF + c0 + FFN_COLS], preferred_element_type=F32)
            y = y + _bdot(_silu(hg) * hu, wdn_b[c0:c0 + FFN_COLS, :])
            if c0 == 0:
                start_in(jnp.where(i + 1 < nvalid, i + 1, 0), 1 - slot)
        obuf[slot] = y.astype(BF16)
        start_out(i, slot)
        return jnp.where(changed, 1 - wslot, wslot)

    for cp in weight_copies(te_ref[0], 0):
        cp.start()
    start_in(0, 0)
    lax.fori_loop(0, nvalid, tile, jnp.int32(0))
    wait_out(nvalid % 2)
    wait_out(1 - nvalid % 2)
    wait_in(nvalid % 2)


def _final_kernel(ys_ref, x1_ref, rw_ref, pp_ref, ps_ref, ln2_g, ln2_b, w_pg, b_pg, w_ple,
                  yp_ref, ys_out_ref, *, n_prompt_tiles):
    i = pl.program_id(0)
    x1 = x1_ref[...]
    slot = lax.broadcasted_iota(I32, (TL, USED_ROWS), 1).astype(F32)
    parts = []
    for b in range(x1.shape[0] // TL):
        rw = rw_ref[b * TL:(b + 1) * TL, :]
        w1, w2, pos1, pos2 = rw[:, 0:1], rw[:, 1:2], rw[:, 2:3], rw[:, 3:4]
        ys = ys_ref[b * CAP:b * CAP + USED_ROWS, :]
        comb = jnp.where(slot == pos1, w1, jnp.where(slot == pos2, w2, 0.0)).astype(BF16)
        parts.append(jnp.dot(comb, ys, preferred_element_type=F32))
    moe = parts[0] if len(parts) == 1 else jnp.concatenate(parts, axis=0)
    x2 = _ln(ALPHA * x1 + moe, ln2_g[...], ln2_b[...])
    gate = _sigmoid(_bdot(x2, w_pg[...]) + b_pg[...])
    p = jnp.where(i < n_prompt_tiles, pp_ref[...], ps_ref[...])
    y = x2 + gate * _bdot(p, w_ple[...])

    @pl.when(i < n_prompt_tiles)
    def _prompt():
        yp_ref[...] = y

    @pl.when(i >= n_prompt_tiles)
    def _sample():
        ys_out_ref[...] = y


def _rope_tables(pos):
    half = RET_DK // 2
    inv_freq = ROPE_BASE ** (-np.arange(half, dtype=np.float64) / half)
    ang = np.asarray(pos, np.float64)[:, None] * inv_freq[None, :]
    cos = np.cos(ang)
    sin = np.sin(ang)
    return (np.concatenate([cos, cos], axis=-1).astype(np.float32),
            np.concatenate([-sin, sin], axis=-1).astype(np.float32))


def _log_gamma():
    return np.log(1.0 - 2.0 ** (-5.0 - np.arange(RET_HEADS, dtype=np.float64)))


def _const_spec(shape):
    nd = len(shape)
    return pl.BlockSpec(shape, lambda *_: (0,) * nd, pipeline_mode=pl.Buffered(1))


def _chunk_plan(meta, n_blocks, n_ffn_tiles):
    assert n_blocks * BLOCK_SPARE >= N_EXPERTS * (TILE_CHUNKS - 1)
    m = meta.reshape(n_blocks, LANES, LANES)
    cnt = m[:, :N_EXPERTS, 0]
    off = m[:, :N_EXPERTS, 1]
    nch = (cnt + (CHUNK - 1)) // CHUNK
    cum = jnp.cumsum(nch, axis=0)
    total = cum[-1:]
    tiles_e = (total + TILE_CHUNKS - 1) // TILE_CHUNKS
    tile_end = jnp.cumsum(tiles_e, axis=1)
    tile_start = tile_end - tiles_e
    tid = jnp.arange(n_ffn_tiles, dtype=I32)[:, None]
    owner = (tid >= tile_start) & (tid < tile_end)
    pick_e = lambda v: jnp.sum(jnp.where(owner, v, 0), axis=1, keepdims=True)
    te = pick_e(jnp.arange(N_EXPERTS, dtype=I32)[None, :])
    k = (tid - pick_e(tile_start)) * TILE_CHUNKS + jnp.arange(TILE_CHUNKS, dtype=I32)[None, :]
    total_t = pick_e(total)
    real = k < total_t
    by_tile = lambda v: jnp.sum(jnp.where(owner[:, None, :], v[None, :, :], 0), axis=2)
    cum_t = by_tile(cum)
    blk = jnp.minimum(jnp.sum((cum_t[:, None, :] <= k[:, :, None]).astype(I32), axis=2), n_blocks - 1)
    at_blk = blk[:, :, None] == jnp.arange(n_blocks, dtype=I32)[None, None, :]
    pick_b = lambda v: jnp.sum(jnp.where(at_blk, v[:, None, :], 0), axis=2)
    excl = pick_b(cum_t - by_tile(nch))
    off_t = pick_b(by_tile(off))
    spare = te * (TILE_CHUNKS - 1) + jnp.maximum(k - total_t, 0) % TILE_CHUNKS
    spare_chunk = (spare // BLOCK_SPARE) * BLOCK_CHUNKS + BLOCK_USED + spare % BLOCK_SPARE
    chunk = jnp.where(real, blk * BLOCK_CHUNKS + off_t + (k - excl), spare_chunk)
    n_valid = jnp.sum(tiles_e, axis=1)
    eid = jnp.arange(N_EXPERTS, dtype=I32)
    later = (eid[None, :] > eid[:, None]) & (tiles_e > 0)
    next_e = jnp.min(jnp.where(later, eid[None, :], N_EXPERTS), axis=1)[None, :]
    next_t = pick_e(jnp.where(next_e < N_EXPERTS, next_e, -1))
    return (te.reshape(-1).astype(I32), n_valid.astype(I32), next_t.reshape(-1).astype(I32),
            chunk.reshape(-1).astype(I32))


def kernel(x_prompt, x_sample, state_ret, state_conv, p_prompt, p_sample, w_in, b_in, ret_gn_g, ret_gn_b,
           w_ret_o, conv_w, conv_b, conv_ln_g, conv_ln_b, w_conv_o, w_out, ln1_g, ln1_b, w_grp, b_grp,
           w_exp, b_exp, w_gu, w_dn, ln2_g, ln2_b, w_pg, b_pg, w_ple):
    assert DEPTH == 1 and w_in.shape[0] == 1
    bp, lp, d = x_prompt.shape
    bs, ls, _ = x_sample.shape
    n_p, n_s = bp * lp, bs * ls
    n_tok = n_p + n_s
    assert lp % TL == 0 and n_s % TL == 0 and bs % BB_SAMPLE == 0 and SUBLANES % ls == 0
    n_blocks = n_tok // TL

    f32c = lambda a, shape: jnp.asarray(np.broadcast_to(a, shape).astype(np.float32))
    lg = _log_gamma()
    c = RET_CHUNK
    idx = np.arange(c, dtype=np.float64)
    rel = idx[:, None] - idx[None, :]
    causal = rel >= 0
    decay = np.where(causal[None], np.exp(np.where(causal, rel, 0.0)[None] * lg[:, None, None]), 0.0)
    decay = f32c(decay, decay.shape)
    q_decay = np.exp((idx[:, None] + 1.0) * lg[None, :])
    k_decay = np.exp((c - 1.0 - idx[:, None]) * lg[None, :])
    chunk_decay = np.exp(c * lg)
    qdec_p = f32c(q_decay.T[:, :, None], (RET_HEADS, c, RET_DK))
    kdec_p = f32c(k_decay.T[:, :, None], (RET_HEADS, c, RET_DK))
    cdec_p = f32c(chunk_decay[:, None, None], (RET_HEADS, 1, RET_DV))
    cos_p, sin_p = (jnp.asarray(a) for a in _rope_tables(np.arange(lp)))

    ts = BB_SAMPLE * ls
    idx_s = np.arange(ls, dtype=np.float64)
    pdec_s = np.exp(idx_s[None, :] * lg[:, None])
    pdec_s = f32c(pdec_s[:, :, None, None], (RET_HEADS, ls, 1, RET_DK))
    qd_s = np.exp((idx_s[:, None] + 1.0) * lg[None, :])
    kd_s = np.exp((ls - 1.0 - idx_s[:, None]) * lg[None, :])
    qdec_s = f32c(np.tile(qd_s.T, (1, BB_SAMPLE))[:, :, None], (RET_HEADS, ts, RET_DK))
    kdec_s = f32c(np.tile(kd_s.T, (1, BB_SAMPLE))[:, :, None], (RET_HEADS, ts, RET_DK))
    cdec_s = f32c(np.exp(ls * lg)[:, None, None], (RET_HEADS, 1, RET_DV))
    cos_s, sin_s = (jnp.asarray(a) for a in _rope_tables(np.tile(PAST_LEN + np.arange(ls), BB_SAMPLE)))

    w_in_b = w_in[0].astype(BF16)
    w_ret_o_b = w_ret_o[0].astype(BF16)
    w_conv_o_b = w_conv_o[0].astype(BF16)
    w_out_b = w_out[0].astype(BF16)
    w_pg_b = w_pg[0].astype(BF16)
    w_ple_b = w_ple[0].astype(BF16)
    n_route = N_GROUPS + N_EXPERTS
    w_r = jnp.concatenate([w_grp[0], w_exp[0], jnp.zeros((d, LANES - n_route), F32)], axis=1)
    wr_hi = w_r.astype(BF16)
    wr_lo = jnp.concatenate([wr_hi, (w_r - wr_hi.astype(F32)).astype(BF16)], axis=1)
    b_r = jnp.concatenate([b_grp[0], b_exp[0], jnp.zeros((LANES - n_route,), F32)]).reshape(1, LANES)
    row = lambda a: a.reshape(1, -1)
    conv_w0 = conv_w[0]
    nstate = CONV_WIDTH - 1

    rep8 = lambda a: jnp.broadcast_to(a[..., None, :], a.shape[:-1] + (SUBLANES, a.shape[-1]))
    shared_w = (w_in_b, rep8(b_in[0]), row(ret_gn_g[0]), row(ret_gn_b[0]), w_ret_o_b)
    tail_w = (row(conv_ln_g[0]), row(conv_ln_b[0]), w_conv_o_b, w_out_b, row(ln1_g[0]), row(ln1_b[0]),
              wr_hi, wr_lo, b_r)

    nbt = bs // BB_SAMPLE
    xs2 = x_sample.reshape(n_s, d)
    conv_state_t = jnp.transpose(state_conv[0], (1, 0, 2))
    sample_in = ((xs2, cos_s, sin_s, pdec_s, qdec_s, kdec_s, cdec_s, rep8(conv_w0), state_ret, conv_state_t)
                 + shared_w + (row(conv_b[0]),) + tail_w)
    conv_state_spec = pl.BlockSpec((nstate, BB_SAMPLE, CONV_CH), lambda i: (0, i, 0))
    sample_specs = (
        [_const_spec(a.shape) for a in sample_in[0:8]]
        + [pl.BlockSpec((1, BB_SAMPLE, RET_HEADS, RET_DK, RET_DV), lambda i: (0, i, 0, 0, 0)), conv_state_spec]
        + [_const_spec(a.shape) for a in sample_in[10:]]
    )
    tok_spec_s = lambda w: pl.BlockSpec((n_s, w), lambda i: (0, 0))
    x1_s, rw_s, ret_s, conv_s_t = pl.pallas_call(
        _sample_mixer_kernel,
        grid=(nbt,),
        in_specs=sample_specs,
        out_specs=[
            tok_spec_s(d), tok_spec_s(LANES),
            pl.BlockSpec((1, BB_SAMPLE, RET_HEADS, RET_DK, RET_DV), lambda i: (0, i, 0, 0, 0)),
            conv_state_spec,
        ],
        out_shape=[
            jax.ShapeDtypeStruct((n_s, d), F32),
            jax.ShapeDtypeStruct((n_s, LANES), F32),
            jax.ShapeDtypeStruct(state_ret.shape, F32),
            jax.ShapeDtypeStruct(conv_state_t.shape, F32),
        ],
        scratch_shapes=[
            pltpu.VMEM((n_s, RET_V), F32),
            pltpu.VMEM((CONV_CH // LANES, n_s, LANES), F32),
            pltpu.VMEM((n_s, RET_V + 2 * D_MODEL), F32),
            pltpu.VMEM((XPAD_ROWS, BB_SAMPLE, CONV_CH), F32),
            pltpu.VMEM((CONV_CH // LANES, ts, LANES), F32),
        ],
        compiler_params=pltpu.CompilerParams(
            dimension_semantics=("arbitrary",), vmem_limit_bytes=VMEM_LIMIT),
        name="sample_mixer",
    )(*sample_in)
    conv_s = jnp.transpose(conv_s_t, (1, 0, 2))[None]

    assert lp % TLM == 0 and n_s % TLM == 0 and TLM % TL == 0
    nlt = lp // TLM
    npt = n_p // TLM
    nst = n_s // TLM
    sub = TLM // TL
    prompt_in = ((x_prompt, x1_s, rw_s, cos_p, sin_p, decay, qdec_p, kdec_p, cdec_p)
                 + shared_w + (rep8(conv_w0), row(conv_b[0])) + tail_w)
    head_tile = lambda s: jnp.minimum(s, npt - 1)
    sample_tile = lambda s: jnp.maximum(s - (npt + 1), 0)
    sample_spec = lambda w: pl.BlockSpec((TLM, w), lambda s: (sample_tile(s), 0))
    prompt_specs = [
        pl.BlockSpec((1, TLM, d), lambda s: (head_tile(s) // nlt, head_tile(s) % nlt, 0)),
        sample_spec(d), sample_spec(LANES),
        pl.BlockSpec((TLM, RET_DK), lambda s: (head_tile(s) % nlt, 0)),
        pl.BlockSpec((TLM, RET_DK), lambda s: (head_tile(s) % nlt, 0)),
    ] + [_const_spec(a.shape) for a in prompt_in[5:]]
    tok_spec_p = lambda rows, w: pl.BlockSpec((rows, w), lambda s: (jnp.maximum(s - 1, 0), 0))
    x1_all, rw_all, xs_all, meta, ret_p, conv_p = pl.pallas_call(
        functools.partial(_prompt_mixer_kernel, n_tiles=npt, tiles_per_seq=nlt),
        grid=(npt + 1 + nst,),
        in_specs=prompt_specs,
        out_specs=[
            tok_spec_p(TLM, d), tok_spec_p(TLM, LANES), tok_spec_p(sub * CAP, d), tok_spec_p(sub * LANES, LANES),
            pl.BlockSpec((1, 1, RET_HEADS, RET_DK, RET_DV), lambda s: (0, head_tile(s) // nlt, 0, 0, 0)),
            pl.BlockSpec((1, 1, nstate, CONV_CH), lambda s: (0, head_tile(s) // nlt, 0, 0)),
        ],
        out_shape=[
            jax.ShapeDtypeStruct((n_tok, d), F32),
            jax.ShapeDtypeStruct((n_tok, LANES), F32),
            jax.ShapeDtypeStruct((n_blocks * CAP, d), BF16),
            jax.ShapeDtypeStruct((n_blocks * LANES, LANES), I32),
            jax.ShapeDtypeStruct((1, bp, RET_HEADS, RET_DK, RET_DV), F32),
            jax.ShapeDtypeStruct((1, bp, nstate, CONV_CH), F32),
        ],
        scratch_shapes=[
            pltpu.VMEM((TLM + CONV_PAD, CONV_CH), F32),
            pltpu.VMEM((SUBLANES - 1, TL + CONV_PAD - SUBLANES, CONV_CH), F32),
            pltpu.VMEM((TLM, 2 * RET_QK + RET_V), F32),
            pltpu.VMEM((TLM, d), BF16),
            pltpu.VMEM((TLM, RET_V), F32),
            pltpu.VMEM((TLM, CONV_CH), F32),
            pltpu.VMEM((TLM, RET_V + 2 * D_MODEL), F32),
            pltpu.VMEM((TLM, d), F32),
            pltpu.VMEM((TLM, LANES), F32),
        ],
        compiler_params=pltpu.CompilerParams(
            dimension_semantics=("arbitrary",), vmem_limit_bytes=VMEM_LIMIT),
        name="prompt_mixer",
    )(*prompt_in)

    assert TOP_K * n_tok // TM_FFN >= 3
    max_chunks = n_blocks * (TOP_K * TL // CHUNK + N_EXPERTS - 1)
    n_ffn_tiles = (max_chunks + N_EXPERTS * (TILE_CHUNKS - 1)) // TILE_CHUNKS
    tile_e, n_valid_tiles, next_e, chunk_ids = _chunk_plan(meta, n_blocks, n_ffn_tiles)

    ys_all = pl.pallas_call(
        _ffn_kernel,
        grid_spec=pltpu.PrefetchScalarGridSpec(
            num_scalar_prefetch=4,
            grid=(1,),
            in_specs=[pl.BlockSpec(memory_space=pl.ANY)] * 3,
            out_specs=pl.BlockSpec(memory_space=pl.ANY),
            scratch_shapes=[
                pltpu.VMEM((2, TM_FFN, d), BF16),
                pltpu.VMEM((2, TM_FFN, d), BF16),
                pltpu.VMEM((2, d, 2 * EXP_FF), F32),
                pltpu.VMEM((2, EXP_FF, d), F32),
                pltpu.VMEM((d, 2 * EXP_FF), BF16),
                pltpu.VMEM((EXP_FF, d), BF16),
                pltpu.SemaphoreType.DMA((2,)),
                pltpu.SemaphoreType.DMA((2,)),
                pltpu.SemaphoreType.DMA((2,)),
            ],
        ),
        out_shape=jax.ShapeDtypeStruct(xs_all.shape, BF16),
        input_output_aliases={4: 0},
        compiler_params=pltpu.CompilerParams(
            dimension_semantics=("arbitrary",), vmem_limit_bytes=VMEM_LIMIT),
        name="expert_ffn",
    )(tile_e, n_valid_tiles, next_e, chunk_ids, xs_all, w_gu[0], w_dn[0])

    assert n_p % TLF == 0 and n_s % TLF == 0
    npt = n_p // TLF
    fsub = TLF // TL
    pp2 = p_prompt.reshape(n_p, PLE_DIM)
    ps2 = p_sample.reshape(n_s, PLE_DIM)
    tok_f = lambda rows, w: pl.BlockSpec((rows, w), lambda i: (i, 0))
    y_p, y_s = pl.pallas_call(
        functools.partial(_final_kernel, n_prompt_tiles=npt),
        grid=(n_tok // TLF,),
        in_specs=[
            tok_f(fsub * CAP, d), tok_f(TLF, d), tok_f(TLF, LANES),
            pl.BlockSpec((TLF, PLE_DIM), lambda i: (jnp.minimum(i, npt - 1), 0)),
            pl.BlockSpec((TLF, PLE_DIM), lambda i: (jnp.maximum(i - npt, 0), 0)),
            _const_spec((1, d)), _const_spec((1, d)), _const_spec((d, d)), _const_spec((1, d)),
            _const_spec((PLE_DIM, d)),
        ],
        out_specs=[
            pl.BlockSpec((TLF, d), lambda i: (jnp.minimum(i, npt - 1), 0)),
            pl.BlockSpec((TLF, d), lambda i: (jnp.maximum(i - npt, 0), 0)),
        ],
        out_shape=[jax.ShapeDtypeStruct((n_p, d), F32), jax.ShapeDtypeStruct((n_s, d), F32)],
        compiler_params=pltpu.CompilerParams(
            dimension_semantics=("arbitrary",), vmem_limit_bytes=VMEM_LIMIT),
        name="moe_combine_final",
    )(ys_all, x1_all, rw_all, pp2, ps2,
      row(ln2_g[0]), row(ln2_b[0]), w_pg_b, row(b_pg[0]), w_ple_b)

    return (y_p.reshape(bp, lp, d), y_s.reshape(bs, ls, d), ret_p, conv_p, ret_s, conv_s)
```

```python
import functools

import jax
import jax.numpy as jnp
import numpy as np
from jax import lax
from jax.experimental import pallas as pl
from jax.experimental.pallas import tpu as pltpu

F32 = jnp.float32
BF16 = jnp.bfloat16
I32 = jnp.int32

D_MODEL = 1024
PAST_LEN = 16384
RET_HEADS = 4
RET_DK = 128
RET_DV = 128
RET_QK = RET_HEADS * RET_DK
RET_V = RET_HEADS * RET_DV
RET_CHUNK = 128
ROPE_BASE = 10000.0
CONV_CH = 512
CONV_WIDTH = 31
N_GROUPS = 4
EXP_PER_GROUP = 4
N_EXPERTS = N_GROUPS * EXP_PER_GROUP
TOP_K = 2
EXP_FF = 512
PLE_DIM = 256
DEPTH = 1
ALPHA = (2 * DEPTH) ** 0.25
LN_EPS = 1e-5
IN_WIDTHS = (RET_QK, RET_QK, RET_V, RET_V, CONV_CH, CONV_CH, D_MODEL, D_MODEL)
IN_OFFS = tuple(int(s) for s in np.cumsum((0,) + IN_WIDTHS))

LANES = 128
SUBLANES = 8
VMEM_LIMIT = 56 * 1024 * 1024

TL = 256
TLM = 512
TLF = 512
BB_SAMPLE = 16
CHUNK = 2 * SUBLANES
TILE_CHUNKS = 32
BLOCK_USED = -(-(TOP_K * TL + N_EXPERTS * (CHUNK - 1)) // LANES) * LANES // CHUNK
BLOCK_SPARE = LANES // CHUNK
BLOCK_CHUNKS = BLOCK_USED + BLOCK_SPARE
USED_ROWS = BLOCK_USED * CHUNK
CAP = BLOCK_CHUNKS * CHUNK
TM_FFN = TILE_CHUNKS * CHUNK
FFN_COLS = 256
CONV_PAD = 32
XPAD_NEW = 32
XPAD_ROWS = 40


def _ln(x, g, b):
    mu = jnp.mean(x, axis=-1, keepdims=True)
    d = x - mu
    var = jnp.mean(d * d, axis=-1, keepdims=True)
    return d * lax.rsqrt(var + LN_EPS) * g + b


def _sigmoid(x):
    return 1.0 / (1.0 + jnp.exp(-x))


def _rep(v8, rows):
    return v8 if rows == SUBLANES else jnp.concatenate([v8] * (rows // SUBLANES), axis=0)


def _silu(x):
    return x * _sigmoid(x)


def _bdot(a, b):
    return jnp.dot(a.astype(BF16), b, preferred_element_type=F32)


def _rot(t, cosf, sinf):
    return t * cosf + pltpu.roll(t, RET_DK // 2, axis=1) * sinf


def _lane_tile(cols, rows):
    lane = lax.broadcasted_iota(I32, (rows, LANES), 1)
    out = jnp.zeros((rows, LANES), F32)
    for i, col in enumerate(cols):
        out = jnp.where(lane == i, col, out)
    return out


def _route(logits):
    lane = lax.broadcasted_iota(I32, logits.shape, 1)
    lanef = lane.astype(F32)
    ninf = jnp.float32(-jnp.inf)
    big = jnp.float32(LANES)
    gmask = lane < N_GROUPS
    gl = jnp.where(gmask, logits, ninf)
    gmax = jnp.max(gl, axis=1, keepdims=True)
    gidx = jnp.min(jnp.where(gmask & (gl == gmax), lanef, big), axis=1, keepdims=True)
    sumexp = jnp.sum(jnp.where(gmask, jnp.exp(gl - gmax), 0.0), axis=1, keepdims=True)
    gw = 1.0 / sumexp
    lo = N_GROUPS + EXP_PER_GROUP * gidx
    emask = (lanef >= lo) & (lanef < lo + EXP_PER_GROUP)
    el = jnp.where(emask, logits, ninf)
    m1 = jnp.max(el, axis=1, keepdims=True)
    i1 = jnp.min(jnp.where(emask & (el == m1), lanef, big), axis=1, keepdims=True)
    emask2 = emask & (lanef != i1)
    el2 = jnp.where(emask2, logits, ninf)
    m2 = jnp.max(el2, axis=1, keepdims=True)
    i2 = jnp.min(jnp.where(emask2 & (el2 == m2), lanef, big), axis=1, keepdims=True)
    t = jnp.exp(m2 - m1)
    den = 1.0 + t
    return (1.0 / den) * gw, (t / den) * gw, i1 - N_GROUPS, i2 - N_GROUPS


def _post_mix_pieces(src, w, sink, n_rows, parts=1, carry=None):
    (w_ret_o, cln_g, cln_b, w_conv_o, w_out, ln1_g, ln1_b, wr_hi, wr_lo, b_r) = w
    st = {}
    pr = n_rows // parts

    def branch_a():
        st["a"] = _bdot(_silu(src["g"]()) * src["ret"](), w_ret_o[...])

    def branch_b():
        st["b"] = _bdot(_silu(_ln(src["cout"](), cln_g[...], cln_b[...])), w_conv_o[...])

    def merge(p):
        rows = slice(p * pr, (p + 1) * pr)
        mix = _sigmoid(src["gt_a"](rows)) * st["a"][rows] + _sigmoid(src["gt_b"](rows)) * st["b"][rows]
        h = ALPHA * src["x"](rows) + _bdot(mix, w_out[...])
        st["x1", p] = _ln(h, ln1_g[...], ln1_b[...])

    def router(p):
        x1 = st["x1", p]
        x1_hi = x1.astype(BF16)
        x1_lo = (x1 - x1_hi.astype(F32)).astype(BF16)
        both = jnp.dot(x1_hi, wr_lo[...], preferred_element_type=F32)
        st["logits", p] = (both[:, :LANES]
                           + (jnp.dot(x1_lo, wr_hi[...], preferred_element_type=F32) + both[:, LANES:])
                           + b_r[...])

    def route(p):
        st["route", p] = _route(st["logits", p])

    def finish(p):
        sink(p, st["x1", p], *st["route", p])

    def park(p):
        rows = slice(p * pr, (p + 1) * pr)
        carry[0][rows, :] = st["x1", p]
        carry[1][rows, :] = st["logits", p]

    stages = (merge, router, route, finish) if carry is None else (merge, router, park)
    return [branch_a, branch_b] + [lambda p=p, stage=stage: stage(p) for stage in stages for p in range(parts)]


def _parked_pieces(carry, sink, n_rows, parts):
    pr = n_rows // parts
    st = {}

    def route(p):
        st[p] = _route(carry[1][p * pr:(p + 1) * pr, :])

    def finish(p):
        sink(p, carry[0][p * pr:(p + 1) * pr, :], *st[p])

    return [lambda p=p, stage=stage: stage(p) for stage in (route, finish) for p in range(parts)]


def _sort_tile(x1, w1, w2, e1, e2, x1_ref, rw_ref, xs_ref, meta_ref):
    t = x1.shape[0]
    ids_t = _lane_tile((e1, e2), t).T
    e1r, e2r = ids_t[0:1, :], ids_t[1:2, :]
    sub = lax.broadcasted_iota(I32, (LANES, t), 0).astype(F32)
    a1 = (sub == e1r).astype(F32)
    a2 = (sub == e2r).astype(F32)
    ri = lax.broadcasted_iota(I32, (t, t), 0)
    ci = lax.broadcasted_iota(I32, (t, t), 1)
    earlier = (ri < ci).astype(BF16)
    r1 = jnp.dot(a1.astype(BF16), earlier, preferred_element_type=F32)
    r2 = jnp.dot(a2.astype(BF16), earlier, preferred_element_type=F32)
    cnt1 = jnp.sum(a1, axis=1, keepdims=True)
    cnt = cnt1 + jnp.sum(a2, axis=1, keepdims=True)
    nch = jnp.floor((cnt + (CHUNK - 1.0)) * (1.0 / CHUNK))
    ui = lax.broadcasted_iota(I32, (LANES, LANES), 0)
    uj = lax.broadcasted_iota(I32, (LANES, LANES), 1)
    before = (uj < ui).astype(BF16)
    off = jnp.dot(before, jnp.broadcast_to(nch, (LANES, LANES)).astype(BF16),
                  preferred_element_type=F32)[:, 0:1]
    base = off * CHUNK
    pos1r = jnp.sum(a1 * (base + r1), axis=0, keepdims=True)
    pos2r = jnp.sum(a2 * (base + cnt1 + r2), axis=0, keepdims=True)
    slot = lax.broadcasted_iota(I32, (USED_ROWS, t), 0).astype(F32)
    onehot = ((slot == pos1r) | (slot == pos2r)).astype(BF16)
    xs = jnp.dot(onehot, x1.astype(BF16), preferred_element_type=F32)
    pos_cols = jnp.where(sub == 2.0, pos1r, jnp.where(sub == 3.0, pos2r, 0.0)).T
    lane = lax.broadcasted_iota(I32, (t, LANES), 1)
    x1_ref[...] = x1
    rw_ref[...] = jnp.where(lane == 0, w1, jnp.where(lane == 1, w2, pos_cols))
    xs_ref[0:USED_ROWS, :] = xs.astype(BF16)
    xs_ref[USED_ROWS:CAP, :] = jnp.zeros((CAP - USED_ROWS, x1.shape[1]), BF16)
    mlane = lax.broadcasted_iota(I32, (LANES, LANES), 1)
    meta = jnp.where(mlane == 0, cnt, jnp.where(mlane == 1, off, 0.0))
    meta_ref[...] = meta.astype(I32)


def _sort_part(i, x1, w1, w2, e1, e2, x1_ref, rw_ref, xs_ref, meta_ref):
    _sort_tile(x1, w1, w2, e1, e2,
               x1_ref.at[pl.ds(i * TL, TL)], rw_ref.at[pl.ds(i * TL, TL)],
               xs_ref.at[pl.ds(i * CAP, CAP)], meta_ref.at[pl.ds(i * LANES, LANES)])


def _sort_tiles(x1, w1, w2, e1, e2, *outs):
    for i in range(x1.shape[0] // TL):
        rows = slice(i * TL, (i + 1) * TL)
        _sort_part(i, x1[rows], w1[rows], w2[rows], e1[rows], e2[rows], *outs)


GATE_COLS = {3: 0, 6: RET_V, 7: RET_V + D_MODEL}
QKV_COLS = {0: 0, 1: RET_QK, 2: 2 * RET_QK}


def _prompt_mixer_kernel(x_ref, x1s_ref, rws_ref, cos_ref, sin_ref, dec_ref, qdec_ref, kdec_ref, cdec_ref,
                         w_in, b_in, gn_g, gn_b, w_ret_o, conv_w, conv_b, cln_g, cln_b,
                         w_conv_o, w_out, ln1_g, ln1_b, wr_hi, wr_lo, b_r,
                         x1_ref, rw_ref, xs_ref, meta_ref, sret_ref, sconv_ref,
                         ubuf, ushift, qkv_scr, xb_scr, ret_scr, cout_scr, gate_scr, x1_carry, lg_carry,
                         *, n_tiles, tiles_per_seq):
    s = pl.program_id(0)
    li = lax.rem(s, tiles_per_seq)
    outs = (x1_ref, rw_ref, xs_ref, meta_ref)
    slot = dict(ret=ret_scr, cout=cout_scr, gates=gate_scr)
    tail_w = (w_ret_o, cln_g, cln_b, w_conv_o, w_out, ln1_g, ln1_b, wr_hi, wr_lo, b_r)
    carry = (x1_carry, lg_carry)
    tl = x_ref.shape[1]
    parts = tl // TL
    sort_sink = lambda *r: _sort_part(*r, *outs)

    @pl.when(s == 0)
    def _first():
        x1_carry[...] = jnp.zeros(x1_carry.shape, F32)
        lg_carry[...] = jnp.zeros(lg_carry.shape, F32)

    @pl.when((s < n_tiles) & (li == 0))
    def _new_sequence():
        sret_ref[...] = jnp.zeros(sret_ref.shape, F32)
        ubuf[0:CONV_PAD, :] = jnp.zeros((CONV_PAD, CONV_CH), F32)

    @pl.when(s < n_tiles)
    def _mix():
        gcols = lambda kk: slice(GATE_COLS[kk], GATE_COLS[kk] + IN_WIDTHS[kk])
        src = dict(x=lambda r: x_ref[0, r, :], ret=lambda: ret_scr[...], cout=lambda: cout_scr[...],
                   g=lambda: gate_scr[:, gcols(3)], gt_a=lambda r: gate_scr[r, gcols(6)],
                   gt_b=lambda r: gate_scr[r, gcols(7)])
        parked = _parked_pieces(carry, sort_sink, tl, parts)
        head = _prompt_head_pieces(x_ref, cos_ref, sin_ref, dec_ref, qdec_ref, kdec_ref, cdec_ref,
                                   w_in, b_in, gn_g, gn_b, conv_w, conv_b, sret_ref,
                                   ubuf, ushift, qkv_scr, xb_scr, slot, extra=parked)
        tail = _post_mix_pieces(src, tail_w, None, tl, parts=parts, carry=carry)
        for piece in head + tail:
            piece()

    @pl.when(s == n_tiles)
    def _drain():
        for piece in _parked_pieces(carry, sort_sink, tl, parts):
            piece()

    @pl.when(s > n_tiles)
    def _append():
        rws = rws_ref[...]
        _sort_tiles(x1s_ref[...], rws[:, 0:1], rws[:, 1:2], rws[:, 2:3], rws[:, 3:4], *outs)

    @pl.when((s < n_tiles) & (li == tiles_per_seq - 1))
    def _conv_state():
        sconv_ref[0, 0] = ubuf[CONV_PAD - (CONV_WIDTH - 1):CONV_PAD, :]


def _prompt_head_pieces(x_ref, cos_ref, sin_ref, dec_ref, qdec_ref, kdec_ref, cdec_ref,
                        w_in, b_in, gn_g, gn_b, conv_w, conv_b, sret_ref,
                        ubuf, ushift, qkv_scr, xb_scr, slot, extra=()):
    tl = x_ref.shape[1]
    st = {}

    def slab_dot(c0, c1):
        return jnp.dot(xb_scr[...], w_in[:, c0:c1], preferred_element_type=F32) + _rep(b_in[:, c0:c1], tl)

    def glu():
        xb_scr[...] = x_ref[0].astype(BF16)
        u = slab_dot(IN_OFFS[4], IN_OFFS[5]) * _sigmoid(slab_dot(IN_OFFS[5], IN_OFFS[6]))
        ubuf[CONV_PAD:CONV_PAD + tl, :] = u

    nsh = ushift.shape[1]
    span = nsh - (CONV_PAD - SUBLANES)

    def shift_copy(h, s):
        ushift[s - 1] = ubuf[h * span + s:h * span + s + nsh, :]

    slab = 256
    slabs = [(kk, c0) for kk in (0, 1, 2, 3, 6, 7) for c0 in range(IN_OFFS[kk], IN_OFFS[kk + 1], slab)]
    rb = 32
    nrb = tl // rb

    def proj_slab(kk, c0):
        val = slab_dot(c0, c0 + slab)
        if kk in QKV_COLS:
            dst = QKV_COLS[kk] + c0 - IN_OFFS[kk]
            qkv_scr[:, dst:dst + slab] = val
        else:
            dst = GATE_COLS[kk] + c0 - IN_OFFS[kk]
            slot["gates"][:, dst:dst + slab] = val

    def conv_block(r):
        h, rl = divmod(r * rb, span)
        acc = jnp.zeros((rb, CONV_CH), F32) + conv_b[...]
        for j in range(CONV_WIDTH):
            off = j + (CONV_PAD - (CONV_WIDTH - 1))
            s = off % SUBLANES
            base = rl + off - s
            win = (ubuf[h * span + base:h * span + base + rb, :] if s == 0
                   else ushift[s - 1, base:base + rb, :])
            acc = acc + _rep(conv_w[j], rb) * win
        slot["cout"][r * rb:(r + 1) * rb, :] = acc
        if r == nrb - 1:
            ubuf[0:CONV_PAD, :] = ubuf[tl:tl + CONV_PAD, :]

    scale = RET_DK ** -0.5

    def retention(c, h):
        rows = slice(c * RET_CHUNK, (c + 1) * RET_CHUNK)
        cols = slice(h * RET_DK, (h + 1) * RET_DK)
        hcol = lambda kk: slice(QKV_COLS[kk] + h * RET_DK, QKV_COLS[kk] + (h + 1) * RET_DK)
        cosf = cos_ref[rows, :]
        sinf = sin_ref[rows, :]
        qh = _rot(qkv_scr[rows, hcol(0)], cosf, sinf)
        kh = _rot(qkv_scr[rows, hcol(1)], cosf, sinf) * scale
        qb = qh.astype(BF16)
        kb = kh.astype(BF16)
        vb = qkv_scr[rows, hcol(2)].astype(BF16)
        s_old = sret_ref[0, 0, h]
        scores = lax.dot_general(qb, kb, (((1,), (1,)), ((), ())),
                                 preferred_element_type=F32) * dec_ref[h]
        inner = jnp.dot(scores.astype(BF16), vb, preferred_element_type=F32)
        cross = jnp.dot(qb, s_old.astype(BF16), preferred_element_type=F32) * qdec_ref[h]
        kd = (kh * kdec_ref[h]).astype(BF16)
        s_new = cdec_ref[h] * s_old + lax.dot_general(
            kd, vb, (((0,), (0,)), ((), ())), preferred_element_type=F32)
        sret_ref[0, 0, h] = s_new
        slot["ret"][rows, cols] = _ln(inner + cross, gn_g[:, cols], gn_b[:, cols])

    vector_pieces = []
    for r in range(nrb):
        if (r * rb) % span == 0:
            vector_pieces += [lambda h=(r * rb) // span, s=s: shift_copy(h, s) for s in range(1, SUBLANES)]
        vector_pieces.append(lambda r=r: conv_block(r))
    def spread(work, between):
        out = []
        for i, piece in enumerate(work):
            out.append(piece)
            out += between[i * len(between) // len(work):(i + 1) * len(between) // len(work)]
        return out

    as_piece = lambda sl: (lambda: proj_slab(*sl))
    early = [as_piece(sl) for sl in slabs if sl[0] in (0, 1, 2, 3)]
    late = [as_piece(sl) for sl in slabs if sl[0] not in (0, 1, 2, 3)]
    for i, piece in enumerate(extra):
        early.insert(min(1 + 2 * i, len(early)), piece)
    ret_pieces = [lambda c=c, h=h: retention(c, h) for c in range(tl // RET_CHUNK) for h in range(RET_HEADS)]
    return [glu] + spread(vector_pieces, early) + spread(ret_pieces, late)


def _sample_mixer_kernel(x_ref, cos_ref, sin_ref, pdec_ref, qdec_ref, kdec_ref, cdec_ref, conv_w,
                         sret_in, sconv_in,
                         w_in, b_in, gn_g, gn_b, w_ret_o, conv_b, cln_g, cln_b,
                         w_conv_o, w_out, ln1_g, ln1_b, wr_hi, wr_lo, b_r,
                         x1_ref, rw_ref, sret_ref, sconv_ref,
                         ret_scr, cout_scr, gate_scr, xwin, u_scr):
    i = pl.program_id(0)
    t = cos_ref.shape[0]
    ls = t // BB_SAMPLE
    nstate = CONV_WIDTH - 1
    r0 = pl.multiple_of(i * t, t)
    xb = x_ref[pl.ds(r0, t), :].astype(BF16)

    def proj(k):
        c0, c1 = IN_OFFS[k], IN_OFFS[k + 1]
        return jnp.dot(xb, w_in[:, c0:c1], preferred_element_type=F32) + _rep(b_in[:, c0:c1], t)

    q = proj(0)
    k = proj(1)
    v = proj(2)
    scale = RET_DK ** -0.5
    cosf = cos_ref[...]
    sinf = sin_ref[...]
    row = lax.broadcasted_iota(I32, (t, RET_DK), 0)
    pos = row % ls
    row8 = lax.broadcasted_iota(I32, (SUBLANES, RET_DK), 0)
    per_tile = SUBLANES // ls
    for h in range(RET_HEADS):
        cols = slice(h * RET_DK, (h + 1) * RET_DK)
        qh = _rot(q[:, cols], cosf, sinf)
        kh = _rot(k[:, cols], cosf, sinf) * scale
        vh = v[:, cols]
        inner = jnp.zeros((t, RET_DV), F32)
        for s in range(ls):
            ks = kh if s == 0 else pltpu.roll(kh, s, axis=0)
            vs = vh if s == 0 else pltpu.roll(vh, s, axis=0)
            dotp = jnp.sum(qh * ks, axis=1, keepdims=True) * pdec_ref[h, s]
            inner = inner + jnp.where(pos >= s, dotp, 0.0) * vs
        kd = kh * kdec_ref[h]
        for tile in range(t // SUBLANES):
            rows = slice(tile * SUBLANES, (tile + 1) * SUBLANES)
            q8 = qh[rows, :]
            kd8 = kd[rows, :]
            v8 = vh[rows, :]
            seqs = [tile * per_tile + sub for sub in range(per_tile)]
            mine = [(row8 >= sub * ls) & (row8 < (sub + 1) * ls) for sub in range(per_tile)]
            s_old = [sret_in[0, b, h] for b in seqs]
            c_all = jnp.dot(q8, jnp.concatenate(s_old, axis=1), preferred_element_type=F32)
            upd = lax.dot_general(jnp.concatenate([jnp.where(m, kd8, 0.0) for m in mine], axis=1), v8,
                                  (((0,), (0,)), ((), ())), preferred_element_type=F32)
            cross8 = jnp.zeros((SUBLANES, RET_DV), F32)
            for sub, b in enumerate(seqs):
                cross8 = jnp.where(mine[sub], c_all[:, sub * RET_DV:(sub + 1) * RET_DV], cross8)
                sret_ref[0, b, h] = cdec_ref[h] * s_old[sub] + upd[sub * RET_DK:(sub + 1) * RET_DK, :]
            ret_scr[pl.ds(r0 + tile * SUBLANES, SUBLANES), cols] = (
                inner[rows, :] + cross8 * qdec_ref[h, rows, :])
        ret_scr[pl.ds(r0, t), cols] = _ln(ret_scr[pl.ds(r0, t), cols], gn_g[:, cols], gn_b[:, cols])

    u = proj(4) * _sigmoid(proj(5))
    nslab = CONV_CH // LANES
    xwin[0:nstate] = sconv_in[...]
    for sl in range(nslab):
        u_scr[sl] = u[:, sl * LANES:(sl + 1) * LANES]
    for p in range(ls):
        for sl in range(nslab):
            xwin[XPAD_NEW + p, :, sl * LANES:(sl + 1) * LANES] = u_scr[sl, pl.ds(p, BB_SAMPLE, stride=ls), :]
    win_row = lambda m: m if m < nstate else XPAD_NEW + (m - nstate)
    for p in range(ls):
        res = jnp.zeros((BB_SAMPLE, CONV_CH), F32) + conv_b[...]
        for j in range(CONV_WIDTH):
            res = res + _rep(conv_w[j], BB_SAMPLE) * xwin[win_row(p + j)]
        for sl in range(nslab):
            cout_scr[sl, pl.ds(r0 + p, BB_SAMPLE, stride=ls), :] = res[:, sl * LANES:(sl + 1) * LANES]
    sconv_ref[0:nstate - ls] = xwin[ls:nstate]
    sconv_ref[nstate - ls:nstate] = xwin[XPAD_NEW:XPAD_NEW + ls]
    for kk in (3, 6, 7):
        gate_scr[pl.ds(r0, t), GATE_COLS[kk]:GATE_COLS[kk] + IN_WIDTHS[kk]] = proj(kk)

    @pl.when(i == pl.num_programs(0) - 1)
    def _second_half():
        n = x_ref.shape[0]

        def sink(p, x1, w1, w2, e1, e2):
            x1_ref[...] = x1
            rw_ref[...] = _lane_tile((w1, w2, e1, e2), n)

        gcols = lambda kk: slice(GATE_COLS[kk], GATE_COLS[kk] + IN_WIDTHS[kk])
        src = dict(x=lambda r: x_ref[r, :], ret=lambda: ret_scr[...],
                   cout=lambda: jnp.concatenate([cout_scr[sl] for sl in range(CONV_CH // LANES)], axis=1),
                   g=lambda: gate_scr[:, gcols(3)], gt_a=lambda r: gate_scr[r, gcols(6)],
                   gt_b=lambda r: gate_scr[r, gcols(7)])
        for piece in _post_mix_pieces(
                src, (w_ret_o, cln_g, cln_b, w_conv_o, w_out, ln1_g, ln1_b, wr_hi, wr_lo, b_r), sink, n):
            piece()


def _ffn_kernel(te_ref, nvalid_ref, nexte_ref, chunk_ref, xs_hbm, wgu_hbm, wdn_hbm, ys_hbm,
                xbuf, obuf, wgu_f, wdn_f, wgu_b, wdn_b, sem_in, sem_out, sem_w):
    del xs_hbm
    nvalid = nvalid_ref[0]

    def chunk_rows(tile, c):
        return pl.ds(pl.multiple_of(chunk_ref[tile * TILE_CHUNKS + c] * CHUNK, CHUNK), CHUNK)

    def start_in(tile, s):
        for c in range(TILE_CHUNKS):
            pltpu.make_async_copy(ys_hbm.at[chunk_rows(tile, c)],
                                  xbuf.at[s, pl.ds(c * CHUNK, CHUNK)], sem_in.at[s]).start()

    def start_out(tile, s):
        for c in range(TILE_CHUNKS):
            pltpu.make_async_copy(obuf.at[s, pl.ds(c * CHUNK, CHUNK)],
                                  ys_hbm.at[chunk_rows(tile, c)], sem_out.at[s]).start()

    def wait_in(s):
        pltpu.make_async_copy(ys_hbm.at[pl.ds(0, TM_FFN)], xbuf.at[s], sem_in.at[s]).wait()

    def wait_out(s):
        pltpu.make_async_copy(obuf.at[s], ys_hbm.at[pl.ds(0, TM_FFN)], sem_out.at[s]).wait()

    def weight_copies(e, s):
        return (pltpu.make_async_copy(wgu_hbm.at[e], wgu_f.at[s], sem_w.at[s]),
                pltpu.make_async_copy(wdn_hbm.at[e], wdn_f.at[s], sem_w.at[s]))

    def tile(i, wslot):
        slot = i % 2

        @pl.when(i >= 2)
        def _retire():
            wait_out(slot)

        wait_in(slot)
        changed = (i == 0) | (te_ref[i] != te_ref[jnp.maximum(i - 1, 0)])

        @pl.when(changed)
        def _new_expert():
            for cp in weight_copies(te_ref[i], wslot):
                cp.wait()
            wgu_b[...] = wgu_f[wslot].astype(BF16)
            wdn_b[...] = wdn_f[wslot].astype(BF16)

            @pl.when(nexte_ref[i] >= 0)
            def _():
                for cp in weight_copies(nexte_ref[i], 1 - wslot):
                    cp.start()

        x = xbuf[slot]
        y = jnp.zeros((TM_FFN, wdn_b.shape[1]), F32)
        for c0 in range(0, EXP_FF, FFN_COLS):
            hg = jnp.dot(x, wgu_b[:, c0:c0 + FFN_COLS], preferred_element_type=F32)
            hu = jnp.dot(x, wgu_b[:, EXP_FF + c0:EXP_FF + c0 + FFN_COLS], preferred_element_type=F32)
            y = y + _bdot(_silu(hg) * hu, wdn_b[c0:c0 + FFN_COLS, :])
            if c0 == 0:
                start_in(jnp.where(i + 1 < nvalid, i + 1, 0), 1 - slot)
        obuf[slot] = y.astype(BF16)
        start_out(i, slot)
        return jnp.where(changed, 1 - wslot, wslot)

    for cp in weight_copies(te_ref[0], 0):
        cp.start()
    start_in(0, 0)
    lax.fori_loop(0, nvalid, tile, jnp.int32(0))
    wait_out(nvalid % 2)
    wait_out(1 - nvalid % 2)
    wait_in(nvalid % 2)


def _final_kernel(ys_ref, x1_ref, rw_ref, pp_ref, ps_ref, ln2_g, ln2_b, w_pg, b_pg, w_ple,
                  yp_ref, ys_out_ref, *, n_prompt_tiles):
    i = pl.program_id(0)
    x1 = x1_ref[...]
    slot = lax.broadcasted_iota(I32, (TL, USED_ROWS), 1).astype(F32)
    parts = []
    for b in range(x1.shape[0] // TL):
        rw = rw_ref[b * TL:(b + 1) * TL, :]
        w1, w2, pos1, pos2 = rw[:, 0:1], rw[:, 1:2], rw[:, 2:3], rw[:, 3:4]
        ys = ys_ref[b * CAP:b * CAP + USED_ROWS, :]
        comb = jnp.where(slot == pos1, w1, jnp.where(slot == pos2, w2, 0.0)).astype(BF16)
        parts.append(jnp.dot(comb, ys, preferred_element_type=F32))
    moe = parts[0] if len(parts) == 1 else jnp.concatenate(parts, axis=0)
    x2 = _ln(ALPHA * x1 + moe, ln2_g[...], ln2_b[...])
    gate = _sigmoid(_bdot(x2, w_pg[...]) + b_pg[...])
    p = jnp.where(i < n_prompt_tiles, pp_ref[...], ps_ref[...])
    y = x2 + gate * _bdot(p, w_ple[...])

    @pl.when(i < n_prompt_tiles)
    def _prompt():
        yp_ref[...] = y

    @pl.when(i >= n_prompt_tiles)
    def _sample():
        ys_out_ref[...] = y


def _rope_tables(pos):
    half = RET_DK // 2
    inv_freq = ROPE_BASE ** (-np.arange(half, dtype=np.float64) / half)
    ang = np.asarray(pos, np.float64)[:, None] * inv_freq[None, :]
    cos = np.cos(ang)
    sin = np.sin(ang)
    return (np.concatenate([cos, cos], axis=-1).astype(np.float32),
            np.concatenate([-sin, sin], axis=-1).astype(np.float32))


def _log_gamma():
    return np.log(1.0 - 2.0 ** (-5.0 - np.arange(RET_HEADS, dtype=np.float64)))


def _const_spec(shape):
    nd = len(shape)
    return pl.BlockSpec(shape, lambda *_: (0,) * nd, pipeline_mode=pl.Buffered(1))


def _chunk_plan(meta, n_blocks, n_ffn_tiles):
    assert n_blocks * BLOCK_SPARE >= N_EXPERTS * (TILE_CHUNKS - 1)
    m = meta.reshape(n_blocks, LANES, LANES)
    cnt = m[:, :N_EXPERTS, 0]
    off = m[:, :N_EXPERTS, 1]
    nch = (cnt + (CHUNK - 1)) // CHUNK
    cum = jnp.cumsum(nch, axis=0)
    total = cum[-1:]
    tiles_e = (total + TILE_CHUNKS - 1) // TILE_CHUNKS
    tile_end = jnp.cumsum(tiles_e, axis=1)
    tile_start = tile_end - tiles_e
    tid = jnp.arange(n_ffn_tiles, dtype=I32)[:, None]
    owner = (tid >= tile_start) & (tid < tile_end)
    pick_e = lambda v: jnp.sum(jnp.where(owner, v, 0), axis=1, keepdims=True)
    te = pick_e(jnp.arange(N_EXPERTS, dtype=I32)[None, :])
    k = (tid - pick_e(tile_start)) * TILE_CHUNKS + jnp.arange(TILE_CHUNKS, dtype=I32)[None, :]
    total_t = pick_e(total)
    real = k < total_t
    by_tile = lambda v: jnp.sum(jnp.where(owner[:, None, :], v[None, :, :], 0), axis=2)
    cum_t = by_tile(cum)
    blk = jnp.minimum(jnp.sum((cum_t[:, None, :] <= k[:, :, None]).astype(I32), axis=2), n_blocks - 1)
    at_blk = blk[:, :, None] == jnp.arange(n_blocks, dtype=I32)[None, None, :]
    pick_b = lambda v: jnp.sum(jnp.where(at_blk, v[:, None, :], 0), axis=2)
    excl = pick_b(cum_t - by_tile(nch))
    off_t = pick_b(by_tile(off))
    spare = te * (TILE_CHUNKS - 1) + jnp.maximum(k - total_t, 0) % TILE_CHUNKS
    spare_chunk = (spare // BLOCK_SPARE) * BLOCK_CHUNKS + BLOCK_USED + spare % BLOCK_SPARE
    chunk = jnp.where(real, blk * BLOCK_CHUNKS + off_t + (k - excl), spare_chunk)
    n_valid = jnp.sum(tiles_e, axis=1)
    eid = jnp.arange(N_EXPERTS, dtype=I32)
    later = (eid[None, :] > eid[:, None]) & (tiles_e > 0)
    next_e = jnp.min(jnp.where(later, eid[None, :], N_EXPERTS), axis=1)[None, :]
    next_t = pick_e(jnp.where(next_e < N_EXPERTS, next_e, -1))
    return (te.reshape(-1).astype(I32), n_valid.astype(I32), next_t.reshape(-1).astype(I32),
            chunk.reshape(-1).astype(I32))


def kernel(x_prompt, x_sample, state_ret, state_conv, p_prompt, p_sample, w_in, b_in, ret_gn_g, ret_gn_b,
           w_ret_o, conv_w, conv_b, conv_ln_g, conv_ln_b, w_conv_o, w_out, ln1_g, ln1_b, w_grp, b_grp,
           w_exp, b_exp, w_gu, w_dn, ln2_g, ln2_b, w_pg, b_pg, w_ple):
    assert DEPTH == 1 and w_in.shape[0] == 1
    bp, lp, d = x_prompt.shape
    bs, ls, _ = x_sample.shape
    n_p, n_s = bp * lp, bs * ls
    n_tok = n_p + n_s
    assert lp % TL == 0 and n_s % TL == 0 and bs % BB_SAMPLE == 0 and SUBLANES % ls == 0
    n_blocks = n_tok // TL

    f32c = lambda a, shape: jnp.asarray(np.broadcast_to(a, shape).astype(np.float32))
    lg = _log_gamma()
    c = RET_CHUNK
    idx = np.arange(c, dtype=np.float64)
    rel = idx[:, None] - idx[None, :]
    causal = rel >= 0
    decay = np.where(causal[None], np.exp(np.where(causal, rel, 0.0)[None] * lg[:, None, None]), 0.0)
    decay = f32c(decay, decay.shape)
    q_decay = np.exp((idx[:, None] + 1.0) * lg[None, :])
    k_decay = np.exp((c - 1.0 - idx[:, None]) * lg[None, :])
    chunk_decay = np.exp(c * lg)
    qdec_p = f32c(q_decay.T[:, :, None], (RET_HEADS, c, RET_DK))
    kdec_p = f32c(k_decay.T[:, :, None], (RET_HEADS, c, RET_DK))
    cdec_p = f32c(chunk_decay[:, None, None], (RET_HEADS, 1, RET_DV))
    cos_p, sin_p = (jnp.asarray(a) for a in _rope_tables(np.arange(lp)))

    ts = BB_SAMPLE * ls
    idx_s = np.arange(ls, dtype=np.float64)
    pdec_s = np.exp(idx_s[None, :] * lg[:, None])
    pdec_s = f32c(pdec_s[:, :, None, None], (RET_HEADS, ls, 1, RET_DK))
    qd_s = np.exp((idx_s[:, None] + 1.0) * lg[None, :])
    kd_s = np.exp((ls - 1.0 - idx_s[:, None]) * lg[None, :])
    qdec_s = f32c(np.tile(qd_s.T, (1, BB_SAMPLE))[:, :, None], (RET_HEADS, ts, RET_DK))
    kdec_s = f32c(np.tile(kd_s.T, (1, BB_SAMPLE))[:, :, None], (RET_HEADS, ts, RET_DK))
    cdec_s = f32c(np.exp(ls * lg)[:, None, None], (RET_HEADS, 1, RET_DV))
    cos_s, sin_s = (jnp.asarray(a) for a in _rope_tables(np.tile(PAST_LEN + np.arange(ls), BB_SAMPLE)))

    w_in_b = w_in[0].astype(BF16)
    w_ret_o_b = w_ret_o[0].astype(BF16)
    w_conv_o_b = w_conv_o[0].astype(BF16)
    w_out_b = w_out[0].astype(BF16)
    w_pg_b = w_pg[0].astype(BF16)
    w_ple_b = w_ple[0].astype(BF16)
    n_route = N_GROUPS + N_EXPERTS
    w_r = jnp.concatenate([w_grp[0], w_exp[0], jnp.zeros((d, LANES - n_route), F32)], axis=1)
    wr_hi = w_r.astype(BF16)
    wr_lo = jnp.concatenate([wr_hi, (w_r - wr_hi.astype(F32)).astype(BF16)], axis=1)
    b_r = jnp.concatenate([b_grp[0], b_exp[0], jnp.zeros((LANES - n_route,), F32)]).reshape(1, LANES)
    row = lambda a: a.reshape(1, -1)
    conv_w0 = conv_w[0]
    nstate = CONV_WIDTH - 1

    rep8 = lambda a: jnp.broadcast_to(a[..., None, :], a.shape[:-1] + (SUBLANES, a.shape[-1]))
    shared_w = (w_in_b, rep8(b_in[0]), row(ret_gn_g[0]), row(ret_gn_b[0]), w_ret_o_b)
    tail_w = (row(conv_ln_g[0]), row(conv_ln_b[0]), w_conv_o_b, w_out_b, row(ln1_g[0]), row(ln1_b[0]),
              wr_hi, wr_lo, b_r)

    nbt = bs // BB_SAMPLE
    xs2 = x_sample.reshape(n_s, d)
    conv_state_t = jnp.transpose(state_conv[0], (1, 0, 2))
    sample_in = ((xs2, cos_s, sin_s, pdec_s, qdec_s, kdec_s, cdec_s, rep8(conv_w0), state_ret, conv_state_t)
                 + shared_w + (row(conv_b[0]),) + tail_w)
    conv_state_spec = pl.BlockSpec((nstate, BB_SAMPLE, CONV_CH), lambda i: (0, i, 0))
    sample_specs = (
        [_const_spec(a.shape) for a in sample_in[0:8]]
        + [pl.BlockSpec((1, BB_SAMPLE, RET_HEADS, RET_DK, RET_DV), lambda i: (0, i, 0, 0, 0)), conv_state_spec]
        + [_const_spec(a.shape) for a in sample_in[10:]]
    )
    tok_spec_s = lambda w: pl.BlockSpec((n_s, w), lambda i: (0, 0))
    x1_s, rw_s, ret_s, conv_s_t = pl.pallas_call(
        _sample_mixer_kernel,
        grid=(nbt,),
        in_specs=sample_specs,
        out_specs=[
            tok_spec_s(d), tok_spec_s(LANES),
            pl.BlockSpec((1, BB_SAMPLE, RET_HEADS, RET_DK, RET_DV), lambda i: (0, i, 0, 0, 0)),
            conv_state_spec,
        ],
        out_shape=[
            jax.ShapeDtypeStruct((n_s, d), F32),
            jax.ShapeDtypeStruct((n_s, LANES), F32),
            jax.ShapeDtypeStruct(state_ret.shape, F32),
            jax.ShapeDtypeStruct(conv_state_t.shape, F32),
        ],
        scratch_shapes=[
            pltpu.VMEM((n_s, RET_V), F32),
            pltpu.VMEM((CONV_CH // LANES, n_s, LANES), F32),
            pltpu.VMEM((n_s, RET_V + 2 * D_MODEL), F32),
            pltpu.VMEM((XPAD_ROWS, BB_SAMPLE, CONV_CH), F32),
            pltpu.VMEM((CONV_CH // LANES, ts, LANES), F32),
        ],
        compiler_params=pltpu.CompilerParams(
            dimension_semantics=("arbitrary",), vmem_limit_bytes=VMEM_LIMIT),
        name="sample_mixer",
    )(*sample_in)
    conv_s = jnp.transpose(conv_s_t, (1, 0, 2))[None]

    assert lp % TLM == 0 and n_s % TLM == 0 and TLM % TL == 0
    nlt = lp // TLM
    npt = n_p // TLM
    nst = n_s // TLM
    sub = TLM // TL
    prompt_in = ((x_prompt, x1_s, rw_s, cos_p, sin_p, decay, qdec_p, kdec_p, cdec_p)
                 + shared_w + (rep8(conv_w0), row(conv_b[0])) + tail_w)
    head_tile = lambda s: jnp.minimum(s, npt - 1)
    sample_tile = lambda s: jnp.maximum(s - (npt + 1), 0)
    sample_spec = lambda w: pl.BlockSpec((TLM, w), lambda s: (sample_tile(s), 0))
    prompt_specs = [
        pl.BlockSpec((1, TLM, d), lambda s: (head_tile(s) // nlt, head_tile(s) % nlt, 0)),
        sample_spec(d), sample_spec(LANES),
        pl.BlockSpec((TLM, RET_DK), lambda s: (head_tile(s) % nlt, 0)),
        pl.BlockSpec((TLM, RET_DK), lambda s: (head_tile(s) % nlt, 0)),
    ] + [_const_spec(a.shape) for a in prompt_in[5:]]
    tok_spec_p = lambda rows, w: pl.BlockSpec((rows, w), lambda s: (jnp.maximum(s - 1, 0), 0))
    x1_all, rw_all, xs_all, meta, ret_p, conv_p = pl.pallas_call(
        functools.partial(_prompt_mixer_kernel, n_tiles=npt, tiles_per_seq=nlt),
        grid=(npt + 1 + nst,),
        in_specs=prompt_specs,
        out_specs=[
            tok_spec_p(TLM, d), tok_spec_p(TLM, LANES), tok_spec_p(sub * CAP, d), tok_spec_p(sub * LANES, LANES),
            pl.BlockSpec((1, 1, RET_HEADS, RET_DK, RET_DV), lambda s: (0, head_tile(s) // nlt, 0, 0, 0)),
            pl.BlockSpec((1, 1, nstate, CONV_CH), lambda s: (0, head_tile(s) // nlt, 0, 0)),
        ],
        out_shape=[
            jax.ShapeDtypeStruct((n_tok, d), F32),
            jax.ShapeDtypeStruct((n_tok, LANES), F32),
            jax.ShapeDtypeStruct((n_blocks * CAP, d), BF16),
            jax.ShapeDtypeStruct((n_blocks * LANES, LANES), I32),
            jax.ShapeDtypeStruct((1, bp, RET_HEADS, RET_DK, RET_DV), F32),
            jax.ShapeDtypeStruct((1, bp, nstate, CONV_CH), F32),
        ],
        scratch_shapes=[
            pltpu.VMEM((TLM + CONV_PAD, CONV_CH), F32),
            pltpu.VMEM((SUBLANES - 1, TL + CONV_PAD - SUBLANES, CONV_CH), F32),
            pltpu.VMEM((TLM, 2 * RET_QK + RET_V), F32),
            pltpu.VMEM((TLM, d), BF16),
            pltpu.VMEM((TLM, RET_V), F32),
            pltpu.VMEM((TLM, CONV_CH), F32),
            pltpu.VMEM((TLM, RET_V + 2 * D_MODEL), F32),
            pltpu.VMEM((TLM, d), F32),
            pltpu.VMEM((TLM, LANES), F32),
        ],
        compiler_params=pltpu.CompilerParams(
            dimension_semantics=("arbitrary",), vmem_limit_bytes=VMEM_LIMIT),
        name="prompt_mixer",
    )(*prompt_in)

    assert TOP_K * n_tok // TM_FFN >= 3
    max_chunks = n_blocks * (TOP_K * TL // CHUNK + N_EXPERTS - 1)
    n_ffn_tiles = (max_chunks + N_EXPERTS * (TILE_CHUNKS - 1)) // TILE_CHUNKS
    tile_e, n_valid_tiles, next_e, chunk_ids = _chunk_plan(meta, n_blocks, n_ffn_tiles)

    ys_all = pl.pallas_call(
        _ffn_kernel,
        grid_spec=pltpu.PrefetchScalarGridSpec(
            num_scalar_prefetch=4,
            grid=(1,),
            in_specs=[pl.BlockSpec(memory_space=pl.ANY)] * 3,
            out_specs=pl.BlockSpec(memory_space=pl.ANY),
            scratch_shapes=[
                pltpu.VMEM((2, TM_FFN, d), BF16),
                pltpu.VMEM((2, TM_FFN, d), BF16),
                pltpu.VMEM((2, d, 2 * EXP_FF), F32),
                pltpu.VMEM((2, EXP_FF, d), F32),
                pltpu.VMEM((d, 2 * EXP_FF), BF16),
                pltpu.VMEM((EXP_FF, d), BF16),
                pltpu.SemaphoreType.DMA((2,)),
                pltpu.SemaphoreType.DMA((2,)),
                pltpu.SemaphoreType.DMA((2,)),
            ],
        ),
        out_shape=jax.ShapeDtypeStruct(xs_all.shape, BF16),
        input_output_aliases={4: 0},
        compiler_params=pltpu.CompilerParams(
            dimension_semantics=("arbitrary",), vmem_limit_bytes=VMEM_LIMIT),
        name="expert_ffn",
    )(tile_e, n_valid_tiles, next_e, chunk_ids, xs_all, w_gu[0], w_dn[0])

    assert n_p % TLF == 0 and n_s % TLF == 0
    npt = n_p // TLF
    fsub = TLF // TL
    pp2 = p_prompt.reshape(n_p, PLE_DIM)
    ps2 = p_sample.reshape(n_s, PLE_DIM)
    tok_f = lambda rows, w: pl.BlockSpec((rows, w), lambda i: (i, 0))
    y_p, y_s = pl.pallas_call(
        functools.partial(_final_kernel, n_prompt_tiles=npt),
        grid=(n_tok // TLF,),
        in_specs=[
            tok_f(fsub * CAP, d), tok_f(TLF, d), tok_f(TLF, LANES),
            pl.BlockSpec((TLF, PLE_DIM), lambda i: (jnp.minimum(i, npt - 1), 0)),
            pl.BlockSpec((TLF, PLE_DIM), lambda i: (jnp.maximum(i - npt, 0), 0)),
            _const_spec((1, d)), _const_spec((1, d)), _const_spec((d, d)), _const_spec((1, d)),
            _const_spec((PLE_DIM, d)),
        ],
        out_specs=[
            pl.BlockSpec((TLF, d), lambda i: (jnp.minimum(i, npt - 1), 0)),
            pl.BlockSpec((TLF, d), lambda i: (jnp.maximum(i - npt, 0), 0)),
        ],
        out_shape=[jax.ShapeDtypeStruct((n_p, d), F32), jax.ShapeDtypeStruct((n_s, d), F32)],
        compiler_params=pltpu.CompilerParams(
            dimension_semantics=("arbitrary",), vmem_limit_bytes=VMEM_LIMIT),
        name="moe_combine_final",
    )(ys_all, x1_all, rw_all, pp2, ps2,
      row(ln2_g[0]), row(ln2_b[0]), w_pg_b, row(b_pg[0]), w_ple_b)

    return (y_p.reshape(bp, lp, d), y_s.reshape(bs, ls, d), ret_p, conv_p, ret_s, conv_s)
```

```python
import functools

import jax
import jax.numpy as jnp
import numpy as np
from jax import lax
from jax.experimental import pallas as pl
from jax.experimental.pallas import tpu as pltpu

F32 = jnp.float32
BF16 = jnp.bfloat16
I32 = jnp.int32

D_MODEL = 1024
PAST_LEN = 16384
RET_HEADS = 4
RET_DK = 128
RET_DV = 128
RET_QK = RET_HEADS * RET_DK
RET_V = RET_HEADS * RET_DV
RET_CHUNK = 128
ROPE_BASE = 10000.0
CONV_CH = 512
CONV_WIDTH = 31
N_GROUPS = 4
EXP_PER_GROUP = 4
N_EXPERTS = N_GROUPS * EXP_PER_GROUP
TOP_K = 2
EXP_FF = 512
PLE_DIM = 256
DEPTH = 1
ALPHA = (2 * DEPTH) ** 0.25
LN_EPS = 1e-5
IN_WIDTHS = (RET_QK, RET_QK, RET_V, RET_V, CONV_CH, CONV_CH, D_MODEL, D_MODEL)
IN_OFFS = tuple(int(s) for s in np.cumsum((0,) + IN_WIDTHS))

LANES = 128
SUBLANES = 8
VMEM_LIMIT = 56 * 1024 * 1024

TL = 256
TLM = 512
TLF = 512
BB_SAMPLE = 16
CHUNK = 2 * SUBLANES
TILE_CHUNKS = 32
BLOCK_USED = -(-(TOP_K * TL + N_EXPERTS * (CHUNK - 1)) // LANES) * LANES // CHUNK
BLOCK_SPARE = LANES // CHUNK
BLOCK_CHUNKS = BLOCK_USED + BLOCK_SPARE
USED_ROWS = BLOCK_USED * CHUNK
CAP = BLOCK_CHUNKS * CHUNK
TM_FFN = TILE_CHUNKS * CHUNK
FFN_COLS = 256
CONV_PAD = 32
XPAD_NEW = 32
XPAD_ROWS = 40


def _ln(x, g, b):
    mu = jnp.mean(x, axis=-1, keepdims=True)
    d = x - mu
    var = jnp.mean(d * d, axis=-1, keepdims=True)
    return d * lax.rsqrt(var + LN_EPS) * g + b


def _sigmoid(x):
    return 1.0 / (1.0 + jnp.exp(-x))


def _rep(v8, rows):
    return v8 if rows == SUBLANES else jnp.concatenate([v8] * (rows // SUBLANES), axis=0)


def _silu(x):
    return x * _sigmoid(x)


def _bdot(a, b):
    return jnp.dot(a.astype(BF16), b, preferred_element_type=F32)


def _rot(t, cosf, sinf):
    return t * cosf + pltpu.roll(t, RET_DK // 2, axis=1) * sinf


def _lane_tile(cols, rows):
    lane = lax.broadcasted_iota(I32, (rows, LANES), 1)
    out = jnp.zeros((rows, LANES), F32)
    for i, col in enumerate(cols):
        out = jnp.where(lane == i, col, out)
    return out


def _route(logits):
    lane = lax.broadcasted_iota(I32, logits.shape, 1)
    lanef = lane.astype(F32)
    ninf = jnp.float32(-jnp.inf)
    big = jnp.float32(LANES)
    gmask = lane < N_GROUPS
    gl = jnp.where(gmask, logits, ninf)
    gmax = jnp.max(gl, axis=1, keepdims=True)
    gidx = jnp.min(jnp.where(gmask & (gl == gmax), lanef, big), axis=1, keepdims=True)
    sumexp = jnp.sum(jnp.where(gmask, jnp.exp(gl - gmax), 0.0), axis=1, keepdims=True)
    gw = 1.0 / sumexp
    lo = N_GROUPS + EXP_PER_GROUP * gidx
    emask = (lanef >= lo) & (lanef < lo + EXP_PER_GROUP)
    el = jnp.where(emask, logits, ninf)
    m1 = jnp.max(el, axis=1, keepdims=True)
    i1 = jnp.min(jnp.where(emask & (el == m1), lanef, big), axis=1, keepdims=True)
    emask2 = emask & (lanef != i1)
    el2 = jnp.where(emask2, logits, ninf)
    m2 = jnp.max(el2, axis=1, keepdims=True)
    i2 = jnp.min(jnp.where(emask2 & (el2 == m2), lanef, big), axis=1, keepdims=True)
    t = jnp.exp(m2 - m1)
    den = 1.0 + t
    return (1.0 / den) * gw, (t / den) * gw, i1 - N_GROUPS, i2 - N_GROUPS


def _post_mix_pieces(src, w, sink, n_rows, parts=1, carry=None):
    (w_ret_o, cln_g, cln_b, w_conv_o, w_out, ln1_g, ln1_b, wr_hi, wr_lo, b_r) = w
    st = {}
    pr = n_rows // parts

    def branch_a():
        st["a"] = _bdot(_silu(src["g"]()) * src["ret"](), w_ret_o[...])

    def branch_b():
        st["b"] = _bdot(_silu(_ln(src["cout"](), cln_g[...], cln_b[...])), w_conv_o[...])

    def merge(p):
        rows = slice(p * pr, (p + 1) * pr)
        mix = _sigmoid(src["gt_a"](rows)) * st["a"][rows] + _sigmoid(src["gt_b"](rows)) * st["b"][rows]
        h = ALPHA * src["x"](rows) + _bdot(mix, w_out[...])
        st["x1", p] = _ln(h, ln1_g[...], ln1_b[...])

    def router(p):
        x1 = st["x1", p]
        x1_hi = x1.astype(BF16)
        x1_lo = (x1 - x1_hi.astype(F32)).astype(BF16)
        both = jnp.dot(x1_hi, wr_lo[...], preferred_element_type=F32)
        st["logits", p] = (both[:, :LANES]
                           + (jnp.dot(x1_lo, wr_hi[...], preferred_element_type=F32) + both[:, LANES:])
                           + b_r[...])

    def route(p):
        st["route", p] = _route(st["logits", p])

    def finish(p):
        sink(p, st["x1", p], *st["route", p])

    def park(p):
        rows = slice(p * pr, (p + 1) * pr)
        carry[0][rows, :] = st["x1", p]
        carry[1][rows, :] = st["logits", p]

    stages = (merge, router, route, finish) if carry is None else (merge, router, park)
    return [branch_a, branch_b] + [lambda p=p, stage=stage: stage(p) for stage in stages for p in range(parts)]


def _parked_pieces(carry, sink, n_rows, parts):
    pr = n_rows // parts
    st = {}

    def route(p):
        st[p] = _route(carry[1][p * pr:(p + 1) * pr, :])

    def finish(p):
        sink(p, carry[0][p * pr:(p + 1) * pr, :], *st[p])

    return [lambda p=p, stage=stage: stage(p) for stage in (route, finish) for p in range(parts)]


def _sort_tile(x1, w1, w2, e1, e2, x1_ref, rw_ref, xs_ref, meta_ref):
    t = x1.shape[0]
    ids_t = _lane_tile((e1, e2), t).T
    e1r, e2r = ids_t[0:1, :], ids_t[1:2, :]
    sub = lax.broadcasted_iota(I32, (LANES, t), 0).astype(F32)
    a1 = (sub == e1r).astype(F32)
    a2 = (sub == e2r).astype(F32)
    ri = lax.broadcasted_iota(I32, (t, t), 0)
    ci = lax.broadcasted_iota(I32, (t, t), 1)
    earlier = (ri < ci).astype(BF16)
    r1 = jnp.dot(a1.astype(BF16), earlier, preferred_element_type=F32)
    r2 = jnp.dot(a2.astype(BF16), earlier, preferred_element_type=F32)
    cnt1 = jnp.sum(a1, axis=1, keepdims=True)
    cnt = cnt1 + jnp.sum(a2, axis=1, keepdims=True)
    nch = jnp.floor((cnt + (CHUNK - 1.0)) * (1.0 / CHUNK))
    ui = lax.broadcasted_iota(I32, (LANES, LANES), 0)
    uj = lax.broadcasted_iota(I32, (LANES, LANES), 1)
    before = (uj < ui).astype(BF16)
    off = jnp.dot(before, jnp.broadcast_to(nch, (LANES, LANES)).astype(BF16),
                  preferred_element_type=F32)[:, 0:1]
    base = off * CHUNK
    pos1r = jnp.sum(a1 * (base + r1), axis=0, keepdims=True)
    pos2r = jnp.sum(a2 * (base + cnt1 + r2), axis=0, keepdims=True)
    slot = lax.broadcasted_iota(I32, (USED_ROWS, t), 0).astype(F32)
    onehot = ((slot == pos1r) | (slot == pos2r)).astype(BF16)
    xs = jnp.dot(onehot, x1.astype(BF16), preferred_element_type=F32)
    pos_cols = jnp.where(sub == 2.0, pos1r, jnp.where(sub == 3.0, pos2r, 0.0)).T
    lane = lax.broadcasted_iota(I32, (t, LANES), 1)
    x1_ref[...] = x1
    rw_ref[...] = jnp.where(lane == 0, w1, jnp.where(lane == 1, w2, pos_cols))
    xs_ref[0:USED_ROWS, :] = xs.astype(BF16)
    xs_ref[USED_ROWS:CAP, :] = jnp.zeros((CAP - USED_ROWS, x1.shape[1]), BF16)
    mlane = lax.broadcasted_iota(I32, (LANES, LANES), 1)
    meta = jnp.where(mlane == 0, cnt, jnp.where(mlane == 1, off, 0.0))
    meta_ref[...] = meta.astype(I32)


def _sort_part(i, x1, w1, w2, e1, e2, x1_ref, rw_ref, xs_ref, meta_ref):
    _sort_tile(x1, w1, w2, e1, e2,
               x1_ref.at[pl.ds(i * TL, TL)], rw_ref.at[pl.ds(i * TL, TL)],
               xs_ref.at[pl.ds(i * CAP, CAP)], meta_ref.at[pl.ds(i * LANES, LANES)])


def _sort_tiles(x1, w1, w2, e1, e2, *outs):
    for i in range(x1.shape[0] // TL):
        rows = slice(i * TL, (i + 1) * TL)
        _sort_part(i, x1[rows], w1[rows], w2[rows], e1[rows], e2[rows], *outs)


GATE_COLS = {3: 0, 6: RET_V, 7: RET_V + D_MODEL}
QKV_COLS = {0: 0, 1: RET_QK, 2: 2 * RET_QK}


def _prompt_mixer_kernel(x_ref, x1s_ref, rws_ref, cos_ref, sin_ref, dec_ref, qdec_ref, kdec_ref, cdec_ref,
                         w_in, b_in, gn_g, gn_b, w_ret_o, conv_w, conv_b, cln_g, cln_b,
                         w_conv_o, w_out, ln1_g, ln1_b, wr_hi, wr_lo, b_r,
                         x1_ref, rw_ref, xs_ref, meta_ref, sret_ref, sconv_ref,
                         ubuf, ushift, qkv_scr, xb_scr, ret_scr, cout_scr, gate_scr, x1_carry, lg_carry,
                         *, n_tiles, tiles_per_seq):
    s = pl.program_id(0)
    li = lax.rem(s, tiles_per_seq)
    outs = (x1_ref, rw_ref, xs_ref, meta_ref)
    slot = dict(ret=ret_scr, cout=cout_scr, gates=gate_scr)
    tail_w = (w_ret_o, cln_g, cln_b, w_conv_o, w_out, ln1_g, ln1_b, wr_hi, wr_lo, b_r)
    carry = (x1_carry, lg_carry)
    tl = x_ref.shape[1]
    parts = tl // TL
    sort_sink = lambda *r: _sort_part(*r, *outs)

    @pl.when(s == 0)
    def _first():
        x1_carry[...] = jnp.zeros(x1_carry.shape, F32)
        lg_carry[...] = jnp.zeros(lg_carry.shape, F32)

    @pl.when((s < n_tiles) & (li == 0))
    def _new_sequence():
        sret_ref[...] = jnp.zeros(sret_ref.shape, F32)
        ubuf[0:CONV_PAD, :] = jnp.zeros((CONV_PAD, CONV_CH), F32)

    @pl.when(s < n_tiles)
    def _mix():
        gcols = lambda kk: slice(GATE_COLS[kk], GATE_COLS[kk] + IN_WIDTHS[kk])
        src = dict(x=lambda r: x_ref[0, r, :], ret=lambda: ret_scr[...], cout=lambda: cout_scr[...],
                   g=lambda: gate_scr[:, gcols(3)], gt_a=lambda r: gate_scr[r, gcols(6)],
                   gt_b=lambda r: gate_scr[r, gcols(7)])
        parked = _parked_pieces(carry, sort_sink, tl, parts)
        head = _prompt_head_pieces(x_ref, cos_ref, sin_ref, dec_ref, qdec_ref, kdec_ref, cdec_ref,
                                   w_in, b_in, gn_g, gn_b, conv_w, conv_b, sret_ref,
                                   ubuf, ushift, qkv_scr, xb_scr, slot, extra=parked)
        tail = _post_mix_pieces(src, tail_w, None, tl, parts=parts, carry=carry)
        for piece in head + tail:
            piece()

    @pl.when(s == n_tiles)
    def _drain():
        for piece in _parked_pieces(carry, sort_sink, tl, parts):
            piece()

    @pl.when(s > n_tiles)
    def _append():
        rws = rws_ref[...]
        _sort_tiles(x1s_ref[...], rws[:, 0:1], rws[:, 1:2], rws[:, 2:3], rws[:, 3:4], *outs)

    @pl.when((s < n_tiles) & (li == tiles_per_seq - 1))
    def _conv_state():
        sconv_ref[0, 0] = ubuf[CONV_PAD - (CONV_WIDTH - 1):CONV_PAD, :]


def _prompt_head_pieces(x_ref, cos_ref, sin_ref, dec_ref, qdec_ref, kdec_ref, cdec_ref,
                        w_in, b_in, gn_g, gn_b, conv_w, conv_b, sret_ref,
                        ubuf, ushift, qkv_scr, xb_scr, slot, extra=()):
    tl = x_ref.shape[1]
    st = {}

    def slab_dot(c0, c1):
        return jnp.dot(xb_scr[...], w_in[:, c0:c1], preferred_element_type=F32) + _rep(b_in[:, c0:c1], tl)

    def glu():
        xb_scr[...] = x_ref[0].astype(BF16)
        u = slab_dot(IN_OFFS[4], IN_OFFS[5]) * _sigmoid(slab_dot(IN_OFFS[5], IN_OFFS[6]))
        ubuf[CONV_PAD:CONV_PAD + tl, :] = u

    nsh = ushift.shape[1]
    span = nsh - (CONV_PAD - SUBLANES)

    def shift_copy(h, s):
        ushift[s - 1] = ubuf[h * span + s:h * span + s + nsh, :]

    slab = 256
    slabs = [(kk, c0) for kk in (0, 1, 2, 3, 6, 7) for c0 in range(IN_OFFS[kk], IN_OFFS[kk + 1], slab)]
    rb = 32
    nrb = tl // rb

    def proj_slab(kk, c0):
        val = slab_dot(c0, c0 + slab)
        if kk in QKV_COLS:
            dst = QKV_COLS[kk] + c0 - IN_OFFS[kk]
            qkv_scr[:, dst:dst + slab] = val
        else:
            dst = GATE_COLS[kk] + c0 - IN_OFFS[kk]
            slot["gates"][:, dst:dst + slab] = val

    def conv_block(r):
        h, rl = divmod(r * rb, span)
        acc = jnp.zeros((rb, CONV_CH), F32) + conv_b[...]
        for j in range(CONV_WIDTH):
            off = j + (CONV_PAD - (CONV_WIDTH - 1))
            s = off % SUBLANES
            base = rl + off - s
            win = (ubuf[h * span + base:h * span + base + rb, :] if s == 0
                   else ushift[s - 1, base:base + rb, :])
            acc = acc + _rep(conv_w[j], rb) * win
        slot["cout"][r * rb:(r + 1) * rb, :] = acc
        if r == nrb - 1:
            ubuf[0:CONV_PAD, :] = ubuf[tl:tl + CONV_PAD, :]

    scale = RET_DK ** -0.5

    def retention(c, h):
        rows = slice(c * RET_CHUNK, (c + 1) * RET_CHUNK)
        cols = slice(h * RET_DK, (h + 1) * RET_DK)
        hcol = lambda kk: slice(QKV_COLS[kk] + h * RET_DK, QKV_COLS[kk] + (h + 1) * RET_DK)
        cosf = cos_ref[rows, :]
        sinf = sin_ref[rows, :]
        qh = _rot(qkv_scr[rows, hcol(0)], cosf, sinf)
        kh = _rot(qkv_scr[rows, hcol(1)], cosf, sinf) * scale
        qb = qh.astype(BF16)
        kb = kh.astype(BF16)
        vb = qkv_scr[rows, hcol(2)].astype(BF16)
        s_old = sret_ref[0, 0, h]
        scores = lax.dot_general(qb, kb, (((1,), (1,)), ((), ())),
                                 preferred_element_type=F32) * dec_ref[h]
        inner = jnp.dot(scores.astype(BF16), vb, preferred_element_type=F32)
        cross = jnp.dot(qb, s_old.astype(BF16), preferred_element_type=F32) * qdec_ref[h]
        kd = (kh * kdec_ref[h]).astype(BF16)
        s_new = cdec_ref[h] * s_old + lax.dot_general(
            kd, vb, (((0,), (0,)), ((), ())), preferred_element_type=F32)
        sret_ref[0, 0, h] = s_new
        slot["ret"][rows, cols] = _ln(inner + cross, gn_g[:, cols], gn_b[:, cols])

    vector_pieces = []
    for r in range(nrb):
        if (r * rb) % span == 0:
            vector_pieces += [lambda h=(r * rb) // span, s=s: shift_copy(h, s) for s in range(1, SUBLANES)]
        vector_pieces.append(lambda r=r: conv_block(r))
    def spread(work, between):
        out = []
        for i, piece in enumerate(work):
            out.append(piece)
            out += between[i * len(between) // len(work):(i + 1) * len(between) // len(work)]
        return out

    as_piece = lambda sl: (lambda: proj_slab(*sl))
    early = [as_piece(sl) for sl in slabs if sl[0] in (0, 1, 2, 3)]
    late = [as_piece(sl) for sl in slabs if sl[0] not in (0, 1, 2, 3)]
    for i, piece in enumerate(extra):
        early.insert(min(1 + 2 * i, len(early)), piece)
    ret_pieces = [lambda c=c, h=h: retention(c, h) for c in range(tl // RET_CHUNK) for h in range(RET_HEADS)]
    return [glu] + spread(vector_pieces, early) + spread(ret_pieces, late)


def _sample_mixer_kernel(x_ref, cos_ref, sin_ref, pdec_ref, qdec_ref, kdec_ref, cdec_ref, conv_w,
                         sret_in, sconv_in,
                         w_in, b_in, gn_g, gn_b, w_ret_o, conv_b, cln_g, cln_b,
                         w_conv_o, w_out, ln1_g, ln1_b, wr_hi, wr_lo, b_r,
                         x1_ref, rw_ref, sret_ref, sconv_ref,
                         ret_scr, cout_scr, gate_scr, xwin, u_scr):
    i = pl.program_id(0)
    t = cos_ref.shape[0]
    ls = t // BB_SAMPLE
    nstate = CONV_WIDTH - 1
    r0 = pl.multiple_of(i * t, t)
    xb = x_ref[pl.ds(r0, t), :].astype(BF16)

    def proj(k):
        c0, c1 = IN_OFFS[k], IN_OFFS[k + 1]
        return jnp.dot(xb, w_in[:, c0:c1], preferred_element_type=F32) + _rep(b_in[:, c0:c1], t)

    q = proj(0)
    k = proj(1)
    v = proj(2)
    scale = RET_DK ** -0.5
    cosf = cos_ref[...]
    sinf = sin_ref[...]
    row = lax.broadcasted_iota(I32, (t, RET_DK), 0)
    pos = row % ls
    row8 = lax.broadcasted_iota(I32, (SUBLANES, RET_DK), 0)
    per_tile = SUBLANES // ls
    for h in range(RET_HEADS):
        cols = slice(h * RET_DK, (h + 1) * RET_DK)
        qh = _rot(q[:, cols], cosf, sinf)
        kh = _rot(k[:, cols], cosf, sinf) * scale
        vh = v[:, cols]
        inner = jnp.zeros((t, RET_DV), F32)
        for s in range(ls):
            ks = kh if s == 0 else pltpu.roll(kh, s, axis=0)
            vs = vh if s == 0 else pltpu.roll(vh, s, axis=0)
            dotp = jnp.sum(qh * ks, axis=1, keepdims=True) * pdec_ref[h, s]
            inner = inner + jnp.where(pos >= s, dotp, 0.0) * vs
        kd = kh * kdec_ref[h]
        for tile in range(t // SUBLANES):
            rows = slice(tile * SUBLANES, (tile + 1) * SUBLANES)
            q8 = qh[rows, :]
            kd8 = kd[rows, :]
            v8 = vh[rows, :]
            seqs = [tile * per_tile + sub for sub in range(per_tile)]
            mine = [(row8 >= sub * ls) & (row8 < (sub + 1) * ls) for sub in range(per_tile)]
            s_old = [sret_in[0, b, h] for b in seqs]
            c_all = jnp.dot(q8, jnp.concatenate(s_old, axis=1), preferred_element_type=F32)
            upd = lax.dot_general(jnp.concatenate([jnp.where(m, kd8, 0.0) for m in mine], axis=1), v8,
                                  (((0,), (0,)), ((), ())), preferred_element_type=F32)
            cross8 = jnp.zeros((SUBLANES, RET_DV), F32)
            for sub, b in enumerate(seqs):
                cross8 = jnp.where(mine[sub], c_all[:, sub * RET_DV:(sub + 1) * RET_DV], cross8)
                sret_ref[0, b, h] = cdec_ref[h] * s_old[sub] + upd[sub * RET_DK:(sub + 1) * RET_DK, :]
            ret_scr[pl.ds(r0 + tile * SUBLANES, SUBLANES), cols] = (
                inner[rows, :] + cross8 * qdec_ref[h, rows, :])
        ret_scr[pl.ds(r0, t), cols] = _ln(ret_scr[pl.ds(r0, t), cols], gn_g[:, cols], gn_b[:, cols])

    u = proj(4) * _sigmoid(proj(5))
    nslab = CONV_CH // LANES
    xwin[0:nstate] = sconv_in[...]
    for sl in range(nslab):
        u_scr[sl] = u[:, sl * LANES:(sl + 1) * LANES]
    for p in range(ls):
        for sl in range(nslab):
            xwin[XPAD_NEW + p, :, sl * LANES:(sl + 1) * LANES] = u_scr[sl, pl.ds(p, BB_SAMPLE, stride=ls), :]
    win_row = lambda m: m if m < nstate else XPAD_NEW + (m - nstate)
    for p in range(ls):
        res = jnp.zeros((BB_SAMPLE, CONV_CH), F32) + conv_b[...]
        for j in range(CONV_WIDTH):
            res = res + _rep(conv_w[j], BB_SAMPLE) * xwin[win_row(p + j)]
        for sl in range(nslab):
            cout_scr[sl, pl.ds(r0 + p, BB_SAMPLE, stride=ls), :] = res[:, sl * LANES:(sl + 1) * LANES]
    sconv_ref[0:nstate - ls] = xwin[ls:nstate]
    sconv_ref[nstate - ls:nstate] = xwin[XPAD_NEW:XPAD_NEW + ls]
    for kk in (3, 6, 7):
        gate_scr[pl.ds(r0, t), GATE_COLS[kk]:GATE_COLS[kk] + IN_WIDTHS[kk]] = proj(kk)

    @pl.when(i == pl.num_programs(0) - 1)
    def _second_half():
        n = x_ref.shape[0]

        def sink(p, x1, w1, w2, e1, e2):
            x1_ref[...] = x1
            rw_ref[...] = _lane_tile((w1, w2, e1, e2), n)

        gcols = lambda kk: slice(GATE_COLS[kk], GATE_COLS[kk] + IN_WIDTHS[kk])
        src = dict(x=lambda r: x_ref[r, :], ret=lambda: ret_scr[...],
                   cout=lambda: jnp.concatenate([cout_scr[sl] for sl in range(CONV_CH // LANES)], axis=1),
                   g=lambda: gate_scr[:, gcols(3)], gt_a=lambda r: gate_scr[r, gcols(6)],
                   gt_b=lambda r: gate_scr[r, gcols(7)])
        for piece in _post_mix_pieces(
                src, (w_ret_o, cln_g, cln_b, w_conv_o, w_out, ln1_g, ln1_b, wr_hi, wr_lo, b_r), sink, n):
            piece()


def _ffn_kernel(te_ref, nvalid_ref, nexte_ref, chunk_ref, xs_hbm, wgu_hbm, wdn_hbm, ys_hbm,
                xbuf, obuf, wgu_f, wdn_f, wgu_b, wdn_b, sem_in, sem_out, sem_w):
    del xs_hbm
    nvalid = nvalid_ref[0]

    def chunk_rows(tile, c):
        return pl.ds(pl.multiple_of(chunk_ref[tile * TILE_CHUNKS + c] * CHUNK, CHUNK), CHUNK)

    def start_in(tile, s):
        for c in range(TILE_CHUNKS):
            pltpu.make_async_copy(ys_hbm.at[chunk_rows(tile, c)],
                                  xbuf.at[s, pl.ds(c * CHUNK, CHUNK)], sem_in.at[s]).start()

    def start_out(tile, s):
        for c in range(TILE_CHUNKS):
            pltpu.make_async_copy(obuf.at[s, pl.ds(c * CHUNK, CHUNK)],
                                  ys_hbm.at[chunk_rows(tile, c)], sem_out.at[s]).start(priority=c % 2)

    def wait_in(s):
        pltpu.make_async_copy(ys_hbm.at[pl.ds(0, TM_FFN)], xbuf.at[s], sem_in.at[s]).wait()

    def wait_out(s):
        pltpu.make_async_copy(obuf.at[s], ys_hbm.at[pl.ds(0, TM_FFN)], sem_out.at[s]).wait()

    def weight_copies(e, s):
        return (pltpu.make_async_copy(wgu_hbm.at[e], wgu_f.at[s], sem_w.at[s]),
                pltpu.make_async_copy(wdn_hbm.at[e], wdn_f.at[s], sem_w.at[s]))

    def tile(i, wslot):
        slot = i % 2

        @pl.when(i >= 2)
        def _retire():
            wait_out(slot)

        wait_in(slot)
        changed = (i == 0) | (te_ref[i] != te_ref[jnp.maximum(i - 1, 0)])

        @pl.when(changed)
        def _new_expert():
            for cp in weight_copies(te_ref[i], wslot):
                cp.wait()
            wgu_b[...] = wgu_f[wslot].astype(BF16)
            wdn_b[...] = wdn_f[wslot].astype(BF16)

            @pl.when(nexte_ref[i] >= 0)
            def _():
                for cp in weight_copies(nexte_ref[i], 1 - wslot):
                    cp.start(priority=1)

        x = xbuf[slot]
        y = jnp.zeros((TM_FFN, wdn_b.shape[1]), F32)
        for c0 in range(0, EXP_FF, FFN_COLS):
            hg = jnp.dot(x, wgu_b[:, c0:c0 + FFN_COLS], preferred_element_type=F32)
            hu = jnp.dot(x, wgu_b[:, EXP_FF + c0:EXP_FF + c0 + FFN_COLS], preferred_element_type=F32)
            y = y + _bdot(_silu(hg) * hu, wdn_b[c0:c0 + FFN_COLS, :])
            if c0 == 0:
                start_in(jnp.where(i + 1 < nvalid, i + 1, 0), 1 - slot)
        obuf[slot] = y.astype(BF16)
        start_out(i, slot)
        return jnp.where(changed, 1 - wslot, wslot)

    for cp in weight_copies(te_ref[0], 0):
        cp.start(priority=1)
    start_in(0, 0)
    lax.fori_loop(0, nvalid, tile, jnp.int32(0))
    wait_out(nvalid % 2)
    wait_out(1 - nvalid % 2)
    wait_in(nvalid % 2)


def _final_kernel(ys_ref, x1_ref, rw_ref, pp_ref, ps_ref, ln2_g, ln2_b, w_pg, b_pg, w_ple,
                  yp_ref, ys_out_ref, *, n_prompt_tiles):
    i = pl.program_id(0)
    x1 = x1_ref[...]
    slot = lax.broadcasted_iota(I32, (TL, USED_ROWS), 1).astype(F32)
    parts = []
    for b in range(x1.shape[0] // TL):
        rw = rw_ref[b * TL:(b + 1) * TL, :]
        w1, w2, pos1, pos2 = rw[:, 0:1], rw[:, 1:2], rw[:, 2:3], rw[:, 3:4]
        ys = ys_ref[b * CAP:b * CAP + USED_ROWS, :]
        comb = jnp.where(slot == pos1, w1, jnp.where(slot == pos2, w2, 0.0)).astype(BF16)
        parts.append(jnp.dot(comb, ys, preferred_element_type=F32))
    moe = parts[0] if len(parts) == 1 else jnp.concatenate(parts, axis=0)
    x2 = _ln(ALPHA * x1 + moe, ln2_g[...], ln2_b[...])
    gate = _sigmoid(_bdot(x2, w_pg[...]) + b_pg[...])
    p = jnp.where(i < n_prompt_tiles, pp_ref[...], ps_ref[...])
    y = x2 + gate * _bdot(p, w_ple[...])

    @pl.when(i < n_prompt_tiles)
    def _prompt():
        yp_ref[...] = y

    @pl.when(i >= n_prompt_tiles)
    def _sample():
        ys_out_ref[...] = y


def _rope_tables(pos):
    half = RET_DK // 2
    inv_freq = ROPE_BASE ** (-np.arange(half, dtype=np.float64) / half)
    ang = np.asarray(pos, np.float64)[:, None] * inv_freq[None, :]
    cos = np.cos(ang)
    sin = np.sin(ang)
    return (np.concatenate([cos, cos], axis=-1).astype(np.float32),
            np.concatenate([-sin, sin], axis=-1).astype(np.float32))


def _log_gamma():
    return np.log(1.0 - 2.0 ** (-5.0 - np.arange(RET_HEADS, dtype=np.float64)))


def _const_spec(shape):
    nd = len(shape)
    return pl.BlockSpec(shape, lambda *_: (0,) * nd, pipeline_mode=pl.Buffered(1))


def _chunk_plan(meta, n_blocks, n_ffn_tiles):
    assert n_blocks * BLOCK_SPARE >= N_EXPERTS * (TILE_CHUNKS - 1)
    m = meta.reshape(n_blocks, LANES, LANES)
    cnt = m[:, :N_EXPERTS, 0]
    off = m[:, :N_EXPERTS, 1]
    nch = (cnt + (CHUNK - 1)) // CHUNK
    cum = jnp.cumsum(nch, axis=0)
    total = cum[-1:]
    tiles_e = (total + TILE_CHUNKS - 1) // TILE_CHUNKS
    tile_end = jnp.cumsum(tiles_e, axis=1)
    tile_start = tile_end - tiles_e
    tid = jnp.arange(n_ffn_tiles, dtype=I32)[:, None]
    owner = (tid >= tile_start) & (tid < tile_end)
    pick_e = lambda v: jnp.sum(jnp.where(owner, v, 0), axis=1, keepdims=True)
    te = pick_e(jnp.arange(N_EXPERTS, dtype=I32)[None, :])
    k = (tid - pick_e(tile_start)) * TILE_CHUNKS + jnp.arange(TILE_CHUNKS, dtype=I32)[None, :]
    total_t = pick_e(total)
    real = k < total_t
    by_tile = lambda v: jnp.sum(jnp.where(owner[:, None, :], v[None, :, :], 0), axis=2)
    cum_t = by_tile(cum)
    blk = jnp.minimum(jnp.sum((cum_t[:, None, :] <= k[:, :, None]).astype(I32), axis=2), n_blocks - 1)
    at_blk = blk[:, :, None] == jnp.arange(n_blocks, dtype=I32)[None, None, :]
    pick_b = lambda v: jnp.sum(jnp.where(at_blk, v[:, None, :], 0), axis=2)
    excl = pick_b(cum_t - by_tile(nch))
    off_t = pick_b(by_tile(off))
    spare = te * (TILE_CHUNKS - 1) + jnp.maximum(k - total_t, 0) % TILE_CHUNKS
    spare_chunk = (spare // BLOCK_SPARE) * BLOCK_CHUNKS + BLOCK_USED + spare % BLOCK_SPARE
    chunk = jnp.where(real, blk * BLOCK_CHUNKS + off_t + (k - excl), spare_chunk)
    n_valid = jnp.sum(tiles_e, axis=1)
    eid = jnp.arange(N_EXPERTS, dtype=I32)
    later = (eid[None, :] > eid[:, None]) & (tiles_e > 0)
    next_e = jnp.min(jnp.where(later, eid[None, :], N_EXPERTS), axis=1)[None, :]
    next_t = pick_e(jnp.where(next_e < N_EXPERTS, next_e, -1))
    return (te.reshape(-1).astype(I32), n_valid.astype(I32), next_t.reshape(-1).astype(I32),
            chunk.reshape(-1).astype(I32))


def kernel(x_prompt, x_sample, state_ret, state_conv, p_prompt, p_sample, w_in, b_in, ret_gn_g, ret_gn_b,
           w_ret_o, conv_w, conv_b, conv_ln_g, conv_ln_b, w_conv_o, w_out, ln1_g, ln1_b, w_grp, b_grp,
           w_exp, b_exp, w_gu, w_dn, ln2_g, ln2_b, w_pg, b_pg, w_ple):
    assert DEPTH == 1 and w_in.shape[0] == 1
    bp, lp, d = x_prompt.shape
    bs, ls, _ = x_sample.shape
    n_p, n_s = bp * lp, bs * ls
    n_tok = n_p + n_s
    assert lp % TL == 0 and n_s % TL == 0 and bs % BB_SAMPLE == 0 and SUBLANES % ls == 0
    n_blocks = n_tok // TL

    f32c = lambda a, shape: jnp.asarray(np.broadcast_to(a, shape).astype(np.float32))
    lg = _log_gamma()
    c = RET_CHUNK
    idx = np.arange(c, dtype=np.float64)
    rel = idx[:, None] - idx[None, :]
    causal = rel >= 0
    decay = np.where(causal[None], np.exp(np.where(causal, rel, 0.0)[None] * lg[:, None, None]), 0.0)
    decay = f32c(decay, decay.shape)
    q_decay = np.exp((idx[:, None] + 1.0) * lg[None, :])
    k_decay = np.exp((c - 1.0 - idx[:, None]) * lg[None, :])
    chunk_decay = np.exp(c * lg)
    qdec_p = f32c(q_decay.T[:, :, None], (RET_HEADS, c, RET_DK))
    kdec_p = f32c(k_decay.T[:, :, None], (RET_HEADS, c, RET_DK))
    cdec_p = f32c(chunk_decay[:, None, None], (RET_HEADS, 1, RET_DV))
    cos_p, sin_p = (jnp.asarray(a) for a in _rope_tables(np.arange(lp)))

    ts = BB_SAMPLE * ls
    idx_s = np.arange(ls, dtype=np.float64)
    pdec_s = np.exp(idx_s[None, :] * lg[:, None])
    pdec_s = f32c(pdec_s[:, :, None, None], (RET_HEADS, ls, 1, RET_DK))
    qd_s = np.exp((idx_s[:, None] + 1.0) * lg[None, :])
    kd_s = np.exp((ls - 1.0 - idx_s[:, None]) * lg[None, :])
    qdec_s = f32c(np.tile(qd_s.T, (1, BB_SAMPLE))[:, :, None], (RET_HEADS, ts, RET_DK))
    kdec_s = f32c(np.tile(kd_s.T, (1, BB_SAMPLE))[:, :, None], (RET_HEADS, ts, RET_DK))
    cdec_s = f32c(np.exp(ls * lg)[:, None, None], (RET_HEADS, 1, RET_DV))
    cos_s, sin_s = (jnp.asarray(a) for a in _rope_tables(np.tile(PAST_LEN + np.arange(ls), BB_SAMPLE)))

    w_in_b = w_in[0].astype(BF16)
    w_ret_o_b = w_ret_o[0].astype(BF16)
    w_conv_o_b = w_conv_o[0].astype(BF16)
    w_out_b = w_out[0].astype(BF16)
    w_pg_b = w_pg[0].astype(BF16)
    w_ple_b = w_ple[0].astype(BF16)
    n_route = N_GROUPS + N_EXPERTS
    w_r = jnp.concatenate([w_grp[0], w_exp[0], jnp.zeros((d, LANES - n_route), F32)], axis=1)
    wr_hi = w_r.astype(BF16)
    wr_lo = jnp.concatenate([wr_hi, (w_r - wr_hi.astype(F32)).astype(BF16)], axis=1)
    b_r = jnp.concatenate([b_grp[0], b_exp[0], jnp.zeros((LANES - n_route,), F32)]).reshape(1, LANES)
    row = lambda a: a.reshape(1, -1)
    conv_w0 = conv_w[0]
    nstate = CONV_WIDTH - 1

    rep8 = lambda a: jnp.broadcast_to(a[..., None, :], a.shape[:-1] + (SUBLANES, a.shape[-1]))
    shared_w = (w_in_b, rep8(b_in[0]), row(ret_gn_g[0]), row(ret_gn_b[0]), w_ret_o_b)
    tail_w = (row(conv_ln_g[0]), row(conv_ln_b[0]), w_conv_o_b, w_out_b, row(ln1_g[0]), row(ln1_b[0]),
              wr_hi, wr_lo, b_r)

    nbt = bs // BB_SAMPLE
    xs2 = x_sample.reshape(n_s, d)
    conv_state_t = jnp.transpose(state_conv[0], (1, 0, 2))
    sample_in = ((xs2, cos_s, sin_s, pdec_s, qdec_s, kdec_s, cdec_s, rep8(conv_w0), state_ret, conv_state_t)
                 + shared_w + (row(conv_b[0]),) + tail_w)
    conv_state_spec = pl.BlockSpec((nstate, BB_SAMPLE, CONV_CH), lambda i: (0, i, 0))
    sample_specs = (
        [_const_spec(a.shape) for a in sample_in[0:8]]
        + [pl.BlockSpec((1, BB_SAMPLE, RET_HEADS, RET_DK, RET_DV), lambda i: (0, i, 0, 0, 0)), conv_state_spec]
        + [_const_spec(a.shape) for a in sample_in[10:]]
    )
    tok_spec_s = lambda w: pl.BlockSpec((n_s, w), lambda i: (0, 0))
    x1_s, rw_s, ret_s, conv_s_t = pl.pallas_call(
        _sample_mixer_kernel,
        grid=(nbt,),
        in_specs=sample_specs,
        out_specs=[
            tok_spec_s(d), tok_spec_s(LANES),
            pl.BlockSpec((1, BB_SAMPLE, RET_HEADS, RET_DK, RET_DV), lambda i: (0, i, 0, 0, 0)),
            conv_state_spec,
        ],
        out_shape=[
            jax.ShapeDtypeStruct((n_s, d), F32),
            jax.ShapeDtypeStruct((n_s, LANES), F32),
            jax.ShapeDtypeStruct(state_ret.shape, F32),
            jax.ShapeDtypeStruct(conv_state_t.shape, F32),
        ],
        scratch_shapes=[
            pltpu.VMEM((n_s, RET_V), F32),
            pltpu.VMEM((CONV_CH // LANES, n_s, LANES), F32),
            pltpu.VMEM((n_s, RET_V + 2 * D_MODEL), F32),
            pltpu.VMEM((XPAD_ROWS, BB_SAMPLE, CONV_CH), F32),
            pltpu.VMEM((CONV_CH // LANES, ts, LANES), F32),
        ],
        compiler_params=pltpu.CompilerParams(
            dimension_semantics=("arbitrary",), vmem_limit_bytes=VMEM_LIMIT),
        name="sample_mixer",
    )(*sample_in)
    conv_s = jnp.transpose(conv_s_t, (1, 0, 2))[None]

    assert lp % TLM == 0 and n_s % TLM == 0 and TLM % TL == 0
    nlt = lp // TLM
    npt = n_p // TLM
    nst = n_s // TLM
    sub = TLM // TL
    prompt_in = ((x_prompt, x1_s, rw_s, cos_p, sin_p, decay, qdec_p, kdec_p, cdec_p)
                 + shared_w + (rep8(conv_w0), row(conv_b[0])) + tail_w)
    head_tile = lambda s: jnp.minimum(s, npt - 1)
    sample_tile = lambda s: jnp.maximum(s - (npt + 1), 0)
    sample_spec = lambda w: pl.BlockSpec((TLM, w), lambda s: (sample_tile(s), 0))
    prompt_specs = [
        pl.BlockSpec((1, TLM, d), lambda s: (head_tile(s) // nlt, head_tile(s) % nlt, 0)),
        sample_spec(d), sample_spec(LANES),
        pl.BlockSpec((TLM, RET_DK), lambda s: (head_tile(s) % nlt, 0)),
        pl.BlockSpec((TLM, RET_DK), lambda s: (head_tile(s) % nlt, 0)),
    ] + [_const_spec(a.shape) for a in prompt_in[5:]]
    tok_spec_p = lambda rows, w: pl.BlockSpec((rows, w), lambda s: (jnp.maximum(s - 1, 0), 0))
    x1_all, rw_all, xs_all, meta, ret_p, conv_p = pl.pallas_call(
        functools.partial(_prompt_mixer_kernel, n_tiles=npt, tiles_per_seq=nlt),
        grid=(npt + 1 + nst,),
        in_specs=prompt_specs,
        out_specs=[
            tok_spec_p(TLM, d), tok_spec_p(TLM, LANES), tok_spec_p(sub * CAP, d), tok_spec_p(sub * LANES, LANES),
            pl.BlockSpec((1, 1, RET_HEADS, RET_DK, RET_DV), lambda s: (0, head_tile(s) // nlt, 0, 0, 0)),
            pl.BlockSpec((1, 1, nstate, CONV_CH), lambda s: (0, head_tile(s) // nlt, 0, 0)),
        ],
        out_shape=[
            jax.ShapeDtypeStruct((n_tok, d), F32),
            jax.ShapeDtypeStruct((n_tok, LANES), F32),
            jax.ShapeDtypeStruct((n_blocks * CAP, d), BF16),
            jax.ShapeDtypeStruct((n_blocks * LANES, LANES), I32),
            jax.ShapeDtypeStruct((1, bp, RET_HEADS, RET_DK, RET_DV), F32),
            jax.ShapeDtypeStruct((1, bp, nstate, CONV_CH), F32),
        ],
        scratch_shapes=[
            pltpu.VMEM((TLM + CONV_PAD, CONV_CH), F32),
            pltpu.VMEM((SUBLANES - 1, TL + CONV_PAD - SUBLANES, CONV_CH), F32),
            pltpu.VMEM((TLM, 2 * RET_QK + RET_V), F32),
            pltpu.VMEM((TLM, d), BF16),
            pltpu.VMEM((TLM, RET_V), F32),
            pltpu.VMEM((TLM, CONV_CH), F32),
            pltpu.VMEM((TLM, RET_V + 2 * D_MODEL), F32),
            pltpu.VMEM((TLM, d), F32),
            pltpu.VMEM((TLM, LANES), F32),
        ],
        compiler_params=pltpu.CompilerParams(
            dimension_semantics=("arbitrary",), vmem_limit_bytes=VMEM_LIMIT),
        name="prompt_mixer",
    )(*prompt_in)

    assert TOP_K * n_tok // TM_FFN >= 3
    max_chunks = n_blocks * (TOP_K * TL // CHUNK + N_EXPERTS - 1)
    n_ffn_tiles = (max_chunks + N_EXPERTS * (TILE_CHUNKS - 1)) // TILE_CHUNKS
    tile_e, n_valid_tiles, next_e, chunk_ids = _chunk_plan(meta, n_blocks, n_ffn_tiles)

    ys_all = pl.pallas_call(
        _ffn_kernel,
        grid_spec=pltpu.PrefetchScalarGridSpec(
            num_scalar_prefetch=4,
            grid=(1,),
            in_specs=[pl.BlockSpec(memory_space=pl.ANY)] * 3,
            out_specs=pl.BlockSpec(memory_space=pl.ANY),
            scratch_shapes=[
                pltpu.VMEM((2, TM_FFN, d), BF16),
                pltpu.VMEM((2, TM_FFN, d), BF16),
                pltpu.VMEM((2, d, 2 * EXP_FF), F32),
                pltpu.VMEM((2, EXP_FF, d), F32),
                pltpu.VMEM((d, 2 * EXP_FF), BF16),
                pltpu.VMEM((EXP_FF, d), BF16),
                pltpu.SemaphoreType.DMA((2,)),
                pltpu.SemaphoreType.DMA((2,)),
                pltpu.SemaphoreType.DMA((2,)),
            ],
        ),
        out_shape=jax.ShapeDtypeStruct(xs_all.shape, BF16),
        input_output_aliases={4: 0},
        compiler_params=pltpu.CompilerParams(
            dimension_semantics=("arbitrary",), vmem_limit_bytes=VMEM_LIMIT),
        name="expert_ffn",
    )(tile_e, n_valid_tiles, next_e, chunk_ids, xs_all, w_gu[0], w_dn[0])

    assert n_p % TLF == 0 and n_s % TLF == 0
    npt = n_p // TLF
    fsub = TLF // TL
    pp2 = p_prompt.reshape(n_p, PLE_DIM)
    ps2 = p_sample.reshape(n_s, PLE_DIM)
    tok_f = lambda rows, w: pl.BlockSpec((rows, w), lambda i: (i, 0))
    y_p, y_s = pl.pallas_call(
        functools.partial(_final_kernel, n_prompt_tiles=npt),
        grid=(n_tok // TLF,),
        in_specs=[
            tok_f(fsub * CAP, d), tok_f(TLF, d), tok_f(TLF, LANES),
            pl.BlockSpec((TLF, PLE_DIM), lambda i: (jnp.minimum(i, npt - 1), 0)),
            pl.BlockSpec((TLF, PLE_DIM), lambda i: (jnp.maximum(i - npt, 0), 0)),
            _const_spec((1, d)), _const_spec((1, d)), _const_spec((d, d)), _const_spec((1, d)),
            _const_spec((PLE_DIM, d)),
        ],
        out_specs=[
            pl.BlockSpec((TLF, d), lambda i: (jnp.minimum(i, npt - 1), 0)),
            pl.BlockSpec((TLF, d), lambda i: (jnp.maximum(i - npt, 0), 0)),
        ],
        out_shape=[jax.ShapeDtypeStruct((n_p, d), F32), jax.ShapeDtypeStruct((n_s, d), F32)],
        compiler_params=pltpu.CompilerParams(
            dimension_semantics=("arbitrary",), vmem_limit_bytes=VMEM_LIMIT),
        name="moe_combine_final",
    )(ys_all, x1_all, rw_all, pp2, ps2,
      row(ln2_g[0]), row(ln2_b[0]), w_pg_b, row(b_pg[0]), w_ple_b)

    return (y_p.reshape(bp, lp, d), y_s.reshape(bs, ls, d), ret_p, conv_p, ret_s, conv_s)
```
